```python
import math
import jax, jax.numpy as jnp
from jax import lax
import numpy as np

D_MODEL = 1024
BATCH = 8
SEQ = 8192
DEPTH = 2

CHUNK = 64
N_MIXERS = 2
N_A = (DEPTH + 1) // 2
N_B = DEPTH // 2

D_RNN = D_MODEL
RG_BLOCKS = 4
RG_BW = D_RNN // RG_BLOCKS
CONV_W = 4
RG_C = 8.0

SB_HEADS = 16
SB_HEAD_DIM = D_MODEL // SB_HEADS
Q_BLOCK = 128

D_FF = int(math.ceil(8 * D_MODEL / 3 / 256) * 256)

RMS_EPS = 1e-6

kernel_name = "hybrid_rglru_stickbreaking_trunk"


def _rmsnorm(x, g):
    x32 = x.astype(jnp.float32)
    y = x32 * lax.rsqrt(jnp.mean(x32 * x32, axis=-1, keepdims=True) + RMS_EPS)
    return (y * g.astype(jnp.float32)).astype(x.dtype)


def _causal_depthwise_conv(x, w, b):
    c = x.shape[-1]
    y = lax.conv_general_dilated(
        x, w.astype(x.dtype)[:, None, :], window_strides=(1,),
        padding=[(CONV_W - 1, 0)], dimension_numbers=("NWC", "WIO", "NWC"),
        feature_group_count=c)
    return y + b.astype(x.dtype)


def _linear_scan(a, u):
    def combine(l, r):
        a_l, b_l = l
        a_r, b_r = r
        return a_l * a_r, a_r * b_l + b_r
    _, h = lax.associative_scan(combine, (a, u), axis=1)
    return h


def _rglru_mixer(h, w_in, conv_w, conv_b, w_r, b_r, w_i, b_i, lam, w_out):
    bsz, s, _ = h.shape
    proj = h @ w_in.astype(h.dtype)
    gate_br, x_br = proj[..., :D_RNN], proj[..., D_RNN:]
    gate = jax.nn.gelu(gate_br, approximate=True)
    xc = _causal_depthwise_conv(x_br, conv_w, conv_b)
    xg = xc.reshape(bsz, s, RG_BLOCKS, RG_BW).astype(jnp.float32)
    r = jax.nn.sigmoid(jnp.einsum("bsnc,ncd->bsnd", xg, w_r.astype(jnp.float32)).reshape(bsz, s, D_RNN)
                       + b_r.astype(jnp.float32))
    i = jax.nn.sigmoid(jnp.einsum("bsnc,ncd->bsnd", xg, w_i.astype(jnp.float32)).reshape(bsz, s, D_RNN)
                       + b_i.astype(jnp.float32))
    log_a = RG_C * r * jax.nn.log_sigmoid(lam.astype(jnp.float32))
    a = jnp.exp(log_a)
    mult = jnp.sqrt(jnp.maximum(-jnp.expm1(2.0 * log_a), 0.0))
    u = mult * (i * xc.astype(jnp.float32))
    hs = _linear_scan(a, u)
    y = (hs * gate.astype(jnp.float32)).astype(h.dtype)
    return y @ w_out.astype(h.dtype)


def _stick_breaking_attention(q, k, v):
    bsz, nh, s, dh = q.shape
    nq = s // Q_BLOCK
    scale = 1.0 / math.sqrt(dh)
    k32 = k.astype(jnp.float32)
    v32 = v.astype(jnp.float32)
    key_pos = jnp.arange(s)
    q_blocks = q.reshape(bsz, nh, nq, Q_BLOCK, dh).transpose(2, 0, 1, 3, 4)
    starts = jnp.arange(nq) * Q_BLOCK

    def block(args):
        qb, start = args
        z = jnp.einsum("bhqd,bhkd->bhqk", qb.astype(jnp.float32), k32) * scale
        q_pos = start + jnp.arange(Q_BLOCK)
        mask = key_pos[None, :] < q_pos[:, None]
        log_beta = jax.nn.log_sigmoid(z)
        log_1m = jnp.where(mask, jax.nn.log_sigmoid(-z), 0.0)
        suffix = lax.cumsum(log_1m, axis=3, reverse=True) - log_1m
        wts = jnp.where(mask, jnp.exp(log_beta + suffix), 0.0)
        return jnp.einsum("bhqk,bhkd->bhqd", wts, v32)

    out = lax.map(block, (q_blocks, starts))
    return out.transpose(1, 2, 0, 3, 4).reshape(bsz, nh, s, dh).astype(q.dtype)


def _sb_mixer(h, w_qkv, w_out):
    bsz, s, _ = h.shape
    qkv = (h @ w_qkv.astype(h.dtype)).reshape(bsz, s, 3, SB_HEADS, SB_HEAD_DIM)
    qkv = qkv.transpose(2, 0, 3, 1, 4)
    o = _stick_breaking_attention(qkv[0], qkv[1], qkv[2])
    o = o.transpose(0, 2, 1, 3).reshape(bsz, s, D_MODEL)
    return o @ w_out.astype(h.dtype)


def _swiglu(h, w_gate, w_up, w_down):
    g = h @ w_gate.astype(h.dtype)
    u = h @ w_up.astype(h.dtype)
    return (jax.nn.silu(g) * u) @ w_down.astype(h.dtype)


def _fwd_setup_inputs(seed: int = 0) -> dict:
    key = jax.random.key(seed)
    ks = jax.random.split(key, 20)
    f32 = jnp.float32

    def nrm(k, shape, fan_in):
        return jax.random.normal(k, shape, f32) * (fan_in ** -0.5)

    a0 = jax.random.uniform(ks[10], (N_A, D_RNN), f32, 0.9, 0.999)
    base = a0 ** (1.0 / RG_C)
    lam = jnp.log(base) - jnp.log1p(-base)

    return {
        "x": jax.random.normal(ks[0], (BATCH, SEQ, D_MODEL), f32),
        "norm_mix_g": 1.0 + 0.02 * jax.random.normal(ks[1], (DEPTH, D_MODEL), f32),
        "norm_ffn_g": 1.0 + 0.02 * jax.random.normal(ks[2], (DEPTH, D_MODEL), f32),
        "a_w_in": nrm(ks[3], (N_A, D_MODEL, 2 * D_RNN), D_MODEL),
        "a_conv_w": nrm(ks[4], (N_A, CONV_W, D_RNN), CONV_W),
        "a_conv_b": 0.01 * jax.random.normal(ks[5], (N_A, D_RNN), f32),
        "a_w_r": nrm(ks[6], (N_A, RG_BLOCKS, RG_BW, RG_BW), RG_BW),
        "a_b_r": 0.01 * jax.random.normal(ks[7], (N_A, D_RNN), f32),
        "a_w_i": nrm(ks[8], (N_A, RG_BLOCKS, RG_BW, RG_BW), RG_BW),
        "a_b_i": 0.01 * jax.random.normal(ks[9], (N_A, D_RNN), f32),
        "a_lambda": lam,
        "a_w_out": nrm(ks[11], (N_A, D_RNN, D_MODEL), D_RNN),
        "b_w_qkv": nrm(ks[12], (N_B, D_MODEL, 3 * D_MODEL), D_MODEL),
        "b_w_out": nrm(ks[13], (N_B, D_MODEL, D_MODEL), D_MODEL),
        "ffn_w_gate": nrm(ks[14], (DEPTH, D_MODEL, D_FF), D_MODEL),
        "ffn_w_up": nrm(ks[15], (DEPTH, D_MODEL, D_FF), D_MODEL),
        "ffn_w_down": nrm(ks[16], (DEPTH, D_FF, D_MODEL), D_FF),
        "final_g": 1.0 + 0.02 * jax.random.normal(ks[17], (D_MODEL,), f32),
    }


def _fwd_reference(x, norm_mix_g, norm_ffn_g, a_w_in, a_conv_w, a_conv_b, a_w_r, a_b_r,
              a_w_i, a_b_i, a_lambda, a_w_out, b_w_qkv, b_w_out,
              ffn_w_gate, ffn_w_up, ffn_w_down, final_g):
    for layer in range(DEPTH):
        h = _rmsnorm(x, norm_mix_g[layer])
        if layer % N_MIXERS == 0:
            j = layer // N_MIXERS
            mix = _rglru_mixer(h, a_w_in[j], a_conv_w[j], a_conv_b[j], a_w_r[j], a_b_r[j],
                               a_w_i[j], a_b_i[j], a_lambda[j], a_w_out[j])
        else:
            j = layer // N_MIXERS
            mix = _sb_mixer(h, b_w_qkv[j], b_w_out[j])
        x = x + mix
        h = _rmsnorm(x, norm_ffn_g[layer])
        x = x + _swiglu(h, ffn_w_gate[layer], ffn_w_up[layer], ffn_w_down[layer])
    return _rmsnorm(x, final_g)


import jax as _jax
import jax.numpy as _jnp

TWIN_FORMAT = 'train_step'
FWD_PARAMS = ['x', 'norm_mix_g', 'norm_ffn_g', 'a_w_in', 'a_conv_w', 'a_conv_b', 'a_w_r', 'a_b_r', 'a_w_i', 'a_b_i', 'a_lambda', 'a_w_out', 'b_w_qkv', 'b_w_out', 'ffn_w_gate', 'ffn_w_up', 'ffn_w_down', 'final_g']
TWIN_WEIGHTS = ['norm_mix_g', 'norm_ffn_g', 'a_w_in', 'a_conv_w', 'a_conv_b', 'a_w_r', 'a_b_r', 'a_w_i', 'a_b_i', 'a_lambda', 'a_w_out', 'b_w_qkv', 'b_w_out', 'ffn_w_gate', 'ffn_w_up', 'ffn_w_down', 'final_g']
TWIN_DIFF_INPUT = 'x'
TWIN_INPUTS = ['x', 'norm_mix_g', 'norm_ffn_g', 'a_w_in', 'a_conv_w', 'a_conv_b', 'a_w_r', 'a_b_r', 'a_w_i', 'a_b_i', 'a_lambda', 'a_w_out', 'b_w_qkv', 'b_w_out', 'ffn_w_gate', 'ffn_w_up', 'ffn_w_down', 'final_g', 'loss_target', 'm_norm_mix_g', 'm_norm_ffn_g', 'm_a_w_in', 'm_a_conv_w', 'm_a_conv_b', 'm_a_w_r', 'm_a_b_r', 'm_a_w_i', 'm_a_b_i', 'm_a_lambda', 'm_a_w_out', 'm_b_w_qkv', 'm_b_w_out', 'm_ffn_w_gate', 'm_ffn_w_up', 'm_ffn_w_down', 'm_final_g', 'v_norm_mix_g', 'v_norm_ffn_g', 'v_a_w_in', 'v_a_conv_w', 'v_a_conv_b', 'v_a_w_r', 'v_a_b_r', 'v_a_w_i', 'v_a_b_i', 'v_a_lambda', 'v_a_w_out', 'v_b_w_qkv', 'v_b_w_out', 'v_ffn_w_gate', 'v_ffn_w_up', 'v_ffn_w_down', 'v_final_g']
TWIN_OUTPUTS = ['loss', 'grad_x', 'grad_norm_mix_g', 'grad_norm_ffn_g', 'grad_a_w_in', 'grad_a_conv_w', 'grad_a_conv_b', 'grad_a_w_r', 'grad_a_b_r', 'grad_a_w_i', 'grad_a_b_i', 'grad_a_lambda', 'grad_a_w_out', 'grad_b_w_qkv', 'grad_b_w_out', 'grad_ffn_w_gate', 'grad_ffn_w_up', 'grad_ffn_w_down', 'grad_final_g', 'delta_norm_mix_g', 'delta_norm_ffn_g', 'delta_a_w_in', 'delta_a_conv_w', 'delta_a_conv_b', 'delta_a_w_r', 'delta_a_b_r', 'delta_a_w_i', 'delta_a_b_i', 'delta_a_lambda', 'delta_a_w_out', 'delta_b_w_qkv', 'delta_b_w_out', 'delta_ffn_w_gate', 'delta_ffn_w_up', 'delta_ffn_w_down', 'delta_final_g', 'new_m_norm_mix_g', 'new_m_norm_ffn_g', 'new_m_a_w_in', 'new_m_a_conv_w', 'new_m_a_conv_b', 'new_m_a_w_r', 'new_m_a_b_r', 'new_m_a_w_i', 'new_m_a_b_i', 'new_m_a_lambda', 'new_m_a_w_out', 'new_m_b_w_qkv', 'new_m_b_w_out', 'new_m_ffn_w_gate', 'new_m_ffn_w_up', 'new_m_ffn_w_down', 'new_m_final_g', 'new_v_norm_mix_g', 'new_v_norm_ffn_g', 'new_v_a_w_in', 'new_v_a_conv_w', 'new_v_a_conv_b', 'new_v_a_w_r', 'new_v_a_b_r', 'new_v_a_w_i', 'new_v_a_b_i', 'new_v_a_lambda', 'new_v_a_w_out', 'new_v_b_w_qkv', 'new_v_b_w_out', 'new_v_ffn_w_gate', 'new_v_ffn_w_up', 'new_v_ffn_w_down', 'new_v_final_g']
TWIN_LEAF_KINDS = {'loss': 'loss', 'grad_x': 'grad_x', 'grad_norm_mix_g': 'grad_w', 'grad_norm_ffn_g': 'grad_w', 'grad_a_w_in': 'grad_w', 'grad_a_conv_w': 'grad_w', 'grad_a_conv_b': 'grad_w', 'grad_a_w_r': 'grad_w', 'grad_a_b_r': 'grad_w', 'grad_a_w_i': 'grad_w', 'grad_a_b_i': 'grad_w', 'grad_a_lambda': 'grad_w', 'grad_a_w_out': 'grad_w', 'grad_b_w_qkv': 'grad_w', 'grad_b_w_out': 'grad_w', 'grad_ffn_w_gate': 'grad_w', 'grad_ffn_w_up': 'grad_w', 'grad_ffn_w_down': 'grad_w', 'grad_final_g': 'grad_w', 'delta_norm_mix_g': 'delta_w', 'delta_norm_ffn_g': 'delta_w', 'delta_a_w_in': 'delta_w', 'delta_a_conv_w': 'delta_w', 'delta_a_conv_b': 'delta_w', 'delta_a_w_r': 'delta_w', 'delta_a_b_r': 'delta_w', 'delta_a_w_i': 'delta_w', 'delta_a_b_i': 'delta_w', 'delta_a_lambda': 'delta_w', 'delta_a_w_out': 'delta_w', 'delta_b_w_qkv': 'delta_w', 'delta_b_w_out': 'delta_w', 'delta_ffn_w_gate': 'delta_w', 'delta_ffn_w_up': 'delta_w', 'delta_ffn_w_down': 'delta_w', 'delta_final_g': 'delta_w', 'new_m_norm_mix_g': 'new_m', 'new_m_norm_ffn_g': 'new_m', 'new_m_a_w_in': 'new_m', 'new_m_a_conv_w': 'new_m', 'new_m_a_conv_b': 'new_m', 'new_m_a_w_r': 'new_m', 'new_m_a_b_r': 'new_m', 'new_m_a_w_i': 'new_m', 'new_m_a_b_i': 'new_m', 'new_m_a_lambda': 'new_m', 'new_m_a_w_out': 'new_m', 'new_m_b_w_qkv': 'new_m', 'new_m_b_w_out': 'new_m', 'new_m_ffn_w_gate': 'new_m', 'new_m_ffn_w_up': 'new_m', 'new_m_ffn_w_down': 'new_m', 'new_m_final_g': 'new_m', 'new_v_norm_mix_g': 'new_v', 'new_v_norm_ffn_g': 'new_v', 'new_v_a_w_in': 'new_v', 'new_v_a_conv_w': 'new_v', 'new_v_a_conv_b': 'new_v', 'new_v_a_w_r': 'new_v', 'new_v_a_b_r': 'new_v', 'new_v_a_w_i': 'new_v', 'new_v_a_b_i': 'new_v', 'new_v_a_lambda': 'new_v', 'new_v_a_w_out': 'new_v', 'new_v_b_w_qkv': 'new_v', 'new_v_b_w_out': 'new_v', 'new_v_ffn_w_gate': 'new_v', 'new_v_ffn_w_up': 'new_v', 'new_v_ffn_w_down': 'new_v', 'new_v_final_g': 'new_v'}


def _forward(args):
    return _fwd_reference(*[args[k] for k in FWD_PARAMS])


def _output_shape():
    def fwd():
        inp = _fwd_setup_inputs(0)
        return _fwd_reference(*[inp[k] for k in FWD_PARAMS])
    out = _jax.eval_shape(fwd)
    return out.shape, out.dtype

N_MICROBATCH = 1
ADAM_LR = 0.001
ADAM_B1 = 0.9
ADAM_B2 = 0.999
ADAM_EPS = 1e-08
ADAM_WD = 0.01
ADAM_STEP = 10
PER_EXAMPLE_BATCH_AXIS = {'x': 0, 'loss_target': 0}
SHARED_INPUTS = []
_WEIGHT_DTYPES = {'norm_mix_g': _jnp.float32, 'norm_ffn_g': _jnp.float32, 'a_w_in': _jnp.float32, 'a_conv_w': _jnp.float32, 'a_conv_b': _jnp.float32, 'a_w_r': _jnp.float32, 'a_b_r': _jnp.float32, 'a_w_i': _jnp.float32, 'a_b_i': _jnp.float32, 'a_lambda': _jnp.float32, 'a_w_out': _jnp.float32, 'b_w_qkv': _jnp.float32, 'b_w_out': _jnp.float32, 'ffn_w_gate': _jnp.float32, 'ffn_w_up': _jnp.float32, 'ffn_w_down': _jnp.float32, 'final_g': _jnp.float32}
MOMENT_SCALE = {'norm_mix_g': 1.710173e-01, 'norm_ffn_g': 1.762725e-01, 'a_w_in': 1.268804e-01, 'a_conv_w': 1.375838e-01, 'a_conv_b': 1.553165e+00, 'a_w_r': 3.682957e-02, 'a_b_r': 3.646282e-02, 'a_w_i': 6.587282e-02, 'a_b_i': 5.164548e-02, 'a_lambda': 7.292684e-02, 'a_w_out': 1.286541e-01, 'b_w_qkv': 8.763578e-02, 'b_w_out': 1.271916e-01, 'ffn_w_gate': 7.538669e-02, 'ffn_w_up': 7.300390e-02, 'ffn_w_down': 1.209673e-01, 'final_g': 6.406202e+01}


def _to_microbatches(a, axis):
    t = _jnp.moveaxis(a, axis, 0)
    t = t.reshape((N_MICROBATCH, t.shape[0] // N_MICROBATCH) + t.shape[1:])
    return _jnp.moveaxis(t, 1, axis + 1)


def setup_inputs(seed: int = 0) -> dict:
    inp = _fwd_setup_inputs(seed)
    key = _jax.random.fold_in(_jax.random.key(seed), 7919)
    shape, _ = _output_shape()
    out = dict(inp)
    out["loss_target"] = _jax.random.normal(_jax.random.fold_in(key, 0), shape, _jnp.float32)
    for i, name in enumerate(TWIN_WEIGHTS):
        w = inp[name].astype(_jnp.float32)
        if MOMENT_SCALE is None:
            s = _jnp.sqrt(_jnp.mean(_jnp.square(w)) + 1e-30)
        else:
            s = MOMENT_SCALE[name]
        km, kv = _jax.random.split(_jax.random.fold_in(key, i + 1))
        out[name] = w
        out["m_" + name] = s * _jax.random.normal(km, w.shape, _jnp.float32)
        out["v_" + name] = (s * s) * _jax.random.uniform(kv, w.shape, _jnp.float32, 0.5, 1.5)
    if N_MICROBATCH > 1:
        for name, axis in PER_EXAMPLE_BATCH_AXIS.items():
            out[name] = _to_microbatches(out[name], axis)
    return {'x': out['x'], 'norm_mix_g': out['norm_mix_g'], 'norm_ffn_g': out['norm_ffn_g'], 'a_w_in': out['a_w_in'], 'a_conv_w': out['a_conv_w'], 'a_conv_b': out['a_conv_b'], 'a_w_r': out['a_w_r'], 'a_b_r': out['a_b_r'], 'a_w_i': out['a_w_i'], 'a_b_i': out['a_b_i'], 'a_lambda': out['a_lambda'], 'a_w_out': out['a_w_out'], 'b_w_qkv': out['b_w_qkv'], 'b_w_out': out['b_w_out'], 'ffn_w_gate': out['ffn_w_gate'], 'ffn_w_up': out['ffn_w_up'], 'ffn_w_down': out['ffn_w_down'], 'final_g': out['final_g'], 'loss_target': out['loss_target'], 'm_norm_mix_g': out['m_norm_mix_g'], 'm_norm_ffn_g': out['m_norm_ffn_g'], 'm_a_w_in': out['m_a_w_in'], 'm_a_conv_w': out['m_a_conv_w'], 'm_a_conv_b': out['m_a_conv_b'], 'm_a_w_r': out['m_a_w_r'], 'm_a_b_r': out['m_a_b_r'], 'm_a_w_i': out['m_a_w_i'], 'm_a_b_i': out['m_a_b_i'], 'm_a_lambda': out['m_a_lambda'], 'm_a_w_out': out['m_a_w_out'], 'm_b_w_qkv': out['m_b_w_qkv'], 'm_b_w_out': out['m_b_w_out'], 'm_ffn_w_gate': out['m_ffn_w_gate'], 'm_ffn_w_up': out['m_ffn_w_up'], 'm_ffn_w_down': out['m_ffn_w_down'], 'm_final_g': out['m_final_g'], 'v_norm_mix_g': out['v_norm_mix_g'], 'v_norm_ffn_g': out['v_norm_ffn_g'], 'v_a_w_in': out['v_a_w_in'], 'v_a_conv_w': out['v_a_conv_w'], 'v_a_conv_b': out['v_a_conv_b'], 'v_a_w_r': out['v_a_w_r'], 'v_a_b_r': out['v_a_b_r'], 'v_a_w_i': out['v_a_w_i'], 'v_a_b_i': out['v_a_b_i'], 'v_a_lambda': out['v_a_lambda'], 'v_a_w_out': out['v_a_w_out'], 'v_b_w_qkv': out['v_b_w_qkv'], 'v_b_w_out': out['v_b_w_out'], 'v_ffn_w_gate': out['v_ffn_w_gate'], 'v_ffn_w_up': out['v_ffn_w_up'], 'v_ffn_w_down': out['v_ffn_w_down'], 'v_final_g': out['v_final_g']}


def _loss(weights, diff, rest, loss_target):
    with _jax.named_scope("forward"):
        args = {**rest, TWIN_DIFF_INPUT: diff, **{k: w.astype(_WEIGHT_DTYPES[k]) for k, w in weights.items()}}
        y = _forward(args)
    with _jax.named_scope("loss_head"):
        err = _jnp.square(y.astype(_jnp.float32) - loss_target)
        return 0.5 * _jnp.sum(_jnp.mean(err, axis=-1)) if err.ndim else 0.5 * err


def _adamw(w, g, m, v):
    m = ADAM_B1 * m + (1.0 - ADAM_B1) * g
    v = ADAM_B2 * v + (1.0 - ADAM_B2) * _jnp.square(g)
    m_hat = m / (1.0 - ADAM_B1 ** ADAM_STEP)
    v_hat = v / (1.0 - ADAM_B2 ** ADAM_STEP)
    delta = -ADAM_LR * (m_hat / (_jnp.sqrt(v_hat) + ADAM_EPS) + ADAM_WD * w)
    return delta, m, v


def reference(x, norm_mix_g, norm_ffn_g, a_w_in, a_conv_w, a_conv_b, a_w_r, a_b_r, a_w_i, a_b_i, a_lambda, a_w_out, b_w_qkv, b_w_out, ffn_w_gate, ffn_w_up, ffn_w_down, final_g, loss_target, m_norm_mix_g, m_norm_ffn_g, m_a_w_in, m_a_conv_w, m_a_conv_b, m_a_w_r, m_a_b_r, m_a_w_i, m_a_b_i, m_a_lambda, m_a_w_out, m_b_w_qkv, m_b_w_out, m_ffn_w_gate, m_ffn_w_up, m_ffn_w_down, m_final_g, v_norm_mix_g, v_norm_ffn_g, v_a_w_in, v_a_conv_w, v_a_conv_b, v_a_w_r, v_a_b_r, v_a_w_i, v_a_b_i, v_a_lambda, v_a_w_out, v_b_w_qkv, v_b_w_out, v_ffn_w_gate, v_ffn_w_up, v_ffn_w_down, v_final_g):
    given = dict(x=x, norm_mix_g=norm_mix_g, norm_ffn_g=norm_ffn_g, a_w_in=a_w_in, a_conv_w=a_conv_w, a_conv_b=a_conv_b, a_w_r=a_w_r, a_b_r=a_b_r, a_w_i=a_w_i, a_b_i=a_b_i, a_lambda=a_lambda, a_w_out=a_w_out, b_w_qkv=b_w_qkv, b_w_out=b_w_out, ffn_w_gate=ffn_w_gate, ffn_w_up=ffn_w_up, ffn_w_down=ffn_w_down, final_g=final_g, loss_target=loss_target, m_norm_mix_g=m_norm_mix_g, m_norm_ffn_g=m_norm_ffn_g, m_a_w_in=m_a_w_in, m_a_conv_w=m_a_conv_w, m_a_conv_b=m_a_conv_b, m_a_w_r=m_a_w_r, m_a_b_r=m_a_b_r, m_a_w_i=m_a_w_i, m_a_b_i=m_a_b_i, m_a_lambda=m_a_lambda, m_a_w_out=m_a_w_out, m_b_w_qkv=m_b_w_qkv, m_b_w_out=m_b_w_out, m_ffn_w_gate=m_ffn_w_gate, m_ffn_w_up=m_ffn_w_up, m_ffn_w_down=m_ffn_w_down, m_final_g=m_final_g, v_norm_mix_g=v_norm_mix_g, v_norm_ffn_g=v_norm_ffn_g, v_a_w_in=v_a_w_in, v_a_conv_w=v_a_conv_w, v_a_conv_b=v_a_conv_b, v_a_w_r=v_a_w_r, v_a_b_r=v_a_b_r, v_a_w_i=v_a_w_i, v_a_b_i=v_a_b_i, v_a_lambda=v_a_lambda, v_a_w_out=v_a_w_out, v_b_w_qkv=v_b_w_qkv, v_b_w_out=v_b_w_out, v_ffn_w_gate=v_ffn_w_gate, v_ffn_w_up=v_ffn_w_up, v_ffn_w_down=v_ffn_w_down, v_final_g=v_final_g)
    weights = {n: given[n] for n in TWIN_WEIGHTS}
    shared = {n: given[n] for n in SHARED_INPUTS}
    per_example = {n: given[n] for n in ['x']}
    grad_fn = _jax.value_and_grad(_loss, argnums=(0, 1))

    def one_microbatch(ex, loss_target):
        ex = dict(ex)
        diff = ex.pop(TWIN_DIFF_INPUT)
        return grad_fn(weights, diff, {**shared, **ex}, loss_target)

    if N_MICROBATCH == 1:
        loss, (grad_w, grad_x) = one_microbatch(per_example, given["loss_target"])
    else:
        def body(carry, xs):
            loss_sum, grad_sum = carry
            l_k, (gw_k, gx_k) = one_microbatch(xs[0], xs[1])
            with _jax.named_scope("update"):
                return (loss_sum + l_k, _jax.tree.map(_jnp.add, grad_sum, gw_k)), gx_k

        init = (_jnp.zeros((), _jnp.float32), _jax.tree.map(_jnp.zeros_like, weights))
        (loss, grad_w), grad_x = _jax.lax.scan(body, init, (per_example, given["loss_target"]))
    with _jax.named_scope("update"):
        delta_w, new_m, new_v = {}, {}, {}
        for n in TWIN_WEIGHTS:
            delta_w[n], new_m[n], new_v[n] = _adamw(weights[n], grad_w[n], given["m_" + n], given["v_" + n])
    return (loss, grad_x, *[grad_w[n] for n in TWIN_WEIGHTS], *[delta_w[n] for n in TWIN_WEIGHTS],
            *[new_m[n] for n in TWIN_WEIGHTS], *[new_v[n] for n in TWIN_WEIGHTS])
```

```python
import functools
import math

import jax
import jax.numpy as jnp
from jax import lax
from jax.experimental import pallas as pl
from jax.experimental.pallas import tpu as pltpu

F32 = jnp.float32
CD = jnp.bfloat16

D_MODEL = 1024
D_RNN = 1024
RG_BLOCKS = 4
RG_BW = 256
CONV_W = 4
RG_C = 8.0
SB_HEADS = 16
SB_HEAD_DIM = 64
D_FF = 2816
RMS_EPS = 1e-6
N_CHIPS = 4
N_DEV = 8

ADAM_LR = 0.001
ADAM_B1 = 0.9
ADAM_B2 = 0.999
ADAM_EPS = 1e-08
ADAM_WD = 0.01
ADAM_STEP = 10

LANES = 128
VMEM_LIMIT = 56 * 1024 * 1024
MESH = pl.DeviceIdType.MESH


def _params(*sem):
    return pltpu.CompilerParams(dimension_semantics=sem, vmem_limit_bytes=VMEM_LIMIT)


def _pick(n, prefs):
    for p in prefs:
        if n % p == 0:
            return p
    return n


def _matmul(pairs, out_dtype, name, *, trans_a=False, a_lbm=False, b_lbm=False, out_lbm=False, addend=None,
            tm=512, tn=None, tk=None):
    a0, b0 = pairs[0]
    if trans_a:
        kdim = a0.shape[1] if a_lbm else a0.shape[0]
        m = a0.shape[0] * LANES if a_lbm else a0.shape[1]
    else:
        m = a0.shape[1] if a_lbm else a0.shape[0]
        kdim = a0.shape[0] * LANES if a_lbm else a0.shape[1]
    n = b0.shape[0] * LANES if b_lbm else b0.shape[1]
    tm = _pick(m, (tm, 1408, 256, 128))
    tn = tn or _pick(n, (1408, 1024, 768, 512, 256, 128))
    tk = tk or _pick(kdim, (1024, 1408, 512, 256, 128))
    nk = kdim // tk
    npair = len(pairs)

    def cat(ref):
        return jnp.concatenate([ref[p] for p in range(ref.shape[0])], axis=-1)

    def body(*refs):
        ins = refs[: 2 * npair]
        pos = 2 * npair
        add_ref = None
        if addend is not None:
            add_ref = refs[pos]
            pos += 1
        o_ref = refs[pos]
        acc_ref = refs[pos + 1]
        k = pl.program_id(2)

        @pl.when(k == 0)
        def _():
            acc_ref[...] = jnp.zeros_like(acc_ref)

        acc = acc_ref[...]
        for p in range(npair):
            a = (cat(ins[2 * p]) if a_lbm else ins[2 * p][...]).astype(CD)
            b = (cat(ins[2 * p + 1]) if b_lbm else ins[2 * p + 1][...]).astype(CD)
            dims = (((0,), (0,)), ((), ())) if trans_a else (((1,), (0,)), ((), ()))
            acc = acc + lax.dot_general(a, b, dims, preferred_element_type=F32)
        acc_ref[...] = acc

        @pl.when(k == nk - 1)
        def _():
            res = acc_ref[...]
            if add_ref is not None:
                res = res + add_ref[...]
            res = res.astype(out_dtype)
            if out_lbm:
                for p in range(tn // LANES):
                    o_ref[p] = res[:, p * LANES:(p + 1) * LANES]
            else:
                o_ref[...] = res

    if trans_a:
        a_spec = (pl.BlockSpec((tm // LANES, tk, LANES), lambda i, j, k: (i, k, 0)) if a_lbm
                  else pl.BlockSpec((tk, tm), lambda i, j, k: (k, i)))
    else:
        a_spec = (pl.BlockSpec((tk // LANES, tm, LANES), lambda i, j, k: (k, i, 0)) if a_lbm
                  else pl.BlockSpec((tm, tk), lambda i, j, k: (i, k)))
    b_spec = (pl.BlockSpec((tn // LANES, tk, LANES), lambda i, j, k: (j, k, 0)) if b_lbm
              else pl.BlockSpec((tk, tn), lambda i, j, k: (k, j)))
    in_specs = []
    args = []
    for a, b in pairs:
        in_specs += [a_spec, b_spec]
        args += [a, b]
    if addend is not None:
        in_specs.append(pl.BlockSpec((tm, tn), lambda i, j, k: (i, j)))
        args.append(addend)
    if out_lbm:
        out_shape = jax.ShapeDtypeStruct((n // LANES, m, LANES), out_dtype)
        out_spec = pl.BlockSpec((tn // LANES, tm, LANES), lambda i, j, k: (j, i, 0))
    else:
        out_shape = jax.ShapeDtypeStruct((m, n), out_dtype)
        out_spec = pl.BlockSpec((tm, tn), lambda i, j, k: (i, j))
    return pl.pallas_call(
        body, name=name, out_shape=out_shape, grid=(m // tm, n // tn, nk),
        in_specs=in_specs, out_specs=out_spec,
        scratch_shapes=[pltpu.VMEM((tm, tn), F32)],
        compiler_params=_params("parallel", "parallel", "arbitrary"),
    )(*args)


ROW_BLOCK = 256


def _rms_fwd(x, g, name):
    s, d = x.shape

    def body(x_ref, g_ref, h_ref):
        xv = x_ref[...]
        rinv = lax.rsqrt(jnp.mean(xv * xv, axis=-1, keepdims=True) + RMS_EPS)
        h_ref[...] = (xv * rinv * g_ref[...]).astype(CD)

    return pl.pallas_call(
        body, name=name, out_shape=jax.ShapeDtypeStruct((s, d), CD), grid=(s // ROW_BLOCK,),
        in_specs=[pl.BlockSpec((ROW_BLOCK, d), lambda i: (i, 0)), pl.BlockSpec((1, d), lambda i: (0, 0))],
        out_specs=pl.BlockSpec((ROW_BLOCK, d), lambda i: (i, 0)),
        compiler_params=_params("parallel"),
    )(x, g.reshape(1, d))


def _rms_bwd(dh, x, g, dx_in, name):
    s, d = x.shape

    def body(dh_ref, x_ref, g_ref, dxin_ref, dx_ref, dg_ref):
        @pl.when(pl.program_id(0) == 0)
        def _():
            dg_ref[...] = jnp.zeros_like(dg_ref)

        xv = x_ref[...]
        dhv = dh_ref[...]
        rinv = lax.rsqrt(jnp.mean(xv * xv, axis=-1, keepdims=True) + RMS_EPS)
        nrm = xv * rinv
        dn = dhv * g_ref[...]
        dx_ref[...] = dxin_ref[...] + rinv * (dn - nrm * jnp.mean(dn * nrm, axis=-1, keepdims=True))
        dg_ref[...] += jnp.sum(dhv * nrm, axis=0, keepdims=True)

    row = pl.BlockSpec((ROW_BLOCK, d), lambda i: (i, 0))
    vec = pl.BlockSpec((1, d), lambda i: (0, 0))
    return pl.pallas_call(
        body, name=name,
        out_shape=(jax.ShapeDtypeStruct((s, d), F32), jax.ShapeDtypeStruct((1, d), F32)),
        grid=(s // ROW_BLOCK,), in_specs=[row, row, vec, row], out_specs=(row, vec),
        compiler_params=_params("arbitrary"),
    )(dh, x, g.reshape(1, d), dx_in)


def _loss_head(x, g, target, name):
    s, d = x.shape

    def body(x_ref, g_ref, t_ref, loss_ref, dx_ref, dg_ref):
        @pl.when(pl.program_id(0) == 0)
        def _():
            dg_ref[...] = jnp.zeros_like(dg_ref)
            loss_ref[...] = jnp.zeros_like(loss_ref)

        xv = x_ref[...]
        gv = g_ref[...]
        rinv = lax.rsqrt(jnp.mean(xv * xv, axis=-1, keepdims=True) + RMS_EPS)
        nrm = xv * rinv
        err = nrm * gv - t_ref[...]
        loss_ref[...] += 0.5 * jnp.sum(jnp.mean(err * err, axis=-1, keepdims=True), axis=0, keepdims=True)
        dy = err * (1.0 / d)
        dn = dy * gv
        dx_ref[...] = rinv * (dn - nrm * jnp.mean(dn * nrm, axis=-1, keepdims=True))
        dg_ref[...] += jnp.sum(dy * nrm, axis=0, keepdims=True)

    row = pl.BlockSpec((ROW_BLOCK, d), lambda i: (i, 0))
    vec = pl.BlockSpec((1, d), lambda i: (0, 0))
    return pl.pallas_call(
        body, name=name,
        out_shape=(jax.ShapeDtypeStruct((1, LANES), F32), jax.ShapeDtypeStruct((s, d), F32),
                   jax.ShapeDtypeStruct((1, d), F32)),
        grid=(s // ROW_BLOCK,), in_specs=[row, vec, row],
        out_specs=(pl.BlockSpec((1, LANES), lambda i: (0, 0)), row, vec),
        compiler_params=_params("arbitrary"),
    )(x, g.reshape(1, d), target)


def _sigmoid(z):
    return 1.0 / (1.0 + jnp.exp(-z))


def _swiglu_fwd(g, u, name):
    s, f = g.shape

    def body(g_ref, u_ref, a_ref):
        gv = g_ref[...]
        a_ref[...] = (gv * _sigmoid(gv) * u_ref[...]).astype(CD)

    row = pl.BlockSpec((ROW_BLOCK, f), lambda i: (i, 0))
    return pl.pallas_call(
        body, name=name, out_shape=jax.ShapeDtypeStruct((s, f), CD), grid=(s // ROW_BLOCK,),
        in_specs=[row, row], out_specs=row, compiler_params=_params("parallel"),
    )(g, u)


def _swiglu_bwd(dact, g, u, name):
    s, f = g.shape

    def body(da_ref, g_ref, u_ref, dg_ref, du_ref):
        gv = g_ref[...]
        da = da_ref[...]
        sg = _sigmoid(gv)
        silu = gv * sg
        dg_ref[...] = (da * u_ref[...] * (sg + silu * (1.0 - sg))).astype(CD)
        du_ref[...] = (da * silu).astype(CD)

    row = pl.BlockSpec((ROW_BLOCK, f), lambda i: (i, 0))
    return pl.pallas_call(
        body, name=name,
        out_shape=(jax.ShapeDtypeStruct((s, f), CD), jax.ShapeDtypeStruct((s, f), CD)),
        grid=(s // ROW_BLOCK,), in_specs=[row, row, row], out_specs=(row, row),
        compiler_params=_params("parallel"),
    )(dact, g, u)


TIME_BLOCK = 256
SUBLANES = 8
GELU_C = math.sqrt(2.0 / math.pi)
GELU_A = 0.044715


def _gelu(x):
    return 0.5 * x * (1.0 + jnp.tanh(GELU_C * (x + GELU_A * x * x * x)))


def _gelu_grad(x):
    t = jnp.tanh(GELU_C * (x + GELU_A * x * x * x))
    return 0.5 * (1.0 + t) + 0.5 * x * (1.0 - t * t) * GELU_C * (1.0 + 3.0 * GELU_A * x * x)


def _neg_expm1(x):
    series = -x * (1.0 + x * (0.5 + x * (1.0 / 6.0 + x * (1.0 / 24.0))))
    return jnp.where(x > -0.05, series, 1.0 - jnp.exp(x))


def _log_sigmoid(x):
    return jnp.minimum(x, 0.0) - jnp.log1p(jnp.exp(-jnp.abs(x)))


def _shift_down(x, tail, s):
    if s == 0:
        return x
    ext = jnp.concatenate([tail, x], axis=0)
    return pltpu.roll(ext, s, axis=0)[SUBLANES:]


def _shift_up(x, head, s):
    if s == 0:
        return x
    n = x.shape[0]
    ext = jnp.concatenate([x, head], axis=0)
    return pltpu.roll(ext, n + SUBLANES - s, axis=0)[:n]


def _rg_gates(xbr, tail, cw_ref, cb, wr, wi, br, bi, ls):
    taps = [_shift_down(xbr, tail, CONV_W - 1 - k) for k in range(CONV_W)]
    xc = cb
    for k in range(CONV_W):
        xc = xc + cw_ref[pl.ds(k, 1), :] * taps[k]
    xcd = xc.astype(CD)
    r = _sigmoid(jnp.dot(xcd, wr, preferred_element_type=F32) + br)
    i = _sigmoid(jnp.dot(xcd, wi, preferred_element_type=F32) + bi)
    log_a = RG_C * r * ls
    a = jnp.exp(log_a)
    mult = jnp.sqrt(jnp.maximum(_neg_expm1(2.0 * log_a), 0.0))
    return taps, xc, r, i, log_a, a, mult


def _scan8_fwd(a, u):
    row = lax.broadcasted_iota(jnp.int32, a.shape, 0)
    for d in (1, 2, 4):
        a_s = pltpu.roll(a, d, axis=0)
        u_s = pltpu.roll(u, d, axis=0)
        m = row >= d
        u = jnp.where(m, a * u_s + u, u)
        a = jnp.where(m, a * a_s, a)
    return a, u


def _scan8_bwd(b, u):
    row = lax.broadcasted_iota(jnp.int32, b.shape, 0)
    for d in (1, 2, 4):
        b_s = pltpu.roll(b, SUBLANES - d, axis=0)
        u_s = pltpu.roll(u, SUBLANES - d, axis=0)
        m = row < SUBLANES - d
        u = jnp.where(m, b * u_s + u, u)
        b = jnp.where(m, b * b_s, b)
    return b, u


def _rglru_fwd(gate_br, x_br, cw, cb, wr, wi, br, bi, lam, name):
    s, c = x_br.shape
    nt = s // TIME_BLOCK
    tb, cbw = TIME_BLOCK, RG_BW
    groups = tb // SUBLANES

    def body(g_ref, x_ref, tail_ref, cw_ref, cb_ref, wr_ref, wi_ref, br_ref, bi_ref, lam_ref,
             y_ref, hs_ref, carry_ref, a_scr, u_scr):
        t = pl.program_id(1)

        @pl.when(t == 0)
        def _():
            carry_ref[...] = jnp.zeros_like(carry_ref)

        tail = jnp.where(t > 0, tail_ref[...], 0.0)
        ls = _log_sigmoid(lam_ref[...])
        _, xc, _, i, _, a, mult = _rg_gates(x_ref[...], tail, cw_ref, cb_ref[...], wr_ref[0], wi_ref[0],
                                            br_ref[...], bi_ref[...], ls)
        a_scr[...] = a
        u_scr[...] = mult * (i * xc)
        carry = carry_ref[...]
        for gi in range(groups):
            rows = pl.ds(gi * SUBLANES, SUBLANES)
            pa, hl = _scan8_fwd(a_scr[rows, :], u_scr[rows, :])
            hs_ref[rows, :] = hl + pa * carry
            carry = hs_ref[pl.ds(gi * SUBLANES + SUBLANES - 1, 1), :]
        carry_ref[...] = carry
        y_ref[...] = (hs_ref[...] * _gelu(g_ref[...])).astype(CD)

    blk = pl.BlockSpec((tb, cbw), lambda n, t: (t, n))
    tail = pl.BlockSpec((SUBLANES, cbw), lambda n, t: (jnp.maximum(t * groups - 1, 0), n))
    vec = pl.BlockSpec((1, cbw), lambda n, t: (0, n))
    wblk = pl.BlockSpec((1, cbw, cbw), lambda n, t: (n, 0, 0))
    return pl.pallas_call(
        body, name=name,
        out_shape=(jax.ShapeDtypeStruct((s, c), CD), jax.ShapeDtypeStruct((s, c), F32)),
        grid=(RG_BLOCKS, nt),
        in_specs=[blk, blk, tail, pl.BlockSpec((CONV_W, cbw), lambda n, t: (0, n)), vec, wblk, wblk, vec, vec, vec],
        out_specs=(blk, blk),
        scratch_shapes=[pltpu.VMEM((1, cbw), F32), pltpu.VMEM((tb, cbw), F32), pltpu.VMEM((tb, cbw), F32)],
        compiler_params=_params("parallel", "arbitrary"),
    )(gate_br, x_br, x_br, cw, cb, wr, wi, br, bi, lam)


def _rglru_bwd(dy, gate_br, x_br, hs, cw, cb, wr, wi, wrt, wit, br, bi, lam, name):
    s, c = x_br.shape
    nt = s // TIME_BLOCK
    tb, cbw = TIME_BLOCK, RG_BW
    groups = tb // SUBLANES

    def body(dy_ref, g_ref, x_ref, tail_ref, hs_ref, hprev_ref, cw_ref, cb_ref, wr_ref, wi_ref, wrt_ref, wit_ref,
             br_ref, bi_ref, lam_ref,
             dg_ref, dx_ref, dcw_ref, dcb_ref, dbr_ref, dbi_ref, dlam_ref, dwr_ref, dwi_ref,
             carry_ref, head_ref, b_scr, u_scr, dh_scr):
        tr = pl.program_id(1)
        first_block = tr == nt - 1

        @pl.when(tr == 0)
        def _():
            carry_ref[...] = jnp.zeros_like(carry_ref)
            head_ref[...] = jnp.zeros_like(head_ref)
            for ref in (dcw_ref, dcb_ref, dbr_ref, dbi_ref, dlam_ref, dwr_ref, dwi_ref):
                ref[...] = jnp.zeros_like(ref)

        tail = jnp.where(first_block, 0.0, tail_ref[...])
        lam_v = lam_ref[...]
        ls = _log_sigmoid(lam_v)
        taps, xc, r, i, log_a, a, mult = _rg_gates(x_ref[...], tail, cw_ref, cb_ref[...], wr_ref[0], wi_ref[0],
                                                   br_ref[...], bi_ref[...], ls)
        gate_v = g_ref[...]
        dyv = dy_ref[...]
        hsv = hs_ref[...]
        dg_ref[...] = (dyv * hsv * _gelu_grad(gate_v)).astype(CD)

        row = lax.broadcasted_iota(jnp.int32, a.shape, 0)
        b_scr[...] = jnp.where(row == tb - 1, 1.0, pltpu.roll(a, tb - 1, axis=0))
        u_scr[...] = dyv * _gelu(gate_v)
        carry = carry_ref[...]
        for gi in reversed(range(groups)):
            rows = pl.ds(gi * SUBLANES, SUBLANES)
            pb, gl = _scan8_bwd(b_scr[rows, :], u_scr[rows, :])
            dh_scr[rows, :] = gl + pb * carry
            carry = dh_scr[pl.ds(gi * SUBLANES, 1), :]
        dh = dh_scr[...]
        carry_ref[...] = carry * jnp.sum(jnp.where(row == 0, a, 0.0), axis=0, keepdims=True)

        hprev_tail = jnp.where(first_block, 0.0, hprev_ref[...])
        h_prev = _shift_down(hsv, hprev_tail, 1)
        da = dh * h_prev
        ixc = i * xc
        dmult = dh * ixc
        di = dh * mult * xc
        dxc = dh * mult * i
        a2 = a * a
        dlog_a = da * a - dmult * a2 / mult
        dpre_r = (dlog_a * (RG_C * ls)) * r * (1.0 - r)
        dpre_i = di * i * (1.0 - i)
        dlam_ref[...] += jnp.sum(dlog_a * r, axis=0, keepdims=True) * (RG_C * _sigmoid(-lam_v))
        dbr_ref[...] += jnp.sum(dpre_r, axis=0, keepdims=True)
        dbi_ref[...] += jnp.sum(dpre_i, axis=0, keepdims=True)
        xcd = xc.astype(CD)
        dprc = dpre_r.astype(CD)
        dpic = dpre_i.astype(CD)
        tn_dims = (((0,), (0,)), ((), ()))
        dwr_ref[0] += lax.dot_general(xcd, dprc, tn_dims, preferred_element_type=F32)
        dwi_ref[0] += lax.dot_general(xcd, dpic, tn_dims, preferred_element_type=F32)
        dxc = dxc + jnp.dot(dprc, wrt_ref[0], preferred_element_type=F32) + jnp.dot(dpic, wit_ref[0],
                                                                                    preferred_element_type=F32)
        dcb_ref[...] += jnp.sum(dxc, axis=0, keepdims=True)
        for k in range(CONV_W):
            dcw_ref[pl.ds(k, 1), :] += jnp.sum(dxc * taps[k], axis=0, keepdims=True)
        head = head_ref[...]
        dxb = jnp.zeros_like(dxc)
        for sft in range(CONV_W):
            dxb = dxb + cw_ref[pl.ds(CONV_W - 1 - sft, 1), :] * _shift_up(dxc, head, sft)
        dx_ref[...] = dxb.astype(CD)
        head_ref[...] = dxc[0:SUBLANES, :]

    blk = pl.BlockSpec((tb, cbw), lambda n, t: (nt - 1 - t, n))
    tail = pl.BlockSpec((SUBLANES, cbw), lambda n, t: (jnp.maximum((nt - 1 - t) * groups - 1, 0), n))
    vec = pl.BlockSpec((1, cbw), lambda n, t: (0, n))
    cwb = pl.BlockSpec((CONV_W, cbw), lambda n, t: (0, n))
    wblk = pl.BlockSpec((1, cbw, cbw), lambda n, t: (n, 0, 0))
    vshape = jax.ShapeDtypeStruct((1, c), F32)
    wshape = jax.ShapeDtypeStruct((RG_BLOCKS, cbw, cbw), F32)
    return pl.pallas_call(
        body, name=name,
        out_shape=(jax.ShapeDtypeStruct((s, c), CD), jax.ShapeDtypeStruct((s, c), CD),
                   jax.ShapeDtypeStruct((CONV_W, c), F32), vshape, vshape, vshape, vshape, wshape, wshape),
        grid=(RG_BLOCKS, nt),
        in_specs=[blk, blk, blk, tail, blk, tail, cwb, vec, wblk, wblk, wblk, wblk, vec, vec, vec],
        out_specs=(blk, blk, cwb, vec, vec, vec, vec, wblk, wblk),
        scratch_shapes=[pltpu.VMEM((1, cbw), F32), pltpu.VMEM((SUBLANES, cbw), F32),
                        pltpu.VMEM((tb, cbw), F32), pltpu.VMEM((tb, cbw), F32), pltpu.VMEM((tb, cbw), F32)],
        compiler_params=_params("parallel", "arbitrary"),
    )(dy, gate_br, x_br, x_br, hs, hs, cw, cb, wr, wi, wrt, wit, br, bi, lam)


ATT_BLOCK = 256
ATT_SCALE = 1.0 / math.sqrt(SB_HEAD_DIM)
N_PAIRS = SB_HEADS * SB_HEAD_DIM // LANES
NT_DIMS = (((1,), (1,)), ((), ()))
TN_DIMS = (((0,), (0,)), ((), ()))


def _split_dot(x, m):
    hi = x.astype(CD)
    lo = (x - hi.astype(F32)).astype(CD)
    return jnp.dot(hi, m, preferred_element_type=F32) + jnp.dot(lo, m, preferred_element_type=F32)


def _sb_logits(qx, kb, valid):
    z = lax.dot_general(qx, kb, NT_DIMS, preferred_element_type=F32) * ATT_SCALE
    sp = jnp.maximum(z, 0.0) + jnp.log1p(jnp.exp(-jnp.abs(z)))
    lb = z - sp
    l1m = -sp
    if valid is not None:
        l1m = jnp.where(valid, l1m, 0.0)
    return lb, l1m


def _tri(strict):
    r = lax.broadcasted_iota(jnp.int32, (ATT_BLOCK, ATT_BLOCK), 0)
    c = lax.broadcasted_iota(jnp.int32, (ATT_BLOCK, ATT_BLOCK), 1)
    return (r > c if strict else r >= c).astype(CD)


def _attn_fwd(qkv, name):
    _, s, _ = qkv.shape
    nblk = s // ATT_BLOCK
    t = ATT_BLOCK

    def body(q_ref, k_ref, v_ref, o_ref):
        i = pl.program_id(1)
        lane = lax.broadcasted_iota(jnp.int32, (1, LANES), 1)
        head_masks = (lane < SB_HEAD_DIM, lane >= SB_HEAD_DIM)
        q = q_ref[0]
        qs = [jnp.where(m, q, jnp.zeros_like(q)) for m in head_masks]
        tri = _tri(True)
        rr = lax.broadcasted_iota(jnp.int32, (t, t), 0)
        cc = lax.broadcasted_iota(jnp.int32, (t, t), 1)
        diag_valid = cc < rr

        def block(j, carry, valid):
            run, oacc = carry
            rows = pl.ds(pl.multiple_of(j * t, t), t)
            kb = k_ref[0, rows, :]
            vb = v_ref[0, rows, :]
            new_run = []
            for hd in range(2):
                lb, l1m = _sb_logits(qs[hd], kb, valid)
                w = jnp.exp(lb + run[hd] + _split_dot(l1m, tri))
                if valid is not None:
                    w = jnp.where(valid, w, 0.0)
                vx = jnp.where(head_masks[hd], vb, jnp.zeros_like(vb))
                oacc = oacc + jnp.dot(w.astype(CD), vx, preferred_element_type=F32)
                new_run.append(run[hd] + jnp.sum(l1m, axis=1, keepdims=True))
            return tuple(new_run), oacc

        zero = jnp.zeros((t, 1), F32)
        carry = block(i, ((zero, zero), jnp.zeros((t, LANES), F32)), diag_valid)
        carry = lax.fori_loop(0, i, lambda jj, cr: block(i - 1 - jj, cr, None), carry)
        o_ref[0] = carry[1]

    return pl.pallas_call(
        body, name=name, out_shape=jax.ShapeDtypeStruct((N_PAIRS, s, LANES), F32), grid=(N_PAIRS, nblk),
        in_specs=[pl.BlockSpec((1, t, LANES), lambda p, i: (p, i, 0)),
                  pl.BlockSpec((1, s, LANES), lambda p, i: (N_PAIRS + p, 0, 0)),
                  pl.BlockSpec((1, s, LANES), lambda p, i: (2 * N_PAIRS + p, 0, 0))],
        out_specs=pl.BlockSpec((1, t, LANES), lambda p, i: (p, i, 0)),
        compiler_params=_params("parallel", "arbitrary"),
    )(qkv, qkv, qkv)


def _attn_bwd(qkv, o, do, name):
    _, s, _ = qkv.shape
    nblk = s // ATT_BLOCK
    t = ATT_BLOCK

    def body(q_ref, k_ref, v_ref, o_ref, do_ref, dq_ref, dk_ref, dv_ref):
        i = pl.program_id(1)

        @pl.when(i == 0)
        def _():
            dk_ref[...] = jnp.zeros_like(dk_ref)
            dv_ref[...] = jnp.zeros_like(dv_ref)

        lane = lax.broadcasted_iota(jnp.int32, (1, LANES), 1)
        head_masks = (lane < SB_HEAD_DIM, lane >= SB_HEAD_DIM)
        q = q_ref[0]
        dov = do_ref[0]
        ov = o_ref[0]
        qs = [jnp.where(m, q, jnp.zeros_like(q)) for m in head_masks]
        docs = [jnp.where(m, dov, 0.0).astype(CD) for m in head_masks]
        totals = [jnp.sum(d.astype(F32) * ov, axis=1, keepdims=True) for d in docs]
        tri = _tri(True)
        tri_incl = _tri(False)
        rr = lax.broadcasted_iota(jnp.int32, (t, t), 0)
        cc = lax.broadcasted_iota(jnp.int32, (t, t), 1)
        diag_valid = cc < rr

        def block(j, carry, valid):
            run, erun, dqacc = carry
            rows = pl.ds(pl.multiple_of(j * t, t), t)
            kb = k_ref[0, rows, :]
            vb = v_ref[0, rows, :]
            dkacc = jnp.zeros((t, LANES), F32)
            dvacc = jnp.zeros((t, LANES), F32)
            new_run, new_erun = [], []
            for hd in range(2):
                lb, l1m = _sb_logits(qs[hd], kb, valid)
                w = jnp.exp(lb + run[hd] + _split_dot(l1m, tri))
                if valid is not None:
                    w = jnp.where(valid, w, 0.0)
                wc = w.astype(CD)
                sig = jnp.exp(lb)
                dw = lax.dot_general(docs[hd], vb, NT_DIMS, preferred_element_type=F32)
                e = dw * wc.astype(F32)
                gsum = totals[hd] - erun[hd] - _split_dot(e, tri_incl)
                dz = (e * (1.0 - sig) - gsum * sig) * ATT_SCALE
                if valid is not None:
                    dz = jnp.where(valid, dz, 0.0)
                dzc = dz.astype(CD)
                kx = jnp.where(head_masks[hd], kb, jnp.zeros_like(kb))
                dqacc = dqacc + jnp.dot(dzc, kx, preferred_element_type=F32)
                dkacc = dkacc + lax.dot_general(dzc, qs[hd], TN_DIMS, preferred_element_type=F32)
                dvacc = dvacc + lax.dot_general(wc, docs[hd], TN_DIMS, preferred_element_type=F32)
                new_run.append(run[hd] + jnp.sum(l1m, axis=1, keepdims=True))
                new_erun.append(erun[hd] + jnp.sum(e, axis=1, keepdims=True))
            dk_ref[0, rows, :] += dkacc
            dv_ref[0, rows, :] += dvacc
            return tuple(new_run), tuple(new_erun), dqacc

        zero = jnp.zeros((t, 1), F32)
        carry = block(i, ((zero, zero), (zero, zero), jnp.zeros((t, LANES), F32)), diag_valid)
        carry = lax.fori_loop(0, i, lambda jj, cr: block(i - 1 - jj, cr, None), carry)
        dq_ref[0] = carry[2]

    qblk = pl.BlockSpec((1, t, LANES), lambda p, i: (p, i, 0))
    full = pl.BlockSpec((1, s, LANES), lambda p, i: (p, 0, 0))
    shape = jax.ShapeDtypeStruct((N_PAIRS, s, LANES), F32)
    return pl.pallas_call(
        body, name=name, out_shape=(shape, shape, shape), grid=(N_PAIRS, nblk),
        in_specs=[qblk,
                  pl.BlockSpec((1, s, LANES), lambda p, i: (N_PAIRS + p, 0, 0)),
                  pl.BlockSpec((1, s, LANES), lambda p, i: (2 * N_PAIRS + p, 0, 0)),
                  qblk, qblk],
        out_specs=(qblk, full, full),
        compiler_params=_params("parallel", "arbitrary"),
    )(qkv, qkv, qkv, o, do)


ADAM_COLS = 1024


def _adamw(w, g, m, v, name):
    shape = w.shape
    rows = w.size // ADAM_COLS
    tr = _pick(rows, (512, 256, 128, 64, 32, 16, 8))

    def body(w_ref, g_ref, m_ref, v_ref, d_ref, nm_ref, nv_ref):
        gv = g_ref[...]
        nm = ADAM_B1 * m_ref[...] + (1.0 - ADAM_B1) * gv
        nv = ADAM_B2 * v_ref[...] + (1.0 - ADAM_B2) * (gv * gv)
        m_hat = nm / (1.0 - ADAM_B1 ** ADAM_STEP)
        v_hat = nv / (1.0 - ADAM_B2 ** ADAM_STEP)
        d_ref[...] = -ADAM_LR * (m_hat / (jnp.sqrt(v_hat) + ADAM_EPS) + ADAM_WD * w_ref[...])
        nm_ref[...] = nm
        nv_ref[...] = nv

    blk = pl.BlockSpec((tr, ADAM_COLS), lambda i: (i, 0))
    out = jax.ShapeDtypeStruct((rows, ADAM_COLS), F32)
    d, nm, nv = pl.pallas_call(
        body, name=name, out_shape=(out, out, out), grid=(rows // tr,),
        in_specs=[blk, blk, blk, blk], out_specs=(blk, blk, blk), compiler_params=_params("parallel"),
    )(*[a.reshape(rows, ADAM_COLS) for a in (w, g, m, v)])
    return d.reshape(shape), nm.reshape(shape), nv.reshape(shape)


HBM = pl.BlockSpec(memory_space=pltpu.HBM)


def _coords():
    return lax.axis_index("x"), lax.axis_index("y"), lax.axis_index("c")


def _other_chips(x, y):
    return [(1 - x, y), (x, 1 - y), (1 - x, 1 - y)]


def _allgather_chips(shard, name):
    r, cols = shard.shape
    half = r // 2

    def body(src_ref, out_ref, send_sems, recv_sems, local_sem):
        x, y, c = _coords()
        sibling = (x, y, 1 - c)
        chips = _other_chips(x, y)

        def rows(px, py, h):
            return out_ref.at[2 * px + py, pl.ds(h * half, half), :]

        def copy(k, block, to, src=None):
            return pltpu.make_async_remote_copy(
                src_ref=rows(*block) if src is None else src, dst_ref=rows(*block),
                send_sem=send_sems.at[k], recv_sem=recv_sems.at[k], device_id=to, device_id_type=MESH)

        mine = pltpu.make_async_copy(src_ref, out_ref.at[2 * x + y], local_sem)
        mine.start()
        my_half = src_ref.at[pl.ds(c * half, half), :]
        first = [copy(j, (x, y, c), (*chip, c), src=my_half) for j, chip in enumerate(chips)]
        for cp in first:
            cp.start()
        passed = [copy(3 + j, (*chip, c), sibling) for j, chip in enumerate(chips)]
        for j, chip in enumerate(chips):
            copy(j, (*chip, c), (x, y, c)).wait_recv()
            passed[j].start()
        for j, chip in enumerate(chips):
            copy(3 + j, (*chip, 1 - c), (x, y, c)).wait_recv()
        for cp in first + passed:
            cp.wait_send()
        mine.wait()

    return pl.pallas_call(
        body, name=name, out_shape=jax.ShapeDtypeStruct((N_CHIPS, r, cols), shard.dtype),
        in_specs=[HBM], out_specs=HBM,
        scratch_shapes=[pltpu.SemaphoreType.DMA((6,)), pltpu.SemaphoreType.DMA((6,)), pltpu.SemaphoreType.DMA],
    )(shard)


def _exchange_sibling_halves(g, name):
    n, r, cols = g.shape
    half = r // 2

    def body(g_ref, out_ref, send_sem, recv_sem):
        x, y, c = _coords()
        cp = pltpu.make_async_remote_copy(
            src_ref=g_ref.at[:, pl.ds((1 - c) * half, half), :], dst_ref=out_ref,
            send_sem=send_sem, recv_sem=recv_sem, device_id=(x, y, 1 - c), device_id_type=MESH)
        cp.start()
        cp.wait()

    return pl.pallas_call(
        body, name=name, out_shape=jax.ShapeDtypeStruct((n, half, cols), g.dtype),
        in_specs=[HBM], out_specs=HBM,
        scratch_shapes=[pltpu.SemaphoreType.DMA, pltpu.SemaphoreType.DMA],
    )(g)


def _scatter_to_chips(p, name):
    n, h, cols = p.shape

    def body(p_ref, out_ref, send_sems, recv_sems, local_sem):
        x, y, c = _coords()
        me = 2 * x + y
        mine = pltpu.make_async_copy(p_ref.at[me], out_ref.at[me], local_sem)
        mine.start()
        sends = []
        for j, (px, py) in enumerate(_other_chips(x, y)):
            sends.append(pltpu.make_async_remote_copy(
                src_ref=p_ref.at[2 * px + py], dst_ref=out_ref.at[me],
                send_sem=send_sems.at[j], recv_sem=recv_sems.at[j], device_id=(px, py, c), device_id_type=MESH))
        for cp in sends:
            cp.start()
        for j, (px, py) in enumerate(_other_chips(x, y)):
            pltpu.make_async_remote_copy(
                src_ref=p_ref.at[me], dst_ref=out_ref.at[2 * px + py],
                send_sem=send_sems.at[j], recv_sem=recv_sems.at[j], device_id=(px, py, c),
                device_id_type=MESH).wait_recv()
        for cp in sends:
            cp.wait_send()
        mine.wait()

    return pl.pallas_call(
        body, name=name, out_shape=jax.ShapeDtypeStruct((n, h, cols), p.dtype),
        in_specs=[HBM], out_specs=HBM,
        scratch_shapes=[pltpu.SemaphoreType.DMA((3,)), pltpu.SemaphoreType.DMA((3,)), pltpu.SemaphoreType.DMA],
    )(p)


def _swap_with_sibling(v, name):
    def body(v_ref, out_ref, send_sem, recv_sem):
        x, y, c = _coords()
        cp = pltpu.make_async_remote_copy(
            src_ref=v_ref, dst_ref=out_ref, send_sem=send_sem, recv_sem=recv_sem,
            device_id=(x, y, 1 - c), device_id_type=MESH)
        cp.start()
        cp.wait()

    return pl.pallas_call(
        body, name=name, out_shape=jax.ShapeDtypeStruct(v.shape, v.dtype),
        in_specs=[HBM], out_specs=HBM,
        scratch_shapes=[pltpu.SemaphoreType.DMA, pltpu.SemaphoreType.DMA],
    )(v)


def _allreduce_small(v, name):
    r, cols = v.shape

    def body(v_ref, out_ref, buf_ref, send_sems, recv_sems):
        x, y, c = _coords()
        me = 4 * x + 2 * y + c
        buf_ref[me] = v_ref[...]
        sends = []
        for k in range(1, N_DEV):
            px = 1 - x if k & 4 else x
            py = 1 - y if k & 2 else y
            pc = 1 - c if k & 1 else c
            sends.append(pltpu.make_async_remote_copy(
                src_ref=v_ref, dst_ref=buf_ref.at[me], send_sem=send_sems.at[k - 1], recv_sem=recv_sems.at[k - 1],
                device_id=(px, py, pc), device_id_type=MESH))
        for cp in sends:
            cp.start()
        for cp in sends:
            cp.wait()
        acc = buf_ref[0]
        for d in range(1, N_DEV):
            acc = acc + buf_ref[d]
        out_ref[...] = acc

    return pl.pallas_call(
        body, name=name, out_shape=jax.ShapeDtypeStruct((r, cols), F32),
        in_specs=[pl.BlockSpec(memory_space=pltpu.VMEM)], out_specs=pl.BlockSpec(memory_space=pltpu.VMEM),
        scratch_shapes=[pltpu.VMEM((N_DEV, r, cols), F32), pltpu.SemaphoreType.DMA((N_DEV - 1,)),
                        pltpu.SemaphoreType.DMA((N_DEV - 1,))],
    )(v)


def _add2(a, b, name):
    n, r, cols = a.shape
    tr = _pick(r, (512, 256, 128))

    def body(a_ref, b_ref, o_ref):
        o_ref[...] = a_ref[...] + b_ref[...]

    blk = pl.BlockSpec((1, tr, cols), lambda s, i: (s, i, 0))
    return pl.pallas_call(
        body, name=name, out_shape=jax.ShapeDtypeStruct(a.shape, a.dtype), grid=(n, r // tr),
        in_specs=[blk, blk], out_specs=blk, compiler_params=_params("parallel", "parallel"),
    )(a, b)


def _sum_slots(p, name):
    n, r, cols = p.shape
    tr = _pick(r, (512, 256, 128))

    def body(p_ref, o_ref):
        o_ref[...] = ((p_ref[0] + p_ref[1]) + p_ref[2]) + p_ref[3]

    return pl.pallas_call(
        body, name=name, out_shape=jax.ShapeDtypeStruct((r, cols), p.dtype), grid=(r // tr,),
        in_specs=[pl.BlockSpec((n, tr, cols), lambda i: (0, i, 0))],
        out_specs=pl.BlockSpec((tr, cols), lambda i: (i, 0)), compiler_params=_params("parallel"),
    )(p)


PACK_COLS = 1024


def _pack_shards(parts):
    return jnp.concatenate([p.reshape(-1, PACK_COLS) for p in parts], axis=0)


def _unpack_shards(buf, shapes):
    out, row = [], 0
    for shp in shapes:
        nrows = math.prod(shp) // PACK_COLS
        out.append(buf[..., row:row + nrows, :].reshape(buf.shape[:-2] + tuple(shp)))
        row += nrows
    return out


def _local_step(x, target, w):
    t = lambda a: a.T
    g = {}
    h0 = _rms_fwd(x, w["norm_mix_g"][0], "rms_mix0")
    w_in_g, w_in_x = w["a_w_in"][:, :D_RNN], w["a_w_in"][:, D_RNN:]
    gate_br = _matmul([(h0, w_in_g)], F32, "mm_a_gate")
    x_br = _matmul([(h0, w_in_x)], F32, "mm_a_xbr")
    y_a, hs = _rglru_fwd(gate_br, x_br, w["a_conv_w"], w["a_conv_b"], w["a_w_r"], w["a_w_i"], w["a_b_r"],
                         w["a_b_i"], w["a_lambda"], "rglru_fwd")
    x1 = _matmul([(y_a, w["a_w_out"])], F32, "mm_a_out", addend=x)
    h1 = _rms_fwd(x1, w["norm_ffn_g"][0], "rms_ffn0")
    fg0 = _matmul([(h1, w["ffn_w_gate"][0])], F32, "mm_f0_gate")
    fu0 = _matmul([(h1, w["ffn_w_up"][0])], F32, "mm_f0_up")
    act0 = _swiglu_fwd(fg0, fu0, "swiglu0_fwd")
    x2 = _matmul([(act0, w["ffn_w_down"][0])], F32, "mm_f0_down", addend=x1)
    h2 = _rms_fwd(x2, w["norm_mix_g"][1], "rms_mix1")
    qkv = _matmul([(h2, w["b_w_qkv"])], CD, "mm_b_qkv", out_lbm=True, tn=1024)
    o = _attn_fwd(qkv, "attn_fwd")
    x3 = _matmul([(o, w["b_w_out"])], F32, "mm_b_out", a_lbm=True, addend=x2)
    h3 = _rms_fwd(x3, w["norm_ffn_g"][1], "rms_ffn1")
    fg1 = _matmul([(h3, w["ffn_w_gate"][1])], F32, "mm_f1_gate")
    fu1 = _matmul([(h3, w["ffn_w_up"][1])], F32, "mm_f1_up")
    act1 = _swiglu_fwd(fg1, fu1, "swiglu1_fwd")
    x4 = _matmul([(act1, w["ffn_w_down"][1])], F32, "mm_f1_down", addend=x3)
    loss, dx4, g["final_g"] = _loss_head(x4, w["final_g"], target, "loss_head")

    def ffn_bwd(dx_out, h, x_in, fg, fu, act, layer, tag):
        dxc = dx_out.astype(CD)
        dact = _matmul([(dxc, t(w["ffn_w_down"][layer]))], F32, "mm_" + tag + "_dact")
        dwd = _matmul([(act, dxc)], F32, "mm_" + tag + "_dwd", trans_a=True)
        dg, du = _swiglu_bwd(dact, fg, fu, "swiglu" + tag + "_bwd")
        dwg = _matmul([(h, dg)], F32, "mm_" + tag + "_dwg", trans_a=True)
        dwu = _matmul([(h, du)], F32, "mm_" + tag + "_dwu", trans_a=True)
        dh = _matmul([(dg, t(w["ffn_w_gate"][layer])), (du, t(w["ffn_w_up"][layer]))], F32, "mm_" + tag + "_dh")
        dx_in, dgain = _rms_bwd(dh, x_in, w["norm_ffn_g"][layer], dx_out, "rms_ffn" + tag + "_bwd")
        return dx_in, dgain, dwg, dwu, dwd

    dx3, dgf1, dwg1, dwu1, dwd1 = ffn_bwd(dx4, h3, x3, fg1, fu1, act1, 1, "f1")
    dx3c = dx3.astype(CD)
    do = _matmul([(dx3c, t(w["b_w_out"]))], F32, "mm_b_do", out_lbm=True, tn=1024)
    g["b_w_out"] = _matmul([(o, dx3c)], F32, "mm_b_dwout", trans_a=True, a_lbm=True)
    dq, dk, dv = _attn_bwd(qkv, o, do, "attn_bwd")
    wq_t = t(w["b_w_qkv"])
    parts = (dq, dk, dv)
    g["b_w_qkv"] = jnp.concatenate(
        [_matmul([(h2, p)], F32, "mm_b_dwqkv%d" % n, trans_a=True, b_lbm=True) for n, p in enumerate(parts)], axis=1)
    dh2 = _matmul([(p, wq_t[n * D_MODEL:(n + 1) * D_MODEL]) for n, p in enumerate(parts)], F32, "mm_b_dh",
                  a_lbm=True)
    dx2, dgm1 = _rms_bwd(dh2, x2, w["norm_mix_g"][1], dx3, "rms_mix1_bwd")
    dx1, dgf0, dwg0, dwu0, dwd0 = ffn_bwd(dx2, h1, x1, fg0, fu0, act0, 0, "f0")
    dx1c = dx1.astype(CD)
    dy_a = _matmul([(dx1c, t(w["a_w_out"]))], F32, "mm_a_dy")
    g["a_w_out"] = _matmul([(y_a, dx1c)], F32, "mm_a_dwout", trans_a=True)
    wrt = jnp.swapaxes(w["a_w_r"], 1, 2)
    wit = jnp.swapaxes(w["a_w_i"], 1, 2)
    (dgate, dxbr, g["a_conv_w"], g["a_conv_b"], g["a_b_r"], g["a_b_i"], g["a_lambda"], g["a_w_r"],
     g["a_w_i"]) = _rglru_bwd(dy_a, gate_br, x_br, hs, w["a_conv_w"], w["a_conv_b"], w["a_w_r"], w["a_w_i"], wrt, wit,
                              w["a_b_r"], w["a_b_i"], w["a_lambda"], "rglru_bwd")
    g["a_w_in"] = jnp.concatenate([_matmul([(h0, dgate)], F32, "mm_a_dwin_g", trans_a=True),
                                   _matmul([(h0, dxbr)], F32, "mm_a_dwin_x", trans_a=True)], axis=1)
    dh0 = _matmul([(dgate, t(w_in_g)), (dxbr, t(w_in_x))], F32, "mm_a_dh")
    dx0, dgm0 = _rms_bwd(dh0, x, w["norm_mix_g"][0], dx1, "rms_mix0_bwd")
    g["norm_mix_g"] = jnp.concatenate([dgm0, dgm1], axis=0)
    g["norm_ffn_g"] = jnp.concatenate([dgf0, dgf1], axis=0)
    g["ffn_w_gate"] = jnp.stack([dwg0, dwg1])
    g["ffn_w_up"] = jnp.stack([dwu0, dwu1])
    g["ffn_w_down"] = jnp.stack([dwd0, dwd1])
    return loss, dx0, g


WEIGHTS = ["norm_mix_g", "norm_ffn_g", "a_w_in", "a_conv_w", "a_conv_b", "a_w_r", "a_b_r", "a_w_i", "a_b_i",
           "a_lambda", "a_w_out", "b_w_qkv", "b_w_out", "ffn_w_gate", "ffn_w_up", "ffn_w_down", "final_g"]
BIG = [("a_w_in", 2), ("a_w_r", 2), ("a_w_i", 2), ("a_w_out", 1), ("b_w_qkv", 2), ("b_w_out", 1),
       ("ffn_w_gate", 2), ("ffn_w_up", 2), ("ffn_w_down", 1)]
SMALL = ["norm_mix_g", "norm_ffn_g", "a_conv_w", "a_conv_b", "a_b_r", "a_b_i", "a_lambda", "final_g"]


def _join_chips(stack, axis):
    return jnp.concatenate([stack[s] for s in range(N_CHIPS)], axis=axis)


def _split_chips(full, axis):
    return jnp.stack(jnp.split(full, N_CHIPS, axis=axis))


def _step(x, target, weights, moments_m, moments_v):
    chip = 2 * lax.axis_index("x") + lax.axis_index("y")
    core = lax.axis_index("c")
    shard_shapes = [weights[n].shape for n, _ in BIG]
    packed = _pack_shards([weights[n].astype(CD) for n, _ in BIG])
    gathered = _allgather_chips(packed, "allgather_weights")
    full = {}
    for (n, axis), stack in zip(BIG, _unpack_shards(gathered, shard_shapes)):
        joined = _join_chips(stack, axis)
        full[n] = joined[0] if joined.shape[0] == 1 else joined
    cw_rows = jnp.zeros((N_CHIPS, CONV_W, RG_BW), F32)
    cw_rows = lax.dynamic_update_slice(cw_rows, jnp.where(core == 0, weights["a_conv_w"], 0.0), (chip, 0, 0))
    cw_all = _allreduce_small(cw_rows.reshape(-1, LANES), "allgather_conv_w").reshape(N_CHIPS, CONV_W, RG_BW)
    full["a_conv_w"] = jnp.concatenate([cw_all[s] for s in range(N_CHIPS)], axis=1)
    for n in ("norm_mix_g", "norm_ffn_g", "final_g"):
        full[n] = weights[n]
    for n in ("a_conv_b", "a_b_r", "a_b_i", "a_lambda"):
        full[n] = weights[n]
    loss, dx, grads = _local_step(x[0], target[0], full)
    small_parts = [grads[n].reshape(-1) for n in SMALL] + [loss.reshape(-1)]
    sizes = [p.shape[0] for p in small_parts]
    small = _allreduce_small(jnp.concatenate(small_parts).reshape(-1, LANES), "allreduce_small").reshape(-1)
    red, pos = {}, 0
    for n, sz in zip(SMALL + ["loss"], sizes):
        red[n] = small[pos:pos + sz]
        pos += sz
    loss_out = red["loss"][0]
    g_out = {}
    for n in SMALL:
        if n == "a_conv_w":
            g_out[n] = lax.dynamic_slice(red[n].reshape(CONV_W, D_RNN), (0, chip * RG_BW), (CONV_W, RG_BW)).reshape(
                weights[n].shape)
        else:
            g_out[n] = red[n].reshape(weights[n].shape)
    stacks = []
    for n, axis in BIG:
        gfull = grads[n].reshape((1,) + grads[n].shape) if grads[n].ndim == len(weights[n].shape) - 1 else grads[n]
        stacks.append(_split_chips(gfull, axis).reshape(N_CHIPS, -1, PACK_COLS))
    gbuf = jnp.concatenate(stacks, axis=1)
    half = gbuf.shape[1] // 2
    from_sibling = _exchange_sibling_halves(gbuf, "rs_sibling")
    mine = lax.dynamic_slice_in_dim(gbuf, core * half, half, axis=1)
    chip_partial = _add2(mine, from_sibling, "rs_add_sibling")
    from_chips = _scatter_to_chips(chip_partial, "rs_chips")
    reduced_half = _sum_slots(from_chips, "rs_sum_chips")
    other_half = _swap_with_sibling(reduced_half, "rs_share")
    lo = jnp.where(core == 0, reduced_half, other_half)
    hi = jnp.where(core == 0, other_half, reduced_half)
    reduced = jnp.concatenate([lo, hi], axis=0)
    for (n, _), gsh in zip(BIG, _unpack_shards(reduced, shard_shapes)):
        g_out[n] = gsh
    outs_g, outs_d, outs_m, outs_v = [], [], [], []
    for n in WEIGHTS:
        d, nm, nv = _adamw(weights[n], g_out[n], moments_m[n], moments_v[n], "adamw_" + n)
        outs_g.append(g_out[n])
        outs_d.append(d)
        outs_m.append(nm)
        outs_v.append(nv)
    return (loss_out, dx[None], *outs_g, *outs_d, *outs_m, *outs_v)


def kernel(x, norm_mix_g, norm_ffn_g, a_w_in, a_conv_w, a_conv_b, a_w_r, a_b_r, a_w_i, a_b_i, a_lambda, a_w_out, b_w_qkv, b_w_out, ffn_w_gate, ffn_w_up, ffn_w_down, final_g, loss_target, m_norm_mix_g, m_norm_ffn_g, m_a_w_in, m_a_conv_w, m_a_conv_b, m_a_w_r, m_a_b_r, m_a_w_i, m_a_b_i, m_a_lambda, m_a_w_out, m_b_w_qkv, m_b_w_out, m_ffn_w_gate, m_ffn_w_up, m_ffn_w_down, m_final_g, v_norm_mix_g, v_norm_ffn_g, v_a_w_in, v_a_conv_w, v_a_conv_b, v_a_w_r, v_a_b_r, v_a_w_i, v_a_b_i, v_a_lambda, v_a_w_out, v_b_w_qkv, v_b_w_out, v_ffn_w_gate, v_ffn_w_up, v_ffn_w_down, v_final_g):
    ws = [norm_mix_g, norm_ffn_g, a_w_in, a_conv_w, a_conv_b, a_w_r, a_b_r, a_w_i, a_b_i, a_lambda, a_w_out, b_w_qkv,
          b_w_out, ffn_w_gate, ffn_w_up, ffn_w_down, final_g]
    ms = [m_norm_mix_g, m_norm_ffn_g, m_a_w_in, m_a_conv_w, m_a_conv_b, m_a_w_r, m_a_b_r, m_a_w_i, m_a_b_i, m_a_lambda,
          m_a_w_out, m_b_w_qkv, m_b_w_out, m_ffn_w_gate, m_ffn_w_up, m_ffn_w_down, m_final_g]
    vs = [v_norm_mix_g, v_norm_ffn_g, v_a_w_in, v_a_conv_w, v_a_conv_b, v_a_w_r, v_a_b_r, v_a_w_i, v_a_b_i, v_a_lambda,
          v_a_w_out, v_b_w_qkv, v_b_w_out, v_ffn_w_gate, v_ffn_w_up, v_ffn_w_down, v_final_g]
    return _step(x, loss_target, dict(zip(WEIGHTS, ws)), dict(zip(WEIGHTS, ms)), dict(zip(WEIGHTS, vs)))
```

```python
import functools
import math

import jax
import jax.numpy as jnp
from jax import lax
from jax.experimental import pallas as pl
from jax.experimental.pallas import tpu as pltpu

F32 = jnp.float32
CD = jnp.bfloat16

D_MODEL = 1024
D_RNN = 1024
RG_BLOCKS = 4
RG_BW = 256
CONV_W = 4
RG_C = 8.0
SB_HEADS = 16
SB_HEAD_DIM = 64
D_FF = 2816
RMS_EPS = 1e-6
N_CHIPS = 4
N_DEV = 8

ADAM_LR = 0.001
ADAM_B1 = 0.9
ADAM_B2 = 0.999
ADAM_EPS = 1e-08
ADAM_WD = 0.01
ADAM_STEP = 10

LANES = 128
VMEM_LIMIT = 56 * 1024 * 1024
MESH = pl.DeviceIdType.MESH


def _params(*sem):
    return pltpu.CompilerParams(dimension_semantics=sem, vmem_limit_bytes=VMEM_LIMIT)


def _pick(n, prefs):
    for p in prefs:
        if n % p == 0:
            return p
    return n


def _matmul(pairs, out_dtype, name, *, trans_a=False, a_lbm=False, b_lbm=False, out_lbm=False, addend=None,
            tm=512, tn=None, tk=None):
    a0, b0 = pairs[0]
    if trans_a:
        kdim = a0.shape[1] if a_lbm else a0.shape[0]
        m = a0.shape[0] * LANES if a_lbm else a0.shape[1]
    else:
        m = a0.shape[1] if a_lbm else a0.shape[0]
        kdim = a0.shape[0] * LANES if a_lbm else a0.shape[1]
    n = b0.shape[0] * LANES if b_lbm else b0.shape[1]
    tm = _pick(m, (tm, 1408, 256, 128))
    tn = tn or _pick(n, (1408, 1024, 768, 512, 256, 128))
    tk = tk or _pick(kdim, (1024, 1408, 512, 256, 128))
    nk = kdim // tk
    npair = len(pairs)

    def cat(ref):
        return jnp.concatenate([ref[p] for p in range(ref.shape[0])], axis=-1)

    def body(*refs):
        ins = refs[: 2 * npair]
        pos = 2 * npair
        add_ref = None
        if addend is not None:
            add_ref = refs[pos]
            pos += 1
        o_ref = refs[pos]
        acc_ref = refs[pos + 1]
        k = pl.program_id(2)

        @pl.when(k == 0)
        def _():
            acc_ref[...] = jnp.zeros_like(acc_ref)

        acc = acc_ref[...]
        for p in range(npair):
            a = (cat(ins[2 * p]) if a_lbm else ins[2 * p][...]).astype(CD)
            b = (cat(ins[2 * p + 1]) if b_lbm else ins[2 * p + 1][...]).astype(CD)
            dims = (((0,), (0,)), ((), ())) if trans_a else (((1,), (0,)), ((), ()))
            acc = acc + lax.dot_general(a, b, dims, preferred_element_type=F32)
        acc_ref[...] = acc

        @pl.when(k == nk - 1)
        def _():
            res = acc_ref[...]
            if add_ref is not None:
                res = res + add_ref[...]
            res = res.astype(out_dtype)
            if out_lbm:
                for p in range(tn // LANES):
                    o_ref[p] = res[:, p * LANES:(p + 1) * LANES]
            else:
                o_ref[...] = res

    if trans_a:
        a_spec = (pl.BlockSpec((tm // LANES, tk, LANES), lambda i, j, k: (i, k, 0)) if a_lbm
                  else pl.BlockSpec((tk, tm), lambda i, j, k: (k, i)))
    else:
        a_spec = (pl.BlockSpec((tk // LANES, tm, LANES), lambda i, j, k: (k, i, 0)) if a_lbm
                  else pl.BlockSpec((tm, tk), lambda i, j, k: (i, k)))
    b_spec = (pl.BlockSpec((tn // LANES, tk, LANES), lambda i, j, k: (j, k, 0)) if b_lbm
              else pl.BlockSpec((tk, tn), lambda i, j, k: (k, j)))
    in_specs = []
    args = []
    for a, b in pairs:
        in_specs += [a_spec, b_spec]
        args += [a, b]
    if addend is not None:
        in_specs.append(pl.BlockSpec((tm, tn), lambda i, j, k: (i, j)))
        args.append(addend)
    if out_lbm:
        out_shape = jax.ShapeDtypeStruct((n // LANES, m, LANES), out_dtype)
        out_spec = pl.BlockSpec((tn // LANES, tm, LANES), lambda i, j, k: (j, i, 0))
    else:
        out_shape = jax.ShapeDtypeStruct((m, n), out_dtype)
        out_spec = pl.BlockSpec((tm, tn), lambda i, j, k: (i, j))
    return pl.pallas_call(
        body, name=name, out_shape=out_shape, grid=(m // tm, n // tn, nk),
        in_specs=in_specs, out_specs=out_spec,
        scratch_shapes=[pltpu.VMEM((tm, tn), F32)],
        compiler_params=_params("parallel", "parallel", "arbitrary"),
    )(*args)


ROW_BLOCK = 256


def _rms_fwd(x, g, name):
    s, d = x.shape

    def body(x_ref, g_ref, h_ref):
        xv = x_ref[...]
        rinv = lax.rsqrt(jnp.mean(xv * xv, axis=-1, keepdims=True) + RMS_EPS)
        h_ref[...] = (xv * rinv * g_ref[...]).astype(CD)

    return pl.pallas_call(
        body, name=name, out_shape=jax.ShapeDtypeStruct((s, d), CD), grid=(s // ROW_BLOCK,),
        in_specs=[pl.BlockSpec((ROW_BLOCK, d), lambda i: (i, 0)), pl.BlockSpec((1, d), lambda i: (0, 0))],
        out_specs=pl.BlockSpec((ROW_BLOCK, d), lambda i: (i, 0)),
        compiler_params=_params("parallel"),
    )(x, g.reshape(1, d))


def _rms_bwd(dh, x, g, dx_in, name):
    s, d = x.shape

    def body(dh_ref, x_ref, g_ref, dxin_ref, dx_ref, dg_ref):
        @pl.when(pl.program_id(0) == 0)
        def _():
            dg_ref[...] = jnp.zeros_like(dg_ref)

        xv = x_ref[...]
        dhv = dh_ref[...]
        rinv = lax.rsqrt(jnp.mean(xv * xv, axis=-1, keepdims=True) + RMS_EPS)
        nrm = xv * rinv
        dn = dhv * g_ref[...]
        dx_ref[...] = dxin_ref[...] + rinv * (dn - nrm * jnp.mean(dn * nrm, axis=-1, keepdims=True))
        dg_ref[...] += jnp.sum(dhv * nrm, axis=0, keepdims=True)

    row = pl.BlockSpec((ROW_BLOCK, d), lambda i: (i, 0))
    vec = pl.BlockSpec((1, d), lambda i: (0, 0))
    return pl.pallas_call(
        body, name=name,
        out_shape=(jax.ShapeDtypeStruct((s, d), F32), jax.ShapeDtypeStruct((1, d), F32)),
        grid=(s // ROW_BLOCK,), in_specs=[row, row, vec, row], out_specs=(row, vec),
        compiler_params=_params("arbitrary"),
    )(dh, x, g.reshape(1, d), dx_in)


def _loss_head(x, g, target, name):
    s, d = x.shape

    def body(x_ref, g_ref, t_ref, loss_ref, dx_ref, dg_ref):
        @pl.when(pl.program_id(0) == 0)
        def _():
            dg_ref[...] = jnp.zeros_like(dg_ref)
            loss_ref[...] = jnp.zeros_like(loss_ref)

        xv = x_ref[...]
        gv = g_ref[...]
        rinv = lax.rsqrt(jnp.mean(xv * xv, axis=-1, keepdims=True) + RMS_EPS)
        nrm = xv * rinv
        err = nrm * gv - t_ref[...]
        loss_ref[...] += 0.5 * jnp.sum(jnp.mean(err * err, axis=-1, keepdims=True), axis=0, keepdims=True)
        dy = err * (1.0 / d)
        dn = dy * gv
        dx_ref[...] = rinv * (dn - nrm * jnp.mean(dn * nrm, axis=-1, keepdims=True))
        dg_ref[...] += jnp.sum(dy * nrm, axis=0, keepdims=True)

    row = pl.BlockSpec((ROW_BLOCK, d), lambda i: (i, 0))
    vec = pl.BlockSpec((1, d), lambda i: (0, 0))
    return pl.pallas_call(
        body, name=name,
        out_shape=(jax.ShapeDtypeStruct((1, LANES), F32), jax.ShapeDtypeStruct((s, d), F32),
                   jax.ShapeDtypeStruct((1, d), F32)),
        grid=(s // ROW_BLOCK,), in_specs=[row, vec, row],
        out_specs=(pl.BlockSpec((1, LANES), lambda i: (0, 0)), row, vec),
        compiler_params=_params("arbitrary"),
    )(x, g.reshape(1, d), target)


def _sigmoid(z):
    return 1.0 / (1.0 + jnp.exp(-z))


def _swiglu_fwd(g, u, name):
    s, f = g.shape

    def body(g_ref, u_ref, a_ref):
        gv = g_ref[...]
        a_ref[...] = (gv * _sigmoid(gv) * u_ref[...]).astype(CD)

    row = pl.BlockSpec((ROW_BLOCK, f), lambda i: (i, 0))
    return pl.pallas_call(
        body, name=name, out_shape=jax.ShapeDtypeStruct((s, f), CD), grid=(s // ROW_BLOCK,),
        in_specs=[row, row], out_specs=row, compiler_params=_params("parallel"),
    )(g, u)


def _swiglu_bwd(dact, g, u, name):
    s, f = g.shape

    def body(da_ref, g_ref, u_ref, dg_ref, du_ref):
        gv = g_ref[...]
        da = da_ref[...]
        sg = _sigmoid(gv)
        silu = gv * sg
        dg_ref[...] = (da * u_ref[...] * (sg + silu * (1.0 - sg))).astype(CD)
        du_ref[...] = (da * silu).astype(CD)

    row = pl.BlockSpec((ROW_BLOCK, f), lambda i: (i, 0))
    return pl.pallas_call(
        body, name=name,
        out_shape=(jax.ShapeDtypeStruct((s, f), CD), jax.ShapeDtypeStruct((s, f), CD)),
        grid=(s // ROW_BLOCK,), in_specs=[row, row, row], out_specs=(row, row),
        compiler_params=_params("parallel"),
    )(dact, g, u)


TIME_BLOCK = 256
SUBLANES = 8
GELU_C = math.sqrt(2.0 / math.pi)
GELU_A = 0.044715


def _gelu(x):
    return 0.5 * x * (1.0 + jnp.tanh(GELU_C * (x + GELU_A * x * x * x)))


def _gelu_grad(x):
    t = jnp.tanh(GELU_C * (x + GELU_A * x * x * x))
    return 0.5 * (1.0 + t) + 0.5 * x * (1.0 - t * t) * GELU_C * (1.0 + 3.0 * GELU_A * x * x)


def _neg_expm1(x):
    series = -x * (1.0 + x * (0.5 + x * (1.0 / 6.0 + x * (1.0 / 24.0))))
    return jnp.where(x > -0.05, series, 1.0 - jnp.exp(x))


def _log_sigmoid(x):
    return jnp.minimum(x, 0.0) - jnp.log1p(jnp.exp(-jnp.abs(x)))


def _shift_down(x, tail, s):
    if s == 0:
        return x
    ext = jnp.concatenate([tail, x], axis=0)
    return pltpu.roll(ext, s, axis=0)[SUBLANES:]


def _shift_up(x, head, s):
    if s == 0:
        return x
    n = x.shape[0]
    ext = jnp.concatenate([x, head], axis=0)
    return pltpu.roll(ext, n + SUBLANES - s, axis=0)[:n]


def _rg_gates(xbr, tail, cw_ref, cb, wr, wi, br, bi, ls):
    taps = [_shift_down(xbr, tail, CONV_W - 1 - k) for k in range(CONV_W)]
    xc = cb
    for k in range(CONV_W):
        xc = xc + cw_ref[pl.ds(k, 1), :] * taps[k]
    xcd = xc.astype(CD)
    r = _sigmoid(jnp.dot(xcd, wr, preferred_element_type=F32) + br)
    i = _sigmoid(jnp.dot(xcd, wi, preferred_element_type=F32) + bi)
    log_a = RG_C * r * ls
    a = jnp.exp(log_a)
    mult = jnp.sqrt(jnp.maximum(_neg_expm1(2.0 * log_a), 0.0))
    return taps, xc, r, i, log_a, a, mult


def _scan8_fwd(a, u):
    row = lax.broadcasted_iota(jnp.int32, a.shape, 0)
    for d in (1, 2, 4):
        a_s = pltpu.roll(a, d, axis=0)
        u_s = pltpu.roll(u, d, axis=0)
        m = row >= d
        u = jnp.where(m, a * u_s + u, u)
        a = jnp.where(m, a * a_s, a)
    return a, u


def _scan8_bwd(b, u):
    row = lax.broadcasted_iota(jnp.int32, b.shape, 0)
    for d in (1, 2, 4):
        b_s = pltpu.roll(b, SUBLANES - d, axis=0)
        u_s = pltpu.roll(u, SUBLANES - d, axis=0)
        m = row < SUBLANES - d
        u = jnp.where(m, b * u_s + u, u)
        b = jnp.where(m, b * b_s, b)
    return b, u


def _rglru_fwd(gate_br, x_br, cw, cb, wr, wi, br, bi, lam, name):
    s, c = x_br.shape
    nt = s // TIME_BLOCK
    tb, cbw = TIME_BLOCK, RG_BW
    groups = tb // SUBLANES

    def body(g_ref, x_ref, tail_ref, cw_ref, cb_ref, wr_ref, wi_ref, br_ref, bi_ref, lam_ref,
             y_ref, hs_ref, carry_ref, a_scr, u_scr):
        t = pl.program_id(1)

        @pl.when(t == 0)
        def _():
            carry_ref[...] = jnp.zeros_like(carry_ref)

        tail = jnp.where(t > 0, tail_ref[...], 0.0)
        ls = _log_sigmoid(lam_ref[...])
        _, xc, _, i, _, a, mult = _rg_gates(x_ref[...], tail, cw_ref, cb_ref[...], wr_ref[0], wi_ref[0],
                                            br_ref[...], bi_ref[...], ls)
        a_scr[...] = a
        u_scr[...] = mult * (i * xc)
        carry = carry_ref[...]
        for gi in range(groups):
            rows = pl.ds(gi * SUBLANES, SUBLANES)
            pa, hl = _scan8_fwd(a_scr[rows, :], u_scr[rows, :])
            hs_ref[rows, :] = hl + pa * carry
            carry = hs_ref[pl.ds(gi * SUBLANES + SUBLANES - 1, 1), :]
        carry_ref[...] = carry
        y_ref[...] = (hs_ref[...] * _gelu(g_ref[...])).astype(CD)

    blk = pl.BlockSpec((tb, cbw), lambda n, t: (t, n))
    tail = pl.BlockSpec((SUBLANES, cbw), lambda n, t: (jnp.maximum(t * groups - 1, 0), n))
    vec = pl.BlockSpec((1, cbw), lambda n, t: (0, n))
    wblk = pl.BlockSpec((1, cbw, cbw), lambda n, t: (n, 0, 0))
    return pl.pallas_call(
        body, name=name,
        out_shape=(jax.ShapeDtypeStruct((s, c), CD), jax.ShapeDtypeStruct((s, c), F32)),
        grid=(RG_BLOCKS, nt),
        in_specs=[blk, blk, tail, pl.BlockSpec((CONV_W, cbw), lambda n, t: (0, n)), vec, wblk, wblk, vec, vec, vec],
        out_specs=(blk, blk),
        scratch_shapes=[pltpu.VMEM((1, cbw), F32), pltpu.VMEM((tb, cbw), F32), pltpu.VMEM((tb, cbw), F32)],
        compiler_params=_params("parallel", "arbitrary"),
    )(gate_br, x_br, x_br, cw, cb, wr, wi, br, bi, lam)


def _rglru_bwd(dy, gate_br, x_br, hs, cw, cb, wr, wi, wrt, wit, br, bi, lam, name):
    s, c = x_br.shape
    nt = s // TIME_BLOCK
    tb, cbw = TIME_BLOCK, RG_BW
    groups = tb // SUBLANES

    def body(dy_ref, g_ref, x_ref, tail_ref, hs_ref, hprev_ref, cw_ref, cb_ref, wr_ref, wi_ref, wrt_ref, wit_ref,
             br_ref, bi_ref, lam_ref,
             dg_ref, dx_ref, dcw_ref, dcb_ref, dbr_ref, dbi_ref, dlam_ref, dwr_ref, dwi_ref,
             carry_ref, head_ref, b_scr, u_scr, dh_scr):
        tr = pl.program_id(1)
        first_block = tr == nt - 1

        @pl.when(tr == 0)
        def _():
            carry_ref[...] = jnp.zeros_like(carry_ref)
            head_ref[...] = jnp.zeros_like(head_ref)
            for ref in (dcw_ref, dcb_ref, dbr_ref, dbi_ref, dlam_ref, dwr_ref, dwi_ref):
                ref[...] = jnp.zeros_like(ref)

        tail = jnp.where(first_block, 0.0, tail_ref[...])
        lam_v = lam_ref[...]
        ls = _log_sigmoid(lam_v)
        taps, xc, r, i, log_a, a, mult = _rg_gates(x_ref[...], tail, cw_ref, cb_ref[...], wr_ref[0], wi_ref[0],
                                                   br_ref[...], bi_ref[...], ls)
        gate_v = g_ref[...]
        dyv = dy_ref[...]
        hsv = hs_ref[...]
        dg_ref[...] = (dyv * hsv * _gelu_grad(gate_v)).astype(CD)

        row = lax.broadcasted_iota(jnp.int32, a.shape, 0)
        b_scr[...] = jnp.where(row == tb - 1, 1.0, pltpu.roll(a, tb - 1, axis=0))
        u_scr[...] = dyv * _gelu(gate_v)
        carry = carry_ref[...]
        for gi in reversed(range(groups)):
            rows = pl.ds(gi * SUBLANES, SUBLANES)
            pb, gl = _scan8_bwd(b_scr[rows, :], u_scr[rows, :])
            dh_scr[rows, :] = gl + pb * carry
            carry = dh_scr[pl.ds(gi * SUBLANES, 1), :]
        dh = dh_scr[...]
        carry_ref[...] = carry * jnp.sum(jnp.where(row == 0, a, 0.0), axis=0, keepdims=True)

        hprev_tail = jnp.where(first_block, 0.0, hprev_ref[...])
        h_prev = _shift_down(hsv, hprev_tail, 1)
        da = dh * h_prev
        ixc = i * xc
        dmult = dh * ixc
        di = dh * mult * xc
        dxc = dh * mult * i
        a2 = a * a
        dlog_a = da * a - dmult * a2 / mult
        dpre_r = (dlog_a * (RG_C * ls)) * r * (1.0 - r)
        dpre_i = di * i * (1.0 - i)
        dlam_ref[...] += jnp.sum(dlog_a * r, axis=0, keepdims=True) * (RG_C * _sigmoid(-lam_v))
        dbr_ref[...] += jnp.sum(dpre_r, axis=0, keepdims=True)
        dbi_ref[...] += jnp.sum(dpre_i, axis=0, keepdims=True)
        xcd = xc.astype(CD)
        dprc = dpre_r.astype(CD)
        dpic = dpre_i.astype(CD)
        tn_dims = (((0,), (0,)), ((), ()))
        dwr_ref[0] += lax.dot_general(xcd, dprc, tn_dims, preferred_element_type=F32)
        dwi_ref[0] += lax.dot_general(xcd, dpic, tn_dims, preferred_element_type=F32)
        dxc = dxc + jnp.dot(dprc, wrt_ref[0], preferred_element_type=F32) + jnp.dot(dpic, wit_ref[0],
                                                                                    preferred_element_type=F32)
        dcb_ref[...] += jnp.sum(dxc, axis=0, keepdims=True)
        for k in range(CONV_W):
            dcw_ref[pl.ds(k, 1), :] += jnp.sum(dxc * taps[k], axis=0, keepdims=True)
        head = head_ref[...]
        dxb = jnp.zeros_like(dxc)
        for sft in range(CONV_W):
            dxb = dxb + cw_ref[pl.ds(CONV_W - 1 - sft, 1), :] * _shift_up(dxc, head, sft)
        dx_ref[...] = dxb.astype(CD)
        head_ref[...] = dxc[0:SUBLANES, :]

    blk = pl.BlockSpec((tb, cbw), lambda n, t: (nt - 1 - t, n))
    tail = pl.BlockSpec((SUBLANES, cbw), lambda n, t: (jnp.maximum((nt - 1 - t) * groups - 1, 0), n))
    vec = pl.BlockSpec((1, cbw), lambda n, t: (0, n))
    cwb = pl.BlockSpec((CONV_W, cbw), lambda n, t: (0, n))
    wblk = pl.BlockSpec((1, cbw, cbw), lambda n, t: (n, 0, 0))
    vshape = jax.ShapeDtypeStruct((1, c), F32)
    wshape = jax.ShapeDtypeStruct((RG_BLOCKS, cbw, cbw), F32)
    return pl.pallas_call(
        body, name=name,
        out_shape=(jax.ShapeDtypeStruct((s, c), CD), jax.ShapeDtypeStruct((s, c), CD),
                   jax.ShapeDtypeStruct((CONV_W, c), F32), vshape, vshape, vshape, vshape, wshape, wshape),
        grid=(RG_BLOCKS, nt),
        in_specs=[blk, blk, blk, tail, blk, tail, cwb, vec, wblk, wblk, wblk, wblk, vec, vec, vec],
        out_specs=(blk, blk, cwb, vec, vec, vec, vec, wblk, wblk),
        scratch_shapes=[pltpu.VMEM((1, cbw), F32), pltpu.VMEM((SUBLANES, cbw), F32),
                        pltpu.VMEM((tb, cbw), F32), pltpu.VMEM((tb, cbw), F32), pltpu.VMEM((tb, cbw), F32)],
        compiler_params=_params("parallel", "arbitrary"),
    )(dy, gate_br, x_br, x_br, hs, hs, cw, cb, wr, wi, wrt, wit, br, bi, lam)


ATT_BLOCK = 256
ATT_SCALE = 1.0 / math.sqrt(SB_HEAD_DIM)
N_PAIRS = SB_HEADS * SB_HEAD_DIM // LANES
NT_DIMS = (((1,), (1,)), ((), ()))
TN_DIMS = (((0,), (0,)), ((), ()))


LOG2E = 1.4426950408889634


def _neg_abs(x):
    bits = lax.bitcast_convert_type(x, jnp.uint32) | jnp.uint32(0x80000000)
    return lax.bitcast_convert_type(bits, F32)


def _sb_logits(qx, kb, valid):
    z2 = lax.dot_general(qx, kb, NT_DIMS, preferred_element_type=F32) * (ATT_SCALE * LOG2E)
    lb2 = jnp.minimum(z2, 0.0) - jnp.log2(1.0 + jnp.exp2(_neg_abs(z2)))
    l2 = lb2 - z2
    if valid is not None:
        l2 = jnp.where(valid, l2, 0.0)
    return lb2, l2


def _hi_lo(x):
    hi = x.astype(CD)
    lo = (x - hi.astype(F32)).astype(CD)
    return jnp.concatenate([hi, lo], axis=1)


def _tri2(strict):
    r = lax.broadcasted_iota(jnp.int32, (ATT_BLOCK, ATT_BLOCK), 0)
    c = lax.broadcasted_iota(jnp.int32, (ATT_BLOCK, ATT_BLOCK), 1)
    m = (r > c if strict else r >= c).astype(CD)
    return jnp.concatenate([m, m], axis=0)


def _attn_fwd(qkv, name):
    _, s, _ = qkv.shape
    nblk = s // ATT_BLOCK
    t = ATT_BLOCK

    def body(q_ref, k_ref, v_ref, o_ref):
        i = pl.program_id(1)
        lane = lax.broadcasted_iota(jnp.int32, (1, LANES), 1)
        head_masks = (lane < SB_HEAD_DIM, lane >= SB_HEAD_DIM)
        q = q_ref[0]
        qs = [jnp.where(m, q, jnp.zeros_like(q)) for m in head_masks]
        tri = _tri2(True)
        rr = lax.broadcasted_iota(jnp.int32, (t, t), 0)
        cc = lax.broadcasted_iota(jnp.int32, (t, t), 1)
        diag_valid = cc < rr

        def block(j, carry, valid):
            run, oacc = carry
            rows = pl.ds(pl.multiple_of(j * t, t), t)
            kb = k_ref[0, rows, :]
            vb = v_ref[0, rows, :]
            lbs, hls, new_run = [], [], []
            for hd in range(2):
                lb2, l2 = _sb_logits(qs[hd], kb, valid)
                lbs.append(lb2)
                hls.append(_hi_lo(l2))
                new_run.append(run[hd] + jnp.sum(l2, axis=1, keepdims=True))
            ws = []
            for hd in range(2):
                w = jnp.exp2(lbs[hd] + (run[hd] + jnp.dot(hls[hd], tri, preferred_element_type=F32)))
                if valid is not None:
                    w = jnp.where(valid, w, 0.0)
                ws.append(w.astype(CD))
            vcat = jnp.concatenate([jnp.where(m, vb, jnp.zeros_like(vb)) for m in head_masks], axis=0)
            oacc = oacc + jnp.dot(jnp.concatenate(ws, axis=1), vcat, preferred_element_type=F32)
            return tuple(new_run), oacc

        zero = jnp.zeros((t, 1), F32)
        carry = block(i, ((zero, zero), jnp.zeros((t, LANES), F32)), diag_valid)
        carry = lax.fori_loop(0, i, lambda jj, cr: block(i - 1 - jj, cr, None), carry)
        o_ref[0] = carry[1]

    return pl.pallas_call(
        body, name=name, out_shape=jax.ShapeDtypeStruct((N_PAIRS, s, LANES), F32), grid=(N_PAIRS, nblk),
        in_specs=[pl.BlockSpec((1, t, LANES), lambda p, i: (p, i, 0)),
                  pl.BlockSpec((1, s, LANES), lambda p, i: (N_PAIRS + p, 0, 0)),
                  pl.BlockSpec((1, s, LANES), lambda p, i: (2 * N_PAIRS + p, 0, 0))],
        out_specs=pl.BlockSpec((1, t, LANES), lambda p, i: (p, i, 0)),
        compiler_params=_params("parallel", "arbitrary"),
    )(qkv, qkv, qkv)


def _attn_bwd(qkv, o, do, name):
    _, s, _ = qkv.shape
    nblk = s // ATT_BLOCK
    t = ATT_BLOCK

    def body(q_ref, k_ref, v_ref, o_ref, do_ref, dq_ref, dk_ref, dv_ref):
        i = pl.program_id(1)

        @pl.when(i == 0)
        def _():
            dk_ref[...] = jnp.zeros_like(dk_ref)
            dv_ref[...] = jnp.zeros_like(dv_ref)

        lane = lax.broadcasted_iota(jnp.int32, (1, LANES), 1)
        head_masks = (lane < SB_HEAD_DIM, lane >= SB_HEAD_DIM)
        q = q_ref[0]
        dov = do_ref[0]
        ov = o_ref[0]
        qs = [jnp.where(m, q, jnp.zeros_like(q)) for m in head_masks]
        q_scaled = jnp.concatenate([qx * ATT_SCALE for qx in qs], axis=0)
        docs = [jnp.where(m, dov, 0.0).astype(CD) for m in head_masks]
        docat = jnp.concatenate(docs, axis=0)
        totals = [jnp.sum(d.astype(F32) * ov, axis=1, keepdims=True) for d in docs]
        tri = _tri2(True)
        tri_incl = _tri2(False)
        rr = lax.broadcasted_iota(jnp.int32, (t, t), 0)
        cc = lax.broadcasted_iota(jnp.int32, (t, t), 1)
        diag_valid = cc < rr

        def block(j, carry, valid):
            run, erun, dqacc = carry
            rows = pl.ds(pl.multiple_of(j * t, t), t)
            kb = k_ref[0, rows, :]
            vb = v_ref[0, rows, :]
            lbs, hls, dws, new_run = [], [], [], []
            for hd in range(2):
                lb2, l2 = _sb_logits(qs[hd], kb, valid)
                lbs.append(lb2)
                hls.append(_hi_lo(l2))
                new_run.append(run[hd] + jnp.sum(l2, axis=1, keepdims=True))
                dws.append(lax.dot_general(docs[hd], vb, NT_DIMS, preferred_element_type=F32))
            wcs, es, ehls, new_erun = [], [], [], []
            for hd in range(2):
                w = jnp.exp2(lbs[hd] + (run[hd] + jnp.dot(hls[hd], tri, preferred_element_type=F32)))
                if valid is not None:
                    w = jnp.where(valid, w, 0.0)
                wc = w.astype(CD)
                e = dws[hd] * wc.astype(F32)
                wcs.append(wc)
                es.append(e)
                ehls.append(_hi_lo(e))
                new_erun.append(erun[hd] + jnp.sum(e, axis=1, keepdims=True))
            dzcs = []
            for hd in range(2):
                prefix = (totals[hd] - erun[hd]) - jnp.dot(ehls[hd], tri_incl, preferred_element_type=F32)
                dz = es[hd] - jnp.exp2(lbs[hd]) * (es[hd] + prefix)
                if valid is not None:
                    dz = jnp.where(valid, dz, 0.0)
                dzcs.append(dz.astype(CD))
            kcat = jnp.concatenate([jnp.where(m, kb, jnp.zeros_like(kb)) for m in head_masks], axis=0)
            dqacc = dqacc + jnp.dot(jnp.concatenate(dzcs, axis=1), kcat, preferred_element_type=F32)
            dk_ref[0, rows, :] += lax.dot_general(jnp.concatenate(dzcs, axis=0), q_scaled, TN_DIMS,
                                                  preferred_element_type=F32)
            dv_ref[0, rows, :] += lax.dot_general(jnp.concatenate(wcs, axis=0), docat, TN_DIMS,
                                                  preferred_element_type=F32)
            return tuple(new_run), tuple(new_erun), dqacc

        zero = jnp.zeros((t, 1), F32)
        carry = block(i, ((zero, zero), (zero, zero), jnp.zeros((t, LANES), F32)), diag_valid)
        carry = lax.fori_loop(0, i, lambda jj, cr: block(i - 1 - jj, cr, None), carry)
        dq_ref[0] = carry[2] * ATT_SCALE

    qblk = pl.BlockSpec((1, t, LANES), lambda p, i: (p, i, 0))
    full = pl.BlockSpec((1, s, LANES), lambda p, i: (p, 0, 0))
    shape = jax.ShapeDtypeStruct((N_PAIRS, s, LANES), F32)
    return pl.pallas_call(
        body, name=name, out_shape=(shape, shape, shape), grid=(N_PAIRS, nblk),
        in_specs=[qblk,
                  pl.BlockSpec((1, s, LANES), lambda p, i: (N_PAIRS + p, 0, 0)),
                  pl.BlockSpec((1, s, LANES), lambda p, i: (2 * N_PAIRS + p, 0, 0)),
                  qblk, qblk],
        out_specs=(qblk, full, full),
        compiler_params=_params("parallel", "arbitrary"),
    )(qkv, qkv, qkv, o, do)


ADAM_COLS = 1024


def _adamw(w, g, m, v, name):
    shape = w.shape
    rows = w.size // ADAM_COLS
    tr = _pick(rows, (512, 256, 128, 64, 32, 16, 8))

    def body(w_ref, g_ref, m_ref, v_ref, d_ref, nm_ref, nv_ref):
        gv = g_ref[...]
        nm = ADAM_B1 * m_ref[...] + (1.0 - ADAM_B1) * gv
        nv = ADAM_B2 * v_ref[...] + (1.0 - ADAM_B2) * (gv * gv)
        m_hat = nm / (1.0 - ADAM_B1 ** ADAM_STEP)
        v_hat = nv / (1.0 - ADAM_B2 ** ADAM_STEP)
        d_ref[...] = -ADAM_LR * (m_hat / (jnp.sqrt(v_hat) + ADAM_EPS) + ADAM_WD * w_ref[...])
        nm_ref[...] = nm
        nv_ref[...] = nv

    blk = pl.BlockSpec((tr, ADAM_COLS), lambda i: (i, 0))
    out = jax.ShapeDtypeStruct((rows, ADAM_COLS), F32)
    d, nm, nv = pl.pallas_call(
        body, name=name, out_shape=(out, out, out), grid=(rows // tr,),
        in_specs=[blk, blk, blk, blk], out_specs=(blk, blk, blk), compiler_params=_params("parallel"),
    )(*[a.reshape(rows, ADAM_COLS) for a in (w, g, m, v)])
    return d.reshape(shape), nm.reshape(shape), nv.reshape(shape)


HBM = pl.BlockSpec(memory_space=pltpu.HBM)


def _coords():
    return lax.axis_index("x"), lax.axis_index("y"), lax.axis_index("c")


def _other_chips(x, y):
    return [(1 - x, y), (x, 1 - y), (1 - x, 1 - y)]


def _allgather_chips(shard, name):
    r, cols = shard.shape
    half = r // 2

    def body(src_ref, out_ref, send_sems, recv_sems, local_sem):
        x, y, c = _coords()
        sibling = (x, y, 1 - c)
        chips = _other_chips(x, y)

        def rows(px, py, h):
            return out_ref.at[2 * px + py, pl.ds(h * half, half), :]

        def copy(k, block, to, src=None):
            return pltpu.make_async_remote_copy(
                src_ref=rows(*block) if src is None else src, dst_ref=rows(*block),
                send_sem=send_sems.at[k], recv_sem=recv_sems.at[k], device_id=to, device_id_type=MESH)

        mine = pltpu.make_async_copy(src_ref, out_ref.at[2 * x + y], local_sem)
        mine.start()
        my_half = src_ref.at[pl.ds(c * half, half), :]
        first = [copy(j, (x, y, c), (*chip, c), src=my_half) for j, chip in enumerate(chips)]
        for cp in first:
            cp.start()
        passed = [copy(3 + j, (*chip, c), sibling) for j, chip in enumerate(chips)]
        for j, chip in enumerate(chips):
            copy(j, (*chip, c), (x, y, c)).wait_recv()
            passed[j].start()
        for j, chip in enumerate(chips):
            copy(3 + j, (*chip, 1 - c), (x, y, c)).wait_recv()
        for cp in first + passed:
            cp.wait_send()
        mine.wait()

    return pl.pallas_call(
        body, name=name, out_shape=jax.ShapeDtypeStruct((N_CHIPS, r, cols), shard.dtype),
        in_specs=[HBM], out_specs=HBM,
        scratch_shapes=[pltpu.SemaphoreType.DMA((6,)), pltpu.SemaphoreType.DMA((6,)), pltpu.SemaphoreType.DMA],
    )(shard)


def _exchange_sibling_halves(g, name):
    n, r, cols = g.shape
    half = r // 2

    def body(g_ref, out_ref, send_sem, recv_sem):
        x, y, c = _coords()
        cp = pltpu.make_async_remote_copy(
            src_ref=g_ref.at[:, pl.ds((1 - c) * half, half), :], dst_ref=out_ref,
            send_sem=send_sem, recv_sem=recv_sem, device_id=(x, y, 1 - c), device_id_type=MESH)
        cp.start()
        cp.wait()

    return pl.pallas_call(
        body, name=name, out_shape=jax.ShapeDtypeStruct((n, half, cols), g.dtype),
        in_specs=[HBM], out_specs=HBM,
        scratch_shapes=[pltpu.SemaphoreType.DMA, pltpu.SemaphoreType.DMA],
    )(g)


def _scatter_to_chips(p, name):
    n, h, cols = p.shape

    def body(p_ref, out_ref, send_sems, recv_sems, local_sem):
        x, y, c = _coords()
        me = 2 * x + y
        mine = pltpu.make_async_copy(p_ref.at[me], out_ref.at[me], local_sem)
        mine.start()
        sends = []
        for j, (px, py) in enumerate(_other_chips(x, y)):
            sends.append(pltpu.make_async_remote_copy(
                src_ref=p_ref.at[2 * px + py], dst_ref=out_ref.at[me],
                send_sem=send_sems.at[j], recv_sem=recv_sems.at[j], device_id=(px, py, c), device_id_type=MESH))
        for cp in sends:
            cp.start()
        for j, (px, py) in enumerate(_other_chips(x, y)):
            pltpu.make_async_remote_copy(
                src_ref=p_ref.at[me], dst_ref=out_ref.at[2 * px + py],
                send_sem=send_sems.at[j], recv_sem=recv_sems.at[j], device_id=(px, py, c),
                device_id_type=MESH).wait_recv()
        for cp in sends:
            cp.wait_send()
        mine.wait()

    return pl.pallas_call(
        body, name=name, out_shape=jax.ShapeDtypeStruct((n, h, cols), p.dtype),
        in_specs=[HBM], out_specs=HBM,
        scratch_shapes=[pltpu.SemaphoreType.DMA((3,)), pltpu.SemaphoreType.DMA((3,)), pltpu.SemaphoreType.DMA],
    )(p)


def _swap_with_sibling(v, name):
    def body(v_ref, out_ref, send_sem, recv_sem):
        x, y, c = _coords()
        cp = pltpu.make_async_remote_copy(
            src_ref=v_ref, dst_ref=out_ref, send_sem=send_sem, recv_sem=recv_sem,
            device_id=(x, y, 1 - c), device_id_type=MESH)
        cp.start()
        cp.wait()

    return pl.pallas_call(
        body, name=name, out_shape=jax.ShapeDtypeStruct(v.shape, v.dtype),
        in_specs=[HBM], out_specs=HBM,
        scratch_shapes=[pltpu.SemaphoreType.DMA, pltpu.SemaphoreType.DMA],
    )(v)


def _allreduce_small(v, name):
    r, cols = v.shape

    def body(v_ref, out_ref, buf_ref, send_sems, recv_sems):
        x, y, c = _coords()
        me = 4 * x + 2 * y + c
        buf_ref[me] = v_ref[...]
        sends = []
        for k in range(1, N_DEV):
            px = 1 - x if k & 4 else x
            py = 1 - y if k & 2 else y
            pc = 1 - c if k & 1 else c
            sends.append(pltpu.make_async_remote_copy(
                src_ref=v_ref, dst_ref=buf_ref.at[me], send_sem=send_sems.at[k - 1], recv_sem=recv_sems.at[k - 1],
                device_id=(px, py, pc), device_id_type=MESH))
        for cp in sends:
            cp.start()
        for cp in sends:
            cp.wait()
        acc = buf_ref[0]
        for d in range(1, N_DEV):
            acc = acc + buf_ref[d]
        out_ref[...] = acc

    return pl.pallas_call(
        body, name=name, out_shape=jax.ShapeDtypeStruct((r, cols), F32),
        in_specs=[pl.BlockSpec(memory_space=pltpu.VMEM)], out_specs=pl.BlockSpec(memory_space=pltpu.VMEM),
        scratch_shapes=[pltpu.VMEM((N_DEV, r, cols), F32), pltpu.SemaphoreType.DMA((N_DEV - 1,)),
                        pltpu.SemaphoreType.DMA((N_DEV - 1,))],
    )(v)


def _add2(a, b, name):
    n, r, cols = a.shape
    tr = _pick(r, (512, 256, 128))

    def body(a_ref, b_ref, o_ref):
        o_ref[...] = a_ref[...] + b_ref[...]

    blk = pl.BlockSpec((1, tr, cols), lambda s, i: (s, i, 0))
    return pl.pallas_call(
        body, name=name, out_shape=jax.ShapeDtypeStruct(a.shape, a.dtype), grid=(n, r // tr),
        in_specs=[blk, blk], out_specs=blk, compiler_params=_params("parallel", "parallel"),
    )(a, b)


def _sum_slots(p, name):
    n, r, cols = p.shape
    tr = _pick(r, (512, 256, 128))

    def body(p_ref, o_ref):
        o_ref[...] = ((p_ref[0] + p_ref[1]) + p_ref[2]) + p_ref[3]

    return pl.pallas_call(
        body, name=name, out_shape=jax.ShapeDtypeStruct((r, cols), p.dtype), grid=(r // tr,),
        in_specs=[pl.BlockSpec((n, tr, cols), lambda i: (0, i, 0))],
        out_specs=pl.BlockSpec((tr, cols), lambda i: (i, 0)), compiler_params=_params("parallel"),
    )(p)


PACK_COLS = 1024


def _pack_shards(parts):
    return jnp.concatenate([p.reshape(-1, PACK_COLS) for p in parts], axis=0)


def _unpack_shards(buf, shapes):
    out, row = [], 0
    for shp in shapes:
        nrows = math.prod(shp) // PACK_COLS
        out.append(buf[..., row:row + nrows, :].reshape(buf.shape[:-2] + tuple(shp)))
        row += nrows
    return out


def _local_step(x, target, w):
    t = lambda a: a.T
    g = {}
    h0 = _rms_fwd(x, w["norm_mix_g"][0], "rms_mix0")
    w_in_g, w_in_x = w["a_w_in"][:, :D_RNN], w["a_w_in"][:, D_RNN:]
    gate_br = _matmul([(h0, w_in_g)], F32, "mm_a_gate")
    x_br = _matmul([(h0, w_in_x)], F32, "mm_a_xbr")
    y_a, hs = _rglru_fwd(gate_br, x_br, w["a_conv_w"], w["a_conv_b"], w["a_w_r"], w["a_w_i"], w["a_b_r"],
                         w["a_b_i"], w["a_lambda"], "rglru_fwd")
    x1 = _matmul([(y_a, w["a_w_out"])], F32, "mm_a_out", addend=x)
    h1 = _rms_fwd(x1, w["norm_ffn_g"][0], "rms_ffn0")
    fg0 = _matmul([(h1, w["ffn_w_gate"][0])], F32, "mm_f0_gate")
    fu0 = _matmul([(h1, w["ffn_w_up"][0])], F32, "mm_f0_up")
    act0 = _swiglu_fwd(fg0, fu0, "swiglu0_fwd")
    x2 = _matmul([(act0, w["ffn_w_down"][0])], F32, "mm_f0_down", addend=x1)
    h2 = _rms_fwd(x2, w["norm_mix_g"][1], "rms_mix1")
    qkv = _matmul([(h2, w["b_w_qkv"])], CD, "mm_b_qkv", out_lbm=True, tn=1024)
    o = _attn_fwd(qkv, "attn_fwd")
    x3 = _matmul([(o, w["b_w_out"])], F32, "mm_b_out", a_lbm=True, addend=x2)
    h3 = _rms_fwd(x3, w["norm_ffn_g"][1], "rms_ffn1")
    fg1 = _matmul([(h3, w["ffn_w_gate"][1])], F32, "mm_f1_gate")
    fu1 = _matmul([(h3, w["ffn_w_up"][1])], F32, "mm_f1_up")
    act1 = _swiglu_fwd(fg1, fu1, "swiglu1_fwd")
    x4 = _matmul([(act1, w["ffn_w_down"][1])], F32, "mm_f1_down", addend=x3)
    loss, dx4, g["final_g"] = _loss_head(x4, w["final_g"], target, "loss_head")

    def ffn_bwd(dx_out, h, x_in, fg, fu, act, layer, tag):
        dxc = dx_out.astype(CD)
        dact = _matmul([(dxc, t(w["ffn_w_down"][layer]))], F32, "mm_" + tag + "_dact")
        dwd = _matmul([(act, dxc)], F32, "mm_" + tag + "_dwd", trans_a=True)
        dg, du = _swiglu_bwd(dact, fg, fu, "swiglu" + tag + "_bwd")
        dwg = _matmul([(h, dg)], F32, "mm_" + tag + "_dwg", trans_a=True)
        dwu = _matmul([(h, du)], F32, "mm_" + tag + "_dwu", trans_a=True)
        dh = _matmul([(dg, t(w["ffn_w_gate"][layer])), (du, t(w["ffn_w_up"][layer]))], F32, "mm_" + tag + "_dh")
        dx_in, dgain = _rms_bwd(dh, x_in, w["norm_ffn_g"][layer], dx_out, "rms_ffn" + tag + "_bwd")
        return dx_in, dgain, dwg, dwu, dwd

    dx3, dgf1, dwg1, dwu1, dwd1 = ffn_bwd(dx4, h3, x3, fg1, fu1, act1, 1, "f1")
    dx3c = dx3.astype(CD)
    do = _matmul([(dx3c, t(w["b_w_out"]))], F32, "mm_b_do", out_lbm=True, tn=1024)
    g["b_w_out"] = _matmul([(o, dx3c)], F32, "mm_b_dwout", trans_a=True, a_lbm=True)
    dq, dk, dv = _attn_bwd(qkv, o, do, "attn_bwd")
    wq_t = t(w["b_w_qkv"])
    parts = (dq, dk, dv)
    g["b_w_qkv"] = jnp.concatenate(
        [_matmul([(h2, p)], F32, "mm_b_dwqkv%d" % n, trans_a=True, b_lbm=True) for n, p in enumerate(parts)], axis=1)
    dh2 = _matmul([(p, wq_t[n * D_MODEL:(n + 1) * D_MODEL]) for n, p in enumerate(parts)], F32, "mm_b_dh",
                  a_lbm=True)
    dx2, dgm1 = _rms_bwd(dh2, x2, w["norm_mix_g"][1], dx3, "rms_mix1_bwd")
    dx1, dgf0, dwg0, dwu0, dwd0 = ffn_bwd(dx2, h1, x1, fg0, fu0, act0, 0, "f0")
    dx1c = dx1.astype(CD)
    dy_a = _matmul([(dx1c, t(w["a_w_out"]))], F32, "mm_a_dy")
    g["a_w_out"] = _matmul([(y_a, dx1c)], F32, "mm_a_dwout", trans_a=True)
    wrt = jnp.swapaxes(w["a_w_r"], 1, 2)
    wit = jnp.swapaxes(w["a_w_i"], 1, 2)
    (dgate, dxbr, g["a_conv_w"], g["a_conv_b"], g["a_b_r"], g["a_b_i"], g["a_lambda"], g["a_w_r"],
     g["a_w_i"]) = _rglru_bwd(dy_a, gate_br, x_br, hs, w["a_conv_w"], w["a_conv_b"], w["a_w_r"], w["a_w_i"], wrt, wit,
                              w["a_b_r"], w["a_b_i"], w["a_lambda"], "rglru_bwd")
    g["a_w_in"] = jnp.concatenate([_matmul([(h0, dgate)], F32, "mm_a_dwin_g", trans_a=True),
                                   _matmul([(h0, dxbr)], F32, "mm_a_dwin_x", trans_a=True)], axis=1)
    dh0 = _matmul([(dgate, t(w_in_g)), (dxbr, t(w_in_x))], F32, "mm_a_dh")
    dx0, dgm0 = _rms_bwd(dh0, x, w["norm_mix_g"][0], dx1, "rms_mix0_bwd")
    g["norm_mix_g"] = jnp.concatenate([dgm0, dgm1], axis=0)
    g["norm_ffn_g"] = jnp.concatenate([dgf0, dgf1], axis=0)
    g["ffn_w_gate"] = jnp.stack([dwg0, dwg1])
    g["ffn_w_up"] = jnp.stack([dwu0, dwu1])
    g["ffn_w_down"] = jnp.stack([dwd0, dwd1])
    return loss, dx0, g


WEIGHTS = ["norm_mix_g", "norm_ffn_g", "a_w_in", "a_conv_w", "a_conv_b", "a_w_r", "a_b_r", "a_w_i", "a_b_i",
           "a_lambda", "a_w_out", "b_w_qkv", "b_w_out", "ffn_w_gate", "ffn_w_up", "ffn_w_down", "final_g"]
BIG = [("a_w_in", 2), ("a_w_r", 2), ("a_w_i", 2), ("a_w_out", 1), ("b_w_qkv", 2), ("b_w_out", 1),
       ("ffn_w_gate", 2), ("ffn_w_up", 2), ("ffn_w_down", 1)]
SMALL = ["norm_mix_g", "norm_ffn_g", "a_conv_w", "a_conv_b", "a_b_r", "a_b_i", "a_lambda", "final_g"]


def _join_chips(stack, axis):
    return jnp.concatenate([stack[s] for s in range(N_CHIPS)], axis=axis)


def _split_chips(full, axis):
    return jnp.stack(jnp.split(full, N_CHIPS, axis=axis))


def _step(x, target, weights, moments_m, moments_v):
    chip = 2 * lax.axis_index("x") + lax.axis_index("y")
    core = lax.axis_index("c")
    shard_shapes = [weights[n].shape for n, _ in BIG]
    packed = _pack_shards([weights[n].astype(CD) for n, _ in BIG])
    gathered = _allgather_chips(packed, "allgather_weights")
    full = {}
    for (n, axis), stack in zip(BIG, _unpack_shards(gathered, shard_shapes)):
        joined = _join_chips(stack, axis)
        full[n] = joined[0] if joined.shape[0] == 1 else joined
    cw_rows = jnp.zeros((N_CHIPS, CONV_W, RG_BW), F32)
    cw_rows = lax.dynamic_update_slice(cw_rows, jnp.where(core == 0, weights["a_conv_w"], 0.0), (chip, 0, 0))
    cw_all = _allreduce_small(cw_rows.reshape(-1, LANES), "allgather_conv_w").reshape(N_CHIPS, CONV_W, RG_BW)
    full["a_conv_w"] = jnp.concatenate([cw_all[s] for s in range(N_CHIPS)], axis=1)
    for n in ("norm_mix_g", "norm_ffn_g", "final_g"):
        full[n] = weights[n]
    for n in ("a_conv_b", "a_b_r", "a_b_i", "a_lambda"):
        full[n] = weights[n]
    loss, dx, grads = _local_step(x[0], target[0], full)
    small_parts = [grads[n].reshape(-1) for n in SMALL] + [loss.reshape(-1)]
    sizes = [p.shape[0] for p in small_parts]
    small = _allreduce_small(jnp.concatenate(small_parts).reshape(-1, LANES), "allreduce_small").reshape(-1)
    red, pos = {}, 0
    for n, sz in zip(SMALL + ["loss"], sizes):
        red[n] = small[pos:pos + sz]
        pos += sz
    loss_out = red["loss"][0]
    g_out = {}
    for n in SMALL:
        if n == "a_conv_w":
            g_out[n] = lax.dynamic_slice(red[n].reshape(CONV_W, D_RNN), (0, chip * RG_BW), (CONV_W, RG_BW)).reshape(
                weights[n].shape)
        else:
            g_out[n] = red[n].reshape(weights[n].shape)
    stacks = []
    for n, axis in BIG:
        gfull = grads[n].reshape((1,) + grads[n].shape) if grads[n].ndim == len(weights[n].shape) - 1 else grads[n]
        stacks.append(_split_chips(gfull, axis).reshape(N_CHIPS, -1, PACK_COLS))
    gbuf = jnp.concatenate(stacks, axis=1)
    half = gbuf.shape[1] // 2
    from_sibling = _exchange_sibling_halves(gbuf, "rs_sibling")
    mine = lax.dynamic_slice_in_dim(gbuf, core * half, half, axis=1)
    chip_partial = _add2(mine, from_sibling, "rs_add_sibling")
    from_chips = _scatter_to_chips(chip_partial, "rs_chips")
    reduced_half = _sum_slots(from_chips, "rs_sum_chips")
    other_half = _swap_with_sibling(reduced_half, "rs_share")
    lo = jnp.where(core == 0, reduced_half, other_half)
    hi = jnp.where(core == 0, other_half, reduced_half)
    reduced = jnp.concatenate([lo, hi], axis=0)
    for (n, _), gsh in zip(BIG, _unpack_shards(reduced, shard_shapes)):
        g_out[n] = gsh
    outs_g, outs_d, outs_m, outs_v = [], [], [], []
    for n in WEIGHTS:
        d, nm, nv = _adamw(weights[n], g_out[n], moments_m[n], moments_v[n], "adamw_" + n)
        outs_g.append(g_out[n])
        outs_d.append(d)
        outs_m.append(nm)
        outs_v.append(nv)
    return (loss_out, dx[None], *outs_g, *outs_d, *outs_m, *outs_v)


def kernel(x, norm_mix_g, norm_ffn_g, a_w_in, a_conv_w, a_conv_b, a_w_r, a_b_r, a_w_i, a_b_i, a_lambda, a_w_out, b_w_qkv, b_w_out, ffn_w_gate, ffn_w_up, ffn_w_down, final_g, loss_target, m_norm_mix_g, m_norm_ffn_g, m_a_w_in, m_a_conv_w, m_a_conv_b, m_a_w_r, m_a_b_r, m_a_w_i, m_a_b_i, m_a_lambda, m_a_w_out, m_b_w_qkv, m_b_w_out, m_ffn_w_gate, m_ffn_w_up, m_ffn_w_down, m_final_g, v_norm_mix_g, v_norm_ffn_g, v_a_w_in, v_a_conv_w, v_a_conv_b, v_a_w_r, v_a_b_r, v_a_w_i, v_a_b_i, v_a_lambda, v_a_w_out, v_b_w_qkv, v_b_w_out, v_ffn_w_gate, v_ffn_w_up, v_ffn_w_down, v_final_g):
    ws = [norm_mix_g, norm_ffn_g, a_w_in, a_conv_w, a_conv_b, a_w_r, a_b_r, a_w_i, a_b_i, a_lambda, a_w_out, b_w_qkv,
          b_w_out, ffn_w_gate, ffn_w_up, ffn_w_down, final_g]
    ms = [m_norm_mix_g, m_norm_ffn_g, m_a_w_in, m_a_conv_w, m_a_conv_b, m_a_w_r, m_a_b_r, m_a_w_i, m_a_b_i, m_a_lambda,
          m_a_w_out, m_b_w_qkv, m_b_w_out, m_ffn_w_gate, m_ffn_w_up, m_ffn_w_down, m_final_g]
    vs = [v_norm_mix_g, v_norm_ffn_g, v_a_w_in, v_a_conv_w, v_a_conv_b, v_a_w_r, v_a_b_r, v_a_w_i, v_a_b_i, v_a_lambda,
          v_a_w_out, v_b_w_qkv, v_b_w_out, v_ffn_w_gate, v_ffn_w_up, v_ffn_w_down, v_final_g]
    return _step(x, loss_target, dict(zip(WEIGHTS, ws)), dict(zip(WEIGHTS, ms)), dict(zip(WEIGHTS, vs)))
```

```python
import functools
import math

import jax
import jax.numpy as jnp
from jax import lax
from jax.experimental import pallas as pl
from jax.experimental.pallas import tpu as pltpu

F32 = jnp.float32
CD = jnp.bfloat16

D_MODEL = 1024
D_RNN = 1024
RG_BLOCKS = 4
RG_BW = 256
CONV_W = 4
RG_C = 8.0
SB_HEADS = 16
SB_HEAD_DIM = 64
D_FF = 2816
RMS_EPS = 1e-6
N_CHIPS = 4
N_DEV = 8

ADAM_LR = 0.001
ADAM_B1 = 0.9
ADAM_B2 = 0.999
ADAM_EPS = 1e-08
ADAM_WD = 0.01
ADAM_STEP = 10

LANES = 128
VMEM_LIMIT = 56 * 1024 * 1024
MESH = pl.DeviceIdType.MESH


def _params(*sem):
    return pltpu.CompilerParams(dimension_semantics=sem, vmem_limit_bytes=VMEM_LIMIT)


def _pick(n, prefs):
    for p in prefs:
        if n % p == 0:
            return p
    return n


def _matmul(pairs, out_dtype, name, *, trans_a=False, a_lbm=False, b_lbm=False, out_lbm=False, addend=None,
            tm=512, tn=None, tk=None):
    a0, b0 = pairs[0]
    if trans_a:
        kdim = a0.shape[1] if a_lbm else a0.shape[0]
        m = a0.shape[0] * LANES if a_lbm else a0.shape[1]
    else:
        m = a0.shape[1] if a_lbm else a0.shape[0]
        kdim = a0.shape[0] * LANES if a_lbm else a0.shape[1]
    n = b0.shape[0] * LANES if b_lbm else b0.shape[1]
    tm = _pick(m, (tm, 1408, 256, 128))
    tn = tn or _pick(n, (1408, 1024, 768, 512, 256, 128))
    tk = tk or _pick(kdim, (1024, 1408, 512, 256, 128))
    nk = kdim // tk
    npair = len(pairs)

    def cat(ref):
        return jnp.concatenate([ref[p] for p in range(ref.shape[0])], axis=-1)

    def body(*refs):
        ins = refs[: 2 * npair]
        pos = 2 * npair
        add_ref = None
        if addend is not None:
            add_ref = refs[pos]
            pos += 1
        o_ref = refs[pos]
        acc_ref = refs[pos + 1]
        k = pl.program_id(2)

        @pl.when(k == 0)
        def _():
            acc_ref[...] = jnp.zeros_like(acc_ref)

        acc = acc_ref[...]
        for p in range(npair):
            a = (cat(ins[2 * p]) if a_lbm else ins[2 * p][...]).astype(CD)
            b = (cat(ins[2 * p + 1]) if b_lbm else ins[2 * p + 1][...]).astype(CD)
            dims = (((0,), (0,)), ((), ())) if trans_a else (((1,), (0,)), ((), ()))
            acc = acc + lax.dot_general(a, b, dims, preferred_element_type=F32)
        acc_ref[...] = acc

        @pl.when(k == nk - 1)
        def _():
            res = acc_ref[...]
            if add_ref is not None:
                res = res + add_ref[...]
            res = res.astype(out_dtype)
            if out_lbm:
                for p in range(tn // LANES):
                    o_ref[p] = res[:, p * LANES:(p + 1) * LANES]
            else:
                o_ref[...] = res

    if trans_a:
        a_spec = (pl.BlockSpec((tm // LANES, tk, LANES), lambda i, j, k: (i, k, 0)) if a_lbm
                  else pl.BlockSpec((tk, tm), lambda i, j, k: (k, i)))
    else:
        a_spec = (pl.BlockSpec((tk // LANES, tm, LANES), lambda i, j, k: (k, i, 0)) if a_lbm
                  else pl.BlockSpec((tm, tk), lambda i, j, k: (i, k)))
    b_spec = (pl.BlockSpec((tn // LANES, tk, LANES), lambda i, j, k: (j, k, 0)) if b_lbm
              else pl.BlockSpec((tk, tn), lambda i, j, k: (k, j)))
    in_specs = []
    args = []
    for a, b in pairs:
        in_specs += [a_spec, b_spec]
        args += [a, b]
    if addend is not None:
        in_specs.append(pl.BlockSpec((tm, tn), lambda i, j, k: (i, j)))
        args.append(addend)
    if out_lbm:
        out_shape = jax.ShapeDtypeStruct((n // LANES, m, LANES), out_dtype)
        out_spec = pl.BlockSpec((tn // LANES, tm, LANES), lambda i, j, k: (j, i, 0))
    else:
        out_shape = jax.ShapeDtypeStruct((m, n), out_dtype)
        out_spec = pl.BlockSpec((tm, tn), lambda i, j, k: (i, j))
    return pl.pallas_call(
        body, name=name, out_shape=out_shape, grid=(m // tm, n // tn, nk),
        in_specs=in_specs, out_specs=out_spec,
        scratch_shapes=[pltpu.VMEM((tm, tn), F32)],
        compiler_params=_params("parallel", "parallel", "arbitrary"),
    )(*args)


ROW_BLOCK = 256


def _rms_fwd(x, g, name):
    s, d = x.shape

    def body(x_ref, g_ref, h_ref):
        xv = x_ref[...]
        rinv = lax.rsqrt(jnp.mean(xv * xv, axis=-1, keepdims=True) + RMS_EPS)
        h_ref[...] = (xv * rinv * g_ref[...]).astype(CD)

    return pl.pallas_call(
        body, name=name, out_shape=jax.ShapeDtypeStruct((s, d), CD), grid=(s // ROW_BLOCK,),
        in_specs=[pl.BlockSpec((ROW_BLOCK, d), lambda i: (i, 0)), pl.BlockSpec((1, d), lambda i: (0, 0))],
        out_specs=pl.BlockSpec((ROW_BLOCK, d), lambda i: (i, 0)),
        compiler_params=_params("parallel"),
    )(x, g.reshape(1, d))


def _rms_bwd(dh, x, g, dx_in, name):
    s, d = x.shape

    def body(dh_ref, x_ref, g_ref, dxin_ref, dx_ref, dg_ref):
        @pl.when(pl.program_id(0) == 0)
        def _():
            dg_ref[...] = jnp.zeros_like(dg_ref)

        xv = x_ref[...]
        dhv = dh_ref[...]
        rinv = lax.rsqrt(jnp.mean(xv * xv, axis=-1, keepdims=True) + RMS_EPS)
        nrm = xv * rinv
        dn = dhv * g_ref[...]
        dx_ref[...] = dxin_ref[...] + rinv * (dn - nrm * jnp.mean(dn * nrm, axis=-1, keepdims=True))
        dg_ref[...] += jnp.sum(dhv * nrm, axis=0, keepdims=True)

    row = pl.BlockSpec((ROW_BLOCK, d), lambda i: (i, 0))
    vec = pl.BlockSpec((1, d), lambda i: (0, 0))
    return pl.pallas_call(
        body, name=name,
        out_shape=(jax.ShapeDtypeStruct((s, d), F32), jax.ShapeDtypeStruct((1, d), F32)),
        grid=(s // ROW_BLOCK,), in_specs=[row, row, vec, row], out_specs=(row, vec),
        compiler_params=_params("arbitrary"),
    )(dh, x, g.reshape(1, d), dx_in)


def _loss_head(x, g, target, name):
    s, d = x.shape

    def body(x_ref, g_ref, t_ref, loss_ref, dx_ref, dg_ref):
        @pl.when(pl.program_id(0) == 0)
        def _():
            dg_ref[...] = jnp.zeros_like(dg_ref)
            loss_ref[...] = jnp.zeros_like(loss_ref)

        xv = x_ref[...]
        gv = g_ref[...]
        rinv = lax.rsqrt(jnp.mean(xv * xv, axis=-1, keepdims=True) + RMS_EPS)
        nrm = xv * rinv
        err = nrm * gv - t_ref[...]
        loss_ref[...] += 0.5 * jnp.sum(jnp.mean(err * err, axis=-1, keepdims=True), axis=0, keepdims=True)
        dy = err * (1.0 / d)
        dn = dy * gv
        dx_ref[...] = rinv * (dn - nrm * jnp.mean(dn * nrm, axis=-1, keepdims=True))
        dg_ref[...] += jnp.sum(dy * nrm, axis=0, keepdims=True)

    row = pl.BlockSpec((ROW_BLOCK, d), lambda i: (i, 0))
    vec = pl.BlockSpec((1, d), lambda i: (0, 0))
    return pl.pallas_call(
        body, name=name,
        out_shape=(jax.ShapeDtypeStruct((1, LANES), F32), jax.ShapeDtypeStruct((s, d), F32),
                   jax.ShapeDtypeStruct((1, d), F32)),
        grid=(s // ROW_BLOCK,), in_specs=[row, vec, row],
        out_specs=(pl.BlockSpec((1, LANES), lambda i: (0, 0)), row, vec),
        compiler_params=_params("arbitrary"),
    )(x, g.reshape(1, d), target)


def _sigmoid(z):
    return 1.0 / (1.0 + jnp.exp(-z))


def _swiglu_fwd(g, u, name):
    s, f = g.shape

    def body(g_ref, u_ref, a_ref):
        gv = g_ref[...]
        a_ref[...] = (gv * _sigmoid(gv) * u_ref[...]).astype(CD)

    row = pl.BlockSpec((ROW_BLOCK, f), lambda i: (i, 0))
    return pl.pallas_call(
        body, name=name, out_shape=jax.ShapeDtypeStruct((s, f), CD), grid=(s // ROW_BLOCK,),
        in_specs=[row, row], out_specs=row, compiler_params=_params("parallel"),
    )(g, u)


def _swiglu_bwd(dact, g, u, name):
    s, f = g.shape

    def body(da_ref, g_ref, u_ref, dg_ref, du_ref):
        gv = g_ref[...]
        da = da_ref[...]
        sg = _sigmoid(gv)
        silu = gv * sg
        dg_ref[...] = (da * u_ref[...] * (sg + silu * (1.0 - sg))).astype(CD)
        du_ref[...] = (da * silu).astype(CD)

    row = pl.BlockSpec((ROW_BLOCK, f), lambda i: (i, 0))
    return pl.pallas_call(
        body, name=name,
        out_shape=(jax.ShapeDtypeStruct((s, f), CD), jax.ShapeDtypeStruct((s, f), CD)),
        grid=(s // ROW_BLOCK,), in_specs=[row, row, row], out_specs=(row, row),
        compiler_params=_params("parallel"),
    )(dact, g, u)


TIME_BLOCK = 256
SUBLANES = 8
GELU_C = math.sqrt(2.0 / math.pi)
GELU_A = 0.044715


def _gelu(x):
    return 0.5 * x * (1.0 + jnp.tanh(GELU_C * (x + GELU_A * x * x * x)))


def _gelu_grad(x):
    t = jnp.tanh(GELU_C * (x + GELU_A * x * x * x))
    return 0.5 * (1.0 + t) + 0.5 * x * (1.0 - t * t) * GELU_C * (1.0 + 3.0 * GELU_A * x * x)


def _neg_expm1(x):
    series = -x * (1.0 + x * (0.5 + x * (1.0 / 6.0 + x * (1.0 / 24.0))))
    return jnp.where(x > -0.05, series, 1.0 - jnp.exp(x))


def _log_sigmoid(x):
    return jnp.minimum(x, 0.0) - jnp.log1p(jnp.exp(-jnp.abs(x)))


def _shift_down(x, tail, s):
    if s == 0:
        return x
    ext = jnp.concatenate([tail, x], axis=0)
    return pltpu.roll(ext, s, axis=0)[SUBLANES:]


def _shift_up(x, head, s):
    if s == 0:
        return x
    n = x.shape[0]
    ext = jnp.concatenate([x, head], axis=0)
    return pltpu.roll(ext, n + SUBLANES - s, axis=0)[:n]


def _rg_gates(xbr, tail, cw_ref, cb, wr, wi, br, bi, ls):
    taps = [_shift_down(xbr, tail, CONV_W - 1 - k) for k in range(CONV_W)]
    xc = cb
    for k in range(CONV_W):
        xc = xc + cw_ref[pl.ds(k, 1), :] * taps[k]
    xcd = xc.astype(CD)
    r = _sigmoid(jnp.dot(xcd, wr, preferred_element_type=F32) + br)
    i = _sigmoid(jnp.dot(xcd, wi, preferred_element_type=F32) + bi)
    log_a = RG_C * r * ls
    a = jnp.exp(log_a)
    mult = jnp.sqrt(jnp.maximum(_neg_expm1(2.0 * log_a), 0.0))
    return taps, xc, r, i, log_a, a, mult


def _scan8_fwd(a, u):
    row = lax.broadcasted_iota(jnp.int32, a.shape, 0)
    for d in (1, 2, 4):
        a_s = pltpu.roll(a, d, axis=0)
        u_s = pltpu.roll(u, d, axis=0)
        m = row >= d
        u = jnp.where(m, a * u_s + u, u)
        a = jnp.where(m, a * a_s, a)
    return a, u


def _scan8_bwd(b, u):
    row = lax.broadcasted_iota(jnp.int32, b.shape, 0)
    for d in (1, 2, 4):
        b_s = pltpu.roll(b, SUBLANES - d, axis=0)
        u_s = pltpu.roll(u, SUBLANES - d, axis=0)
        m = row < SUBLANES - d
        u = jnp.where(m, b * u_s + u, u)
        b = jnp.where(m, b * b_s, b)
    return b, u


def _rglru_fwd(gate_br, x_br, cw, cb, wr, wi, br, bi, lam, name):
    s, c = x_br.shape
    nt = s // TIME_BLOCK
    tb, cbw = TIME_BLOCK, RG_BW
    groups = tb // SUBLANES

    def body(g_ref, x_ref, tail_ref, cw_ref, cb_ref, wr_ref, wi_ref, br_ref, bi_ref, lam_ref,
             y_ref, hs_ref, carry_ref, a_scr, u_scr):
        t = pl.program_id(1)

        @pl.when(t == 0)
        def _():
            carry_ref[...] = jnp.zeros_like(carry_ref)

        tail = jnp.where(t > 0, tail_ref[...], 0.0)
        ls = _log_sigmoid(lam_ref[...])
        _, xc, _, i, _, a, mult = _rg_gates(x_ref[...], tail, cw_ref, cb_ref[...], wr_ref[0], wi_ref[0],
                                            br_ref[...], bi_ref[...], ls)
        a_scr[...] = a
        u_scr[...] = mult * (i * xc)
        carry = carry_ref[...]
        for gi in range(groups):
            rows = pl.ds(gi * SUBLANES, SUBLANES)
            pa, hl = _scan8_fwd(a_scr[rows, :], u_scr[rows, :])
            hs_ref[rows, :] = hl + pa * carry
            carry = hs_ref[pl.ds(gi * SUBLANES + SUBLANES - 1, 1), :]
        carry_ref[...] = carry
        y_ref[...] = (hs_ref[...] * _gelu(g_ref[...])).astype(CD)

    blk = pl.BlockSpec((tb, cbw), lambda n, t: (t, n))
    tail = pl.BlockSpec((SUBLANES, cbw), lambda n, t: (jnp.maximum(t * groups - 1, 0), n))
    vec = pl.BlockSpec((1, cbw), lambda n, t: (0, n))
    wblk = pl.BlockSpec((1, cbw, cbw), lambda n, t: (n, 0, 0))
    return pl.pallas_call(
        body, name=name,
        out_shape=(jax.ShapeDtypeStruct((s, c), CD), jax.ShapeDtypeStruct((s, c), F32)),
        grid=(RG_BLOCKS, nt),
        in_specs=[blk, blk, tail, pl.BlockSpec((CONV_W, cbw), lambda n, t: (0, n)), vec, wblk, wblk, vec, vec, vec],
        out_specs=(blk, blk),
        scratch_shapes=[pltpu.VMEM((1, cbw), F32), pltpu.VMEM((tb, cbw), F32), pltpu.VMEM((tb, cbw), F32)],
        compiler_params=_params("parallel", "arbitrary"),
    )(gate_br, x_br, x_br, cw, cb, wr, wi, br, bi, lam)


def _rglru_bwd(dy, gate_br, x_br, hs, cw, cb, wr, wi, wrt, wit, br, bi, lam, name):
    s, c = x_br.shape
    nt = s // TIME_BLOCK
    tb, cbw = TIME_BLOCK, RG_BW
    groups = tb // SUBLANES

    def body(dy_ref, g_ref, x_ref, tail_ref, hs_ref, hprev_ref, cw_ref, cb_ref, wr_ref, wi_ref, wrt_ref, wit_ref,
             br_ref, bi_ref, lam_ref,
             dg_ref, dx_ref, dcw_ref, dcb_ref, dbr_ref, dbi_ref, dlam_ref, dwr_ref, dwi_ref,
             carry_ref, head_ref, b_scr, u_scr, dh_scr):
        tr = pl.program_id(1)
        first_block = tr == nt - 1

        @pl.when(tr == 0)
        def _():
            carry_ref[...] = jnp.zeros_like(carry_ref)
            head_ref[...] = jnp.zeros_like(head_ref)
            for ref in (dcw_ref, dcb_ref, dbr_ref, dbi_ref, dlam_ref, dwr_ref, dwi_ref):
                ref[...] = jnp.zeros_like(ref)

        tail = jnp.where(first_block, 0.0, tail_ref[...])
        lam_v = lam_ref[...]
        ls = _log_sigmoid(lam_v)
        taps, xc, r, i, log_a, a, mult = _rg_gates(x_ref[...], tail, cw_ref, cb_ref[...], wr_ref[0], wi_ref[0],
                                                   br_ref[...], bi_ref[...], ls)
        gate_v = g_ref[...]
        dyv = dy_ref[...]
        hsv = hs_ref[...]
        dg_ref[...] = (dyv * hsv * _gelu_grad(gate_v)).astype(CD)

        row = lax.broadcasted_iota(jnp.int32, a.shape, 0)
        b_scr[...] = jnp.where(row == tb - 1, 1.0, pltpu.roll(a, tb - 1, axis=0))
        u_scr[...] = dyv * _gelu(gate_v)
        carry = carry_ref[...]
        for gi in reversed(range(groups)):
            rows = pl.ds(gi * SUBLANES, SUBLANES)
            pb, gl = _scan8_bwd(b_scr[rows, :], u_scr[rows, :])
            dh_scr[rows, :] = gl + pb * carry
            carry = dh_scr[pl.ds(gi * SUBLANES, 1), :]
        dh = dh_scr[...]
        carry_ref[...] = carry * jnp.sum(jnp.where(row == 0, a, 0.0), axis=0, keepdims=True)

        hprev_tail = jnp.where(first_block, 0.0, hprev_ref[...])
        h_prev = _shift_down(hsv, hprev_tail, 1)
        da = dh * h_prev
        ixc = i * xc
        dmult = dh * ixc
        di = dh * mult * xc
        dxc = dh * mult * i
        a2 = a * a
        dlog_a = da * a - dmult * a2 / mult
        dpre_r = (dlog_a * (RG_C * ls)) * r * (1.0 - r)
        dpre_i = di * i * (1.0 - i)
        dlam_ref[...] += jnp.sum(dlog_a * r, axis=0, keepdims=True) * (RG_C * _sigmoid(-lam_v))
        dbr_ref[...] += jnp.sum(dpre_r, axis=0, keepdims=True)
        dbi_ref[...] += jnp.sum(dpre_i, axis=0, keepdims=True)
        xcd = xc.astype(CD)
        dprc = dpre_r.astype(CD)
        dpic = dpre_i.astype(CD)
        tn_dims = (((0,), (0,)), ((), ()))
        dwr_ref[0] += lax.dot_general(xcd, dprc, tn_dims, preferred_element_type=F32)
        dwi_ref[0] += lax.dot_general(xcd, dpic, tn_dims, preferred_element_type=F32)
        dxc = dxc + jnp.dot(dprc, wrt_ref[0], preferred_element_type=F32) + jnp.dot(dpic, wit_ref[0],
                                                                                    preferred_element_type=F32)
        dcb_ref[...] += jnp.sum(dxc, axis=0, keepdims=True)
        for k in range(CONV_W):
            dcw_ref[pl.ds(k, 1), :] += jnp.sum(dxc * taps[k], axis=0, keepdims=True)
        head = head_ref[...]
        dxb = jnp.zeros_like(dxc)
        for sft in range(CONV_W):
            dxb = dxb + cw_ref[pl.ds(CONV_W - 1 - sft, 1), :] * _shift_up(dxc, head, sft)
        dx_ref[...] = dxb.astype(CD)
        head_ref[...] = dxc[0:SUBLANES, :]

    blk = pl.BlockSpec((tb, cbw), lambda n, t: (nt - 1 - t, n))
    tail = pl.BlockSpec((SUBLANES, cbw), lambda n, t: (jnp.maximum((nt - 1 - t) * groups - 1, 0), n))
    vec = pl.BlockSpec((1, cbw), lambda n, t: (0, n))
    cwb = pl.BlockSpec((CONV_W, cbw), lambda n, t: (0, n))
    wblk = pl.BlockSpec((1, cbw, cbw), lambda n, t: (n, 0, 0))
    vshape = jax.ShapeDtypeStruct((1, c), F32)
    wshape = jax.ShapeDtypeStruct((RG_BLOCKS, cbw, cbw), F32)
    return pl.pallas_call(
        body, name=name,
        out_shape=(jax.ShapeDtypeStruct((s, c), CD), jax.ShapeDtypeStruct((s, c), CD),
                   jax.ShapeDtypeStruct((CONV_W, c), F32), vshape, vshape, vshape, vshape, wshape, wshape),
        grid=(RG_BLOCKS, nt),
        in_specs=[blk, blk, blk, tail, blk, tail, cwb, vec, wblk, wblk, wblk, wblk, vec, vec, vec],
        out_specs=(blk, blk, cwb, vec, vec, vec, vec, wblk, wblk),
        scratch_shapes=[pltpu.VMEM((1, cbw), F32), pltpu.VMEM((SUBLANES, cbw), F32),
                        pltpu.VMEM((tb, cbw), F32), pltpu.VMEM((tb, cbw), F32), pltpu.VMEM((tb, cbw), F32)],
        compiler_params=_params("parallel", "arbitrary"),
    )(dy, gate_br, x_br, x_br, hs, hs, cw, cb, wr, wi, wrt, wit, br, bi, lam)


ATT_BLOCK = 256
ATT_SCALE = 1.0 / math.sqrt(SB_HEAD_DIM)
N_PAIRS = SB_HEADS * SB_HEAD_DIM // LANES
NT_DIMS = (((1,), (1,)), ((), ()))
TN_DIMS = (((0,), (0,)), ((), ()))


LOG2E = 1.4426950408889634


def _neg_abs(x):
    bits = lax.bitcast_convert_type(x, jnp.uint32) | jnp.uint32(0x80000000)
    return lax.bitcast_convert_type(bits, F32)


def _qk(qx, kb):
    return lax.dot_general(qx, kb, NT_DIMS, preferred_element_type=F32)


def _sb_logits(qk, valid):
    z2 = qk * (ATT_SCALE * LOG2E)
    lb2 = jnp.minimum(z2, 0.0) - jnp.log2(1.0 + jnp.exp2(_neg_abs(z2)))
    l2 = lb2 - z2
    if valid is not None:
        l2 = jnp.where(valid, l2, 0.0)
    return lb2, l2


def _hi_lo(x):
    hi = x.astype(CD)
    lo = (x - hi.astype(F32)).astype(CD)
    return jnp.concatenate([hi, lo], axis=1)


def _tri2(strict):
    r = lax.broadcasted_iota(jnp.int32, (ATT_BLOCK, ATT_BLOCK), 0)
    c = lax.broadcasted_iota(jnp.int32, (ATT_BLOCK, ATT_BLOCK), 1)
    m = (r > c if strict else r >= c).astype(CD)
    return jnp.concatenate([m, m], axis=0)


def _attn_fwd(qkv, name):
    _, s, _ = qkv.shape
    nblk = s // ATT_BLOCK
    t = ATT_BLOCK

    def body(q_ref, k_ref, v_ref, o_ref, qk_scr, w_scr):
        i = pl.program_id(1)
        lane = lax.broadcasted_iota(jnp.int32, (1, LANES), 1)
        head_masks = (lane < SB_HEAD_DIM, lane >= SB_HEAD_DIM)
        q = q_ref[0]
        qs = [jnp.where(m, q, jnp.zeros_like(q)) for m in head_masks]
        tri = _tri2(True)
        rr = lax.broadcasted_iota(jnp.int32, (t, t), 0)
        cc = lax.broadcasted_iota(jnp.int32, (t, t), 1)
        diag_valid = cc < rr

        def rows_of(j):
            return pl.ds(pl.multiple_of(j * t, t), t)

        def start_logits(j):
            kb = k_ref[0, rows_of(j), :]
            for hd in range(2):
                qk_scr[hd] = _qk(qs[hd], kb)

        def weights(run, valid):
            new_run = []
            for hd in range(2):
                lb2, l2 = _sb_logits(qk_scr[hd], valid)
                w = jnp.exp2(lb2 + (run[hd] + jnp.dot(_hi_lo(l2), tri, preferred_element_type=F32)))
                if valid is not None:
                    w = jnp.where(valid, w, 0.0)
                w_scr[:, hd * t:(hd + 1) * t] = w.astype(CD)
                new_run.append(run[hd] + jnp.sum(l2, axis=1, keepdims=True))
            return tuple(new_run)

        def apply_weights(j):
            vb = v_ref[0, rows_of(j), :]
            vcat = jnp.concatenate([jnp.where(m, vb, jnp.zeros_like(vb)) for m in head_masks], axis=0)
            return jnp.dot(w_scr[...], vcat, preferred_element_type=F32)

        zero = jnp.zeros((t, 1), F32)
        start_logits(i)
        run = weights((zero, zero), diag_valid)
        start_logits(jnp.maximum(i - 1, 0))

        def step(jj, carry):
            run, oacc = carry
            b = i - 1 - jj
            oacc = oacc + apply_weights(b + 1)
            run = weights(run, None)
            start_logits(jnp.maximum(b - 1, 0))
            return run, oacc

        run, oacc = lax.fori_loop(0, i, step, (run, jnp.zeros((t, LANES), F32)))
        o_ref[0] = oacc + apply_weights(0)

    return pl.pallas_call(
        body, name=name, out_shape=jax.ShapeDtypeStruct((N_PAIRS, s, LANES), F32), grid=(N_PAIRS, nblk),
        in_specs=[pl.BlockSpec((1, t, LANES), lambda p, i: (p, i, 0)),
                  pl.BlockSpec((1, s, LANES), lambda p, i: (N_PAIRS + p, 0, 0)),
                  pl.BlockSpec((1, s, LANES), lambda p, i: (2 * N_PAIRS + p, 0, 0))],
        out_specs=pl.BlockSpec((1, t, LANES), lambda p, i: (p, i, 0)),
        scratch_shapes=[pltpu.VMEM((2, t, t), F32), pltpu.VMEM((t, 2 * t), CD)],
        compiler_params=_params("parallel", "arbitrary"),
    )(qkv, qkv, qkv)


def _attn_bwd(qkv, o, do, name):
    _, s, _ = qkv.shape
    nblk = s // ATT_BLOCK
    t = ATT_BLOCK

    def body(q_ref, k_ref, v_ref, o_ref, do_ref, dq_ref, dk_ref, dv_ref, qk_scr, dw_scr, w_scr, dz_scr):
        i = pl.program_id(1)

        @pl.when(i == 0)
        def _():
            dk_ref[...] = jnp.zeros_like(dk_ref)
            dv_ref[...] = jnp.zeros_like(dv_ref)

        lane = lax.broadcasted_iota(jnp.int32, (1, LANES), 1)
        head_masks = (lane < SB_HEAD_DIM, lane >= SB_HEAD_DIM)
        q = q_ref[0]
        dov = do_ref[0]
        ov = o_ref[0]
        qs = [jnp.where(m, q, jnp.zeros_like(q)) for m in head_masks]
        q_scaled = jnp.concatenate([qx * ATT_SCALE for qx in qs], axis=0)
        docs = [jnp.where(m, dov, 0.0).astype(CD) for m in head_masks]
        docat = jnp.concatenate(docs, axis=0)
        totals = [jnp.sum(d.astype(F32) * ov, axis=1, keepdims=True) for d in docs]
        tri = _tri2(True)
        tri_incl = _tri2(False)
        rr = lax.broadcasted_iota(jnp.int32, (t, t), 0)
        cc = lax.broadcasted_iota(jnp.int32, (t, t), 1)
        diag_valid = cc < rr

        def rows_of(j):
            return pl.ds(pl.multiple_of(j * t, t), t)

        def start_products(j):
            kb = k_ref[0, rows_of(j), :]
            vb = v_ref[0, rows_of(j), :]
            for hd in range(2):
                qk_scr[hd] = _qk(qs[hd], kb)
                dw_scr[hd] = lax.dot_general(docs[hd], vb, NT_DIMS, preferred_element_type=F32)

        def logit_grads(run, erun, valid):
            new_run, new_erun = [], []
            for hd in range(2):
                lb2, l2 = _sb_logits(qk_scr[hd], valid)
                w = jnp.exp2(lb2 + (run[hd] + jnp.dot(_hi_lo(l2), tri, preferred_element_type=F32)))
                if valid is not None:
                    w = jnp.where(valid, w, 0.0)
                wc = w.astype(CD)
                w_scr[hd * t:(hd + 1) * t, :] = wc
                e = dw_scr[hd] * wc.astype(F32)
                prefix = (totals[hd] - erun[hd]) - jnp.dot(_hi_lo(e), tri_incl, preferred_element_type=F32)
                dz = e - jnp.exp2(lb2) * (e + prefix)
                if valid is not None:
                    dz = jnp.where(valid, dz, 0.0)
                dz_scr[hd * t:(hd + 1) * t, :] = dz.astype(CD)
                new_run.append(run[hd] + jnp.sum(l2, axis=1, keepdims=True))
                new_erun.append(erun[hd] + jnp.sum(e, axis=1, keepdims=True))
            return tuple(new_run), tuple(new_erun)

        def apply_grads(j):
            kb = k_ref[0, rows_of(j), :]
            kcat = jnp.concatenate([jnp.where(m, kb, jnp.zeros_like(kb)) for m in head_masks], axis=0)
            dz2 = dz_scr[...]
            dk_ref[0, rows_of(j), :] += lax.dot_general(dz2, q_scaled, TN_DIMS, preferred_element_type=F32)
            dv_ref[0, rows_of(j), :] += lax.dot_general(w_scr[...], docat, TN_DIMS, preferred_element_type=F32)
            return jnp.dot(jnp.concatenate([dz2[:t], dz2[t:]], axis=1), kcat, preferred_element_type=F32)

        zero = jnp.zeros((t, 1), F32)
        start_products(i)
        run, erun = logit_grads((zero, zero), (zero, zero), diag_valid)
        start_products(jnp.maximum(i - 1, 0))

        def step(jj, carry):
            run, erun, dqacc = carry
            b = i - 1 - jj
            dqacc = dqacc + apply_grads(b + 1)
            run, erun = logit_grads(run, erun, None)
            start_products(jnp.maximum(b - 1, 0))
            return run, erun, dqacc

        run, erun, dqacc = lax.fori_loop(0, i, step, (run, erun, jnp.zeros((t, LANES), F32)))
        dq_ref[0] = (dqacc + apply_grads(0)) * ATT_SCALE

    qblk = pl.BlockSpec((1, t, LANES), lambda p, i: (p, i, 0))
    full = pl.BlockSpec((1, s, LANES), lambda p, i: (p, 0, 0))
    shape = jax.ShapeDtypeStruct((N_PAIRS, s, LANES), F32)
    return pl.pallas_call(
        body, name=name, out_shape=(shape, shape, shape), grid=(N_PAIRS, nblk),
        in_specs=[qblk,
                  pl.BlockSpec((1, s, LANES), lambda p, i: (N_PAIRS + p, 0, 0)),
                  pl.BlockSpec((1, s, LANES), lambda p, i: (2 * N_PAIRS + p, 0, 0)),
                  qblk, qblk],
        out_specs=(qblk, full, full),
        scratch_shapes=[pltpu.VMEM((2, t, t), F32), pltpu.VMEM((2, t, t), F32),
                        pltpu.VMEM((2 * t, t), CD), pltpu.VMEM((2 * t, t), CD)],
        compiler_params=_params("parallel", "arbitrary"),
    )(qkv, qkv, qkv, o, do)


ADAM_COLS = 1024


def _adamw(w, g, m, v, name):
    shape = w.shape
    rows = w.size // ADAM_COLS
    tr = _pick(rows, (512, 256, 128, 64, 32, 16, 8))

    def body(w_ref, g_ref, m_ref, v_ref, d_ref, nm_ref, nv_ref):
        gv = g_ref[...]
        nm = ADAM_B1 * m_ref[...] + (1.0 - ADAM_B1) * gv
        nv = ADAM_B2 * v_ref[...] + (1.0 - ADAM_B2) * (gv * gv)
        m_hat = nm / (1.0 - ADAM_B1 ** ADAM_STEP)
        v_hat = nv / (1.0 - ADAM_B2 ** ADAM_STEP)
        d_ref[...] = -ADAM_LR * (m_hat / (jnp.sqrt(v_hat) + ADAM_EPS) + ADAM_WD * w_ref[...])
        nm_ref[...] = nm
        nv_ref[...] = nv

    blk = pl.BlockSpec((tr, ADAM_COLS), lambda i: (i, 0))
    out = jax.ShapeDtypeStruct((rows, ADAM_COLS), F32)
    d, nm, nv = pl.pallas_call(
        body, name=name, out_shape=(out, out, out), grid=(rows // tr,),
        in_specs=[blk, blk, blk, blk], out_specs=(blk, blk, blk), compiler_params=_params("parallel"),
    )(*[a.reshape(rows, ADAM_COLS) for a in (w, g, m, v)])
    return d.reshape(shape), nm.reshape(shape), nv.reshape(shape)


HBM = pl.BlockSpec(memory_space=pltpu.HBM)


def _coords():
    return lax.axis_index("x"), lax.axis_index("y"), lax.axis_index("c")


def _other_chips(x, y):
    return [(1 - x, y), (x, 1 - y), (1 - x, 1 - y)]


def _allgather_chips(shard, name):
    r, cols = shard.shape
    half = r // 2

    def body(src_ref, out_ref, send_sems, recv_sems, local_sem):
        x, y, c = _coords()
        sibling = (x, y, 1 - c)
        chips = _other_chips(x, y)

        def rows(px, py, h):
            return out_ref.at[2 * px + py, pl.ds(h * half, half), :]

        def copy(k, block, to, src=None):
            return pltpu.make_async_remote_copy(
                src_ref=rows(*block) if src is None else src, dst_ref=rows(*block),
                send_sem=send_sems.at[k], recv_sem=recv_sems.at[k], device_id=to, device_id_type=MESH)

        mine = pltpu.make_async_copy(src_ref, out_ref.at[2 * x + y], local_sem)
        mine.start()
        my_half = src_ref.at[pl.ds(c * half, half), :]
        first = [copy(j, (x, y, c), (*chip, c), src=my_half) for j, chip in enumerate(chips)]
        for cp in first:
            cp.start()
        passed = [copy(3 + j, (*chip, c), sibling) for j, chip in enumerate(chips)]
        for j, chip in enumerate(chips):
            copy(j, (*chip, c), (x, y, c)).wait_recv()
            passed[j].start()
        for j, chip in enumerate(chips):
            copy(3 + j, (*chip, 1 - c), (x, y, c)).wait_recv()
        for cp in first + passed:
            cp.wait_send()
        mine.wait()

    return pl.pallas_call(
        body, name=name, out_shape=jax.ShapeDtypeStruct((N_CHIPS, r, cols), shard.dtype),
        in_specs=[HBM], out_specs=HBM,
        scratch_shapes=[pltpu.SemaphoreType.DMA((6,)), pltpu.SemaphoreType.DMA((6,)), pltpu.SemaphoreType.DMA],
    )(shard)


def _exchange_sibling_halves(g, name):
    n, r, cols = g.shape
    half = r // 2

    def body(g_ref, out_ref, send_sem, recv_sem):
        x, y, c = _coords()
        cp = pltpu.make_async_remote_copy(
            src_ref=g_ref.at[:, pl.ds((1 - c) * half, half), :], dst_ref=out_ref,
            send_sem=send_sem, recv_sem=recv_sem, device_id=(x, y, 1 - c), device_id_type=MESH)
        cp.start()
        cp.wait()

    return pl.pallas_call(
        body, name=name, out_shape=jax.ShapeDtypeStruct((n, half, cols), g.dtype),
        in_specs=[HBM], out_specs=HBM,
        scratch_shapes=[pltpu.SemaphoreType.DMA, pltpu.SemaphoreType.DMA],
    )(g)


def _scatter_to_chips(p, name):
    n, h, cols = p.shape

    def body(p_ref, out_ref, send_sems, recv_sems, local_sem):
        x, y, c = _coords()
        me = 2 * x + y
        mine = pltpu.make_async_copy(p_ref.at[me], out_ref.at[me], local_sem)
        mine.start()
        sends = []
        for j, (px, py) in enumerate(_other_chips(x, y)):
            sends.append(pltpu.make_async_remote_copy(
                src_ref=p_ref.at[2 * px + py], dst_ref=out_ref.at[me],
                send_sem=send_sems.at[j], recv_sem=recv_sems.at[j], device_id=(px, py, c), device_id_type=MESH))
        for cp in sends:
            cp.start()
        for j, (px, py) in enumerate(_other_chips(x, y)):
            pltpu.make_async_remote_copy(
                src_ref=p_ref.at[me], dst_ref=out_ref.at[2 * px + py],
                send_sem=send_sems.at[j], recv_sem=recv_sems.at[j], device_id=(px, py, c),
                device_id_type=MESH).wait_recv()
        for cp in sends:
            cp.wait_send()
        mine.wait()

    return pl.pallas_call(
        body, name=name, out_shape=jax.ShapeDtypeStruct((n, h, cols), p.dtype),
        in_specs=[HBM], out_specs=HBM,
        scratch_shapes=[pltpu.SemaphoreType.DMA((3,)), pltpu.SemaphoreType.DMA((3,)), pltpu.SemaphoreType.DMA],
    )(p)


def _swap_with_sibling(v, name):
    def body(v_ref, out_ref, send_sem, recv_sem):
        x, y, c = _coords()
        cp = pltpu.make_async_remote_copy(
            src_ref=v_ref, dst_ref=out_ref, send_sem=send_sem, recv_sem=recv_sem,
            device_id=(x, y, 1 - c), device_id_type=MESH)
        cp.start()
        cp.wait()

    return pl.pallas_call(
        body, name=name, out_shape=jax.ShapeDtypeStruct(v.shape, v.dtype),
        in_specs=[HBM], out_specs=HBM,
        scratch_shapes=[pltpu.SemaphoreType.DMA, pltpu.SemaphoreType.DMA],
    )(v)


def _allreduce_small(v, name):
    r, cols = v.shape

    def body(v_ref, out_ref, buf_ref, send_sems, recv_sems):
        x, y, c = _coords()
        me = 4 * x + 2 * y + c
        buf_ref[me] = v_ref[...]
        sends = []
        for k in range(1, N_DEV):
            px = 1 - x if k & 4 else x
            py = 1 - y if k & 2 else y
            pc = 1 - c if k & 1 else c
            sends.append(pltpu.make_async_remote_copy(
                src_ref=v_ref, dst_ref=buf_ref.at[me], send_sem=send_sems.at[k - 1], recv_sem=recv_sems.at[k - 1],
                device_id=(px, py, pc), device_id_type=MESH))
        for cp in sends:
            cp.start()
        for cp in sends:
            cp.wait()
        acc = buf_ref[0]
        for d in range(1, N_DEV):
            acc = acc + buf_ref[d]
        out_ref[...] = acc

    return pl.pallas_call(
        body, name=name, out_shape=jax.ShapeDtypeStruct((r, cols), F32),
        in_specs=[pl.BlockSpec(memory_space=pltpu.VMEM)], out_specs=pl.BlockSpec(memory_space=pltpu.VMEM),
        scratch_shapes=[pltpu.VMEM((N_DEV, r, cols), F32), pltpu.SemaphoreType.DMA((N_DEV - 1,)),
                        pltpu.SemaphoreType.DMA((N_DEV - 1,))],
    )(v)


def _add2(a, b, name):
    n, r, cols = a.shape
    tr = _pick(r, (512, 256, 128))

    def body(a_ref, b_ref, o_ref):
        o_ref[...] = a_ref[...] + b_ref[...]

    blk = pl.BlockSpec((1, tr, cols), lambda s, i: (s, i, 0))
    return pl.pallas_call(
        body, name=name, out_shape=jax.ShapeDtypeStruct(a.shape, a.dtype), grid=(n, r // tr),
        in_specs=[blk, blk], out_specs=blk, compiler_params=_params("parallel", "parallel"),
    )(a, b)


def _sum_slots(p, name):
    n, r, cols = p.shape
    tr = _pick(r, (512, 256, 128))

    def body(p_ref, o_ref):
        o_ref[...] = ((p_ref[0] + p_ref[1]) + p_ref[2]) + p_ref[3]

    return pl.pallas_call(
        body, name=name, out_shape=jax.ShapeDtypeStruct((r, cols), p.dtype), grid=(r // tr,),
        in_specs=[pl.BlockSpec((n, tr, cols), lambda i: (0, i, 0))],
        out_specs=pl.BlockSpec((tr, cols), lambda i: (i, 0)), compiler_params=_params("parallel"),
    )(p)


PACK_COLS = 1024


def _pack_shards(parts):
    return jnp.concatenate([p.reshape(-1, PACK_COLS) for p in parts], axis=0)


def _unpack_shards(buf, shapes):
    out, row = [], 0
    for shp in shapes:
        nrows = math.prod(shp) // PACK_COLS
        out.append(buf[..., row:row + nrows, :].reshape(buf.shape[:-2] + tuple(shp)))
        row += nrows
    return out


def _local_step(x, target, w):
    t = lambda a: a.T
    g = {}
    h0 = _rms_fwd(x, w["norm_mix_g"][0], "rms_mix0")
    w_in_g, w_in_x = w["a_w_in"][:, :D_RNN], w["a_w_in"][:, D_RNN:]
    gate_br = _matmul([(h0, w_in_g)], F32, "mm_a_gate")
    x_br = _matmul([(h0, w_in_x)], F32, "mm_a_xbr")
    y_a, hs = _rglru_fwd(gate_br, x_br, w["a_conv_w"], w["a_conv_b"], w["a_w_r"], w["a_w_i"], w["a_b_r"],
                         w["a_b_i"], w["a_lambda"], "rglru_fwd")
    x1 = _matmul([(y_a, w["a_w_out"])], F32, "mm_a_out", addend=x)
    h1 = _rms_fwd(x1, w["norm_ffn_g"][0], "rms_ffn0")
    fg0 = _matmul([(h1, w["ffn_w_gate"][0])], F32, "mm_f0_gate")
    fu0 = _matmul([(h1, w["ffn_w_up"][0])], F32, "mm_f0_up")
    act0 = _swiglu_fwd(fg0, fu0, "swiglu0_fwd")
    x2 = _matmul([(act0, w["ffn_w_down"][0])], F32, "mm_f0_down", addend=x1)
    h2 = _rms_fwd(x2, w["norm_mix_g"][1], "rms_mix1")
    qkv = _matmul([(h2, w["b_w_qkv"])], CD, "mm_b_qkv", out_lbm=True, tn=1024)
    o = _attn_fwd(qkv, "attn_fwd")
    x3 = _matmul([(o, w["b_w_out"])], F32, "mm_b_out", a_lbm=True, addend=x2)
    h3 = _rms_fwd(x3, w["norm_ffn_g"][1], "rms_ffn1")
    fg1 = _matmul([(h3, w["ffn_w_gate"][1])], F32, "mm_f1_gate")
    fu1 = _matmul([(h3, w["ffn_w_up"][1])], F32, "mm_f1_up")
    act1 = _swiglu_fwd(fg1, fu1, "swiglu1_fwd")
    x4 = _matmul([(act1, w["ffn_w_down"][1])], F32, "mm_f1_down", addend=x3)
    loss, dx4, g["final_g"] = _loss_head(x4, w["final_g"], target, "loss_head")

    def ffn_bwd(dx_out, h, x_in, fg, fu, act, layer, tag):
        dxc = dx_out.astype(CD)
        dact = _matmul([(dxc, t(w["ffn_w_down"][layer]))], F32, "mm_" + tag + "_dact")
        dwd = _matmul([(act, dxc)], F32, "mm_" + tag + "_dwd", trans_a=True)
        dg, du = _swiglu_bwd(dact, fg, fu, "swiglu" + tag + "_bwd")
        dwg = _matmul([(h, dg)], F32, "mm_" + tag + "_dwg", trans_a=True)
        dwu = _matmul([(h, du)], F32, "mm_" + tag + "_dwu", trans_a=True)
        dh = _matmul([(dg, t(w["ffn_w_gate"][layer])), (du, t(w["ffn_w_up"][layer]))], F32, "mm_" + tag + "_dh")
        dx_in, dgain = _rms_bwd(dh, x_in, w["norm_ffn_g"][layer], dx_out, "rms_ffn" + tag + "_bwd")
        return dx_in, dgain, dwg, dwu, dwd

    dx3, dgf1, dwg1, dwu1, dwd1 = ffn_bwd(dx4, h3, x3, fg1, fu1, act1, 1, "f1")
    dx3c = dx3.astype(CD)
    do = _matmul([(dx3c, t(w["b_w_out"]))], F32, "mm_b_do", out_lbm=True, tn=1024)
    g["b_w_out"] = _matmul([(o, dx3c)], F32, "mm_b_dwout", trans_a=True, a_lbm=True)
    dq, dk, dv = _attn_bwd(qkv, o, do, "attn_bwd")
    wq_t = t(w["b_w_qkv"])
    parts = (dq, dk, dv)
    g["b_w_qkv"] = jnp.concatenate(
        [_matmul([(h2, p)], F32, "mm_b_dwqkv%d" % n, trans_a=True, b_lbm=True) for n, p in enumerate(parts)], axis=1)
    dh2 = _matmul([(p, wq_t[n * D_MODEL:(n + 1) * D_MODEL]) for n, p in enumerate(parts)], F32, "mm_b_dh",
                  a_lbm=True)
    dx2, dgm1 = _rms_bwd(dh2, x2, w["norm_mix_g"][1], dx3, "rms_mix1_bwd")
    dx1, dgf0, dwg0, dwu0, dwd0 = ffn_bwd(dx2, h1, x1, fg0, fu0, act0, 0, "f0")
    dx1c = dx1.astype(CD)
    dy_a = _matmul([(dx1c, t(w["a_w_out"]))], F32, "mm_a_dy")
    g["a_w_out"] = _matmul([(y_a, dx1c)], F32, "mm_a_dwout", trans_a=True)
    wrt = jnp.swapaxes(w["a_w_r"], 1, 2)
    wit = jnp.swapaxes(w["a_w_i"], 1, 2)
    (dgate, dxbr, g["a_conv_w"], g["a_conv_b"], g["a_b_r"], g["a_b_i"], g["a_lambda"], g["a_w_r"],
     g["a_w_i"]) = _rglru_bwd(dy_a, gate_br, x_br, hs, w["a_conv_w"], w["a_conv_b"], w["a_w_r"], w["a_w_i"], wrt, wit,
                              w["a_b_r"], w["a_b_i"], w["a_lambda"], "rglru_bwd")
    g["a_w_in"] = jnp.concatenate([_matmul([(h0, dgate)], F32, "mm_a_dwin_g", trans_a=True),
                                   _matmul([(h0, dxbr)], F32, "mm_a_dwin_x", trans_a=True)], axis=1)
    dh0 = _matmul([(dgate, t(w_in_g)), (dxbr, t(w_in_x))], F32, "mm_a_dh")
    dx0, dgm0 = _rms_bwd(dh0, x, w["norm_mix_g"][0], dx1, "rms_mix0_bwd")
    g["norm_mix_g"] = jnp.concatenate([dgm0, dgm1], axis=0)
    g["norm_ffn_g"] = jnp.concatenate([dgf0, dgf1], axis=0)
    g["ffn_w_gate"] = jnp.stack([dwg0, dwg1])
    g["ffn_w_up"] = jnp.stack([dwu0, dwu1])
    g["ffn_w_down"] = jnp.stack([dwd0, dwd1])
    return loss, dx0, g


WEIGHTS = ["norm_mix_g", "norm_ffn_g", "a_w_in", "a_conv_w", "a_conv_b", "a_w_r", "a_b_r", "a_w_i", "a_b_i",
           "a_lambda", "a_w_out", "b_w_qkv", "b_w_out", "ffn_w_gate", "ffn_w_up", "ffn_w_down", "final_g"]
BIG = [("a_w_in", 2), ("a_w_r", 2), ("a_w_i", 2), ("a_w_out", 1), ("b_w_qkv", 2), ("b_w_out", 1),
       ("ffn_w_gate", 2), ("ffn_w_up", 2), ("ffn_w_down", 1)]
SMALL = ["norm_mix_g", "norm_ffn_g", "a_conv_w", "a_conv_b", "a_b_r", "a_b_i", "a_lambda", "final_g"]


def _join_chips(stack, axis):
    return jnp.concatenate([stack[s] for s in range(N_CHIPS)], axis=axis)


def _split_chips(full, axis):
    return jnp.stack(jnp.split(full, N_CHIPS, axis=axis))


def _step(x, target, weights, moments_m, moments_v):
    chip = 2 * lax.axis_index("x") + lax.axis_index("y")
    core = lax.axis_index("c")
    shard_shapes = [weights[n].shape for n, _ in BIG]
    packed = _pack_shards([weights[n].astype(CD) for n, _ in BIG])
    gathered = _allgather_chips(packed, "allgather_weights")
    full = {}
    for (n, axis), stack in zip(BIG, _unpack_shards(gathered, shard_shapes)):
        joined = _join_chips(stack, axis)
        full[n] = joined[0] if joined.shape[0] == 1 else joined
    cw_rows = jnp.zeros((N_CHIPS, CONV_W, RG_BW), F32)
    cw_rows = lax.dynamic_update_slice(cw_rows, jnp.where(core == 0, weights["a_conv_w"], 0.0), (chip, 0, 0))
    cw_all = _allreduce_small(cw_rows.reshape(-1, LANES), "allgather_conv_w").reshape(N_CHIPS, CONV_W, RG_BW)
    full["a_conv_w"] = jnp.concatenate([cw_all[s] for s in range(N_CHIPS)], axis=1)
    for n in ("norm_mix_g", "norm_ffn_g", "final_g"):
        full[n] = weights[n]
    for n in ("a_conv_b", "a_b_r", "a_b_i", "a_lambda"):
        full[n] = weights[n]
    loss, dx, grads = _local_step(x[0], target[0], full)
    small_parts = [grads[n].reshape(-1) for n in SMALL] + [loss.reshape(-1)]
    sizes = [p.shape[0] for p in small_parts]
    small = _allreduce_small(jnp.concatenate(small_parts).reshape(-1, LANES), "allreduce_small").reshape(-1)
    red, pos = {}, 0
    for n, sz in zip(SMALL + ["loss"], sizes):
        red[n] = small[pos:pos + sz]
        pos += sz
    loss_out = red["loss"][0]
    g_out = {}
    for n in SMALL:
        if n == "a_conv_w":
            g_out[n] = lax.dynamic_slice(red[n].reshape(CONV_W, D_RNN), (0, chip * RG_BW), (CONV_W, RG_BW)).reshape(
                weights[n].shape)
        else:
            g_out[n] = red[n].reshape(weights[n].shape)
    stacks = []
    for n, axis in BIG:
        gfull = grads[n].reshape((1,) + grads[n].shape) if grads[n].ndim == len(weights[n].shape) - 1 else grads[n]
        stacks.append(_split_chips(gfull, axis).reshape(N_CHIPS, -1, PACK_COLS))
    gbuf = jnp.concatenate(stacks, axis=1)
    half = gbuf.shape[1] // 2
    from_sibling = _exchange_sibling_halves(gbuf, "rs_sibling")
    mine = lax.dynamic_slice_in_dim(gbuf, core * half, half, axis=1)
    chip_partial = _add2(mine, from_sibling, "rs_add_sibling")
    from_chips = _scatter_to_chips(chip_partial, "rs_chips")
    reduced_half = _sum_slots(from_chips, "rs_sum_chips")
    other_half = _swap_with_sibling(reduced_half, "rs_share")
    lo = jnp.where(core == 0, reduced_half, other_half)
    hi = jnp.where(core == 0, other_half, reduced_half)
    reduced = jnp.concatenate([lo, hi], axis=0)
    for (n, _), gsh in zip(BIG, _unpack_shards(reduced, shard_shapes)):
        g_out[n] = gsh
    outs_g, outs_d, outs_m, outs_v = [], [], [], []
    for n in WEIGHTS:
        d, nm, nv = _adamw(weights[n], g_out[n], moments_m[n], moments_v[n], "adamw_" + n)
        outs_g.append(g_out[n])
        outs_d.append(d)
        outs_m.append(nm)
        outs_v.append(nv)
    return (loss_out, dx[None], *outs_g, *outs_d, *outs_m, *outs_v)


def kernel(x, norm_mix_g, norm_ffn_g, a_w_in, a_conv_w, a_conv_b, a_w_r, a_b_r, a_w_i, a_b_i, a_lambda, a_w_out, b_w_qkv, b_w_out, ffn_w_gate, ffn_w_up, ffn_w_down, final_g, loss_target, m_norm_mix_g, m_norm_ffn_g, m_a_w_in, m_a_conv_w, m_a_conv_b, m_a_w_r, m_a_b_r, m_a_w_i, m_a_b_i, m_a_lambda, m_a_w_out, m_b_w_qkv, m_b_w_out, m_ffn_w_gate, m_ffn_w_up, m_ffn_w_down, m_final_g, v_norm_mix_g, v_norm_ffn_g, v_a_w_in, v_a_conv_w, v_a_conv_b, v_a_w_r, v_a_b_r, v_a_w_i, v_a_b_i, v_a_lambda, v_a_w_out, v_b_w_qkv, v_b_w_out, v_ffn_w_gate, v_ffn_w_up, v_ffn_w_down, v_final_g):
    ws = [norm_mix_g, norm_ffn_g, a_w_in, a_conv_w, a_conv_b, a_w_r, a_b_r, a_w_i, a_b_i, a_lambda, a_w_out, b_w_qkv,
          b_w_out, ffn_w_gate, ffn_w_up, ffn_w_down, final_g]
    ms = [m_norm_mix_g, m_norm_ffn_g, m_a_w_in, m_a_conv_w, m_a_conv_b, m_a_w_r, m_a_b_r, m_a_w_i, m_a_b_i, m_a_lambda,
          m_a_w_out, m_b_w_qkv, m_b_w_out, m_ffn_w_gate, m_ffn_w_up, m_ffn_w_down, m_final_g]
    vs = [v_norm_mix_g, v_norm_ffn_g, v_a_w_in, v_a_conv_w, v_a_conv_b, v_a_w_r, v_a_b_r, v_a_w_i, v_a_b_i, v_a_lambda,
          v_a_w_out, v_b_w_qkv, v_b_w_out, v_ffn_w_gate, v_ffn_w_up, v_ffn_w_down, v_final_g]
    return _step(x, loss_target, dict(zip(WEIGHTS, ws)), dict(zip(WEIGHTS, ms)), dict(zip(WEIGHTS, vs)))
```

```python
import functools
import math

import jax
import jax.numpy as jnp
from jax import lax
from jax.experimental import pallas as pl
from jax.experimental.pallas import tpu as pltpu

F32 = jnp.float32
CD = jnp.bfloat16

D_MODEL = 1024
D_RNN = 1024
RG_BLOCKS = 4
RG_BW = 256
CONV_W = 4
RG_C = 8.0
SB_HEADS = 16
SB_HEAD_DIM = 64
D_FF = 2816
RMS_EPS = 1e-6
N_CHIPS = 4
N_DEV = 8

ADAM_LR = 0.001
ADAM_B1 = 0.9
ADAM_B2 = 0.999
ADAM_EPS = 1e-08
ADAM_WD = 0.01
ADAM_STEP = 10

LANES = 128
VMEM_LIMIT = 56 * 1024 * 1024
MESH = pl.DeviceIdType.MESH


def _params(*sem):
    return pltpu.CompilerParams(dimension_semantics=sem, vmem_limit_bytes=VMEM_LIMIT)


def _pick(n, prefs):
    for p in prefs:
        if n % p == 0:
            return p
    return n


def _matmul(pairs, out_dtype, name, *, trans_a=False, a_lbm=False, b_lbm=False, out_lbm=False, addend=None,
            tm=512, tn=None, tk=None):
    a0, b0 = pairs[0]
    if trans_a:
        kdim = a0.shape[1] if a_lbm else a0.shape[0]
        m = a0.shape[0] * LANES if a_lbm else a0.shape[1]
    else:
        m = a0.shape[1] if a_lbm else a0.shape[0]
        kdim = a0.shape[0] * LANES if a_lbm else a0.shape[1]
    n = b0.shape[0] * LANES if b_lbm else b0.shape[1]
    tm = _pick(m, (tm, 1408, 256, 128))
    tn = tn or _pick(n, (1408, 1024, 768, 512, 256, 128))
    tk = tk or _pick(kdim, (1024, 1408, 512, 256, 128))
    nk = kdim // tk
    npair = len(pairs)

    def cat(ref):
        return jnp.concatenate([ref[p] for p in range(ref.shape[0])], axis=-1)

    def body(*refs):
        ins = refs[: 2 * npair]
        pos = 2 * npair
        add_ref = None
        if addend is not None:
            add_ref = refs[pos]
            pos += 1
        o_ref = refs[pos]
        acc_ref = refs[pos + 1]
        k = pl.program_id(2)

        @pl.when(k == 0)
        def _():
            acc_ref[...] = jnp.zeros_like(acc_ref)

        acc = acc_ref[...]
        for p in range(npair):
            a = (cat(ins[2 * p]) if a_lbm else ins[2 * p][...]).astype(CD)
            b = (cat(ins[2 * p + 1]) if b_lbm else ins[2 * p + 1][...]).astype(CD)
            dims = (((0,), (0,)), ((), ())) if trans_a else (((1,), (0,)), ((), ()))
            acc = acc + lax.dot_general(a, b, dims, preferred_element_type=F32)
        acc_ref[...] = acc

        @pl.when(k == nk - 1)
        def _():
            res = acc_ref[...]
            if add_ref is not None:
                res = res + add_ref[...]
            res = res.astype(out_dtype)
            if out_lbm:
                for p in range(tn // LANES):
                    o_ref[p] = res[:, p * LANES:(p + 1) * LANES]
            else:
                o_ref[...] = res

    if trans_a:
        a_spec = (pl.BlockSpec((tm // LANES, tk, LANES), lambda i, j, k: (i, k, 0)) if a_lbm
                  else pl.BlockSpec((tk, tm), lambda i, j, k: (k, i)))
    else:
        a_spec = (pl.BlockSpec((tk // LANES, tm, LANES), lambda i, j, k: (k, i, 0)) if a_lbm
                  else pl.BlockSpec((tm, tk), lambda i, j, k: (i, k)))
    b_spec = (pl.BlockSpec((tn // LANES, tk, LANES), lambda i, j, k: (j, k, 0)) if b_lbm
              else pl.BlockSpec((tk, tn), lambda i, j, k: (k, j)))
    in_specs = []
    args = []
    for a, b in pairs:
        in_specs += [a_spec, b_spec]
        args += [a, b]
    if addend is not None:
        in_specs.append(pl.BlockSpec((tm, tn), lambda i, j, k: (i, j)))
        args.append(addend)
    if out_lbm:
        out_shape = jax.ShapeDtypeStruct((n // LANES, m, LANES), out_dtype)
        out_spec = pl.BlockSpec((tn // LANES, tm, LANES), lambda i, j, k: (j, i, 0))
    else:
        out_shape = jax.ShapeDtypeStruct((m, n), out_dtype)
        out_spec = pl.BlockSpec((tm, tn), lambda i, j, k: (i, j))
    return pl.pallas_call(
        body, name=name, out_shape=out_shape, grid=(m // tm, n // tn, nk),
        in_specs=in_specs, out_specs=out_spec,
        scratch_shapes=[pltpu.VMEM((tm, tn), F32)],
        compiler_params=_params("parallel", "parallel", "arbitrary"),
    )(*args)


ROW_BLOCK = 256


def _rms_fwd(x, g, name):
    s, d = x.shape

    def body(x_ref, g_ref, h_ref):
        xv = x_ref[...]
        rinv = lax.rsqrt(jnp.mean(xv * xv, axis=-1, keepdims=True) + RMS_EPS)
        h_ref[...] = (xv * rinv * g_ref[...]).astype(CD)

    return pl.pallas_call(
        body, name=name, out_shape=jax.ShapeDtypeStruct((s, d), CD), grid=(s // ROW_BLOCK,),
        in_specs=[pl.BlockSpec((ROW_BLOCK, d), lambda i: (i, 0)), pl.BlockSpec((1, d), lambda i: (0, 0))],
        out_specs=pl.BlockSpec((ROW_BLOCK, d), lambda i: (i, 0)),
        compiler_params=_params("parallel"),
    )(x, g.reshape(1, d))


def _rms_bwd(dh, x, g, dx_in, name):
    s, d = x.shape

    def body(dh_ref, x_ref, g_ref, dxin_ref, dx_ref, dxc_ref, dg_ref):
        @pl.when(pl.program_id(0) == 0)
        def _():
            dg_ref[...] = jnp.zeros_like(dg_ref)

        xv = x_ref[...]
        dhv = dh_ref[...]
        rinv = lax.rsqrt(jnp.mean(xv * xv, axis=-1, keepdims=True) + RMS_EPS)
        nrm = xv * rinv
        dn = dhv * g_ref[...]
        dx = dxin_ref[...] + rinv * (dn - nrm * jnp.mean(dn * nrm, axis=-1, keepdims=True))
        dx_ref[...] = dx
        dxc_ref[...] = dx.astype(CD)
        dg_ref[...] += jnp.sum(dhv * nrm, axis=0, keepdims=True)

    row = pl.BlockSpec((ROW_BLOCK, d), lambda i: (i, 0))
    vec = pl.BlockSpec((1, d), lambda i: (0, 0))
    return pl.pallas_call(
        body, name=name,
        out_shape=(jax.ShapeDtypeStruct((s, d), F32), jax.ShapeDtypeStruct((s, d), CD),
                   jax.ShapeDtypeStruct((1, d), F32)),
        grid=(s // ROW_BLOCK,), in_specs=[row, row, vec, row], out_specs=(row, row, vec),
        compiler_params=_params("arbitrary"),
    )(dh, x, g.reshape(1, d), dx_in)


def _loss_head(x, g, target, name):
    s, d = x.shape

    def body(x_ref, g_ref, t_ref, loss_ref, dx_ref, dxc_ref, dg_ref):
        @pl.when(pl.program_id(0) == 0)
        def _():
            dg_ref[...] = jnp.zeros_like(dg_ref)
            loss_ref[...] = jnp.zeros_like(loss_ref)

        xv = x_ref[...]
        gv = g_ref[...]
        rinv = lax.rsqrt(jnp.mean(xv * xv, axis=-1, keepdims=True) + RMS_EPS)
        nrm = xv * rinv
        err = nrm * gv - t_ref[...]
        loss_ref[...] += 0.5 * jnp.sum(jnp.mean(err * err, axis=-1, keepdims=True), axis=0, keepdims=True)
        dy = err * (1.0 / d)
        dn = dy * gv
        dx = rinv * (dn - nrm * jnp.mean(dn * nrm, axis=-1, keepdims=True))
        dx_ref[...] = dx
        dxc_ref[...] = dx.astype(CD)
        dg_ref[...] += jnp.sum(dy * nrm, axis=0, keepdims=True)

    row = pl.BlockSpec((ROW_BLOCK, d), lambda i: (i, 0))
    vec = pl.BlockSpec((1, d), lambda i: (0, 0))
    return pl.pallas_call(
        body, name=name,
        out_shape=(jax.ShapeDtypeStruct((1, LANES), F32), jax.ShapeDtypeStruct((s, d), F32),
                   jax.ShapeDtypeStruct((s, d), CD), jax.ShapeDtypeStruct((1, d), F32)),
        grid=(s // ROW_BLOCK,), in_specs=[row, vec, row],
        out_specs=(pl.BlockSpec((1, LANES), lambda i: (0, 0)), row, row, vec),
        compiler_params=_params("arbitrary"),
    )(x, g.reshape(1, d), target)


def _sigmoid(z):
    return 1.0 / (1.0 + jnp.exp(-z))


FFN_TM = 512
FFN_TN = 1408


def _ffn_up(h, wg, wu, name):
    s, d = h.shape
    f = wg.shape[1]
    tm = _pick(s, (FFN_TM, 256))

    def body(h_ref, wg_ref, wu_ref, g_ref, u_ref, a_ref):
        hv = h_ref[...]
        gv = jnp.dot(hv, wg_ref[...], preferred_element_type=F32)
        uv = jnp.dot(hv, wu_ref[...], preferred_element_type=F32)
        g_ref[...] = gv
        u_ref[...] = uv
        a_ref[...] = (gv * _sigmoid(gv) * uv).astype(CD)

    a_spec = pl.BlockSpec((tm, d), lambda i, j: (i, 0))
    w_spec = pl.BlockSpec((d, FFN_TN), lambda i, j: (0, j))
    o_spec = pl.BlockSpec((tm, FFN_TN), lambda i, j: (i, j))
    return pl.pallas_call(
        body, name=name,
        out_shape=(jax.ShapeDtypeStruct((s, f), F32), jax.ShapeDtypeStruct((s, f), F32),
                   jax.ShapeDtypeStruct((s, f), CD)),
        grid=(s // tm, f // FFN_TN), in_specs=[a_spec, w_spec, w_spec], out_specs=(o_spec, o_spec, o_spec),
        compiler_params=_params("parallel", "parallel"),
    )(h, wg, wu)


def _ffn_dact(dxc, wd_t, g, u, name):
    s, d = dxc.shape
    f = wd_t.shape[1]
    tm = _pick(s, (FFN_TM, 256))

    def body(dx_ref, w_ref, g_ref, u_ref, dg_ref, du_ref):
        da = jnp.dot(dx_ref[...], w_ref[...], preferred_element_type=F32)
        gv = g_ref[...]
        sg = _sigmoid(gv)
        silu = gv * sg
        dg_ref[...] = (da * u_ref[...] * (sg + silu * (1.0 - sg))).astype(CD)
        du_ref[...] = (da * silu).astype(CD)

    a_spec = pl.BlockSpec((tm, d), lambda i, j: (i, 0))
    w_spec = pl.BlockSpec((d, FFN_TN), lambda i, j: (0, j))
    o_spec = pl.BlockSpec((tm, FFN_TN), lambda i, j: (i, j))
    return pl.pallas_call(
        body, name=name,
        out_shape=(jax.ShapeDtypeStruct((s, f), CD), jax.ShapeDtypeStruct((s, f), CD)),
        grid=(s // tm, f // FFN_TN), in_specs=[a_spec, w_spec, o_spec, o_spec], out_specs=(o_spec, o_spec),
        compiler_params=_params("parallel", "parallel"),
    )(dxc, wd_t, g, u)


TIME_BLOCK = 256
SUBLANES = 8
GELU_C = math.sqrt(2.0 / math.pi)
GELU_A = 0.044715


def _gelu(x):
    return 0.5 * x * (1.0 + jnp.tanh(GELU_C * (x + GELU_A * x * x * x)))


def _gelu_grad(x):
    t = jnp.tanh(GELU_C * (x + GELU_A * x * x * x))
    return 0.5 * (1.0 + t) + 0.5 * x * (1.0 - t * t) * GELU_C * (1.0 + 3.0 * GELU_A * x * x)


def _neg_expm1(x):
    series = -x * (1.0 + x * (0.5 + x * (1.0 / 6.0 + x * (1.0 / 24.0))))
    return jnp.where(x > -0.05, series, 1.0 - jnp.exp(x))


def _log_sigmoid(x):
    return jnp.minimum(x, 0.0) - jnp.log1p(jnp.exp(-jnp.abs(x)))


def _shift_down(x, tail, s):
    if s == 0:
        return x
    ext = jnp.concatenate([tail, x], axis=0)
    return pltpu.roll(ext, s, axis=0)[SUBLANES:]


def _shift_up(x, head, s):
    if s == 0:
        return x
    n = x.shape[0]
    ext = jnp.concatenate([x, head], axis=0)
    return pltpu.roll(ext, n + SUBLANES - s, axis=0)[:n]


def _rg_gates(xbr, tail, cw_ref, cb, wr, wi, br, bi, ls):
    taps = [_shift_down(xbr, tail, CONV_W - 1 - k) for k in range(CONV_W)]
    xc = cb
    for k in range(CONV_W):
        xc = xc + cw_ref[pl.ds(k, 1), :] * taps[k]
    xcd = xc.astype(CD)
    r = _sigmoid(jnp.dot(xcd, wr, preferred_element_type=F32) + br)
    i = _sigmoid(jnp.dot(xcd, wi, preferred_element_type=F32) + bi)
    log_a = RG_C * r * ls
    a = jnp.exp(log_a)
    mult = jnp.sqrt(jnp.maximum(_neg_expm1(2.0 * log_a), 0.0))
    return taps, xc, r, i, log_a, a, mult


def _scan8_fwd(a, u):
    row = lax.broadcasted_iota(jnp.int32, a.shape, 0)
    for d in (1, 2, 4):
        a_s = pltpu.roll(a, d, axis=0)
        u_s = pltpu.roll(u, d, axis=0)
        m = row >= d
        u = jnp.where(m, a * u_s + u, u)
        a = jnp.where(m, a * a_s, a)
    return a, u


def _scan8_bwd(b, u):
    row = lax.broadcasted_iota(jnp.int32, b.shape, 0)
    for d in (1, 2, 4):
        b_s = pltpu.roll(b, SUBLANES - d, axis=0)
        u_s = pltpu.roll(u, SUBLANES - d, axis=0)
        m = row < SUBLANES - d
        u = jnp.where(m, b * u_s + u, u)
        b = jnp.where(m, b * b_s, b)
    return b, u


def _rglru_fwd(gate_br, x_br, cw, cb, wr, wi, br, bi, lam, name):
    s, c = x_br.shape
    nt = s // TIME_BLOCK
    tb, cbw = TIME_BLOCK, RG_BW
    groups = tb // SUBLANES

    def body(g_ref, x_ref, tail_ref, cw_ref, cb_ref, wr_ref, wi_ref, br_ref, bi_ref, lam_ref,
             y_ref, hs_ref, carry_ref, a_scr, u_scr):
        t = pl.program_id(1)

        @pl.when(t == 0)
        def _():
            carry_ref[...] = jnp.zeros_like(carry_ref)

        tail = jnp.where(t > 0, tail_ref[...], 0.0)
        ls = _log_sigmoid(lam_ref[...])
        _, xc, _, i, _, a, mult = _rg_gates(x_ref[...], tail, cw_ref, cb_ref[...], wr_ref[0], wi_ref[0],
                                            br_ref[...], bi_ref[...], ls)
        a_scr[...] = a
        u_scr[...] = mult * (i * xc)
        carry = carry_ref[...]
        for gi in range(groups):
            rows = pl.ds(gi * SUBLANES, SUBLANES)
            pa, hl = _scan8_fwd(a_scr[rows, :], u_scr[rows, :])
            hs_ref[rows, :] = hl + pa * carry
            carry = hs_ref[pl.ds(gi * SUBLANES + SUBLANES - 1, 1), :]
        carry_ref[...] = carry
        y_ref[...] = (hs_ref[...] * _gelu(g_ref[...])).astype(CD)

    blk = pl.BlockSpec((tb, cbw), lambda n, t: (t, n))
    tail = pl.BlockSpec((SUBLANES, cbw), lambda n, t: (jnp.maximum(t * groups - 1, 0), n))
    vec = pl.BlockSpec((1, cbw), lambda n, t: (0, n))
    wblk = pl.BlockSpec((1, cbw, cbw), lambda n, t: (n, 0, 0))
    return pl.pallas_call(
        body, name=name,
        out_shape=(jax.ShapeDtypeStruct((s, c), CD), jax.ShapeDtypeStruct((s, c), F32)),
        grid=(RG_BLOCKS, nt),
        in_specs=[blk, blk, tail, pl.BlockSpec((CONV_W, cbw), lambda n, t: (0, n)), vec, wblk, wblk, vec, vec, vec],
        out_specs=(blk, blk),
        scratch_shapes=[pltpu.VMEM((1, cbw), F32), pltpu.VMEM((tb, cbw), F32), pltpu.VMEM((tb, cbw), F32)],
        compiler_params=_params("parallel", "arbitrary"),
    )(gate_br, x_br, x_br, cw, cb, wr, wi, br, bi, lam)


def _rglru_bwd(dy, gate_br, x_br, hs, cw, cb, wr, wi, wrt, wit, br, bi, lam, name):
    s, c = x_br.shape
    nt = s // TIME_BLOCK
    tb, cbw = TIME_BLOCK, RG_BW
    groups = tb // SUBLANES

    def body(dy_ref, g_ref, x_ref, tail_ref, hs_ref, hprev_ref, cw_ref, cb_ref, wr_ref, wi_ref, wrt_ref, wit_ref,
             br_ref, bi_ref, lam_ref,
             dg_ref, dx_ref, dcw_ref, dcb_ref, dbr_ref, dbi_ref, dlam_ref, dwr_ref, dwi_ref,
             carry_ref, head_ref, b_scr, u_scr, dh_scr):
        tr = pl.program_id(1)
        first_block = tr == nt - 1

        @pl.when(tr == 0)
        def _():
            carry_ref[...] = jnp.zeros_like(carry_ref)
            head_ref[...] = jnp.zeros_like(head_ref)
            for ref in (dcw_ref, dcb_ref, dbr_ref, dbi_ref, dlam_ref, dwr_ref, dwi_ref):
                ref[...] = jnp.zeros_like(ref)

        tail = jnp.where(first_block, 0.0, tail_ref[...])
        lam_v = lam_ref[...]
        ls = _log_sigmoid(lam_v)
        taps, xc, r, i, log_a, a, mult = _rg_gates(x_ref[...], tail, cw_ref, cb_ref[...], wr_ref[0], wi_ref[0],
                                                   br_ref[...], bi_ref[...], ls)
        gate_v = g_ref[...]
        dyv = dy_ref[...]
        hsv = hs_ref[...]
        dg_ref[...] = (dyv * hsv * _gelu_grad(gate_v)).astype(CD)

        row = lax.broadcasted_iota(jnp.int32, a.shape, 0)
        b_scr[...] = jnp.where(row == tb - 1, 1.0, pltpu.roll(a, tb - 1, axis=0))
        u_scr[...] = dyv * _gelu(gate_v)
        carry = carry_ref[...]
        for gi in reversed(range(groups)):
            rows = pl.ds(gi * SUBLANES, SUBLANES)
            pb, gl = _scan8_bwd(b_scr[rows, :], u_scr[rows, :])
            dh_scr[rows, :] = gl + pb * carry
            carry = dh_scr[pl.ds(gi * SUBLANES, 1), :]
        dh = dh_scr[...]
        carry_ref[...] = carry * jnp.sum(jnp.where(row == 0, a, 0.0), axis=0, keepdims=True)

        hprev_tail = jnp.where(first_block, 0.0, hprev_ref[...])
        h_prev = _shift_down(hsv, hprev_tail, 1)
        da = dh * h_prev
        ixc = i * xc
        dmult = dh * ixc
        di = dh * mult * xc
        dxc = dh * mult * i
        a2 = a * a
        dlog_a = da * a - dmult * a2 / mult
        dpre_r = (dlog_a * (RG_C * ls)) * r * (1.0 - r)
        dpre_i = di * i * (1.0 - i)
        dlam_ref[...] += jnp.sum(dlog_a * r, axis=0, keepdims=True) * (RG_C * _sigmoid(-lam_v))
        dbr_ref[...] += jnp.sum(dpre_r, axis=0, keepdims=True)
        dbi_ref[...] += jnp.sum(dpre_i, axis=0, keepdims=True)
        xcd = xc.astype(CD)
        dprc = dpre_r.astype(CD)
        dpic = dpre_i.astype(CD)
        tn_dims = (((0,), (0,)), ((), ()))
        dwr_ref[0] += lax.dot_general(xcd, dprc, tn_dims, preferred_element_type=F32)
        dwi_ref[0] += lax.dot_general(xcd, dpic, tn_dims, preferred_element_type=F32)
        dxc = dxc + jnp.dot(dprc, wrt_ref[0], preferred_element_type=F32) + jnp.dot(dpic, wit_ref[0],
                                                                                    preferred_element_type=F32)
        dcb_ref[...] += jnp.sum(dxc, axis=0, keepdims=True)
        for k in range(CONV_W):
            dcw_ref[pl.ds(k, 1), :] += jnp.sum(dxc * taps[k], axis=0, keepdims=True)
        head = head_ref[...]
        dxb = jnp.zeros_like(dxc)
        for sft in range(CONV_W):
            dxb = dxb + cw_ref[pl.ds(CONV_W - 1 - sft, 1), :] * _shift_up(dxc, head, sft)
        dx_ref[...] = dxb.astype(CD)
        head_ref[...] = dxc[0:SUBLANES, :]

    blk = pl.BlockSpec((tb, cbw), lambda n, t: (nt - 1 - t, n))
    tail = pl.BlockSpec((SUBLANES, cbw), lambda n, t: (jnp.maximum((nt - 1 - t) * groups - 1, 0), n))
    vec = pl.BlockSpec((1, cbw), lambda n, t: (0, n))
    cwb = pl.BlockSpec((CONV_W, cbw), lambda n, t: (0, n))
    wblk = pl.BlockSpec((1, cbw, cbw), lambda n, t: (n, 0, 0))
    vshape = jax.ShapeDtypeStruct((1, c), F32)
    wshape = jax.ShapeDtypeStruct((RG_BLOCKS, cbw, cbw), F32)
    return pl.pallas_call(
        body, name=name,
        out_shape=(jax.ShapeDtypeStruct((s, c), CD), jax.ShapeDtypeStruct((s, c), CD),
                   jax.ShapeDtypeStruct((CONV_W, c), F32), vshape, vshape, vshape, vshape, wshape, wshape),
        grid=(RG_BLOCKS, nt),
        in_specs=[blk, blk, blk, tail, blk, tail, cwb, vec, wblk, wblk, wblk, wblk, vec, vec, vec],
        out_specs=(blk, blk, cwb, vec, vec, vec, vec, wblk, wblk),
        scratch_shapes=[pltpu.VMEM((1, cbw), F32), pltpu.VMEM((SUBLANES, cbw), F32),
                        pltpu.VMEM((tb, cbw), F32), pltpu.VMEM((tb, cbw), F32), pltpu.VMEM((tb, cbw), F32)],
        compiler_params=_params("parallel", "arbitrary"),
    )(dy, gate_br, x_br, x_br, hs, hs, cw, cb, wr, wi, wrt, wit, br, bi, lam)


ATT_BLOCK = 256
ATT_SCALE = 1.0 / math.sqrt(SB_HEAD_DIM)
N_PAIRS = SB_HEADS * SB_HEAD_DIM // LANES
NT_DIMS = (((1,), (1,)), ((), ()))
TN_DIMS = (((0,), (0,)), ((), ()))


LOG2E = 1.4426950408889634


def _neg_abs(x):
    bits = lax.bitcast_convert_type(x, jnp.uint32) | jnp.uint32(0x80000000)
    return lax.bitcast_convert_type(bits, F32)


def _qk(qx, kb):
    return lax.dot_general(qx, kb, NT_DIMS, preferred_element_type=F32)


def _sb_logits(qk, valid):
    z2 = qk * (ATT_SCALE * LOG2E)
    lb2 = jnp.minimum(z2, 0.0) - jnp.log2(1.0 + jnp.exp2(_neg_abs(z2)))
    l2 = lb2 - z2
    if valid is not None:
        l2 = jnp.where(valid, l2, 0.0)
    return lb2, l2


def _hi_lo(x):
    hi = x.astype(CD)
    lo = (x - hi.astype(F32)).astype(CD)
    return jnp.concatenate([hi, lo], axis=1)


def _tri2(strict):
    r = lax.broadcasted_iota(jnp.int32, (ATT_BLOCK, ATT_BLOCK), 0)
    c = lax.broadcasted_iota(jnp.int32, (ATT_BLOCK, ATT_BLOCK), 1)
    m = (r > c if strict else r >= c).astype(CD)
    return jnp.concatenate([m, m], axis=0)


def _attn_fwd(qkv, name):
    _, s, _ = qkv.shape
    nblk = s // ATT_BLOCK
    t = ATT_BLOCK

    def body(q_ref, k_ref, v_ref, o_ref, qk_scr, w_scr):
        i = pl.program_id(1)
        lane = lax.broadcasted_iota(jnp.int32, (1, LANES), 1)
        head_masks = (lane < SB_HEAD_DIM, lane >= SB_HEAD_DIM)
        q = q_ref[0]
        qs = [jnp.where(m, q, jnp.zeros_like(q)) for m in head_masks]
        tri = _tri2(True)
        rr = lax.broadcasted_iota(jnp.int32, (t, t), 0)
        cc = lax.broadcasted_iota(jnp.int32, (t, t), 1)
        diag_valid = cc < rr

        def rows_of(j):
            return pl.ds(pl.multiple_of(j * t, t), t)

        def start_logits(j):
            kb = k_ref[0, rows_of(j), :]
            for hd in range(2):
                qk_scr[hd] = _qk(qs[hd], kb)

        def weights(run, valid):
            new_run = []
            for hd in range(2):
                lb2, l2 = _sb_logits(qk_scr[hd], valid)
                w = jnp.exp2(lb2 + (run[hd] + jnp.dot(_hi_lo(l2), tri, preferred_element_type=F32)))
                if valid is not None:
                    w = jnp.where(valid, w, 0.0)
                w_scr[:, hd * t:(hd + 1) * t] = w.astype(CD)
                new_run.append(run[hd] + jnp.sum(l2, axis=1, keepdims=True))
            return tuple(new_run)

        def apply_weights(j):
            vb = v_ref[0, rows_of(j), :]
            vcat = jnp.concatenate([jnp.where(m, vb, jnp.zeros_like(vb)) for m in head_masks], axis=0)
            return jnp.dot(w_scr[...], vcat, preferred_element_type=F32)

        zero = jnp.zeros((t, 1), F32)
        start_logits(i)
        run = weights((zero, zero), diag_valid)
        start_logits(jnp.maximum(i - 1, 0))

        def step(jj, carry):
            run, oacc = carry
            b = i - 1 - jj
            oacc = oacc + apply_weights(b + 1)
            run = weights(run, None)
            start_logits(jnp.maximum(b - 1, 0))
            return run, oacc

        run, oacc = lax.fori_loop(0, i, step, (run, jnp.zeros((t, LANES), F32)))
        o_ref[0] = oacc + apply_weights(0)

    return pl.pallas_call(
        body, name=name, out_shape=jax.ShapeDtypeStruct((N_PAIRS, s, LANES), F32), grid=(N_PAIRS, nblk),
        in_specs=[pl.BlockSpec((1, t, LANES), lambda p, i: (p, i, 0)),
                  pl.BlockSpec((1, s, LANES), lambda p, i: (N_PAIRS + p, 0, 0)),
                  pl.BlockSpec((1, s, LANES), lambda p, i: (2 * N_PAIRS + p, 0, 0))],
        out_specs=pl.BlockSpec((1, t, LANES), lambda p, i: (p, i, 0)),
        scratch_shapes=[pltpu.VMEM((2, t, t), F32), pltpu.VMEM((t, 2 * t), CD)],
        compiler_params=_params("parallel", "arbitrary"),
    )(qkv, qkv, qkv)


def _attn_bwd(qkv, o, do, name):
    _, s, _ = qkv.shape
    nblk = s // ATT_BLOCK
    t = ATT_BLOCK

    def body(q_ref, k_ref, v_ref, o_ref, do_ref, dq_ref, dk_ref, dv_ref, qk_scr, dw_scr, w_scr, dz_scr):
        i = pl.program_id(1)

        @pl.when(i == 0)
        def _():
            dk_ref[...] = jnp.zeros_like(dk_ref)
            dv_ref[...] = jnp.zeros_like(dv_ref)

        lane = lax.broadcasted_iota(jnp.int32, (1, LANES), 1)
        head_masks = (lane < SB_HEAD_DIM, lane >= SB_HEAD_DIM)
        q = q_ref[0]
        dov = do_ref[0]
        ov = o_ref[0]
        qs = [jnp.where(m, q, jnp.zeros_like(q)) for m in head_masks]
        q_scaled_t = jnp.concatenate([(qx.astype(F32) * ATT_SCALE).T for qx in qs], axis=1).astype(CD)
        docs = [jnp.where(m, dov, 0.0).astype(CD) for m in head_masks]
        docat_t = jnp.concatenate([jnp.where(m, dov, 0.0).T for m in head_masks], axis=1).astype(CD)
        totals = [jnp.sum(d.astype(F32) * ov, axis=1, keepdims=True) for d in docs]
        tri = _tri2(True)
        tri_incl = _tri2(False)
        rr = lax.broadcasted_iota(jnp.int32, (t, t), 0)
        cc = lax.broadcasted_iota(jnp.int32, (t, t), 1)
        diag_valid = cc < rr

        def rows_of(j):
            return pl.ds(pl.multiple_of(j * t, t), t)

        def start_products(j):
            kb = k_ref[0, rows_of(j), :]
            vb = v_ref[0, rows_of(j), :]
            for hd in range(2):
                qk_scr[hd] = _qk(qs[hd], kb)
                dw_scr[hd] = lax.dot_general(docs[hd], vb, NT_DIMS, preferred_element_type=F32)

        def logit_grads(run, erun, valid):
            new_run, new_erun = [], []
            for hd in range(2):
                lb2, l2 = _sb_logits(qk_scr[hd], valid)
                w = jnp.exp2(lb2 + (run[hd] + jnp.dot(_hi_lo(l2), tri, preferred_element_type=F32)))
                if valid is not None:
                    w = jnp.where(valid, w, 0.0)
                wc = w.astype(CD)
                w_scr[hd * t:(hd + 1) * t, :] = wc
                e = dw_scr[hd] * wc.astype(F32)
                prefix = (totals[hd] - erun[hd]) - jnp.dot(_hi_lo(e), tri_incl, preferred_element_type=F32)
                dz = e - jnp.exp2(lb2) * (e + prefix)
                if valid is not None:
                    dz = jnp.where(valid, dz, 0.0)
                dz_scr[hd * t:(hd + 1) * t, :] = dz.astype(CD)
                new_run.append(run[hd] + jnp.sum(l2, axis=1, keepdims=True))
                new_erun.append(erun[hd] + jnp.sum(e, axis=1, keepdims=True))
            return tuple(new_run), tuple(new_erun)

        def apply_grads(j):
            rows = rows_of(j)
            kb = k_ref[0, rows, :]
            kcat = jnp.concatenate([jnp.where(m, kb, jnp.zeros_like(kb)) for m in head_masks], axis=0)
            dz2 = dz_scr[...]
            dk_ref[0, :, rows] += jnp.dot(q_scaled_t, dz2, preferred_element_type=F32)
            dv_ref[0, :, rows] += jnp.dot(docat_t, w_scr[...], preferred_element_type=F32)
            return jnp.dot(jnp.concatenate([dz2[:t], dz2[t:]], axis=1), kcat, preferred_element_type=F32)

        zero = jnp.zeros((t, 1), F32)
        start_products(i)
        run, erun = logit_grads((zero, zero), (zero, zero), diag_valid)
        start_products(jnp.maximum(i - 1, 0))

        def step(jj, carry):
            run, erun, dqacc = carry
            b = i - 1 - jj
            dqacc = dqacc + apply_grads(b + 1)
            run, erun = logit_grads(run, erun, None)
            start_products(jnp.maximum(b - 1, 0))
            return run, erun, dqacc

        run, erun, dqacc = lax.fori_loop(0, i, step, (run, erun, jnp.zeros((t, LANES), F32)))
        dq_ref[0] = (dqacc + apply_grads(0)) * ATT_SCALE

    qblk = pl.BlockSpec((1, t, LANES), lambda p, i: (p, i, 0))
    full = pl.BlockSpec((1, LANES, s), lambda p, i: (p, 0, 0))
    shape = jax.ShapeDtypeStruct((N_PAIRS, s, LANES), F32)
    shape_t = jax.ShapeDtypeStruct((N_PAIRS, LANES, s), F32)
    dq, dk_t, dv_t = pl.pallas_call(
        body, name=name, out_shape=(shape, shape_t, shape_t), grid=(N_PAIRS, nblk),
        in_specs=[qblk,
                  pl.BlockSpec((1, s, LANES), lambda p, i: (N_PAIRS + p, 0, 0)),
                  pl.BlockSpec((1, s, LANES), lambda p, i: (2 * N_PAIRS + p, 0, 0)),
                  qblk, qblk],
        out_specs=(qblk, full, full),
        scratch_shapes=[pltpu.VMEM((2, t, t), F32), pltpu.VMEM((2, t, t), F32),
                        pltpu.VMEM((2 * t, t), CD), pltpu.VMEM((2 * t, t), CD)],
        compiler_params=_params("parallel", "arbitrary"),
    )(qkv, qkv, qkv, o, do)
    return dq, jnp.swapaxes(dk_t, 1, 2), jnp.swapaxes(dv_t, 1, 2)


ADAM_COLS = 1024


def _adamw(w, g, m, v, name):
    shape = w.shape
    rows, cols = (shape[-2], shape[-1]) if len(shape) >= 2 else (1, shape[-1])
    lead = w.size // (rows * cols)
    tr = _pick(rows, (512, 256, 128, 64, 32, 16, 8))

    def body(w_ref, g_ref, m_ref, v_ref, d_ref, nm_ref, nv_ref):
        gv = g_ref[...]
        nm = ADAM_B1 * m_ref[...] + (1.0 - ADAM_B1) * gv
        nv = ADAM_B2 * v_ref[...] + (1.0 - ADAM_B2) * (gv * gv)
        m_hat = nm / (1.0 - ADAM_B1 ** ADAM_STEP)
        v_hat = nv / (1.0 - ADAM_B2 ** ADAM_STEP)
        d_ref[...] = -ADAM_LR * (m_hat / (jnp.sqrt(v_hat) + ADAM_EPS) + ADAM_WD * w_ref[...])
        nm_ref[...] = nm
        nv_ref[...] = nv

    blk = pl.BlockSpec((1, tr, cols), lambda l, i: (l, i, 0))
    out = jax.ShapeDtypeStruct((lead, rows, cols), F32)
    d, nm, nv = pl.pallas_call(
        body, name=name, out_shape=(out, out, out), grid=(lead, rows // tr),
        in_specs=[blk, blk, blk, blk], out_specs=(blk, blk, blk), compiler_params=_params("parallel", "parallel"),
    )(*[a.reshape(lead, rows, cols) for a in (w, g, m, v)])
    return d.reshape(shape), nm.reshape(shape), nv.reshape(shape)


HBM = pl.BlockSpec(memory_space=pltpu.HBM)


def _coords():
    return lax.axis_index("x"), lax.axis_index("y"), lax.axis_index("c")


def _other_chips(x, y):
    return [(1 - x, y), (x, 1 - y), (1 - x, 1 - y)]


def _allgather_chips(shard, name):
    r, cols = shard.shape
    half = r // 2

    def body(src_ref, out_ref, send_sems, recv_sems, local_sem):
        x, y, c = _coords()
        sibling = (x, y, 1 - c)
        chips = _other_chips(x, y)

        def rows(px, py, h):
            return out_ref.at[2 * px + py, pl.ds(h * half, half), :]

        def copy(k, block, to, src=None):
            return pltpu.make_async_remote_copy(
                src_ref=rows(*block) if src is None else src, dst_ref=rows(*block),
                send_sem=send_sems.at[k], recv_sem=recv_sems.at[k], device_id=to, device_id_type=MESH)

        mine = pltpu.make_async_copy(src_ref, out_ref.at[2 * x + y], local_sem)
        mine.start()
        my_half = src_ref.at[pl.ds(c * half, half), :]
        first = [copy(j, (x, y, c), (*chip, c), src=my_half) for j, chip in enumerate(chips)]
        for cp in first:
            cp.start()
        passed = [copy(3 + j, (*chip, c), sibling) for j, chip in enumerate(chips)]
        for j, chip in enumerate(chips):
            copy(j, (*chip, c), (x, y, c)).wait_recv()
            passed[j].start()
        for j, chip in enumerate(chips):
            copy(3 + j, (*chip, 1 - c), (x, y, c)).wait_recv()
        for cp in first + passed:
            cp.wait_send()
        mine.wait()

    return pl.pallas_call(
        body, name=name, out_shape=jax.ShapeDtypeStruct((N_CHIPS, r, cols), shard.dtype),
        in_specs=[HBM], out_specs=HBM,
        scratch_shapes=[pltpu.SemaphoreType.DMA((6,)), pltpu.SemaphoreType.DMA((6,)), pltpu.SemaphoreType.DMA],
    )(shard)


def _exchange_sibling_halves(g, name):
    n, r, cols = g.shape
    half = r // 2

    def body(g_ref, out_ref, send_sem, recv_sem):
        x, y, c = _coords()
        cp = pltpu.make_async_remote_copy(
            src_ref=g_ref.at[:, pl.ds((1 - c) * half, half), :], dst_ref=out_ref,
            send_sem=send_sem, recv_sem=recv_sem, device_id=(x, y, 1 - c), device_id_type=MESH)
        cp.start()
        cp.wait()

    return pl.pallas_call(
        body, name=name, out_shape=jax.ShapeDtypeStruct((n, half, cols), g.dtype),
        in_specs=[HBM], out_specs=HBM,
        scratch_shapes=[pltpu.SemaphoreType.DMA, pltpu.SemaphoreType.DMA],
    )(g)


def _scatter_to_chips(p, name):
    n, h, cols = p.shape

    def body(p_ref, out_ref, send_sems, recv_sems, local_sem):
        x, y, c = _coords()
        me = 2 * x + y
        mine = pltpu.make_async_copy(p_ref.at[me], out_ref.at[me], local_sem)
        mine.start()
        sends = []
        for j, (px, py) in enumerate(_other_chips(x, y)):
            sends.append(pltpu.make_async_remote_copy(
                src_ref=p_ref.at[2 * px + py], dst_ref=out_ref.at[me],
                send_sem=send_sems.at[j], recv_sem=recv_sems.at[j], device_id=(px, py, c), device_id_type=MESH))
        for cp in sends:
            cp.start()
        for j, (px, py) in enumerate(_other_chips(x, y)):
            pltpu.make_async_remote_copy(
                src_ref=p_ref.at[me], dst_ref=out_ref.at[2 * px + py],
                send_sem=send_sems.at[j], recv_sem=recv_sems.at[j], device_id=(px, py, c),
                device_id_type=MESH).wait_recv()
        for cp in sends:
            cp.wait_send()
        mine.wait()

    return pl.pallas_call(
        body, name=name, out_shape=jax.ShapeDtypeStruct((n, h, cols), p.dtype),
        in_specs=[HBM], out_specs=HBM,
        scratch_shapes=[pltpu.SemaphoreType.DMA((3,)), pltpu.SemaphoreType.DMA((3,)), pltpu.SemaphoreType.DMA],
    )(p)


def _share_halves(v, name):
    h, cols = v.shape

    def body(v_ref, out_ref, send_sem, recv_sem, local_sem):
        x, y, c = _coords()
        mine = pltpu.make_async_copy(v_ref, out_ref.at[pl.ds(c * h, h), :], local_sem)
        mine.start()
        cp = pltpu.make_async_remote_copy(
            src_ref=v_ref, dst_ref=out_ref.at[pl.ds(c * h, h), :], send_sem=send_sem, recv_sem=recv_sem,
            device_id=(x, y, 1 - c), device_id_type=MESH)
        cp.start()
        pltpu.make_async_remote_copy(
            src_ref=v_ref, dst_ref=out_ref.at[pl.ds((1 - c) * h, h), :], send_sem=send_sem, recv_sem=recv_sem,
            device_id=(x, y, 1 - c), device_id_type=MESH).wait_recv()
        cp.wait_send()
        mine.wait()

    return pl.pallas_call(
        body, name=name, out_shape=jax.ShapeDtypeStruct((2 * h, cols), v.dtype),
        in_specs=[HBM], out_specs=HBM,
        scratch_shapes=[pltpu.SemaphoreType.DMA, pltpu.SemaphoreType.DMA, pltpu.SemaphoreType.DMA],
    )(v)


def _allreduce_small(v, name):
    r, cols = v.shape

    def body(v_ref, out_ref, buf_ref, send_sems, recv_sems):
        x, y, c = _coords()
        me = 4 * x + 2 * y + c
        buf_ref[me] = v_ref[...]
        sends = []
        for k in range(1, N_DEV):
            px = 1 - x if k & 4 else x
            py = 1 - y if k & 2 else y
            pc = 1 - c if k & 1 else c
            sends.append(pltpu.make_async_remote_copy(
                src_ref=v_ref, dst_ref=buf_ref.at[me], send_sem=send_sems.at[k - 1], recv_sem=recv_sems.at[k - 1],
                device_id=(px, py, pc), device_id_type=MESH))
        for cp in sends:
            cp.start()
        for cp in sends:
            cp.wait()
        acc = buf_ref[0]
        for d in range(1, N_DEV):
            acc = acc + buf_ref[d]
        out_ref[...] = acc

    return pl.pallas_call(
        body, name=name, out_shape=jax.ShapeDtypeStruct((r, cols), F32),
        in_specs=[pl.BlockSpec(memory_space=pltpu.VMEM)], out_specs=pl.BlockSpec(memory_space=pltpu.VMEM),
        scratch_shapes=[pltpu.VMEM((N_DEV, r, cols), F32), pltpu.SemaphoreType.DMA((N_DEV - 1,)),
                        pltpu.SemaphoreType.DMA((N_DEV - 1,))],
    )(v)


def _add_sibling(g, from_sibling, core, name):
    n, h, cols = from_sibling.shape
    tr = _pick(h, (512, 256, 128))
    steps = h // tr

    def body(core_ref, a_ref, b_ref, o_ref):
        o_ref[...] = (a_ref[...] + b_ref[...]).astype(o_ref.dtype)

    return pl.pallas_call(
        body, name=name, out_shape=jax.ShapeDtypeStruct(from_sibling.shape, jnp.bfloat16),
        grid_spec=pltpu.PrefetchScalarGridSpec(
            num_scalar_prefetch=1, grid=(n, steps),
            in_specs=[pl.BlockSpec((1, tr, cols), lambda s, i, core_ref: (s, core_ref[0] * steps + i, 0)),
                      pl.BlockSpec((1, tr, cols), lambda s, i, core_ref: (s, i, 0))],
            out_specs=pl.BlockSpec((1, tr, cols), lambda s, i, core_ref: (s, i, 0))),
        compiler_params=_params("parallel", "parallel"),
    )(core.reshape(1).astype(jnp.int32), g, from_sibling)


def _sum_slots(p, name):
    n, r, cols = p.shape
    tr = _pick(r, (512, 256, 128))

    def body(p_ref, o_ref):
        o_ref[...] = ((p_ref[0].astype(F32) + p_ref[1].astype(F32)) + p_ref[2].astype(F32)) + p_ref[3].astype(F32)

    return pl.pallas_call(
        body, name=name, out_shape=jax.ShapeDtypeStruct((r, cols), F32), grid=(r // tr,),
        in_specs=[pl.BlockSpec((n, tr, cols), lambda i: (0, i, 0))],
        out_specs=pl.BlockSpec((tr, cols), lambda i: (i, 0)), compiler_params=_params("parallel"),
    )(p)


PACK_COLS = 1024


def _pack_shards(parts):
    return jnp.concatenate([p.reshape(-1, PACK_COLS) for p in parts], axis=0)


def _unpack_shards(buf, shapes):
    out, row = [], 0
    for shp in shapes:
        nrows = math.prod(shp) // PACK_COLS
        out.append(buf[..., row:row + nrows, :].reshape(buf.shape[:-2] + tuple(shp)))
        row += nrows
    return out


def _local_step(x, target, w):
    t = lambda a: a.T
    g = {}
    h0 = _rms_fwd(x, w["norm_mix_g"][0], "rms_mix0")
    w_in_g, w_in_x = w["a_w_in"][:, :D_RNN], w["a_w_in"][:, D_RNN:]
    gate_br = _matmul([(h0, w_in_g)], F32, "mm_a_gate")
    x_br = _matmul([(h0, w_in_x)], F32, "mm_a_xbr")
    y_a, hs = _rglru_fwd(gate_br, x_br, w["a_conv_w"], w["a_conv_b"], w["a_w_r"], w["a_w_i"], w["a_b_r"],
                         w["a_b_i"], w["a_lambda"], "rglru_fwd")
    x1 = _matmul([(y_a, w["a_w_out"])], F32, "mm_a_out", addend=x)
    h1 = _rms_fwd(x1, w["norm_ffn_g"][0], "rms_ffn0")
    fg0, fu0, act0 = _ffn_up(h1, w["ffn_w_gate"][0], w["ffn_w_up"][0], "ffn0_up")
    x2 = _matmul([(act0, w["ffn_w_down"][0])], F32, "mm_f0_down", addend=x1)
    h2 = _rms_fwd(x2, w["norm_mix_g"][1], "rms_mix1")
    qkv = _matmul([(h2, w["b_w_qkv"])], CD, "mm_b_qkv", out_lbm=True, tn=1024)
    o = _attn_fwd(qkv, "attn_fwd")
    x3 = _matmul([(o, w["b_w_out"])], F32, "mm_b_out", a_lbm=True, addend=x2)
    h3 = _rms_fwd(x3, w["norm_ffn_g"][1], "rms_ffn1")
    fg1, fu1, act1 = _ffn_up(h3, w["ffn_w_gate"][1], w["ffn_w_up"][1], "ffn1_up")
    x4 = _matmul([(act1, w["ffn_w_down"][1])], F32, "mm_f1_down", addend=x3)
    loss, dx4, dx4c, g["final_g"] = _loss_head(x4, w["final_g"], target, "loss_head")

    def ffn_bwd(dx_out, dxc, h, x_in, fg, fu, act, layer, tag):
        dg, du = _ffn_dact(dxc, t(w["ffn_w_down"][layer]), fg, fu, "ffn_" + tag + "_dact")
        dwd = _matmul([(act, dxc)], F32, "mm_" + tag + "_dwd", trans_a=True)
        dwg = _matmul([(h, dg)], F32, "mm_" + tag + "_dwg", trans_a=True)
        dwu = _matmul([(h, du)], F32, "mm_" + tag + "_dwu", trans_a=True)
        dh = _matmul([(dg, t(w["ffn_w_gate"][layer])), (du, t(w["ffn_w_up"][layer]))], F32, "mm_" + tag + "_dh")
        dx_in, dx_in_c, dgain = _rms_bwd(dh, x_in, w["norm_ffn_g"][layer], dx_out, "rms_ffn" + tag + "_bwd")
        return dx_in, dx_in_c, dgain, dwg, dwu, dwd

    dx3, dx3c, dgf1, dwg1, dwu1, dwd1 = ffn_bwd(dx4, dx4c, h3, x3, fg1, fu1, act1, 1, "f1")
    do = _matmul([(dx3c, t(w["b_w_out"]))], F32, "mm_b_do", out_lbm=True, tn=1024)
    g["b_w_out"] = _matmul([(o, dx3c)], F32, "mm_b_dwout", trans_a=True, a_lbm=True)
    dq, dk, dv = _attn_bwd(qkv, o, do, "attn_bwd")
    wq_t = t(w["b_w_qkv"])
    parts = (dq, dk, dv)
    g["b_w_qkv"] = jnp.concatenate(
        [_matmul([(h2, p)], F32, "mm_b_dwqkv%d" % n, trans_a=True, b_lbm=True) for n, p in enumerate(parts)], axis=1)
    dh2 = _matmul([(p, wq_t[n * D_MODEL:(n + 1) * D_MODEL]) for n, p in enumerate(parts)], F32, "mm_b_dh",
                  a_lbm=True)
    dx2, dx2c, dgm1 = _rms_bwd(dh2, x2, w["norm_mix_g"][1], dx3, "rms_mix1_bwd")
    dx1, dx1c, dgf0, dwg0, dwu0, dwd0 = ffn_bwd(dx2, dx2c, h1, x1, fg0, fu0, act0, 0, "f0")
    dy_a = _matmul([(dx1c, t(w["a_w_out"]))], F32, "mm_a_dy")
    g["a_w_out"] = _matmul([(y_a, dx1c)], F32, "mm_a_dwout", trans_a=True)
    wrt = jnp.swapaxes(w["a_w_r"], 1, 2)
    wit = jnp.swapaxes(w["a_w_i"], 1, 2)
    (dgate, dxbr, g["a_conv_w"], g["a_conv_b"], g["a_b_r"], g["a_b_i"], g["a_lambda"], g["a_w_r"],
     g["a_w_i"]) = _rglru_bwd(dy_a, gate_br, x_br, hs, w["a_conv_w"], w["a_conv_b"], w["a_w_r"], w["a_w_i"], wrt, wit,
                              w["a_b_r"], w["a_b_i"], w["a_lambda"], "rglru_bwd")
    g["a_w_in"] = jnp.concatenate([_matmul([(h0, dgate)], F32, "mm_a_dwin_g", trans_a=True),
                                   _matmul([(h0, dxbr)], F32, "mm_a_dwin_x", trans_a=True)], axis=1)
    dh0 = _matmul([(dgate, t(w_in_g)), (dxbr, t(w_in_x))], F32, "mm_a_dh")
    dx0, _, dgm0 = _rms_bwd(dh0, x, w["norm_mix_g"][0], dx1, "rms_mix0_bwd")
    g["norm_mix_g"] = jnp.concatenate([dgm0, dgm1], axis=0)
    g["norm_ffn_g"] = jnp.concatenate([dgf0, dgf1], axis=0)
    g["ffn_w_gate"] = jnp.stack([dwg0, dwg1])
    g["ffn_w_up"] = jnp.stack([dwu0, dwu1])
    g["ffn_w_down"] = jnp.stack([dwd0, dwd1])
    return loss, dx0, g


WEIGHTS = ["norm_mix_g", "norm_ffn_g", "a_w_in", "a_conv_w", "a_conv_b", "a_w_r", "a_b_r", "a_w_i", "a_b_i",
           "a_lambda", "a_w_out", "b_w_qkv", "b_w_out", "ffn_w_gate", "ffn_w_up", "ffn_w_down", "final_g"]
BIG = [("a_w_in", 2), ("a_w_r", 2), ("a_w_i", 2), ("a_w_out", 1), ("b_w_qkv", 2), ("b_w_out", 1),
       ("ffn_w_gate", 2), ("ffn_w_up", 2), ("ffn_w_down", 1)]
SMALL = ["norm_mix_g", "norm_ffn_g", "a_conv_w", "a_conv_b", "a_b_r", "a_b_i", "a_lambda", "final_g"]


def _join_chips(stack, axis):
    return jnp.concatenate([stack[s] for s in range(N_CHIPS)], axis=axis)


def _split_chips(full, axis):
    return jnp.stack(jnp.split(full, N_CHIPS, axis=axis))


def _step(x, target, weights, moments_m, moments_v):
    chip = 2 * lax.axis_index("x") + lax.axis_index("y")
    core = lax.axis_index("c")
    shard_shapes = [weights[n].shape for n, _ in BIG]
    packed = _pack_shards([weights[n].astype(CD) for n, _ in BIG])
    gathered = _allgather_chips(packed, "allgather_weights")
    full = {}
    for (n, axis), stack in zip(BIG, _unpack_shards(gathered, shard_shapes)):
        joined = _join_chips(stack, axis)
        full[n] = joined[0] if joined.shape[0] == 1 else joined
    cw_rows = jnp.zeros((N_CHIPS, CONV_W, RG_BW), F32)
    cw_rows = lax.dynamic_update_slice(cw_rows, jnp.where(core == 0, weights["a_conv_w"], 0.0), (chip, 0, 0))
    cw_all = _allreduce_small(cw_rows.reshape(-1, LANES), "allgather_conv_w").reshape(N_CHIPS, CONV_W, RG_BW)
    full["a_conv_w"] = jnp.concatenate([cw_all[s] for s in range(N_CHIPS)], axis=1)
    for n in ("norm_mix_g", "norm_ffn_g", "final_g"):
        full[n] = weights[n]
    for n in ("a_conv_b", "a_b_r", "a_b_i", "a_lambda"):
        full[n] = weights[n]
    loss, dx, grads = _local_step(x[0], target[0], full)
    small_parts = [grads[n].reshape(-1) for n in SMALL] + [loss.reshape(-1)]
    sizes = [p.shape[0] for p in small_parts]
    small = _allreduce_small(jnp.concatenate(small_parts).reshape(-1, LANES), "allreduce_small").reshape(-1)
    red, pos = {}, 0
    for n, sz in zip(SMALL + ["loss"], sizes):
        red[n] = small[pos:pos + sz]
        pos += sz
    loss_out = red["loss"][0]
    g_out = {}
    for n in SMALL:
        if n == "a_conv_w":
            g_out[n] = lax.dynamic_slice(red[n].reshape(CONV_W, D_RNN), (0, chip * RG_BW), (CONV_W, RG_BW)).reshape(
                weights[n].shape)
        else:
            g_out[n] = red[n].reshape(weights[n].shape)
    stacks = []
    for n, axis in BIG:
        gfull = grads[n].reshape((1,) + grads[n].shape) if grads[n].ndim == len(weights[n].shape) - 1 else grads[n]
        stacks.append(_split_chips(gfull, axis).reshape(N_CHIPS, -1, PACK_COLS))
    gbuf = jnp.concatenate(stacks, axis=1)
    from_sibling = _exchange_sibling_halves(gbuf, "rs_sibling")
    chip_partial = _add_sibling(gbuf, from_sibling, core, "rs_add_sibling")
    from_chips = _scatter_to_chips(chip_partial, "rs_chips")
    reduced_half = _sum_slots(from_chips, "rs_sum_chips")
    reduced = _share_halves(reduced_half, "rs_share")
    for (n, _), gsh in zip(BIG, _unpack_shards(reduced, shard_shapes)):
        g_out[n] = gsh
    outs_g, outs_d, outs_m, outs_v = [], [], [], []
    for n in WEIGHTS:
        d, nm, nv = _adamw(weights[n], g_out[n], moments_m[n], moments_v[n], "adamw_" + n)
        outs_g.append(g_out[n])
        outs_d.append(d)
        outs_m.append(nm)
        outs_v.append(nv)
    return (loss_out, dx[None], *outs_g, *outs_d, *outs_m, *outs_v)


def kernel(x, norm_mix_g, norm_ffn_g, a_w_in, a_conv_w, a_conv_b, a_w_r, a_b_r, a_w_i, a_b_i, a_lambda, a_w_out, b_w_qkv, b_w_out, ffn_w_gate, ffn_w_up, ffn_w_down, final_g, loss_target, m_norm_mix_g, m_norm_ffn_g, m_a_w_in, m_a_conv_w, m_a_conv_b, m_a_w_r, m_a_b_r, m_a_w_i, m_a_b_i, m_a_lambda, m_a_w_out, m_b_w_qkv, m_b_w_out, m_ffn_w_gate, m_ffn_w_up, m_ffn_w_down, m_final_g, v_norm_mix_g, v_norm_ffn_g, v_a_w_in, v_a_conv_w, v_a_conv_b, v_a_w_r, v_a_b_r, v_a_w_i, v_a_b_i, v_a_lambda, v_a_w_out, v_b_w_qkv, v_b_w_out, v_ffn_w_gate, v_ffn_w_up, v_ffn_w_down, v_final_g):
    ws = [norm_mix_g, norm_ffn_g, a_w_in, a_conv_w, a_conv_b, a_w_r, a_b_r, a_w_i, a_b_i, a_lambda, a_w_out, b_w_qkv,
          b_w_out, ffn_w_gate, ffn_w_up, ffn_w_down, final_g]
    ms = [m_norm_mix_g, m_norm_ffn_g, m_a_w_in, m_a_conv_w, m_a_conv_b, m_a_w_r, m_a_b_r, m_a_w_i, m_a_b_i, m_a_lambda,
          m_a_w_out, m_b_w_qkv, m_b_w_out, m_ffn_w_gate, m_ffn_w_up, m_ffn_w_down, m_final_g]
    vs = [v_norm_mix_g, v_norm_ffn_g, v_a_w_in, v_a_conv_w, v_a_conv_b, v_a_w_r, v_a_b_r, v_a_w_i, v_a_b_i, v_a_lambda,
          v_a_w_out, v_b_w_qkv, v_b_w_out, v_ffn_w_gate, v_ffn_w_up, v_ffn_w_down, v_final_g]
    return _step(x, loss_target, dict(zip(WEIGHTS, ws)), dict(zip(WEIGHTS, ms)), dict(zip(WEIGHTS, vs)))
```

```python
import functools
import math

import jax
import jax.numpy as jnp
from jax import lax
from jax.experimental import pallas as pl
from jax.experimental.pallas import tpu as pltpu

F32 = jnp.float32
CD = jnp.bfloat16

D_MODEL = 1024
D_RNN = 1024
RG_BLOCKS = 4
RG_BW = 256
CONV_W = 4
RG_C = 8.0
SB_HEADS = 16
SB_HEAD_DIM = 64
D_FF = 2816
RMS_EPS = 1e-6
N_CHIPS = 4
N_DEV = 8

ADAM_LR = 0.001
ADAM_B1 = 0.9
ADAM_B2 = 0.999
ADAM_EPS = 1e-08
ADAM_WD = 0.01
ADAM_STEP = 10

LANES = 128
VMEM_LIMIT = 56 * 1024 * 1024
MESH = pl.DeviceIdType.MESH


def _params(*sem):
    return pltpu.CompilerParams(dimension_semantics=sem, vmem_limit_bytes=VMEM_LIMIT)


def _pick(n, prefs):
    for p in prefs:
        if n % p == 0:
            return p
    return n


def _matmul(pairs, out_dtype, name, *, trans_a=False, a_lbm=False, b_lbm=False, out_lbm=False, addend=None,
            tm=512, tn=None, tk=None):
    a0, b0 = pairs[0]
    if trans_a:
        kdim = a0.shape[1] if a_lbm else a0.shape[0]
        m = a0.shape[0] * LANES if a_lbm else a0.shape[1]
    else:
        m = a0.shape[1] if a_lbm else a0.shape[0]
        kdim = a0.shape[0] * LANES if a_lbm else a0.shape[1]
    n = b0.shape[0] * LANES if b_lbm else b0.shape[1]
    tm = _pick(m, (tm, 1408, 256, 128))
    tn = tn or _pick(n, (1408, 1024, 768, 512, 256, 128))
    tk = tk or _pick(kdim, (1024, 1408, 512, 256, 128))
    nk = kdim // tk
    npair = len(pairs)

    def cat(ref):
        return jnp.concatenate([ref[p] for p in range(ref.shape[0])], axis=-1)

    def body(*refs):
        ins = refs[: 2 * npair]
        pos = 2 * npair
        add_ref = None
        if addend is not None:
            add_ref = refs[pos]
            pos += 1
        o_ref = refs[pos]
        acc_ref = refs[pos + 1]
        k = pl.program_id(2)

        @pl.when(k == 0)
        def _():
            acc_ref[...] = jnp.zeros_like(acc_ref)

        acc = acc_ref[...]
        for p in range(npair):
            a = (cat(ins[2 * p]) if a_lbm else ins[2 * p][...]).astype(CD)
            b = (cat(ins[2 * p + 1]) if b_lbm else ins[2 * p + 1][...]).astype(CD)
            dims = (((0,), (0,)), ((), ())) if trans_a else (((1,), (0,)), ((), ()))
            acc = acc + lax.dot_general(a, b, dims, preferred_element_type=F32)
        acc_ref[...] = acc

        @pl.when(k == nk - 1)
        def _():
            res = acc_ref[...]
            if add_ref is not None:
                res = res + add_ref[...]
            res = res.astype(out_dtype)
            if out_lbm:
                for p in range(tn // LANES):
                    o_ref[p] = res[:, p * LANES:(p + 1) * LANES]
            else:
                o_ref[...] = res

    if trans_a:
        a_spec = (pl.BlockSpec((tm // LANES, tk, LANES), lambda i, j, k: (i, k, 0)) if a_lbm
                  else pl.BlockSpec((tk, tm), lambda i, j, k: (k, i)))
    else:
        a_spec = (pl.BlockSpec((tk // LANES, tm, LANES), lambda i, j, k: (k, i, 0)) if a_lbm
                  else pl.BlockSpec((tm, tk), lambda i, j, k: (i, k)))
    b_spec = (pl.BlockSpec((tn // LANES, tk, LANES), lambda i, j, k: (j, k, 0)) if b_lbm
              else pl.BlockSpec((tk, tn), lambda i, j, k: (k, j)))
    in_specs = []
    args = []
    for a, b in pairs:
        in_specs += [a_spec, b_spec]
        args += [a, b]
    if addend is not None:
        in_specs.append(pl.BlockSpec((tm, tn), lambda i, j, k: (i, j)))
        args.append(addend)
    if out_lbm:
        out_shape = jax.ShapeDtypeStruct((n // LANES, m, LANES), out_dtype)
        out_spec = pl.BlockSpec((tn // LANES, tm, LANES), lambda i, j, k: (j, i, 0))
    else:
        out_shape = jax.ShapeDtypeStruct((m, n), out_dtype)
        out_spec = pl.BlockSpec((tm, tn), lambda i, j, k: (i, j))
    return pl.pallas_call(
        body, name=name, out_shape=out_shape, grid=(m // tm, n // tn, nk),
        in_specs=in_specs, out_specs=out_spec,
        scratch_shapes=[pltpu.VMEM((tm, tn), F32)],
        compiler_params=_params("parallel", "parallel", "arbitrary"),
    )(*args)


ROW_BLOCK = 256


def _rms_fwd(x, g, name):
    s, d = x.shape

    def body(x_ref, g_ref, h_ref):
        xv = x_ref[...]
        rinv = lax.rsqrt(jnp.mean(xv * xv, axis=-1, keepdims=True) + RMS_EPS)
        h_ref[...] = (xv * rinv * g_ref[...]).astype(CD)

    return pl.pallas_call(
        body, name=name, out_shape=jax.ShapeDtypeStruct((s, d), CD), grid=(s // ROW_BLOCK,),
        in_specs=[pl.BlockSpec((ROW_BLOCK, d), lambda i: (i, 0)), pl.BlockSpec((1, d), lambda i: (0, 0))],
        out_specs=pl.BlockSpec((ROW_BLOCK, d), lambda i: (i, 0)),
        compiler_params=_params("parallel"),
    )(x, g.reshape(1, d))


def _rms_bwd(dh, x, g, dx_in, name):
    s, d = x.shape

    def body(dh_ref, x_ref, g_ref, dxin_ref, dx_ref, dxc_ref, dg_ref):
        @pl.when(pl.program_id(0) == 0)
        def _():
            dg_ref[...] = jnp.zeros_like(dg_ref)

        xv = x_ref[...]
        dhv = dh_ref[...]
        rinv = lax.rsqrt(jnp.mean(xv * xv, axis=-1, keepdims=True) + RMS_EPS)
        nrm = xv * rinv
        dn = dhv * g_ref[...]
        dx = dxin_ref[...] + rinv * (dn - nrm * jnp.mean(dn * nrm, axis=-1, keepdims=True))
        dx_ref[...] = dx
        dxc_ref[...] = dx.astype(CD)
        dg_ref[...] += jnp.sum(dhv * nrm, axis=0, keepdims=True)

    row = pl.BlockSpec((ROW_BLOCK, d), lambda i: (i, 0))
    vec = pl.BlockSpec((1, d), lambda i: (0, 0))
    return pl.pallas_call(
        body, name=name,
        out_shape=(jax.ShapeDtypeStruct((s, d), F32), jax.ShapeDtypeStruct((s, d), CD),
                   jax.ShapeDtypeStruct((1, d), F32)),
        grid=(s // ROW_BLOCK,), in_specs=[row, row, vec, row], out_specs=(row, row, vec),
        compiler_params=_params("arbitrary"),
    )(dh, x, g.reshape(1, d), dx_in)


def _loss_head(x, g, target, name):
    s, d = x.shape

    def body(x_ref, g_ref, t_ref, loss_ref, dx_ref, dxc_ref, dg_ref):
        @pl.when(pl.program_id(0) == 0)
        def _():
            dg_ref[...] = jnp.zeros_like(dg_ref)
            loss_ref[...] = jnp.zeros_like(loss_ref)

        xv = x_ref[...]
        gv = g_ref[...]
        rinv = lax.rsqrt(jnp.mean(xv * xv, axis=-1, keepdims=True) + RMS_EPS)
        nrm = xv * rinv
        err = nrm * gv - t_ref[...]
        loss_ref[...] += 0.5 * jnp.sum(jnp.mean(err * err, axis=-1, keepdims=True), axis=0, keepdims=True)
        dy = err * (1.0 / d)
        dn = dy * gv
        dx = rinv * (dn - nrm * jnp.mean(dn * nrm, axis=-1, keepdims=True))
        dx_ref[...] = dx
        dxc_ref[...] = dx.astype(CD)
        dg_ref[...] += jnp.sum(dy * nrm, axis=0, keepdims=True)

    row = pl.BlockSpec((ROW_BLOCK, d), lambda i: (i, 0))
    vec = pl.BlockSpec((1, d), lambda i: (0, 0))
    return pl.pallas_call(
        body, name=name,
        out_shape=(jax.ShapeDtypeStruct((1, LANES), F32), jax.ShapeDtypeStruct((s, d), F32),
                   jax.ShapeDtypeStruct((s, d), CD), jax.ShapeDtypeStruct((1, d), F32)),
        grid=(s // ROW_BLOCK,), in_specs=[row, vec, row],
        out_specs=(pl.BlockSpec((1, LANES), lambda i: (0, 0)), row, row, vec),
        compiler_params=_params("arbitrary"),
    )(x, g.reshape(1, d), target)


def _sigmoid(z):
    return 1.0 / (1.0 + jnp.exp(-z))


FFN_TM = 512
FFN_TN = 1408


def _ffn_up(h, wg, wu, name):
    s, d = h.shape
    f = wg.shape[1]
    tm = _pick(s, (FFN_TM, 256))

    def body(h_ref, wg_ref, wu_ref, g_ref, u_ref, a_ref):
        hv = h_ref[...]
        gv = jnp.dot(hv, wg_ref[...], preferred_element_type=F32)
        uv = jnp.dot(hv, wu_ref[...], preferred_element_type=F32)
        g_ref[...] = gv
        u_ref[...] = uv
        a_ref[...] = (gv * _sigmoid(gv) * uv).astype(CD)

    a_spec = pl.BlockSpec((tm, d), lambda i, j: (i, 0))
    w_spec = pl.BlockSpec((d, FFN_TN), lambda i, j: (0, j))
    o_spec = pl.BlockSpec((tm, FFN_TN), lambda i, j: (i, j))
    return pl.pallas_call(
        body, name=name,
        out_shape=(jax.ShapeDtypeStruct((s, f), F32), jax.ShapeDtypeStruct((s, f), F32),
                   jax.ShapeDtypeStruct((s, f), CD)),
        grid=(s // tm, f // FFN_TN), in_specs=[a_spec, w_spec, w_spec], out_specs=(o_spec, o_spec, o_spec),
        compiler_params=_params("parallel", "parallel"),
    )(h, wg, wu)


def _ffn_dact(dxc, wd_t, g, u, name):
    s, d = dxc.shape
    f = wd_t.shape[1]
    tm = _pick(s, (FFN_TM, 256))

    def body(dx_ref, w_ref, g_ref, u_ref, dg_ref, du_ref):
        da = jnp.dot(dx_ref[...], w_ref[...], preferred_element_type=F32)
        gv = g_ref[...]
        sg = _sigmoid(gv)
        silu = gv * sg
        dg_ref[...] = (da * u_ref[...] * (sg + silu * (1.0 - sg))).astype(CD)
        du_ref[...] = (da * silu).astype(CD)

    a_spec = pl.BlockSpec((tm, d), lambda i, j: (i, 0))
    w_spec = pl.BlockSpec((d, FFN_TN), lambda i, j: (0, j))
    o_spec = pl.BlockSpec((tm, FFN_TN), lambda i, j: (i, j))
    return pl.pallas_call(
        body, name=name,
        out_shape=(jax.ShapeDtypeStruct((s, f), CD), jax.ShapeDtypeStruct((s, f), CD)),
        grid=(s // tm, f // FFN_TN), in_specs=[a_spec, w_spec, o_spec, o_spec], out_specs=(o_spec, o_spec),
        compiler_params=_params("parallel", "parallel"),
    )(dxc, wd_t, g, u)


TIME_BLOCK = 256
SUBLANES = 8
GELU_C = math.sqrt(2.0 / math.pi)
GELU_A = 0.044715


def _gelu(x):
    return 0.5 * x * (1.0 + jnp.tanh(GELU_C * (x + GELU_A * x * x * x)))


def _gelu_grad(x):
    t = jnp.tanh(GELU_C * (x + GELU_A * x * x * x))
    return 0.5 * (1.0 + t) + 0.5 * x * (1.0 - t * t) * GELU_C * (1.0 + 3.0 * GELU_A * x * x)


def _neg_expm1(x):
    series = -x * (1.0 + x * (0.5 + x * (1.0 / 6.0 + x * (1.0 / 24.0))))
    return jnp.where(x > -0.05, series, 1.0 - jnp.exp(x))


def _log_sigmoid(x):
    return jnp.minimum(x, 0.0) - jnp.log1p(jnp.exp(-jnp.abs(x)))


def _shift_down(x, tail, s):
    if s == 0:
        return x
    ext = jnp.concatenate([tail, x], axis=0)
    return pltpu.roll(ext, s, axis=0)[SUBLANES:]


def _shift_up(x, head, s):
    if s == 0:
        return x
    n = x.shape[0]
    ext = jnp.concatenate([x, head], axis=0)
    return pltpu.roll(ext, n + SUBLANES - s, axis=0)[:n]


def _rg_gates(xbr, tail, cw_ref, cb, wr, wi, br, bi, ls):
    taps = [_shift_down(xbr, tail, CONV_W - 1 - k) for k in range(CONV_W)]
    xc = cb
    for k in range(CONV_W):
        xc = xc + cw_ref[pl.ds(k, 1), :] * taps[k]
    xcd = xc.astype(CD)
    r = _sigmoid(jnp.dot(xcd, wr, preferred_element_type=F32) + br)
    i = _sigmoid(jnp.dot(xcd, wi, preferred_element_type=F32) + bi)
    log_a = RG_C * r * ls
    a = jnp.exp(log_a)
    mult = jnp.sqrt(jnp.maximum(_neg_expm1(2.0 * log_a), 0.0))
    return taps, xc, r, i, log_a, a, mult


def _scan8_fwd(a, u):
    row = lax.broadcasted_iota(jnp.int32, a.shape, 0)
    for d in (1, 2, 4):
        a_s = pltpu.roll(a, d, axis=0)
        u_s = pltpu.roll(u, d, axis=0)
        m = row >= d
        u = jnp.where(m, a * u_s + u, u)
        a = jnp.where(m, a * a_s, a)
    return a, u


def _scan8_bwd(b, u):
    row = lax.broadcasted_iota(jnp.int32, b.shape, 0)
    for d in (1, 2, 4):
        b_s = pltpu.roll(b, SUBLANES - d, axis=0)
        u_s = pltpu.roll(u, SUBLANES - d, axis=0)
        m = row < SUBLANES - d
        u = jnp.where(m, b * u_s + u, u)
        b = jnp.where(m, b * b_s, b)
    return b, u


def _rglru_fwd(gate_br, x_br, cw, cb, wr, wi, br, bi, lam, name):
    s, c = x_br.shape
    nt = s // TIME_BLOCK
    tb, cbw = TIME_BLOCK, RG_BW
    groups = tb // SUBLANES

    def body(g_ref, x_ref, tail_ref, cw_ref, cb_ref, wr_ref, wi_ref, br_ref, bi_ref, lam_ref,
             y_ref, hs_ref, carry_ref, a_scr, u_scr):
        t = pl.program_id(1)

        @pl.when(t == 0)
        def _():
            carry_ref[...] = jnp.zeros_like(carry_ref)

        tail = jnp.where(t > 0, tail_ref[...], 0.0)
        ls = _log_sigmoid(lam_ref[...])
        _, xc, _, i, _, a, mult = _rg_gates(x_ref[...], tail, cw_ref, cb_ref[...], wr_ref[0], wi_ref[0],
                                            br_ref[...], bi_ref[...], ls)
        a_scr[...] = a
        u_scr[...] = mult * (i * xc)
        carry = carry_ref[...]
        for gi in range(groups):
            rows = pl.ds(gi * SUBLANES, SUBLANES)
            pa, hl = _scan8_fwd(a_scr[rows, :], u_scr[rows, :])
            hs_ref[rows, :] = hl + pa * carry
            carry = hs_ref[pl.ds(gi * SUBLANES + SUBLANES - 1, 1), :]
        carry_ref[...] = carry
        y_ref[...] = (hs_ref[...] * _gelu(g_ref[...])).astype(CD)

    blk = pl.BlockSpec((tb, cbw), lambda n, t: (t, n))
    tail = pl.BlockSpec((SUBLANES, cbw), lambda n, t: (jnp.maximum(t * groups - 1, 0), n))
    vec = pl.BlockSpec((1, cbw), lambda n, t: (0, n))
    wblk = pl.BlockSpec((1, cbw, cbw), lambda n, t: (n, 0, 0))
    return pl.pallas_call(
        body, name=name,
        out_shape=(jax.ShapeDtypeStruct((s, c), CD), jax.ShapeDtypeStruct((s, c), F32)),
        grid=(RG_BLOCKS, nt),
        in_specs=[blk, blk, tail, pl.BlockSpec((CONV_W, cbw), lambda n, t: (0, n)), vec, wblk, wblk, vec, vec, vec],
        out_specs=(blk, blk),
        scratch_shapes=[pltpu.VMEM((1, cbw), F32), pltpu.VMEM((tb, cbw), F32), pltpu.VMEM((tb, cbw), F32)],
        compiler_params=_params("parallel", "arbitrary"),
    )(gate_br, x_br, x_br, cw, cb, wr, wi, br, bi, lam)


def _rglru_bwd(dy, gate_br, x_br, hs, cw, cb, wr, wi, wrt, wit, br, bi, lam, name):
    s, c = x_br.shape
    nt = s // TIME_BLOCK
    tb, cbw = TIME_BLOCK, RG_BW
    groups = tb // SUBLANES

    def body(dy_ref, g_ref, x_ref, tail_ref, hs_ref, hprev_ref, cw_ref, cb_ref, wr_ref, wi_ref, wrt_ref, wit_ref,
             br_ref, bi_ref, lam_ref,
             dg_ref, dx_ref, dcw_ref, dcb_ref, dbr_ref, dbi_ref, dlam_ref, dwr_ref, dwi_ref,
             carry_ref, head_ref, b_scr, u_scr, dh_scr):
        tr = pl.program_id(1)
        first_block = tr == nt - 1

        @pl.when(tr == 0)
        def _():
            carry_ref[...] = jnp.zeros_like(carry_ref)
            head_ref[...] = jnp.zeros_like(head_ref)
            for ref in (dcw_ref, dcb_ref, dbr_ref, dbi_ref, dlam_ref, dwr_ref, dwi_ref):
                ref[...] = jnp.zeros_like(ref)

        tail = jnp.where(first_block, 0.0, tail_ref[...])
        lam_v = lam_ref[...]
        ls = _log_sigmoid(lam_v)
        taps, xc, r, i, log_a, a, mult = _rg_gates(x_ref[...], tail, cw_ref, cb_ref[...], wr_ref[0], wi_ref[0],
                                                   br_ref[...], bi_ref[...], ls)
        gate_v = g_ref[...]
        dyv = dy_ref[...]
        hsv = hs_ref[...]
        dg_ref[...] = (dyv * hsv * _gelu_grad(gate_v)).astype(CD)

        row = lax.broadcasted_iota(jnp.int32, a.shape, 0)
        b_scr[...] = jnp.where(row == tb - 1, 1.0, pltpu.roll(a, tb - 1, axis=0))
        u_scr[...] = dyv * _gelu(gate_v)
        carry = carry_ref[...]
        for gi in reversed(range(groups)):
            rows = pl.ds(gi * SUBLANES, SUBLANES)
            pb, gl = _scan8_bwd(b_scr[rows, :], u_scr[rows, :])
            dh_scr[rows, :] = gl + pb * carry
            carry = dh_scr[pl.ds(gi * SUBLANES, 1), :]
        dh = dh_scr[...]
        carry_ref[...] = carry * jnp.sum(jnp.where(row == 0, a, 0.0), axis=0, keepdims=True)

        hprev_tail = jnp.where(first_block, 0.0, hprev_ref[...])
        h_prev = _shift_down(hsv, hprev_tail, 1)
        da = dh * h_prev
        ixc = i * xc
        dmult = dh * ixc
        di = dh * mult * xc
        dxc = dh * mult * i
        a2 = a * a
        dlog_a = da * a - dmult * a2 / mult
        dpre_r = (dlog_a * (RG_C * ls)) * r * (1.0 - r)
        dpre_i = di * i * (1.0 - i)
        dlam_ref[...] += jnp.sum(dlog_a * r, axis=0, keepdims=True) * (RG_C * _sigmoid(-lam_v))
        dbr_ref[...] += jnp.sum(dpre_r, axis=0, keepdims=True)
        dbi_ref[...] += jnp.sum(dpre_i, axis=0, keepdims=True)
        xcd = xc.astype(CD)
        dprc = dpre_r.astype(CD)
        dpic = dpre_i.astype(CD)
        tn_dims = (((0,), (0,)), ((), ()))
        dwr_ref[0] += lax.dot_general(xcd, dprc, tn_dims, preferred_element_type=F32)
        dwi_ref[0] += lax.dot_general(xcd, dpic, tn_dims, preferred_element_type=F32)
        dxc = dxc + jnp.dot(dprc, wrt_ref[0], preferred_element_type=F32) + jnp.dot(dpic, wit_ref[0],
                                                                                    preferred_element_type=F32)
        dcb_ref[...] += jnp.sum(dxc, axis=0, keepdims=True)
        for k in range(CONV_W):
            dcw_ref[pl.ds(k, 1), :] += jnp.sum(dxc * taps[k], axis=0, keepdims=True)
        head = head_ref[...]
        dxb = jnp.zeros_like(dxc)
        for sft in range(CONV_W):
            dxb = dxb + cw_ref[pl.ds(CONV_W - 1 - sft, 1), :] * _shift_up(dxc, head, sft)
        dx_ref[...] = dxb.astype(CD)
        head_ref[...] = dxc[0:SUBLANES, :]

    blk = pl.BlockSpec((tb, cbw), lambda n, t: (nt - 1 - t, n))
    tail = pl.BlockSpec((SUBLANES, cbw), lambda n, t: (jnp.maximum((nt - 1 - t) * groups - 1, 0), n))
    vec = pl.BlockSpec((1, cbw), lambda n, t: (0, n))
    cwb = pl.BlockSpec((CONV_W, cbw), lambda n, t: (0, n))
    wblk = pl.BlockSpec((1, cbw, cbw), lambda n, t: (n, 0, 0))
    vshape = jax.ShapeDtypeStruct((1, c), F32)
    wshape = jax.ShapeDtypeStruct((RG_BLOCKS, cbw, cbw), F32)
    return pl.pallas_call(
        body, name=name,
        out_shape=(jax.ShapeDtypeStruct((s, c), CD), jax.ShapeDtypeStruct((s, c), CD),
                   jax.ShapeDtypeStruct((CONV_W, c), F32), vshape, vshape, vshape, vshape, wshape, wshape),
        grid=(RG_BLOCKS, nt),
        in_specs=[blk, blk, blk, tail, blk, tail, cwb, vec, wblk, wblk, wblk, wblk, vec, vec, vec],
        out_specs=(blk, blk, cwb, vec, vec, vec, vec, wblk, wblk),
        scratch_shapes=[pltpu.VMEM((1, cbw), F32), pltpu.VMEM((SUBLANES, cbw), F32),
                        pltpu.VMEM((tb, cbw), F32), pltpu.VMEM((tb, cbw), F32), pltpu.VMEM((tb, cbw), F32)],
        compiler_params=_params("parallel", "arbitrary"),
    )(dy, gate_br, x_br, x_br, hs, hs, cw, cb, wr, wi, wrt, wit, br, bi, lam)


ATT_BLOCK = 256
ATT_SCALE = 1.0 / math.sqrt(SB_HEAD_DIM)
N_PAIRS = SB_HEADS * SB_HEAD_DIM // LANES
NT_DIMS = (((1,), (1,)), ((), ()))
TN_DIMS = (((0,), (0,)), ((), ()))


LOG2E = 1.4426950408889634


def _neg_abs(x):
    bits = lax.bitcast_convert_type(x, jnp.uint32) | jnp.uint32(0x80000000)
    return lax.bitcast_convert_type(bits, F32)


def _qk(qx, kb):
    return lax.dot_general(qx, kb, NT_DIMS, preferred_element_type=F32)


def _sb_logits(qk, valid):
    z2 = qk * (ATT_SCALE * LOG2E)
    lb2 = jnp.minimum(z2, 0.0) - jnp.log2(1.0 + jnp.exp2(_neg_abs(z2)))
    l2 = lb2 - z2
    if valid is not None:
        l2 = jnp.where(valid, l2, 0.0)
    return lb2, l2


def _hi_lo(x):
    hi = x.astype(CD)
    lo = (x - hi.astype(F32)).astype(CD)
    return jnp.concatenate([hi, lo], axis=1)


def _tri(strict, stacked):
    r = lax.broadcasted_iota(jnp.int32, (ATT_BLOCK, ATT_BLOCK), 0)
    c = lax.broadcasted_iota(jnp.int32, (ATT_BLOCK, ATT_BLOCK), 1)
    m = (r > c if strict else r >= c).astype(CD)
    return jnp.concatenate([m, m], axis=0) if stacked else m


def _attn_fwd(qkv, name):
    _, s, _ = qkv.shape
    nblk = s // ATT_BLOCK
    t = ATT_BLOCK

    def body(q_ref, k_ref, v_ref, o_ref, qk_scr, w_scr):
        i = pl.program_id(1)
        lane = lax.broadcasted_iota(jnp.int32, (1, LANES), 1)
        head_masks = (lane < SB_HEAD_DIM, lane >= SB_HEAD_DIM)
        q = q_ref[0]
        qs = [jnp.where(m, q, jnp.zeros_like(q)) for m in head_masks]
        tri = _tri(True, False)
        rr = lax.broadcasted_iota(jnp.int32, (t, t), 0)
        cc = lax.broadcasted_iota(jnp.int32, (t, t), 1)
        diag_valid = cc < rr

        def rows_of(j):
            return pl.ds(pl.multiple_of(j * t, t), t)

        def start_logits(j):
            kb = k_ref[0, rows_of(j), :]
            for hd in range(2):
                qk_scr[hd] = _qk(qs[hd], kb)

        def weights(run, valid):
            new_run = []
            for hd in range(2):
                lb2, l2 = _sb_logits(qk_scr[hd], valid)
                w = jnp.exp2(lb2 + (run[hd] + jnp.dot(l2.astype(CD), tri, preferred_element_type=F32)))
                if valid is not None:
                    w = jnp.where(valid, w, 0.0)
                w_scr[:, hd * t:(hd + 1) * t] = w.astype(CD)
                new_run.append(run[hd] + jnp.sum(l2, axis=1, keepdims=True))
            return tuple(new_run)

        def apply_weights(j):
            vb = v_ref[0, rows_of(j), :]
            vcat = jnp.concatenate([jnp.where(m, vb, jnp.zeros_like(vb)) for m in head_masks], axis=0)
            return jnp.dot(w_scr[...], vcat, preferred_element_type=F32)

        zero = jnp.zeros((t, 1), F32)
        start_logits(i)
        run = weights((zero, zero), diag_valid)
        start_logits(jnp.maximum(i - 1, 0))

        def step(jj, carry):
            run, oacc = carry
            b = i - 1 - jj
            oacc = oacc + apply_weights(b + 1)
            run = weights(run, None)
            start_logits(jnp.maximum(b - 1, 0))
            return run, oacc

        run, oacc = lax.fori_loop(0, i, step, (run, jnp.zeros((t, LANES), F32)))
        o_ref[0] = oacc + apply_weights(0)

    return pl.pallas_call(
        body, name=name, out_shape=jax.ShapeDtypeStruct((N_PAIRS, s, LANES), F32), grid=(N_PAIRS, nblk),
        in_specs=[pl.BlockSpec((1, t, LANES), lambda p, i: (p, i, 0)),
                  pl.BlockSpec((1, s, LANES), lambda p, i: (N_PAIRS + p, 0, 0)),
                  pl.BlockSpec((1, s, LANES), lambda p, i: (2 * N_PAIRS + p, 0, 0))],
        out_specs=pl.BlockSpec((1, t, LANES), lambda p, i: (p, i, 0)),
        scratch_shapes=[pltpu.VMEM((2, t, t), F32), pltpu.VMEM((t, 2 * t), CD)],
        compiler_params=_params("parallel", "arbitrary"),
    )(qkv, qkv, qkv)


def _attn_bwd(qkv, o, do, name):
    _, s, _ = qkv.shape
    nblk = s // ATT_BLOCK
    t = ATT_BLOCK

    def body(q_ref, k_ref, v_ref, o_ref, do_ref, dq_ref, dk_ref, dv_ref, qk_scr, dw_scr, w_scr, dz_scr):
        i = pl.program_id(1)

        @pl.when(i == 0)
        def _():
            dk_ref[...] = jnp.zeros_like(dk_ref)
            dv_ref[...] = jnp.zeros_like(dv_ref)

        lane = lax.broadcasted_iota(jnp.int32, (1, LANES), 1)
        head_masks = (lane < SB_HEAD_DIM, lane >= SB_HEAD_DIM)
        q = q_ref[0]
        dov = do_ref[0]
        ov = o_ref[0]
        qs = [jnp.where(m, q, jnp.zeros_like(q)) for m in head_masks]
        q_scaled_t = jnp.concatenate([(qx.astype(F32) * ATT_SCALE).T for qx in qs], axis=1).astype(CD)
        docs = [jnp.where(m, dov, 0.0).astype(CD) for m in head_masks]
        docat_t = jnp.concatenate([jnp.where(m, dov, 0.0).T for m in head_masks], axis=1).astype(CD)
        totals = [jnp.sum(d.astype(F32) * ov, axis=1, keepdims=True) for d in docs]
        tri = _tri(True, False)
        tri_incl = _tri(False, True)
        rr = lax.broadcasted_iota(jnp.int32, (t, t), 0)
        cc = lax.broadcasted_iota(jnp.int32, (t, t), 1)
        diag_valid = cc < rr

        def rows_of(j):
            return pl.ds(pl.multiple_of(j * t, t), t)

        def start_products(j):
            kb = k_ref[0, rows_of(j), :]
            vb = v_ref[0, rows_of(j), :]
            for hd in range(2):
                qk_scr[hd] = _qk(qs[hd], kb)
                dw_scr[hd] = lax.dot_general(docs[hd], vb, NT_DIMS, preferred_element_type=F32)

        def logit_grads(run, erun, valid):
            new_run, new_erun = [], []
            for hd in range(2):
                lb2, l2 = _sb_logits(qk_scr[hd], valid)
                w = jnp.exp2(lb2 + (run[hd] + jnp.dot(l2.astype(CD), tri, preferred_element_type=F32)))
                if valid is not None:
                    w = jnp.where(valid, w, 0.0)
                wc = w.astype(CD)
                w_scr[hd * t:(hd + 1) * t, :] = wc
                e = dw_scr[hd] * wc.astype(F32)
                prefix = (totals[hd] - erun[hd]) - jnp.dot(_hi_lo(e), tri_incl, preferred_element_type=F32)
                dz = e - jnp.exp2(lb2) * (e + prefix)
                if valid is not None:
                    dz = jnp.where(valid, dz, 0.0)
                dz_scr[hd * t:(hd + 1) * t, :] = dz.astype(CD)
                new_run.append(run[hd] + jnp.sum(l2, axis=1, keepdims=True))
                new_erun.append(erun[hd] + jnp.sum(e, axis=1, keepdims=True))
            return tuple(new_run), tuple(new_erun)

        def apply_grads(j):
            rows = rows_of(j)
            kb = k_ref[0, rows, :]
            kcat = jnp.concatenate([jnp.where(m, kb, jnp.zeros_like(kb)) for m in head_masks], axis=0)
            dz2 = dz_scr[...]
            dk_ref[0, :, rows] += jnp.dot(q_scaled_t, dz2, preferred_element_type=F32)
            dv_ref[0, :, rows] += jnp.dot(docat_t, w_scr[...], preferred_element_type=F32)
            return jnp.dot(jnp.concatenate([dz2[:t], dz2[t:]], axis=1), kcat, preferred_element_type=F32)

        zero = jnp.zeros((t, 1), F32)
        start_products(i)
        run, erun = logit_grads((zero, zero), (zero, zero), diag_valid)
        start_products(jnp.maximum(i - 1, 0))

        def step(jj, carry):
            run, erun, dqacc = carry
            b = i - 1 - jj
            dqacc = dqacc + apply_grads(b + 1)
            run, erun = logit_grads(run, erun, None)
            start_products(jnp.maximum(b - 1, 0))
            return run, erun, dqacc

        run, erun, dqacc = lax.fori_loop(0, i, step, (run, erun, jnp.zeros((t, LANES), F32)))
        dq_ref[0] = (dqacc + apply_grads(0)) * ATT_SCALE

    qblk = pl.BlockSpec((1, t, LANES), lambda p, i: (p, i, 0))
    full = pl.BlockSpec((1, LANES, s), lambda p, i: (p, 0, 0))
    shape = jax.ShapeDtypeStruct((N_PAIRS, s, LANES), F32)
    shape_t = jax.ShapeDtypeStruct((N_PAIRS, LANES, s), F32)
    dq, dk_t, dv_t = pl.pallas_call(
        body, name=name, out_shape=(shape, shape_t, shape_t), grid=(N_PAIRS, nblk),
        in_specs=[qblk,
                  pl.BlockSpec((1, s, LANES), lambda p, i: (N_PAIRS + p, 0, 0)),
                  pl.BlockSpec((1, s, LANES), lambda p, i: (2 * N_PAIRS + p, 0, 0)),
                  qblk, qblk],
        out_specs=(qblk, full, full),
        scratch_shapes=[pltpu.VMEM((2, t, t), F32), pltpu.VMEM((2, t, t), F32),
                        pltpu.VMEM((2 * t, t), CD), pltpu.VMEM((2 * t, t), CD)],
        compiler_params=_params("parallel", "arbitrary"),
    )(qkv, qkv, qkv, o, do)
    return dq, jnp.swapaxes(dk_t, 1, 2), jnp.swapaxes(dv_t, 1, 2)


ADAM_COLS = 1024


def _adamw(w, g, m, v, name):
    shape = w.shape
    rows, cols = (shape[-2], shape[-1]) if len(shape) >= 2 else (1, shape[-1])
    lead = w.size // (rows * cols)
    tr = _pick(rows, (512, 256, 128, 64, 32, 16, 8))

    def body(w_ref, g_ref, m_ref, v_ref, d_ref, nm_ref, nv_ref):
        gv = g_ref[...]
        nm = ADAM_B1 * m_ref[...] + (1.0 - ADAM_B1) * gv
        nv = ADAM_B2 * v_ref[...] + (1.0 - ADAM_B2) * (gv * gv)
        m_hat = nm / (1.0 - ADAM_B1 ** ADAM_STEP)
        v_hat = nv / (1.0 - ADAM_B2 ** ADAM_STEP)
        d_ref[...] = -ADAM_LR * (m_hat / (jnp.sqrt(v_hat) + ADAM_EPS) + ADAM_WD * w_ref[...])
        nm_ref[...] = nm
        nv_ref[...] = nv

    blk = pl.BlockSpec((1, tr, cols), lambda l, i: (l, i, 0))
    out = jax.ShapeDtypeStruct((lead, rows, cols), F32)
    d, nm, nv = pl.pallas_call(
        body, name=name, out_shape=(out, out, out), grid=(lead, rows // tr),
        in_specs=[blk, blk, blk, blk], out_specs=(blk, blk, blk), compiler_params=_params("parallel", "parallel"),
    )(*[a.reshape(lead, rows, cols) for a in (w, g, m, v)])
    return d.reshape(shape), nm.reshape(shape), nv.reshape(shape)


HBM = pl.BlockSpec(memory_space=pltpu.HBM)


def _coords():
    return lax.axis_index("x"), lax.axis_index("y"), lax.axis_index("c")


def _other_chips(x, y):
    return [(1 - x, y), (x, 1 - y), (1 - x, 1 - y)]


def _allgather_chips(shard, name):
    r, cols = shard.shape
    half = r // 2

    def body(src_ref, out_ref, send_sems, recv_sems):
        x, y, c = _coords()
        sibling = (x, y, 1 - c)
        chips = _other_chips(x, y)

        def rows(px, py, h):
            return out_ref.at[2 * px + py, pl.ds(h * half, half), :]

        def copy(k, block, to, src=None):
            return pltpu.make_async_remote_copy(
                src_ref=rows(*block) if src is None else src, dst_ref=rows(*block),
                send_sem=send_sems.at[k], recv_sem=recv_sems.at[k], device_id=to, device_id_type=MESH)

        my_half = src_ref.at[pl.ds(c * half, half), :]
        first = [copy(j, (x, y, c), (*chip, c), src=my_half) for j, chip in enumerate(chips)]
        for cp in first:
            cp.start()
        passed = [copy(3 + j, (*chip, c), sibling) for j, chip in enumerate(chips)]
        for j, chip in enumerate(chips):
            copy(j, (*chip, c), (x, y, c)).wait_recv()
            passed[j].start()
        for j, chip in enumerate(chips):
            copy(3 + j, (*chip, 1 - c), (x, y, c)).wait_recv()
        for cp in first + passed:
            cp.wait_send()

    gathered = pl.pallas_call(
        body, name=name, out_shape=jax.ShapeDtypeStruct((N_CHIPS, r, cols), shard.dtype),
        in_specs=[HBM], out_specs=HBM,
        scratch_shapes=[pltpu.SemaphoreType.DMA((6,)), pltpu.SemaphoreType.DMA((6,))],
    )(shard)
    chip = 2 * lax.axis_index("x") + lax.axis_index("y")
    return lax.dynamic_update_slice(gathered, shard[None], (chip, 0, 0))


def _exchange_sibling_halves(g, name):
    n, r, cols = g.shape
    half = r // 2

    def body(g_ref, out_ref, send_sem, recv_sem):
        x, y, c = _coords()
        cp = pltpu.make_async_remote_copy(
            src_ref=g_ref.at[:, pl.ds((1 - c) * half, half), :], dst_ref=out_ref,
            send_sem=send_sem, recv_sem=recv_sem, device_id=(x, y, 1 - c), device_id_type=MESH)
        cp.start()
        cp.wait()

    return pl.pallas_call(
        body, name=name, out_shape=jax.ShapeDtypeStruct((n, half, cols), g.dtype),
        in_specs=[HBM], out_specs=HBM,
        scratch_shapes=[pltpu.SemaphoreType.DMA, pltpu.SemaphoreType.DMA],
    )(g)


def _scatter_to_chips(p, name):
    n, h, cols = p.shape

    def body(p_ref, out_ref, send_sems, recv_sems, local_sem):
        x, y, c = _coords()
        me = 2 * x + y
        mine = pltpu.make_async_copy(p_ref.at[me], out_ref.at[me], local_sem)
        mine.start()
        sends = []
        for j, (px, py) in enumerate(_other_chips(x, y)):
            sends.append(pltpu.make_async_remote_copy(
                src_ref=p_ref.at[2 * px + py], dst_ref=out_ref.at[me],
                send_sem=send_sems.at[j], recv_sem=recv_sems.at[j], device_id=(px, py, c), device_id_type=MESH))
        for cp in sends:
            cp.start()
        for j, (px, py) in enumerate(_other_chips(x, y)):
            pltpu.make_async_remote_copy(
                src_ref=p_ref.at[me], dst_ref=out_ref.at[2 * px + py],
                send_sem=send_sems.at[j], recv_sem=recv_sems.at[j], device_id=(px, py, c),
                device_id_type=MESH).wait_recv()
        for cp in sends:
            cp.wait_send()
        mine.wait()

    return pl.pallas_call(
        body, name=name, out_shape=jax.ShapeDtypeStruct((n, h, cols), p.dtype),
        in_specs=[HBM], out_specs=HBM,
        scratch_shapes=[pltpu.SemaphoreType.DMA((3,)), pltpu.SemaphoreType.DMA((3,)), pltpu.SemaphoreType.DMA],
    )(p)


def _share_halves(v, name):
    h = v.shape[0] // 2

    def body(v_ref, out_ref, send_sem, recv_sem):
        x, y, c = _coords()
        cp = pltpu.make_async_remote_copy(
            src_ref=v_ref.at[pl.ds(c * h, h), :], dst_ref=out_ref.at[pl.ds(c * h, h), :],
            send_sem=send_sem, recv_sem=recv_sem, device_id=(x, y, 1 - c), device_id_type=MESH)
        cp.start()
        pltpu.make_async_remote_copy(
            src_ref=v_ref.at[pl.ds(c * h, h), :], dst_ref=out_ref.at[pl.ds((1 - c) * h, h), :],
            send_sem=send_sem, recv_sem=recv_sem, device_id=(x, y, 1 - c), device_id_type=MESH).wait_recv()
        cp.wait_send()

    return pl.pallas_call(
        body, name=name, out_shape=jax.ShapeDtypeStruct(v.shape, v.dtype),
        in_specs=[HBM], out_specs=HBM, input_output_aliases={0: 0},
        scratch_shapes=[pltpu.SemaphoreType.DMA, pltpu.SemaphoreType.DMA],
    )(v)


def _allreduce_small(v, name):
    r, cols = v.shape

    def body(v_ref, out_ref, buf_ref, send_sems, recv_sems):
        x, y, c = _coords()
        me = 4 * x + 2 * y + c
        buf_ref[me] = v_ref[...]
        sends = []
        for k in range(1, N_DEV):
            px = 1 - x if k & 4 else x
            py = 1 - y if k & 2 else y
            pc = 1 - c if k & 1 else c
            sends.append(pltpu.make_async_remote_copy(
                src_ref=v_ref, dst_ref=buf_ref.at[me], send_sem=send_sems.at[k - 1], recv_sem=recv_sems.at[k - 1],
                device_id=(px, py, pc), device_id_type=MESH))
        for cp in sends:
            cp.start()
        for cp in sends:
            cp.wait()
        acc = buf_ref[0]
        for d in range(1, N_DEV):
            acc = acc + buf_ref[d]
        out_ref[...] = acc

    return pl.pallas_call(
        body, name=name, out_shape=jax.ShapeDtypeStruct((r, cols), F32),
        in_specs=[pl.BlockSpec(memory_space=pltpu.VMEM)], out_specs=pl.BlockSpec(memory_space=pltpu.VMEM),
        scratch_shapes=[pltpu.VMEM((N_DEV, r, cols), F32), pltpu.SemaphoreType.DMA((N_DEV - 1,)),
                        pltpu.SemaphoreType.DMA((N_DEV - 1,))],
    )(v)


def _add_sibling(g, from_sibling, core, name):
    n, h, cols = from_sibling.shape
    tr = _pick(h, (512, 256, 128))
    steps = h // tr

    def body(core_ref, a_ref, b_ref, o_ref):
        o_ref[...] = (a_ref[...] + b_ref[...]).astype(o_ref.dtype)

    return pl.pallas_call(
        body, name=name, out_shape=jax.ShapeDtypeStruct(from_sibling.shape, jnp.bfloat16),
        grid_spec=pltpu.PrefetchScalarGridSpec(
            num_scalar_prefetch=1, grid=(n, steps),
            in_specs=[pl.BlockSpec((1, tr, cols), lambda s, i, core_ref: (s, core_ref[0] * steps + i, 0)),
                      pl.BlockSpec((1, tr, cols), lambda s, i, core_ref: (s, i, 0))],
            out_specs=pl.BlockSpec((1, tr, cols), lambda s, i, core_ref: (s, i, 0))),
        compiler_params=_params("parallel", "parallel"),
    )(core.reshape(1).astype(jnp.int32), g, from_sibling)


def _sum_slots(p, core, name):
    n, r, cols = p.shape
    tr = _pick(r, (512, 256, 128))
    steps = r // tr

    def body(core_ref, p_ref, o_ref):
        o_ref[...] = ((p_ref[0].astype(F32) + p_ref[1].astype(F32)) + p_ref[2].astype(F32)) + p_ref[3].astype(F32)

    return pl.pallas_call(
        body, name=name, out_shape=jax.ShapeDtypeStruct((2 * r, cols), F32),
        grid_spec=pltpu.PrefetchScalarGridSpec(
            num_scalar_prefetch=1, grid=(steps,),
            in_specs=[pl.BlockSpec((n, tr, cols), lambda i, core_ref: (0, i, 0))],
            out_specs=pl.BlockSpec((tr, cols), lambda i, core_ref: (core_ref[0] * steps + i, 0))),
        compiler_params=_params("parallel"),
    )(core.reshape(1).astype(jnp.int32), p)


PACK_COLS = 1024


def _pack_shards(parts):
    return jnp.concatenate([p.reshape(-1, PACK_COLS) for p in parts], axis=0)


def _unpack_shards(buf, shapes):
    out, row = [], 0
    for shp in shapes:
        nrows = math.prod(shp) // PACK_COLS
        out.append(buf[..., row:row + nrows, :].reshape(buf.shape[:-2] + tuple(shp)))
        row += nrows
    return out


def _local_step(x, target, w):
    t = lambda a: a.T
    g = {}
    h0 = _rms_fwd(x, w["norm_mix_g"][0], "rms_mix0")
    w_in_g, w_in_x = w["a_w_in"][:, :D_RNN], w["a_w_in"][:, D_RNN:]
    gate_br = _matmul([(h0, w_in_g)], F32, "mm_a_gate")
    x_br = _matmul([(h0, w_in_x)], F32, "mm_a_xbr")
    y_a, hs = _rglru_fwd(gate_br, x_br, w["a_conv_w"], w["a_conv_b"], w["a_w_r"], w["a_w_i"], w["a_b_r"],
                         w["a_b_i"], w["a_lambda"], "rglru_fwd")
    x1 = _matmul([(y_a, w["a_w_out"])], F32, "mm_a_out", addend=x)
    h1 = _rms_fwd(x1, w["norm_ffn_g"][0], "rms_ffn0")
    fg0, fu0, act0 = _ffn_up(h1, w["ffn_w_gate"][0], w["ffn_w_up"][0], "ffn0_up")
    x2 = _matmul([(act0, w["ffn_w_down"][0])], F32, "mm_f0_down", addend=x1)
    h2 = _rms_fwd(x2, w["norm_mix_g"][1], "rms_mix1")
    qkv = _matmul([(h2, w["b_w_qkv"])], CD, "mm_b_qkv", out_lbm=True, tn=1024)
    o = _attn_fwd(qkv, "attn_fwd")
    x3 = _matmul([(o, w["b_w_out"])], F32, "mm_b_out", a_lbm=True, addend=x2)
    h3 = _rms_fwd(x3, w["norm_ffn_g"][1], "rms_ffn1")
    fg1, fu1, act1 = _ffn_up(h3, w["ffn_w_gate"][1], w["ffn_w_up"][1], "ffn1_up")
    x4 = _matmul([(act1, w["ffn_w_down"][1])], F32, "mm_f1_down", addend=x3)
    loss, dx4, dx4c, g["final_g"] = _loss_head(x4, w["final_g"], target, "loss_head")

    def ffn_bwd(dx_out, dxc, h, x_in, fg, fu, act, layer, tag):
        dg, du = _ffn_dact(dxc, t(w["ffn_w_down"][layer]), fg, fu, "ffn_" + tag + "_dact")
        dwd = _matmul([(act, dxc)], F32, "mm_" + tag + "_dwd", trans_a=True)
        dwg = _matmul([(h, dg)], F32, "mm_" + tag + "_dwg", trans_a=True)
        dwu = _matmul([(h, du)], F32, "mm_" + tag + "_dwu", trans_a=True)
        dh = _matmul([(dg, t(w["ffn_w_gate"][layer])), (du, t(w["ffn_w_up"][layer]))], F32, "mm_" + tag + "_dh")
        dx_in, dx_in_c, dgain = _rms_bwd(dh, x_in, w["norm_ffn_g"][layer], dx_out, "rms_ffn" + tag + "_bwd")
        return dx_in, dx_in_c, dgain, dwg, dwu, dwd

    dx3, dx3c, dgf1, dwg1, dwu1, dwd1 = ffn_bwd(dx4, dx4c, h3, x3, fg1, fu1, act1, 1, "f1")
    do = _matmul([(dx3c, t(w["b_w_out"]))], F32, "mm_b_do", out_lbm=True, tn=1024)
    g["b_w_out"] = _matmul([(o, dx3c)], F32, "mm_b_dwout", trans_a=True, a_lbm=True)
    dq, dk, dv = _attn_bwd(qkv, o, do, "attn_bwd")
    wq_t = t(w["b_w_qkv"])
    parts = (dq, dk, dv)
    g["b_w_qkv"] = jnp.concatenate(
        [_matmul([(h2, p)], F32, "mm_b_dwqkv%d" % n, trans_a=True, b_lbm=True) for n, p in enumerate(parts)], axis=1)
    dh2 = _matmul([(p, wq_t[n * D_MODEL:(n + 1) * D_MODEL]) for n, p in enumerate(parts)], F32, "mm_b_dh",
                  a_lbm=True)
    dx2, dx2c, dgm1 = _rms_bwd(dh2, x2, w["norm_mix_g"][1], dx3, "rms_mix1_bwd")
    dx1, dx1c, dgf0, dwg0, dwu0, dwd0 = ffn_bwd(dx2, dx2c, h1, x1, fg0, fu0, act0, 0, "f0")
    dy_a = _matmul([(dx1c, t(w["a_w_out"]))], F32, "mm_a_dy")
    g["a_w_out"] = _matmul([(y_a, dx1c)], F32, "mm_a_dwout", trans_a=True)
    wrt = jnp.swapaxes(w["a_w_r"], 1, 2)
    wit = jnp.swapaxes(w["a_w_i"], 1, 2)
    (dgate, dxbr, g["a_conv_w"], g["a_conv_b"], g["a_b_r"], g["a_b_i"], g["a_lambda"], g["a_w_r"],
     g["a_w_i"]) = _rglru_bwd(dy_a, gate_br, x_br, hs, w["a_conv_w"], w["a_conv_b"], w["a_w_r"], w["a_w_i"], wrt, wit,
                              w["a_b_r"], w["a_b_i"], w["a_lambda"], "rglru_bwd")
    g["a_w_in"] = jnp.concatenate([_matmul([(h0, dgate)], F32, "mm_a_dwin_g", trans_a=True),
                                   _matmul([(h0, dxbr)], F32, "mm_a_dwin_x", trans_a=True)], axis=1)
    dh0 = _matmul([(dgate, t(w_in_g)), (dxbr, t(w_in_x))], F32, "mm_a_dh")
    dx0, _, dgm0 = _rms_bwd(dh0, x, w["norm_mix_g"][0], dx1, "rms_mix0_bwd")
    g["norm_mix_g"] = jnp.concatenate([dgm0, dgm1], axis=0)
    g["norm_ffn_g"] = jnp.concatenate([dgf0, dgf1], axis=0)
    g["ffn_w_gate"] = jnp.stack([dwg0, dwg1])
    g["ffn_w_up"] = jnp.stack([dwu0, dwu1])
    g["ffn_w_down"] = jnp.stack([dwd0, dwd1])
    return loss, dx0, g


WEIGHTS = ["norm_mix_g", "norm_ffn_g", "a_w_in", "a_conv_w", "a_conv_b", "a_w_r", "a_b_r", "a_w_i", "a_b_i",
           "a_lambda", "a_w_out", "b_w_qkv", "b_w_out", "ffn_w_gate", "ffn_w_up", "ffn_w_down", "final_g"]
BIG = [("a_w_in", 2), ("a_w_r", 2), ("a_w_i", 2), ("a_w_out", 1), ("b_w_qkv", 2), ("b_w_out", 1),
       ("ffn_w_gate", 2), ("ffn_w_up", 2), ("ffn_w_down", 1)]
SMALL = ["norm_mix_g", "norm_ffn_g", "a_conv_w", "a_conv_b", "a_b_r", "a_b_i", "a_lambda", "final_g"]


def _join_chips(stack, axis):
    return jnp.concatenate([stack[s] for s in range(N_CHIPS)], axis=axis)


def _split_chips(full, axis):
    return jnp.stack(jnp.split(full, N_CHIPS, axis=axis))


def _step(x, target, weights, moments_m, moments_v):
    chip = 2 * lax.axis_index("x") + lax.axis_index("y")
    core = lax.axis_index("c")
    shard_shapes = [weights[n].shape for n, _ in BIG]
    packed = _pack_shards([weights[n].astype(CD) for n, _ in BIG])
    gathered = _allgather_chips(packed, "allgather_weights")
    full = {}
    for (n, axis), stack in zip(BIG, _unpack_shards(gathered, shard_shapes)):
        joined = _join_chips(stack, axis)
        full[n] = joined[0] if joined.shape[0] == 1 else joined
    cw_rows = jnp.zeros((N_CHIPS, CONV_W, RG_BW), F32)
    cw_rows = lax.dynamic_update_slice(cw_rows, jnp.where(core == 0, weights["a_conv_w"], 0.0), (chip, 0, 0))
    cw_all = _allreduce_small(cw_rows.reshape(-1, LANES), "allgather_conv_w").reshape(N_CHIPS, CONV_W, RG_BW)
    full["a_conv_w"] = jnp.concatenate([cw_all[s] for s in range(N_CHIPS)], axis=1)
    for n in ("norm_mix_g", "norm_ffn_g", "final_g"):
        full[n] = weights[n]
    for n in ("a_conv_b", "a_b_r", "a_b_i", "a_lambda"):
        full[n] = weights[n]
    loss, dx, grads = _local_step(x[0], target[0], full)
    small_parts = [grads[n].reshape(-1) for n in SMALL] + [loss.reshape(-1)]
    sizes = [p.shape[0] for p in small_parts]
    small = _allreduce_small(jnp.concatenate(small_parts).reshape(-1, LANES), "allreduce_small").reshape(-1)
    red, pos = {}, 0
    for n, sz in zip(SMALL + ["loss"], sizes):
        red[n] = small[pos:pos + sz]
        pos += sz
    loss_out = red["loss"][0]
    g_out = {}
    for n in SMALL:
        if n == "a_conv_w":
            g_out[n] = lax.dynamic_slice(red[n].reshape(CONV_W, D_RNN), (0, chip * RG_BW), (CONV_W, RG_BW)).reshape(
                weights[n].shape)
        else:
            g_out[n] = red[n].reshape(weights[n].shape)
    stacks = []
    for n, axis in BIG:
        gfull = grads[n].reshape((1,) + grads[n].shape) if grads[n].ndim == len(weights[n].shape) - 1 else grads[n]
        stacks.append(_split_chips(gfull, axis).reshape(N_CHIPS, -1, PACK_COLS))
    gbuf = jnp.concatenate(stacks, axis=1)
    from_sibling = _exchange_sibling_halves(gbuf, "rs_sibling")
    chip_partial = _add_sibling(gbuf, from_sibling, core, "rs_add_sibling")
    from_chips = _scatter_to_chips(chip_partial, "rs_chips")
    reduced = _share_halves(_sum_slots(from_chips, core, "rs_sum_chips"), "rs_share")
    for (n, _), gsh in zip(BIG, _unpack_shards(reduced, shard_shapes)):
        g_out[n] = gsh
    outs_g, outs_d, outs_m, outs_v = [], [], [], []
    for n in WEIGHTS:
        d, nm, nv = _adamw(weights[n], g_out[n], moments_m[n], moments_v[n], "adamw_" + n)
        outs_g.append(g_out[n])
        outs_d.append(d)
        outs_m.append(nm)
        outs_v.append(nv)
    return (loss_out, dx[None], *outs_g, *outs_d, *outs_m, *outs_v)


def kernel(x, norm_mix_g, norm_ffn_g, a_w_in, a_conv_w, a_conv_b, a_w_r, a_b_r, a_w_i, a_b_i, a_lambda, a_w_out, b_w_qkv, b_w_out, ffn_w_gate, ffn_w_up, ffn_w_down, final_g, loss_target, m_norm_mix_g, m_norm_ffn_g, m_a_w_in, m_a_conv_w, m_a_conv_b, m_a_w_r, m_a_b_r, m_a_w_i, m_a_b_i, m_a_lambda, m_a_w_out, m_b_w_qkv, m_b_w_out, m_ffn_w_gate, m_ffn_w_up, m_ffn_w_down, m_final_g, v_norm_mix_g, v_norm_ffn_g, v_a_w_in, v_a_conv_w, v_a_conv_b, v_a_w_r, v_a_b_r, v_a_w_i, v_a_b_i, v_a_lambda, v_a_w_out, v_b_w_qkv, v_b_w_out, v_ffn_w_gate, v_ffn_w_up, v_ffn_w_down, v_final_g):
    ws = [norm_mix_g, norm_ffn_g, a_w_in, a_conv_w, a_conv_b, a_w_r, a_b_r, a_w_i, a_b_i, a_lambda, a_w_out, b_w_qkv,
          b_w_out, ffn_w_gate, ffn_w_up, ffn_w_down, final_g]
    ms = [m_norm_mix_g, m_norm_ffn_g, m_a_w_in, m_a_conv_w, m_a_conv_b, m_a_w_r, m_a_b_r, m_a_w_i, m_a_b_i, m_a_lambda,
          m_a_w_out, m_b_w_qkv, m_b_w_out, m_ffn_w_gate, m_ffn_w_up, m_ffn_w_down, m_final_g]
    vs = [v_norm_mix_g, v_norm_ffn_g, v_a_w_in, v_a_conv_w, v_a_conv_b, v_a_w_r, v_a_b_r, v_a_w_i, v_a_b_i, v_a_lambda,
          v_a_w_out, v_b_w_qkv, v_b_w_out, v_ffn_w_gate, v_ffn_w_up, v_ffn_w_down, v_final_g]
    return _step(x, loss_target, dict(zip(WEIGHTS, ws)), dict(zip(WEIGHTS, ms)), dict(zip(WEIGHTS, vs)))
```

```python
import functools
import math

import jax
import jax.numpy as jnp
from jax import lax
from jax.experimental import pallas as pl
from jax.experimental.pallas import tpu as pltpu

F32 = jnp.float32
CD = jnp.bfloat16

D_MODEL = 1024
D_RNN = 1024
RG_BLOCKS = 4
RG_BW = 256
CONV_W = 4
RG_C = 8.0
SB_HEADS = 16
SB_HEAD_DIM = 64
D_FF = 2816
RMS_EPS = 1e-6
N_CHIPS = 4
N_DEV = 8

ADAM_LR = 0.001
ADAM_B1 = 0.9
ADAM_B2 = 0.999
ADAM_EPS = 1e-08
ADAM_WD = 0.01
ADAM_STEP = 10

LANES = 128
VMEM_LIMIT = 56 * 1024 * 1024
MESH = pl.DeviceIdType.MESH


def _params(*sem):
    return pltpu.CompilerParams(dimension_semantics=sem, vmem_limit_bytes=VMEM_LIMIT)


def _pick(n, prefs):
    for p in prefs:
        if n % p == 0:
            return p
    return n


def _matmul(pairs, out_dtype, name, *, trans_a=False, a_lbm=False, b_lbm=False, out_lbm=False, addend=None,
            tm=512, tn=None, tk=None):
    a0, b0 = pairs[0]
    if trans_a:
        kdim = a0.shape[1] if a_lbm else a0.shape[0]
        m = a0.shape[0] * LANES if a_lbm else a0.shape[1]
    else:
        m = a0.shape[1] if a_lbm else a0.shape[0]
        kdim = a0.shape[0] * LANES if a_lbm else a0.shape[1]
    n = b0.shape[0] * LANES if b_lbm else b0.shape[1]
    tm = _pick(m, (tm, 1408, 256, 128))
    tn = tn or _pick(n, (1408, 1024, 768, 512, 256, 128))
    tk = tk or _pick(kdim, (1024, 1408, 512, 256, 128))
    nk = kdim // tk
    npair = len(pairs)

    def cat(ref):
        return jnp.concatenate([ref[p] for p in range(ref.shape[0])], axis=-1)

    def body(*refs):
        ins = refs[: 2 * npair]
        pos = 2 * npair
        add_ref = None
        if addend is not None:
            add_ref = refs[pos]
            pos += 1
        o_ref = refs[pos]
        acc_ref = refs[pos + 1]
        k = pl.program_id(2)

        @pl.when(k == 0)
        def _():
            acc_ref[...] = jnp.zeros_like(acc_ref)

        acc = acc_ref[...]
        for p in range(npair):
            a = (cat(ins[2 * p]) if a_lbm else ins[2 * p][...]).astype(CD)
            b = (cat(ins[2 * p + 1]) if b_lbm else ins[2 * p + 1][...]).astype(CD)
            dims = (((0,), (0,)), ((), ())) if trans_a else (((1,), (0,)), ((), ()))
            acc = acc + lax.dot_general(a, b, dims, preferred_element_type=F32)
        acc_ref[...] = acc

        @pl.when(k == nk - 1)
        def _():
            res = acc_ref[...]
            if add_ref is not None:
                res = res + add_ref[...]
            res = res.astype(out_dtype)
            if out_lbm:
                for p in range(tn // LANES):
                    o_ref[p] = res[:, p * LANES:(p + 1) * LANES]
            else:
                o_ref[...] = res

    if trans_a:
        a_spec = (pl.BlockSpec((tm // LANES, tk, LANES), lambda i, j, k: (i, k, 0)) if a_lbm
                  else pl.BlockSpec((tk, tm), lambda i, j, k: (k, i)))
    else:
        a_spec = (pl.BlockSpec((tk // LANES, tm, LANES), lambda i, j, k: (k, i, 0)) if a_lbm
                  else pl.BlockSpec((tm, tk), lambda i, j, k: (i, k)))
    b_spec = (pl.BlockSpec((tn // LANES, tk, LANES), lambda i, j, k: (j, k, 0)) if b_lbm
              else pl.BlockSpec((tk, tn), lambda i, j, k: (k, j)))
    in_specs = []
    args = []
    for a, b in pairs:
        in_specs += [a_spec, b_spec]
        args += [a, b]
    if addend is not None:
        in_specs.append(pl.BlockSpec((tm, tn), lambda i, j, k: (i, j)))
        args.append(addend)
    if out_lbm:
        out_shape = jax.ShapeDtypeStruct((n // LANES, m, LANES), out_dtype)
        out_spec = pl.BlockSpec((tn // LANES, tm, LANES), lambda i, j, k: (j, i, 0))
    else:
        out_shape = jax.ShapeDtypeStruct((m, n), out_dtype)
        out_spec = pl.BlockSpec((tm, tn), lambda i, j, k: (i, j))
    return pl.pallas_call(
        body, name=name, out_shape=out_shape, grid=(m // tm, n // tn, nk),
        in_specs=in_specs, out_specs=out_spec,
        scratch_shapes=[pltpu.VMEM((tm, tn), F32)],
        compiler_params=_params("parallel", "parallel", "arbitrary"),
    )(*args)


ROW_BLOCK = 256


def _rms_fwd(x, g, name):
    s, d = x.shape

    def body(x_ref, g_ref, h_ref):
        xv = x_ref[...]
        rinv = lax.rsqrt(jnp.mean(xv * xv, axis=-1, keepdims=True) + RMS_EPS)
        h_ref[...] = (xv * rinv * g_ref[...]).astype(CD)

    return pl.pallas_call(
        body, name=name, out_shape=jax.ShapeDtypeStruct((s, d), CD), grid=(s // ROW_BLOCK,),
        in_specs=[pl.BlockSpec((ROW_BLOCK, d), lambda i: (i, 0)), pl.BlockSpec((1, d), lambda i: (0, 0))],
        out_specs=pl.BlockSpec((ROW_BLOCK, d), lambda i: (i, 0)),
        compiler_params=_params("parallel"),
    )(x, g.reshape(1, d))


def _rms_bwd(dh, x, g, dx_in, name):
    s, d = x.shape

    def body(dh_ref, x_ref, g_ref, dxin_ref, dx_ref, dxc_ref, dg_ref):
        @pl.when(pl.program_id(0) == 0)
        def _():
            dg_ref[...] = jnp.zeros_like(dg_ref)

        xv = x_ref[...]
        dhv = dh_ref[...]
        rinv = lax.rsqrt(jnp.mean(xv * xv, axis=-1, keepdims=True) + RMS_EPS)
        nrm = xv * rinv
        dn = dhv * g_ref[...]
        dx = dxin_ref[...] + rinv * (dn - nrm * jnp.mean(dn * nrm, axis=-1, keepdims=True))
        dx_ref[...] = dx
        dxc_ref[...] = dx.astype(CD)
        dg_ref[...] += jnp.sum(dhv * nrm, axis=0, keepdims=True)

    row = pl.BlockSpec((ROW_BLOCK, d), lambda i: (i, 0))
    vec = pl.BlockSpec((1, d), lambda i: (0, 0))
    return pl.pallas_call(
        body, name=name,
        out_shape=(jax.ShapeDtypeStruct((s, d), F32), jax.ShapeDtypeStruct((s, d), CD),
                   jax.ShapeDtypeStruct((1, d), F32)),
        grid=(s // ROW_BLOCK,), in_specs=[row, row, vec, row], out_specs=(row, row, vec),
        compiler_params=_params("arbitrary"),
    )(dh, x, g.reshape(1, d), dx_in)


def _loss_head(x, g, target, name):
    s, d = x.shape

    def body(x_ref, g_ref, t_ref, loss_ref, dx_ref, dxc_ref, dg_ref):
        @pl.when(pl.program_id(0) == 0)
        def _():
            dg_ref[...] = jnp.zeros_like(dg_ref)
            loss_ref[...] = jnp.zeros_like(loss_ref)

        xv = x_ref[...]
        gv = g_ref[...]
        rinv = lax.rsqrt(jnp.mean(xv * xv, axis=-1, keepdims=True) + RMS_EPS)
        nrm = xv * rinv
        err = nrm * gv - t_ref[...]
        loss_ref[...] += 0.5 * jnp.sum(jnp.mean(err * err, axis=-1, keepdims=True), axis=0, keepdims=True)
        dy = err * (1.0 / d)
        dn = dy * gv
        dx = rinv * (dn - nrm * jnp.mean(dn * nrm, axis=-1, keepdims=True))
        dx_ref[...] = dx
        dxc_ref[...] = dx.astype(CD)
        dg_ref[...] += jnp.sum(dy * nrm, axis=0, keepdims=True)

    row = pl.BlockSpec((ROW_BLOCK, d), lambda i: (i, 0))
    vec = pl.BlockSpec((1, d), lambda i: (0, 0))
    return pl.pallas_call(
        body, name=name,
        out_shape=(jax.ShapeDtypeStruct((1, LANES), F32), jax.ShapeDtypeStruct((s, d), F32),
                   jax.ShapeDtypeStruct((s, d), CD), jax.ShapeDtypeStruct((1, d), F32)),
        grid=(s // ROW_BLOCK,), in_specs=[row, vec, row],
        out_specs=(pl.BlockSpec((1, LANES), lambda i: (0, 0)), row, row, vec),
        compiler_params=_params("arbitrary"),
    )(x, g.reshape(1, d), target)


def _sigmoid(z):
    return 1.0 / (1.0 + jnp.exp(-z))


FFN_TM = 512
FFN_TN = 1408


def _ffn_up(h, wg, wu, name):
    s, d = h.shape
    f = wg.shape[1]
    tm = _pick(s, (FFN_TM, 256))

    def body(h_ref, wg_ref, wu_ref, g_ref, u_ref, a_ref):
        hv = h_ref[...]
        gv = jnp.dot(hv, wg_ref[...], preferred_element_type=F32)
        uv = jnp.dot(hv, wu_ref[...], preferred_element_type=F32)
        g_ref[...] = gv
        u_ref[...] = uv
        a_ref[...] = (gv * _sigmoid(gv) * uv).astype(CD)

    a_spec = pl.BlockSpec((tm, d), lambda i, j: (i, 0))
    w_spec = pl.BlockSpec((d, FFN_TN), lambda i, j: (0, j))
    o_spec = pl.BlockSpec((tm, FFN_TN), lambda i, j: (i, j))
    return pl.pallas_call(
        body, name=name,
        out_shape=(jax.ShapeDtypeStruct((s, f), F32), jax.ShapeDtypeStruct((s, f), F32),
                   jax.ShapeDtypeStruct((s, f), CD)),
        grid=(s // tm, f // FFN_TN), in_specs=[a_spec, w_spec, w_spec], out_specs=(o_spec, o_spec, o_spec),
        compiler_params=_params("parallel", "parallel"),
    )(h, wg, wu)


def _ffn_dact(dxc, wd_t, g, u, name):
    s, d = dxc.shape
    f = wd_t.shape[1]
    tm = _pick(s, (FFN_TM, 256))

    def body(dx_ref, w_ref, g_ref, u_ref, dg_ref, du_ref):
        da = jnp.dot(dx_ref[...], w_ref[...], preferred_element_type=F32)
        gv = g_ref[...]
        sg = _sigmoid(gv)
        silu = gv * sg
        dg_ref[...] = (da * u_ref[...] * (sg + silu * (1.0 - sg))).astype(CD)
        du_ref[...] = (da * silu).astype(CD)

    a_spec = pl.BlockSpec((tm, d), lambda i, j: (i, 0))
    w_spec = pl.BlockSpec((d, FFN_TN), lambda i, j: (0, j))
    o_spec = pl.BlockSpec((tm, FFN_TN), lambda i, j: (i, j))
    return pl.pallas_call(
        body, name=name,
        out_shape=(jax.ShapeDtypeStruct((s, f), CD), jax.ShapeDtypeStruct((s, f), CD)),
        grid=(s // tm, f // FFN_TN), in_specs=[a_spec, w_spec, o_spec, o_spec], out_specs=(o_spec, o_spec),
        compiler_params=_params("parallel", "parallel"),
    )(dxc, wd_t, g, u)


TIME_BLOCK = 256
SUBLANES = 8
GELU_C = math.sqrt(2.0 / math.pi)
GELU_A = 0.044715


def _gelu(x):
    return 0.5 * x * (1.0 + jnp.tanh(GELU_C * (x + GELU_A * x * x * x)))


def _gelu_grad(x):
    t = jnp.tanh(GELU_C * (x + GELU_A * x * x * x))
    return 0.5 * (1.0 + t) + 0.5 * x * (1.0 - t * t) * GELU_C * (1.0 + 3.0 * GELU_A * x * x)


def _neg_expm1(x):
    series = -x * (1.0 + x * (0.5 + x * (1.0 / 6.0 + x * (1.0 / 24.0))))
    return jnp.where(x > -0.05, series, 1.0 - jnp.exp(x))


def _log_sigmoid(x):
    return jnp.minimum(x, 0.0) - jnp.log1p(jnp.exp(-jnp.abs(x)))


def _shift_down(x, tail, s):
    if s == 0:
        return x
    ext = jnp.concatenate([tail, x], axis=0)
    return pltpu.roll(ext, s, axis=0)[SUBLANES:]


def _shift_up(x, head, s):
    if s == 0:
        return x
    n = x.shape[0]
    ext = jnp.concatenate([x, head], axis=0)
    return pltpu.roll(ext, n + SUBLANES - s, axis=0)[:n]


def _rg_gates(xbr, tail, cw_ref, cb, wr, wi, br, bi, ls):
    taps = [_shift_down(xbr, tail, CONV_W - 1 - k) for k in range(CONV_W)]
    xc = cb
    for k in range(CONV_W):
        xc = xc + cw_ref[pl.ds(k, 1), :] * taps[k]
    xcd = xc.astype(CD)
    r = _sigmoid(jnp.dot(xcd, wr, preferred_element_type=F32) + br)
    i = _sigmoid(jnp.dot(xcd, wi, preferred_element_type=F32) + bi)
    log_a = RG_C * r * ls
    a = jnp.exp(log_a)
    mult = jnp.sqrt(jnp.maximum(_neg_expm1(2.0 * log_a), 0.0))
    return taps, xc, r, i, log_a, a, mult


def _scan8_fwd(a, u):
    row = lax.broadcasted_iota(jnp.int32, a.shape, 0)
    for d in (1, 2, 4):
        a_s = pltpu.roll(a, d, axis=0)
        u_s = pltpu.roll(u, d, axis=0)
        m = row >= d
        u = jnp.where(m, a * u_s + u, u)
        a = jnp.where(m, a * a_s, a)
    return a, u


def _scan8_bwd(b, u):
    row = lax.broadcasted_iota(jnp.int32, b.shape, 0)
    for d in (1, 2, 4):
        b_s = pltpu.roll(b, SUBLANES - d, axis=0)
        u_s = pltpu.roll(u, SUBLANES - d, axis=0)
        m = row < SUBLANES - d
        u = jnp.where(m, b * u_s + u, u)
        b = jnp.where(m, b * b_s, b)
    return b, u


def _rglru_fwd(gate_br, x_br, cw, cb, wr, wi, br, bi, lam, name):
    s, c = x_br.shape
    nt = s // TIME_BLOCK
    tb, cbw = TIME_BLOCK, RG_BW
    groups = tb // SUBLANES

    def body(g_ref, x_ref, tail_ref, cw_ref, cb_ref, wr_ref, wi_ref, br_ref, bi_ref, lam_ref,
             y_ref, hs_ref, carry_ref, a_scr, u_scr):
        t = pl.program_id(1)

        @pl.when(t == 0)
        def _():
            carry_ref[...] = jnp.zeros_like(carry_ref)

        tail = jnp.where(t > 0, tail_ref[...], 0.0)
        ls = _log_sigmoid(lam_ref[...])
        _, xc, _, i, _, a, mult = _rg_gates(x_ref[...], tail, cw_ref, cb_ref[...], wr_ref[0], wi_ref[0],
                                            br_ref[...], bi_ref[...], ls)
        a_scr[...] = a
        u_scr[...] = mult * (i * xc)
        carry = carry_ref[...]
        for gi in range(groups):
            rows = pl.ds(gi * SUBLANES, SUBLANES)
            pa, hl = _scan8_fwd(a_scr[rows, :], u_scr[rows, :])
            hs_ref[rows, :] = hl + pa * carry
            carry = hs_ref[pl.ds(gi * SUBLANES + SUBLANES - 1, 1), :]
        carry_ref[...] = carry
        y_ref[...] = (hs_ref[...] * _gelu(g_ref[...])).astype(CD)

    blk = pl.BlockSpec((tb, cbw), lambda n, t: (t, n))
    tail = pl.BlockSpec((SUBLANES, cbw), lambda n, t: (jnp.maximum(t * groups - 1, 0), n))
    vec = pl.BlockSpec((1, cbw), lambda n, t: (0, n))
    wblk = pl.BlockSpec((1, cbw, cbw), lambda n, t: (n, 0, 0))
    return pl.pallas_call(
        body, name=name,
        out_shape=(jax.ShapeDtypeStruct((s, c), CD), jax.ShapeDtypeStruct((s, c), F32)),
        grid=(RG_BLOCKS, nt),
        in_specs=[blk, blk, tail, pl.BlockSpec((CONV_W, cbw), lambda n, t: (0, n)), vec, wblk, wblk, vec, vec, vec],
        out_specs=(blk, blk),
        scratch_shapes=[pltpu.VMEM((1, cbw), F32), pltpu.VMEM((tb, cbw), F32), pltpu.VMEM((tb, cbw), F32)],
        compiler_params=_params("parallel", "arbitrary"),
    )(gate_br, x_br, x_br, cw, cb, wr, wi, br, bi, lam)


def _rglru_bwd(dy, gate_br, x_br, hs, cw, cb, wr, wi, wrt, wit, br, bi, lam, name):
    s, c = x_br.shape
    nt = s // TIME_BLOCK
    tb, cbw = TIME_BLOCK, RG_BW
    groups = tb // SUBLANES

    def body(dy_ref, g_ref, x_ref, tail_ref, hs_ref, hprev_ref, cw_ref, cb_ref, wr_ref, wi_ref, wrt_ref, wit_ref,
             br_ref, bi_ref, lam_ref,
             dg_ref, dx_ref, dcw_ref, dcb_ref, dbr_ref, dbi_ref, dlam_ref, dwr_ref, dwi_ref,
             carry_ref, head_ref, b_scr, u_scr, dh_scr):
        tr = pl.program_id(1)
        first_block = tr == nt - 1

        @pl.when(tr == 0)
        def _():
            carry_ref[...] = jnp.zeros_like(carry_ref)
            head_ref[...] = jnp.zeros_like(head_ref)
            for ref in (dcw_ref, dcb_ref, dbr_ref, dbi_ref, dlam_ref, dwr_ref, dwi_ref):
                ref[...] = jnp.zeros_like(ref)

        tail = jnp.where(first_block, 0.0, tail_ref[...])
        lam_v = lam_ref[...]
        ls = _log_sigmoid(lam_v)
        taps, xc, r, i, log_a, a, mult = _rg_gates(x_ref[...], tail, cw_ref, cb_ref[...], wr_ref[0], wi_ref[0],
                                                   br_ref[...], bi_ref[...], ls)
        gate_v = g_ref[...]
        dyv = dy_ref[...]
        hsv = hs_ref[...]
        dg_ref[...] = (dyv * hsv * _gelu_grad(gate_v)).astype(CD)

        row = lax.broadcasted_iota(jnp.int32, a.shape, 0)
        b_scr[...] = jnp.where(row == tb - 1, 1.0, pltpu.roll(a, tb - 1, axis=0))
        u_scr[...] = dyv * _gelu(gate_v)
        carry = carry_ref[...]
        for gi in reversed(range(groups)):
            rows = pl.ds(gi * SUBLANES, SUBLANES)
            pb, gl = _scan8_bwd(b_scr[rows, :], u_scr[rows, :])
            dh_scr[rows, :] = gl + pb * carry
            carry = dh_scr[pl.ds(gi * SUBLANES, 1), :]
        dh = dh_scr[...]
        carry_ref[...] = carry * jnp.sum(jnp.where(row == 0, a, 0.0), axis=0, keepdims=True)

        hprev_tail = jnp.where(first_block, 0.0, hprev_ref[...])
        h_prev = _shift_down(hsv, hprev_tail, 1)
        da = dh * h_prev
        ixc = i * xc
        dmult = dh * ixc
        di = dh * mult * xc
        dxc = dh * mult * i
        a2 = a * a
        dlog_a = da * a - dmult * a2 / mult
        dpre_r = (dlog_a * (RG_C * ls)) * r * (1.0 - r)
        dpre_i = di * i * (1.0 - i)
        dlam_ref[...] += jnp.sum(dlog_a * r, axis=0, keepdims=True) * (RG_C * _sigmoid(-lam_v))
        dbr_ref[...] += jnp.sum(dpre_r, axis=0, keepdims=True)
        dbi_ref[...] += jnp.sum(dpre_i, axis=0, keepdims=True)
        xcd = xc.astype(CD)
        dprc = dpre_r.astype(CD)
        dpic = dpre_i.astype(CD)
        tn_dims = (((0,), (0,)), ((), ()))
        dwr_ref[0] += lax.dot_general(xcd, dprc, tn_dims, preferred_element_type=F32)
        dwi_ref[0] += lax.dot_general(xcd, dpic, tn_dims, preferred_element_type=F32)
        dxc = dxc + jnp.dot(dprc, wrt_ref[0], preferred_element_type=F32) + jnp.dot(dpic, wit_ref[0],
                                                                                    preferred_element_type=F32)
        dcb_ref[...] += jnp.sum(dxc, axis=0, keepdims=True)
        for k in range(CONV_W):
            dcw_ref[pl.ds(k, 1), :] += jnp.sum(dxc * taps[k], axis=0, keepdims=True)
        head = head_ref[...]
        dxb = jnp.zeros_like(dxc)
        for sft in range(CONV_W):
            dxb = dxb + cw_ref[pl.ds(CONV_W - 1 - sft, 1), :] * _shift_up(dxc, head, sft)
        dx_ref[...] = dxb.astype(CD)
        head_ref[...] = dxc[0:SUBLANES, :]

    blk = pl.BlockSpec((tb, cbw), lambda n, t: (nt - 1 - t, n))
    tail = pl.BlockSpec((SUBLANES, cbw), lambda n, t: (jnp.maximum((nt - 1 - t) * groups - 1, 0), n))
    vec = pl.BlockSpec((1, cbw), lambda n, t: (0, n))
    cwb = pl.BlockSpec((CONV_W, cbw), lambda n, t: (0, n))
    wblk = pl.BlockSpec((1, cbw, cbw), lambda n, t: (n, 0, 0))
    vshape = jax.ShapeDtypeStruct((1, c), F32)
    wshape = jax.ShapeDtypeStruct((RG_BLOCKS, cbw, cbw), F32)
    return pl.pallas_call(
        body, name=name,
        out_shape=(jax.ShapeDtypeStruct((s, c), CD), jax.ShapeDtypeStruct((s, c), CD),
                   jax.ShapeDtypeStruct((CONV_W, c), F32), vshape, vshape, vshape, vshape, wshape, wshape),
        grid=(RG_BLOCKS, nt),
        in_specs=[blk, blk, blk, tail, blk, tail, cwb, vec, wblk, wblk, wblk, wblk, vec, vec, vec],
        out_specs=(blk, blk, cwb, vec, vec, vec, vec, wblk, wblk),
        scratch_shapes=[pltpu.VMEM((1, cbw), F32), pltpu.VMEM((SUBLANES, cbw), F32),
                        pltpu.VMEM((tb, cbw), F32), pltpu.VMEM((tb, cbw), F32), pltpu.VMEM((tb, cbw), F32)],
        compiler_params=_params("parallel", "arbitrary"),
    )(dy, gate_br, x_br, x_br, hs, hs, cw, cb, wr, wi, wrt, wit, br, bi, lam)


ATT_BLOCK = 256
ATT_Q_BLOCK = 512
ATT_RATIO = ATT_Q_BLOCK // ATT_BLOCK
ATT_SCALE = 1.0 / math.sqrt(SB_HEAD_DIM)
N_PAIRS = SB_HEADS * SB_HEAD_DIM // LANES
NT_DIMS = (((1,), (1,)), ((), ()))
TN_DIMS = (((0,), (0,)), ((), ()))


LOG2E = 1.4426950408889634


def _neg_abs(x):
    bits = lax.bitcast_convert_type(x, jnp.uint32) | jnp.uint32(0x80000000)
    return lax.bitcast_convert_type(bits, F32)


def _qk(qx, kb):
    return lax.dot_general(qx, kb, NT_DIMS, preferred_element_type=F32)


def _sb_logits(qk, valid):
    z2 = qk * (ATT_SCALE * LOG2E)
    lb2 = jnp.minimum(z2, 0.0) - jnp.log2(1.0 + jnp.exp2(_neg_abs(z2)))
    l2 = lb2 - z2
    if valid is not None:
        l2 = jnp.where(valid, l2, 0.0)
    return lb2, l2


def _hi_lo(x):
    hi = x.astype(CD)
    lo = (x - hi.astype(F32)).astype(CD)
    return jnp.concatenate([hi, lo], axis=1)


def _tri(strict, stacked):
    r = lax.broadcasted_iota(jnp.int32, (ATT_BLOCK, ATT_BLOCK), 0)
    c = lax.broadcasted_iota(jnp.int32, (ATT_BLOCK, ATT_BLOCK), 1)
    m = (r > c if strict else r >= c).astype(CD)
    return jnp.concatenate([m, m], axis=0) if stacked else m


def _attn_fwd(qkv, name):
    _, s, _ = qkv.shape
    tq, t = ATT_Q_BLOCK, ATT_BLOCK
    nblk = s // tq

    def body(q_ref, k_ref, v_ref, o_ref, qk_scr, w_scr):
        i = pl.program_id(1)
        lane = lax.broadcasted_iota(jnp.int32, (1, LANES), 1)
        head_masks = (lane < SB_HEAD_DIM, lane >= SB_HEAD_DIM)
        q = q_ref[0]
        qs = [jnp.where(m, q, jnp.zeros_like(q)) for m in head_masks]
        tri = _tri(True, False)
        rr = lax.broadcasted_iota(jnp.int32, (tq, t), 0)
        cc = lax.broadcasted_iota(jnp.int32, (tq, t), 1)
        diag_valid = [cc + d * t < rr for d in range(ATT_RATIO)]

        def rows_of(j):
            return pl.ds(pl.multiple_of(j * t, t), t)

        def start_logits(j):
            kb = k_ref[0, rows_of(j), :]
            for hd in range(2):
                qk_scr[hd] = _qk(qs[hd], kb)

        def weights(run, valid):
            new_run = []
            for hd in range(2):
                lb2, l2 = _sb_logits(qk_scr[hd], valid)
                w = jnp.exp2(lb2 + (run[hd] + jnp.dot(l2.astype(CD), tri, preferred_element_type=F32)))
                if valid is not None:
                    w = jnp.where(valid, w, 0.0)
                w_scr[:, hd * t:(hd + 1) * t] = w.astype(CD)
                new_run.append(run[hd] + jnp.sum(l2, axis=1, keepdims=True))
            return tuple(new_run)

        def apply_weights(j):
            vb = v_ref[0, rows_of(j), :]
            vcat = jnp.concatenate([jnp.where(m, vb, jnp.zeros_like(vb)) for m in head_masks], axis=0)
            return jnp.dot(w_scr[...], vcat, preferred_element_type=F32)

        zero = jnp.zeros((tq, 1), F32)
        top = ATT_RATIO * i + ATT_RATIO - 1
        start_logits(top)
        run = weights((zero, zero), diag_valid[ATT_RATIO - 1])
        oacc = jnp.zeros((tq, LANES), F32)
        for d in reversed(range(ATT_RATIO - 1)):
            start_logits(ATT_RATIO * i + d)
            oacc = oacc + apply_weights(ATT_RATIO * i + d + 1)
            run = weights(run, diag_valid[d])
        start_logits(jnp.maximum(ATT_RATIO * i - 1, 0))

        def step(jj, carry):
            run, oacc = carry
            b = ATT_RATIO * i - 1 - jj
            oacc = oacc + apply_weights(b + 1)
            run = weights(run, None)
            start_logits(jnp.maximum(b - 1, 0))
            return run, oacc

        run, oacc = lax.fori_loop(0, ATT_RATIO * i, step, (run, oacc))
        o_ref[0] = oacc + apply_weights(0)

    return pl.pallas_call(
        body, name=name, out_shape=jax.ShapeDtypeStruct((N_PAIRS, s, LANES), F32), grid=(N_PAIRS, nblk),
        in_specs=[pl.BlockSpec((1, tq, LANES), lambda p, i: (p, i, 0)),
                  pl.BlockSpec((1, s, LANES), lambda p, i: (N_PAIRS + p, 0, 0)),
                  pl.BlockSpec((1, s, LANES), lambda p, i: (2 * N_PAIRS + p, 0, 0))],
        out_specs=pl.BlockSpec((1, tq, LANES), lambda p, i: (p, i, 0)),
        scratch_shapes=[pltpu.VMEM((2, tq, t), F32), pltpu.VMEM((tq, 2 * t), CD)],
        compiler_params=_params("parallel", "arbitrary"),
    )(qkv, qkv, qkv)


def _attn_bwd(qkv, o, do, name):
    _, s, _ = qkv.shape
    tq, t = ATT_Q_BLOCK, ATT_BLOCK
    nblk = s // tq

    def body(q_ref, k_ref, v_ref, o_ref, do_ref, dq_ref, dk_ref, dv_ref, qk_scr, dw_scr, w_scr, dz_scr):
        i = pl.program_id(1)

        @pl.when(i == 0)
        def _():
            dk_ref[...] = jnp.zeros_like(dk_ref)
            dv_ref[...] = jnp.zeros_like(dv_ref)

        lane = lax.broadcasted_iota(jnp.int32, (1, LANES), 1)
        head_masks = (lane < SB_HEAD_DIM, lane >= SB_HEAD_DIM)
        q = q_ref[0]
        dov = do_ref[0]
        ov = o_ref[0]
        qs = [jnp.where(m, q, jnp.zeros_like(q)) for m in head_masks]
        q_scaled_t = jnp.concatenate([(qx.astype(F32) * ATT_SCALE).T for qx in qs], axis=1).astype(CD)
        docs = [jnp.where(m, dov, 0.0).astype(CD) for m in head_masks]
        docat_t = jnp.concatenate([jnp.where(m, dov, 0.0).T for m in head_masks], axis=1).astype(CD)
        totals = [jnp.sum(d.astype(F32) * ov, axis=1, keepdims=True) for d in docs]
        tri = _tri(True, False)
        tri_incl = _tri(False, True)
        rr = lax.broadcasted_iota(jnp.int32, (tq, t), 0)
        cc = lax.broadcasted_iota(jnp.int32, (tq, t), 1)
        diag_valid = [cc + d * t < rr for d in range(ATT_RATIO)]

        def rows_of(j):
            return pl.ds(pl.multiple_of(j * t, t), t)

        def start_products(j):
            kb = k_ref[0, rows_of(j), :]
            vb = v_ref[0, rows_of(j), :]
            for hd in range(2):
                qk_scr[hd] = _qk(qs[hd], kb)
                dw_scr[hd] = lax.dot_general(docs[hd], vb, NT_DIMS, preferred_element_type=F32)

        def logit_grads(run, erun, valid):
            new_run, new_erun = [], []
            for hd in range(2):
                lb2, l2 = _sb_logits(qk_scr[hd], valid)
                w = jnp.exp2(lb2 + (run[hd] + jnp.dot(l2.astype(CD), tri, preferred_element_type=F32)))
                if valid is not None:
                    w = jnp.where(valid, w, 0.0)
                wc = w.astype(CD)
                w_scr[hd * tq:(hd + 1) * tq, :] = wc
                e = dw_scr[hd] * wc.astype(F32)
                prefix = (totals[hd] - erun[hd]) - jnp.dot(_hi_lo(e), tri_incl, preferred_element_type=F32)
                dz = e - jnp.exp2(lb2) * (e + prefix)
                if valid is not None:
                    dz = jnp.where(valid, dz, 0.0)
                dz_scr[hd * tq:(hd + 1) * tq, :] = dz.astype(CD)
                new_run.append(run[hd] + jnp.sum(l2, axis=1, keepdims=True))
                new_erun.append(erun[hd] + jnp.sum(e, axis=1, keepdims=True))
            return tuple(new_run), tuple(new_erun)

        def apply_grads(j):
            rows = rows_of(j)
            kb = k_ref[0, rows, :]
            kcat = jnp.concatenate([jnp.where(m, kb, jnp.zeros_like(kb)) for m in head_masks], axis=0)
            dz2 = dz_scr[...]
            dk_ref[0, :, rows] += jnp.dot(q_scaled_t, dz2, preferred_element_type=F32)
            dv_ref[0, :, rows] += jnp.dot(docat_t, w_scr[...], preferred_element_type=F32)
            return jnp.dot(jnp.concatenate([dz2[:tq], dz2[tq:]], axis=1), kcat, preferred_element_type=F32)

        zero = jnp.zeros((tq, 1), F32)
        top = ATT_RATIO * i + ATT_RATIO - 1
        start_products(top)
        run, erun = logit_grads((zero, zero), (zero, zero), diag_valid[ATT_RATIO - 1])
        dqacc = jnp.zeros((tq, LANES), F32)
        for d in reversed(range(ATT_RATIO - 1)):
            start_products(ATT_RATIO * i + d)
            dqacc = dqacc + apply_grads(ATT_RATIO * i + d + 1)
            run, erun = logit_grads(run, erun, diag_valid[d])
        start_products(jnp.maximum(ATT_RATIO * i - 1, 0))

        def step(jj, carry):
            run, erun, dqacc = carry
            b = ATT_RATIO * i - 1 - jj
            dqacc = dqacc + apply_grads(b + 1)
            run, erun = logit_grads(run, erun, None)
            start_products(jnp.maximum(b - 1, 0))
            return run, erun, dqacc

        run, erun, dqacc = lax.fori_loop(0, ATT_RATIO * i, step, (run, erun, dqacc))
        dq_ref[0] = (dqacc + apply_grads(0)) * ATT_SCALE

    qblk = pl.BlockSpec((1, tq, LANES), lambda p, i: (p, i, 0))
    full = pl.BlockSpec((1, LANES, s), lambda p, i: (p, 0, 0))
    shape = jax.ShapeDtypeStruct((N_PAIRS, s, LANES), F32)
    shape_t = jax.ShapeDtypeStruct((N_PAIRS, LANES, s), F32)
    dq, dk_t, dv_t = pl.pallas_call(
        body, name=name, out_shape=(shape, shape_t, shape_t), grid=(N_PAIRS, nblk),
        in_specs=[qblk,
                  pl.BlockSpec((1, s, LANES), lambda p, i: (N_PAIRS + p, 0, 0)),
                  pl.BlockSpec((1, s, LANES), lambda p, i: (2 * N_PAIRS + p, 0, 0)),
                  qblk, qblk],
        out_specs=(qblk, full, full),
        scratch_shapes=[pltpu.VMEM((2, tq, t), F32), pltpu.VMEM((2, tq, t), F32),
                        pltpu.VMEM((2 * tq, t), CD), pltpu.VMEM((2 * tq, t), CD)],
        compiler_params=_params("parallel", "arbitrary"),
    )(qkv, qkv, qkv, o, do)
    return dq, jnp.swapaxes(dk_t, 1, 2), jnp.swapaxes(dv_t, 1, 2)


ADAM_COLS = 1024


def _adamw(w, g, m, v, name):
    shape = w.shape
    rows, cols = (shape[-2], shape[-1]) if len(shape) >= 2 else (1, shape[-1])
    lead = w.size // (rows * cols)
    tr = _pick(rows, (512, 256, 128, 64, 32, 16, 8))

    def body(w_ref, g_ref, m_ref, v_ref, d_ref, nm_ref, nv_ref):
        gv = g_ref[...]
        nm = ADAM_B1 * m_ref[...] + (1.0 - ADAM_B1) * gv
        nv = ADAM_B2 * v_ref[...] + (1.0 - ADAM_B2) * (gv * gv)
        m_hat = nm / (1.0 - ADAM_B1 ** ADAM_STEP)
        v_hat = nv / (1.0 - ADAM_B2 ** ADAM_STEP)
        d_ref[...] = -ADAM_LR * (m_hat / (jnp.sqrt(v_hat) + ADAM_EPS) + ADAM_WD * w_ref[...])
        nm_ref[...] = nm
        nv_ref[...] = nv

    blk = pl.BlockSpec((1, tr, cols), lambda l, i: (l, i, 0))
    out = jax.ShapeDtypeStruct((lead, rows, cols), F32)
    d, nm, nv = pl.pallas_call(
        body, name=name, out_shape=(out, out, out), grid=(lead, rows // tr),
        in_specs=[blk, blk, blk, blk], out_specs=(blk, blk, blk), compiler_params=_params("parallel", "parallel"),
    )(*[a.reshape(lead, rows, cols) for a in (w, g, m, v)])
    return d.reshape(shape), nm.reshape(shape), nv.reshape(shape)


HBM = pl.BlockSpec(memory_space=pltpu.HBM)


def _coords():
    return lax.axis_index("x"), lax.axis_index("y"), lax.axis_index("c")


def _other_chips(x, y):
    return [(1 - x, y), (x, 1 - y), (1 - x, 1 - y)]


def _allgather_chips(shard, name):
    r, cols = shard.shape
    half = r // 2

    def body(src_ref, out_ref, send_sems, recv_sems):
        x, y, c = _coords()
        sibling = (x, y, 1 - c)
        chips = _other_chips(x, y)

        def rows(px, py, h):
            return out_ref.at[2 * px + py, pl.ds(h * half, half), :]

        def copy(k, block, to, src=None):
            return pltpu.make_async_remote_copy(
                src_ref=rows(*block) if src is None else src, dst_ref=rows(*block),
                send_sem=send_sems.at[k], recv_sem=recv_sems.at[k], device_id=to, device_id_type=MESH)

        my_half = src_ref.at[pl.ds(c * half, half), :]
        first = [copy(j, (x, y, c), (*chip, c), src=my_half) for j, chip in enumerate(chips)]
        for cp in first:
            cp.start()
        passed = [copy(3 + j, (*chip, c), sibling) for j, chip in enumerate(chips)]
        for j, chip in enumerate(chips):
            copy(j, (*chip, c), (x, y, c)).wait_recv()
            passed[j].start()
        for j, chip in enumerate(chips):
            copy(3 + j, (*chip, 1 - c), (x, y, c)).wait_recv()
        for cp in first + passed:
            cp.wait_send()

    return pl.pallas_call(
        body, name=name, out_shape=jax.ShapeDtypeStruct((N_CHIPS, r, cols), shard.dtype),
        in_specs=[HBM], out_specs=HBM,
        scratch_shapes=[pltpu.SemaphoreType.DMA((6,)), pltpu.SemaphoreType.DMA((6,))],
    )(shard)


def _exchange_sibling_halves(g, name):
    n, r, cols = g.shape
    half = r // 2

    def body(g_ref, out_ref, send_sem, recv_sem):
        x, y, c = _coords()
        cp = pltpu.make_async_remote_copy(
            src_ref=g_ref.at[:, pl.ds((1 - c) * half, half), :], dst_ref=out_ref,
            send_sem=send_sem, recv_sem=recv_sem, device_id=(x, y, 1 - c), device_id_type=MESH)
        cp.start()
        cp.wait()

    return pl.pallas_call(
        body, name=name, out_shape=jax.ShapeDtypeStruct((n, half, cols), g.dtype),
        in_specs=[HBM], out_specs=HBM,
        scratch_shapes=[pltpu.SemaphoreType.DMA, pltpu.SemaphoreType.DMA],
    )(g)


def _scatter_to_chips(p, name):
    n, h, cols = p.shape

    def body(p_ref, out_ref, send_sems, recv_sems, local_sem):
        x, y, c = _coords()
        me = 2 * x + y
        mine = pltpu.make_async_copy(p_ref.at[me], out_ref.at[me], local_sem)
        mine.start()
        sends = []
        for j, (px, py) in enumerate(_other_chips(x, y)):
            sends.append(pltpu.make_async_remote_copy(
                src_ref=p_ref.at[2 * px + py], dst_ref=out_ref.at[me],
                send_sem=send_sems.at[j], recv_sem=recv_sems.at[j], device_id=(px, py, c), device_id_type=MESH))
        for cp in sends:
            cp.start()
        for j, (px, py) in enumerate(_other_chips(x, y)):
            pltpu.make_async_remote_copy(
                src_ref=p_ref.at[me], dst_ref=out_ref.at[2 * px + py],
                send_sem=send_sems.at[j], recv_sem=recv_sems.at[j], device_id=(px, py, c),
                device_id_type=MESH).wait_recv()
        for cp in sends:
            cp.wait_send()
        mine.wait()

    return pl.pallas_call(
        body, name=name, out_shape=jax.ShapeDtypeStruct((n, h, cols), p.dtype),
        in_specs=[HBM], out_specs=HBM,
        scratch_shapes=[pltpu.SemaphoreType.DMA((3,)), pltpu.SemaphoreType.DMA((3,)), pltpu.SemaphoreType.DMA],
    )(p)


def _share_halves(v, name):
    h = v.shape[0] // 2

    def body(v_ref, out_ref, send_sem, recv_sem):
        x, y, c = _coords()
        cp = pltpu.make_async_remote_copy(
            src_ref=v_ref.at[pl.ds(c * h, h), :], dst_ref=out_ref.at[pl.ds(c * h, h), :],
            send_sem=send_sem, recv_sem=recv_sem, device_id=(x, y, 1 - c), device_id_type=MESH)
        cp.start()
        pltpu.make_async_remote_copy(
            src_ref=v_ref.at[pl.ds(c * h, h), :], dst_ref=out_ref.at[pl.ds((1 - c) * h, h), :],
            send_sem=send_sem, recv_sem=recv_sem, device_id=(x, y, 1 - c), device_id_type=MESH).wait_recv()
        cp.wait_send()

    return pl.pallas_call(
        body, name=name, out_shape=jax.ShapeDtypeStruct(v.shape, v.dtype),
        in_specs=[HBM], out_specs=HBM, input_output_aliases={0: 0},
        scratch_shapes=[pltpu.SemaphoreType.DMA, pltpu.SemaphoreType.DMA],
    )(v)


def _allreduce_small(v, name):
    r, cols = v.shape

    def body(v_ref, out_ref, buf_ref, send_sems, recv_sems):
        x, y, c = _coords()
        me = 4 * x + 2 * y + c
        buf_ref[me] = v_ref[...]
        sends = []
        for k in range(1, N_DEV):
            px = 1 - x if k & 4 else x
            py = 1 - y if k & 2 else y
            pc = 1 - c if k & 1 else c
            sends.append(pltpu.make_async_remote_copy(
                src_ref=v_ref, dst_ref=buf_ref.at[me], send_sem=send_sems.at[k - 1], recv_sem=recv_sems.at[k - 1],
                device_id=(px, py, pc), device_id_type=MESH))
        for cp in sends:
            cp.start()
        for cp in sends:
            cp.wait()
        acc = buf_ref[0]
        for d in range(1, N_DEV):
            acc = acc + buf_ref[d]
        out_ref[...] = acc

    return pl.pallas_call(
        body, name=name, out_shape=jax.ShapeDtypeStruct((r, cols), F32),
        in_specs=[pl.BlockSpec(memory_space=pltpu.VMEM)], out_specs=pl.BlockSpec(memory_space=pltpu.VMEM),
        scratch_shapes=[pltpu.VMEM((N_DEV, r, cols), F32), pltpu.SemaphoreType.DMA((N_DEV - 1,)),
                        pltpu.SemaphoreType.DMA((N_DEV - 1,))],
    )(v)


def _add_sibling(g, from_sibling, core, name):
    n, h, cols = from_sibling.shape
    tr = _pick(h, (512, 256, 128))
    steps = h // tr

    def body(core_ref, a_ref, b_ref, o_ref):
        o_ref[...] = (a_ref[...] + b_ref[...]).astype(o_ref.dtype)

    return pl.pallas_call(
        body, name=name, out_shape=jax.ShapeDtypeStruct(from_sibling.shape, jnp.bfloat16),
        grid_spec=pltpu.PrefetchScalarGridSpec(
            num_scalar_prefetch=1, grid=(n, steps),
            in_specs=[pl.BlockSpec((1, tr, cols), lambda s, i, core_ref: (s, core_ref[0] * steps + i, 0)),
                      pl.BlockSpec((1, tr, cols), lambda s, i, core_ref: (s, i, 0))],
            out_specs=pl.BlockSpec((1, tr, cols), lambda s, i, core_ref: (s, i, 0))),
        compiler_params=_params("parallel", "parallel"),
    )(core.reshape(1).astype(jnp.int32), g, from_sibling)


def _sum_slots(p, core, name):
    n, r, cols = p.shape
    tr = _pick(r, (512, 256, 128))
    steps = r // tr

    def body(core_ref, p_ref, o_ref):
        o_ref[...] = ((p_ref[0].astype(F32) + p_ref[1].astype(F32)) + p_ref[2].astype(F32)) + p_ref[3].astype(F32)

    return pl.pallas_call(
        body, name=name, out_shape=jax.ShapeDtypeStruct((2 * r, cols), F32),
        grid_spec=pltpu.PrefetchScalarGridSpec(
            num_scalar_prefetch=1, grid=(steps,),
            in_specs=[pl.BlockSpec((n, tr, cols), lambda i, core_ref: (0, i, 0))],
            out_specs=pl.BlockSpec((tr, cols), lambda i, core_ref: (core_ref[0] * steps + i, 0))),
        compiler_params=_params("parallel"),
    )(core.reshape(1).astype(jnp.int32), p)


PACK_COLS = 1024


def _pack_shards(parts):
    return jnp.concatenate([p.reshape(-1, PACK_COLS) for p in parts], axis=0)


def _unpack_shards(buf, shapes):
    out, row = [], 0
    for shp in shapes:
        nrows = math.prod(shp) // PACK_COLS
        out.append(buf[..., row:row + nrows, :].reshape(buf.shape[:-2] + tuple(shp)))
        row += nrows
    return out


def _local_step(x, target, w):
    t = lambda a: a.T
    g = {}
    h0 = _rms_fwd(x, w["norm_mix_g"][0], "rms_mix0")
    w_in_g, w_in_x = w["a_w_in"][:, :D_RNN], w["a_w_in"][:, D_RNN:]
    gate_br = _matmul([(h0, w_in_g)], F32, "mm_a_gate")
    x_br = _matmul([(h0, w_in_x)], F32, "mm_a_xbr")
    y_a, hs = _rglru_fwd(gate_br, x_br, w["a_conv_w"], w["a_conv_b"], w["a_w_r"], w["a_w_i"], w["a_b_r"],
                         w["a_b_i"], w["a_lambda"], "rglru_fwd")
    x1 = _matmul([(y_a, w["a_w_out"])], F32, "mm_a_out", addend=x)
    h1 = _rms_fwd(x1, w["norm_ffn_g"][0], "rms_ffn0")
    fg0, fu0, act0 = _ffn_up(h1, w["ffn_w_gate"][0], w["ffn_w_up"][0], "ffn0_up")
    x2 = _matmul([(act0, w["ffn_w_down"][0])], F32, "mm_f0_down", addend=x1)
    h2 = _rms_fwd(x2, w["norm_mix_g"][1], "rms_mix1")
    qkv = _matmul([(h2, w["b_w_qkv"])], CD, "mm_b_qkv", out_lbm=True, tn=1024)
    o = _attn_fwd(qkv, "attn_fwd")
    x3 = _matmul([(o, w["b_w_out"])], F32, "mm_b_out", a_lbm=True, addend=x2)
    h3 = _rms_fwd(x3, w["norm_ffn_g"][1], "rms_ffn1")
    fg1, fu1, act1 = _ffn_up(h3, w["ffn_w_gate"][1], w["ffn_w_up"][1], "ffn1_up")
    x4 = _matmul([(act1, w["ffn_w_down"][1])], F32, "mm_f1_down", addend=x3)
    loss, dx4, dx4c, g["final_g"] = _loss_head(x4, w["final_g"], target, "loss_head")

    def ffn_bwd(dx_out, dxc, h, x_in, fg, fu, act, layer, tag):
        dg, du = _ffn_dact(dxc, t(w["ffn_w_down"][layer]), fg, fu, "ffn_" + tag + "_dact")
        dwd = _matmul([(act, dxc)], F32, "mm_" + tag + "_dwd", trans_a=True)
        dwg = _matmul([(h, dg)], F32, "mm_" + tag + "_dwg", trans_a=True)
        dwu = _matmul([(h, du)], F32, "mm_" + tag + "_dwu", trans_a=True)
        dh = _matmul([(dg, t(w["ffn_w_gate"][layer])), (du, t(w["ffn_w_up"][layer]))], F32, "mm_" + tag + "_dh")
        dx_in, dx_in_c, dgain = _rms_bwd(dh, x_in, w["norm_ffn_g"][layer], dx_out, "rms_ffn" + tag + "_bwd")
        return dx_in, dx_in_c, dgain, dwg, dwu, dwd

    dx3, dx3c, dgf1, dwg1, dwu1, dwd1 = ffn_bwd(dx4, dx4c, h3, x3, fg1, fu1, act1, 1, "f1")
    do = _matmul([(dx3c, t(w["b_w_out"]))], F32, "mm_b_do", out_lbm=True, tn=1024)
    g["b_w_out"] = _matmul([(o, dx3c)], F32, "mm_b_dwout", trans_a=True, a_lbm=True)
    dq, dk, dv = _attn_bwd(qkv, o, do, "attn_bwd")
    wq_t = t(w["b_w_qkv"])
    parts = (dq, dk, dv)
    g["b_w_qkv"] = jnp.concatenate(
        [_matmul([(h2, p)], F32, "mm_b_dwqkv%d" % n, trans_a=True, b_lbm=True) for n, p in enumerate(parts)], axis=1)
    dh2 = _matmul([(p, wq_t[n * D_MODEL:(n + 1) * D_MODEL]) for n, p in enumerate(parts)], F32, "mm_b_dh",
                  a_lbm=True)
    dx2, dx2c, dgm1 = _rms_bwd(dh2, x2, w["norm_mix_g"][1], dx3, "rms_mix1_bwd")
    dx1, dx1c, dgf0, dwg0, dwu0, dwd0 = ffn_bwd(dx2, dx2c, h1, x1, fg0, fu0, act0, 0, "f0")
    dy_a = _matmul([(dx1c, t(w["a_w_out"]))], F32, "mm_a_dy")
    g["a_w_out"] = _matmul([(y_a, dx1c)], F32, "mm_a_dwout", trans_a=True)
    wrt = jnp.swapaxes(w["a_w_r"], 1, 2)
    wit = jnp.swapaxes(w["a_w_i"], 1, 2)
    (dgate, dxbr, g["a_conv_w"], g["a_conv_b"], g["a_b_r"], g["a_b_i"], g["a_lambda"], g["a_w_r"],
     g["a_w_i"]) = _rglru_bwd(dy_a, gate_br, x_br, hs, w["a_conv_w"], w["a_conv_b"], w["a_w_r"], w["a_w_i"], wrt, wit,
                              w["a_b_r"], w["a_b_i"], w["a_lambda"], "rglru_bwd")
    g["a_w_in"] = jnp.concatenate([_matmul([(h0, dgate)], F32, "mm_a_dwin_g", trans_a=True),
                                   _matmul([(h0, dxbr)], F32, "mm_a_dwin_x", trans_a=True)], axis=1)
    dh0 = _matmul([(dgate, t(w_in_g)), (dxbr, t(w_in_x))], F32, "mm_a_dh")
    dx0, _, dgm0 = _rms_bwd(dh0, x, w["norm_mix_g"][0], dx1, "rms_mix0_bwd")
    g["norm_mix_g"] = jnp.concatenate([dgm0, dgm1], axis=0)
    g["norm_ffn_g"] = jnp.concatenate([dgf0, dgf1], axis=0)
    g["ffn_w_gate"] = jnp.stack([dwg0, dwg1])
    g["ffn_w_up"] = jnp.stack([dwu0, dwu1])
    g["ffn_w_down"] = jnp.stack([dwd0, dwd1])
    return loss, dx0, g


WEIGHTS = ["norm_mix_g", "norm_ffn_g", "a_w_in", "a_conv_w", "a_conv_b", "a_w_r", "a_b_r", "a_w_i", "a_b_i",
           "a_lambda", "a_w_out", "b_w_qkv", "b_w_out", "ffn_w_gate", "ffn_w_up", "ffn_w_down", "final_g"]
BIG = [("a_w_in", 2), ("a_w_r", 2), ("a_w_i", 2), ("a_w_out", 1), ("b_w_qkv", 2), ("b_w_out", 1),
       ("ffn_w_gate", 2), ("ffn_w_up", 2), ("ffn_w_down", 1)]
SMALL = ["norm_mix_g", "norm_ffn_g", "a_conv_w", "a_conv_b", "a_b_r", "a_b_i", "a_lambda", "final_g"]


def _join_chips(stack, axis):
    return jnp.concatenate([stack[s] for s in range(N_CHIPS)], axis=axis)


def _split_chips(full, axis):
    return jnp.stack(jnp.split(full, N_CHIPS, axis=axis))


def _step(x, target, weights, moments_m, moments_v):
    chip = 2 * lax.axis_index("x") + lax.axis_index("y")
    core = lax.axis_index("c")
    shard_shapes = [weights[n].shape for n, _ in BIG]
    packed = _pack_shards([weights[n].astype(CD) for n, _ in BIG])
    gathered = _allgather_chips(packed, "allgather_weights")
    full = {}
    for (n, axis), stack in zip(BIG, _unpack_shards(gathered, shard_shapes)):
        own = weights[n].astype(CD)
        joined = jnp.concatenate([jnp.where(chip == s, own, stack[s]) for s in range(N_CHIPS)], axis=axis)
        full[n] = joined[0] if joined.shape[0] == 1 else joined
    cw_rows = jnp.zeros((N_CHIPS, CONV_W, RG_BW), F32)
    cw_rows = lax.dynamic_update_slice(cw_rows, jnp.where(core == 0, weights["a_conv_w"], 0.0), (chip, 0, 0))
    cw_all = _allreduce_small(cw_rows.reshape(-1, LANES), "allgather_conv_w").reshape(N_CHIPS, CONV_W, RG_BW)
    full["a_conv_w"] = jnp.concatenate([cw_all[s] for s in range(N_CHIPS)], axis=1)
    for n in ("norm_mix_g", "norm_ffn_g", "final_g"):
        full[n] = weights[n]
    for n in ("a_conv_b", "a_b_r", "a_b_i", "a_lambda"):
        full[n] = weights[n]
    loss, dx, grads = _local_step(x[0], target[0], full)
    small_parts = [grads[n].reshape(-1) for n in SMALL] + [loss.reshape(-1)]
    sizes = [p.shape[0] for p in small_parts]
    small = _allreduce_small(jnp.concatenate(small_parts).reshape(-1, LANES), "allreduce_small").reshape(-1)
    red, pos = {}, 0
    for n, sz in zip(SMALL + ["loss"], sizes):
        red[n] = small[pos:pos + sz]
        pos += sz
    loss_out = red["loss"][0]
    g_out = {}
    for n in SMALL:
        if n == "a_conv_w":
            g_out[n] = lax.dynamic_slice(red[n].reshape(CONV_W, D_RNN), (0, chip * RG_BW), (CONV_W, RG_BW)).reshape(
                weights[n].shape)
        else:
            g_out[n] = red[n].reshape(weights[n].shape)
    stacks = []
    for n, axis in BIG:
        gfull = grads[n].reshape((1,) + grads[n].shape) if grads[n].ndim == len(weights[n].shape) - 1 else grads[n]
        stacks.append(_split_chips(gfull, axis).reshape(N_CHIPS, -1, PACK_COLS))
    gbuf = jnp.concatenate(stacks, axis=1)
    from_sibling = _exchange_sibling_halves(gbuf, "rs_sibling")
    chip_partial = _add_sibling(gbuf, from_sibling, core, "rs_add_sibling")
    from_chips = _scatter_to_chips(chip_partial, "rs_chips")
    reduced = _share_halves(_sum_slots(from_chips, core, "rs_sum_chips"), "rs_share")
    for (n, _), gsh in zip(BIG, _unpack_shards(reduced, shard_shapes)):
        g_out[n] = gsh
    outs_g, outs_d, outs_m, outs_v = [], [], [], []
    for n in WEIGHTS:
        d, nm, nv = _adamw(weights[n], g_out[n], moments_m[n], moments_v[n], "adamw_" + n)
        outs_g.append(g_out[n])
        outs_d.append(d)
        outs_m.append(nm)
        outs_v.append(nv)
    return (loss_out, dx[None], *outs_g, *outs_d, *outs_m, *outs_v)


def kernel(x, norm_mix_g, norm_ffn_g, a_w_in, a_conv_w, a_conv_b, a_w_r, a_b_r, a_w_i, a_b_i, a_lambda, a_w_out, b_w_qkv, b_w_out, ffn_w_gate, ffn_w_up, ffn_w_down, final_g, loss_target, m_norm_mix_g, m_norm_ffn_g, m_a_w_in, m_a_conv_w, m_a_conv_b, m_a_w_r, m_a_b_r, m_a_w_i, m_a_b_i, m_a_lambda, m_a_w_out, m_b_w_qkv, m_b_w_out, m_ffn_w_gate, m_ffn_w_up, m_ffn_w_down, m_final_g, v_norm_mix_g, v_norm_ffn_g, v_a_w_in, v_a_conv_w, v_a_conv_b, v_a_w_r, v_a_b_r, v_a_w_i, v_a_b_i, v_a_lambda, v_a_w_out, v_b_w_qkv, v_b_w_out, v_ffn_w_gate, v_ffn_w_up, v_ffn_w_down, v_final_g):
    ws = [norm_mix_g, norm_ffn_g, a_w_in, a_conv_w, a_conv_b, a_w_r, a_b_r, a_w_i, a_b_i, a_lambda, a_w_out, b_w_qkv,
          b_w_out, ffn_w_gate, ffn_w_up, ffn_w_down, final_g]
    ms = [m_norm_mix_g, m_norm_ffn_g, m_a_w_in, m_a_conv_w, m_a_conv_b, m_a_w_r, m_a_b_r, m_a_w_i, m_a_b_i, m_a_lambda,
          m_a_w_out, m_b_w_qkv, m_b_w_out, m_ffn_w_gate, m_ffn_w_up, m_ffn_w_down, m_final_g]
    vs = [v_norm_mix_g, v_norm_ffn_g, v_a_w_in, v_a_conv_w, v_a_conv_b, v_a_w_r, v_a_b_r, v_a_w_i, v_a_b_i, v_a_lambda,
          v_a_w_out, v_b_w_qkv, v_b_w_out, v_ffn_w_gate, v_ffn_w_up, v_ffn_w_down, v_final_g]
    return _step(x, loss_target, dict(zip(WEIGHTS, ws)), dict(zip(WEIGHTS, ms)), dict(zip(WEIGHTS, vs)))
```

```python
import functools
import math

import jax
import jax.numpy as jnp
from jax import lax
from jax.experimental import pallas as pl
from jax.experimental.pallas import tpu as pltpu

F32 = jnp.float32
CD = jnp.bfloat16

D_MODEL = 1024
D_RNN = 1024
RG_BLOCKS = 4
RG_BW = 256
CONV_W = 4
RG_C = 8.0
SB_HEADS = 16
SB_HEAD_DIM = 64
D_FF = 2816
RMS_EPS = 1e-6
N_CHIPS = 4
N_DEV = 8

ADAM_LR = 0.001
ADAM_B1 = 0.9
ADAM_B2 = 0.999
ADAM_EPS = 1e-08
ADAM_WD = 0.01
ADAM_STEP = 10

LANES = 128
VMEM_LIMIT = 56 * 1024 * 1024
MESH = pl.DeviceIdType.MESH


def _params(*sem):
    return pltpu.CompilerParams(dimension_semantics=sem, vmem_limit_bytes=VMEM_LIMIT)


def _pick(n, prefs):
    for p in prefs:
        if n % p == 0:
            return p
    return n


def _matmul(pairs, out_dtype, name, *, trans_a=False, a_lbm=False, b_lbm=False, out_lbm=False, addend=None,
            tm=512, tn=None, tk=None):
    a0, b0 = pairs[0]
    if trans_a:
        kdim = a0.shape[1] if a_lbm else a0.shape[0]
        m = a0.shape[0] * LANES if a_lbm else a0.shape[1]
    else:
        m = a0.shape[1] if a_lbm else a0.shape[0]
        kdim = a0.shape[0] * LANES if a_lbm else a0.shape[1]
    n = b0.shape[0] * LANES if b_lbm else b0.shape[1]
    tm = _pick(m, (tm, 1408, 256, 128))
    tn = tn or _pick(n, (1408, 1024, 768, 512, 256, 128))
    tk = tk or _pick(kdim, (1024, 1408, 512, 256, 128))
    nk = kdim // tk
    npair = len(pairs)

    def cat(ref):
        return jnp.concatenate([ref[p] for p in range(ref.shape[0])], axis=-1)

    def body(*refs):
        ins = refs[: 2 * npair]
        pos = 2 * npair
        add_ref = None
        if addend is not None:
            add_ref = refs[pos]
            pos += 1
        o_ref = refs[pos]
        acc_ref = refs[pos + 1]
        k = pl.program_id(2)

        @pl.when(k == 0)
        def _():
            acc_ref[...] = jnp.zeros_like(acc_ref)

        acc = acc_ref[...]
        for p in range(npair):
            a = (cat(ins[2 * p]) if a_lbm else ins[2 * p][...]).astype(CD)
            b = (cat(ins[2 * p + 1]) if b_lbm else ins[2 * p + 1][...]).astype(CD)
            dims = (((0,), (0,)), ((), ())) if trans_a else (((1,), (0,)), ((), ()))
            acc = acc + lax.dot_general(a, b, dims, preferred_element_type=F32)
        acc_ref[...] = acc

        @pl.when(k == nk - 1)
        def _():
            res = acc_ref[...]
            if add_ref is not None:
                res = res + add_ref[...]
            res = res.astype(out_dtype)
            if out_lbm:
                for p in range(tn // LANES):
                    o_ref[p] = res[:, p * LANES:(p + 1) * LANES]
            else:
                o_ref[...] = res

    if trans_a:
        a_spec = (pl.BlockSpec((tm // LANES, tk, LANES), lambda i, j, k: (i, k, 0)) if a_lbm
                  else pl.BlockSpec((tk, tm), lambda i, j, k: (k, i)))
    else:
        a_spec = (pl.BlockSpec((tk // LANES, tm, LANES), lambda i, j, k: (k, i, 0)) if a_lbm
                  else pl.BlockSpec((tm, tk), lambda i, j, k: (i, k)))
    b_spec = (pl.BlockSpec((tn // LANES, tk, LANES), lambda i, j, k: (j, k, 0)) if b_lbm
              else pl.BlockSpec((tk, tn), lambda i, j, k: (k, j)))
    in_specs = []
    args = []
    for a, b in pairs:
        in_specs += [a_spec, b_spec]
        args += [a, b]
    if addend is not None:
        in_specs.append(pl.BlockSpec((tm, tn), lambda i, j, k: (i, j)))
        args.append(addend)
    if out_lbm:
        out_shape = jax.ShapeDtypeStruct((n // LANES, m, LANES), out_dtype)
        out_spec = pl.BlockSpec((tn // LANES, tm, LANES), lambda i, j, k: (j, i, 0))
    else:
        out_shape = jax.ShapeDtypeStruct((m, n), out_dtype)
        out_spec = pl.BlockSpec((tm, tn), lambda i, j, k: (i, j))
    return pl.pallas_call(
        body, name=name, out_shape=out_shape, grid=(m // tm, n // tn, nk),
        in_specs=in_specs, out_specs=out_spec,
        scratch_shapes=[pltpu.VMEM((tm, tn), F32)],
        compiler_params=_params("parallel", "parallel", "arbitrary"),
    )(*args)


ROW_BLOCK = 256


def _rms_fwd(x, g, name):
    s, d = x.shape

    def body(x_ref, g_ref, h_ref):
        xv = x_ref[...]
        rinv = lax.rsqrt(jnp.mean(xv * xv, axis=-1, keepdims=True) + RMS_EPS)
        h_ref[...] = (xv * rinv * g_ref[...]).astype(CD)

    return pl.pallas_call(
        body, name=name, out_shape=jax.ShapeDtypeStruct((s, d), CD), grid=(s // ROW_BLOCK,),
        in_specs=[pl.BlockSpec((ROW_BLOCK, d), lambda i: (i, 0)), pl.BlockSpec((1, d), lambda i: (0, 0))],
        out_specs=pl.BlockSpec((ROW_BLOCK, d), lambda i: (i, 0)),
        compiler_params=_params("parallel"),
    )(x, g.reshape(1, d))


def _rms_bwd(dh, x, g, dx_in, name):
    s, d = x.shape

    def body(dh_ref, x_ref, g_ref, dxin_ref, dx_ref, dxc_ref, dg_ref):
        @pl.when(pl.program_id(0) == 0)
        def _():
            dg_ref[...] = jnp.zeros_like(dg_ref)

        xv = x_ref[...]
        dhv = dh_ref[...]
        rinv = lax.rsqrt(jnp.mean(xv * xv, axis=-1, keepdims=True) + RMS_EPS)
        nrm = xv * rinv
        dn = dhv * g_ref[...]
        dx = dxin_ref[...] + rinv * (dn - nrm * jnp.mean(dn * nrm, axis=-1, keepdims=True))
        dx_ref[...] = dx
        dxc_ref[...] = dx.astype(CD)
        dg_ref[...] += jnp.sum(dhv * nrm, axis=0, keepdims=True)

    row = pl.BlockSpec((ROW_BLOCK, d), lambda i: (i, 0))
    vec = pl.BlockSpec((1, d), lambda i: (0, 0))
    return pl.pallas_call(
        body, name=name,
        out_shape=(jax.ShapeDtypeStruct((s, d), F32), jax.ShapeDtypeStruct((s, d), CD),
                   jax.ShapeDtypeStruct((1, d), F32)),
        grid=(s // ROW_BLOCK,), in_specs=[row, row, vec, row], out_specs=(row, row, vec),
        compiler_params=_params("arbitrary"),
    )(dh, x, g.reshape(1, d), dx_in)


def _loss_head(x, g, target, name):
    s, d = x.shape

    def body(x_ref, g_ref, t_ref, loss_ref, dx_ref, dxc_ref, dg_ref):
        @pl.when(pl.program_id(0) == 0)
        def _():
            dg_ref[...] = jnp.zeros_like(dg_ref)
            loss_ref[...] = jnp.zeros_like(loss_ref)

        xv = x_ref[...]
        gv = g_ref[...]
        rinv = lax.rsqrt(jnp.mean(xv * xv, axis=-1, keepdims=True) + RMS_EPS)
        nrm = xv * rinv
        err = nrm * gv - t_ref[...]
        loss_ref[...] += 0.5 * jnp.sum(jnp.mean(err * err, axis=-1, keepdims=True), axis=0, keepdims=True)
        dy = err * (1.0 / d)
        dn = dy * gv
        dx = rinv * (dn - nrm * jnp.mean(dn * nrm, axis=-1, keepdims=True))
        dx_ref[...] = dx
        dxc_ref[...] = dx.astype(CD)
        dg_ref[...] += jnp.sum(dy * nrm, axis=0, keepdims=True)

    row = pl.BlockSpec((ROW_BLOCK, d), lambda i: (i, 0))
    vec = pl.BlockSpec((1, d), lambda i: (0, 0))
    return pl.pallas_call(
        body, name=name,
        out_shape=(jax.ShapeDtypeStruct((1, LANES), F32), jax.ShapeDtypeStruct((s, d), F32),
                   jax.ShapeDtypeStruct((s, d), CD), jax.ShapeDtypeStruct((1, d), F32)),
        grid=(s // ROW_BLOCK,), in_specs=[row, vec, row],
        out_specs=(pl.BlockSpec((1, LANES), lambda i: (0, 0)), row, row, vec),
        compiler_params=_params("arbitrary"),
    )(x, g.reshape(1, d), target)


def _sigmoid(z):
    return 1.0 / (1.0 + jnp.exp(-z))


FFN_TM = 512
FFN_TN = 1408


def _ffn_up(h, wg, wu, name):
    s, d = h.shape
    f = wg.shape[1]
    tm = _pick(s, (FFN_TM, 256))

    def body(h_ref, wg_ref, wu_ref, g_ref, u_ref, a_ref):
        hv = h_ref[...]
        gv = jnp.dot(hv, wg_ref[...], preferred_element_type=F32)
        uv = jnp.dot(hv, wu_ref[...], preferred_element_type=F32)
        g_ref[...] = gv
        u_ref[...] = uv
        a_ref[...] = (gv * _sigmoid(gv) * uv).astype(CD)

    a_spec = pl.BlockSpec((tm, d), lambda i, j: (i, 0))
    w_spec = pl.BlockSpec((d, FFN_TN), lambda i, j: (0, j))
    o_spec = pl.BlockSpec((tm, FFN_TN), lambda i, j: (i, j))
    return pl.pallas_call(
        body, name=name,
        out_shape=(jax.ShapeDtypeStruct((s, f), F32), jax.ShapeDtypeStruct((s, f), F32),
                   jax.ShapeDtypeStruct((s, f), CD)),
        grid=(s // tm, f // FFN_TN), in_specs=[a_spec, w_spec, w_spec], out_specs=(o_spec, o_spec, o_spec),
        compiler_params=_params("parallel", "parallel"),
    )(h, wg, wu)


def _ffn_dact(dxc, wd_t, g, u, name):
    s, d = dxc.shape
    f = wd_t.shape[1]
    tm = _pick(s, (FFN_TM, 256))

    def body(dx_ref, w_ref, g_ref, u_ref, dg_ref, du_ref):
        da = jnp.dot(dx_ref[...], w_ref[...], preferred_element_type=F32)
        gv = g_ref[...]
        sg = _sigmoid(gv)
        silu = gv * sg
        dg_ref[...] = (da * u_ref[...] * (sg + silu * (1.0 - sg))).astype(CD)
        du_ref[...] = (da * silu).astype(CD)

    a_spec = pl.BlockSpec((tm, d), lambda i, j: (i, 0))
    w_spec = pl.BlockSpec((d, FFN_TN), lambda i, j: (0, j))
    o_spec = pl.BlockSpec((tm, FFN_TN), lambda i, j: (i, j))
    return pl.pallas_call(
        body, name=name,
        out_shape=(jax.ShapeDtypeStruct((s, f), CD), jax.ShapeDtypeStruct((s, f), CD)),
        grid=(s // tm, f // FFN_TN), in_specs=[a_spec, w_spec, o_spec, o_spec], out_specs=(o_spec, o_spec),
        compiler_params=_params("parallel", "parallel"),
    )(dxc, wd_t, g, u)


TIME_BLOCK = 256
SUBLANES = 8
GELU_C = math.sqrt(2.0 / math.pi)
GELU_A = 0.044715


def _gelu(x):
    return 0.5 * x * (1.0 + jnp.tanh(GELU_C * (x + GELU_A * x * x * x)))


def _gelu_grad(x):
    t = jnp.tanh(GELU_C * (x + GELU_A * x * x * x))
    return 0.5 * (1.0 + t) + 0.5 * x * (1.0 - t * t) * GELU_C * (1.0 + 3.0 * GELU_A * x * x)


def _neg_expm1(x):
    series = -x * (1.0 + x * (0.5 + x * (1.0 / 6.0 + x * (1.0 / 24.0))))
    return jnp.where(x > -0.05, series, 1.0 - jnp.exp(x))


def _log_sigmoid(x):
    return jnp.minimum(x, 0.0) - jnp.log1p(jnp.exp(-jnp.abs(x)))


def _shift_down(x, tail, s):
    if s == 0:
        return x
    ext = jnp.concatenate([tail, x], axis=0)
    return pltpu.roll(ext, s, axis=0)[SUBLANES:]


def _shift_up(x, head, s):
    if s == 0:
        return x
    n = x.shape[0]
    ext = jnp.concatenate([x, head], axis=0)
    return pltpu.roll(ext, n + SUBLANES - s, axis=0)[:n]


def _rg_gates(xbr, tail, cw_ref, cb, wr, wi, br, bi, ls):
    taps = [_shift_down(xbr, tail, CONV_W - 1 - k) for k in range(CONV_W)]
    xc = cb
    for k in range(CONV_W):
        xc = xc + cw_ref[pl.ds(k, 1), :] * taps[k]
    xcd = xc.astype(CD)
    r = _sigmoid(jnp.dot(xcd, wr, preferred_element_type=F32) + br)
    i = _sigmoid(jnp.dot(xcd, wi, preferred_element_type=F32) + bi)
    log_a = RG_C * r * ls
    a = jnp.exp(log_a)
    mult = jnp.sqrt(jnp.maximum(_neg_expm1(2.0 * log_a), 0.0))
    return taps, xc, r, i, log_a, a, mult


def _scan8_fwd(a, u):
    row = lax.broadcasted_iota(jnp.int32, a.shape, 0)
    for d in (1, 2, 4):
        a_s = pltpu.roll(a, d, axis=0)
        u_s = pltpu.roll(u, d, axis=0)
        m = row >= d
        u = jnp.where(m, a * u_s + u, u)
        a = jnp.where(m, a * a_s, a)
    return a, u


def _scan8_bwd(b, u):
    row = lax.broadcasted_iota(jnp.int32, b.shape, 0)
    for d in (1, 2, 4):
        b_s = pltpu.roll(b, SUBLANES - d, axis=0)
        u_s = pltpu.roll(u, SUBLANES - d, axis=0)
        m = row < SUBLANES - d
        u = jnp.where(m, b * u_s + u, u)
        b = jnp.where(m, b * b_s, b)
    return b, u


def _rglru_fwd(gate_br, x_br, cw, cb, wr, wi, br, bi, lam, name):
    s, c = x_br.shape
    nt = s // TIME_BLOCK
    tb, cbw = TIME_BLOCK, RG_BW
    groups = tb // SUBLANES

    def body(g_ref, x_ref, tail_ref, cw_ref, cb_ref, wr_ref, wi_ref, br_ref, bi_ref, lam_ref,
             y_ref, hs_ref, carry_ref, a_scr, u_scr):
        t = pl.program_id(1)

        @pl.when(t == 0)
        def _():
            carry_ref[...] = jnp.zeros_like(carry_ref)

        tail = jnp.where(t > 0, tail_ref[...], 0.0)
        ls = _log_sigmoid(lam_ref[...])
        _, xc, _, i, _, a, mult = _rg_gates(x_ref[...], tail, cw_ref, cb_ref[...], wr_ref[0], wi_ref[0],
                                            br_ref[...], bi_ref[...], ls)
        a_scr[...] = a
        u_scr[...] = mult * (i * xc)
        carry = carry_ref[...]
        for gi in range(groups):
            rows = pl.ds(gi * SUBLANES, SUBLANES)
            pa, hl = _scan8_fwd(a_scr[rows, :], u_scr[rows, :])
            hs_ref[rows, :] = hl + pa * carry
            carry = hs_ref[pl.ds(gi * SUBLANES + SUBLANES - 1, 1), :]
        carry_ref[...] = carry
        y_ref[...] = (hs_ref[...] * _gelu(g_ref[...])).astype(CD)

    blk = pl.BlockSpec((tb, cbw), lambda n, t: (t, n))
    tail = pl.BlockSpec((SUBLANES, cbw), lambda n, t: (jnp.maximum(t * groups - 1, 0), n))
    vec = pl.BlockSpec((1, cbw), lambda n, t: (0, n))
    wblk = pl.BlockSpec((1, cbw, cbw), lambda n, t: (n, 0, 0))
    return pl.pallas_call(
        body, name=name,
        out_shape=(jax.ShapeDtypeStruct((s, c), CD), jax.ShapeDtypeStruct((s, c), F32)),
        grid=(RG_BLOCKS, nt),
        in_specs=[blk, blk, tail, pl.BlockSpec((CONV_W, cbw), lambda n, t: (0, n)), vec, wblk, wblk, vec, vec, vec],
        out_specs=(blk, blk),
        scratch_shapes=[pltpu.VMEM((1, cbw), F32), pltpu.VMEM((tb, cbw), F32), pltpu.VMEM((tb, cbw), F32)],
        compiler_params=_params("parallel", "arbitrary"),
    )(gate_br, x_br, x_br, cw, cb, wr, wi, br, bi, lam)


def _rglru_bwd(dy, gate_br, x_br, hs, cw, cb, wr, wi, wrt, wit, br, bi, lam, name):
    s, c = x_br.shape
    nt = s // TIME_BLOCK
    tb, cbw = TIME_BLOCK, RG_BW
    groups = tb // SUBLANES

    def body(dy_ref, g_ref, x_ref, tail_ref, hs_ref, hprev_ref, cw_ref, cb_ref, wr_ref, wi_ref, wrt_ref, wit_ref,
             br_ref, bi_ref, lam_ref,
             dg_ref, dx_ref, dcw_ref, dcb_ref, dbr_ref, dbi_ref, dlam_ref, dwr_ref, dwi_ref,
             carry_ref, head_ref, b_scr, u_scr, dh_scr):
        tr = pl.program_id(1)
        first_block = tr == nt - 1

        @pl.when(tr == 0)
        def _():
            carry_ref[...] = jnp.zeros_like(carry_ref)
            head_ref[...] = jnp.zeros_like(head_ref)
            for ref in (dcw_ref, dcb_ref, dbr_ref, dbi_ref, dlam_ref, dwr_ref, dwi_ref):
                ref[...] = jnp.zeros_like(ref)

        tail = jnp.where(first_block, 0.0, tail_ref[...])
        lam_v = lam_ref[...]
        ls = _log_sigmoid(lam_v)
        taps, xc, r, i, log_a, a, mult = _rg_gates(x_ref[...], tail, cw_ref, cb_ref[...], wr_ref[0], wi_ref[0],
                                                   br_ref[...], bi_ref[...], ls)
        gate_v = g_ref[...]
        dyv = dy_ref[...]
        hsv = hs_ref[...]
        dg_ref[...] = (dyv * hsv * _gelu_grad(gate_v)).astype(CD)

        row = lax.broadcasted_iota(jnp.int32, a.shape, 0)
        b_scr[...] = jnp.where(row == tb - 1, 1.0, pltpu.roll(a, tb - 1, axis=0))
        u_scr[...] = dyv * _gelu(gate_v)
        carry = carry_ref[...]
        for gi in reversed(range(groups)):
            rows = pl.ds(gi * SUBLANES, SUBLANES)
            pb, gl = _scan8_bwd(b_scr[rows, :], u_scr[rows, :])
            dh_scr[rows, :] = gl + pb * carry
            carry = dh_scr[pl.ds(gi * SUBLANES, 1), :]
        dh = dh_scr[...]
        carry_ref[...] = carry * jnp.sum(jnp.where(row == 0, a, 0.0), axis=0, keepdims=True)

        hprev_tail = jnp.where(first_block, 0.0, hprev_ref[...])
        h_prev = _shift_down(hsv, hprev_tail, 1)
        da = dh * h_prev
        ixc = i * xc
        dmult = dh * ixc
        di = dh * mult * xc
        dxc = dh * mult * i
        a2 = a * a
        dlog_a = da * a - dmult * a2 / mult
        dpre_r = (dlog_a * (RG_C * ls)) * r * (1.0 - r)
        dpre_i = di * i * (1.0 - i)
        dlam_ref[...] += jnp.sum(dlog_a * r, axis=0, keepdims=True) * (RG_C * _sigmoid(-lam_v))
        dbr_ref[...] += jnp.sum(dpre_r, axis=0, keepdims=True)
        dbi_ref[...] += jnp.sum(dpre_i, axis=0, keepdims=True)
        xcd = xc.astype(CD)
        dprc = dpre_r.astype(CD)
        dpic = dpre_i.astype(CD)
        tn_dims = (((0,), (0,)), ((), ()))
        dwr_ref[0] += lax.dot_general(xcd, dprc, tn_dims, preferred_element_type=F32)
        dwi_ref[0] += lax.dot_general(xcd, dpic, tn_dims, preferred_element_type=F32)
        dxc = dxc + jnp.dot(dprc, wrt_ref[0], preferred_element_type=F32) + jnp.dot(dpic, wit_ref[0],
                                                                                    preferred_element_type=F32)
        dcb_ref[...] += jnp.sum(dxc, axis=0, keepdims=True)
        for k in range(CONV_W):
            dcw_ref[pl.ds(k, 1), :] += jnp.sum(dxc * taps[k], axis=0, keepdims=True)
        head = head_ref[...]
        dxb = jnp.zeros_like(dxc)
        for sft in range(CONV_W):
            dxb = dxb + cw_ref[pl.ds(CONV_W - 1 - sft, 1), :] * _shift_up(dxc, head, sft)
        dx_ref[...] = dxb.astype(CD)
        head_ref[...] = dxc[0:SUBLANES, :]

    blk = pl.BlockSpec((tb, cbw), lambda n, t: (nt - 1 - t, n))
    tail = pl.BlockSpec((SUBLANES, cbw), lambda n, t: (jnp.maximum((nt - 1 - t) * groups - 1, 0), n))
    vec = pl.BlockSpec((1, cbw), lambda n, t: (0, n))
    cwb = pl.BlockSpec((CONV_W, cbw), lambda n, t: (0, n))
    wblk = pl.BlockSpec((1, cbw, cbw), lambda n, t: (n, 0, 0))
    vshape = jax.ShapeDtypeStruct((1, c), F32)
    wshape = jax.ShapeDtypeStruct((RG_BLOCKS, cbw, cbw), F32)
    return pl.pallas_call(
        body, name=name,
        out_shape=(jax.ShapeDtypeStruct((s, c), CD), jax.ShapeDtypeStruct((s, c), CD),
                   jax.ShapeDtypeStruct((CONV_W, c), F32), vshape, vshape, vshape, vshape, wshape, wshape),
        grid=(RG_BLOCKS, nt),
        in_specs=[blk, blk, blk, tail, blk, tail, cwb, vec, wblk, wblk, wblk, wblk, vec, vec, vec],
        out_specs=(blk, blk, cwb, vec, vec, vec, vec, wblk, wblk),
        scratch_shapes=[pltpu.VMEM((1, cbw), F32), pltpu.VMEM((SUBLANES, cbw), F32),
                        pltpu.VMEM((tb, cbw), F32), pltpu.VMEM((tb, cbw), F32), pltpu.VMEM((tb, cbw), F32)],
        compiler_params=_params("parallel", "arbitrary"),
    )(dy, gate_br, x_br, x_br, hs, hs, cw, cb, wr, wi, wrt, wit, br, bi, lam)


ATT_BLOCK = 256
ATT_Q_BLOCK = 1024
ATT_RATIO = ATT_Q_BLOCK // ATT_BLOCK
ATT_SCALE = 1.0 / math.sqrt(SB_HEAD_DIM)
N_PAIRS = SB_HEADS * SB_HEAD_DIM // LANES
NT_DIMS = (((1,), (1,)), ((), ()))
TN_DIMS = (((0,), (0,)), ((), ()))


LOG2E = 1.4426950408889634


def _neg_abs(x):
    bits = lax.bitcast_convert_type(x, jnp.uint32) | jnp.uint32(0x80000000)
    return lax.bitcast_convert_type(bits, F32)


def _qk(qx, kb):
    return lax.dot_general(qx, kb, NT_DIMS, preferred_element_type=F32)


def _sb_logits(qk, valid):
    z2 = qk * (ATT_SCALE * LOG2E)
    lb2 = jnp.minimum(z2, 0.0) - jnp.log2(1.0 + jnp.exp2(_neg_abs(z2)))
    l2 = lb2 - z2
    if valid is not None:
        l2 = jnp.where(valid, l2, 0.0)
    return lb2, l2


def _hi_lo(x):
    hi = x.astype(CD)
    lo = (x - hi.astype(F32)).astype(CD)
    return jnp.concatenate([hi, lo], axis=1)


def _tri(strict, stacked):
    r = lax.broadcasted_iota(jnp.int32, (ATT_BLOCK, ATT_BLOCK), 0)
    c = lax.broadcasted_iota(jnp.int32, (ATT_BLOCK, ATT_BLOCK), 1)
    m = (r > c if strict else r >= c).astype(CD)
    return jnp.concatenate([m, m], axis=0) if stacked else m


def _attn_fwd(qkv, name):
    _, s, _ = qkv.shape
    tq, t = ATT_Q_BLOCK, ATT_BLOCK
    nblk = s // tq

    def body(q_ref, k_ref, v_ref, o_ref, qk_scr, w_scr):
        i = pl.program_id(1)
        lane = lax.broadcasted_iota(jnp.int32, (1, LANES), 1)
        head_masks = (lane < SB_HEAD_DIM, lane >= SB_HEAD_DIM)
        q = q_ref[0]
        qs = [jnp.where(m, q, jnp.zeros_like(q)) for m in head_masks]
        tri = _tri(True, False)
        rr = lax.broadcasted_iota(jnp.int32, (tq, t), 0)
        cc = lax.broadcasted_iota(jnp.int32, (tq, t), 1)
        diag_valid = [cc + d * t < rr for d in range(ATT_RATIO)]

        def rows_of(j):
            return pl.ds(pl.multiple_of(j * t, t), t)

        def start_logits(j):
            kb = k_ref[0, rows_of(j), :]
            for hd in range(2):
                qk_scr[hd] = _qk(qs[hd], kb)

        def weights(run, valid):
            new_run = []
            for hd in range(2):
                lb2, l2 = _sb_logits(qk_scr[hd], valid)
                w = jnp.exp2(lb2 + (run[hd] + jnp.dot(l2.astype(CD), tri, preferred_element_type=F32)))
                if valid is not None:
                    w = jnp.where(valid, w, 0.0)
                w_scr[:, hd * t:(hd + 1) * t] = w.astype(CD)
                new_run.append(run[hd] + jnp.sum(l2, axis=1, keepdims=True))
            return tuple(new_run)

        def apply_weights(j):
            vb = v_ref[0, rows_of(j), :]
            vcat = jnp.concatenate([jnp.where(m, vb, jnp.zeros_like(vb)) for m in head_masks], axis=0)
            return jnp.dot(w_scr[...], vcat, preferred_element_type=F32)

        zero = jnp.zeros((tq, 1), F32)
        top = ATT_RATIO * i + ATT_RATIO - 1
        start_logits(top)
        run = weights((zero, zero), diag_valid[ATT_RATIO - 1])
        oacc = jnp.zeros((tq, LANES), F32)
        for d in reversed(range(ATT_RATIO - 1)):
            start_logits(ATT_RATIO * i + d)
            oacc = oacc + apply_weights(ATT_RATIO * i + d + 1)
            run = weights(run, diag_valid[d])
        start_logits(jnp.maximum(ATT_RATIO * i - 1, 0))

        def step(jj, carry):
            run, oacc = carry
            b = ATT_RATIO * i - 1 - jj
            oacc = oacc + apply_weights(b + 1)
            run = weights(run, None)
            start_logits(jnp.maximum(b - 1, 0))
            return run, oacc

        run, oacc = lax.fori_loop(0, ATT_RATIO * i, step, (run, oacc))
        o_ref[0] = oacc + apply_weights(0)

    return pl.pallas_call(
        body, name=name, out_shape=jax.ShapeDtypeStruct((N_PAIRS, s, LANES), F32), grid=(N_PAIRS, nblk),
        in_specs=[pl.BlockSpec((1, tq, LANES), lambda p, i: (p, i, 0)),
                  pl.BlockSpec((1, s, LANES), lambda p, i: (N_PAIRS + p, 0, 0)),
                  pl.BlockSpec((1, s, LANES), lambda p, i: (2 * N_PAIRS + p, 0, 0))],
        out_specs=pl.BlockSpec((1, tq, LANES), lambda p, i: (p, i, 0)),
        scratch_shapes=[pltpu.VMEM((2, tq, t), F32), pltpu.VMEM((tq, 2 * t), CD)],
        compiler_params=_params("parallel", "arbitrary"),
    )(qkv, qkv, qkv)


def _attn_bwd(qkv, o, do, name):
    _, s, _ = qkv.shape
    tq, t = ATT_Q_BLOCK, ATT_BLOCK
    nblk = s // tq

    def body(q_ref, k_ref, v_ref, o_ref, do_ref, dq_ref, dk_ref, dv_ref, qk_scr, dw_scr, w_scr, dz_scr):
        i = pl.program_id(1)

        @pl.when(i == 0)
        def _():
            dk_ref[...] = jnp.zeros_like(dk_ref)
            dv_ref[...] = jnp.zeros_like(dv_ref)

        lane = lax.broadcasted_iota(jnp.int32, (1, LANES), 1)
        head_masks = (lane < SB_HEAD_DIM, lane >= SB_HEAD_DIM)
        q = q_ref[0]
        dov = do_ref[0]
        ov = o_ref[0]
        qs = [jnp.where(m, q, jnp.zeros_like(q)) for m in head_masks]
        q_scaled_t = jnp.concatenate([(qx.astype(F32) * ATT_SCALE).T for qx in qs], axis=1).astype(CD)
        docs = [jnp.where(m, dov, 0.0).astype(CD) for m in head_masks]
        docat_t = jnp.concatenate([jnp.where(m, dov, 0.0).T for m in head_masks], axis=1).astype(CD)
        totals = [jnp.sum(d.astype(F32) * ov, axis=1, keepdims=True) for d in docs]
        tri = _tri(True, False)
        tri_incl = _tri(False, True)
        rr = lax.broadcasted_iota(jnp.int32, (tq, t), 0)
        cc = lax.broadcasted_iota(jnp.int32, (tq, t), 1)
        diag_valid = [cc + d * t < rr for d in range(ATT_RATIO)]

        def rows_of(j):
            return pl.ds(pl.multiple_of(j * t, t), t)

        def start_products(j):
            kb = k_ref[0, rows_of(j), :]
            vb = v_ref[0, rows_of(j), :]
            for hd in range(2):
                qk_scr[hd] = _qk(qs[hd], kb)
                dw_scr[hd] = lax.dot_general(docs[hd], vb, NT_DIMS, preferred_element_type=F32)

        def logit_grads(run, erun, valid):
            new_run, new_erun = [], []
            for hd in range(2):
                lb2, l2 = _sb_logits(qk_scr[hd], valid)
                w = jnp.exp2(lb2 + (run[hd] + jnp.dot(l2.astype(CD), tri, preferred_element_type=F32)))
                if valid is not None:
                    w = jnp.where(valid, w, 0.0)
                wc = w.astype(CD)
                w_scr[hd * tq:(hd + 1) * tq, :] = wc
                e = dw_scr[hd] * wc.astype(F32)
                prefix = (totals[hd] - erun[hd]) - jnp.dot(_hi_lo(e), tri_incl, preferred_element_type=F32)
                dz = e - jnp.exp2(lb2) * (e + prefix)
                if valid is not None:
                    dz = jnp.where(valid, dz, 0.0)
                dz_scr[hd * tq:(hd + 1) * tq, :] = dz.astype(CD)
                new_run.append(run[hd] + jnp.sum(l2, axis=1, keepdims=True))
                new_erun.append(erun[hd] + jnp.sum(e, axis=1, keepdims=True))
            return tuple(new_run), tuple(new_erun)

        def apply_grads(j):
            rows = rows_of(j)
            kb = k_ref[0, rows, :]
            kcat = jnp.concatenate([jnp.where(m, kb, jnp.zeros_like(kb)) for m in head_masks], axis=0)
            dz2 = dz_scr[...]
            dk_ref[0, :, rows] += jnp.dot(q_scaled_t, dz2, preferred_element_type=F32)
            dv_ref[0, :, rows] += jnp.dot(docat_t, w_scr[...], preferred_element_type=F32)
            return jnp.dot(jnp.concatenate([dz2[:tq], dz2[tq:]], axis=1), kcat, preferred_element_type=F32)

        zero = jnp.zeros((tq, 1), F32)
        top = ATT_RATIO * i + ATT_RATIO - 1
        start_products(top)
        run, erun = logit_grads((zero, zero), (zero, zero), diag_valid[ATT_RATIO - 1])
        dqacc = jnp.zeros((tq, LANES), F32)
        for d in reversed(range(ATT_RATIO - 1)):
            start_products(ATT_RATIO * i + d)
            dqacc = dqacc + apply_grads(ATT_RATIO * i + d + 1)
            run, erun = logit_grads(run, erun, diag_valid[d])
        start_products(jnp.maximum(ATT_RATIO * i - 1, 0))

        def step(jj, carry):
            run, erun, dqacc = carry
            b = ATT_RATIO * i - 1 - jj
            dqacc = dqacc + apply_grads(b + 1)
            run, erun = logit_grads(run, erun, None)
            start_products(jnp.maximum(b - 1, 0))
            return run, erun, dqacc

        run, erun, dqacc = lax.fori_loop(0, ATT_RATIO * i, step, (run, erun, dqacc))
        dq_ref[0] = (dqacc + apply_grads(0)) * ATT_SCALE

    qblk = pl.BlockSpec((1, tq, LANES), lambda p, i: (p, i, 0))
    full = pl.BlockSpec((1, LANES, s), lambda p, i: (p, 0, 0))
    shape = jax.ShapeDtypeStruct((N_PAIRS, s, LANES), F32)
    shape_t = jax.ShapeDtypeStruct((N_PAIRS, LANES, s), F32)
    dq, dk_t, dv_t = pl.pallas_call(
        body, name=name, out_shape=(shape, shape_t, shape_t), grid=(N_PAIRS, nblk),
        in_specs=[qblk,
                  pl.BlockSpec((1, s, LANES), lambda p, i: (N_PAIRS + p, 0, 0)),
                  pl.BlockSpec((1, s, LANES), lambda p, i: (2 * N_PAIRS + p, 0, 0)),
                  qblk, qblk],
        out_specs=(qblk, full, full),
        scratch_shapes=[pltpu.VMEM((2, tq, t), F32), pltpu.VMEM((2, tq, t), F32),
                        pltpu.VMEM((2 * tq, t), CD), pltpu.VMEM((2 * tq, t), CD)],
        compiler_params=_params("parallel", "arbitrary"),
    )(qkv, qkv, qkv, o, do)
    return dq, jnp.swapaxes(dk_t, 1, 2), jnp.swapaxes(dv_t, 1, 2)


ADAM_COLS = 1024


def _adamw(w, g, m, v, name):
    shape = w.shape
    rows, cols = (shape[-2], shape[-1]) if len(shape) >= 2 else (1, shape[-1])
    lead = w.size // (rows * cols)
    tr = _pick(rows, (512, 256, 128, 64, 32, 16, 8))

    def body(w_ref, g_ref, m_ref, v_ref, d_ref, nm_ref, nv_ref):
        gv = g_ref[...]
        nm = ADAM_B1 * m_ref[...] + (1.0 - ADAM_B1) * gv
        nv = ADAM_B2 * v_ref[...] + (1.0 - ADAM_B2) * (gv * gv)
        m_hat = nm / (1.0 - ADAM_B1 ** ADAM_STEP)
        v_hat = nv / (1.0 - ADAM_B2 ** ADAM_STEP)
        d_ref[...] = -ADAM_LR * (m_hat / (jnp.sqrt(v_hat) + ADAM_EPS) + ADAM_WD * w_ref[...])
        nm_ref[...] = nm
        nv_ref[...] = nv

    blk = pl.BlockSpec((1, tr, cols), lambda l, i: (l, i, 0))
    out = jax.ShapeDtypeStruct((lead, rows, cols), F32)
    d, nm, nv = pl.pallas_call(
        body, name=name, out_shape=(out, out, out), grid=(lead, rows // tr),
        in_specs=[blk, blk, blk, blk], out_specs=(blk, blk, blk), compiler_params=_params("parallel", "parallel"),
    )(*[a.reshape(lead, rows, cols) for a in (w, g, m, v)])
    return d.reshape(shape), nm.reshape(shape), nv.reshape(shape)


HBM = pl.BlockSpec(memory_space=pltpu.HBM)


def _coords():
    return lax.axis_index("x"), lax.axis_index("y"), lax.axis_index("c")


def _other_chips(x, y):
    return [(1 - x, y), (x, 1 - y), (1 - x, 1 - y)]


def _allgather_chips(shard, name):
    r, cols = shard.shape
    half = r // 2

    def body(src_ref, out_ref, send_sems, recv_sems):
        x, y, c = _coords()
        sibling = (x, y, 1 - c)
        chips = _other_chips(x, y)

        def rows(px, py, h):
            return out_ref.at[2 * px + py, pl.ds(h * half, half), :]

        def copy(k, block, to, src=None):
            return pltpu.make_async_remote_copy(
                src_ref=rows(*block) if src is None else src, dst_ref=rows(*block),
                send_sem=send_sems.at[k], recv_sem=recv_sems.at[k], device_id=to, device_id_type=MESH)

        my_half = src_ref.at[pl.ds(c * half, half), :]
        first = [copy(j, (x, y, c), (*chip, c), src=my_half) for j, chip in enumerate(chips)]
        for cp in first:
            cp.start()
        passed = [copy(3 + j, (*chip, c), sibling) for j, chip in enumerate(chips)]
        for j, chip in enumerate(chips):
            copy(j, (*chip, c), (x, y, c)).wait_recv()
            passed[j].start()
        for j, chip in enumerate(chips):
            copy(3 + j, (*chip, 1 - c), (x, y, c)).wait_recv()
        for cp in first + passed:
            cp.wait_send()

    return pl.pallas_call(
        body, name=name, out_shape=jax.ShapeDtypeStruct((N_CHIPS, r, cols), shard.dtype),
        in_specs=[HBM], out_specs=HBM,
        scratch_shapes=[pltpu.SemaphoreType.DMA((6,)), pltpu.SemaphoreType.DMA((6,))],
    )(shard)


def _exchange_sibling_halves(g, name):
    n, r, cols = g.shape
    half = r // 2

    def body(g_ref, out_ref, send_sem, recv_sem):
        x, y, c = _coords()
        cp = pltpu.make_async_remote_copy(
            src_ref=g_ref.at[:, pl.ds((1 - c) * half, half), :], dst_ref=out_ref,
            send_sem=send_sem, recv_sem=recv_sem, device_id=(x, y, 1 - c), device_id_type=MESH)
        cp.start()
        cp.wait()

    return pl.pallas_call(
        body, name=name, out_shape=jax.ShapeDtypeStruct((n, half, cols), g.dtype),
        in_specs=[HBM], out_specs=HBM,
        scratch_shapes=[pltpu.SemaphoreType.DMA, pltpu.SemaphoreType.DMA],
    )(g)


def _scatter_to_chips(p, name):
    n, h, cols = p.shape

    def body(p_ref, out_ref, send_sems, recv_sems, local_sem):
        x, y, c = _coords()
        me = 2 * x + y
        mine = pltpu.make_async_copy(p_ref.at[me], out_ref.at[me], local_sem)
        mine.start()
        sends = []
        for j, (px, py) in enumerate(_other_chips(x, y)):
            sends.append(pltpu.make_async_remote_copy(
                src_ref=p_ref.at[2 * px + py], dst_ref=out_ref.at[me],
                send_sem=send_sems.at[j], recv_sem=recv_sems.at[j], device_id=(px, py, c), device_id_type=MESH))
        for cp in sends:
            cp.start()
        for j, (px, py) in enumerate(_other_chips(x, y)):
            pltpu.make_async_remote_copy(
                src_ref=p_ref.at[me], dst_ref=out_ref.at[2 * px + py],
                send_sem=send_sems.at[j], recv_sem=recv_sems.at[j], device_id=(px, py, c),
                device_id_type=MESH).wait_recv()
        for cp in sends:
            cp.wait_send()
        mine.wait()

    return pl.pallas_call(
        body, name=name, out_shape=jax.ShapeDtypeStruct((n, h, cols), p.dtype),
        in_specs=[HBM], out_specs=HBM,
        scratch_shapes=[pltpu.SemaphoreType.DMA((3,)), pltpu.SemaphoreType.DMA((3,)), pltpu.SemaphoreType.DMA],
    )(p)


def _share_halves(v, name):
    h = v.shape[0] // 2

    def body(v_ref, out_ref, send_sem, recv_sem):
        x, y, c = _coords()
        cp = pltpu.make_async_remote_copy(
            src_ref=v_ref.at[pl.ds(c * h, h), :], dst_ref=out_ref.at[pl.ds(c * h, h), :],
            send_sem=send_sem, recv_sem=recv_sem, device_id=(x, y, 1 - c), device_id_type=MESH)
        cp.start()
        pltpu.make_async_remote_copy(
            src_ref=v_ref.at[pl.ds(c * h, h), :], dst_ref=out_ref.at[pl.ds((1 - c) * h, h), :],
            send_sem=send_sem, recv_sem=recv_sem, device_id=(x, y, 1 - c), device_id_type=MESH).wait_recv()
        cp.wait_send()

    return pl.pallas_call(
        body, name=name, out_shape=jax.ShapeDtypeStruct(v.shape, v.dtype),
        in_specs=[HBM], out_specs=HBM, input_output_aliases={0: 0},
        scratch_shapes=[pltpu.SemaphoreType.DMA, pltpu.SemaphoreType.DMA],
    )(v)


def _allreduce_small(v, name):
    r, cols = v.shape

    def body(v_ref, out_ref, buf_ref, send_sems, recv_sems):
        x, y, c = _coords()
        me = 4 * x + 2 * y + c
        buf_ref[me] = v_ref[...]
        sends = []
        for k in range(1, N_DEV):
            px = 1 - x if k & 4 else x
            py = 1 - y if k & 2 else y
            pc = 1 - c if k & 1 else c
            sends.append(pltpu.make_async_remote_copy(
                src_ref=v_ref, dst_ref=buf_ref.at[me], send_sem=send_sems.at[k - 1], recv_sem=recv_sems.at[k - 1],
                device_id=(px, py, pc), device_id_type=MESH))
        for cp in sends:
            cp.start()
        for cp in sends:
            cp.wait()
        acc = buf_ref[0]
        for d in range(1, N_DEV):
            acc = acc + buf_ref[d]
        out_ref[...] = acc

    return pl.pallas_call(
        body, name=name, out_shape=jax.ShapeDtypeStruct((r, cols), F32),
        in_specs=[pl.BlockSpec(memory_space=pltpu.VMEM)], out_specs=pl.BlockSpec(memory_space=pltpu.VMEM),
        scratch_shapes=[pltpu.VMEM((N_DEV, r, cols), F32), pltpu.SemaphoreType.DMA((N_DEV - 1,)),
                        pltpu.SemaphoreType.DMA((N_DEV - 1,))],
    )(v)


def _add_sibling(g, from_sibling, core, name):
    n, h, cols = from_sibling.shape
    tr = _pick(h, (512, 256, 128))
    steps = h // tr

    def body(core_ref, a_ref, b_ref, o_ref):
        o_ref[...] = (a_ref[...] + b_ref[...]).astype(o_ref.dtype)

    return pl.pallas_call(
        body, name=name, out_shape=jax.ShapeDtypeStruct(from_sibling.shape, jnp.bfloat16),
        grid_spec=pltpu.PrefetchScalarGridSpec(
            num_scalar_prefetch=1, grid=(n, steps),
            in_specs=[pl.BlockSpec((1, tr, cols), lambda s, i, core_ref: (s, core_ref[0] * steps + i, 0)),
                      pl.BlockSpec((1, tr, cols), lambda s, i, core_ref: (s, i, 0))],
            out_specs=pl.BlockSpec((1, tr, cols), lambda s, i, core_ref: (s, i, 0))),
        compiler_params=_params("parallel", "parallel"),
    )(core.reshape(1).astype(jnp.int32), g, from_sibling)


def _sum_slots(p, core, name):
    n, r, cols = p.shape
    tr = _pick(r, (512, 256, 128))
    steps = r // tr

    def body(core_ref, p_ref, o_ref):
        o_ref[...] = ((p_ref[0].astype(F32) + p_ref[1].astype(F32)) + p_ref[2].astype(F32)) + p_ref[3].astype(F32)

    return pl.pallas_call(
        body, name=name, out_shape=jax.ShapeDtypeStruct((2 * r, cols), F32),
        grid_spec=pltpu.PrefetchScalarGridSpec(
            num_scalar_prefetch=1, grid=(steps,),
            in_specs=[pl.BlockSpec((n, tr, cols), lambda i, core_ref: (0, i, 0))],
            out_specs=pl.BlockSpec((tr, cols), lambda i, core_ref: (core_ref[0] * steps + i, 0))),
        compiler_params=_params("parallel"),
    )(core.reshape(1).astype(jnp.int32), p)


PACK_COLS = 1024


def _pack_shards(parts):
    return jnp.concatenate([p.reshape(-1, PACK_COLS) for p in parts], axis=0)


def _unpack_shards(buf, shapes):
    out, row = [], 0
    for shp in shapes:
        nrows = math.prod(shp) // PACK_COLS
        out.append(buf[..., row:row + nrows, :].reshape(buf.shape[:-2] + tuple(shp)))
        row += nrows
    return out


def _local_step(x, target, w):
    t = lambda a: a.T
    g = {}
    h0 = _rms_fwd(x, w["norm_mix_g"][0], "rms_mix0")
    w_in_g, w_in_x = w["a_w_in"][:, :D_RNN], w["a_w_in"][:, D_RNN:]
    gate_br = _matmul([(h0, w_in_g)], F32, "mm_a_gate")
    x_br = _matmul([(h0, w_in_x)], F32, "mm_a_xbr")
    y_a, hs = _rglru_fwd(gate_br, x_br, w["a_conv_w"], w["a_conv_b"], w["a_w_r"], w["a_w_i"], w["a_b_r"],
                         w["a_b_i"], w["a_lambda"], "rglru_fwd")
    x1 = _matmul([(y_a, w["a_w_out"])], F32, "mm_a_out", addend=x)
    h1 = _rms_fwd(x1, w["norm_ffn_g"][0], "rms_ffn0")
    fg0, fu0, act0 = _ffn_up(h1, w["ffn_w_gate"][0], w["ffn_w_up"][0], "ffn0_up")
    x2 = _matmul([(act0, w["ffn_w_down"][0])], F32, "mm_f0_down", addend=x1)
    h2 = _rms_fwd(x2, w["norm_mix_g"][1], "rms_mix1")
    qkv = _matmul([(h2, w["b_w_qkv"])], CD, "mm_b_qkv", out_lbm=True, tn=1024)
    o = _attn_fwd(qkv, "attn_fwd")
    x3 = _matmul([(o, w["b_w_out"])], F32, "mm_b_out", a_lbm=True, addend=x2)
    h3 = _rms_fwd(x3, w["norm_ffn_g"][1], "rms_ffn1")
    fg1, fu1, act1 = _ffn_up(h3, w["ffn_w_gate"][1], w["ffn_w_up"][1], "ffn1_up")
    x4 = _matmul([(act1, w["ffn_w_down"][1])], F32, "mm_f1_down", addend=x3)
    loss, dx4, dx4c, g["final_g"] = _loss_head(x4, w["final_g"], target, "loss_head")

    def ffn_bwd(dx_out, dxc, h, x_in, fg, fu, act, layer, tag):
        dg, du = _ffn_dact(dxc, t(w["ffn_w_down"][layer]), fg, fu, "ffn_" + tag + "_dact")
        dwd = _matmul([(act, dxc)], F32, "mm_" + tag + "_dwd", trans_a=True)
        dwg = _matmul([(h, dg)], F32, "mm_" + tag + "_dwg", trans_a=True)
        dwu = _matmul([(h, du)], F32, "mm_" + tag + "_dwu", trans_a=True)
        dh = _matmul([(dg, t(w["ffn_w_gate"][layer])), (du, t(w["ffn_w_up"][layer]))], F32, "mm_" + tag + "_dh")
        dx_in, dx_in_c, dgain = _rms_bwd(dh, x_in, w["norm_ffn_g"][layer], dx_out, "rms_ffn" + tag + "_bwd")
        return dx_in, dx_in_c, dgain, dwg, dwu, dwd

    dx3, dx3c, dgf1, dwg1, dwu1, dwd1 = ffn_bwd(dx4, dx4c, h3, x3, fg1, fu1, act1, 1, "f1")
    do = _matmul([(dx3c, t(w["b_w_out"]))], F32, "mm_b_do", out_lbm=True, tn=1024)
    g["b_w_out"] = _matmul([(o, dx3c)], F32, "mm_b_dwout", trans_a=True, a_lbm=True)
    dq, dk, dv = _attn_bwd(qkv, o, do, "attn_bwd")
    wq_t = t(w["b_w_qkv"])
    parts = (dq, dk, dv)
    g["b_w_qkv"] = jnp.concatenate(
        [_matmul([(h2, p)], F32, "mm_b_dwqkv%d" % n, trans_a=True, b_lbm=True) for n, p in enumerate(parts)], axis=1)
    dh2 = _matmul([(p, wq_t[n * D_MODEL:(n + 1) * D_MODEL]) for n, p in enumerate(parts)], F32, "mm_b_dh",
                  a_lbm=True)
    dx2, dx2c, dgm1 = _rms_bwd(dh2, x2, w["norm_mix_g"][1], dx3, "rms_mix1_bwd")
    dx1, dx1c, dgf0, dwg0, dwu0, dwd0 = ffn_bwd(dx2, dx2c, h1, x1, fg0, fu0, act0, 0, "f0")
    dy_a = _matmul([(dx1c, t(w["a_w_out"]))], F32, "mm_a_dy")
    g["a_w_out"] = _matmul([(y_a, dx1c)], F32, "mm_a_dwout", trans_a=True)
    wrt = jnp.swapaxes(w["a_w_r"], 1, 2)
    wit = jnp.swapaxes(w["a_w_i"], 1, 2)
    (dgate, dxbr, g["a_conv_w"], g["a_conv_b"], g["a_b_r"], g["a_b_i"], g["a_lambda"], g["a_w_r"],
     g["a_w_i"]) = _rglru_bwd(dy_a, gate_br, x_br, hs, w["a_conv_w"], w["a_conv_b"], w["a_w_r"], w["a_w_i"], wrt, wit,
                              w["a_b_r"], w["a_b_i"], w["a_lambda"], "rglru_bwd")
    g["a_w_in"] = jnp.concatenate([_matmul([(h0, dgate)], F32, "mm_a_dwin_g", trans_a=True),
                                   _matmul([(h0, dxbr)], F32, "mm_a_dwin_x", trans_a=True)], axis=1)
    dh0 = _matmul([(dgate, t(w_in_g)), (dxbr, t(w_in_x))], F32, "mm_a_dh")
    dx0, _, dgm0 = _rms_bwd(dh0, x, w["norm_mix_g"][0], dx1, "rms_mix0_bwd")
    g["norm_mix_g"] = jnp.concatenate([dgm0, dgm1], axis=0)
    g["norm_ffn_g"] = jnp.concatenate([dgf0, dgf1], axis=0)
    g["ffn_w_gate"] = jnp.stack([dwg0, dwg1])
    g["ffn_w_up"] = jnp.stack([dwu0, dwu1])
    g["ffn_w_down"] = jnp.stack([dwd0, dwd1])
    return loss, dx0, g


WEIGHTS = ["norm_mix_g", "norm_ffn_g", "a_w_in", "a_conv_w", "a_conv_b", "a_w_r", "a_b_r", "a_w_i", "a_b_i",
           "a_lambda", "a_w_out", "b_w_qkv", "b_w_out", "ffn_w_gate", "ffn_w_up", "ffn_w_down", "final_g"]
BIG = [("a_w_in", 2), ("a_w_r", 2), ("a_w_i", 2), ("a_w_out", 1), ("b_w_qkv", 2), ("b_w_out", 1),
       ("ffn_w_gate", 2), ("ffn_w_up", 2), ("ffn_w_down", 1)]
SMALL = ["norm_mix_g", "norm_ffn_g", "a_conv_w", "a_conv_b", "a_b_r", "a_b_i", "a_lambda", "final_g"]


def _join_chips(stack, axis):
    return jnp.concatenate([stack[s] for s in range(N_CHIPS)], axis=axis)


def _split_chips(full, axis):
    return jnp.stack(jnp.split(full, N_CHIPS, axis=axis))


def _step(x, target, weights, moments_m, moments_v):
    chip = 2 * lax.axis_index("x") + lax.axis_index("y")
    core = lax.axis_index("c")
    shard_shapes = [weights[n].shape for n, _ in BIG]
    packed = _pack_shards([weights[n].astype(CD) for n, _ in BIG])
    gathered = _allgather_chips(packed, "allgather_weights")
    full = {}
    for (n, axis), stack in zip(BIG, _unpack_shards(gathered, shard_shapes)):
        own = weights[n].astype(CD)
        joined = jnp.concatenate([jnp.where(chip == s, own, stack[s]) for s in range(N_CHIPS)], axis=axis)
        full[n] = joined[0] if joined.shape[0] == 1 else joined
    cw_rows = jnp.zeros((N_CHIPS, CONV_W, RG_BW), F32)
    cw_rows = lax.dynamic_update_slice(cw_rows, jnp.where(core == 0, weights["a_conv_w"], 0.0), (chip, 0, 0))
    cw_all = _allreduce_small(cw_rows.reshape(-1, LANES), "allgather_conv_w").reshape(N_CHIPS, CONV_W, RG_BW)
    full["a_conv_w"] = jnp.concatenate([cw_all[s] for s in range(N_CHIPS)], axis=1)
    for n in ("norm_mix_g", "norm_ffn_g", "final_g"):
        full[n] = weights[n]
    for n in ("a_conv_b", "a_b_r", "a_b_i", "a_lambda"):
        full[n] = weights[n]
    loss, dx, grads = _local_step(x[0], target[0], full)
    small_parts = [grads[n].reshape(-1) for n in SMALL] + [loss.reshape(-1)]
    sizes = [p.shape[0] for p in small_parts]
    small = _allreduce_small(jnp.concatenate(small_parts).reshape(-1, LANES), "allreduce_small").reshape(-1)
    red, pos = {}, 0
    for n, sz in zip(SMALL + ["loss"], sizes):
        red[n] = small[pos:pos + sz]
        pos += sz
    loss_out = red["loss"][0]
    g_out = {}
    for n in SMALL:
        if n == "a_conv_w":
            g_out[n] = lax.dynamic_slice(red[n].reshape(CONV_W, D_RNN), (0, chip * RG_BW), (CONV_W, RG_BW)).reshape(
                weights[n].shape)
        else:
            g_out[n] = red[n].reshape(weights[n].shape)
    stacks = []
    for n, axis in BIG:
        gfull = grads[n].reshape((1,) + grads[n].shape) if grads[n].ndim == len(weights[n].shape) - 1 else grads[n]
        stacks.append(_split_chips(gfull, axis).reshape(N_CHIPS, -1, PACK_COLS))
    gbuf = jnp.concatenate(stacks, axis=1)
    from_sibling = _exchange_sibling_halves(gbuf, "rs_sibling")
    chip_partial = _add_sibling(gbuf, from_sibling, core, "rs_add_sibling")
    from_chips = _scatter_to_chips(chip_partial, "rs_chips")
    reduced = _share_halves(_sum_slots(from_chips, core, "rs_sum_chips"), "rs_share")
    for (n, _), gsh in zip(BIG, _unpack_shards(reduced, shard_shapes)):
        g_out[n] = gsh
    outs_g, outs_d, outs_m, outs_v = [], [], [], []
    for n in WEIGHTS:
        d, nm, nv = _adamw(weights[n], g_out[n], moments_m[n], moments_v[n], "adamw_" + n)
        outs_g.append(g_out[n])
        outs_d.append(d)
        outs_m.append(nm)
        outs_v.append(nv)
    return (loss_out, dx[None], *outs_g, *outs_d, *outs_m, *outs_v)


def kernel(x, norm_mix_g, norm_ffn_g, a_w_in, a_conv_w, a_conv_b, a_w_r, a_b_r, a_w_i, a_b_i, a_lambda, a_w_out, b_w_qkv, b_w_out, ffn_w_gate, ffn_w_up, ffn_w_down, final_g, loss_target, m_norm_mix_g, m_norm_ffn_g, m_a_w_in, m_a_conv_w, m_a_conv_b, m_a_w_r, m_a_b_r, m_a_w_i, m_a_b_i, m_a_lambda, m_a_w_out, m_b_w_qkv, m_b_w_out, m_ffn_w_gate, m_ffn_w_up, m_ffn_w_down, m_final_g, v_norm_mix_g, v_norm_ffn_g, v_a_w_in, v_a_conv_w, v_a_conv_b, v_a_w_r, v_a_b_r, v_a_w_i, v_a_b_i, v_a_lambda, v_a_w_out, v_b_w_qkv, v_b_w_out, v_ffn_w_gate, v_ffn_w_up, v_ffn_w_down, v_final_g):
    ws = [norm_mix_g, norm_ffn_g, a_w_in, a_conv_w, a_conv_b, a_w_r, a_b_r, a_w_i, a_b_i, a_lambda, a_w_out, b_w_qkv,
          b_w_out, ffn_w_gate, ffn_w_up, ffn_w_down, final_g]
    ms = [m_norm_mix_g, m_norm_ffn_g, m_a_w_in, m_a_conv_w, m_a_conv_b, m_a_w_r, m_a_b_r, m_a_w_i, m_a_b_i, m_a_lambda,
          m_a_w_out, m_b_w_qkv, m_b_w_out, m_ffn_w_gate, m_ffn_w_up, m_ffn_w_down, m_final_g]
    vs = [v_norm_mix_g, v_norm_ffn_g, v_a_w_in, v_a_conv_w, v_a_conv_b, v_a_w_r, v_a_b_r, v_a_w_i, v_a_b_i, v_a_lambda,
          v_a_w_out, v_b_w_qkv, v_b_w_out, v_ffn_w_gate, v_ffn_w_up, v_ffn_w_down, v_final_g]
    return _step(x, loss_target, dict(zip(WEIGHTS, ws)), dict(zip(WEIGHTS, ms)), dict(zip(WEIGHTS, vs)))
```

```python
import functools
import math

import jax
import jax.numpy as jnp
from jax import lax
from jax.experimental import pallas as pl
from jax.experimental.pallas import tpu as pltpu

F32 = jnp.float32
CD = jnp.bfloat16

D_MODEL = 1024
D_RNN = 1024
RG_BLOCKS = 4
RG_BW = 256
CONV_W = 4
RG_C = 8.0
SB_HEADS = 16
SB_HEAD_DIM = 64
D_FF = 2816
RMS_EPS = 1e-6
N_CHIPS = 4
N_DEV = 8

ADAM_LR = 0.001
ADAM_B1 = 0.9
ADAM_B2 = 0.999
ADAM_EPS = 1e-08
ADAM_WD = 0.01
ADAM_STEP = 10

LANES = 128
VMEM_LIMIT = 56 * 1024 * 1024
MESH = pl.DeviceIdType.MESH


def _params(*sem):
    return pltpu.CompilerParams(dimension_semantics=sem, vmem_limit_bytes=VMEM_LIMIT)


def _pick(n, prefs):
    for p in prefs:
        if n % p == 0:
            return p
    return n


def _matmul(pairs, out_dtype, name, *, trans_a=False, a_lbm=False, b_lbm=False, out_lbm=False, addend=None,
            tm=512, tn=None, tk=None):
    a0, b0 = pairs[0]
    if trans_a:
        kdim = a0.shape[1] if a_lbm else a0.shape[0]
        m = a0.shape[0] * LANES if a_lbm else a0.shape[1]
    else:
        m = a0.shape[1] if a_lbm else a0.shape[0]
        kdim = a0.shape[0] * LANES if a_lbm else a0.shape[1]
    n = b0.shape[0] * LANES if b_lbm else b0.shape[1]
    tm = _pick(m, (tm, 1408, 256, 128))
    tn = tn or _pick(n, (1408, 1024, 768, 512, 256, 128))
    tk = tk or _pick(kdim, (1024, 1408, 512, 256, 128))
    nk = kdim // tk
    npair = len(pairs)

    def cat(ref):
        return jnp.concatenate([ref[p] for p in range(ref.shape[0])], axis=-1)

    def body(*refs):
        ins = refs[: 2 * npair]
        pos = 2 * npair
        add_ref = None
        if addend is not None:
            add_ref = refs[pos]
            pos += 1
        o_ref = refs[pos]
        acc_ref = refs[pos + 1]
        k = pl.program_id(2)

        @pl.when(k == 0)
        def _():
            acc_ref[...] = jnp.zeros_like(acc_ref)

        acc = acc_ref[...]
        for p in range(npair):
            a = (cat(ins[2 * p]) if a_lbm else ins[2 * p][...]).astype(CD)
            b = (cat(ins[2 * p + 1]) if b_lbm else ins[2 * p + 1][...]).astype(CD)
            dims = (((0,), (0,)), ((), ())) if trans_a else (((1,), (0,)), ((), ()))
            acc = acc + lax.dot_general(a, b, dims, preferred_element_type=F32)
        acc_ref[...] = acc

        @pl.when(k == nk - 1)
        def _():
            res = acc_ref[...]
            if add_ref is not None:
                res = res + add_ref[...]
            res = res.astype(out_dtype)
            if out_lbm:
                for p in range(tn // LANES):
                    o_ref[p] = res[:, p * LANES:(p + 1) * LANES]
            else:
                o_ref[...] = res

    if trans_a:
        a_spec = (pl.BlockSpec((tm // LANES, tk, LANES), lambda i, j, k: (i, k, 0)) if a_lbm
                  else pl.BlockSpec((tk, tm), lambda i, j, k: (k, i)))
    else:
        a_spec = (pl.BlockSpec((tk // LANES, tm, LANES), lambda i, j, k: (k, i, 0)) if a_lbm
                  else pl.BlockSpec((tm, tk), lambda i, j, k: (i, k)))
    b_spec = (pl.BlockSpec((tn // LANES, tk, LANES), lambda i, j, k: (j, k, 0)) if b_lbm
              else pl.BlockSpec((tk, tn), lambda i, j, k: (k, j)))
    in_specs = []
    args = []
    for a, b in pairs:
        in_specs += [a_spec, b_spec]
        args += [a, b]
    if addend is not None:
        in_specs.append(pl.BlockSpec((tm, tn), lambda i, j, k: (i, j)))
        args.append(addend)
    if out_lbm:
        out_shape = jax.ShapeDtypeStruct((n // LANES, m, LANES), out_dtype)
        out_spec = pl.BlockSpec((tn // LANES, tm, LANES), lambda i, j, k: (j, i, 0))
    else:
        out_shape = jax.ShapeDtypeStruct((m, n), out_dtype)
        out_spec = pl.BlockSpec((tm, tn), lambda i, j, k: (i, j))
    return pl.pallas_call(
        body, name=name, out_shape=out_shape, grid=(m // tm, n // tn, nk),
        in_specs=in_specs, out_specs=out_spec,
        scratch_shapes=[pltpu.VMEM((tm, tn), F32)],
        compiler_params=_params("parallel", "parallel", "arbitrary"),
    )(*args)


ROW_BLOCK = 256


def _rms_fwd(x, g, name):
    s, d = x.shape

    def body(x_ref, g_ref, h_ref):
        xv = x_ref[...]
        rinv = lax.rsqrt(jnp.mean(xv * xv, axis=-1, keepdims=True) + RMS_EPS)
        h_ref[...] = (xv * rinv * g_ref[...]).astype(CD)

    return pl.pallas_call(
        body, name=name, out_shape=jax.ShapeDtypeStruct((s, d), CD), grid=(s // ROW_BLOCK,),
        in_specs=[pl.BlockSpec((ROW_BLOCK, d), lambda i: (i, 0)), pl.BlockSpec((1, d), lambda i: (0, 0))],
        out_specs=pl.BlockSpec((ROW_BLOCK, d), lambda i: (i, 0)),
        compiler_params=_params("parallel"),
    )(x, g.reshape(1, d))


def _rms_bwd(dh, x, g, dx_in, name):
    s, d = x.shape

    def body(dh_ref, x_ref, g_ref, dxin_ref, dx_ref, dxc_ref, dg_ref):
        @pl.when(pl.program_id(0) == 0)
        def _():
            dg_ref[...] = jnp.zeros_like(dg_ref)

        xv = x_ref[...]
        dhv = dh_ref[...]
        rinv = lax.rsqrt(jnp.mean(xv * xv, axis=-1, keepdims=True) + RMS_EPS)
        nrm = xv * rinv
        dn = dhv * g_ref[...]
        dx = dxin_ref[...] + rinv * (dn - nrm * jnp.mean(dn * nrm, axis=-1, keepdims=True))
        dx_ref[...] = dx
        dxc_ref[...] = dx.astype(CD)
        dg_ref[...] += jnp.sum(dhv * nrm, axis=0, keepdims=True)

    row = pl.BlockSpec((ROW_BLOCK, d), lambda i: (i, 0))
    vec = pl.BlockSpec((1, d), lambda i: (0, 0))
    return pl.pallas_call(
        body, name=name,
        out_shape=(jax.ShapeDtypeStruct((s, d), F32), jax.ShapeDtypeStruct((s, d), CD),
                   jax.ShapeDtypeStruct((1, d), F32)),
        grid=(s // ROW_BLOCK,), in_specs=[row, row, vec, row], out_specs=(row, row, vec),
        compiler_params=_params("arbitrary"),
    )(dh, x, g.reshape(1, d), dx_in)


def _loss_head(x, g, target, name):
    s, d = x.shape

    def body(x_ref, g_ref, t_ref, loss_ref, dx_ref, dxc_ref, dg_ref):
        @pl.when(pl.program_id(0) == 0)
        def _():
            dg_ref[...] = jnp.zeros_like(dg_ref)
            loss_ref[...] = jnp.zeros_like(loss_ref)

        xv = x_ref[...]
        gv = g_ref[...]
        rinv = lax.rsqrt(jnp.mean(xv * xv, axis=-1, keepdims=True) + RMS_EPS)
        nrm = xv * rinv
        err = nrm * gv - t_ref[...]
        loss_ref[...] += 0.5 * jnp.sum(jnp.mean(err * err, axis=-1, keepdims=True), axis=0, keepdims=True)
        dy = err * (1.0 / d)
        dn = dy * gv
        dx = rinv * (dn - nrm * jnp.mean(dn * nrm, axis=-1, keepdims=True))
        dx_ref[...] = dx
        dxc_ref[...] = dx.astype(CD)
        dg_ref[...] += jnp.sum(dy * nrm, axis=0, keepdims=True)

    row = pl.BlockSpec((ROW_BLOCK, d), lambda i: (i, 0))
    vec = pl.BlockSpec((1, d), lambda i: (0, 0))
    return pl.pallas_call(
        body, name=name,
        out_shape=(jax.ShapeDtypeStruct((1, LANES), F32), jax.ShapeDtypeStruct((s, d), F32),
                   jax.ShapeDtypeStruct((s, d), CD), jax.ShapeDtypeStruct((1, d), F32)),
        grid=(s // ROW_BLOCK,), in_specs=[row, vec, row],
        out_specs=(pl.BlockSpec((1, LANES), lambda i: (0, 0)), row, row, vec),
        compiler_params=_params("arbitrary"),
    )(x, g.reshape(1, d), target)


def _sigmoid(z):
    return 1.0 / (1.0 + jnp.exp(-z))


FFN_TM = 512
FFN_TN = 1408


def _ffn_up(h, wg, wu, name):
    s, d = h.shape
    f = wg.shape[1]
    tm = _pick(s, (FFN_TM, 256))

    def body(h_ref, wg_ref, wu_ref, g_ref, u_ref, a_ref):
        hv = h_ref[...]
        gv = jnp.dot(hv, wg_ref[...], preferred_element_type=F32)
        uv = jnp.dot(hv, wu_ref[...], preferred_element_type=F32)
        g_ref[...] = gv
        u_ref[...] = uv
        a_ref[...] = (gv * _sigmoid(gv) * uv).astype(CD)

    a_spec = pl.BlockSpec((tm, d), lambda i, j: (i, 0))
    w_spec = pl.BlockSpec((d, FFN_TN), lambda i, j: (0, j))
    o_spec = pl.BlockSpec((tm, FFN_TN), lambda i, j: (i, j))
    return pl.pallas_call(
        body, name=name,
        out_shape=(jax.ShapeDtypeStruct((s, f), F32), jax.ShapeDtypeStruct((s, f), F32),
                   jax.ShapeDtypeStruct((s, f), CD)),
        grid=(s // tm, f // FFN_TN), in_specs=[a_spec, w_spec, w_spec], out_specs=(o_spec, o_spec, o_spec),
        compiler_params=_params("parallel", "parallel"),
    )(h, wg, wu)


def _ffn_dact(dxc, wd_t, g, u, name):
    s, d = dxc.shape
    f = wd_t.shape[1]
    tm = _pick(s, (FFN_TM, 256))

    def body(dx_ref, w_ref, g_ref, u_ref, dg_ref, du_ref):
        da = jnp.dot(dx_ref[...], w_ref[...], preferred_element_type=F32)
        gv = g_ref[...]
        sg = _sigmoid(gv)
        silu = gv * sg
        dg_ref[...] = (da * u_ref[...] * (sg + silu * (1.0 - sg))).astype(CD)
        du_ref[...] = (da * silu).astype(CD)

    a_spec = pl.BlockSpec((tm, d), lambda i, j: (i, 0))
    w_spec = pl.BlockSpec((d, FFN_TN), lambda i, j: (0, j))
    o_spec = pl.BlockSpec((tm, FFN_TN), lambda i, j: (i, j))
    return pl.pallas_call(
        body, name=name,
        out_shape=(jax.ShapeDtypeStruct((s, f), CD), jax.ShapeDtypeStruct((s, f), CD)),
        grid=(s // tm, f // FFN_TN), in_specs=[a_spec, w_spec, o_spec, o_spec], out_specs=(o_spec, o_spec),
        compiler_params=_params("parallel", "parallel"),
    )(dxc, wd_t, g, u)


TIME_BLOCK = 256
SUBLANES = 8
GELU_C = math.sqrt(2.0 / math.pi)
GELU_A = 0.044715


def _gelu(x):
    return 0.5 * x * (1.0 + jnp.tanh(GELU_C * (x + GELU_A * x * x * x)))


def _gelu_grad(x):
    t = jnp.tanh(GELU_C * (x + GELU_A * x * x * x))
    return 0.5 * (1.0 + t) + 0.5 * x * (1.0 - t * t) * GELU_C * (1.0 + 3.0 * GELU_A * x * x)


def _neg_expm1(x):
    series = -x * (1.0 + x * (0.5 + x * (1.0 / 6.0 + x * (1.0 / 24.0))))
    return jnp.where(x > -0.05, series, 1.0 - jnp.exp(x))


def _log_sigmoid(x):
    return jnp.minimum(x, 0.0) - jnp.log1p(jnp.exp(-jnp.abs(x)))


def _shift_down(x, tail, s):
    if s == 0:
        return x
    ext = jnp.concatenate([tail, x], axis=0)
    return pltpu.roll(ext, s, axis=0)[SUBLANES:]


def _shift_up(x, head, s):
    if s == 0:
        return x
    n = x.shape[0]
    ext = jnp.concatenate([x, head], axis=0)
    return pltpu.roll(ext, n + SUBLANES - s, axis=0)[:n]


def _rg_gates(xbr, tail, cw_ref, cb, wr, wi, br, bi, ls):
    taps = [_shift_down(xbr, tail, CONV_W - 1 - k) for k in range(CONV_W)]
    xc = cb
    for k in range(CONV_W):
        xc = xc + cw_ref[pl.ds(k, 1), :] * taps[k]
    xcd = xc.astype(CD)
    r = _sigmoid(jnp.dot(xcd, wr, preferred_element_type=F32) + br)
    i = _sigmoid(jnp.dot(xcd, wi, preferred_element_type=F32) + bi)
    log_a = RG_C * r * ls
    a = jnp.exp(log_a)
    mult = jnp.sqrt(jnp.maximum(_neg_expm1(2.0 * log_a), 0.0))
    return taps, xc, r, i, log_a, a, mult


def _scan8_fwd(a, u):
    row = lax.broadcasted_iota(jnp.int32, a.shape, 0)
    for d in (1, 2, 4):
        a_s = pltpu.roll(a, d, axis=0)
        u_s = pltpu.roll(u, d, axis=0)
        m = row >= d
        u = jnp.where(m, a * u_s + u, u)
        a = jnp.where(m, a * a_s, a)
    return a, u


def _scan8_bwd(b, u):
    row = lax.broadcasted_iota(jnp.int32, b.shape, 0)
    for d in (1, 2, 4):
        b_s = pltpu.roll(b, SUBLANES - d, axis=0)
        u_s = pltpu.roll(u, SUBLANES - d, axis=0)
        m = row < SUBLANES - d
        u = jnp.where(m, b * u_s + u, u)
        b = jnp.where(m, b * b_s, b)
    return b, u


def _rglru_fwd(gate_br, x_br, cw, cb, wr, wi, br, bi, lam, name):
    s, c = x_br.shape
    nt = s // TIME_BLOCK
    tb, cbw = TIME_BLOCK, RG_BW
    groups = tb // SUBLANES

    def body(g_ref, x_ref, tail_ref, cw_ref, cb_ref, wr_ref, wi_ref, br_ref, bi_ref, lam_ref,
             y_ref, hs_ref, carry_ref, a_scr, u_scr):
        t = pl.program_id(1)

        @pl.when(t == 0)
        def _():
            carry_ref[...] = jnp.zeros_like(carry_ref)

        tail = jnp.where(t > 0, tail_ref[...], 0.0)
        ls = _log_sigmoid(lam_ref[...])
        _, xc, _, i, _, a, mult = _rg_gates(x_ref[...], tail, cw_ref, cb_ref[...], wr_ref[0], wi_ref[0],
                                            br_ref[...], bi_ref[...], ls)
        a_scr[...] = a
        u_scr[...] = mult * (i * xc)
        carry = carry_ref[...]
        for gi in range(groups):
            rows = pl.ds(gi * SUBLANES, SUBLANES)
            pa, hl = _scan8_fwd(a_scr[rows, :], u_scr[rows, :])
            hs_ref[rows, :] = hl + pa * carry
            carry = hs_ref[pl.ds(gi * SUBLANES + SUBLANES - 1, 1), :]
        carry_ref[...] = carry
        y_ref[...] = (hs_ref[...] * _gelu(g_ref[...])).astype(CD)

    blk = pl.BlockSpec((tb, cbw), lambda n, t: (t, n))
    tail = pl.BlockSpec((SUBLANES, cbw), lambda n, t: (jnp.maximum(t * groups - 1, 0), n))
    vec = pl.BlockSpec((1, cbw), lambda n, t: (0, n))
    wblk = pl.BlockSpec((1, cbw, cbw), lambda n, t: (n, 0, 0))
    return pl.pallas_call(
        body, name=name,
        out_shape=(jax.ShapeDtypeStruct((s, c), CD), jax.ShapeDtypeStruct((s, c), F32)),
        grid=(RG_BLOCKS, nt),
        in_specs=[blk, blk, tail, pl.BlockSpec((CONV_W, cbw), lambda n, t: (0, n)), vec, wblk, wblk, vec, vec, vec],
        out_specs=(blk, blk),
        scratch_shapes=[pltpu.VMEM((1, cbw), F32), pltpu.VMEM((tb, cbw), F32), pltpu.VMEM((tb, cbw), F32)],
        compiler_params=_params("parallel", "arbitrary"),
    )(gate_br, x_br, x_br, cw, cb, wr, wi, br, bi, lam)


def _rglru_bwd(dy, gate_br, x_br, hs, cw, cb, wr, wi, wrt, wit, br, bi, lam, name):
    s, c = x_br.shape
    nt = s // TIME_BLOCK
    tb, cbw = TIME_BLOCK, RG_BW
    groups = tb // SUBLANES

    def body(dy_ref, g_ref, x_ref, tail_ref, hs_ref, hprev_ref, cw_ref, cb_ref, wr_ref, wi_ref, wrt_ref, wit_ref,
             br_ref, bi_ref, lam_ref,
             dg_ref, dx_ref, dcw_ref, dcb_ref, dbr_ref, dbi_ref, dlam_ref, dwr_ref, dwi_ref,
             carry_ref, head_ref, b_scr, u_scr, dh_scr):
        tr = pl.program_id(1)
        first_block = tr == nt - 1

        @pl.when(tr == 0)
        def _():
            carry_ref[...] = jnp.zeros_like(carry_ref)
            head_ref[...] = jnp.zeros_like(head_ref)
            for ref in (dcw_ref, dcb_ref, dbr_ref, dbi_ref, dlam_ref, dwr_ref, dwi_ref):
                ref[...] = jnp.zeros_like(ref)

        tail = jnp.where(first_block, 0.0, tail_ref[...])
        lam_v = lam_ref[...]
        ls = _log_sigmoid(lam_v)
        taps, xc, r, i, log_a, a, mult = _rg_gates(x_ref[...], tail, cw_ref, cb_ref[...], wr_ref[0], wi_ref[0],
                                                   br_ref[...], bi_ref[...], ls)
        gate_v = g_ref[...]
        dyv = dy_ref[...]
        hsv = hs_ref[...]
        dg_ref[...] = (dyv * hsv * _gelu_grad(gate_v)).astype(CD)

        row = lax.broadcasted_iota(jnp.int32, a.shape, 0)
        b_scr[...] = jnp.where(row == tb - 1, 1.0, pltpu.roll(a, tb - 1, axis=0))
        u_scr[...] = dyv * _gelu(gate_v)
        carry = carry_ref[...]
        for gi in reversed(range(groups)):
            rows = pl.ds(gi * SUBLANES, SUBLANES)
            pb, gl = _scan8_bwd(b_scr[rows, :], u_scr[rows, :])
            dh_scr[rows, :] = gl + pb * carry
            carry = dh_scr[pl.ds(gi * SUBLANES, 1), :]
        dh = dh_scr[...]
        carry_ref[...] = carry * jnp.sum(jnp.where(row == 0, a, 0.0), axis=0, keepdims=True)

        hprev_tail = jnp.where(first_block, 0.0, hprev_ref[...])
        h_prev = _shift_down(hsv, hprev_tail, 1)
        da = dh * h_prev
        ixc = i * xc
        dmult = dh * ixc
        di = dh * mult * xc
        dxc = dh * mult * i
        a2 = a * a
        dlog_a = da * a - dmult * a2 / mult
        dpre_r = (dlog_a * (RG_C * ls)) * r * (1.0 - r)
        dpre_i = di * i * (1.0 - i)
        dlam_ref[...] += jnp.sum(dlog_a * r, axis=0, keepdims=True) * (RG_C * _sigmoid(-lam_v))
        dbr_ref[...] += jnp.sum(dpre_r, axis=0, keepdims=True)
        dbi_ref[...] += jnp.sum(dpre_i, axis=0, keepdims=True)
        xcd = xc.astype(CD)
        dprc = dpre_r.astype(CD)
        dpic = dpre_i.astype(CD)
        tn_dims = (((0,), (0,)), ((), ()))
        dwr_ref[0] += lax.dot_general(xcd, dprc, tn_dims, preferred_element_type=F32)
        dwi_ref[0] += lax.dot_general(xcd, dpic, tn_dims, preferred_element_type=F32)
        dxc = dxc + jnp.dot(dprc, wrt_ref[0], preferred_element_type=F32) + jnp.dot(dpic, wit_ref[0],
                                                                                    preferred_element_type=F32)
        dcb_ref[...] += jnp.sum(dxc, axis=0, keepdims=True)
        for k in range(CONV_W):
            dcw_ref[pl.ds(k, 1), :] += jnp.sum(dxc * taps[k], axis=0, keepdims=True)
        head = head_ref[...]
        dxb = jnp.zeros_like(dxc)
        for sft in range(CONV_W):
            dxb = dxb + cw_ref[pl.ds(CONV_W - 1 - sft, 1), :] * _shift_up(dxc, head, sft)
        dx_ref[...] = dxb.astype(CD)
        head_ref[...] = dxc[0:SUBLANES, :]

    blk = pl.BlockSpec((tb, cbw), lambda n, t: (nt - 1 - t, n))
    tail = pl.BlockSpec((SUBLANES, cbw), lambda n, t: (jnp.maximum((nt - 1 - t) * groups - 1, 0), n))
    vec = pl.BlockSpec((1, cbw), lambda n, t: (0, n))
    cwb = pl.BlockSpec((CONV_W, cbw), lambda n, t: (0, n))
    wblk = pl.BlockSpec((1, cbw, cbw), lambda n, t: (n, 0, 0))
    vshape = jax.ShapeDtypeStruct((1, c), F32)
    wshape = jax.ShapeDtypeStruct((RG_BLOCKS, cbw, cbw), F32)
    return pl.pallas_call(
        body, name=name,
        out_shape=(jax.ShapeDtypeStruct((s, c), CD), jax.ShapeDtypeStruct((s, c), CD),
                   jax.ShapeDtypeStruct((CONV_W, c), F32), vshape, vshape, vshape, vshape, wshape, wshape),
        grid=(RG_BLOCKS, nt),
        in_specs=[blk, blk, blk, tail, blk, tail, cwb, vec, wblk, wblk, wblk, wblk, vec, vec, vec],
        out_specs=(blk, blk, cwb, vec, vec, vec, vec, wblk, wblk),
        scratch_shapes=[pltpu.VMEM((1, cbw), F32), pltpu.VMEM((SUBLANES, cbw), F32),
                        pltpu.VMEM((tb, cbw), F32), pltpu.VMEM((tb, cbw), F32), pltpu.VMEM((tb, cbw), F32)],
        compiler_params=_params("parallel", "arbitrary"),
    )(dy, gate_br, x_br, x_br, hs, hs, cw, cb, wr, wi, wrt, wit, br, bi, lam)


ATT_BLOCK = 256
ATT_Q_BLOCK = 1024
ATT_RATIO = ATT_Q_BLOCK // ATT_BLOCK
ATT_SCALE = 1.0 / math.sqrt(SB_HEAD_DIM)
N_PAIRS = SB_HEADS * SB_HEAD_DIM // LANES
NT_DIMS = (((1,), (1,)), ((), ()))
TN_DIMS = (((0,), (0,)), ((), ()))


LOG2E = 1.4426950408889634


def _neg_abs(x):
    bits = lax.bitcast_convert_type(x, jnp.uint32) | jnp.uint32(0x80000000)
    return lax.bitcast_convert_type(bits, F32)


def _qk(qx, kb):
    return lax.dot_general(qx, kb, NT_DIMS, preferred_element_type=F32)


def _sb_logits(qk, valid):
    z2 = qk * (ATT_SCALE * LOG2E)
    lb2 = jnp.minimum(z2, 0.0) - jnp.log2(1.0 + jnp.exp2(_neg_abs(z2)))
    l2 = lb2 - z2
    if valid is not None:
        l2 = jnp.where(valid, l2, 0.0)
    return lb2, l2


def _hi_lo(x):
    hi = x.astype(CD)
    lo = (x - hi.astype(F32)).astype(CD)
    return jnp.concatenate([hi, lo], axis=1)


def _tri(strict, stacked):
    r = lax.broadcasted_iota(jnp.int32, (ATT_BLOCK, ATT_BLOCK), 0)
    c = lax.broadcasted_iota(jnp.int32, (ATT_BLOCK, ATT_BLOCK), 1)
    m = (r > c if strict else r >= c).astype(CD)
    return jnp.concatenate([m, m], axis=0) if stacked else m


def _attn_fwd(qkv, name):
    _, s, _ = qkv.shape
    tq, t = ATT_Q_BLOCK, ATT_BLOCK
    nblk = s // tq

    def body(q_ref, k_ref, v_ref, o_ref, qk_scr, w_scr):
        i = pl.program_id(1)
        lane = lax.broadcasted_iota(jnp.int32, (1, LANES), 1)
        head_masks = (lane < SB_HEAD_DIM, lane >= SB_HEAD_DIM)
        q = q_ref[0]
        qs = [jnp.where(m, q, jnp.zeros_like(q)) for m in head_masks]
        tri = _tri(True, False)
        rr = lax.broadcasted_iota(jnp.int32, (tq, t), 0)
        cc = lax.broadcasted_iota(jnp.int32, (tq, t), 1)

        def rows_of(j):
            return pl.ds(pl.multiple_of(j * t, t), t)

        def tail(x, row0):
            return x if row0 == 0 else x[row0:]

        def start_logits(j, row0=0):
            kb = k_ref[0, rows_of(j), :]
            for hd in range(2):
                qk_scr[hd, row0:, :] = _qk(tail(qs[hd], row0), kb)

        def weights(run, diagonal=False, row0=0):
            new_run = []
            valid = (cc < rr)[:tq - row0] if diagonal else None
            for hd in range(2):
                lb2, l2 = _sb_logits(qk_scr[hd, row0:, :], valid)
                w = jnp.exp2(lb2 + (tail(run[hd], row0) + jnp.dot(l2.astype(CD), tri, preferred_element_type=F32)))
                if valid is not None:
                    w = jnp.where(valid, w, 0.0)
                w_scr[row0:, hd * t:(hd + 1) * t] = w.astype(CD)
                rowsum = jnp.sum(l2, axis=1, keepdims=True)
                if row0:
                    rowsum = jnp.concatenate([jnp.zeros((row0, 1), F32), rowsum], axis=0)
                new_run.append(run[hd] + rowsum)
            return tuple(new_run)

        def apply_weights(j, row0=0):
            vb = v_ref[0, rows_of(j), :]
            vcat = jnp.concatenate([jnp.where(m, vb, jnp.zeros_like(vb)) for m in head_masks], axis=0)
            inc = jnp.dot(w_scr[row0:, :], vcat, preferred_element_type=F32)
            return inc if row0 == 0 else jnp.concatenate([jnp.zeros((row0, LANES), F32), inc], axis=0)

        zero = jnp.zeros((tq, 1), F32)
        last = ATT_RATIO - 1
        start_logits(ATT_RATIO * i + last, last * t)
        run = weights((zero, zero), True, last * t)
        oacc = jnp.zeros((tq, LANES), F32)
        for d in reversed(range(last)):
            start_logits(ATT_RATIO * i + d, d * t)
            oacc = oacc + apply_weights(ATT_RATIO * i + d + 1, (d + 1) * t)
            run = weights(run, True, d * t)
        start_logits(jnp.maximum(ATT_RATIO * i - 1, 0))

        def step(jj, carry):
            run, oacc = carry
            b = ATT_RATIO * i - 1 - jj
            oacc = oacc + apply_weights(b + 1)
            run = weights(run)
            start_logits(jnp.maximum(b - 1, 0))
            return run, oacc

        run, oacc = lax.fori_loop(0, ATT_RATIO * i, step, (run, oacc))
        o_ref[0] = oacc + apply_weights(0)

    return pl.pallas_call(
        body, name=name, out_shape=jax.ShapeDtypeStruct((N_PAIRS, s, LANES), F32), grid=(N_PAIRS, nblk),
        in_specs=[pl.BlockSpec((1, tq, LANES), lambda p, i: (p, i, 0)),
                  pl.BlockSpec((1, s, LANES), lambda p, i: (N_PAIRS + p, 0, 0)),
                  pl.BlockSpec((1, s, LANES), lambda p, i: (2 * N_PAIRS + p, 0, 0))],
        out_specs=pl.BlockSpec((1, tq, LANES), lambda p, i: (p, i, 0)),
        scratch_shapes=[pltpu.VMEM((2, tq, t), F32), pltpu.VMEM((tq, 2 * t), CD)],
        compiler_params=_params("parallel", "arbitrary"),
    )(qkv, qkv, qkv)


def _attn_bwd(qkv, o, do, name):
    _, s, _ = qkv.shape
    tq, t = ATT_Q_BLOCK, ATT_BLOCK
    nblk = s // tq

    def body(q_ref, k_ref, v_ref, o_ref, do_ref, dq_ref, dk_ref, dv_ref, qk_scr, dw_scr, w_scr, dz_scr):
        i = pl.program_id(1)

        @pl.when(i == 0)
        def _():
            dk_ref[...] = jnp.zeros_like(dk_ref)
            dv_ref[...] = jnp.zeros_like(dv_ref)

        lane = lax.broadcasted_iota(jnp.int32, (1, LANES), 1)
        head_masks = (lane < SB_HEAD_DIM, lane >= SB_HEAD_DIM)
        q = q_ref[0]
        dov = do_ref[0]
        ov = o_ref[0]
        qs = [jnp.where(m, q, jnp.zeros_like(q)) for m in head_masks]
        q_scaled_t = jnp.concatenate([(qx.astype(F32) * ATT_SCALE).T for qx in qs], axis=1).astype(CD)
        docs = [jnp.where(m, dov, 0.0).astype(CD) for m in head_masks]
        docat_t = jnp.concatenate([jnp.where(m, dov, 0.0).T for m in head_masks], axis=1).astype(CD)
        totals = [jnp.sum(d.astype(F32) * ov, axis=1, keepdims=True) for d in docs]
        tri = _tri(True, False)
        tri_incl = _tri(False, True)
        rr = lax.broadcasted_iota(jnp.int32, (tq, t), 0)
        cc = lax.broadcasted_iota(jnp.int32, (tq, t), 1)

        def rows_of(j):
            return pl.ds(pl.multiple_of(j * t, t), t)

        def tail(x, row0):
            return x if row0 == 0 else x[row0:]

        def pad_rows(x, row0):
            return x if row0 == 0 else jnp.concatenate([jnp.zeros((row0, x.shape[1]), x.dtype), x], axis=0)

        def start_products(j, row0=0):
            kb = k_ref[0, rows_of(j), :]
            vb = v_ref[0, rows_of(j), :]
            for hd in range(2):
                qk_scr[hd, row0:, :] = _qk(tail(qs[hd], row0), kb)
                dw_scr[hd, row0:, :] = lax.dot_general(tail(docs[hd], row0), vb, NT_DIMS, preferred_element_type=F32)

        def logit_grads(run, erun, diagonal=False, row0=0):
            new_run, new_erun = [], []
            valid = (cc < rr)[:tq - row0] if diagonal else None
            for hd in range(2):
                lb2, l2 = _sb_logits(qk_scr[hd, row0:, :], valid)
                w = jnp.exp2(lb2 + (tail(run[hd], row0) + jnp.dot(l2.astype(CD), tri, preferred_element_type=F32)))
                if valid is not None:
                    w = jnp.where(valid, w, 0.0)
                wc = w.astype(CD)
                w_scr[hd * tq + row0:(hd + 1) * tq, :] = wc
                e = dw_scr[hd, row0:, :] * wc.astype(F32)
                prefix = (tail(totals[hd] - erun[hd], row0)
                          - jnp.dot(_hi_lo(e), tri_incl, preferred_element_type=F32))
                dz = e - jnp.exp2(lb2) * (e + prefix)
                if valid is not None:
                    dz = jnp.where(valid, dz, 0.0)
                dz_scr[hd * tq + row0:(hd + 1) * tq, :] = dz.astype(CD)
                new_run.append(run[hd] + pad_rows(jnp.sum(l2, axis=1, keepdims=True), row0))
                new_erun.append(erun[hd] + pad_rows(jnp.sum(e, axis=1, keepdims=True), row0))
            return tuple(new_run), tuple(new_erun)

        def apply_grads(j, row0=0):
            rows = rows_of(j)
            kb = k_ref[0, rows, :]
            kcat = jnp.concatenate([jnp.where(m, kb, jnp.zeros_like(kb)) for m in head_masks], axis=0)
            dz_heads = [dz_scr[hd * tq + row0:(hd + 1) * tq, :] for hd in range(2)]
            w_heads = [w_scr[hd * tq + row0:(hd + 1) * tq, :] for hd in range(2)]
            q_t = jnp.concatenate([q_scaled_t[:, hd * tq + row0:(hd + 1) * tq] for hd in range(2)], axis=1)
            do_t = jnp.concatenate([docat_t[:, hd * tq + row0:(hd + 1) * tq] for hd in range(2)], axis=1)
            dk_ref[0, :, rows] += jnp.dot(q_t, jnp.concatenate(dz_heads, axis=0), preferred_element_type=F32)
            dv_ref[0, :, rows] += jnp.dot(do_t, jnp.concatenate(w_heads, axis=0), preferred_element_type=F32)
            return pad_rows(jnp.dot(jnp.concatenate(dz_heads, axis=1), kcat, preferred_element_type=F32), row0)

        zero = jnp.zeros((tq, 1), F32)
        last = ATT_RATIO - 1
        start_products(ATT_RATIO * i + last, last * t)
        run, erun = logit_grads((zero, zero), (zero, zero), True, last * t)
        dqacc = jnp.zeros((tq, LANES), F32)
        for d in reversed(range(last)):
            start_products(ATT_RATIO * i + d, d * t)
            dqacc = dqacc + apply_grads(ATT_RATIO * i + d + 1, (d + 1) * t)
            run, erun = logit_grads(run, erun, True, d * t)
        start_products(jnp.maximum(ATT_RATIO * i - 1, 0))

        def step(jj, carry):
            run, erun, dqacc = carry
            b = ATT_RATIO * i - 1 - jj
            dqacc = dqacc + apply_grads(b + 1)
            run, erun = logit_grads(run, erun)
            start_products(jnp.maximum(b - 1, 0))
            return run, erun, dqacc

        run, erun, dqacc = lax.fori_loop(0, ATT_RATIO * i, step, (run, erun, dqacc))
        dq_ref[0] = (dqacc + apply_grads(0)) * ATT_SCALE

    qblk = pl.BlockSpec((1, tq, LANES), lambda p, i: (p, i, 0))
    full = pl.BlockSpec((1, LANES, s), lambda p, i: (p, 0, 0))
    shape = jax.ShapeDtypeStruct((N_PAIRS, s, LANES), F32)
    shape_t = jax.ShapeDtypeStruct((N_PAIRS, LANES, s), F32)
    dq, dk_t, dv_t = pl.pallas_call(
        body, name=name, out_shape=(shape, shape_t, shape_t), grid=(N_PAIRS, nblk),
        in_specs=[qblk,
                  pl.BlockSpec((1, s, LANES), lambda p, i: (N_PAIRS + p, 0, 0)),
                  pl.BlockSpec((1, s, LANES), lambda p, i: (2 * N_PAIRS + p, 0, 0)),
                  qblk, qblk],
        out_specs=(qblk, full, full),
        scratch_shapes=[pltpu.VMEM((2, tq, t), F32), pltpu.VMEM((2, tq, t), F32),
                        pltpu.VMEM((2 * tq, t), CD), pltpu.VMEM((2 * tq, t), CD)],
        compiler_params=_params("parallel", "arbitrary"),
    )(qkv, qkv, qkv, o, do)
    return dq, jnp.swapaxes(dk_t, 1, 2), jnp.swapaxes(dv_t, 1, 2)


ADAM_COLS = 1024


def _adamw(w, g, m, v, name):
    shape = w.shape
    rows, cols = (shape[-2], shape[-1]) if len(shape) >= 2 else (1, shape[-1])
    lead = w.size // (rows * cols)
    tr = _pick(rows, (512, 256, 128, 64, 32, 16, 8))

    def body(w_ref, g_ref, m_ref, v_ref, d_ref, nm_ref, nv_ref):
        gv = g_ref[...]
        nm = ADAM_B1 * m_ref[...] + (1.0 - ADAM_B1) * gv
        nv = ADAM_B2 * v_ref[...] + (1.0 - ADAM_B2) * (gv * gv)
        m_hat = nm / (1.0 - ADAM_B1 ** ADAM_STEP)
        v_hat = nv / (1.0 - ADAM_B2 ** ADAM_STEP)
        d_ref[...] = -ADAM_LR * (m_hat / (jnp.sqrt(v_hat) + ADAM_EPS) + ADAM_WD * w_ref[...])
        nm_ref[...] = nm
        nv_ref[...] = nv

    blk = pl.BlockSpec((1, tr, cols), lambda l, i: (l, i, 0))
    out = jax.ShapeDtypeStruct((lead, rows, cols), F32)
    d, nm, nv = pl.pallas_call(
        body, name=name, out_shape=(out, out, out), grid=(lead, rows // tr),
        in_specs=[blk, blk, blk, blk], out_specs=(blk, blk, blk), compiler_params=_params("parallel", "parallel"),
    )(*[a.reshape(lead, rows, cols) for a in (w, g, m, v)])
    return d.reshape(shape), nm.reshape(shape), nv.reshape(shape)


HBM = pl.BlockSpec(memory_space=pltpu.HBM)


def _coords():
    return lax.axis_index("x"), lax.axis_index("y"), lax.axis_index("c")


def _other_chips(x, y):
    return [(1 - x, y), (x, 1 - y), (1 - x, 1 - y)]


def _allgather_chips(shard, name):
    r, cols = shard.shape
    half = r // 2

    def body(src_ref, out_ref, send_sems, recv_sems):
        x, y, c = _coords()
        sibling = (x, y, 1 - c)
        chips = _other_chips(x, y)

        def rows(px, py, h):
            return out_ref.at[2 * px + py, pl.ds(h * half, half), :]

        def copy(k, block, to, src=None):
            return pltpu.make_async_remote_copy(
                src_ref=rows(*block) if src is None else src, dst_ref=rows(*block),
                send_sem=send_sems.at[k], recv_sem=recv_sems.at[k], device_id=to, device_id_type=MESH)

        my_half = src_ref.at[pl.ds(c * half, half), :]
        first = [copy(j, (x, y, c), (*chip, c), src=my_half) for j, chip in enumerate(chips)]
        for cp in first:
            cp.start()
        passed = [copy(3 + j, (*chip, c), sibling) for j, chip in enumerate(chips)]
        for j, chip in enumerate(chips):
            copy(j, (*chip, c), (x, y, c)).wait_recv()
            passed[j].start()
        for j, chip in enumerate(chips):
            copy(3 + j, (*chip, 1 - c), (x, y, c)).wait_recv()
        for cp in first + passed:
            cp.wait_send()

    return pl.pallas_call(
        body, name=name, out_shape=jax.ShapeDtypeStruct((N_CHIPS, r, cols), shard.dtype),
        in_specs=[HBM], out_specs=HBM,
        scratch_shapes=[pltpu.SemaphoreType.DMA((6,)), pltpu.SemaphoreType.DMA((6,))],
    )(shard)


def _exchange_sibling_halves(g, name):
    n, r, cols = g.shape
    half = r // 2

    def body(g_ref, out_ref, send_sem, recv_sem):
        x, y, c = _coords()
        cp = pltpu.make_async_remote_copy(
            src_ref=g_ref.at[:, pl.ds((1 - c) * half, half), :], dst_ref=out_ref,
            send_sem=send_sem, recv_sem=recv_sem, device_id=(x, y, 1 - c), device_id_type=MESH)
        cp.start()
        cp.wait()

    return pl.pallas_call(
        body, name=name, out_shape=jax.ShapeDtypeStruct((n, half, cols), g.dtype),
        in_specs=[HBM], out_specs=HBM,
        scratch_shapes=[pltpu.SemaphoreType.DMA, pltpu.SemaphoreType.DMA],
    )(g)


def _scatter_to_chips(p, name):
    n, h, cols = p.shape

    def body(p_ref, out_ref, send_sems, recv_sems, local_sem):
        x, y, c = _coords()
        me = 2 * x + y
        mine = pltpu.make_async_copy(p_ref.at[me], out_ref.at[me], local_sem)
        mine.start()
        sends = []
        for j, (px, py) in enumerate(_other_chips(x, y)):
            sends.append(pltpu.make_async_remote_copy(
                src_ref=p_ref.at[2 * px + py], dst_ref=out_ref.at[me],
                send_sem=send_sems.at[j], recv_sem=recv_sems.at[j], device_id=(px, py, c), device_id_type=MESH))
        for cp in sends:
            cp.start()
        for j, (px, py) in enumerate(_other_chips(x, y)):
            pltpu.make_async_remote_copy(
                src_ref=p_ref.at[me], dst_ref=out_ref.at[2 * px + py],
                send_sem=send_sems.at[j], recv_sem=recv_sems.at[j], device_id=(px, py, c),
                device_id_type=MESH).wait_recv()
        for cp in sends:
            cp.wait_send()
        mine.wait()

    return pl.pallas_call(
        body, name=name, out_shape=jax.ShapeDtypeStruct((n, h, cols), p.dtype),
        in_specs=[HBM], out_specs=HBM,
        scratch_shapes=[pltpu.SemaphoreType.DMA((3,)), pltpu.SemaphoreType.DMA((3,)), pltpu.SemaphoreType.DMA],
    )(p)


def _share_halves(v, name):
    h = v.shape[0] // 2

    def body(v_ref, out_ref, send_sem, recv_sem):
        x, y, c = _coords()
        cp = pltpu.make_async_remote_copy(
            src_ref=v_ref.at[pl.ds(c * h, h), :], dst_ref=out_ref.at[pl.ds(c * h, h), :],
            send_sem=send_sem, recv_sem=recv_sem, device_id=(x, y, 1 - c), device_id_type=MESH)
        cp.start()
        pltpu.make_async_remote_copy(
            src_ref=v_ref.at[pl.ds(c * h, h), :], dst_ref=out_ref.at[pl.ds((1 - c) * h, h), :],
            send_sem=send_sem, recv_sem=recv_sem, device_id=(x, y, 1 - c), device_id_type=MESH).wait_recv()
        cp.wait_send()

    return pl.pallas_call(
        body, name=name, out_shape=jax.ShapeDtypeStruct(v.shape, v.dtype),
        in_specs=[HBM], out_specs=HBM, input_output_aliases={0: 0},
        scratch_shapes=[pltpu.SemaphoreType.DMA, pltpu.SemaphoreType.DMA],
    )(v)


def _allreduce_small(v, name):
    r, cols = v.shape

    def body(v_ref, out_ref, buf_ref, send_sems, recv_sems):
        x, y, c = _coords()
        me = 4 * x + 2 * y + c
        buf_ref[me] = v_ref[...]
        sends = []
        for k in range(1, N_DEV):
            px = 1 - x if k & 4 else x
            py = 1 - y if k & 2 else y
            pc = 1 - c if k & 1 else c
            sends.append(pltpu.make_async_remote_copy(
                src_ref=v_ref, dst_ref=buf_ref.at[me], send_sem=send_sems.at[k - 1], recv_sem=recv_sems.at[k - 1],
                device_id=(px, py, pc), device_id_type=MESH))
        for cp in sends:
            cp.start()
        for cp in sends:
            cp.wait()
        acc = buf_ref[0]
        for d in range(1, N_DEV):
            acc = acc + buf_ref[d]
        out_ref[...] = acc

    return pl.pallas_call(
        body, name=name, out_shape=jax.ShapeDtypeStruct((r, cols), F32),
        in_specs=[pl.BlockSpec(memory_space=pltpu.VMEM)], out_specs=pl.BlockSpec(memory_space=pltpu.VMEM),
        scratch_shapes=[pltpu.VMEM((N_DEV, r, cols), F32), pltpu.SemaphoreType.DMA((N_DEV - 1,)),
                        pltpu.SemaphoreType.DMA((N_DEV - 1,))],
    )(v)


def _add_sibling(g, from_sibling, core, name):
    n, h, cols = from_sibling.shape
    tr = _pick(h, (512, 256, 128))
    steps = h // tr

    def body(core_ref, a_ref, b_ref, o_ref):
        o_ref[...] = (a_ref[...] + b_ref[...]).astype(o_ref.dtype)

    return pl.pallas_call(
        body, name=name, out_shape=jax.ShapeDtypeStruct(from_sibling.shape, jnp.bfloat16),
        grid_spec=pltpu.PrefetchScalarGridSpec(
            num_scalar_prefetch=1, grid=(n, steps),
            in_specs=[pl.BlockSpec((1, tr, cols), lambda s, i, core_ref: (s, core_ref[0] * steps + i, 0)),
                      pl.BlockSpec((1, tr, cols), lambda s, i, core_ref: (s, i, 0))],
            out_specs=pl.BlockSpec((1, tr, cols), lambda s, i, core_ref: (s, i, 0))),
        compiler_params=_params("parallel", "parallel"),
    )(core.reshape(1).astype(jnp.int32), g, from_sibling)


def _sum_slots(p, core, name):
    n, r, cols = p.shape
    tr = _pick(r, (512, 256, 128))
    steps = r // tr

    def body(core_ref, p_ref, o_ref):
        o_ref[...] = ((p_ref[0].astype(F32) + p_ref[1].astype(F32)) + p_ref[2].astype(F32)) + p_ref[3].astype(F32)

    return pl.pallas_call(
        body, name=name, out_shape=jax.ShapeDtypeStruct((2 * r, cols), F32),
        grid_spec=pltpu.PrefetchScalarGridSpec(
            num_scalar_prefetch=1, grid=(steps,),
            in_specs=[pl.BlockSpec((n, tr, cols), lambda i, core_ref: (0, i, 0))],
            out_specs=pl.BlockSpec((tr, cols), lambda i, core_ref: (core_ref[0] * steps + i, 0))),
        compiler_params=_params("parallel"),
    )(core.reshape(1).astype(jnp.int32), p)


PACK_COLS = 1024


def _pack_shards(parts):
    return jnp.concatenate([p.reshape(-1, PACK_COLS) for p in parts], axis=0)


def _unpack_shards(buf, shapes):
    out, row = [], 0
    for shp in shapes:
        nrows = math.prod(shp) // PACK_COLS
        out.append(buf[..., row:row + nrows, :].reshape(buf.shape[:-2] + tuple(shp)))
        row += nrows
    return out


def _local_step(x, target, w):
    t = lambda a: a.T
    g = {}
    h0 = _rms_fwd(x, w["norm_mix_g"][0], "rms_mix0")
    w_in_g, w_in_x = w["a_w_in"][:, :D_RNN], w["a_w_in"][:, D_RNN:]
    gate_br = _matmul([(h0, w_in_g)], F32, "mm_a_gate")
    x_br = _matmul([(h0, w_in_x)], F32, "mm_a_xbr")
    y_a, hs = _rglru_fwd(gate_br, x_br, w["a_conv_w"], w["a_conv_b"], w["a_w_r"], w["a_w_i"], w["a_b_r"],
                         w["a_b_i"], w["a_lambda"], "rglru_fwd")
    x1 = _matmul([(y_a, w["a_w_out"])], F32, "mm_a_out", addend=x)
    h1 = _rms_fwd(x1, w["norm_ffn_g"][0], "rms_ffn0")
    fg0, fu0, act0 = _ffn_up(h1, w["ffn_w_gate"][0], w["ffn_w_up"][0], "ffn0_up")
    x2 = _matmul([(act0, w["ffn_w_down"][0])], F32, "mm_f0_down", addend=x1)
    h2 = _rms_fwd(x2, w["norm_mix_g"][1], "rms_mix1")
    qkv = _matmul([(h2, w["b_w_qkv"])], CD, "mm_b_qkv", out_lbm=True, tn=1024)
    o = _attn_fwd(qkv, "attn_fwd")
    x3 = _matmul([(o, w["b_w_out"])], F32, "mm_b_out", a_lbm=True, addend=x2)
    h3 = _rms_fwd(x3, w["norm_ffn_g"][1], "rms_ffn1")
    fg1, fu1, act1 = _ffn_up(h3, w["ffn_w_gate"][1], w["ffn_w_up"][1], "ffn1_up")
    x4 = _matmul([(act1, w["ffn_w_down"][1])], F32, "mm_f1_down", addend=x3)
    loss, dx4, dx4c, g["final_g"] = _loss_head(x4, w["final_g"], target, "loss_head")

    def ffn_bwd(dx_out, dxc, h, x_in, fg, fu, act, layer, tag):
        dg, du = _ffn_dact(dxc, t(w["ffn_w_down"][layer]), fg, fu, "ffn_" + tag + "_dact")
        dwd = _matmul([(act, dxc)], F32, "mm_" + tag + "_dwd", trans_a=True)
        dwg = _matmul([(h, dg)], F32, "mm_" + tag + "_dwg", trans_a=True)
        dwu = _matmul([(h, du)], F32, "mm_" + tag + "_dwu", trans_a=True)
        dh = _matmul([(dg, t(w["ffn_w_gate"][layer])), (du, t(w["ffn_w_up"][layer]))], F32, "mm_" + tag + "_dh")
        dx_in, dx_in_c, dgain = _rms_bwd(dh, x_in, w["norm_ffn_g"][layer], dx_out, "rms_ffn" + tag + "_bwd")
        return dx_in, dx_in_c, dgain, dwg, dwu, dwd

    dx3, dx3c, dgf1, dwg1, dwu1, dwd1 = ffn_bwd(dx4, dx4c, h3, x3, fg1, fu1, act1, 1, "f1")
    do = _matmul([(dx3c, t(w["b_w_out"]))], F32, "mm_b_do", out_lbm=True, tn=1024)
    g["b_w_out"] = _matmul([(o, dx3c)], F32, "mm_b_dwout", trans_a=True, a_lbm=True)
    dq, dk, dv = _attn_bwd(qkv, o, do, "attn_bwd")
    wq_t = t(w["b_w_qkv"])
    parts = (dq, dk, dv)
    g["b_w_qkv"] = jnp.concatenate(
        [_matmul([(h2, p)], F32, "mm_b_dwqkv%d" % n, trans_a=True, b_lbm=True) for n, p in enumerate(parts)], axis=1)
    dh2 = _matmul([(p, wq_t[n * D_MODEL:(n + 1) * D_MODEL]) for n, p in enumerate(parts)], F32, "mm_b_dh",
                  a_lbm=True)
    dx2, dx2c, dgm1 = _rms_bwd(dh2, x2, w["norm_mix_g"][1], dx3, "rms_mix1_bwd")
    dx1, dx1c, dgf0, dwg0, dwu0, dwd0 = ffn_bwd(dx2, dx2c, h1, x1, fg0, fu0, act0, 0, "f0")
    dy_a = _matmul([(dx1c, t(w["a_w_out"]))], F32, "mm_a_dy")
    g["a_w_out"] = _matmul([(y_a, dx1c)], F32, "mm_a_dwout", trans_a=True)
    wrt = jnp.swapaxes(w["a_w_r"], 1, 2)
    wit = jnp.swapaxes(w["a_w_i"], 1, 2)
    (dgate, dxbr, g["a_conv_w"], g["a_conv_b"], g["a_b_r"], g["a_b_i"], g["a_lambda"], g["a_w_r"],
     g["a_w_i"]) = _rglru_bwd(dy_a, gate_br, x_br, hs, w["a_conv_w"], w["a_conv_b"], w["a_w_r"], w["a_w_i"], wrt, wit,
                              w["a_b_r"], w["a_b_i"], w["a_lambda"], "rglru_bwd")
    g["a_w_in"] = jnp.concatenate([_matmul([(h0, dgate)], F32, "mm_a_dwin_g", trans_a=True),
                                   _matmul([(h0, dxbr)], F32, "mm_a_dwin_x", trans_a=True)], axis=1)
    dh0 = _matmul([(dgate, t(w_in_g)), (dxbr, t(w_in_x))], F32, "mm_a_dh")
    dx0, _, dgm0 = _rms_bwd(dh0, x, w["norm_mix_g"][0], dx1, "rms_mix0_bwd")
    g["norm_mix_g"] = jnp.concatenate([dgm0, dgm1], axis=0)
    g["norm_ffn_g"] = jnp.concatenate([dgf0, dgf1], axis=0)
    g["ffn_w_gate"] = jnp.stack([dwg0, dwg1])
    g["ffn_w_up"] = jnp.stack([dwu0, dwu1])
    g["ffn_w_down"] = jnp.stack([dwd0, dwd1])
    return loss, dx0, g


WEIGHTS = ["norm_mix_g", "norm_ffn_g", "a_w_in", "a_conv_w", "a_conv_b", "a_w_r", "a_b_r", "a_w_i", "a_b_i",
           "a_lambda", "a_w_out", "b_w_qkv", "b_w_out", "ffn_w_gate", "ffn_w_up", "ffn_w_down", "final_g"]
BIG = [("a_w_in", 2), ("a_w_r", 2), ("a_w_i", 2), ("a_w_out", 1), ("b_w_qkv", 2), ("b_w_out", 1),
       ("ffn_w_gate", 2), ("ffn_w_up", 2), ("ffn_w_down", 1)]
SMALL = ["norm_mix_g", "norm_ffn_g", "a_conv_w", "a_conv_b", "a_b_r", "a_b_i", "a_lambda", "final_g"]


def _join_chips(stack, axis):
    return jnp.concatenate([stack[s] for s in range(N_CHIPS)], axis=axis)


def _split_chips(full, axis):
    return jnp.stack(jnp.split(full, N_CHIPS, axis=axis))


def _step(x, target, weights, moments_m, moments_v):
    chip = 2 * lax.axis_index("x") + lax.axis_index("y")
    core = lax.axis_index("c")
    shard_shapes = [weights[n].shape for n, _ in BIG]
    packed = _pack_shards([weights[n].astype(CD) for n, _ in BIG])
    gathered = _allgather_chips(packed, "allgather_weights")
    full = {}
    for (n, axis), stack in zip(BIG, _unpack_shards(gathered, shard_shapes)):
        own = weights[n].astype(CD)
        joined = jnp.concatenate([jnp.where(chip == s, own, stack[s]) for s in range(N_CHIPS)], axis=axis)
        full[n] = joined[0] if joined.shape[0] == 1 else joined
    cw_rows = jnp.zeros((N_CHIPS, CONV_W, RG_BW), F32)
    cw_rows = lax.dynamic_update_slice(cw_rows, jnp.where(core == 0, weights["a_conv_w"], 0.0), (chip, 0, 0))
    cw_all = _allreduce_small(cw_rows.reshape(-1, LANES), "allgather_conv_w").reshape(N_CHIPS, CONV_W, RG_BW)
    full["a_conv_w"] = jnp.concatenate([cw_all[s] for s in range(N_CHIPS)], axis=1)
    for n in ("norm_mix_g", "norm_ffn_g", "final_g"):
        full[n] = weights[n]
    for n in ("a_conv_b", "a_b_r", "a_b_i", "a_lambda"):
        full[n] = weights[n]
    loss, dx, grads = _local_step(x[0], target[0], full)
    small_parts = [grads[n].reshape(-1) for n in SMALL] + [loss.reshape(-1)]
    sizes = [p.shape[0] for p in small_parts]
    small = _allreduce_small(jnp.concatenate(small_parts).reshape(-1, LANES), "allreduce_small").reshape(-1)
    red, pos = {}, 0
    for n, sz in zip(SMALL + ["loss"], sizes):
        red[n] = small[pos:pos + sz]
        pos += sz
    loss_out = red["loss"][0]
    g_out = {}
    for n in SMALL:
        if n == "a_conv_w":
            g_out[n] = lax.dynamic_slice(red[n].reshape(CONV_W, D_RNN), (0, chip * RG_BW), (CONV_W, RG_BW)).reshape(
                weights[n].shape)
        else:
            g_out[n] = red[n].reshape(weights[n].shape)
    stacks = []
    for n, axis in BIG:
        gfull = grads[n].reshape((1,) + grads[n].shape) if grads[n].ndim == len(weights[n].shape) - 1 else grads[n]
        stacks.append(_split_chips(gfull, axis).reshape(N_CHIPS, -1, PACK_COLS))
    gbuf = jnp.concatenate(stacks, axis=1)
    from_sibling = _exchange_sibling_halves(gbuf, "rs_sibling")
    chip_partial = _add_sibling(gbuf, from_sibling, core, "rs_add_sibling")
    from_chips = _scatter_to_chips(chip_partial, "rs_chips")
    reduced = _share_halves(_sum_slots(from_chips, core, "rs_sum_chips"), "rs_share")
    for (n, _), gsh in zip(BIG, _unpack_shards(reduced, shard_shapes)):
        g_out[n] = gsh
    outs_g, outs_d, outs_m, outs_v = [], [], [], []
    for n in WEIGHTS:
        d, nm, nv = _adamw(weights[n], g_out[n], moments_m[n], moments_v[n], "adamw_" + n)
        outs_g.append(g_out[n])
        outs_d.append(d)
        outs_m.append(nm)
        outs_v.append(nv)
    return (loss_out, dx[None], *outs_g, *outs_d, *outs_m, *outs_v)


def kernel(x, norm_mix_g, norm_ffn_g, a_w_in, a_conv_w, a_conv_b, a_w_r, a_b_r, a_w_i, a_b_i, a_lambda, a_w_out, b_w_qkv, b_w_out, ffn_w_gate, ffn_w_up, ffn_w_down, final_g, loss_target, m_norm_mix_g, m_norm_ffn_g, m_a_w_in, m_a_conv_w, m_a_conv_b, m_a_w_r, m_a_b_r, m_a_w_i, m_a_b_i, m_a_lambda, m_a_w_out, m_b_w_qkv, m_b_w_out, m_ffn_w_gate, m_ffn_w_up, m_ffn_w_down, m_final_g, v_norm_mix_g, v_norm_ffn_g, v_a_w_in, v_a_conv_w, v_a_conv_b, v_a_w_r, v_a_b_r, v_a_w_i, v_a_b_i, v_a_lambda, v_a_w_out, v_b_w_qkv, v_b_w_out, v_ffn_w_gate, v_ffn_w_up, v_ffn_w_down, v_final_g):
    ws = [norm_mix_g, norm_ffn_g, a_w_in, a_conv_w, a_conv_b, a_w_r, a_b_r, a_w_i, a_b_i, a_lambda, a_w_out, b_w_qkv,
          b_w_out, ffn_w_gate, ffn_w_up, ffn_w_down, final_g]
    ms = [m_norm_mix_g, m_norm_ffn_g, m_a_w_in, m_a_conv_w, m_a_conv_b, m_a_w_r, m_a_b_r, m_a_w_i, m_a_b_i, m_a_lambda,
          m_a_w_out, m_b_w_qkv, m_b_w_out, m_ffn_w_gate, m_ffn_w_up, m_ffn_w_down, m_final_g]
    vs = [v_norm_mix_g, v_norm_ffn_g, v_a_w_in, v_a_conv_w, v_a_conv_b, v_a_w_r, v_a_b_r, v_a_w_i, v_a_b_i, v_a_lambda,
          v_a_w_out, v_b_w_qkv, v_b_w_out, v_ffn_w_gate, v_ffn_w_up, v_ffn_w_down, v_final_g]
    return _step(x, loss_target, dict(zip(WEIGHTS, ws)), dict(zip(WEIGHTS, ms)), dict(zip(WEIGHTS, vs)))
```

```python
import functools
import math

import jax
import jax.numpy as jnp
from jax import lax
from jax.experimental import pallas as pl
from jax.experimental.pallas import tpu as pltpu

F32 = jnp.float32
CD = jnp.bfloat16

D_MODEL = 1024
D_RNN = 1024
RG_BLOCKS = 4
RG_BW = 256
CONV_W = 4
RG_C = 8.0
SB_HEADS = 16
SB_HEAD_DIM = 64
D_FF = 2816
RMS_EPS = 1e-6
N_CHIPS = 4
N_DEV = 8

ADAM_LR = 0.001
ADAM_B1 = 0.9
ADAM_B2 = 0.999
ADAM_EPS = 1e-08
ADAM_WD = 0.01
ADAM_STEP = 10

LANES = 128
VMEM_LIMIT = 56 * 1024 * 1024
MESH = pl.DeviceIdType.MESH


def _params(*sem):
    return pltpu.CompilerParams(dimension_semantics=sem, vmem_limit_bytes=VMEM_LIMIT)


def _pick(n, prefs):
    for p in prefs:
        if n % p == 0:
            return p
    return n


def _matmul(pairs, out_dtype, name, *, trans_a=False, a_lbm=False, b_lbm=False, out_lbm=False, addend=None,
            tm=512, tn=None, tk=None, norm_gain=None, norm_bwd=None):
    a0, b0 = pairs[0]
    if trans_a:
        kdim = a0.shape[1] if a_lbm else a0.shape[0]
        m = a0.shape[0] * LANES if a_lbm else a0.shape[1]
    else:
        m = a0.shape[1] if a_lbm else a0.shape[0]
        kdim = a0.shape[0] * LANES if a_lbm else a0.shape[1]
    n = b0.shape[0] * LANES if b_lbm else b0.shape[1]
    tm = _pick(m, (tm, 1408, 256, 128))
    tn = tn or _pick(n, (1408, 1024, 768, 512, 256, 128))
    tk = tk or _pick(kdim, (1024, 1408, 512, 256, 128))
    nk = kdim // tk
    npair = len(pairs)

    def cat(ref):
        return jnp.concatenate([ref[p] for p in range(ref.shape[0])], axis=-1)

    def body(*refs):
        ins = refs[: 2 * npair]
        pos = 2 * npair
        add_ref = None
        if addend is not None:
            add_ref = refs[pos]
            pos += 1
        gain_ref = x_ref = dxin_ref = None
        if norm_gain is not None:
            gain_ref = refs[pos]
            pos += 1
        if norm_bwd is not None:
            x_ref, gain_ref, dxin_ref = refs[pos:pos + 3]
            pos += 3
        o_ref = refs[pos]
        extra_out = refs[pos + 1:-1]
        acc_ref = refs[-1]
        k = pl.program_id(2)

        @pl.when(k == 0)
        def _():
            acc_ref[...] = jnp.zeros_like(acc_ref)

        if norm_bwd is not None:
            @pl.when((k == 0) & (pl.program_id(0) == 0))
            def _():
                extra_out[1][...] = jnp.zeros_like(extra_out[1])

        acc = acc_ref[...]
        for p in range(npair):
            a = (cat(ins[2 * p]) if a_lbm else ins[2 * p][...]).astype(CD)
            b = (cat(ins[2 * p + 1]) if b_lbm else ins[2 * p + 1][...]).astype(CD)
            dims = (((0,), (0,)), ((), ())) if trans_a else (((1,), (0,)), ((), ()))
            acc = acc + lax.dot_general(a, b, dims, preferred_element_type=F32)
        acc_ref[...] = acc

        @pl.when(k == nk - 1)
        def _():
            res = acc_ref[...]
            if add_ref is not None:
                res = res + add_ref[...]
            if norm_gain is not None:
                rinv = lax.rsqrt(jnp.mean(res * res, axis=-1, keepdims=True) + RMS_EPS)
                extra_out[0][...] = (res * rinv * gain_ref[...]).astype(CD)
            if norm_bwd is not None:
                xv = x_ref[...]
                rinv = lax.rsqrt(jnp.mean(xv * xv, axis=-1, keepdims=True) + RMS_EPS)
                nrm = xv * rinv
                dn = res * gain_ref[...]
                extra_out[1][...] += jnp.sum(res * nrm, axis=0, keepdims=True)
                res = dxin_ref[...] + rinv * (dn - nrm * jnp.mean(dn * nrm, axis=-1, keepdims=True))
                extra_out[0][...] = res.astype(CD)
            res = res.astype(out_dtype)
            if out_lbm:
                for p in range(tn // LANES):
                    o_ref[p] = res[:, p * LANES:(p + 1) * LANES]
            else:
                o_ref[...] = res

    if trans_a:
        a_spec = (pl.BlockSpec((tm // LANES, tk, LANES), lambda i, j, k: (i, k, 0)) if a_lbm
                  else pl.BlockSpec((tk, tm), lambda i, j, k: (k, i)))
    else:
        a_spec = (pl.BlockSpec((tk // LANES, tm, LANES), lambda i, j, k: (k, i, 0)) if a_lbm
                  else pl.BlockSpec((tm, tk), lambda i, j, k: (i, k)))
    b_spec = (pl.BlockSpec((tn // LANES, tk, LANES), lambda i, j, k: (j, k, 0)) if b_lbm
              else pl.BlockSpec((tk, tn), lambda i, j, k: (k, j)))
    in_specs = []
    args = []
    for a, b in pairs:
        in_specs += [a_spec, b_spec]
        args += [a, b]
    if addend is not None:
        in_specs.append(pl.BlockSpec((tm, tn), lambda i, j, k: (i, j)))
        args.append(addend)
    tile = pl.BlockSpec((tm, tn), lambda i, j, k: (i, j))
    vec = pl.BlockSpec((1, tn), lambda i, j, k: (0, j))
    if out_lbm:
        out_shape = jax.ShapeDtypeStruct((n // LANES, m, LANES), out_dtype)
        out_spec = pl.BlockSpec((tn // LANES, tm, LANES), lambda i, j, k: (j, i, 0))
    else:
        out_shape = jax.ShapeDtypeStruct((m, n), out_dtype)
        out_spec = tile
    sem = ("parallel", "parallel", "arbitrary")
    if norm_gain is not None or norm_bwd is not None:
        assert tn == n and not out_lbm, "the norm needs whole rows in one tile"
        out_shape, out_spec = [out_shape, jax.ShapeDtypeStruct((m, n), CD)], [out_spec, tile]
    if norm_gain is not None:
        in_specs.append(vec)
        args.append(norm_gain.reshape(1, n))
    if norm_bwd is not None:
        x_in, gain, dx_in = norm_bwd
        in_specs += [tile, vec, tile]
        args += [x_in, gain.reshape(1, n), dx_in]
        out_shape.append(jax.ShapeDtypeStruct((1, n), F32))
        out_spec.append(vec)
        sem = ("arbitrary", "arbitrary", "arbitrary")
    return pl.pallas_call(
        body, name=name, out_shape=out_shape, grid=(m // tm, n // tn, nk),
        in_specs=in_specs, out_specs=out_spec,
        scratch_shapes=[pltpu.VMEM((tm, tn), F32)],
        compiler_params=_params(*sem),
    )(*args)


ROW_BLOCK = 256


def _rms_fwd(x, g, name):
    s, d = x.shape

    def body(x_ref, g_ref, h_ref):
        xv = x_ref[...]
        rinv = lax.rsqrt(jnp.mean(xv * xv, axis=-1, keepdims=True) + RMS_EPS)
        h_ref[...] = (xv * rinv * g_ref[...]).astype(CD)

    return pl.pallas_call(
        body, name=name, out_shape=jax.ShapeDtypeStruct((s, d), CD), grid=(s // ROW_BLOCK,),
        in_specs=[pl.BlockSpec((ROW_BLOCK, d), lambda i: (i, 0)), pl.BlockSpec((1, d), lambda i: (0, 0))],
        out_specs=pl.BlockSpec((ROW_BLOCK, d), lambda i: (i, 0)),
        compiler_params=_params("parallel"),
    )(x, g.reshape(1, d))


def _loss_head(x, g, target, name):
    s, d = x.shape

    def body(x_ref, g_ref, t_ref, loss_ref, dx_ref, dxc_ref, dg_ref):
        @pl.when(pl.program_id(0) == 0)
        def _():
            dg_ref[...] = jnp.zeros_like(dg_ref)
            loss_ref[...] = jnp.zeros_like(loss_ref)

        xv = x_ref[...]
        gv = g_ref[...]
        rinv = lax.rsqrt(jnp.mean(xv * xv, axis=-1, keepdims=True) + RMS_EPS)
        nrm = xv * rinv
        err = nrm * gv - t_ref[...]
        loss_ref[...] += 0.5 * jnp.sum(jnp.mean(err * err, axis=-1, keepdims=True), axis=0, keepdims=True)
        dy = err * (1.0 / d)
        dn = dy * gv
        dx = rinv * (dn - nrm * jnp.mean(dn * nrm, axis=-1, keepdims=True))
        dx_ref[...] = dx
        dxc_ref[...] = dx.astype(CD)
        dg_ref[...] += jnp.sum(dy * nrm, axis=0, keepdims=True)

    row = pl.BlockSpec((ROW_BLOCK, d), lambda i: (i, 0))
    vec = pl.BlockSpec((1, d), lambda i: (0, 0))
    return pl.pallas_call(
        body, name=name,
        out_shape=(jax.ShapeDtypeStruct((1, LANES), F32), jax.ShapeDtypeStruct((s, d), F32),
                   jax.ShapeDtypeStruct((s, d), CD), jax.ShapeDtypeStruct((1, d), F32)),
        grid=(s // ROW_BLOCK,), in_specs=[row, vec, row],
        out_specs=(pl.BlockSpec((1, LANES), lambda i: (0, 0)), row, row, vec),
        compiler_params=_params("arbitrary"),
    )(x, g.reshape(1, d), target)


def _sigmoid(z):
    return 1.0 / (1.0 + jnp.exp(-z))


FFN_TM = 512
FFN_TN = 1408


def _ffn_up(h, wg, wu, name):
    s, d = h.shape
    f = wg.shape[1]
    tm = _pick(s, (FFN_TM, 256))

    def body(h_ref, wg_ref, wu_ref, g_ref, u_ref, a_ref):
        hv = h_ref[...]
        gv = jnp.dot(hv, wg_ref[...], preferred_element_type=F32)
        uv = jnp.dot(hv, wu_ref[...], preferred_element_type=F32)
        g_ref[...] = gv
        u_ref[...] = uv
        a_ref[...] = (gv * _sigmoid(gv) * uv).astype(CD)

    a_spec = pl.BlockSpec((tm, d), lambda i, j: (i, 0))
    w_spec = pl.BlockSpec((d, FFN_TN), lambda i, j: (0, j))
    o_spec = pl.BlockSpec((tm, FFN_TN), lambda i, j: (i, j))
    return pl.pallas_call(
        body, name=name,
        out_shape=(jax.ShapeDtypeStruct((s, f), F32), jax.ShapeDtypeStruct((s, f), F32),
                   jax.ShapeDtypeStruct((s, f), CD)),
        grid=(s // tm, f // FFN_TN), in_specs=[a_spec, w_spec, w_spec], out_specs=(o_spec, o_spec, o_spec),
        compiler_params=_params("parallel", "parallel"),
    )(h, wg, wu)


def _ffn_dact(dxc, wd_t, g, u, name):
    s, d = dxc.shape
    f = wd_t.shape[1]
    tm = _pick(s, (FFN_TM, 256))

    def body(dx_ref, w_ref, g_ref, u_ref, dg_ref, du_ref):
        da = jnp.dot(dx_ref[...], w_ref[...], preferred_element_type=F32)
        gv = g_ref[...]
        sg = _sigmoid(gv)
        silu = gv * sg
        dg_ref[...] = (da * u_ref[...] * (sg + silu * (1.0 - sg))).astype(CD)
        du_ref[...] = (da * silu).astype(CD)

    a_spec = pl.BlockSpec((tm, d), lambda i, j: (i, 0))
    w_spec = pl.BlockSpec((d, FFN_TN), lambda i, j: (0, j))
    o_spec = pl.BlockSpec((tm, FFN_TN), lambda i, j: (i, j))
    return pl.pallas_call(
        body, name=name,
        out_shape=(jax.ShapeDtypeStruct((s, f), CD), jax.ShapeDtypeStruct((s, f), CD)),
        grid=(s // tm, f // FFN_TN), in_specs=[a_spec, w_spec, o_spec, o_spec], out_specs=(o_spec, o_spec),
        compiler_params=_params("parallel", "parallel"),
    )(dxc, wd_t, g, u)


TIME_BLOCK = 256
SUBLANES = 8
GELU_C = math.sqrt(2.0 / math.pi)
GELU_A = 0.044715


def _gelu(x):
    return 0.5 * x * (1.0 + jnp.tanh(GELU_C * (x + GELU_A * x * x * x)))


def _gelu_grad(x):
    t = jnp.tanh(GELU_C * (x + GELU_A * x * x * x))
    return 0.5 * (1.0 + t) + 0.5 * x * (1.0 - t * t) * GELU_C * (1.0 + 3.0 * GELU_A * x * x)


def _neg_expm1(x):
    series = -x * (1.0 + x * (0.5 + x * (1.0 / 6.0 + x * (1.0 / 24.0))))
    return jnp.where(x > -0.05, series, 1.0 - jnp.exp(x))


def _log_sigmoid(x):
    return jnp.minimum(x, 0.0) - jnp.log1p(jnp.exp(-jnp.abs(x)))


def _shift_down(x, tail, s):
    if s == 0:
        return x
    ext = jnp.concatenate([tail, x], axis=0)
    return pltpu.roll(ext, s, axis=0)[SUBLANES:]


def _shift_up(x, head, s):
    if s == 0:
        return x
    n = x.shape[0]
    ext = jnp.concatenate([x, head], axis=0)
    return pltpu.roll(ext, n + SUBLANES - s, axis=0)[:n]


def _rg_gates(xbr, tail, cw_ref, cb, wr, wi, br, bi, ls):
    taps = [_shift_down(xbr, tail, CONV_W - 1 - k) for k in range(CONV_W)]
    xc = cb
    for k in range(CONV_W):
        xc = xc + cw_ref[pl.ds(k, 1), :] * taps[k]
    xcd = xc.astype(CD)
    r = _sigmoid(jnp.dot(xcd, wr, preferred_element_type=F32) + br)
    i = _sigmoid(jnp.dot(xcd, wi, preferred_element_type=F32) + bi)
    log_a = RG_C * r * ls
    a = jnp.exp(log_a)
    mult = jnp.sqrt(jnp.maximum(_neg_expm1(2.0 * log_a), 0.0))
    return taps, xc, r, i, log_a, a, mult


def _scan8_fwd(a, u):
    row = lax.broadcasted_iota(jnp.int32, a.shape, 0)
    for d in (1, 2, 4):
        a_s = pltpu.roll(a, d, axis=0)
        u_s = pltpu.roll(u, d, axis=0)
        m = row >= d
        u = jnp.where(m, a * u_s + u, u)
        a = jnp.where(m, a * a_s, a)
    return a, u


def _scan8_bwd(b, u):
    row = lax.broadcasted_iota(jnp.int32, b.shape, 0)
    for d in (1, 2, 4):
        b_s = pltpu.roll(b, SUBLANES - d, axis=0)
        u_s = pltpu.roll(u, SUBLANES - d, axis=0)
        m = row < SUBLANES - d
        u = jnp.where(m, b * u_s + u, u)
        b = jnp.where(m, b * b_s, b)
    return b, u


def _rglru_fwd(gate_br, x_br, cw, cb, wr, wi, br, bi, lam, name):
    s, c = x_br.shape
    nt = s // TIME_BLOCK
    tb, cbw = TIME_BLOCK, RG_BW
    groups = tb // SUBLANES

    def body(g_ref, x_ref, tail_ref, cw_ref, cb_ref, wr_ref, wi_ref, br_ref, bi_ref, lam_ref,
             y_ref, hs_ref, carry_ref, a_scr, u_scr):
        t = pl.program_id(1)

        @pl.when(t == 0)
        def _():
            carry_ref[...] = jnp.zeros_like(carry_ref)

        tail = jnp.where(t > 0, tail_ref[...], 0.0)
        ls = _log_sigmoid(lam_ref[...])
        _, xc, _, i, _, a, mult = _rg_gates(x_ref[...], tail, cw_ref, cb_ref[...], wr_ref[0], wi_ref[0],
                                            br_ref[...], bi_ref[...], ls)
        a_scr[...] = a
        u_scr[...] = mult * (i * xc)
        carry = carry_ref[...]
        for gi in range(groups):
            rows = pl.ds(gi * SUBLANES, SUBLANES)
            pa, hl = _scan8_fwd(a_scr[rows, :], u_scr[rows, :])
            hs_ref[rows, :] = hl + pa * carry
            carry = hs_ref[pl.ds(gi * SUBLANES + SUBLANES - 1, 1), :]
        carry_ref[...] = carry
        y_ref[...] = (hs_ref[...] * _gelu(g_ref[...])).astype(CD)

    blk = pl.BlockSpec((tb, cbw), lambda n, t: (t, n))
    tail = pl.BlockSpec((SUBLANES, cbw), lambda n, t: (jnp.maximum(t * groups - 1, 0), n))
    vec = pl.BlockSpec((1, cbw), lambda n, t: (0, n))
    wblk = pl.BlockSpec((1, cbw, cbw), lambda n, t: (n, 0, 0))
    return pl.pallas_call(
        body, name=name,
        out_shape=(jax.ShapeDtypeStruct((s, c), CD), jax.ShapeDtypeStruct((s, c), F32)),
        grid=(RG_BLOCKS, nt),
        in_specs=[blk, blk, tail, pl.BlockSpec((CONV_W, cbw), lambda n, t: (0, n)), vec, wblk, wblk, vec, vec, vec],
        out_specs=(blk, blk),
        scratch_shapes=[pltpu.VMEM((1, cbw), F32), pltpu.VMEM((tb, cbw), F32), pltpu.VMEM((tb, cbw), F32)],
        compiler_params=_params("parallel", "arbitrary"),
    )(gate_br, x_br, x_br, cw, cb, wr, wi, br, bi, lam)


def _rglru_bwd(dy, gate_br, x_br, hs, cw, cb, wr, wi, wrt, wit, br, bi, lam, name):
    s, c = x_br.shape
    nt = s // TIME_BLOCK
    tb, cbw = TIME_BLOCK, RG_BW
    groups = tb // SUBLANES

    def body(dy_ref, g_ref, x_ref, tail_ref, hs_ref, hprev_ref, cw_ref, cb_ref, wr_ref, wi_ref, wrt_ref, wit_ref,
             br_ref, bi_ref, lam_ref,
             dg_ref, dx_ref, dcw_ref, dcb_ref, dbr_ref, dbi_ref, dlam_ref, dwr_ref, dwi_ref,
             carry_ref, head_ref, b_scr, u_scr, dh_scr):
        tr = pl.program_id(1)
        first_block = tr == nt - 1

        @pl.when(tr == 0)
        def _():
            carry_ref[...] = jnp.zeros_like(carry_ref)
            head_ref[...] = jnp.zeros_like(head_ref)
            for ref in (dcw_ref, dcb_ref, dbr_ref, dbi_ref, dlam_ref, dwr_ref, dwi_ref):
                ref[...] = jnp.zeros_like(ref)

        tail = jnp.where(first_block, 0.0, tail_ref[...])
        lam_v = lam_ref[...]
        ls = _log_sigmoid(lam_v)
        taps, xc, r, i, log_a, a, mult = _rg_gates(x_ref[...], tail, cw_ref, cb_ref[...], wr_ref[0], wi_ref[0],
                                                   br_ref[...], bi_ref[...], ls)
        gate_v = g_ref[...]
        dyv = dy_ref[...]
        hsv = hs_ref[...]
        dg_ref[...] = (dyv * hsv * _gelu_grad(gate_v)).astype(CD)

        row = lax.broadcasted_iota(jnp.int32, a.shape, 0)
        b_scr[...] = jnp.where(row == tb - 1, 1.0, pltpu.roll(a, tb - 1, axis=0))
        u_scr[...] = dyv * _gelu(gate_v)
        carry = carry_ref[...]
        for gi in reversed(range(groups)):
            rows = pl.ds(gi * SUBLANES, SUBLANES)
            pb, gl = _scan8_bwd(b_scr[rows, :], u_scr[rows, :])
            dh_scr[rows, :] = gl + pb * carry
            carry = dh_scr[pl.ds(gi * SUBLANES, 1), :]
        dh = dh_scr[...]
        carry_ref[...] = carry * jnp.sum(jnp.where(row == 0, a, 0.0), axis=0, keepdims=True)

        hprev_tail = jnp.where(first_block, 0.0, hprev_ref[...])
        h_prev = _shift_down(hsv, hprev_tail, 1)
        da = dh * h_prev
        ixc = i * xc
        dmult = dh * ixc
        di = dh * mult * xc
        dxc = dh * mult * i
        a2 = a * a
        dlog_a = da * a - dmult * a2 / mult
        dpre_r = (dlog_a * (RG_C * ls)) * r * (1.0 - r)
        dpre_i = di * i * (1.0 - i)
        dlam_ref[...] += jnp.sum(dlog_a * r, axis=0, keepdims=True) * (RG_C * _sigmoid(-lam_v))
        dbr_ref[...] += jnp.sum(dpre_r, axis=0, keepdims=True)
        dbi_ref[...] += jnp.sum(dpre_i, axis=0, keepdims=True)
        xcd = xc.astype(CD)
        dprc = dpre_r.astype(CD)
        dpic = dpre_i.astype(CD)
        tn_dims = (((0,), (0,)), ((), ()))
        dwr_ref[0] += lax.dot_general(xcd, dprc, tn_dims, preferred_element_type=F32)
        dwi_ref[0] += lax.dot_general(xcd, dpic, tn_dims, preferred_element_type=F32)
        dxc = dxc + jnp.dot(dprc, wrt_ref[0], preferred_element_type=F32) + jnp.dot(dpic, wit_ref[0],
                                                                                    preferred_element_type=F32)
        dcb_ref[...] += jnp.sum(dxc, axis=0, keepdims=True)
        for k in range(CONV_W):
            dcw_ref[pl.ds(k, 1), :] += jnp.sum(dxc * taps[k], axis=0, keepdims=True)
        head = head_ref[...]
        dxb = jnp.zeros_like(dxc)
        for sft in range(CONV_W):
            dxb = dxb + cw_ref[pl.ds(CONV_W - 1 - sft, 1), :] * _shift_up(dxc, head, sft)
        dx_ref[...] = dxb.astype(CD)
        head_ref[...] = dxc[0:SUBLANES, :]

    blk = pl.BlockSpec((tb, cbw), lambda n, t: (nt - 1 - t, n))
    tail = pl.BlockSpec((SUBLANES, cbw), lambda n, t: (jnp.maximum((nt - 1 - t) * groups - 1, 0), n))
    vec = pl.BlockSpec((1, cbw), lambda n, t: (0, n))
    cwb = pl.BlockSpec((CONV_W, cbw), lambda n, t: (0, n))
    wblk = pl.BlockSpec((1, cbw, cbw), lambda n, t: (n, 0, 0))
    vshape = jax.ShapeDtypeStruct((1, c), F32)
    wshape = jax.ShapeDtypeStruct((RG_BLOCKS, cbw, cbw), F32)
    return pl.pallas_call(
        body, name=name,
        out_shape=(jax.ShapeDtypeStruct((s, c), CD), jax.ShapeDtypeStruct((s, c), CD),
                   jax.ShapeDtypeStruct((CONV_W, c), F32), vshape, vshape, vshape, vshape, wshape, wshape),
        grid=(RG_BLOCKS, nt),
        in_specs=[blk, blk, blk, tail, blk, tail, cwb, vec, wblk, wblk, wblk, wblk, vec, vec, vec],
        out_specs=(blk, blk, cwb, vec, vec, vec, vec, wblk, wblk),
        scratch_shapes=[pltpu.VMEM((1, cbw), F32), pltpu.VMEM((SUBLANES, cbw), F32),
                        pltpu.VMEM((tb, cbw), F32), pltpu.VMEM((tb, cbw), F32), pltpu.VMEM((tb, cbw), F32)],
        compiler_params=_params("parallel", "arbitrary"),
    )(dy, gate_br, x_br, x_br, hs, hs, cw, cb, wr, wi, wrt, wit, br, bi, lam)


ATT_BLOCK = 256
ATT_Q_BLOCK = 1024
ATT_RATIO = ATT_Q_BLOCK // ATT_BLOCK
ATT_SCALE = 1.0 / math.sqrt(SB_HEAD_DIM)
N_PAIRS = SB_HEADS * SB_HEAD_DIM // LANES
NT_DIMS = (((1,), (1,)), ((), ()))
TN_DIMS = (((0,), (0,)), ((), ()))


LOG2E = 1.4426950408889634


def _neg_abs(x):
    bits = lax.bitcast_convert_type(x, jnp.uint32) | jnp.uint32(0x80000000)
    return lax.bitcast_convert_type(bits, F32)


def _qk(qx, kb):
    return lax.dot_general(qx, kb, NT_DIMS, preferred_element_type=F32)


def _sb_logits(qk, valid):
    z2 = qk * (ATT_SCALE * LOG2E)
    lb2 = jnp.minimum(z2, 0.0) - jnp.log2(1.0 + jnp.exp2(_neg_abs(z2)))
    l2 = lb2 - z2
    if valid is not None:
        l2 = jnp.where(valid, l2, 0.0)
    return lb2, l2


def _hi_lo(x):
    hi = x.astype(CD)
    lo = (x - hi.astype(F32)).astype(CD)
    return jnp.concatenate([hi, lo], axis=1)


def _tri(strict, stacked):
    r = lax.broadcasted_iota(jnp.int32, (ATT_BLOCK, ATT_BLOCK), 0)
    c = lax.broadcasted_iota(jnp.int32, (ATT_BLOCK, ATT_BLOCK), 1)
    m = (r > c if strict else r >= c).astype(CD)
    return jnp.concatenate([m, m], axis=0) if stacked else m


def _attn_fwd(qkv, name):
    _, s, _ = qkv.shape
    tq, t = ATT_Q_BLOCK, ATT_BLOCK
    nblk = s // tq

    def body(q_ref, k_ref, v_ref, o_ref, qk_scr, w_scr):
        i = pl.program_id(1)
        lane = lax.broadcasted_iota(jnp.int32, (1, LANES), 1)
        head_masks = (lane < SB_HEAD_DIM, lane >= SB_HEAD_DIM)
        q = q_ref[0]
        qs = [jnp.where(m, q, jnp.zeros_like(q)) for m in head_masks]
        tri = _tri(True, False)
        rr = lax.broadcasted_iota(jnp.int32, (tq, t), 0)
        cc = lax.broadcasted_iota(jnp.int32, (tq, t), 1)

        def rows_of(j):
            return pl.ds(pl.multiple_of(j * t, t), t)

        def tail(x, row0):
            return x if row0 == 0 else x[row0:]

        def start_logits(j, row0=0):
            kb = k_ref[0, rows_of(j), :]
            for hd in range(2):
                qk_scr[hd, row0:, :] = _qk(tail(qs[hd], row0), kb)

        def weights(run, diagonal=False, row0=0):
            new_run = []
            valid = (cc < rr)[:tq - row0] if diagonal else None
            for hd in range(2):
                lb2, l2 = _sb_logits(qk_scr[hd, row0:, :], valid)
                w = jnp.exp2(lb2 + (tail(run[hd], row0) + jnp.dot(l2.astype(CD), tri, preferred_element_type=F32)))
                if valid is not None:
                    w = jnp.where(valid, w, 0.0)
                w_scr[row0:, hd * t:(hd + 1) * t] = w.astype(CD)
                rowsum = jnp.sum(l2, axis=1, keepdims=True)
                if row0:
                    rowsum = jnp.concatenate([jnp.zeros((row0, 1), F32), rowsum], axis=0)
                new_run.append(run[hd] + rowsum)
            return tuple(new_run)

        def apply_weights(j, row0=0):
            vb = v_ref[0, rows_of(j), :]
            vcat = jnp.concatenate([jnp.where(m, vb, jnp.zeros_like(vb)) for m in head_masks], axis=0)
            inc = jnp.dot(w_scr[row0:, :], vcat, preferred_element_type=F32)
            return inc if row0 == 0 else jnp.concatenate([jnp.zeros((row0, LANES), F32), inc], axis=0)

        zero = jnp.zeros((tq, 1), F32)
        last = ATT_RATIO - 1
        start_logits(ATT_RATIO * i + last, last * t)
        run = weights((zero, zero), True, last * t)
        oacc = jnp.zeros((tq, LANES), F32)
        for d in reversed(range(last)):
            start_logits(ATT_RATIO * i + d, d * t)
            oacc = oacc + apply_weights(ATT_RATIO * i + d + 1, (d + 1) * t)
            run = weights(run, True, d * t)
        start_logits(jnp.maximum(ATT_RATIO * i - 1, 0))

        def step(jj, carry):
            run, oacc = carry
            b = ATT_RATIO * i - 1 - jj
            oacc = oacc + apply_weights(b + 1)
            run = weights(run)
            start_logits(jnp.maximum(b - 1, 0))
            return run, oacc

        run, oacc = lax.fori_loop(0, ATT_RATIO * i, step, (run, oacc))
        o_ref[0] = oacc + apply_weights(0)

    return pl.pallas_call(
        body, name=name, out_shape=jax.ShapeDtypeStruct((N_PAIRS, s, LANES), F32), grid=(N_PAIRS, nblk),
        in_specs=[pl.BlockSpec((1, tq, LANES), lambda p, i: (p, i, 0)),
                  pl.BlockSpec((1, s, LANES), lambda p, i: (N_PAIRS + p, 0, 0)),
                  pl.BlockSpec((1, s, LANES), lambda p, i: (2 * N_PAIRS + p, 0, 0))],
        out_specs=pl.BlockSpec((1, tq, LANES), lambda p, i: (p, i, 0)),
        scratch_shapes=[pltpu.VMEM((2, tq, t), F32), pltpu.VMEM((tq, 2 * t), CD)],
        compiler_params=_params("parallel", "arbitrary"),
    )(qkv, qkv, qkv)


def _attn_bwd(qkv, o, do, name):
    _, s, _ = qkv.shape
    tq, t = ATT_Q_BLOCK, ATT_BLOCK
    nblk = s // tq

    def body(q_ref, k_ref, v_ref, o_ref, do_ref, dq_ref, dk_ref, dv_ref, qk_scr, dw_scr, w_scr, dz_scr):
        i = pl.program_id(1)

        @pl.when(i == 0)
        def _():
            dk_ref[...] = jnp.zeros_like(dk_ref)
            dv_ref[...] = jnp.zeros_like(dv_ref)

        lane = lax.broadcasted_iota(jnp.int32, (1, LANES), 1)
        head_masks = (lane < SB_HEAD_DIM, lane >= SB_HEAD_DIM)
        q = q_ref[0]
        dov = do_ref[0]
        ov = o_ref[0]
        qs = [jnp.where(m, q, jnp.zeros_like(q)) for m in head_masks]
        q_scaled_t = jnp.concatenate([(qx.astype(F32) * ATT_SCALE).T for qx in qs], axis=1).astype(CD)
        docs = [jnp.where(m, dov, 0.0).astype(CD) for m in head_masks]
        docat_t = jnp.concatenate([jnp.where(m, dov, 0.0).T for m in head_masks], axis=1).astype(CD)
        totals = [jnp.sum(d.astype(F32) * ov, axis=1, keepdims=True) for d in docs]
        tri = _tri(True, False)
        tri_incl = _tri(False, True)
        rr = lax.broadcasted_iota(jnp.int32, (tq, t), 0)
        cc = lax.broadcasted_iota(jnp.int32, (tq, t), 1)

        def rows_of(j):
            return pl.ds(pl.multiple_of(j * t, t), t)

        def tail(x, row0):
            return x if row0 == 0 else x[row0:]

        def pad_rows(x, row0):
            return x if row0 == 0 else jnp.concatenate([jnp.zeros((row0, x.shape[1]), x.dtype), x], axis=0)

        def start_products(j, row0=0):
            kb = k_ref[0, rows_of(j), :]
            vb = v_ref[0, rows_of(j), :]
            for hd in range(2):
                qk_scr[hd, row0:, :] = _qk(tail(qs[hd], row0), kb)
                dw_scr[hd, row0:, :] = lax.dot_general(tail(docs[hd], row0), vb, NT_DIMS, preferred_element_type=F32)

        def logit_grads(run, erun, diagonal=False, row0=0):
            new_run, new_erun = [], []
            valid = (cc < rr)[:tq - row0] if diagonal else None
            for hd in range(2):
                lb2, l2 = _sb_logits(qk_scr[hd, row0:, :], valid)
                w = jnp.exp2(lb2 + (tail(run[hd], row0) + jnp.dot(l2.astype(CD), tri, preferred_element_type=F32)))
                if valid is not None:
                    w = jnp.where(valid, w, 0.0)
                wc = w.astype(CD)
                w_scr[hd * tq + row0:(hd + 1) * tq, :] = wc
                e = dw_scr[hd, row0:, :] * wc.astype(F32)
                prefix = (tail(totals[hd] - erun[hd], row0)
                          - jnp.dot(_hi_lo(e), tri_incl, preferred_element_type=F32))
                dz = e - jnp.exp2(lb2) * (e + prefix)
                if valid is not None:
                    dz = jnp.where(valid, dz, 0.0)
                dz_scr[hd * tq + row0:(hd + 1) * tq, :] = dz.astype(CD)
                new_run.append(run[hd] + pad_rows(jnp.sum(l2, axis=1, keepdims=True), row0))
                new_erun.append(erun[hd] + pad_rows(jnp.sum(e, axis=1, keepdims=True), row0))
            return tuple(new_run), tuple(new_erun)

        def apply_grads(j, row0=0):
            rows = rows_of(j)
            kb = k_ref[0, rows, :]
            kcat = jnp.concatenate([jnp.where(m, kb, jnp.zeros_like(kb)) for m in head_masks], axis=0)
            dz_heads = [dz_scr[hd * tq + row0:(hd + 1) * tq, :] for hd in range(2)]
            w_heads = [w_scr[hd * tq + row0:(hd + 1) * tq, :] for hd in range(2)]
            q_t = jnp.concatenate([q_scaled_t[:, hd * tq + row0:(hd + 1) * tq] for hd in range(2)], axis=1)
            do_t = jnp.concatenate([docat_t[:, hd * tq + row0:(hd + 1) * tq] for hd in range(2)], axis=1)
            dk_ref[0, :, rows] += jnp.dot(q_t, jnp.concatenate(dz_heads, axis=0), preferred_element_type=F32)
            dv_ref[0, :, rows] += jnp.dot(do_t, jnp.concatenate(w_heads, axis=0), preferred_element_type=F32)
            return pad_rows(jnp.dot(jnp.concatenate(dz_heads, axis=1), kcat, preferred_element_type=F32), row0)

        zero = jnp.zeros((tq, 1), F32)
        last = ATT_RATIO - 1
        start_products(ATT_RATIO * i + last, last * t)
        run, erun = logit_grads((zero, zero), (zero, zero), True, last * t)
        dqacc = jnp.zeros((tq, LANES), F32)
        for d in reversed(range(last)):
            start_products(ATT_RATIO * i + d, d * t)
            dqacc = dqacc + apply_grads(ATT_RATIO * i + d + 1, (d + 1) * t)
            run, erun = logit_grads(run, erun, True, d * t)
        start_products(jnp.maximum(ATT_RATIO * i - 1, 0))

        def step(jj, carry):
            run, erun, dqacc = carry
            b = ATT_RATIO * i - 1 - jj
            dqacc = dqacc + apply_grads(b + 1)
            run, erun = logit_grads(run, erun)
            start_products(jnp.maximum(b - 1, 0))
            return run, erun, dqacc

        run, erun, dqacc = lax.fori_loop(0, ATT_RATIO * i, step, (run, erun, dqacc))
        dq_ref[0] = (dqacc + apply_grads(0)) * ATT_SCALE

    qblk = pl.BlockSpec((1, tq, LANES), lambda p, i: (p, i, 0))
    full = pl.BlockSpec((1, LANES, s), lambda p, i: (p, 0, 0))
    shape = jax.ShapeDtypeStruct((N_PAIRS, s, LANES), F32)
    shape_t = jax.ShapeDtypeStruct((N_PAIRS, LANES, s), F32)
    dq, dk_t, dv_t = pl.pallas_call(
        body, name=name, out_shape=(shape, shape_t, shape_t), grid=(N_PAIRS, nblk),
        in_specs=[qblk,
                  pl.BlockSpec((1, s, LANES), lambda p, i: (N_PAIRS + p, 0, 0)),
                  pl.BlockSpec((1, s, LANES), lambda p, i: (2 * N_PAIRS + p, 0, 0)),
                  qblk, qblk],
        out_specs=(qblk, full, full),
        scratch_shapes=[pltpu.VMEM((2, tq, t), F32), pltpu.VMEM((2, tq, t), F32),
                        pltpu.VMEM((2 * tq, t), CD), pltpu.VMEM((2 * tq, t), CD)],
        compiler_params=_params("parallel", "arbitrary"),
    )(qkv, qkv, qkv, o, do)
    return dq, jnp.swapaxes(dk_t, 1, 2), jnp.swapaxes(dv_t, 1, 2)


ADAM_COLS = 1024


def _adamw(w, g, m, v, name):
    shape = w.shape
    rows, cols = (shape[-2], shape[-1]) if len(shape) >= 2 else (1, shape[-1])
    lead = w.size // (rows * cols)
    tr = _pick(rows, (512, 256, 128, 64, 32, 16, 8))

    def body(w_ref, g_ref, m_ref, v_ref, d_ref, nm_ref, nv_ref):
        gv = g_ref[...]
        nm = ADAM_B1 * m_ref[...] + (1.0 - ADAM_B1) * gv
        nv = ADAM_B2 * v_ref[...] + (1.0 - ADAM_B2) * (gv * gv)
        m_hat = nm / (1.0 - ADAM_B1 ** ADAM_STEP)
        v_hat = nv / (1.0 - ADAM_B2 ** ADAM_STEP)
        d_ref[...] = -ADAM_LR * (m_hat / (jnp.sqrt(v_hat) + ADAM_EPS) + ADAM_WD * w_ref[...])
        nm_ref[...] = nm
        nv_ref[...] = nv

    blk = pl.BlockSpec((1, tr, cols), lambda l, i: (l, i, 0))
    out = jax.ShapeDtypeStruct((lead, rows, cols), F32)
    d, nm, nv = pl.pallas_call(
        body, name=name, out_shape=(out, out, out), grid=(lead, rows // tr),
        in_specs=[blk, blk, blk, blk], out_specs=(blk, blk, blk), compiler_params=_params("parallel", "parallel"),
    )(*[a.reshape(lead, rows, cols) for a in (w, g, m, v)])
    return d.reshape(shape), nm.reshape(shape), nv.reshape(shape)


HBM = pl.BlockSpec(memory_space=pltpu.HBM)


def _coords():
    return lax.axis_index("x"), lax.axis_index("y"), lax.axis_index("c")


def _other_chips(x, y):
    return [(1 - x, y), (x, 1 - y), (1 - x, 1 - y)]


def _allgather_chips(shard, name):
    r, cols = shard.shape
    half = r // 2

    def body(src_ref, out_ref, send_sems, recv_sems):
        x, y, c = _coords()
        sibling = (x, y, 1 - c)
        chips = _other_chips(x, y)

        def rows(px, py, h):
            return out_ref.at[2 * px + py, pl.ds(h * half, half), :]

        def copy(k, block, to, src=None):
            return pltpu.make_async_remote_copy(
                src_ref=rows(*block) if src is None else src, dst_ref=rows(*block),
                send_sem=send_sems.at[k], recv_sem=recv_sems.at[k], device_id=to, device_id_type=MESH)

        my_half = src_ref.at[pl.ds(c * half, half), :]
        first = [copy(j, (x, y, c), (*chip, c), src=my_half) for j, chip in enumerate(chips)]
        for cp in first:
            cp.start()
        passed = [copy(3 + j, (*chip, c), sibling) for j, chip in enumerate(chips)]
        for j, chip in enumerate(chips):
            copy(j, (*chip, c), (x, y, c)).wait_recv()
            passed[j].start()
        for j, chip in enumerate(chips):
            copy(3 + j, (*chip, 1 - c), (x, y, c)).wait_recv()
        for cp in first + passed:
            cp.wait_send()

    return pl.pallas_call(
        body, name=name, out_shape=jax.ShapeDtypeStruct((N_CHIPS, r, cols), shard.dtype),
        in_specs=[HBM], out_specs=HBM,
        scratch_shapes=[pltpu.SemaphoreType.DMA((6,)), pltpu.SemaphoreType.DMA((6,))],
    )(shard)


def _exchange_sibling_halves(g, name):
    n, r, cols = g.shape
    half = r // 2

    def body(g_ref, out_ref, send_sem, recv_sem):
        x, y, c = _coords()
        cp = pltpu.make_async_remote_copy(
            src_ref=g_ref.at[:, pl.ds((1 - c) * half, half), :], dst_ref=out_ref,
            send_sem=send_sem, recv_sem=recv_sem, device_id=(x, y, 1 - c), device_id_type=MESH)
        cp.start()
        cp.wait()

    return pl.pallas_call(
        body, name=name, out_shape=jax.ShapeDtypeStruct((n, half, cols), g.dtype),
        in_specs=[HBM], out_specs=HBM,
        scratch_shapes=[pltpu.SemaphoreType.DMA, pltpu.SemaphoreType.DMA],
    )(g)


def _scatter_to_chips(p, name):
    n, h, cols = p.shape

    def body(p_ref, out_ref, send_sems, recv_sems, local_sem):
        x, y, c = _coords()
        me = 2 * x + y
        mine = pltpu.make_async_copy(p_ref.at[me], out_ref.at[me], local_sem)
        mine.start()
        sends = []
        for j, (px, py) in enumerate(_other_chips(x, y)):
            sends.append(pltpu.make_async_remote_copy(
                src_ref=p_ref.at[2 * px + py], dst_ref=out_ref.at[me],
                send_sem=send_sems.at[j], recv_sem=recv_sems.at[j], device_id=(px, py, c), device_id_type=MESH))
        for cp in sends:
            cp.start()
        for j, (px, py) in enumerate(_other_chips(x, y)):
            pltpu.make_async_remote_copy(
                src_ref=p_ref.at[me], dst_ref=out_ref.at[2 * px + py],
                send_sem=send_sems.at[j], recv_sem=recv_sems.at[j], device_id=(px, py, c),
                device_id_type=MESH).wait_recv()
        for cp in sends:
            cp.wait_send()
        mine.wait()

    return pl.pallas_call(
        body, name=name, out_shape=jax.ShapeDtypeStruct((n, h, cols), p.dtype),
        in_specs=[HBM], out_specs=HBM,
        scratch_shapes=[pltpu.SemaphoreType.DMA((3,)), pltpu.SemaphoreType.DMA((3,)), pltpu.SemaphoreType.DMA],
    )(p)


def _share_halves(v, name):
    h = v.shape[0] // 2

    def body(v_ref, out_ref, send_sem, recv_sem):
        x, y, c = _coords()
        cp = pltpu.make_async_remote_copy(
            src_ref=v_ref.at[pl.ds(c * h, h), :], dst_ref=out_ref.at[pl.ds(c * h, h), :],
            send_sem=send_sem, recv_sem=recv_sem, device_id=(x, y, 1 - c), device_id_type=MESH)
        cp.start()
        pltpu.make_async_remote_copy(
            src_ref=v_ref.at[pl.ds(c * h, h), :], dst_ref=out_ref.at[pl.ds((1 - c) * h, h), :],
            send_sem=send_sem, recv_sem=recv_sem, device_id=(x, y, 1 - c), device_id_type=MESH).wait_recv()
        cp.wait_send()

    return pl.pallas_call(
        body, name=name, out_shape=jax.ShapeDtypeStruct(v.shape, v.dtype),
        in_specs=[HBM], out_specs=HBM, input_output_aliases={0: 0},
        scratch_shapes=[pltpu.SemaphoreType.DMA, pltpu.SemaphoreType.DMA],
    )(v)


def _allreduce_small(v, name):
    r, cols = v.shape

    def body(v_ref, out_ref, buf_ref, send_sems, recv_sems):
        x, y, c = _coords()
        me = 4 * x + 2 * y + c
        buf_ref[me] = v_ref[...]
        sends = []
        for k in range(1, N_DEV):
            px = 1 - x if k & 4 else x
            py = 1 - y if k & 2 else y
            pc = 1 - c if k & 1 else c
            sends.append(pltpu.make_async_remote_copy(
                src_ref=v_ref, dst_ref=buf_ref.at[me], send_sem=send_sems.at[k - 1], recv_sem=recv_sems.at[k - 1],
                device_id=(px, py, pc), device_id_type=MESH))
        for cp in sends:
            cp.start()
        for cp in sends:
            cp.wait()
        acc = buf_ref[0]
        for d in range(1, N_DEV):
            acc = acc + buf_ref[d]
        out_ref[...] = acc

    return pl.pallas_call(
        body, name=name, out_shape=jax.ShapeDtypeStruct((r, cols), F32),
        in_specs=[pl.BlockSpec(memory_space=pltpu.VMEM)], out_specs=pl.BlockSpec(memory_space=pltpu.VMEM),
        scratch_shapes=[pltpu.VMEM((N_DEV, r, cols), F32), pltpu.SemaphoreType.DMA((N_DEV - 1,)),
                        pltpu.SemaphoreType.DMA((N_DEV - 1,))],
    )(v)


def _add_sibling(g, from_sibling, core, name):
    n, h, cols = from_sibling.shape
    tr = _pick(h, (512, 256, 128))
    steps = h // tr

    def body(core_ref, a_ref, b_ref, o_ref):
        o_ref[...] = (a_ref[...] + b_ref[...]).astype(o_ref.dtype)

    return pl.pallas_call(
        body, name=name, out_shape=jax.ShapeDtypeStruct(from_sibling.shape, jnp.bfloat16),
        grid_spec=pltpu.PrefetchScalarGridSpec(
            num_scalar_prefetch=1, grid=(n, steps),
            in_specs=[pl.BlockSpec((1, tr, cols), lambda s, i, core_ref: (s, core_ref[0] * steps + i, 0)),
                      pl.BlockSpec((1, tr, cols), lambda s, i, core_ref: (s, i, 0))],
            out_specs=pl.BlockSpec((1, tr, cols), lambda s, i, core_ref: (s, i, 0))),
        compiler_params=_params("parallel", "parallel"),
    )(core.reshape(1).astype(jnp.int32), g, from_sibling)


def _sum_slots(p, core, name):
    n, r, cols = p.shape
    tr = _pick(r, (512, 256, 128))
    steps = r // tr

    def body(core_ref, p_ref, o_ref):
        o_ref[...] = ((p_ref[0].astype(F32) + p_ref[1].astype(F32)) + p_ref[2].astype(F32)) + p_ref[3].astype(F32)

    return pl.pallas_call(
        body, name=name, out_shape=jax.ShapeDtypeStruct((2 * r, cols), F32),
        grid_spec=pltpu.PrefetchScalarGridSpec(
            num_scalar_prefetch=1, grid=(steps,),
            in_specs=[pl.BlockSpec((n, tr, cols), lambda i, core_ref: (0, i, 0))],
            out_specs=pl.BlockSpec((tr, cols), lambda i, core_ref: (core_ref[0] * steps + i, 0))),
        compiler_params=_params("parallel"),
    )(core.reshape(1).astype(jnp.int32), p)


PACK_COLS = 1024


def _pack_shards(parts):
    return jnp.concatenate([p.reshape(-1, PACK_COLS) for p in parts], axis=0)


def _unpack_shards(buf, shapes):
    out, row = [], 0
    for shp in shapes:
        nrows = math.prod(shp) // PACK_COLS
        out.append(buf[..., row:row + nrows, :].reshape(buf.shape[:-2] + tuple(shp)))
        row += nrows
    return out


def _local_step(x, target, w):
    t = lambda a: a.T
    g = {}
    h0 = _rms_fwd(x, w["norm_mix_g"][0], "rms_mix0")
    w_in_g, w_in_x = w["a_w_in"][:, :D_RNN], w["a_w_in"][:, D_RNN:]
    gate_br = _matmul([(h0, w_in_g)], F32, "mm_a_gate")
    x_br = _matmul([(h0, w_in_x)], F32, "mm_a_xbr")
    y_a, hs = _rglru_fwd(gate_br, x_br, w["a_conv_w"], w["a_conv_b"], w["a_w_r"], w["a_w_i"], w["a_b_r"],
                         w["a_b_i"], w["a_lambda"], "rglru_fwd")
    x1, h1 = _matmul([(y_a, w["a_w_out"])], F32, "mm_a_out", addend=x, norm_gain=w["norm_ffn_g"][0])
    fg0, fu0, act0 = _ffn_up(h1, w["ffn_w_gate"][0], w["ffn_w_up"][0], "ffn0_up")
    x2, h2 = _matmul([(act0, w["ffn_w_down"][0])], F32, "mm_f0_down", addend=x1, norm_gain=w["norm_mix_g"][1])
    qkv = _matmul([(h2, w["b_w_qkv"])], CD, "mm_b_qkv", out_lbm=True, tn=1024)
    o = _attn_fwd(qkv, "attn_fwd")
    x3, h3 = _matmul([(o, w["b_w_out"])], F32, "mm_b_out", a_lbm=True, addend=x2, norm_gain=w["norm_ffn_g"][1])
    fg1, fu1, act1 = _ffn_up(h3, w["ffn_w_gate"][1], w["ffn_w_up"][1], "ffn1_up")
    x4 = _matmul([(act1, w["ffn_w_down"][1])], F32, "mm_f1_down", addend=x3)
    loss, dx4, dx4c, g["final_g"] = _loss_head(x4, w["final_g"], target, "loss_head")

    def ffn_bwd(dx_out, dxc, h, x_in, fg, fu, act, layer, tag):
        dg, du = _ffn_dact(dxc, t(w["ffn_w_down"][layer]), fg, fu, "ffn_" + tag + "_dact")
        dwd = _matmul([(act, dxc)], F32, "mm_" + tag + "_dwd", trans_a=True)
        dwg = _matmul([(h, dg)], F32, "mm_" + tag + "_dwg", trans_a=True)
        dwu = _matmul([(h, du)], F32, "mm_" + tag + "_dwu", trans_a=True)
        dx_in, dx_in_c, dgain = _matmul([(dg, t(w["ffn_w_gate"][layer])), (du, t(w["ffn_w_up"][layer]))], F32,
                                        "mm_" + tag + "_dh", norm_bwd=(x_in, w["norm_ffn_g"][layer], dx_out))
        return dx_in, dx_in_c, dgain, dwg, dwu, dwd

    dx3, dx3c, dgf1, dwg1, dwu1, dwd1 = ffn_bwd(dx4, dx4c, h3, x3, fg1, fu1, act1, 1, "f1")
    do = _matmul([(dx3c, t(w["b_w_out"]))], F32, "mm_b_do", out_lbm=True, tn=1024)
    g["b_w_out"] = _matmul([(o, dx3c)], F32, "mm_b_dwout", trans_a=True, a_lbm=True)
    dq, dk, dv = _attn_bwd(qkv, o, do, "attn_bwd")
    wq_t = t(w["b_w_qkv"])
    parts = (dq, dk, dv)
    g["b_w_qkv"] = jnp.concatenate(
        [_matmul([(h2, p)], F32, "mm_b_dwqkv%d" % n, trans_a=True, b_lbm=True) for n, p in enumerate(parts)], axis=1)
    dx2, dx2c, dgm1 = _matmul([(p, wq_t[n * D_MODEL:(n + 1) * D_MODEL]) for n, p in enumerate(parts)], F32, "mm_b_dh",
                              a_lbm=True, norm_bwd=(x2, w["norm_mix_g"][1], dx3))
    dx1, dx1c, dgf0, dwg0, dwu0, dwd0 = ffn_bwd(dx2, dx2c, h1, x1, fg0, fu0, act0, 0, "f0")
    dy_a = _matmul([(dx1c, t(w["a_w_out"]))], F32, "mm_a_dy")
    g["a_w_out"] = _matmul([(y_a, dx1c)], F32, "mm_a_dwout", trans_a=True)
    wrt = jnp.swapaxes(w["a_w_r"], 1, 2)
    wit = jnp.swapaxes(w["a_w_i"], 1, 2)
    (dgate, dxbr, g["a_conv_w"], g["a_conv_b"], g["a_b_r"], g["a_b_i"], g["a_lambda"], g["a_w_r"],
     g["a_w_i"]) = _rglru_bwd(dy_a, gate_br, x_br, hs, w["a_conv_w"], w["a_conv_b"], w["a_w_r"], w["a_w_i"], wrt, wit,
                              w["a_b_r"], w["a_b_i"], w["a_lambda"], "rglru_bwd")
    g["a_w_in"] = jnp.concatenate([_matmul([(h0, dgate)], F32, "mm_a_dwin_g", trans_a=True),
                                   _matmul([(h0, dxbr)], F32, "mm_a_dwin_x", trans_a=True)], axis=1)
    dx0, _, dgm0 = _matmul([(dgate, t(w_in_g)), (dxbr, t(w_in_x))], F32, "mm_a_dh",
                           norm_bwd=(x, w["norm_mix_g"][0], dx1))
    g["norm_mix_g"] = jnp.concatenate([dgm0, dgm1], axis=0)
    g["norm_ffn_g"] = jnp.concatenate([dgf0, dgf1], axis=0)
    g["ffn_w_gate"] = jnp.stack([dwg0, dwg1])
    g["ffn_w_up"] = jnp.stack([dwu0, dwu1])
    g["ffn_w_down"] = jnp.stack([dwd0, dwd1])
    return loss, dx0, g


WEIGHTS = ["norm_mix_g", "norm_ffn_g", "a_w_in", "a_conv_w", "a_conv_b", "a_w_r", "a_b_r", "a_w_i", "a_b_i",
           "a_lambda", "a_w_out", "b_w_qkv", "b_w_out", "ffn_w_gate", "ffn_w_up", "ffn_w_down", "final_g"]
BIG = [("a_w_in", 2), ("a_w_r", 2), ("a_w_i", 2), ("a_w_out", 1), ("b_w_qkv", 2), ("b_w_out", 1),
       ("ffn_w_gate", 2), ("ffn_w_up", 2), ("ffn_w_down", 1)]
SMALL = ["norm_mix_g", "norm_ffn_g", "a_conv_w", "a_conv_b", "a_b_r", "a_b_i", "a_lambda", "final_g"]


def _split_chips(full, axis):
    return jnp.stack(jnp.split(full, N_CHIPS, axis=axis))


def _step(x, target, weights, moments_m, moments_v):
    chip = 2 * lax.axis_index("x") + lax.axis_index("y")
    core = lax.axis_index("c")
    shard_shapes = [weights[n].shape for n, _ in BIG]
    packed = _pack_shards([weights[n].astype(CD) for n, _ in BIG])
    gathered = _allgather_chips(packed, "allgather_weights")
    full = {}
    for (n, axis), stack in zip(BIG, _unpack_shards(gathered, shard_shapes)):
        own = weights[n].astype(CD)
        joined = jnp.concatenate([jnp.where(chip == s, own, stack[s]) for s in range(N_CHIPS)], axis=axis)
        full[n] = joined[0] if joined.shape[0] == 1 else joined
    cw_rows = jnp.zeros((N_CHIPS, CONV_W, RG_BW), F32)
    cw_rows = lax.dynamic_update_slice(cw_rows, jnp.where(core == 0, weights["a_conv_w"], 0.0), (chip, 0, 0))
    cw_all = _allreduce_small(cw_rows.reshape(-1, LANES), "allgather_conv_w").reshape(N_CHIPS, CONV_W, RG_BW)
    full["a_conv_w"] = jnp.concatenate([cw_all[s] for s in range(N_CHIPS)], axis=1)
    for n in ("norm_mix_g", "norm_ffn_g", "final_g"):
        full[n] = weights[n]
    for n in ("a_conv_b", "a_b_r", "a_b_i", "a_lambda"):
        full[n] = weights[n]
    loss, dx, grads = _local_step(x[0], target[0], full)
    small_parts = [grads[n].reshape(-1) for n in SMALL] + [loss.reshape(-1)]
    sizes = [p.shape[0] for p in small_parts]
    small = _allreduce_small(jnp.concatenate(small_parts).reshape(-1, LANES), "allreduce_small").reshape(-1)
    red, pos = {}, 0
    for n, sz in zip(SMALL + ["loss"], sizes):
        red[n] = small[pos:pos + sz]
        pos += sz
    loss_out = red["loss"][0]
    g_out = {}
    for n in SMALL:
        if n == "a_conv_w":
            g_out[n] = lax.dynamic_slice(red[n].reshape(CONV_W, D_RNN), (0, chip * RG_BW), (CONV_W, RG_BW)).reshape(
                weights[n].shape)
        else:
            g_out[n] = red[n].reshape(weights[n].shape)
    stacks = []
    for n, axis in BIG:
        gfull = grads[n].reshape((1,) + grads[n].shape) if grads[n].ndim == len(weights[n].shape) - 1 else grads[n]
        stacks.append(_split_chips(gfull, axis).reshape(N_CHIPS, -1, PACK_COLS))
    gbuf = jnp.concatenate(stacks, axis=1)
    from_sibling = _exchange_sibling_halves(gbuf, "rs_sibling")
    chip_partial = _add_sibling(gbuf, from_sibling, core, "rs_add_sibling")
    from_chips = _scatter_to_chips(chip_partial, "rs_chips")
    reduced = _share_halves(_sum_slots(from_chips, core, "rs_sum_chips"), "rs_share")
    for (n, _), gsh in zip(BIG, _unpack_shards(reduced, shard_shapes)):
        g_out[n] = gsh
    outs_g, outs_d, outs_m, outs_v = [], [], [], []
    for n in WEIGHTS:
        d, nm, nv = _adamw(weights[n], g_out[n], moments_m[n], moments_v[n], "adamw_" + n)
        outs_g.append(g_out[n])
        outs_d.append(d)
        outs_m.append(nm)
        outs_v.append(nv)
    return (loss_out, dx[None], *outs_g, *outs_d, *outs_m, *outs_v)


def kernel(x, norm_mix_g, norm_ffn_g, a_w_in, a_conv_w, a_conv_b, a_w_r, a_b_r, a_w_i, a_b_i, a_lambda, a_w_out, b_w_qkv, b_w_out, ffn_w_gate, ffn_w_up, ffn_w_down, final_g, loss_target, m_norm_mix_g, m_norm_ffn_g, m_a_w_in, m_a_conv_w, m_a_conv_b, m_a_w_r, m_a_b_r, m_a_w_i, m_a_b_i, m_a_lambda, m_a_w_out, m_b_w_qkv, m_b_w_out, m_ffn_w_gate, m_ffn_w_up, m_ffn_w_down, m_final_g, v_norm_mix_g, v_norm_ffn_g, v_a_w_in, v_a_conv_w, v_a_conv_b, v_a_w_r, v_a_b_r, v_a_w_i, v_a_b_i, v_a_lambda, v_a_w_out, v_b_w_qkv, v_b_w_out, v_ffn_w_gate, v_ffn_w_up, v_ffn_w_down, v_final_g):
    ws = [norm_mix_g, norm_ffn_g, a_w_in, a_conv_w, a_conv_b, a_w_r, a_b_r, a_w_i, a_b_i, a_lambda, a_w_out, b_w_qkv,
          b_w_out, ffn_w_gate, ffn_w_up, ffn_w_down, final_g]
    ms = [m_norm_mix_g, m_norm_ffn_g, m_a_w_in, m_a_conv_w, m_a_conv_b, m_a_w_r, m_a_b_r, m_a_w_i, m_a_b_i, m_a_lambda,
          m_a_w_out, m_b_w_qkv, m_b_w_out, m_ffn_w_gate, m_ffn_w_up, m_ffn_w_down, m_final_g]
    vs = [v_norm_mix_g, v_norm_ffn_g, v_a_w_in, v_a_conv_w, v_a_conv_b, v_a_w_r, v_a_b_r, v_a_w_i, v_a_b_i, v_a_lambda,
          v_a_w_out, v_b_w_qkv, v_b_w_out, v_ffn_w_gate, v_ffn_w_up, v_ffn_w_down, v_final_g]
    return _step(x, loss_target, dict(zip(WEIGHTS, ws)), dict(zip(WEIGHTS, ms)), dict(zip(WEIGHTS, vs)))
```

```python
import functools
import math

import jax
import jax.numpy as jnp
from jax import lax
from jax.experimental import pallas as pl
from jax.experimental.pallas import tpu as pltpu

F32 = jnp.float32
CD = jnp.bfloat16

D_MODEL = 1024
D_RNN = 1024
RG_BLOCKS = 4
RG_BW = 256
CONV_W = 4
RG_C = 8.0
SB_HEADS = 16
SB_HEAD_DIM = 64
D_FF = 2816
RMS_EPS = 1e-6
N_CHIPS = 4
N_DEV = 8

ADAM_LR = 0.001
ADAM_B1 = 0.9
ADAM_B2 = 0.999
ADAM_EPS = 1e-08
ADAM_WD = 0.01
ADAM_STEP = 10

LANES = 128
VMEM_LIMIT = 56 * 1024 * 1024
MESH = pl.DeviceIdType.MESH


def _params(*sem):
    return pltpu.CompilerParams(dimension_semantics=sem, vmem_limit_bytes=VMEM_LIMIT)


def _pick(n, prefs):
    for p in prefs:
        if n % p == 0:
            return p
    return n


def _matmul(pairs, out_dtype, name, *, trans_a=False, a_lbm=False, b_lbm=False, out_lbm=False, addend=None,
            tm=512, tn=None, tk=None, norm_gain=None, norm_bwd=None):
    a0, b0 = pairs[0]
    if trans_a:
        kdim = a0.shape[1] if a_lbm else a0.shape[0]
        m = a0.shape[0] * LANES if a_lbm else a0.shape[1]
    else:
        m = a0.shape[1] if a_lbm else a0.shape[0]
        kdim = a0.shape[0] * LANES if a_lbm else a0.shape[1]
    n = b0.shape[0] * LANES if b_lbm else b0.shape[1]
    tm = _pick(m, (tm, 1408, 256, 128))
    tn = tn or _pick(n, (1408, 1024, 768, 512, 256, 128))
    tk = tk or _pick(kdim, (1024, 1408, 512, 256, 128))
    nk = kdim // tk
    npair = len(pairs)

    def cat(ref):
        return jnp.concatenate([ref[p] for p in range(ref.shape[0])], axis=-1)

    def body(*refs):
        ins = refs[: 2 * npair]
        pos = 2 * npair
        add_ref = None
        if addend is not None:
            add_ref = refs[pos]
            pos += 1
        gain_ref = x_ref = dxin_ref = None
        if norm_gain is not None:
            gain_ref = refs[pos]
            pos += 1
        if norm_bwd is not None:
            x_ref, gain_ref, dxin_ref = refs[pos:pos + 3]
            pos += 3
        o_ref = refs[pos]
        extra_out = refs[pos + 1:-1]
        acc_ref = refs[-1]
        k = pl.program_id(2)

        @pl.when(k == 0)
        def _():
            acc_ref[...] = jnp.zeros_like(acc_ref)

        if norm_bwd is not None:
            @pl.when((k == 0) & (pl.program_id(0) == 0))
            def _():
                extra_out[1][...] = jnp.zeros_like(extra_out[1])

        acc = acc_ref[...]
        for p in range(npair):
            a = (cat(ins[2 * p]) if a_lbm else ins[2 * p][...]).astype(CD)
            b = (cat(ins[2 * p + 1]) if b_lbm else ins[2 * p + 1][...]).astype(CD)
            dims = (((0,), (0,)), ((), ())) if trans_a else (((1,), (0,)), ((), ()))
            acc = acc + lax.dot_general(a, b, dims, preferred_element_type=F32)
        acc_ref[...] = acc

        @pl.when(k == nk - 1)
        def _():
            res = acc_ref[...]
            if add_ref is not None:
                res = res + add_ref[...]
            if norm_gain is not None:
                rinv = lax.rsqrt(jnp.mean(res * res, axis=-1, keepdims=True) + RMS_EPS)
                extra_out[0][...] = (res * rinv * gain_ref[...]).astype(CD)
            if norm_bwd is not None:
                xv = x_ref[...]
                rinv = lax.rsqrt(jnp.mean(xv * xv, axis=-1, keepdims=True) + RMS_EPS)
                nrm = xv * rinv
                dn = res * gain_ref[...]
                extra_out[1][...] += jnp.sum(res * nrm, axis=0, keepdims=True)
                res = dxin_ref[...] + rinv * (dn - nrm * jnp.mean(dn * nrm, axis=-1, keepdims=True))
                extra_out[0][...] = res.astype(CD)
            res = res.astype(out_dtype)
            if out_lbm:
                for p in range(tn // LANES):
                    o_ref[p] = res[:, p * LANES:(p + 1) * LANES]
            else:
                o_ref[...] = res

    if trans_a:
        a_spec = (pl.BlockSpec((tm // LANES, tk, LANES), lambda i, j, k: (i, k, 0)) if a_lbm
                  else pl.BlockSpec((tk, tm), lambda i, j, k: (k, i)))
    else:
        a_spec = (pl.BlockSpec((tk // LANES, tm, LANES), lambda i, j, k: (k, i, 0)) if a_lbm
                  else pl.BlockSpec((tm, tk), lambda i, j, k: (i, k)))
    b_spec = (pl.BlockSpec((tn // LANES, tk, LANES), lambda i, j, k: (j, k, 0)) if b_lbm
              else pl.BlockSpec((tk, tn), lambda i, j, k: (k, j)))
    in_specs = []
    args = []
    for a, b in pairs:
        in_specs += [a_spec, b_spec]
        args += [a, b]
    if addend is not None:
        in_specs.append(pl.BlockSpec((tm, tn), lambda i, j, k: (i, j)))
        args.append(addend)
    tile = pl.BlockSpec((tm, tn), lambda i, j, k: (i, j))
    vec = pl.BlockSpec((1, tn), lambda i, j, k: (0, j))
    if out_lbm:
        out_shape = jax.ShapeDtypeStruct((n // LANES, m, LANES), out_dtype)
        out_spec = pl.BlockSpec((tn // LANES, tm, LANES), lambda i, j, k: (j, i, 0))
    else:
        out_shape = jax.ShapeDtypeStruct((m, n), out_dtype)
        out_spec = tile
    sem = ("parallel", "parallel", "arbitrary")
    if norm_gain is not None or norm_bwd is not None:
        assert tn == n and not out_lbm, "the norm needs whole rows in one tile"
        out_shape, out_spec = [out_shape, jax.ShapeDtypeStruct((m, n), CD)], [out_spec, tile]
    if norm_gain is not None:
        in_specs.append(vec)
        args.append(norm_gain.reshape(1, n))
    if norm_bwd is not None:
        x_in, gain, dx_in = norm_bwd
        in_specs += [tile, vec, tile]
        args += [x_in, gain.reshape(1, n), dx_in]
        out_shape.append(jax.ShapeDtypeStruct((1, n), F32))
        out_spec.append(vec)
        sem = ("arbitrary", "arbitrary", "arbitrary")
    return pl.pallas_call(
        body, name=name, out_shape=out_shape, grid=(m // tm, n // tn, nk),
        in_specs=in_specs, out_specs=out_spec,
        scratch_shapes=[pltpu.VMEM((tm, tn), F32)],
        compiler_params=_params(*sem),
    )(*args)


ROW_BLOCK = 256


def _rms_fwd(x, g, name):
    s, d = x.shape

    def body(x_ref, g_ref, h_ref):
        xv = x_ref[...]
        rinv = lax.rsqrt(jnp.mean(xv * xv, axis=-1, keepdims=True) + RMS_EPS)
        h_ref[...] = (xv * rinv * g_ref[...]).astype(CD)

    return pl.pallas_call(
        body, name=name, out_shape=jax.ShapeDtypeStruct((s, d), CD), grid=(s // ROW_BLOCK,),
        in_specs=[pl.BlockSpec((ROW_BLOCK, d), lambda i: (i, 0)), pl.BlockSpec((1, d), lambda i: (0, 0))],
        out_specs=pl.BlockSpec((ROW_BLOCK, d), lambda i: (i, 0)),
        compiler_params=_params("parallel"),
    )(x, g.reshape(1, d))


def _loss_head(x, g, target, name):
    s, d = x.shape

    def body(x_ref, g_ref, t_ref, loss_ref, dx_ref, dxc_ref, dg_ref):
        @pl.when(pl.program_id(0) == 0)
        def _():
            dg_ref[...] = jnp.zeros_like(dg_ref)
            loss_ref[...] = jnp.zeros_like(loss_ref)

        xv = x_ref[...]
        gv = g_ref[...]
        rinv = lax.rsqrt(jnp.mean(xv * xv, axis=-1, keepdims=True) + RMS_EPS)
        nrm = xv * rinv
        err = nrm * gv - t_ref[...]
        loss_ref[...] += 0.5 * jnp.sum(jnp.mean(err * err, axis=-1, keepdims=True), axis=0, keepdims=True)
        dy = err * (1.0 / d)
        dn = dy * gv
        dx = rinv * (dn - nrm * jnp.mean(dn * nrm, axis=-1, keepdims=True))
        dx_ref[...] = dx
        dxc_ref[...] = dx.astype(CD)
        dg_ref[...] += jnp.sum(dy * nrm, axis=0, keepdims=True)

    row = pl.BlockSpec((ROW_BLOCK, d), lambda i: (i, 0))
    vec = pl.BlockSpec((1, d), lambda i: (0, 0))
    return pl.pallas_call(
        body, name=name,
        out_shape=(jax.ShapeDtypeStruct((1, LANES), F32), jax.ShapeDtypeStruct((s, d), F32),
                   jax.ShapeDtypeStruct((s, d), CD), jax.ShapeDtypeStruct((1, d), F32)),
        grid=(s // ROW_BLOCK,), in_specs=[row, vec, row],
        out_specs=(pl.BlockSpec((1, LANES), lambda i: (0, 0)), row, row, vec),
        compiler_params=_params("arbitrary"),
    )(x, g.reshape(1, d), target)


def _sigmoid(z):
    return 1.0 / (1.0 + jnp.exp(-z))


FFN_TM = 512
FFN_TN = 1408


def _ffn_up(h, wg, wu, name):
    s, d = h.shape
    f = wg.shape[1]
    tm = _pick(s, (FFN_TM, 256))

    def body(h_ref, wg_ref, wu_ref, g_ref, u_ref, a_ref):
        hv = h_ref[...]
        gv = jnp.dot(hv, wg_ref[...], preferred_element_type=F32)
        uv = jnp.dot(hv, wu_ref[...], preferred_element_type=F32)
        g_ref[...] = gv
        u_ref[...] = uv
        a_ref[...] = (gv * _sigmoid(gv) * uv).astype(CD)

    a_spec = pl.BlockSpec((tm, d), lambda i, j: (i, 0))
    w_spec = pl.BlockSpec((d, FFN_TN), lambda i, j: (0, j))
    o_spec = pl.BlockSpec((tm, FFN_TN), lambda i, j: (i, j))
    return pl.pallas_call(
        body, name=name,
        out_shape=(jax.ShapeDtypeStruct((s, f), F32), jax.ShapeDtypeStruct((s, f), F32),
                   jax.ShapeDtypeStruct((s, f), CD)),
        grid=(s // tm, f // FFN_TN), in_specs=[a_spec, w_spec, w_spec], out_specs=(o_spec, o_spec, o_spec),
        compiler_params=_params("parallel", "parallel"),
    )(h, wg, wu)


def _ffn_dact(dxc, wd_t, g, u, name):
    s, d = dxc.shape
    f = wd_t.shape[1]
    tm = _pick(s, (FFN_TM, 256))

    def body(dx_ref, w_ref, g_ref, u_ref, dg_ref, du_ref):
        da = jnp.dot(dx_ref[...], w_ref[...], preferred_element_type=F32)
        gv = g_ref[...]
        sg = _sigmoid(gv)
        silu = gv * sg
        dg_ref[...] = (da * u_ref[...] * (sg + silu * (1.0 - sg))).astype(CD)
        du_ref[...] = (da * silu).astype(CD)

    a_spec = pl.BlockSpec((tm, d), lambda i, j: (i, 0))
    w_spec = pl.BlockSpec((d, FFN_TN), lambda i, j: (0, j))
    o_spec = pl.BlockSpec((tm, FFN_TN), lambda i, j: (i, j))
    return pl.pallas_call(
        body, name=name,
        out_shape=(jax.ShapeDtypeStruct((s, f), CD), jax.ShapeDtypeStruct((s, f), CD)),
        grid=(s // tm, f // FFN_TN), in_specs=[a_spec, w_spec, o_spec, o_spec], out_specs=(o_spec, o_spec),
        compiler_params=_params("parallel", "parallel"),
    )(dxc, wd_t, g, u)


TIME_BLOCK = 256
SUBLANES = 8
GELU_C = math.sqrt(2.0 / math.pi)
GELU_A = 0.044715


def _gelu(x):
    return 0.5 * x * (1.0 + jnp.tanh(GELU_C * (x + GELU_A * x * x * x)))


def _gelu_grad(x):
    t = jnp.tanh(GELU_C * (x + GELU_A * x * x * x))
    return 0.5 * (1.0 + t) + 0.5 * x * (1.0 - t * t) * GELU_C * (1.0 + 3.0 * GELU_A * x * x)


def _neg_expm1(x):
    series = -x * (1.0 + x * (0.5 + x * (1.0 / 6.0 + x * (1.0 / 24.0))))
    return jnp.where(x > -0.05, series, 1.0 - jnp.exp(x))


def _log_sigmoid(x):
    return jnp.minimum(x, 0.0) - jnp.log1p(jnp.exp(-jnp.abs(x)))


def _shift_down(x, tail, s):
    if s == 0:
        return x
    ext = jnp.concatenate([tail, x], axis=0)
    return pltpu.roll(ext, s, axis=0)[SUBLANES:]


def _shift_up(x, head, s):
    if s == 0:
        return x
    n = x.shape[0]
    ext = jnp.concatenate([x, head], axis=0)
    return pltpu.roll(ext, n + SUBLANES - s, axis=0)[:n]


def _rg_gates(xbr, tail, cw_ref, cb, wr, wi, br, bi, ls):
    taps = [_shift_down(xbr, tail, CONV_W - 1 - k) for k in range(CONV_W)]
    xc = cb
    for k in range(CONV_W):
        xc = xc + cw_ref[pl.ds(k, 1), :] * taps[k]
    xcd = xc.astype(CD)
    r = _sigmoid(jnp.dot(xcd, wr, preferred_element_type=F32) + br)
    i = _sigmoid(jnp.dot(xcd, wi, preferred_element_type=F32) + bi)
    log_a = RG_C * r * ls
    a = jnp.exp(log_a)
    mult = jnp.sqrt(jnp.maximum(_neg_expm1(2.0 * log_a), 0.0))
    return taps, xc, r, i, log_a, a, mult


def _scan8_fwd(a, u):
    row = lax.broadcasted_iota(jnp.int32, a.shape, 0)
    for d in (1, 2, 4):
        a_s = pltpu.roll(a, d, axis=0)
        u_s = pltpu.roll(u, d, axis=0)
        m = row >= d
        u = jnp.where(m, a * u_s + u, u)
        a = jnp.where(m, a * a_s, a)
    return a, u


def _scan8_bwd(b, u):
    row = lax.broadcasted_iota(jnp.int32, b.shape, 0)
    for d in (1, 2, 4):
        b_s = pltpu.roll(b, SUBLANES - d, axis=0)
        u_s = pltpu.roll(u, SUBLANES - d, axis=0)
        m = row < SUBLANES - d
        u = jnp.where(m, b * u_s + u, u)
        b = jnp.where(m, b * b_s, b)
    return b, u


def _rglru_fwd(gate_br, x_br, cw, cb, wr, wi, br, bi, lam, name):
    s, c = x_br.shape
    nt = s // TIME_BLOCK
    tb, cbw = TIME_BLOCK, RG_BW
    groups = tb // SUBLANES

    def body(g_ref, x_ref, tail_ref, cw_ref, cb_ref, wr_ref, wi_ref, br_ref, bi_ref, lam_ref,
             y_ref, hs_ref, carry_ref, a_scr, u_scr):
        t = pl.program_id(1)

        @pl.when(t == 0)
        def _():
            carry_ref[...] = jnp.zeros_like(carry_ref)

        tail = jnp.where(t > 0, tail_ref[...], 0.0)
        ls = _log_sigmoid(lam_ref[...])
        _, xc, _, i, _, a, mult = _rg_gates(x_ref[...], tail, cw_ref, cb_ref[...], wr_ref[0], wi_ref[0],
                                            br_ref[...], bi_ref[...], ls)
        a_scr[...] = a
        u_scr[...] = mult * (i * xc)
        carry = carry_ref[...]
        for gi in range(groups):
            rows = pl.ds(gi * SUBLANES, SUBLANES)
            pa, hl = _scan8_fwd(a_scr[rows, :], u_scr[rows, :])
            hs_ref[rows, :] = hl + pa * carry
            carry = hs_ref[pl.ds(gi * SUBLANES + SUBLANES - 1, 1), :]
        carry_ref[...] = carry
        y_ref[...] = (hs_ref[...] * _gelu(g_ref[...])).astype(CD)

    blk = pl.BlockSpec((tb, cbw), lambda n, t: (t, n))
    tail = pl.BlockSpec((SUBLANES, cbw), lambda n, t: (jnp.maximum(t * groups - 1, 0), n))
    vec = pl.BlockSpec((1, cbw), lambda n, t: (0, n))
    wblk = pl.BlockSpec((1, cbw, cbw), lambda n, t: (n, 0, 0))
    return pl.pallas_call(
        body, name=name,
        out_shape=(jax.ShapeDtypeStruct((s, c), CD), jax.ShapeDtypeStruct((s, c), F32)),
        grid=(RG_BLOCKS, nt),
        in_specs=[blk, blk, tail, pl.BlockSpec((CONV_W, cbw), lambda n, t: (0, n)), vec, wblk, wblk, vec, vec, vec],
        out_specs=(blk, blk),
        scratch_shapes=[pltpu.VMEM((1, cbw), F32), pltpu.VMEM((tb, cbw), F32), pltpu.VMEM((tb, cbw), F32)],
        compiler_params=_params("parallel", "arbitrary"),
    )(gate_br, x_br, x_br, cw, cb, wr, wi, br, bi, lam)


def _rglru_bwd(dy, gate_br, x_br, hs, cw, cb, wr, wi, wrt, wit, br, bi, lam, name):
    s, c = x_br.shape
    nt = s // TIME_BLOCK
    tb, cbw = TIME_BLOCK, RG_BW
    groups = tb // SUBLANES

    def body(dy_ref, g_ref, x_ref, tail_ref, hs_ref, hprev_ref, cw_ref, cb_ref, wr_ref, wi_ref, wrt_ref, wit_ref,
             br_ref, bi_ref, lam_ref,
             dg_ref, dx_ref, dcw_ref, dcb_ref, dbr_ref, dbi_ref, dlam_ref, dwr_ref, dwi_ref,
             carry_ref, head_ref, b_scr, u_scr, dh_scr):
        tr = pl.program_id(1)
        first_block = tr == nt - 1

        @pl.when(tr == 0)
        def _():
            carry_ref[...] = jnp.zeros_like(carry_ref)
            head_ref[...] = jnp.zeros_like(head_ref)
            for ref in (dcw_ref, dcb_ref, dbr_ref, dbi_ref, dlam_ref, dwr_ref, dwi_ref):
                ref[...] = jnp.zeros_like(ref)

        tail = jnp.where(first_block, 0.0, tail_ref[...])
        lam_v = lam_ref[...]
        ls = _log_sigmoid(lam_v)
        taps, xc, r, i, log_a, a, mult = _rg_gates(x_ref[...], tail, cw_ref, cb_ref[...], wr_ref[0], wi_ref[0],
                                                   br_ref[...], bi_ref[...], ls)
        gate_v = g_ref[...]
        dyv = dy_ref[...]
        hsv = hs_ref[...]
        dg_ref[...] = (dyv * hsv * _gelu_grad(gate_v)).astype(CD)

        row = lax.broadcasted_iota(jnp.int32, a.shape, 0)
        b_scr[...] = jnp.where(row == tb - 1, 1.0, pltpu.roll(a, tb - 1, axis=0))
        u_scr[...] = dyv * _gelu(gate_v)
        carry = carry_ref[...]
        for gi in reversed(range(groups)):
            rows = pl.ds(gi * SUBLANES, SUBLANES)
            pb, gl = _scan8_bwd(b_scr[rows, :], u_scr[rows, :])
            dh_scr[rows, :] = gl + pb * carry
            carry = dh_scr[pl.ds(gi * SUBLANES, 1), :]
        dh = dh_scr[...]
        carry_ref[...] = carry * jnp.sum(jnp.where(row == 0, a, 0.0), axis=0, keepdims=True)

        hprev_tail = jnp.where(first_block, 0.0, hprev_ref[...])
        h_prev = _shift_down(hsv, hprev_tail, 1)
        da = dh * h_prev
        ixc = i * xc
        dmult = dh * ixc
        di = dh * mult * xc
        dxc = dh * mult * i
        a2 = a * a
        dlog_a = da * a - dmult * a2 / mult
        dpre_r = (dlog_a * (RG_C * ls)) * r * (1.0 - r)
        dpre_i = di * i * (1.0 - i)
        dlam_ref[...] += jnp.sum(dlog_a * r, axis=0, keepdims=True) * (RG_C * _sigmoid(-lam_v))
        dbr_ref[...] += jnp.sum(dpre_r, axis=0, keepdims=True)
        dbi_ref[...] += jnp.sum(dpre_i, axis=0, keepdims=True)
        xcd = xc.astype(CD)
        dprc = dpre_r.astype(CD)
        dpic = dpre_i.astype(CD)
        tn_dims = (((0,), (0,)), ((), ()))
        dwr_ref[0] += lax.dot_general(xcd, dprc, tn_dims, preferred_element_type=F32)
        dwi_ref[0] += lax.dot_general(xcd, dpic, tn_dims, preferred_element_type=F32)
        dxc = dxc + jnp.dot(dprc, wrt_ref[0], preferred_element_type=F32) + jnp.dot(dpic, wit_ref[0],
                                                                                    preferred_element_type=F32)
        dcb_ref[...] += jnp.sum(dxc, axis=0, keepdims=True)
        for k in range(CONV_W):
            dcw_ref[pl.ds(k, 1), :] += jnp.sum(dxc * taps[k], axis=0, keepdims=True)
        head = head_ref[...]
        dxb = jnp.zeros_like(dxc)
        for sft in range(CONV_W):
            dxb = dxb + cw_ref[pl.ds(CONV_W - 1 - sft, 1), :] * _shift_up(dxc, head, sft)
        dx_ref[...] = dxb.astype(CD)
        head_ref[...] = dxc[0:SUBLANES, :]

    blk = pl.BlockSpec((tb, cbw), lambda n, t: (nt - 1 - t, n))
    tail = pl.BlockSpec((SUBLANES, cbw), lambda n, t: (jnp.maximum((nt - 1 - t) * groups - 1, 0), n))
    vec = pl.BlockSpec((1, cbw), lambda n, t: (0, n))
    cwb = pl.BlockSpec((CONV_W, cbw), lambda n, t: (0, n))
    wblk = pl.BlockSpec((1, cbw, cbw), lambda n, t: (n, 0, 0))
    vshape = jax.ShapeDtypeStruct((1, c), F32)
    wshape = jax.ShapeDtypeStruct((RG_BLOCKS, cbw, cbw), F32)
    return pl.pallas_call(
        body, name=name,
        out_shape=(jax.ShapeDtypeStruct((s, c), CD), jax.ShapeDtypeStruct((s, c), CD),
                   jax.ShapeDtypeStruct((CONV_W, c), F32), vshape, vshape, vshape, vshape, wshape, wshape),
        grid=(RG_BLOCKS, nt),
        in_specs=[blk, blk, blk, tail, blk, tail, cwb, vec, wblk, wblk, wblk, wblk, vec, vec, vec],
        out_specs=(blk, blk, cwb, vec, vec, vec, vec, wblk, wblk),
        scratch_shapes=[pltpu.VMEM((1, cbw), F32), pltpu.VMEM((SUBLANES, cbw), F32),
                        pltpu.VMEM((tb, cbw), F32), pltpu.VMEM((tb, cbw), F32), pltpu.VMEM((tb, cbw), F32)],
        compiler_params=_params("parallel", "arbitrary"),
    )(dy, gate_br, x_br, x_br, hs, hs, cw, cb, wr, wi, wrt, wit, br, bi, lam)


ATT_BLOCK = 256
ATT_Q_BLOCK = 1024
ATT_RATIO = ATT_Q_BLOCK // ATT_BLOCK
ATT_SCALE = 1.0 / math.sqrt(SB_HEAD_DIM)
N_PAIRS = SB_HEADS * SB_HEAD_DIM // LANES
NT_DIMS = (((1,), (1,)), ((), ()))
TN_DIMS = (((0,), (0,)), ((), ()))


LOG2E = 1.4426950408889634


def _neg_abs(x):
    bits = lax.bitcast_convert_type(x, jnp.uint32) | jnp.uint32(0x80000000)
    return lax.bitcast_convert_type(bits, F32)


def _qk(qx, kb):
    return lax.dot_general(qx, kb, NT_DIMS, preferred_element_type=F32)


def _sb_logits(qk, valid):
    z2 = qk * (ATT_SCALE * LOG2E)
    lb2 = jnp.minimum(z2, 0.0) - jnp.log2(1.0 + jnp.exp2(_neg_abs(z2)))
    l2 = lb2 - z2
    if valid is not None:
        l2 = jnp.where(valid, l2, 0.0)
    return lb2, l2


def _hi_lo(x):
    hi = x.astype(CD)
    lo = (x - hi.astype(F32)).astype(CD)
    return jnp.concatenate([hi, lo], axis=1)


def _tri(strict, stacked):
    r = lax.broadcasted_iota(jnp.int32, (ATT_BLOCK, ATT_BLOCK), 0)
    c = lax.broadcasted_iota(jnp.int32, (ATT_BLOCK, ATT_BLOCK), 1)
    m = (r > c if strict else r >= c).astype(CD)
    return jnp.concatenate([m, m], axis=0) if stacked else m


def _attn_fwd(qkv, name):
    _, s, _ = qkv.shape
    tq, t = ATT_Q_BLOCK, ATT_BLOCK
    nblk = s // tq

    def body(q_ref, k_ref, v_ref, o_ref, qk_scr, w_scr):
        i = pl.program_id(1)
        lane = lax.broadcasted_iota(jnp.int32, (1, LANES), 1)
        head_masks = (lane < SB_HEAD_DIM, lane >= SB_HEAD_DIM)
        q = q_ref[0]
        qs = [jnp.where(m, q, jnp.zeros_like(q)) for m in head_masks]
        tri = _tri(True, False)
        rr = lax.broadcasted_iota(jnp.int32, (tq, t), 0)
        cc = lax.broadcasted_iota(jnp.int32, (tq, t), 1)

        def rows_of(j):
            return pl.ds(pl.multiple_of(j * t, t), t)

        def tail(x, row0):
            return x if row0 == 0 else x[row0:]

        def start_logits(j, row0=0):
            kb = k_ref[0, rows_of(j), :]
            for hd in range(2):
                qk_scr[hd, row0:, :] = _qk(tail(qs[hd], row0), kb)

        def weights(run, diagonal=False, row0=0):
            new_run = []
            valid = (cc < rr)[:tq - row0] if diagonal else None
            for hd in range(2):
                lb2, l2 = _sb_logits(qk_scr[hd, row0:, :], valid)
                w = jnp.exp2(lb2 + (tail(run[hd], row0) + jnp.dot(l2.astype(CD), tri, preferred_element_type=F32)))
                if valid is not None:
                    w = jnp.where(valid, w, 0.0)
                w_scr[row0:, hd * t:(hd + 1) * t] = w.astype(CD)
                rowsum = jnp.sum(l2, axis=1, keepdims=True)
                if row0:
                    rowsum = jnp.concatenate([jnp.zeros((row0, 1), F32), rowsum], axis=0)
                new_run.append(run[hd] + rowsum)
            return tuple(new_run)

        def apply_weights(j, row0=0):
            vb = v_ref[0, rows_of(j), :]
            vcat = jnp.concatenate([jnp.where(m, vb, jnp.zeros_like(vb)) for m in head_masks], axis=0)
            inc = jnp.dot(w_scr[row0:, :], vcat, preferred_element_type=F32)
            return inc if row0 == 0 else jnp.concatenate([jnp.zeros((row0, LANES), F32), inc], axis=0)

        zero = jnp.zeros((tq, 1), F32)
        last = ATT_RATIO - 1
        start_logits(ATT_RATIO * i + last, last * t)
        run = weights((zero, zero), True, last * t)
        oacc = jnp.zeros((tq, LANES), F32)
        for d in reversed(range(last)):
            start_logits(ATT_RATIO * i + d, d * t)
            oacc = oacc + apply_weights(ATT_RATIO * i + d + 1, (d + 1) * t)
            run = weights(run, True, d * t)
        start_logits(jnp.maximum(ATT_RATIO * i - 1, 0))

        def step(jj, carry):
            run, oacc = carry
            b = ATT_RATIO * i - 1 - jj
            oacc = oacc + apply_weights(b + 1)
            run = weights(run)
            start_logits(jnp.maximum(b - 1, 0))
            return run, oacc

        run, oacc = lax.fori_loop(0, ATT_RATIO * i, step, (run, oacc))
        o_ref[0] = oacc + apply_weights(0)

    return pl.pallas_call(
        body, name=name, out_shape=jax.ShapeDtypeStruct((N_PAIRS, s, LANES), F32), grid=(N_PAIRS, nblk),
        in_specs=[pl.BlockSpec((1, tq, LANES), lambda p, i: (p, i, 0)),
                  pl.BlockSpec((1, s, LANES), lambda p, i: (N_PAIRS + p, 0, 0)),
                  pl.BlockSpec((1, s, LANES), lambda p, i: (2 * N_PAIRS + p, 0, 0))],
        out_specs=pl.BlockSpec((1, tq, LANES), lambda p, i: (p, i, 0)),
        scratch_shapes=[pltpu.VMEM((2, tq, t), F32), pltpu.VMEM((tq, 2 * t), CD)],
        compiler_params=_params("parallel", "arbitrary"),
    )(qkv, qkv, qkv)


def _attn_bwd(qkv, o, do, name):
    _, s, _ = qkv.shape
    tq, t = ATT_Q_BLOCK, ATT_BLOCK
    nblk = s // tq

    def body(q_ref, k_ref, v_ref, o_ref, do_ref, dq_ref, dk_ref, dv_ref, qk_scr, dw_scr, w_scr, dz_scr):
        i = pl.program_id(1)

        @pl.when(i == 0)
        def _():
            dk_ref[...] = jnp.zeros_like(dk_ref)
            dv_ref[...] = jnp.zeros_like(dv_ref)

        lane = lax.broadcasted_iota(jnp.int32, (1, LANES), 1)
        head_masks = (lane < SB_HEAD_DIM, lane >= SB_HEAD_DIM)
        q = q_ref[0]
        dov = do_ref[0]
        ov = o_ref[0]
        qs = [jnp.where(m, q, jnp.zeros_like(q)) for m in head_masks]
        q_scaled_t = jnp.concatenate([(qx.astype(F32) * ATT_SCALE).T for qx in qs], axis=1).astype(CD)
        docs = [jnp.where(m, dov, 0.0).astype(CD) for m in head_masks]
        docat_t = jnp.concatenate([jnp.where(m, dov, 0.0).T for m in head_masks], axis=1).astype(CD)
        totals = [jnp.sum(d.astype(F32) * ov, axis=1, keepdims=True) for d in docs]
        tri = _tri(True, False)
        tri_incl = _tri(False, True)
        rr = lax.broadcasted_iota(jnp.int32, (tq, t), 0)
        cc = lax.broadcasted_iota(jnp.int32, (tq, t), 1)

        def rows_of(j):
            return pl.ds(pl.multiple_of(j * t, t), t)

        def tail(x, row0):
            return x if row0 == 0 else x[row0:]

        def pad_rows(x, row0):
            return x if row0 == 0 else jnp.concatenate([jnp.zeros((row0, x.shape[1]), x.dtype), x], axis=0)

        def start_products(j, row0=0):
            kb = k_ref[0, rows_of(j), :]
            vb = v_ref[0, rows_of(j), :]
            for hd in range(2):
                qk_scr[hd, row0:, :] = _qk(tail(qs[hd], row0), kb)
                dw_scr[hd, row0:, :] = lax.dot_general(tail(docs[hd], row0), vb, NT_DIMS, preferred_element_type=F32)

        def logit_grads(run, erun, diagonal=False, row0=0):
            new_run, new_erun = [], []
            valid = (cc < rr)[:tq - row0] if diagonal else None
            for hd in range(2):
                lb2, l2 = _sb_logits(qk_scr[hd, row0:, :], valid)
                w = jnp.exp2(lb2 + (tail(run[hd], row0) + jnp.dot(l2.astype(CD), tri, preferred_element_type=F32)))
                if valid is not None:
                    w = jnp.where(valid, w, 0.0)
                wc = w.astype(CD)
                w_scr[hd * tq + row0:(hd + 1) * tq, :] = wc
                e = dw_scr[hd, row0:, :] * wc.astype(F32)
                prefix = (tail(totals[hd] - erun[hd], row0)
                          - jnp.dot(_hi_lo(e), tri_incl, preferred_element_type=F32))
                dz = e - jnp.exp2(lb2) * (e + prefix)
                if valid is not None:
                    dz = jnp.where(valid, dz, 0.0)
                dz_scr[hd * tq + row0:(hd + 1) * tq, :] = dz.astype(CD)
                new_run.append(run[hd] + pad_rows(jnp.sum(l2, axis=1, keepdims=True), row0))
                new_erun.append(erun[hd] + pad_rows(jnp.sum(e, axis=1, keepdims=True), row0))
            return tuple(new_run), tuple(new_erun)

        def apply_grads(j, row0=0):
            rows = rows_of(j)
            kb = k_ref[0, rows, :]
            kcat = jnp.concatenate([jnp.where(m, kb, jnp.zeros_like(kb)) for m in head_masks], axis=0)
            dz_heads = [dz_scr[hd * tq + row0:(hd + 1) * tq, :] for hd in range(2)]
            w_heads = [w_scr[hd * tq + row0:(hd + 1) * tq, :] for hd in range(2)]
            q_t = jnp.concatenate([q_scaled_t[:, hd * tq + row0:(hd + 1) * tq] for hd in range(2)], axis=1)
            do_t = jnp.concatenate([docat_t[:, hd * tq + row0:(hd + 1) * tq] for hd in range(2)], axis=1)
            dk_ref[0, :, rows] += jnp.dot(q_t, jnp.concatenate(dz_heads, axis=0), preferred_element_type=F32)
            dv_ref[0, :, rows] += jnp.dot(do_t, jnp.concatenate(w_heads, axis=0), preferred_element_type=F32)
            return pad_rows(jnp.dot(jnp.concatenate(dz_heads, axis=1), kcat, preferred_element_type=F32), row0)

        zero = jnp.zeros((tq, 1), F32)
        last = ATT_RATIO - 1
        start_products(ATT_RATIO * i + last, last * t)
        run, erun = logit_grads((zero, zero), (zero, zero), True, last * t)
        dqacc = jnp.zeros((tq, LANES), F32)
        for d in reversed(range(last)):
            start_products(ATT_RATIO * i + d, d * t)
            dqacc = dqacc + apply_grads(ATT_RATIO * i + d + 1, (d + 1) * t)
            run, erun = logit_grads(run, erun, True, d * t)
        start_products(jnp.maximum(ATT_RATIO * i - 1, 0))

        def step(jj, carry):
            run, erun, dqacc = carry
            b = ATT_RATIO * i - 1 - jj
            dqacc = dqacc + apply_grads(b + 1)
            run, erun = logit_grads(run, erun)
            start_products(jnp.maximum(b - 1, 0))
            return run, erun, dqacc

        run, erun, dqacc = lax.fori_loop(0, ATT_RATIO * i, step, (run, erun, dqacc))
        dq_ref[0] = (dqacc + apply_grads(0)) * ATT_SCALE

    qblk = pl.BlockSpec((1, tq, LANES), lambda p, i: (p, i, 0))
    full = pl.BlockSpec((1, LANES, s), lambda p, i: (p, 0, 0))
    shape = jax.ShapeDtypeStruct((N_PAIRS, s, LANES), F32)
    shape_t = jax.ShapeDtypeStruct((N_PAIRS, LANES, s), F32)
    dq, dk_t, dv_t = pl.pallas_call(
        body, name=name, out_shape=(shape, shape_t, shape_t), grid=(N_PAIRS, nblk),
        in_specs=[qblk,
                  pl.BlockSpec((1, s, LANES), lambda p, i: (N_PAIRS + p, 0, 0)),
                  pl.BlockSpec((1, s, LANES), lambda p, i: (2 * N_PAIRS + p, 0, 0)),
                  qblk, qblk],
        out_specs=(qblk, full, full),
        scratch_shapes=[pltpu.VMEM((2, tq, t), F32), pltpu.VMEM((2, tq, t), F32),
                        pltpu.VMEM((2 * tq, t), CD), pltpu.VMEM((2 * tq, t), CD)],
        compiler_params=_params("parallel", "arbitrary"),
    )(qkv, qkv, qkv, o, do)
    return dq, jnp.swapaxes(dk_t, 1, 2), jnp.swapaxes(dv_t, 1, 2)


def _adamw(w, g, m, v, name):
    shape = w.shape
    rows, cols = (shape[-2], shape[-1]) if len(shape) >= 2 else (1, shape[-1])
    lead = w.size // (rows * cols)
    tr = _pick(rows, (512, 256, 128, 64, 32, 16, 8))

    def body(w_ref, g_ref, m_ref, v_ref, d_ref, nm_ref, nv_ref):
        gv = g_ref[...]
        nm = ADAM_B1 * m_ref[...] + (1.0 - ADAM_B1) * gv
        nv = ADAM_B2 * v_ref[...] + (1.0 - ADAM_B2) * (gv * gv)
        m_hat = nm / (1.0 - ADAM_B1 ** ADAM_STEP)
        v_hat = nv / (1.0 - ADAM_B2 ** ADAM_STEP)
        d_ref[...] = -ADAM_LR * (m_hat / (jnp.sqrt(v_hat) + ADAM_EPS) + ADAM_WD * w_ref[...])
        nm_ref[...] = nm
        nv_ref[...] = nv

    blk = pl.BlockSpec((1, tr, cols), lambda l, i: (l, i, 0))
    out = jax.ShapeDtypeStruct((lead, rows, cols), F32)
    d, nm, nv = pl.pallas_call(
        body, name=name, out_shape=(out, out, out), grid=(lead, rows // tr),
        in_specs=[blk, blk, blk, blk], out_specs=(blk, blk, blk), compiler_params=_params("parallel", "parallel"),
    )(*[a.reshape(lead, rows, cols) for a in (w, g, m, v)])
    return d.reshape(shape), nm.reshape(shape), nv.reshape(shape)


HBM = pl.BlockSpec(memory_space=pltpu.HBM)


def _coords():
    return lax.axis_index("x"), lax.axis_index("y"), lax.axis_index("c")


def _other_chips(x, y):
    return [(1 - x, y), (x, 1 - y), (1 - x, 1 - y)]


def _allgather_chips(shard, name):
    r, cols = shard.shape
    half = r // 2
    quarter = half // 2

    def body(src_ref, out_ref, send_sems, recv_sems):
        x, y, c = _coords()
        sibling = (x, y, 1 - c)
        nx, ny, diag = (1 - x, y), (x, 1 - y), (1 - x, 1 - y)

        def piece(chip, core, lo, n):
            return out_ref.at[2 * chip[0] + chip[1], pl.ds(core * half + lo, n), :]

        def copy(k, dst, to, src=None):
            return pltpu.make_async_remote_copy(
                src_ref=dst if src is None else src, dst_ref=dst,
                send_sem=send_sems.at[k], recv_sem=recv_sems.at[k], device_id=to, device_id_type=MESH)

        me = (x, y)
        mine = src_ref.at[pl.ds(c * half, half), :]
        direct = [copy(0, piece(me, c, 0, half), (*nx, c), src=mine), copy(1, piece(me, c, 0, half), (*ny, c), src=mine)]
        for cp in direct:
            cp.start()
        arrivals = [piece(nx, c, 0, half), piece(ny, c, 0, half), piece(diag, c, 0, quarter),
                    piece(diag, c, quarter, quarter)]
        onward = [copy(2, piece(nx, c, 0, quarter), (*ny, c)), copy(3, piece(ny, c, quarter, quarter), (*nx, c))]
        to_sibling = [copy(4 + k, dst, sibling) for k, dst in enumerate(arrivals)]
        for k, dst in enumerate(arrivals):
            copy(k, dst, (x, y, c)).wait_recv()
            if k < 2:
                onward[k].start()
            to_sibling[k].start()
        from_sibling = [piece(nx, 1 - c, 0, half), piece(ny, 1 - c, 0, half), piece(diag, 1 - c, 0, quarter),
                        piece(diag, 1 - c, quarter, quarter)]
        for k, dst in enumerate(from_sibling):
            copy(4 + k, dst, (x, y, c)).wait_recv()
        for cp in direct + onward + to_sibling:
            cp.wait_send()

    return pl.pallas_call(
        body, name=name, out_shape=jax.ShapeDtypeStruct((N_CHIPS, r, cols), shard.dtype),
        in_specs=[HBM], out_specs=HBM,
        scratch_shapes=[pltpu.SemaphoreType.DMA((8,)), pltpu.SemaphoreType.DMA((8,))],
    )(shard)


def _exchange_sibling_halves(g, name):
    n, r, cols = g.shape
    half = r // 2

    def body(g_ref, out_ref, send_sem, recv_sem):
        x, y, c = _coords()
        cp = pltpu.make_async_remote_copy(
            src_ref=g_ref.at[:, pl.ds((1 - c) * half, half), :], dst_ref=out_ref,
            send_sem=send_sem, recv_sem=recv_sem, device_id=(x, y, 1 - c), device_id_type=MESH)
        cp.start()
        cp.wait()

    return pl.pallas_call(
        body, name=name, out_shape=jax.ShapeDtypeStruct((n, half, cols), g.dtype),
        in_specs=[HBM], out_specs=HBM,
        scratch_shapes=[pltpu.SemaphoreType.DMA, pltpu.SemaphoreType.DMA],
    )(g)


def _scatter_to_chips(p, name):
    n, h, cols = p.shape

    def body(p_ref, out_ref, send_sems, recv_sems):
        x, y, c = _coords()
        me = 2 * x + y
        sends = []
        for j, (px, py) in enumerate(_other_chips(x, y)):
            sends.append(pltpu.make_async_remote_copy(
                src_ref=p_ref.at[2 * px + py], dst_ref=out_ref.at[me],
                send_sem=send_sems.at[j], recv_sem=recv_sems.at[j], device_id=(px, py, c), device_id_type=MESH))
        for cp in sends:
            cp.start()
        for j, (px, py) in enumerate(_other_chips(x, y)):
            pltpu.make_async_remote_copy(
                src_ref=p_ref.at[me], dst_ref=out_ref.at[2 * px + py],
                send_sem=send_sems.at[j], recv_sem=recv_sems.at[j], device_id=(px, py, c),
                device_id_type=MESH).wait_recv()
        for cp in sends:
            cp.wait_send()

    return pl.pallas_call(
        body, name=name, out_shape=jax.ShapeDtypeStruct((n, h, cols), p.dtype),
        in_specs=[HBM], out_specs=HBM,
        scratch_shapes=[pltpu.SemaphoreType.DMA((3,)), pltpu.SemaphoreType.DMA((3,))],
    )(p)


def _share_halves(v, name):
    h = v.shape[0] // 2

    def body(v_ref, out_ref, send_sem, recv_sem):
        x, y, c = _coords()
        cp = pltpu.make_async_remote_copy(
            src_ref=v_ref.at[pl.ds(c * h, h), :], dst_ref=out_ref.at[pl.ds(c * h, h), :],
            send_sem=send_sem, recv_sem=recv_sem, device_id=(x, y, 1 - c), device_id_type=MESH)
        cp.start()
        pltpu.make_async_remote_copy(
            src_ref=v_ref.at[pl.ds(c * h, h), :], dst_ref=out_ref.at[pl.ds((1 - c) * h, h), :],
            send_sem=send_sem, recv_sem=recv_sem, device_id=(x, y, 1 - c), device_id_type=MESH).wait_recv()
        cp.wait_send()

    return pl.pallas_call(
        body, name=name, out_shape=jax.ShapeDtypeStruct(v.shape, v.dtype),
        in_specs=[HBM], out_specs=HBM, input_output_aliases={0: 0},
        scratch_shapes=[pltpu.SemaphoreType.DMA, pltpu.SemaphoreType.DMA],
    )(v)


def _allreduce_small(v, name):
    r, cols = v.shape

    def body(v_ref, out_ref, buf_ref, send_sems, recv_sems):
        x, y, c = _coords()
        me = 4 * x + 2 * y + c
        buf_ref[me] = v_ref[...]
        sends = []
        for k in range(1, N_DEV):
            px = 1 - x if k & 4 else x
            py = 1 - y if k & 2 else y
            pc = 1 - c if k & 1 else c
            sends.append(pltpu.make_async_remote_copy(
                src_ref=v_ref, dst_ref=buf_ref.at[me], send_sem=send_sems.at[k - 1], recv_sem=recv_sems.at[k - 1],
                device_id=(px, py, pc), device_id_type=MESH))
        for cp in sends:
            cp.start()
        for cp in sends:
            cp.wait()
        acc = buf_ref[0]
        for d in range(1, N_DEV):
            acc = acc + buf_ref[d]
        out_ref[...] = acc

    return pl.pallas_call(
        body, name=name, out_shape=jax.ShapeDtypeStruct((r, cols), F32),
        in_specs=[pl.BlockSpec(memory_space=pltpu.VMEM)], out_specs=pl.BlockSpec(memory_space=pltpu.VMEM),
        scratch_shapes=[pltpu.VMEM((N_DEV, r, cols), F32), pltpu.SemaphoreType.DMA((N_DEV - 1,)),
                        pltpu.SemaphoreType.DMA((N_DEV - 1,))],
    )(v)


def _add_sibling(g, from_sibling, core, name):
    n, h, cols = from_sibling.shape
    tr = _pick(h, (512, 256, 128))
    steps = h // tr

    def body(core_ref, a_ref, b_ref, o_ref):
        o_ref[...] = (a_ref[...] + b_ref[...]).astype(o_ref.dtype)

    return pl.pallas_call(
        body, name=name, out_shape=jax.ShapeDtypeStruct(from_sibling.shape, jnp.bfloat16),
        grid_spec=pltpu.PrefetchScalarGridSpec(
            num_scalar_prefetch=1, grid=(n, steps),
            in_specs=[pl.BlockSpec((1, tr, cols), lambda s, i, core_ref: (s, core_ref[0] * steps + i, 0)),
                      pl.BlockSpec((1, tr, cols), lambda s, i, core_ref: (s, i, 0))],
            out_specs=pl.BlockSpec((1, tr, cols), lambda s, i, core_ref: (s, i, 0))),
        compiler_params=_params("parallel", "parallel"),
    )(core.reshape(1).astype(jnp.int32), g, from_sibling)


def _sum_slots(p, own, chip, core, name):
    n, r, cols = p.shape
    tr = _pick(r, (512, 256, 128))
    steps = r // tr

    def body(core_ref, chip_ref, p_ref, own_ref, o_ref):
        parts = [jnp.where(chip_ref[0] == s, own_ref[0], p_ref[s]).astype(F32) for s in range(n)]
        o_ref[...] = ((parts[0] + parts[1]) + parts[2]) + parts[3]

    return pl.pallas_call(
        body, name=name, out_shape=jax.ShapeDtypeStruct((2 * r, cols), F32),
        grid_spec=pltpu.PrefetchScalarGridSpec(
            num_scalar_prefetch=2, grid=(steps,),
            in_specs=[pl.BlockSpec((n, tr, cols), lambda i, core_ref, chip_ref: (0, i, 0)),
                      pl.BlockSpec((1, tr, cols), lambda i, core_ref, chip_ref: (chip_ref[0], i, 0))],
            out_specs=pl.BlockSpec((tr, cols), lambda i, core_ref, chip_ref: (core_ref[0] * steps + i, 0))),
        compiler_params=_params("parallel"),
    )(core.reshape(1).astype(jnp.int32), chip.reshape(1).astype(jnp.int32), p, own)


PACK_COLS = 1024


def _pack_shards(parts):
    return jnp.concatenate([p.reshape(-1, PACK_COLS) for p in parts], axis=0)


def _unpack_shards(buf, shapes):
    out, row = [], 0
    for shp in shapes:
        nrows = math.prod(shp) // PACK_COLS
        out.append(buf[..., row:row + nrows, :].reshape(buf.shape[:-2] + tuple(shp)))
        row += nrows
    return out


def _local_step(x, target, w):
    t = lambda a: a.T
    g = {}
    h0 = _rms_fwd(x, w["norm_mix_g"][0], "rms_mix0")
    w_in_g, w_in_x = w["a_w_in"][:, :D_RNN], w["a_w_in"][:, D_RNN:]
    gate_br = _matmul([(h0, w_in_g)], F32, "mm_a_gate")
    x_br = _matmul([(h0, w_in_x)], F32, "mm_a_xbr")
    y_a, hs = _rglru_fwd(gate_br, x_br, w["a_conv_w"], w["a_conv_b"], w["a_w_r"], w["a_w_i"], w["a_b_r"],
                         w["a_b_i"], w["a_lambda"], "rglru_fwd")
    x1, h1 = _matmul([(y_a, w["a_w_out"])], F32, "mm_a_out", addend=x, norm_gain=w["norm_ffn_g"][0])
    fg0, fu0, act0 = _ffn_up(h1, w["ffn_w_gate"][0], w["ffn_w_up"][0], "ffn0_up")
    x2, h2 = _matmul([(act0, w["ffn_w_down"][0])], F32, "mm_f0_down", addend=x1, norm_gain=w["norm_mix_g"][1])
    qkv = _matmul([(h2, w["b_w_qkv"])], CD, "mm_b_qkv", out_lbm=True, tn=1024)
    o = _attn_fwd(qkv, "attn_fwd")
    x3, h3 = _matmul([(o, w["b_w_out"])], F32, "mm_b_out", a_lbm=True, addend=x2, norm_gain=w["norm_ffn_g"][1])
    fg1, fu1, act1 = _ffn_up(h3, w["ffn_w_gate"][1], w["ffn_w_up"][1], "ffn1_up")
    x4 = _matmul([(act1, w["ffn_w_down"][1])], F32, "mm_f1_down", addend=x3)
    loss, dx4, dx4c, g["final_g"] = _loss_head(x4, w["final_g"], target, "loss_head")

    def ffn_bwd(dx_out, dxc, h, x_in, fg, fu, act, layer, tag):
        dg, du = _ffn_dact(dxc, t(w["ffn_w_down"][layer]), fg, fu, "ffn_" + tag + "_dact")
        dwd = _matmul([(act, dxc)], F32, "mm_" + tag + "_dwd", trans_a=True)
        dwg = _matmul([(h, dg)], F32, "mm_" + tag + "_dwg", trans_a=True)
        dwu = _matmul([(h, du)], F32, "mm_" + tag + "_dwu", trans_a=True)
        dx_in, dx_in_c, dgain = _matmul([(dg, t(w["ffn_w_gate"][layer])), (du, t(w["ffn_w_up"][layer]))], F32,
                                        "mm_" + tag + "_dh", norm_bwd=(x_in, w["norm_ffn_g"][layer], dx_out))
        return dx_in, dx_in_c, dgain, dwg, dwu, dwd

    dx3, dx3c, dgf1, dwg1, dwu1, dwd1 = ffn_bwd(dx4, dx4c, h3, x3, fg1, fu1, act1, 1, "f1")
    do = _matmul([(dx3c, t(w["b_w_out"]))], F32, "mm_b_do", out_lbm=True, tn=1024)
    g["b_w_out"] = _matmul([(o, dx3c)], F32, "mm_b_dwout", trans_a=True, a_lbm=True)
    dq, dk, dv = _attn_bwd(qkv, o, do, "attn_bwd")
    wq_t = t(w["b_w_qkv"])
    parts = (dq, dk, dv)
    g["b_w_qkv"] = jnp.concatenate(
        [_matmul([(h2, p)], F32, "mm_b_dwqkv%d" % n, trans_a=True, b_lbm=True) for n, p in enumerate(parts)], axis=1)
    dx2, dx2c, dgm1 = _matmul([(p, wq_t[n * D_MODEL:(n + 1) * D_MODEL]) for n, p in enumerate(parts)], F32, "mm_b_dh",
                              a_lbm=True, norm_bwd=(x2, w["norm_mix_g"][1], dx3))
    dx1, dx1c, dgf0, dwg0, dwu0, dwd0 = ffn_bwd(dx2, dx2c, h1, x1, fg0, fu0, act0, 0, "f0")
    dy_a = _matmul([(dx1c, t(w["a_w_out"]))], F32, "mm_a_dy")
    g["a_w_out"] = _matmul([(y_a, dx1c)], F32, "mm_a_dwout", trans_a=True)
    wrt = jnp.swapaxes(w["a_w_r"], 1, 2)
    wit = jnp.swapaxes(w["a_w_i"], 1, 2)
    (dgate, dxbr, g["a_conv_w"], g["a_conv_b"], g["a_b_r"], g["a_b_i"], g["a_lambda"], g["a_w_r"],
     g["a_w_i"]) = _rglru_bwd(dy_a, gate_br, x_br, hs, w["a_conv_w"], w["a_conv_b"], w["a_w_r"], w["a_w_i"], wrt, wit,
                              w["a_b_r"], w["a_b_i"], w["a_lambda"], "rglru_bwd")
    g["a_w_in"] = jnp.concatenate([_matmul([(h0, dgate)], F32, "mm_a_dwin_g", trans_a=True),
                                   _matmul([(h0, dxbr)], F32, "mm_a_dwin_x", trans_a=True)], axis=1)
    dx0, _, dgm0 = _matmul([(dgate, t(w_in_g)), (dxbr, t(w_in_x))], F32, "mm_a_dh",
                           norm_bwd=(x, w["norm_mix_g"][0], dx1))
    g["norm_mix_g"] = jnp.concatenate([dgm0, dgm1], axis=0)
    g["norm_ffn_g"] = jnp.concatenate([dgf0, dgf1], axis=0)
    g["ffn_w_gate"] = jnp.stack([dwg0, dwg1])
    g["ffn_w_up"] = jnp.stack([dwu0, dwu1])
    g["ffn_w_down"] = jnp.stack([dwd0, dwd1])
    return loss, dx0, g


WEIGHTS = ["norm_mix_g", "norm_ffn_g", "a_w_in", "a_conv_w", "a_conv_b", "a_w_r", "a_b_r", "a_w_i", "a_b_i",
           "a_lambda", "a_w_out", "b_w_qkv", "b_w_out", "ffn_w_gate", "ffn_w_up", "ffn_w_down", "final_g"]
BIG = [("a_w_in", 2), ("a_w_r", 2), ("a_w_i", 2), ("a_w_out", 1), ("b_w_qkv", 2), ("b_w_out", 1),
       ("ffn_w_gate", 2), ("ffn_w_up", 2), ("ffn_w_down", 1)]
SMALL = ["norm_mix_g", "norm_ffn_g", "a_conv_w", "a_conv_b", "a_b_r", "a_b_i", "a_lambda", "final_g"]


def _split_chips(full, axis):
    return jnp.stack(jnp.split(full, N_CHIPS, axis=axis))


def _step(x, target, weights, moments_m, moments_v):
    chip = 2 * lax.axis_index("x") + lax.axis_index("y")
    core = lax.axis_index("c")
    shard_shapes = [weights[n].shape for n, _ in BIG]
    packed = _pack_shards([weights[n].astype(CD) for n, _ in BIG])
    gathered = _allgather_chips(packed, "allgather_weights")
    full = {}
    for (n, axis), stack in zip(BIG, _unpack_shards(gathered, shard_shapes)):
        own = weights[n].astype(CD)
        joined = jnp.concatenate([jnp.where(chip == s, own, stack[s]) for s in range(N_CHIPS)], axis=axis)
        full[n] = joined[0] if joined.shape[0] == 1 else joined
    cw_rows = jnp.zeros((N_CHIPS, CONV_W, RG_BW), F32)
    cw_rows = lax.dynamic_update_slice(cw_rows, jnp.where(core == 0, weights["a_conv_w"], 0.0), (chip, 0, 0))
    cw_all = _allreduce_small(cw_rows.reshape(-1, LANES), "allgather_conv_w").reshape(N_CHIPS, CONV_W, RG_BW)
    full["a_conv_w"] = jnp.concatenate([cw_all[s] for s in range(N_CHIPS)], axis=1)
    for n in ("norm_mix_g", "norm_ffn_g", "final_g"):
        full[n] = weights[n]
    for n in ("a_conv_b", "a_b_r", "a_b_i", "a_lambda"):
        full[n] = weights[n]
    loss, dx, grads = _local_step(x[0], target[0], full)
    small_parts = [grads[n].reshape(-1) for n in SMALL] + [loss.reshape(-1)]
    sizes = [p.shape[0] for p in small_parts]
    small = _allreduce_small(jnp.concatenate(small_parts).reshape(-1, LANES), "allreduce_small").reshape(-1)
    red, pos = {}, 0
    for n, sz in zip(SMALL + ["loss"], sizes):
        red[n] = small[pos:pos + sz]
        pos += sz
    loss_out = red["loss"][0]
    g_out = {}
    for n in SMALL:
        if n == "a_conv_w":
            g_out[n] = lax.dynamic_slice(red[n].reshape(CONV_W, D_RNN), (0, chip * RG_BW), (CONV_W, RG_BW)).reshape(
                weights[n].shape)
        else:
            g_out[n] = red[n].reshape(weights[n].shape)
    stacks = []
    for n, axis in BIG:
        gfull = grads[n].reshape((1,) + grads[n].shape) if grads[n].ndim == len(weights[n].shape) - 1 else grads[n]
        stacks.append(_split_chips(gfull, axis).reshape(N_CHIPS, -1, PACK_COLS))
    gbuf = jnp.concatenate(stacks, axis=1)
    from_sibling = _exchange_sibling_halves(gbuf, "rs_sibling")
    chip_partial = _add_sibling(gbuf, from_sibling, core, "rs_add_sibling")
    from_chips = _scatter_to_chips(chip_partial, "rs_chips")
    reduced = _share_halves(_sum_slots(from_chips, chip_partial, chip, core, "rs_sum_chips"), "rs_share")
    for (n, _), gsh in zip(BIG, _unpack_shards(reduced, shard_shapes)):
        g_out[n] = gsh
    outs_g, outs_d, outs_m, outs_v = [], [], [], []
    for n in WEIGHTS:
        d, nm, nv = _adamw(weights[n], g_out[n], moments_m[n], moments_v[n], "adamw_" + n)
        outs_g.append(g_out[n])
        outs_d.append(d)
        outs_m.append(nm)
        outs_v.append(nv)
    return (loss_out, dx[None], *outs_g, *outs_d, *outs_m, *outs_v)


def kernel(x, norm_mix_g, norm_ffn_g, a_w_in, a_conv_w, a_conv_b, a_w_r, a_b_r, a_w_i, a_b_i, a_lambda, a_w_out, b_w_qkv, b_w_out, ffn_w_gate, ffn_w_up, ffn_w_down, final_g, loss_target, m_norm_mix_g, m_norm_ffn_g, m_a_w_in, m_a_conv_w, m_a_conv_b, m_a_w_r, m_a_b_r, m_a_w_i, m_a_b_i, m_a_lambda, m_a_w_out, m_b_w_qkv, m_b_w_out, m_ffn_w_gate, m_ffn_w_up, m_ffn_w_down, m_final_g, v_norm_mix_g, v_norm_ffn_g, v_a_w_in, v_a_conv_w, v_a_conv_b, v_a_w_r, v_a_b_r, v_a_w_i, v_a_b_i, v_a_lambda, v_a_w_out, v_b_w_qkv, v_b_w_out, v_ffn_w_gate, v_ffn_w_up, v_ffn_w_down, v_final_g):
    ws = [norm_mix_g, norm_ffn_g, a_w_in, a_conv_w, a_conv_b, a_w_r, a_b_r, a_w_i, a_b_i, a_lambda, a_w_out, b_w_qkv,
          b_w_out, ffn_w_gate, ffn_w_up, ffn_w_down, final_g]
    ms = [m_norm_mix_g, m_norm_ffn_g, m_a_w_in, m_a_conv_w, m_a_conv_b, m_a_w_r, m_a_b_r, m_a_w_i, m_a_b_i, m_a_lambda,
          m_a_w_out, m_b_w_qkv, m_b_w_out, m_ffn_w_gate, m_ffn_w_up, m_ffn_w_down, m_final_g]
    vs = [v_norm_mix_g, v_norm_ffn_g, v_a_w_in, v_a_conv_w, v_a_conv_b, v_a_w_r, v_a_b_r, v_a_w_i, v_a_b_i, v_a_lambda,
          v_a_w_out, v_b_w_qkv, v_b_w_out, v_ffn_w_gate, v_ffn_w_up, v_ffn_w_down, v_final_g]
    return _step(x, loss_target, dict(zip(WEIGHTS, ws)), dict(zip(WEIGHTS, ms)), dict(zip(WEIGHTS, vs)))
```

```python
import functools
import math

import jax
import jax.numpy as jnp
from jax import lax
from jax.experimental import pallas as pl
from jax.experimental.pallas import tpu as pltpu
from jax.experimental.pallas import tpu_sc as plsc

F32 = jnp.float32
CD = jnp.bfloat16

D_MODEL = 1024
D_RNN = 1024
RG_BLOCKS = 4
RG_BW = 256
CONV_W = 4
RG_C = 8.0
SB_HEADS = 16
SB_HEAD_DIM = 64
D_FF = 2816
RMS_EPS = 1e-6
N_CHIPS = 4
N_DEV = 8

ADAM_LR = 0.001
ADAM_B1 = 0.9
ADAM_B2 = 0.999
ADAM_EPS = 1e-08
ADAM_WD = 0.01
ADAM_STEP = 10

LANES = 128
VMEM_LIMIT = 56 * 1024 * 1024
MESH = pl.DeviceIdType.MESH


def _params(*sem):
    return pltpu.CompilerParams(dimension_semantics=sem, vmem_limit_bytes=VMEM_LIMIT)


def _pick(n, prefs):
    for p in prefs:
        if n % p == 0:
            return p
    return n


def _row_tile(rows):
    return max(d for d in range(16, 1025, 16) if rows % d == 0)


def _matmul(pairs, out_dtype, name, *, trans_a=False, a_lbm=False, b_lbm=False, out_lbm=False, addend=None,
            tm=512, tn=None, tk=None, norm_gain=None, norm_bwd=None):
    a0, b0 = pairs[0]
    if trans_a:
        kdim = a0.shape[1] if a_lbm else a0.shape[0]
        m = a0.shape[0] * LANES if a_lbm else a0.shape[1]
    else:
        m = a0.shape[1] if a_lbm else a0.shape[0]
        kdim = a0.shape[0] * LANES if a_lbm else a0.shape[1]
    n = b0.shape[0] * LANES if b_lbm else b0.shape[1]
    tm = _pick(m, (tm, 1408, 256, 128))
    tn = tn or _pick(n, (1408, 1024, 768, 512, 256, 128))
    tk = tk or _pick(kdim, (1024, 1408, 512, 256, 128))
    nk = kdim // tk
    npair = len(pairs)

    def cat(ref):
        return jnp.concatenate([ref[p] for p in range(ref.shape[0])], axis=-1)

    def body(*refs):
        ins = refs[: 2 * npair]
        pos = 2 * npair
        add_ref = None
        if addend is not None:
            add_ref = refs[pos]
            pos += 1
        gain_ref = x_ref = dxin_ref = None
        if norm_gain is not None:
            gain_ref = refs[pos]
            pos += 1
        if norm_bwd is not None:
            x_ref, gain_ref, dxin_ref = refs[pos:pos + 3]
            pos += 3
        o_ref = refs[pos]
        extra_out = refs[pos + 1:-1]
        acc_ref = refs[-1]
        k = pl.program_id(2)

        @pl.when(k == 0)
        def _():
            acc_ref[...] = jnp.zeros_like(acc_ref)

        if norm_bwd is not None:
            @pl.when((k == 0) & (pl.program_id(0) == 0))
            def _():
                extra_out[1][...] = jnp.zeros_like(extra_out[1])

        acc = acc_ref[...]
        for p in range(npair):
            a = (cat(ins[2 * p]) if a_lbm else ins[2 * p][...]).astype(CD)
            b = (cat(ins[2 * p + 1]) if b_lbm else ins[2 * p + 1][...]).astype(CD)
            dims = (((0,), (0,)), ((), ())) if trans_a else (((1,), (0,)), ((), ()))
            acc = acc + lax.dot_general(a, b, dims, preferred_element_type=F32)
        acc_ref[...] = acc

        @pl.when(k == nk - 1)
        def _():
            res = acc_ref[...]
            if add_ref is not None:
                res = res + add_ref[...]
            if norm_gain is not None:
                rinv = lax.rsqrt(jnp.mean(res * res, axis=-1, keepdims=True) + RMS_EPS)
                extra_out[0][...] = (res * rinv * gain_ref[...]).astype(CD)
            if norm_bwd is not None:
                xv = x_ref[...]
                rinv = lax.rsqrt(jnp.mean(xv * xv, axis=-1, keepdims=True) + RMS_EPS)
                nrm = xv * rinv
                dn = res * gain_ref[...]
                extra_out[1][...] += jnp.sum(res * nrm, axis=0, keepdims=True)
                res = dxin_ref[...] + rinv * (dn - nrm * jnp.mean(dn * nrm, axis=-1, keepdims=True))
                extra_out[0][...] = res.astype(CD)
            res = res.astype(out_dtype)
            if out_lbm:
                for p in range(tn // LANES):
                    o_ref[p] = res[:, p * LANES:(p + 1) * LANES]
            else:
                o_ref[...] = res

    if trans_a:
        a_spec = (pl.BlockSpec((tm // LANES, tk, LANES), lambda i, j, k: (i, k, 0)) if a_lbm
                  else pl.BlockSpec((tk, tm), lambda i, j, k: (k, i)))
    else:
        a_spec = (pl.BlockSpec((tk // LANES, tm, LANES), lambda i, j, k: (k, i, 0)) if a_lbm
                  else pl.BlockSpec((tm, tk), lambda i, j, k: (i, k)))
    b_spec = (pl.BlockSpec((tn // LANES, tk, LANES), lambda i, j, k: (j, k, 0)) if b_lbm
              else pl.BlockSpec((tk, tn), lambda i, j, k: (k, j)))
    in_specs = []
    args = []
    for a, b in pairs:
        in_specs += [a_spec, b_spec]
        args += [a, b]
    if addend is not None:
        in_specs.append(pl.BlockSpec((tm, tn), lambda i, j, k: (i, j)))
        args.append(addend)
    tile = pl.BlockSpec((tm, tn), lambda i, j, k: (i, j))
    vec = pl.BlockSpec((1, tn), lambda i, j, k: (0, j))
    if out_lbm:
        out_shape = jax.ShapeDtypeStruct((n // LANES, m, LANES), out_dtype)
        out_spec = pl.BlockSpec((tn // LANES, tm, LANES), lambda i, j, k: (j, i, 0))
    else:
        out_shape = jax.ShapeDtypeStruct((m, n), out_dtype)
        out_spec = tile
    sem = ("parallel", "parallel", "arbitrary")
    if norm_gain is not None or norm_bwd is not None:
        assert tn == n and not out_lbm, "the norm needs whole rows in one tile"
        out_shape, out_spec = [out_shape, jax.ShapeDtypeStruct((m, n), CD)], [out_spec, tile]
    if norm_gain is not None:
        in_specs.append(vec)
        args.append(norm_gain.reshape(1, n))
    if norm_bwd is not None:
        x_in, gain, dx_in = norm_bwd
        in_specs += [tile, vec, tile]
        args += [x_in, gain.reshape(1, n), dx_in]
        out_shape.append(jax.ShapeDtypeStruct((1, n), F32))
        out_spec.append(vec)
        sem = ("arbitrary", "arbitrary", "arbitrary")
    return pl.pallas_call(
        body, name=name, out_shape=out_shape, grid=(m // tm, n // tn, nk),
        in_specs=in_specs, out_specs=out_spec,
        scratch_shapes=[pltpu.VMEM((tm, tn), F32)],
        compiler_params=_params(*sem),
    )(*args)


ROW_BLOCK = 256


def _rms_fwd(x, g, name):
    s, d = x.shape

    def body(x_ref, g_ref, h_ref):
        xv = x_ref[...]
        rinv = lax.rsqrt(jnp.mean(xv * xv, axis=-1, keepdims=True) + RMS_EPS)
        h_ref[...] = (xv * rinv * g_ref[...]).astype(CD)

    return pl.pallas_call(
        body, name=name, out_shape=jax.ShapeDtypeStruct((s, d), CD), grid=(s // ROW_BLOCK,),
        in_specs=[pl.BlockSpec((ROW_BLOCK, d), lambda i: (i, 0)), pl.BlockSpec((1, d), lambda i: (0, 0))],
        out_specs=pl.BlockSpec((ROW_BLOCK, d), lambda i: (i, 0)),
        compiler_params=_params("parallel"),
    )(x, g.reshape(1, d))


def _loss_head(x, g, target, name):
    s, d = x.shape

    def body(x_ref, g_ref, t_ref, loss_ref, dx_ref, dxc_ref, dg_ref):
        @pl.when(pl.program_id(0) == 0)
        def _():
            dg_ref[...] = jnp.zeros_like(dg_ref)
            loss_ref[...] = jnp.zeros_like(loss_ref)

        xv = x_ref[...]
        gv = g_ref[...]
        rinv = lax.rsqrt(jnp.mean(xv * xv, axis=-1, keepdims=True) + RMS_EPS)
        nrm = xv * rinv
        err = nrm * gv - t_ref[...]
        loss_ref[...] += 0.5 * jnp.sum(jnp.mean(err * err, axis=-1, keepdims=True), axis=0, keepdims=True)
        dy = err * (1.0 / d)
        dn = dy * gv
        dx = rinv * (dn - nrm * jnp.mean(dn * nrm, axis=-1, keepdims=True))
        dx_ref[...] = dx
        dxc_ref[...] = dx.astype(CD)
        dg_ref[...] += jnp.sum(dy * nrm, axis=0, keepdims=True)

    row = pl.BlockSpec((ROW_BLOCK, d), lambda i: (i, 0))
    vec = pl.BlockSpec((1, d), lambda i: (0, 0))
    return pl.pallas_call(
        body, name=name,
        out_shape=(jax.ShapeDtypeStruct((1, LANES), F32), jax.ShapeDtypeStruct((s, d), F32),
                   jax.ShapeDtypeStruct((s, d), CD), jax.ShapeDtypeStruct((1, d), F32)),
        grid=(s // ROW_BLOCK,), in_specs=[row, vec, row],
        out_specs=(pl.BlockSpec((1, LANES), lambda i: (0, 0)), row, row, vec),
        compiler_params=_params("arbitrary"),
    )(x, g.reshape(1, d), target)


def _sigmoid(z):
    return 1.0 / (1.0 + jnp.exp(-z))


FFN_TM = 512
FFN_TN = 1408


def _ffn_up(h, wg, wu, name):
    s, d = h.shape
    f = wg.shape[1]
    tm = _pick(s, (FFN_TM, 256))

    def body(h_ref, wg_ref, wu_ref, g_ref, u_ref, a_ref):
        hv = h_ref[...]
        gv = jnp.dot(hv, wg_ref[...], preferred_element_type=F32)
        uv = jnp.dot(hv, wu_ref[...], preferred_element_type=F32)
        g_ref[...] = gv
        u_ref[...] = uv
        a_ref[...] = (gv * _sigmoid(gv) * uv).astype(CD)

    a_spec = pl.BlockSpec((tm, d), lambda i, j: (i, 0))
    w_spec = pl.BlockSpec((d, FFN_TN), lambda i, j: (0, j))
    o_spec = pl.BlockSpec((tm, FFN_TN), lambda i, j: (i, j))
    return pl.pallas_call(
        body, name=name,
        out_shape=(jax.ShapeDtypeStruct((s, f), F32), jax.ShapeDtypeStruct((s, f), F32),
                   jax.ShapeDtypeStruct((s, f), CD)),
        grid=(s // tm, f // FFN_TN), in_specs=[a_spec, w_spec, w_spec], out_specs=(o_spec, o_spec, o_spec),
        compiler_params=_params("parallel", "parallel"),
    )(h, wg, wu)


def _ffn_dact(dxc, wd_t, g, u, name):
    s, d = dxc.shape
    f = wd_t.shape[1]
    tm = _pick(s, (FFN_TM, 256))

    def body(dx_ref, w_ref, g_ref, u_ref, dg_ref, du_ref):
        da = jnp.dot(dx_ref[...], w_ref[...], preferred_element_type=F32)
        gv = g_ref[...]
        sg = _sigmoid(gv)
        silu = gv * sg
        dg_ref[...] = (da * u_ref[...] * (sg + silu * (1.0 - sg))).astype(CD)
        du_ref[...] = (da * silu).astype(CD)

    a_spec = pl.BlockSpec((tm, d), lambda i, j: (i, 0))
    w_spec = pl.BlockSpec((d, FFN_TN), lambda i, j: (0, j))
    o_spec = pl.BlockSpec((tm, FFN_TN), lambda i, j: (i, j))
    return pl.pallas_call(
        body, name=name,
        out_shape=(jax.ShapeDtypeStruct((s, f), CD), jax.ShapeDtypeStruct((s, f), CD)),
        grid=(s // tm, f // FFN_TN), in_specs=[a_spec, w_spec, o_spec, o_spec], out_specs=(o_spec, o_spec),
        compiler_params=_params("parallel", "parallel"),
    )(dxc, wd_t, g, u)


TIME_BLOCK = 256
SUBLANES = 8
GELU_C = math.sqrt(2.0 / math.pi)
GELU_A = 0.044715


def _gelu(x):
    return 0.5 * x * (1.0 + jnp.tanh(GELU_C * (x + GELU_A * x * x * x)))


def _gelu_grad(x):
    t = jnp.tanh(GELU_C * (x + GELU_A * x * x * x))
    return 0.5 * (1.0 + t) + 0.5 * x * (1.0 - t * t) * GELU_C * (1.0 + 3.0 * GELU_A * x * x)


def _neg_expm1(x):
    series = -x * (1.0 + x * (0.5 + x * (1.0 / 6.0 + x * (1.0 / 24.0))))
    return jnp.where(x > -0.05, series, 1.0 - jnp.exp(x))


def _log_sigmoid(x):
    return jnp.minimum(x, 0.0) - jnp.log1p(jnp.exp(-jnp.abs(x)))


def _shift_down(x, tail, s):
    if s == 0:
        return x
    ext = jnp.concatenate([tail, x], axis=0)
    return pltpu.roll(ext, s, axis=0)[SUBLANES:]


def _shift_up(x, head, s):
    if s == 0:
        return x
    n = x.shape[0]
    ext = jnp.concatenate([x, head], axis=0)
    return pltpu.roll(ext, n + SUBLANES - s, axis=0)[:n]


def _rg_gates(xbr, tail, cw_ref, cb, wr, wi, br, bi, ls):
    taps = [_shift_down(xbr, tail, CONV_W - 1 - k) for k in range(CONV_W)]
    xc = cb
    for k in range(CONV_W):
        xc = xc + cw_ref[pl.ds(k, 1), :] * taps[k]
    xcd = xc.astype(CD)
    r = _sigmoid(jnp.dot(xcd, wr, preferred_element_type=F32) + br)
    i = _sigmoid(jnp.dot(xcd, wi, preferred_element_type=F32) + bi)
    log_a = RG_C * r * ls
    a = jnp.exp(log_a)
    mult = jnp.sqrt(jnp.maximum(_neg_expm1(2.0 * log_a), 0.0))
    return taps, xc, r, i, log_a, a, mult


def _scan8_fwd(a, u):
    row = lax.broadcasted_iota(jnp.int32, a.shape, 0)
    for d in (1, 2, 4):
        a_s = pltpu.roll(a, d, axis=0)
        u_s = pltpu.roll(u, d, axis=0)
        m = row >= d
        u = jnp.where(m, a * u_s + u, u)
        a = jnp.where(m, a * a_s, a)
    return a, u


def _scan8_bwd(b, u):
    row = lax.broadcasted_iota(jnp.int32, b.shape, 0)
    for d in (1, 2, 4):
        b_s = pltpu.roll(b, SUBLANES - d, axis=0)
        u_s = pltpu.roll(u, SUBLANES - d, axis=0)
        m = row < SUBLANES - d
        u = jnp.where(m, b * u_s + u, u)
        b = jnp.where(m, b * b_s, b)
    return b, u


def _rglru_fwd(gate_br, x_br, cw, cb, wr, wi, br, bi, lam, name):
    s, c = x_br.shape
    nt = s // TIME_BLOCK
    tb, cbw = TIME_BLOCK, RG_BW
    groups = tb // SUBLANES

    def body(g_ref, x_ref, tail_ref, cw_ref, cb_ref, wr_ref, wi_ref, br_ref, bi_ref, lam_ref,
             y_ref, hs_ref, carry_ref, a_scr, u_scr):
        t = pl.program_id(1)

        @pl.when(t == 0)
        def _():
            carry_ref[...] = jnp.zeros_like(carry_ref)

        tail = jnp.where(t > 0, tail_ref[...], 0.0)
        ls = _log_sigmoid(lam_ref[...])
        _, xc, _, i, _, a, mult = _rg_gates(x_ref[...], tail, cw_ref, cb_ref[...], wr_ref[0], wi_ref[0],
                                            br_ref[...], bi_ref[...], ls)
        a_scr[...] = a
        u_scr[...] = mult * (i * xc)
        carry = carry_ref[...]
        for gi in range(groups):
            rows = pl.ds(gi * SUBLANES, SUBLANES)
            pa, hl = _scan8_fwd(a_scr[rows, :], u_scr[rows, :])
            hs_ref[rows, :] = hl + pa * carry
            carry = hs_ref[pl.ds(gi * SUBLANES + SUBLANES - 1, 1), :]
        carry_ref[...] = carry
        y_ref[...] = (hs_ref[...] * _gelu(g_ref[...])).astype(CD)

    blk = pl.BlockSpec((tb, cbw), lambda n, t: (t, n))
    tail = pl.BlockSpec((SUBLANES, cbw), lambda n, t: (jnp.maximum(t * groups - 1, 0), n))
    vec = pl.BlockSpec((1, cbw), lambda n, t: (0, n))
    wblk = pl.BlockSpec((1, cbw, cbw), lambda n, t: (n, 0, 0))
    return pl.pallas_call(
        body, name=name,
        out_shape=(jax.ShapeDtypeStruct((s, c), CD), jax.ShapeDtypeStruct((s, c), F32)),
        grid=(RG_BLOCKS, nt),
        in_specs=[blk, blk, tail, pl.BlockSpec((CONV_W, cbw), lambda n, t: (0, n)), vec, wblk, wblk, vec, vec, vec],
        out_specs=(blk, blk),
        scratch_shapes=[pltpu.VMEM((1, cbw), F32), pltpu.VMEM((tb, cbw), F32), pltpu.VMEM((tb, cbw), F32)],
        compiler_params=_params("parallel", "arbitrary"),
    )(gate_br, x_br, x_br, cw, cb, wr, wi, br, bi, lam)


def _rglru_bwd(dy, gate_br, x_br, hs, cw, cb, wr, wi, wrt, wit, br, bi, lam, name):
    s, c = x_br.shape
    nt = s // TIME_BLOCK
    tb, cbw = TIME_BLOCK, RG_BW
    groups = tb // SUBLANES

    def body(dy_ref, g_ref, x_ref, tail_ref, hs_ref, hprev_ref, cw_ref, cb_ref, wr_ref, wi_ref, wrt_ref, wit_ref,
             br_ref, bi_ref, lam_ref,
             dg_ref, dx_ref, dcw_ref, dcb_ref, dbr_ref, dbi_ref, dlam_ref, dwr_ref, dwi_ref,
             carry_ref, head_ref, b_scr, u_scr, dh_scr):
        tr = pl.program_id(1)
        first_block = tr == nt - 1

        @pl.when(tr == 0)
        def _():
            carry_ref[...] = jnp.zeros_like(carry_ref)
            head_ref[...] = jnp.zeros_like(head_ref)
            for ref in (dcw_ref, dcb_ref, dbr_ref, dbi_ref, dlam_ref, dwr_ref, dwi_ref):
                ref[...] = jnp.zeros_like(ref)

        tail = jnp.where(first_block, 0.0, tail_ref[...])
        lam_v = lam_ref[...]
        ls = _log_sigmoid(lam_v)
        taps, xc, r, i, log_a, a, mult = _rg_gates(x_ref[...], tail, cw_ref, cb_ref[...], wr_ref[0], wi_ref[0],
                                                   br_ref[...], bi_ref[...], ls)
        gate_v = g_ref[...]
        dyv = dy_ref[...]
        hsv = hs_ref[...]
        dg_ref[...] = (dyv * hsv * _gelu_grad(gate_v)).astype(CD)

        row = lax.broadcasted_iota(jnp.int32, a.shape, 0)
        b_scr[...] = jnp.where(row == tb - 1, 1.0, pltpu.roll(a, tb - 1, axis=0))
        u_scr[...] = dyv * _gelu(gate_v)
        carry = carry_ref[...]
        for gi in reversed(range(groups)):
            rows = pl.ds(gi * SUBLANES, SUBLANES)
            pb, gl = _scan8_bwd(b_scr[rows, :], u_scr[rows, :])
            dh_scr[rows, :] = gl + pb * carry
            carry = dh_scr[pl.ds(gi * SUBLANES, 1), :]
        dh = dh_scr[...]
        carry_ref[...] = carry * jnp.sum(jnp.where(row == 0, a, 0.0), axis=0, keepdims=True)

        hprev_tail = jnp.where(first_block, 0.0, hprev_ref[...])
        h_prev = _shift_down(hsv, hprev_tail, 1)
        da = dh * h_prev
        ixc = i * xc
        dmult = dh * ixc
        di = dh * mult * xc
        dxc = dh * mult * i
        a2 = a * a
        dlog_a = da * a - dmult * a2 / mult
        dpre_r = (dlog_a * (RG_C * ls)) * r * (1.0 - r)
        dpre_i = di * i * (1.0 - i)
        dlam_ref[...] += jnp.sum(dlog_a * r, axis=0, keepdims=True) * (RG_C * _sigmoid(-lam_v))
        dbr_ref[...] += jnp.sum(dpre_r, axis=0, keepdims=True)
        dbi_ref[...] += jnp.sum(dpre_i, axis=0, keepdims=True)
        xcd = xc.astype(CD)
        dprc = dpre_r.astype(CD)
        dpic = dpre_i.astype(CD)
        tn_dims = (((0,), (0,)), ((), ()))
        dwr_ref[0] += lax.dot_general(xcd, dprc, tn_dims, preferred_element_type=F32)
        dwi_ref[0] += lax.dot_general(xcd, dpic, tn_dims, preferred_element_type=F32)
        dxc = dxc + jnp.dot(dprc, wrt_ref[0], preferred_element_type=F32) + jnp.dot(dpic, wit_ref[0],
                                                                                    preferred_element_type=F32)
        dcb_ref[...] += jnp.sum(dxc, axis=0, keepdims=True)
        for k in range(CONV_W):
            dcw_ref[pl.ds(k, 1), :] += jnp.sum(dxc * taps[k], axis=0, keepdims=True)
        head = head_ref[...]
        dxb = jnp.zeros_like(dxc)
        for sft in range(CONV_W):
            dxb = dxb + cw_ref[pl.ds(CONV_W - 1 - sft, 1), :] * _shift_up(dxc, head, sft)
        dx_ref[...] = dxb.astype(CD)
        head_ref[...] = dxc[0:SUBLANES, :]

    blk = pl.BlockSpec((tb, cbw), lambda n, t: (nt - 1 - t, n))
    tail = pl.BlockSpec((SUBLANES, cbw), lambda n, t: (jnp.maximum((nt - 1 - t) * groups - 1, 0), n))
    vec = pl.BlockSpec((1, cbw), lambda n, t: (0, n))
    cwb = pl.BlockSpec((CONV_W, cbw), lambda n, t: (0, n))
    wblk = pl.BlockSpec((1, cbw, cbw), lambda n, t: (n, 0, 0))
    vshape = jax.ShapeDtypeStruct((1, c), F32)
    wshape = jax.ShapeDtypeStruct((RG_BLOCKS, cbw, cbw), F32)
    return pl.pallas_call(
        body, name=name,
        out_shape=(jax.ShapeDtypeStruct((s, c), CD), jax.ShapeDtypeStruct((s, c), CD),
                   jax.ShapeDtypeStruct((CONV_W, c), F32), vshape, vshape, vshape, vshape, wshape, wshape),
        grid=(RG_BLOCKS, nt),
        in_specs=[blk, blk, blk, tail, blk, tail, cwb, vec, wblk, wblk, wblk, wblk, vec, vec, vec],
        out_specs=(blk, blk, cwb, vec, vec, vec, vec, wblk, wblk),
        scratch_shapes=[pltpu.VMEM((1, cbw), F32), pltpu.VMEM((SUBLANES, cbw), F32),
                        pltpu.VMEM((tb, cbw), F32), pltpu.VMEM((tb, cbw), F32), pltpu.VMEM((tb, cbw), F32)],
        compiler_params=_params("parallel", "arbitrary"),
    )(dy, gate_br, x_br, x_br, hs, hs, cw, cb, wr, wi, wrt, wit, br, bi, lam)


ATT_BLOCK = 256
ATT_Q_BLOCK = 1024
ATT_RATIO = ATT_Q_BLOCK // ATT_BLOCK
ATT_SCALE = 1.0 / math.sqrt(SB_HEAD_DIM)
N_PAIRS = SB_HEADS * SB_HEAD_DIM // LANES
NT_DIMS = (((1,), (1,)), ((), ()))
TN_DIMS = (((0,), (0,)), ((), ()))


LOG2E = 1.4426950408889634


def _neg_abs(x):
    bits = lax.bitcast_convert_type(x, jnp.uint32) | jnp.uint32(0x80000000)
    return lax.bitcast_convert_type(bits, F32)


def _qk(qx, kb):
    return lax.dot_general(qx, kb, NT_DIMS, preferred_element_type=F32)


def _sb_logits(qk, valid):
    z2 = qk * (ATT_SCALE * LOG2E)
    lb2 = jnp.minimum(z2, 0.0) - jnp.log2(1.0 + jnp.exp2(_neg_abs(z2)))
    l2 = lb2 - z2
    if valid is not None:
        l2 = jnp.where(valid, l2, 0.0)
    return lb2, l2


def _hi_lo(x):
    hi = x.astype(CD)
    lo = (x - hi.astype(F32)).astype(CD)
    return jnp.concatenate([hi, lo], axis=1)


def _tri(strict, stacked):
    r = lax.broadcasted_iota(jnp.int32, (ATT_BLOCK, ATT_BLOCK), 0)
    c = lax.broadcasted_iota(jnp.int32, (ATT_BLOCK, ATT_BLOCK), 1)
    m = (r > c if strict else r >= c).astype(CD)
    return jnp.concatenate([m, m], axis=0) if stacked else m


def _attn_fwd(qkv, name):
    _, s, _ = qkv.shape
    tq, t = ATT_Q_BLOCK, ATT_BLOCK
    nblk = s // tq

    def body(q_ref, k_ref, v_ref, o_ref, qk_scr, w_scr):
        i = pl.program_id(1)
        lane = lax.broadcasted_iota(jnp.int32, (1, LANES), 1)
        head_masks = (lane < SB_HEAD_DIM, lane >= SB_HEAD_DIM)
        q = q_ref[0]
        qs = [jnp.where(m, q, jnp.zeros_like(q)) for m in head_masks]
        tri = _tri(True, False)
        rr = lax.broadcasted_iota(jnp.int32, (tq, t), 0)
        cc = lax.broadcasted_iota(jnp.int32, (tq, t), 1)

        def rows_of(j):
            return pl.ds(pl.multiple_of(j * t, t), t)

        def tail(x, row0):
            return x if row0 == 0 else x[row0:]

        def start_logits(j, row0=0):
            kb = k_ref[0, rows_of(j), :]
            for hd in range(2):
                qk_scr[hd, row0:, :] = _qk(tail(qs[hd], row0), kb)

        def weights(run, diagonal=False, row0=0):
            new_run = []
            valid = (cc < rr)[:tq - row0] if diagonal else None
            for hd in range(2):
                lb2, l2 = _sb_logits(qk_scr[hd, row0:, :], valid)
                w = jnp.exp2(lb2 + (tail(run[hd], row0) + jnp.dot(l2.astype(CD), tri, preferred_element_type=F32)))
                if valid is not None:
                    w = jnp.where(valid, w, 0.0)
                w_scr[row0:, hd * t:(hd + 1) * t] = w.astype(CD)
                rowsum = jnp.sum(l2, axis=1, keepdims=True)
                if row0:
                    rowsum = jnp.concatenate([jnp.zeros((row0, 1), F32), rowsum], axis=0)
                new_run.append(run[hd] + rowsum)
            return tuple(new_run)

        def apply_weights(j, row0=0):
            vb = v_ref[0, rows_of(j), :]
            vcat = jnp.concatenate([jnp.where(m, vb, jnp.zeros_like(vb)) for m in head_masks], axis=0)
            inc = jnp.dot(w_scr[row0:, :], vcat, preferred_element_type=F32)
            return inc if row0 == 0 else jnp.concatenate([jnp.zeros((row0, LANES), F32), inc], axis=0)

        zero = jnp.zeros((tq, 1), F32)
        last = ATT_RATIO - 1
        start_logits(ATT_RATIO * i + last, last * t)
        run = weights((zero, zero), True, last * t)
        oacc = jnp.zeros((tq, LANES), F32)
        for d in reversed(range(last)):
            start_logits(ATT_RATIO * i + d, d * t)
            oacc = oacc + apply_weights(ATT_RATIO * i + d + 1, (d + 1) * t)
            run = weights(run, True, d * t)
        start_logits(jnp.maximum(ATT_RATIO * i - 1, 0))

        def step(jj, carry):
            run, oacc = carry
            b = ATT_RATIO * i - 1 - jj
            oacc = oacc + apply_weights(b + 1)
            run = weights(run)
            start_logits(jnp.maximum(b - 1, 0))
            return run, oacc

        run, oacc = lax.fori_loop(0, ATT_RATIO * i, step, (run, oacc))
        o_ref[0] = oacc + apply_weights(0)

    return pl.pallas_call(
        body, name=name, out_shape=jax.ShapeDtypeStruct((N_PAIRS, s, LANES), F32), grid=(N_PAIRS, nblk),
        in_specs=[pl.BlockSpec((1, tq, LANES), lambda p, i: (p, i, 0)),
                  pl.BlockSpec((1, s, LANES), lambda p, i: (N_PAIRS + p, 0, 0)),
                  pl.BlockSpec((1, s, LANES), lambda p, i: (2 * N_PAIRS + p, 0, 0))],
        out_specs=pl.BlockSpec((1, tq, LANES), lambda p, i: (p, i, 0)),
        scratch_shapes=[pltpu.VMEM((2, tq, t), F32), pltpu.VMEM((tq, 2 * t), CD)],
        compiler_params=_params("parallel", "arbitrary"),
    )(qkv, qkv, qkv)


def _attn_bwd(qkv, o, do, name):
    _, s, _ = qkv.shape
    tq, t = ATT_Q_BLOCK, ATT_BLOCK
    nblk = s // tq

    def body(q_ref, k_ref, v_ref, o_ref, do_ref, dq_ref, dk_ref, dv_ref, qk_scr, dw_scr, w_scr, dz_scr):
        i = pl.program_id(1)

        @pl.when(i == 0)
        def _():
            dk_ref[...] = jnp.zeros_like(dk_ref)
            dv_ref[...] = jnp.zeros_like(dv_ref)

        lane = lax.broadcasted_iota(jnp.int32, (1, LANES), 1)
        head_masks = (lane < SB_HEAD_DIM, lane >= SB_HEAD_DIM)
        q = q_ref[0]
        dov = do_ref[0]
        ov = o_ref[0]
        qs = [jnp.where(m, q, jnp.zeros_like(q)) for m in head_masks]
        q_scaled_t = jnp.concatenate([(qx.astype(F32) * ATT_SCALE).T for qx in qs], axis=1).astype(CD)
        docs = [jnp.where(m, dov, 0.0).astype(CD) for m in head_masks]
        docat_t = jnp.concatenate([jnp.where(m, dov, 0.0).T for m in head_masks], axis=1).astype(CD)
        totals = [jnp.sum(d.astype(F32) * ov, axis=1, keepdims=True) for d in docs]
        tri = _tri(True, False)
        tri_incl = _tri(False, True)
        rr = lax.broadcasted_iota(jnp.int32, (tq, t), 0)
        cc = lax.broadcasted_iota(jnp.int32, (tq, t), 1)

        def rows_of(j):
            return pl.ds(pl.multiple_of(j * t, t), t)

        def tail(x, row0):
            return x if row0 == 0 else x[row0:]

        def pad_rows(x, row0):
            return x if row0 == 0 else jnp.concatenate([jnp.zeros((row0, x.shape[1]), x.dtype), x], axis=0)

        def start_products(j, row0=0):
            kb = k_ref[0, rows_of(j), :]
            vb = v_ref[0, rows_of(j), :]
            for hd in range(2):
                qk_scr[hd, row0:, :] = _qk(tail(qs[hd], row0), kb)
                dw_scr[hd, row0:, :] = lax.dot_general(tail(docs[hd], row0), vb, NT_DIMS, preferred_element_type=F32)

        def logit_grads(run, erun, diagonal=False, row0=0):
            new_run, new_erun = [], []
            valid = (cc < rr)[:tq - row0] if diagonal else None
            for hd in range(2):
                lb2, l2 = _sb_logits(qk_scr[hd, row0:, :], valid)
                w = jnp.exp2(lb2 + (tail(run[hd], row0) + jnp.dot(l2.astype(CD), tri, preferred_element_type=F32)))
                if valid is not None:
                    w = jnp.where(valid, w, 0.0)
                wc = w.astype(CD)
                w_scr[hd * tq + row0:(hd + 1) * tq, :] = wc
                e = dw_scr[hd, row0:, :] * wc.astype(F32)
                prefix = (tail(totals[hd] - erun[hd], row0)
                          - jnp.dot(_hi_lo(e), tri_incl, preferred_element_type=F32))
                dz = e - jnp.exp2(lb2) * (e + prefix)
                if valid is not None:
                    dz = jnp.where(valid, dz, 0.0)
                dz_scr[hd * tq + row0:(hd + 1) * tq, :] = dz.astype(CD)
                new_run.append(run[hd] + pad_rows(jnp.sum(l2, axis=1, keepdims=True), row0))
                new_erun.append(erun[hd] + pad_rows(jnp.sum(e, axis=1, keepdims=True), row0))
            return tuple(new_run), tuple(new_erun)

        def apply_grads(j, row0=0):
            rows = rows_of(j)
            kb = k_ref[0, rows, :]
            kcat = jnp.concatenate([jnp.where(m, kb, jnp.zeros_like(kb)) for m in head_masks], axis=0)
            dz_heads = [dz_scr[hd * tq + row0:(hd + 1) * tq, :] for hd in range(2)]
            w_heads = [w_scr[hd * tq + row0:(hd + 1) * tq, :] for hd in range(2)]
            q_t = jnp.concatenate([q_scaled_t[:, hd * tq + row0:(hd + 1) * tq] for hd in range(2)], axis=1)
            do_t = jnp.concatenate([docat_t[:, hd * tq + row0:(hd + 1) * tq] for hd in range(2)], axis=1)
            dk_ref[0, :, rows] += jnp.dot(q_t, jnp.concatenate(dz_heads, axis=0), preferred_element_type=F32)
            dv_ref[0, :, rows] += jnp.dot(do_t, jnp.concatenate(w_heads, axis=0), preferred_element_type=F32)
            return pad_rows(jnp.dot(jnp.concatenate(dz_heads, axis=1), kcat, preferred_element_type=F32), row0)

        zero = jnp.zeros((tq, 1), F32)
        last = ATT_RATIO - 1
        start_products(ATT_RATIO * i + last, last * t)
        run, erun = logit_grads((zero, zero), (zero, zero), True, last * t)
        dqacc = jnp.zeros((tq, LANES), F32)
        for d in reversed(range(last)):
            start_products(ATT_RATIO * i + d, d * t)
            dqacc = dqacc + apply_grads(ATT_RATIO * i + d + 1, (d + 1) * t)
            run, erun = logit_grads(run, erun, True, d * t)
        start_products(jnp.maximum(ATT_RATIO * i - 1, 0))

        def step(jj, carry):
            run, erun, dqacc = carry
            b = ATT_RATIO * i - 1 - jj
            dqacc = dqacc + apply_grads(b + 1)
            run, erun = logit_grads(run, erun)
            start_products(jnp.maximum(b - 1, 0))
            return run, erun, dqacc

        run, erun, dqacc = lax.fori_loop(0, ATT_RATIO * i, step, (run, erun, dqacc))
        dq_ref[0] = (dqacc + apply_grads(0)) * ATT_SCALE

    qblk = pl.BlockSpec((1, tq, LANES), lambda p, i: (p, i, 0))
    full = pl.BlockSpec((1, LANES, s), lambda p, i: (p, 0, 0))
    shape = jax.ShapeDtypeStruct((N_PAIRS, s, LANES), F32)
    shape_t = jax.ShapeDtypeStruct((N_PAIRS, LANES, s), F32)
    dq, dk_t, dv_t = pl.pallas_call(
        body, name=name, out_shape=(shape, shape_t, shape_t), grid=(N_PAIRS, nblk),
        in_specs=[qblk,
                  pl.BlockSpec((1, s, LANES), lambda p, i: (N_PAIRS + p, 0, 0)),
                  pl.BlockSpec((1, s, LANES), lambda p, i: (2 * N_PAIRS + p, 0, 0)),
                  qblk, qblk],
        out_specs=(qblk, full, full),
        scratch_shapes=[pltpu.VMEM((2, tq, t), F32), pltpu.VMEM((2, tq, t), F32),
                        pltpu.VMEM((2 * tq, t), CD), pltpu.VMEM((2 * tq, t), CD)],
        compiler_params=_params("parallel", "arbitrary"),
    )(qkv, qkv, qkv, o, do)
    return dq, jnp.swapaxes(dk_t, 1, 2), jnp.swapaxes(dv_t, 1, 2)


def _adamw(w, g, m, v, name):
    shape = w.shape
    rows, cols = (shape[-2], shape[-1]) if len(shape) >= 2 else (1, shape[-1])
    lead = w.size // (rows * cols)
    tr = _pick(rows, (512, 256, 128, 64, 32, 16, 8))

    def body(w_ref, g_ref, m_ref, v_ref, d_ref, nm_ref, nv_ref):
        gv = g_ref[...]
        nm = ADAM_B1 * m_ref[...] + (1.0 - ADAM_B1) * gv
        nv = ADAM_B2 * v_ref[...] + (1.0 - ADAM_B2) * (gv * gv)
        m_hat = nm / (1.0 - ADAM_B1 ** ADAM_STEP)
        v_hat = nv / (1.0 - ADAM_B2 ** ADAM_STEP)
        d_ref[...] = -ADAM_LR * (m_hat / (jnp.sqrt(v_hat) + ADAM_EPS) + ADAM_WD * w_ref[...])
        nm_ref[...] = nm
        nv_ref[...] = nv

    blk = pl.BlockSpec((1, tr, cols), lambda l, i: (l, i, 0))
    out = jax.ShapeDtypeStruct((lead, rows, cols), F32)
    d, nm, nv = pl.pallas_call(
        body, name=name, out_shape=(out, out, out), grid=(lead, rows // tr),
        in_specs=[blk, blk, blk, blk], out_specs=(blk, blk, blk), compiler_params=_params("parallel", "parallel"),
    )(*[a.reshape(lead, rows, cols) for a in (w, g, m, v)])
    return d.reshape(shape), nm.reshape(shape), nv.reshape(shape)


HBM = pl.BlockSpec(memory_space=pltpu.HBM)


def _coords():
    return lax.axis_index("x"), lax.axis_index("y"), lax.axis_index("c")


def _other_chips(x, y):
    return [(1 - x, y), (x, 1 - y), (1 - x, 1 - y)]


def _allgather_chips(shard, name):
    r, cols = shard.shape
    half = r // 2
    quarter = half // 2

    def body(src_ref, out_ref, send_sems, recv_sems):
        x, y, c = _coords()
        sibling = (x, y, 1 - c)
        nx, ny, diag = (1 - x, y), (x, 1 - y), (1 - x, 1 - y)

        def piece(chip, core, lo, n):
            return out_ref.at[2 * chip[0] + chip[1], pl.ds(core * half + lo, n), :]

        def copy(k, dst, to, src=None):
            return pltpu.make_async_remote_copy(
                src_ref=dst if src is None else src, dst_ref=dst,
                send_sem=send_sems.at[k], recv_sem=recv_sems.at[k], device_id=to, device_id_type=MESH)

        me = (x, y)
        mine = src_ref.at[pl.ds(c * half, half), :]
        direct = [copy(0, piece(me, c, 0, half), (*nx, c), src=mine), copy(1, piece(me, c, 0, half), (*ny, c), src=mine)]
        for cp in direct:
            cp.start()
        arrivals = [piece(nx, c, 0, half), piece(ny, c, 0, half), piece(diag, c, 0, quarter),
                    piece(diag, c, quarter, quarter)]
        onward = [copy(2, piece(nx, c, 0, quarter), (*ny, c)), copy(3, piece(ny, c, quarter, quarter), (*nx, c))]
        to_sibling = [copy(4 + k, dst, sibling) for k, dst in enumerate(arrivals)]
        for k, dst in enumerate(arrivals):
            copy(k, dst, (x, y, c)).wait_recv()
            if k < 2:
                onward[k].start()
            to_sibling[k].start()
        from_sibling = [piece(nx, 1 - c, 0, half), piece(ny, 1 - c, 0, half), piece(diag, 1 - c, 0, quarter),
                        piece(diag, 1 - c, quarter, quarter)]
        for k, dst in enumerate(from_sibling):
            copy(4 + k, dst, (x, y, c)).wait_recv()
        for cp in direct + onward + to_sibling:
            cp.wait_send()

    return pl.pallas_call(
        body, name=name, out_shape=jax.ShapeDtypeStruct((N_CHIPS, r, cols), shard.dtype),
        in_specs=[HBM], out_specs=HBM,
        scratch_shapes=[pltpu.SemaphoreType.DMA((8,)), pltpu.SemaphoreType.DMA((8,))],
    )(shard)


def _exchange_sibling_halves(g, name):
    n, r, cols = g.shape
    half = r // 2

    def body(g_ref, out_ref, send_sem, recv_sem):
        x, y, c = _coords()
        cp = pltpu.make_async_remote_copy(
            src_ref=g_ref.at[:, pl.ds((1 - c) * half, half), :], dst_ref=out_ref,
            send_sem=send_sem, recv_sem=recv_sem, device_id=(x, y, 1 - c), device_id_type=MESH)
        cp.start()
        cp.wait()

    return pl.pallas_call(
        body, name=name, out_shape=jax.ShapeDtypeStruct((n, half, cols), g.dtype),
        in_specs=[HBM], out_specs=HBM,
        scratch_shapes=[pltpu.SemaphoreType.DMA, pltpu.SemaphoreType.DMA],
    )(g)


def _scatter_to_chips(p, name):
    n, h, cols = p.shape

    def body(p_ref, out_ref, send_sems, recv_sems):
        x, y, c = _coords()
        me = 2 * x + y
        sends = []
        for j, (px, py) in enumerate(_other_chips(x, y)):
            sends.append(pltpu.make_async_remote_copy(
                src_ref=p_ref.at[2 * px + py], dst_ref=out_ref.at[me],
                send_sem=send_sems.at[j], recv_sem=recv_sems.at[j], device_id=(px, py, c), device_id_type=MESH))
        for cp in sends:
            cp.start()
        for j, (px, py) in enumerate(_other_chips(x, y)):
            pltpu.make_async_remote_copy(
                src_ref=p_ref.at[me], dst_ref=out_ref.at[2 * px + py],
                send_sem=send_sems.at[j], recv_sem=recv_sems.at[j], device_id=(px, py, c),
                device_id_type=MESH).wait_recv()
        for cp in sends:
            cp.wait_send()

    return pl.pallas_call(
        body, name=name, out_shape=jax.ShapeDtypeStruct((n, h, cols), p.dtype),
        in_specs=[HBM], out_specs=HBM,
        scratch_shapes=[pltpu.SemaphoreType.DMA((3,)), pltpu.SemaphoreType.DMA((3,))],
    )(p)


SC_SCATTER_COLLECTIVE_ID = 3


def _scatter_to_chips_async(p, name):
    p_ref = jax.new_ref(p, memory_space=pltpu.MemorySpace.HBM)
    out_ref = jax.empty_ref(jax.ShapeDtypeStruct(p.shape, p.dtype), memory_space=pltpu.MemorySpace.HBM)

    @pl.kernel(mesh=plsc.ScalarSubcoreMesh(axis_name="sequencer", num_cores=1), name=name,
               scratch_types=(pltpu.SemaphoreType.DMA((3,)), pltpu.SemaphoreType.DMA((3,))),
               compiler_params=pltpu.CompilerParams(collective_id=SC_SCATTER_COLLECTIVE_ID))
    def launch(send_sems, recv_sems):
        x, y, c = _coords()
        me = 2 * x + y
        barrier = pltpu.get_barrier_semaphore()
        for px, py in _other_chips(x, y):
            pl.semaphore_signal(barrier, inc=1, device_id=(px, py, c), device_id_type=MESH)
        pl.semaphore_wait(barrier, 3)
        sends = []
        for j, (px, py) in enumerate(_other_chips(x, y)):
            sends.append(pltpu.make_async_remote_copy(
                src_ref=p_ref.at[2 * px + py], dst_ref=out_ref.at[me],
                send_sem=send_sems.at[j], recv_sem=recv_sems.at[j], device_id=(px, py, c), device_id_type=MESH))
        for cp in sends:
            cp.start()
        for j, (px, py) in enumerate(_other_chips(x, y)):
            pltpu.make_async_remote_copy(
                src_ref=p_ref.at[me], dst_ref=out_ref.at[2 * px + py],
                send_sem=send_sems.at[j], recv_sem=recv_sems.at[j], device_id=(px, py, c),
                device_id_type=MESH).wait_recv()
        for cp in sends:
            cp.wait_send()

    launch()
    return out_ref[...]


def _share_halves(v, name):
    h = v.shape[0] // 2

    def body(v_ref, out_ref, send_sem, recv_sem):
        x, y, c = _coords()
        cp = pltpu.make_async_remote_copy(
            src_ref=v_ref.at[pl.ds(c * h, h), :], dst_ref=out_ref.at[pl.ds(c * h, h), :],
            send_sem=send_sem, recv_sem=recv_sem, device_id=(x, y, 1 - c), device_id_type=MESH)
        cp.start()
        pltpu.make_async_remote_copy(
            src_ref=v_ref.at[pl.ds(c * h, h), :], dst_ref=out_ref.at[pl.ds((1 - c) * h, h), :],
            send_sem=send_sem, recv_sem=recv_sem, device_id=(x, y, 1 - c), device_id_type=MESH).wait_recv()
        cp.wait_send()

    return pl.pallas_call(
        body, name=name, out_shape=jax.ShapeDtypeStruct(v.shape, v.dtype),
        in_specs=[HBM], out_specs=HBM, input_output_aliases={0: 0},
        scratch_shapes=[pltpu.SemaphoreType.DMA, pltpu.SemaphoreType.DMA],
    )(v)


def _allreduce_small(v, name):
    r, cols = v.shape

    def body(v_ref, out_ref, buf_ref, send_sems, recv_sems):
        x, y, c = _coords()
        me = 4 * x + 2 * y + c
        buf_ref[me] = v_ref[...]
        sends = []
        for k in range(1, N_DEV):
            px = 1 - x if k & 4 else x
            py = 1 - y if k & 2 else y
            pc = 1 - c if k & 1 else c
            sends.append(pltpu.make_async_remote_copy(
                src_ref=v_ref, dst_ref=buf_ref.at[me], send_sem=send_sems.at[k - 1], recv_sem=recv_sems.at[k - 1],
                device_id=(px, py, pc), device_id_type=MESH))
        for cp in sends:
            cp.start()
        for cp in sends:
            cp.wait()
        acc = buf_ref[0]
        for d in range(1, N_DEV):
            acc = acc + buf_ref[d]
        out_ref[...] = acc

    return pl.pallas_call(
        body, name=name, out_shape=jax.ShapeDtypeStruct((r, cols), F32),
        in_specs=[pl.BlockSpec(memory_space=pltpu.VMEM)], out_specs=pl.BlockSpec(memory_space=pltpu.VMEM),
        scratch_shapes=[pltpu.VMEM((N_DEV, r, cols), F32), pltpu.SemaphoreType.DMA((N_DEV - 1,)),
                        pltpu.SemaphoreType.DMA((N_DEV - 1,))],
    )(v)


def _add_sibling(g, from_sibling, core, name):
    n, h, cols = from_sibling.shape
    tr = _row_tile(h)
    steps = h // tr

    def body(core_ref, a_ref, b_ref, o_ref):
        o_ref[...] = (a_ref[...] + b_ref[...]).astype(o_ref.dtype)

    return pl.pallas_call(
        body, name=name, out_shape=jax.ShapeDtypeStruct(from_sibling.shape, jnp.bfloat16),
        grid_spec=pltpu.PrefetchScalarGridSpec(
            num_scalar_prefetch=1, grid=(n, steps),
            in_specs=[pl.BlockSpec((1, tr, cols), lambda s, i, core_ref: (s, core_ref[0] * steps + i, 0)),
                      pl.BlockSpec((1, tr, cols), lambda s, i, core_ref: (s, i, 0))],
            out_specs=pl.BlockSpec((1, tr, cols), lambda s, i, core_ref: (s, i, 0))),
        compiler_params=_params("parallel", "parallel"),
    )(core.reshape(1).astype(jnp.int32), g, from_sibling)


def _sum_slots(p, own, chip, core, name):
    n, r, cols = p.shape
    tr = _row_tile(r)
    steps = r // tr

    def body(core_ref, chip_ref, p_ref, own_ref, o_ref):
        parts = [jnp.where(chip_ref[0] == s, own_ref[0], p_ref[s]).astype(F32) for s in range(n)]
        o_ref[...] = ((parts[0] + parts[1]) + parts[2]) + parts[3]

    return pl.pallas_call(
        body, name=name, out_shape=jax.ShapeDtypeStruct((2 * r, cols), F32),
        grid_spec=pltpu.PrefetchScalarGridSpec(
            num_scalar_prefetch=2, grid=(steps,),
            in_specs=[pl.BlockSpec((n, tr, cols), lambda i, core_ref, chip_ref: (0, i, 0)),
                      pl.BlockSpec((1, tr, cols), lambda i, core_ref, chip_ref: (chip_ref[0], i, 0))],
            out_specs=pl.BlockSpec((tr, cols), lambda i, core_ref, chip_ref: (core_ref[0] * steps + i, 0))),
        compiler_params=_params("parallel"),
    )(core.reshape(1).astype(jnp.int32), chip.reshape(1).astype(jnp.int32), p, own)


PACK_COLS = 1024


def _pack_shards(parts):
    return jnp.concatenate([p.reshape(-1, PACK_COLS) for p in parts], axis=0)


def _unpack_shards(buf, shapes):
    out, row = [], 0
    for shp in shapes:
        nrows = math.prod(shp) // PACK_COLS
        out.append(buf[..., row:row + nrows, :].reshape(buf.shape[:-2] + tuple(shp)))
        row += nrows
    return out


def _local_step(x, target, w):
    t = lambda a: a.T
    g = {}
    h0 = _rms_fwd(x, w["norm_mix_g"][0], "rms_mix0")
    w_in_g, w_in_x = w["a_w_in"][:, :D_RNN], w["a_w_in"][:, D_RNN:]
    gate_br = _matmul([(h0, w_in_g)], F32, "mm_a_gate")
    x_br = _matmul([(h0, w_in_x)], F32, "mm_a_xbr")
    y_a, hs = _rglru_fwd(gate_br, x_br, w["a_conv_w"], w["a_conv_b"], w["a_w_r"], w["a_w_i"], w["a_b_r"],
                         w["a_b_i"], w["a_lambda"], "rglru_fwd")
    x1, h1 = _matmul([(y_a, w["a_w_out"])], F32, "mm_a_out", addend=x, norm_gain=w["norm_ffn_g"][0])
    fg0, fu0, act0 = _ffn_up(h1, w["ffn_w_gate"][0], w["ffn_w_up"][0], "ffn0_up")
    x2, h2 = _matmul([(act0, w["ffn_w_down"][0])], F32, "mm_f0_down", addend=x1, norm_gain=w["norm_mix_g"][1])
    qkv = _matmul([(h2, w["b_w_qkv"])], CD, "mm_b_qkv", out_lbm=True, tn=1024)
    o = _attn_fwd(qkv, "attn_fwd")
    x3, h3 = _matmul([(o, w["b_w_out"])], F32, "mm_b_out", a_lbm=True, addend=x2, norm_gain=w["norm_ffn_g"][1])
    fg1, fu1, act1 = _ffn_up(h3, w["ffn_w_gate"][1], w["ffn_w_up"][1], "ffn1_up")
    x4 = _matmul([(act1, w["ffn_w_down"][1])], F32, "mm_f1_down", addend=x3)
    loss, dx4, dx4c, g["final_g"] = _loss_head(x4, w["final_g"], target, "loss_head")

    def ffn_bwd(dx_out, dxc, h, x_in, fg, fu, act, layer, tag):
        dg, du = _ffn_dact(dxc, t(w["ffn_w_down"][layer]), fg, fu, "ffn_" + tag + "_dact")
        dwd = _matmul([(act, dxc)], F32, "mm_" + tag + "_dwd", trans_a=True)
        dwg = _matmul([(h, dg)], F32, "mm_" + tag + "_dwg", trans_a=True)
        dwu = _matmul([(h, du)], F32, "mm_" + tag + "_dwu", trans_a=True)
        dx_in, dx_in_c, dgain = _matmul([(dg, t(w["ffn_w_gate"][layer])), (du, t(w["ffn_w_up"][layer]))], F32,
                                        "mm_" + tag + "_dh", norm_bwd=(x_in, w["norm_ffn_g"][layer], dx_out))
        return dx_in, dx_in_c, dgain, dwg, dwu, dwd

    dx3, dx3c, dgf1, dwg1, dwu1, dwd1 = ffn_bwd(dx4, dx4c, h3, x3, fg1, fu1, act1, 1, "f1")
    do = _matmul([(dx3c, t(w["b_w_out"]))], F32, "mm_b_do", out_lbm=True, tn=1024)
    g["b_w_out"] = _matmul([(o, dx3c)], F32, "mm_b_dwout", trans_a=True, a_lbm=True)
    dq, dk, dv = _attn_bwd(qkv, o, do, "attn_bwd")
    wq_t = t(w["b_w_qkv"])
    parts = (dq, dk, dv)
    g["b_w_qkv"] = jnp.concatenate(
        [_matmul([(h2, p)], F32, "mm_b_dwqkv%d" % n, trans_a=True, b_lbm=True) for n, p in enumerate(parts)], axis=1)
    dx2, dx2c, dgm1 = _matmul([(p, wq_t[n * D_MODEL:(n + 1) * D_MODEL]) for n, p in enumerate(parts)], F32, "mm_b_dh",
                              a_lbm=True, norm_bwd=(x2, w["norm_mix_g"][1], dx3))
    dx1, dx1c, dgf0, dwg0, dwu0, dwd0 = ffn_bwd(dx2, dx2c, h1, x1, fg0, fu0, act0, 0, "f0")
    dy_a = _matmul([(dx1c, t(w["a_w_out"]))], F32, "mm_a_dy")
    g["a_w_out"] = _matmul([(y_a, dx1c)], F32, "mm_a_dwout", trans_a=True)
    wrt = jnp.swapaxes(w["a_w_r"], 1, 2)
    wit = jnp.swapaxes(w["a_w_i"], 1, 2)
    (dgate, dxbr, g["a_conv_w"], g["a_conv_b"], g["a_b_r"], g["a_b_i"], g["a_lambda"], g["a_w_r"],
     g["a_w_i"]) = _rglru_bwd(dy_a, gate_br, x_br, hs, w["a_conv_w"], w["a_conv_b"], w["a_w_r"], w["a_w_i"], wrt, wit,
                              w["a_b_r"], w["a_b_i"], w["a_lambda"], "rglru_bwd")
    g["a_w_in"] = jnp.concatenate([_matmul([(h0, dgate)], F32, "mm_a_dwin_g", trans_a=True),
                                   _matmul([(h0, dxbr)], F32, "mm_a_dwin_x", trans_a=True)], axis=1)
    dx0, _, dgm0 = _matmul([(dgate, t(w_in_g)), (dxbr, t(w_in_x))], F32, "mm_a_dh",
                           norm_bwd=(x, w["norm_mix_g"][0], dx1))
    g["norm_mix_g"] = jnp.concatenate([dgm0, dgm1], axis=0)
    g["norm_ffn_g"] = jnp.concatenate([dgf0, dgf1], axis=0)
    g["ffn_w_gate"] = [dwg0, dwg1]
    g["ffn_w_up"] = [dwu0, dwu1]
    g["ffn_w_down"] = [dwd0, dwd1]
    return loss, dx0, g


WEIGHTS = ["norm_mix_g", "norm_ffn_g", "a_w_in", "a_conv_w", "a_conv_b", "a_w_r", "a_b_r", "a_w_i", "a_b_i",
           "a_lambda", "a_w_out", "b_w_qkv", "b_w_out", "ffn_w_gate", "ffn_w_up", "ffn_w_down", "final_g"]
BIG = [("a_w_in", 2), ("a_w_r", 2), ("a_w_i", 2), ("a_w_out", 1), ("b_w_qkv", 2), ("b_w_out", 1),
       ("ffn_w_gate", 2), ("ffn_w_up", 2), ("ffn_w_down", 1)]
LAYER1 = ["b_w_qkv", "b_w_out", "ffn_w_gate", "ffn_w_up", "ffn_w_down"]
LAYER0 = ["a_w_in", "a_w_r", "a_w_i", "a_w_out", "ffn_w_gate", "ffn_w_up", "ffn_w_down"]
SMALL = ["norm_mix_g", "norm_ffn_g", "a_conv_w", "a_conv_b", "a_b_r", "a_b_i", "a_lambda", "final_g"]


def _split_chips(full, axis):
    return jnp.stack(jnp.split(full, N_CHIPS, axis=axis))


def _step(x, target, weights, moments_m, moments_v):
    chip = 2 * lax.axis_index("x") + lax.axis_index("y")
    core = lax.axis_index("c")
    shard_shapes = [weights[n].shape for n, _ in BIG]
    packed = _pack_shards([weights[n].astype(CD) for n, _ in BIG])
    gathered = _allgather_chips(packed, "allgather_weights")
    full = {}
    for (n, axis), stack in zip(BIG, _unpack_shards(gathered, shard_shapes)):
        own = weights[n].astype(CD)
        joined = jnp.concatenate([jnp.where(chip == s, own, stack[s]) for s in range(N_CHIPS)], axis=axis)
        full[n] = joined[0] if joined.shape[0] == 1 else joined
    cw_rows = jnp.zeros((N_CHIPS, CONV_W, RG_BW), F32)
    cw_rows = lax.dynamic_update_slice(cw_rows, jnp.where(core == 0, weights["a_conv_w"], 0.0), (chip, 0, 0))
    cw_all = _allreduce_small(cw_rows.reshape(-1, LANES), "allgather_conv_w").reshape(N_CHIPS, CONV_W, RG_BW)
    full["a_conv_w"] = jnp.concatenate([cw_all[s] for s in range(N_CHIPS)], axis=1)
    for n in ("norm_mix_g", "norm_ffn_g", "final_g"):
        full[n] = weights[n]
    for n in ("a_conv_b", "a_b_r", "a_b_i", "a_lambda"):
        full[n] = weights[n]
    loss, dx, grads = _local_step(x[0], target[0], full)
    small_parts = [grads[n].reshape(-1) for n in SMALL] + [loss.reshape(-1)]
    sizes = [p.shape[0] for p in small_parts]
    small = _allreduce_small(jnp.concatenate(small_parts).reshape(-1, LANES), "allreduce_small").reshape(-1)
    red, pos = {}, 0
    for n, sz in zip(SMALL + ["loss"], sizes):
        red[n] = small[pos:pos + sz]
        pos += sz
    loss_out = red["loss"][0]
    g_out = {}
    for n in SMALL:
        if n == "a_conv_w":
            g_out[n] = lax.dynamic_slice(red[n].reshape(CONV_W, D_RNN), (0, chip * RG_BW), (CONV_W, RG_BW)).reshape(
                weights[n].shape)
        else:
            g_out[n] = red[n].reshape(weights[n].shape)
    axis_of = dict(BIG)
    pieces = {}
    for group, layer, tag, beside in ((LAYER1, 1, "1", True), (LAYER0, 0, "0", False)):
        stacks, shapes = [], []
        for n in group:
            per_layer = isinstance(grads[n], list)
            gfull = grads[n][layer] if per_layer else grads[n]
            shard_shape = weights[n].shape[1:]
            gfull = gfull.reshape((1,) + gfull.shape)
            stacks.append(_split_chips(gfull, axis_of[n]).reshape(N_CHIPS, -1, PACK_COLS))
            shapes.append((1,) + tuple(shard_shape))
        gbuf = jnp.concatenate(stacks, axis=1)
        from_sibling = _exchange_sibling_halves(gbuf, "rs_sibling" + tag)
        chip_partial = _add_sibling(gbuf, from_sibling, core, "rs_add" + tag)
        scatter = _scatter_to_chips_async if beside else _scatter_to_chips
        from_chips = scatter(chip_partial, "rs_chips" + tag)
        reduced = _share_halves(_sum_slots(from_chips, chip_partial, chip, core, "rs_sum" + tag), "rs_share" + tag)
        for n, piece in zip(group, _unpack_shards(reduced, shapes)):
            pieces.setdefault(n, {})[layer] = piece
    for n, _ in BIG:
        layers = pieces[n]
        g_out[n] = jnp.concatenate([layers[k] for k in sorted(layers)], axis=0)
    outs_g, outs_d, outs_m, outs_v = [], [], [], []
    for n in WEIGHTS:
        d, nm, nv = _adamw(weights[n], g_out[n], moments_m[n], moments_v[n], "adamw_" + n)
        outs_g.append(g_out[n])
        outs_d.append(d)
        outs_m.append(nm)
        outs_v.append(nv)
    return (loss_out, dx[None], *outs_g, *outs_d, *outs_m, *outs_v)


def kernel(x, norm_mix_g, norm_ffn_g, a_w_in, a_conv_w, a_conv_b, a_w_r, a_b_r, a_w_i, a_b_i, a_lambda, a_w_out, b_w_qkv, b_w_out, ffn_w_gate, ffn_w_up, ffn_w_down, final_g, loss_target, m_norm_mix_g, m_norm_ffn_g, m_a_w_in, m_a_conv_w, m_a_conv_b, m_a_w_r, m_a_b_r, m_a_w_i, m_a_b_i, m_a_lambda, m_a_w_out, m_b_w_qkv, m_b_w_out, m_ffn_w_gate, m_ffn_w_up, m_ffn_w_down, m_final_g, v_norm_mix_g, v_norm_ffn_g, v_a_w_in, v_a_conv_w, v_a_conv_b, v_a_w_r, v_a_b_r, v_a_w_i, v_a_b_i, v_a_lambda, v_a_w_out, v_b_w_qkv, v_b_w_out, v_ffn_w_gate, v_ffn_w_up, v_ffn_w_down, v_final_g):
    ws = [norm_mix_g, norm_ffn_g, a_w_in, a_conv_w, a_conv_b, a_w_r, a_b_r, a_w_i, a_b_i, a_lambda, a_w_out, b_w_qkv,
          b_w_out, ffn_w_gate, ffn_w_up, ffn_w_down, final_g]
    ms = [m_norm_mix_g, m_norm_ffn_g, m_a_w_in, m_a_conv_w, m_a_conv_b, m_a_w_r, m_a_b_r, m_a_w_i, m_a_b_i, m_a_lambda,
          m_a_w_out, m_b_w_qkv, m_b_w_out, m_ffn_w_gate, m_ffn_w_up, m_ffn_w_down, m_final_g]
    vs = [v_norm_mix_g, v_norm_ffn_g, v_a_w_in, v_a_conv_w, v_a_conv_b, v_a_w_r, v_a_b_r, v_a_w_i, v_a_b_i, v_a_lambda,
          v_a_w_out, v_b_w_qkv, v_b_w_out, v_ffn_w_gate, v_ffn_w_up, v_ffn_w_down, v_final_g]
    return _step(x, loss_target, dict(zip(WEIGHTS, ws)), dict(zip(WEIGHTS, ms)), dict(zip(WEIGHTS, vs)))
```

```python
import functools
import math

import jax
import jax.numpy as jnp
from jax import lax
from jax.experimental import pallas as pl
from jax.experimental.pallas import tpu as pltpu
from jax.experimental.pallas import tpu_sc as plsc

F32 = jnp.float32
CD = jnp.bfloat16

D_MODEL = 1024
D_RNN = 1024
RG_BLOCKS = 4
RG_BW = 256
CONV_W = 4
RG_C = 8.0
SB_HEADS = 16
SB_HEAD_DIM = 64
D_FF = 2816
RMS_EPS = 1e-6
N_CHIPS = 4
N_DEV = 8

ADAM_LR = 0.001
ADAM_B1 = 0.9
ADAM_B2 = 0.999
ADAM_EPS = 1e-08
ADAM_WD = 0.01
ADAM_STEP = 10

LANES = 128
VMEM_LIMIT = 56 * 1024 * 1024
MESH = pl.DeviceIdType.MESH


def _params(*sem):
    return pltpu.CompilerParams(dimension_semantics=sem, vmem_limit_bytes=VMEM_LIMIT)


def _pick(n, prefs):
    for p in prefs:
        if n % p == 0:
            return p
    return n


def _row_tile(rows):
    return max(d for d in range(16, 1025, 16) if rows % d == 0)


def _matmul(pairs, out_dtype, name, *, trans_a=False, a_lbm=False, b_lbm=False, out_lbm=False, addend=None,
            tm=512, tn=None, tk=None, norm_gain=None, norm_bwd=None):
    a0, b0 = pairs[0]
    if trans_a:
        kdim = a0.shape[1] if a_lbm else a0.shape[0]
        m = a0.shape[0] * LANES if a_lbm else a0.shape[1]
    else:
        m = a0.shape[1] if a_lbm else a0.shape[0]
        kdim = a0.shape[0] * LANES if a_lbm else a0.shape[1]
    n = b0.shape[0] * LANES if b_lbm else b0.shape[1]
    tm = _pick(m, (tm, 1408, 256, 128))
    tn = tn or _pick(n, (1408, 1024, 768, 512, 256, 128))
    tk = tk or _pick(kdim, (1024, 1408, 512, 256, 128))
    nk = kdim // tk
    npair = len(pairs)

    def cat(ref):
        return jnp.concatenate([ref[p] for p in range(ref.shape[0])], axis=-1)

    def body(*refs):
        ins = refs[: 2 * npair]
        pos = 2 * npair
        add_ref = None
        if addend is not None:
            add_ref = refs[pos]
            pos += 1
        gain_ref = x_ref = dxin_ref = None
        if norm_gain is not None:
            gain_ref = refs[pos]
            pos += 1
        if norm_bwd is not None:
            x_ref, gain_ref, dxin_ref = refs[pos:pos + 3]
            pos += 3
        o_ref = refs[pos]
        extra_out = refs[pos + 1:-1]
        acc_ref = refs[-1]
        k = pl.program_id(2)

        @pl.when(k == 0)
        def _():
            acc_ref[...] = jnp.zeros_like(acc_ref)

        if norm_bwd is not None:
            @pl.when((k == 0) & (pl.program_id(0) == 0))
            def _():
                extra_out[1][...] = jnp.zeros_like(extra_out[1])

        acc = acc_ref[...]
        for p in range(npair):
            a = (cat(ins[2 * p]) if a_lbm else ins[2 * p][...]).astype(CD)
            b = (cat(ins[2 * p + 1]) if b_lbm else ins[2 * p + 1][...]).astype(CD)
            dims = (((0,), (0,)), ((), ())) if trans_a else (((1,), (0,)), ((), ()))
            acc = acc + lax.dot_general(a, b, dims, preferred_element_type=F32)
        acc_ref[...] = acc

        @pl.when(k == nk - 1)
        def _():
            res = acc_ref[...]
            if add_ref is not None:
                res = res + add_ref[...]
            if norm_gain is not None:
                rinv = lax.rsqrt(jnp.mean(res * res, axis=-1, keepdims=True) + RMS_EPS)
                extra_out[0][...] = (res * rinv * gain_ref[...]).astype(CD)
            if norm_bwd is not None:
                xv = x_ref[...]
                rinv = lax.rsqrt(jnp.mean(xv * xv, axis=-1, keepdims=True) + RMS_EPS)
                nrm = xv * rinv
                dn = res * gain_ref[...]
                extra_out[1][...] += jnp.sum(res * nrm, axis=0, keepdims=True)
                res = dxin_ref[...] + rinv * (dn - nrm * jnp.mean(dn * nrm, axis=-1, keepdims=True))
                extra_out[0][...] = res.astype(CD)
            res = res.astype(out_dtype)
            if out_lbm:
                for p in range(tn // LANES):
                    o_ref[p] = res[:, p * LANES:(p + 1) * LANES]
            else:
                o_ref[...] = res

    if trans_a:
        a_spec = (pl.BlockSpec((tm // LANES, tk, LANES), lambda i, j, k: (i, k, 0)) if a_lbm
                  else pl.BlockSpec((tk, tm), lambda i, j, k: (k, i)))
    else:
        a_spec = (pl.BlockSpec((tk // LANES, tm, LANES), lambda i, j, k: (k, i, 0)) if a_lbm
                  else pl.BlockSpec((tm, tk), lambda i, j, k: (i, k)))
    b_spec = (pl.BlockSpec((tn // LANES, tk, LANES), lambda i, j, k: (j, k, 0)) if b_lbm
              else pl.BlockSpec((tk, tn), lambda i, j, k: (k, j)))
    in_specs = []
    args = []
    for a, b in pairs:
        in_specs += [a_spec, b_spec]
        args += [a, b]
    if addend is not None:
        in_specs.append(pl.BlockSpec((tm, tn), lambda i, j, k: (i, j)))
        args.append(addend)
    tile = pl.BlockSpec((tm, tn), lambda i, j, k: (i, j))
    vec = pl.BlockSpec((1, tn), lambda i, j, k: (0, j))
    if out_lbm:
        out_shape = jax.ShapeDtypeStruct((n // LANES, m, LANES), out_dtype)
        out_spec = pl.BlockSpec((tn // LANES, tm, LANES), lambda i, j, k: (j, i, 0))
    else:
        out_shape = jax.ShapeDtypeStruct((m, n), out_dtype)
        out_spec = tile
    sem = ("parallel", "parallel", "arbitrary")
    if norm_gain is not None or norm_bwd is not None:
        assert tn == n and not out_lbm, "the norm needs whole rows in one tile"
        out_shape, out_spec = [out_shape, jax.ShapeDtypeStruct((m, n), CD)], [out_spec, tile]
    if norm_gain is not None:
        in_specs.append(vec)
        args.append(norm_gain.reshape(1, n))
    if norm_bwd is not None:
        x_in, gain, dx_in = norm_bwd
        in_specs += [tile, vec, tile]
        args += [x_in, gain.reshape(1, n), dx_in]
        out_shape.append(jax.ShapeDtypeStruct((1, n), F32))
        out_spec.append(vec)
        sem = ("arbitrary", "arbitrary", "arbitrary")
    return pl.pallas_call(
        body, name=name, out_shape=out_shape, grid=(m // tm, n // tn, nk),
        in_specs=in_specs, out_specs=out_spec,
        scratch_shapes=[pltpu.VMEM((tm, tn), F32)],
        compiler_params=_params(*sem),
    )(*args)


ROW_BLOCK = 256


def _rms_fwd(x, g, name):
    s, d = x.shape

    def body(x_ref, g_ref, h_ref):
        xv = x_ref[...]
        rinv = lax.rsqrt(jnp.mean(xv * xv, axis=-1, keepdims=True) + RMS_EPS)
        h_ref[...] = (xv * rinv * g_ref[...]).astype(CD)

    return pl.pallas_call(
        body, name=name, out_shape=jax.ShapeDtypeStruct((s, d), CD), grid=(s // ROW_BLOCK,),
        in_specs=[pl.BlockSpec((ROW_BLOCK, d), lambda i: (i, 0)), pl.BlockSpec((1, d), lambda i: (0, 0))],
        out_specs=pl.BlockSpec((ROW_BLOCK, d), lambda i: (i, 0)),
        compiler_params=_params("parallel"),
    )(x, g.reshape(1, d))


def _loss_head(x, g, target, name):
    s, d = x.shape

    def body(x_ref, g_ref, t_ref, loss_ref, dx_ref, dxc_ref, dg_ref):
        @pl.when(pl.program_id(0) == 0)
        def _():
            dg_ref[...] = jnp.zeros_like(dg_ref)
            loss_ref[...] = jnp.zeros_like(loss_ref)

        xv = x_ref[...]
        gv = g_ref[...]
        rinv = lax.rsqrt(jnp.mean(xv * xv, axis=-1, keepdims=True) + RMS_EPS)
        nrm = xv * rinv
        err = nrm * gv - t_ref[...]
        loss_ref[...] += 0.5 * jnp.sum(jnp.mean(err * err, axis=-1, keepdims=True), axis=0, keepdims=True)
        dy = err * (1.0 / d)
        dn = dy * gv
        dx = rinv * (dn - nrm * jnp.mean(dn * nrm, axis=-1, keepdims=True))
        dx_ref[...] = dx
        dxc_ref[...] = dx.astype(CD)
        dg_ref[...] += jnp.sum(dy * nrm, axis=0, keepdims=True)

    row = pl.BlockSpec((ROW_BLOCK, d), lambda i: (i, 0))
    vec = pl.BlockSpec((1, d), lambda i: (0, 0))
    return pl.pallas_call(
        body, name=name,
        out_shape=(jax.ShapeDtypeStruct((1, LANES), F32), jax.ShapeDtypeStruct((s, d), F32),
                   jax.ShapeDtypeStruct((s, d), CD), jax.ShapeDtypeStruct((1, d), F32)),
        grid=(s // ROW_BLOCK,), in_specs=[row, vec, row],
        out_specs=(pl.BlockSpec((1, LANES), lambda i: (0, 0)), row, row, vec),
        compiler_params=_params("arbitrary"),
    )(x, g.reshape(1, d), target)


def _sigmoid(z):
    return 1.0 / (1.0 + jnp.exp(-z))


FFN_TM = 512
FFN_TN = 1408


def _ffn_up(h, wg, wu, name):
    s, d = h.shape
    f = wg.shape[1]
    tm = _pick(s, (FFN_TM, 256))

    def body(h_ref, wg_ref, wu_ref, g_ref, u_ref, a_ref):
        hv = h_ref[...]
        gv = jnp.dot(hv, wg_ref[...], preferred_element_type=F32)
        uv = jnp.dot(hv, wu_ref[...], preferred_element_type=F32)
        g_ref[...] = gv
        u_ref[...] = uv
        a_ref[...] = (gv * _sigmoid(gv) * uv).astype(CD)

    a_spec = pl.BlockSpec((tm, d), lambda i, j: (i, 0))
    w_spec = pl.BlockSpec((d, FFN_TN), lambda i, j: (0, j))
    o_spec = pl.BlockSpec((tm, FFN_TN), lambda i, j: (i, j))
    return pl.pallas_call(
        body, name=name,
        out_shape=(jax.ShapeDtypeStruct((s, f), F32), jax.ShapeDtypeStruct((s, f), F32),
                   jax.ShapeDtypeStruct((s, f), CD)),
        grid=(s // tm, f // FFN_TN), in_specs=[a_spec, w_spec, w_spec], out_specs=(o_spec, o_spec, o_spec),
        compiler_params=_params("parallel", "parallel"),
    )(h, wg, wu)


def _ffn_dact(dxc, wd_t, g, u, name):
    s, d = dxc.shape
    f = wd_t.shape[1]
    tm = _pick(s, (FFN_TM, 256))

    def body(dx_ref, w_ref, g_ref, u_ref, dg_ref, du_ref):
        da = jnp.dot(dx_ref[...], w_ref[...], preferred_element_type=F32)
        gv = g_ref[...]
        sg = _sigmoid(gv)
        silu = gv * sg
        dg_ref[...] = (da * u_ref[...] * (sg + silu * (1.0 - sg))).astype(CD)
        du_ref[...] = (da * silu).astype(CD)

    a_spec = pl.BlockSpec((tm, d), lambda i, j: (i, 0))
    w_spec = pl.BlockSpec((d, FFN_TN), lambda i, j: (0, j))
    o_spec = pl.BlockSpec((tm, FFN_TN), lambda i, j: (i, j))
    return pl.pallas_call(
        body, name=name,
        out_shape=(jax.ShapeDtypeStruct((s, f), CD), jax.ShapeDtypeStruct((s, f), CD)),
        grid=(s // tm, f // FFN_TN), in_specs=[a_spec, w_spec, o_spec, o_spec], out_specs=(o_spec, o_spec),
        compiler_params=_params("parallel", "parallel"),
    )(dxc, wd_t, g, u)


TIME_BLOCK = 256
SUBLANES = 8
GELU_C = math.sqrt(2.0 / math.pi)
GELU_A = 0.044715


def _gelu(x):
    return 0.5 * x * (1.0 + jnp.tanh(GELU_C * (x + GELU_A * x * x * x)))


def _gelu_grad(x):
    t = jnp.tanh(GELU_C * (x + GELU_A * x * x * x))
    return 0.5 * (1.0 + t) + 0.5 * x * (1.0 - t * t) * GELU_C * (1.0 + 3.0 * GELU_A * x * x)


def _neg_expm1(x):
    series = -x * (1.0 + x * (0.5 + x * (1.0 / 6.0 + x * (1.0 / 24.0))))
    return jnp.where(x > -0.05, series, 1.0 - jnp.exp(x))


def _log_sigmoid(x):
    return jnp.minimum(x, 0.0) - jnp.log1p(jnp.exp(-jnp.abs(x)))


def _shift_down(x, tail, s):
    if s == 0:
        return x
    ext = jnp.concatenate([tail, x], axis=0)
    return pltpu.roll(ext, s, axis=0)[SUBLANES:]


def _shift_up(x, head, s):
    if s == 0:
        return x
    n = x.shape[0]
    ext = jnp.concatenate([x, head], axis=0)
    return pltpu.roll(ext, n + SUBLANES - s, axis=0)[:n]


def _rg_gates(xbr, tail, cw_ref, cb, wr, wi, br, bi, ls):
    taps = [_shift_down(xbr, tail, CONV_W - 1 - k) for k in range(CONV_W)]
    xc = cb
    for k in range(CONV_W):
        xc = xc + cw_ref[pl.ds(k, 1), :] * taps[k]
    xcd = xc.astype(CD)
    r = _sigmoid(jnp.dot(xcd, wr, preferred_element_type=F32) + br)
    i = _sigmoid(jnp.dot(xcd, wi, preferred_element_type=F32) + bi)
    log_a = RG_C * r * ls
    a = jnp.exp(log_a)
    mult = jnp.sqrt(jnp.maximum(_neg_expm1(2.0 * log_a), 0.0))
    return taps, xc, r, i, log_a, a, mult


def _scan8_fwd(a, u):
    row = lax.broadcasted_iota(jnp.int32, a.shape, 0)
    for d in (1, 2, 4):
        a_s = pltpu.roll(a, d, axis=0)
        u_s = pltpu.roll(u, d, axis=0)
        m = row >= d
        u = jnp.where(m, a * u_s + u, u)
        a = jnp.where(m, a * a_s, a)
    return a, u


def _scan8_bwd(b, u):
    row = lax.broadcasted_iota(jnp.int32, b.shape, 0)
    for d in (1, 2, 4):
        b_s = pltpu.roll(b, SUBLANES - d, axis=0)
        u_s = pltpu.roll(u, SUBLANES - d, axis=0)
        m = row < SUBLANES - d
        u = jnp.where(m, b * u_s + u, u)
        b = jnp.where(m, b * b_s, b)
    return b, u


def _rglru_fwd(gate_br, x_br, cw, cb, wr, wi, br, bi, lam, name):
    s, c = x_br.shape
    nt = s // TIME_BLOCK
    tb, cbw = TIME_BLOCK, RG_BW
    groups = tb // SUBLANES

    def body(g_ref, x_ref, tail_ref, cw_ref, cb_ref, wr_ref, wi_ref, br_ref, bi_ref, lam_ref,
             y_ref, hs_ref, carry_ref, a_scr, u_scr):
        t = pl.program_id(1)

        @pl.when(t == 0)
        def _():
            carry_ref[...] = jnp.zeros_like(carry_ref)

        tail = jnp.where(t > 0, tail_ref[...], 0.0)
        ls = _log_sigmoid(lam_ref[...])
        _, xc, _, i, _, a, mult = _rg_gates(x_ref[...], tail, cw_ref, cb_ref[...], wr_ref[0], wi_ref[0],
                                            br_ref[...], bi_ref[...], ls)
        a_scr[...] = a
        u_scr[...] = mult * (i * xc)
        carry = carry_ref[...]
        for gi in range(groups):
            rows = pl.ds(gi * SUBLANES, SUBLANES)
            pa, hl = _scan8_fwd(a_scr[rows, :], u_scr[rows, :])
            hs_ref[rows, :] = hl + pa * carry
            carry = hs_ref[pl.ds(gi * SUBLANES + SUBLANES - 1, 1), :]
        carry_ref[...] = carry
        y_ref[...] = (hs_ref[...] * _gelu(g_ref[...])).astype(CD)

    blk = pl.BlockSpec((tb, cbw), lambda n, t: (t, n))
    tail = pl.BlockSpec((SUBLANES, cbw), lambda n, t: (jnp.maximum(t * groups - 1, 0), n))
    vec = pl.BlockSpec((1, cbw), lambda n, t: (0, n))
    wblk = pl.BlockSpec((1, cbw, cbw), lambda n, t: (n, 0, 0))
    return pl.pallas_call(
        body, name=name,
        out_shape=(jax.ShapeDtypeStruct((s, c), CD), jax.ShapeDtypeStruct((s, c), F32)),
        grid=(RG_BLOCKS, nt),
        in_specs=[blk, blk, tail, pl.BlockSpec((CONV_W, cbw), lambda n, t: (0, n)), vec, wblk, wblk, vec, vec, vec],
        out_specs=(blk, blk),
        scratch_shapes=[pltpu.VMEM((1, cbw), F32), pltpu.VMEM((tb, cbw), F32), pltpu.VMEM((tb, cbw), F32)],
        compiler_params=_params("parallel", "arbitrary"),
    )(gate_br, x_br, x_br, cw, cb, wr, wi, br, bi, lam)


def _rglru_bwd(dy, gate_br, x_br, hs, cw, cb, wr, wi, wrt, wit, br, bi, lam, name):
    s, c = x_br.shape
    nt = s // TIME_BLOCK
    tb, cbw = TIME_BLOCK, RG_BW
    groups = tb // SUBLANES

    def body(dy_ref, g_ref, x_ref, tail_ref, hs_ref, hprev_ref, cw_ref, cb_ref, wr_ref, wi_ref, wrt_ref, wit_ref,
             br_ref, bi_ref, lam_ref,
             dg_ref, dx_ref, dcw_ref, dcb_ref, dbr_ref, dbi_ref, dlam_ref, dwr_ref, dwi_ref,
             carry_ref, head_ref, b_scr, u_scr, dh_scr):
        tr = pl.program_id(1)
        first_block = tr == nt - 1

        @pl.when(tr == 0)
        def _():
            carry_ref[...] = jnp.zeros_like(carry_ref)
            head_ref[...] = jnp.zeros_like(head_ref)
            for ref in (dcw_ref, dcb_ref, dbr_ref, dbi_ref, dlam_ref, dwr_ref, dwi_ref):
                ref[...] = jnp.zeros_like(ref)

        tail = jnp.where(first_block, 0.0, tail_ref[...])
        lam_v = lam_ref[...]
        ls = _log_sigmoid(lam_v)
        taps, xc, r, i, log_a, a, mult = _rg_gates(x_ref[...], tail, cw_ref, cb_ref[...], wr_ref[0], wi_ref[0],
                                                   br_ref[...], bi_ref[...], ls)
        gate_v = g_ref[...]
        dyv = dy_ref[...]
        hsv = hs_ref[...]
        dg_ref[...] = (dyv * hsv * _gelu_grad(gate_v)).astype(CD)

        row = lax.broadcasted_iota(jnp.int32, a.shape, 0)
        b_scr[...] = jnp.where(row == tb - 1, 1.0, pltpu.roll(a, tb - 1, axis=0))
        u_scr[...] = dyv * _gelu(gate_v)
        carry = carry_ref[...]
        for gi in reversed(range(groups)):
            rows = pl.ds(gi * SUBLANES, SUBLANES)
            pb, gl = _scan8_bwd(b_scr[rows, :], u_scr[rows, :])
            dh_scr[rows, :] = gl + pb * carry
            carry = dh_scr[pl.ds(gi * SUBLANES, 1), :]
        dh = dh_scr[...]
        carry_ref[...] = carry * jnp.sum(jnp.where(row == 0, a, 0.0), axis=0, keepdims=True)

        hprev_tail = jnp.where(first_block, 0.0, hprev_ref[...])
        h_prev = _shift_down(hsv, hprev_tail, 1)
        da = dh * h_prev
        ixc = i * xc
        dmult = dh * ixc
        di = dh * mult * xc
        dxc = dh * mult * i
        a2 = a * a
        dlog_a = da * a - dmult * a2 / mult
        dpre_r = (dlog_a * (RG_C * ls)) * r * (1.0 - r)
        dpre_i = di * i * (1.0 - i)
        dlam_ref[...] += jnp.sum(dlog_a * r, axis=0, keepdims=True) * (RG_C * _sigmoid(-lam_v))
        dbr_ref[...] += jnp.sum(dpre_r, axis=0, keepdims=True)
        dbi_ref[...] += jnp.sum(dpre_i, axis=0, keepdims=True)
        xcd = xc.astype(CD)
        dprc = dpre_r.astype(CD)
        dpic = dpre_i.astype(CD)
        tn_dims = (((0,), (0,)), ((), ()))
        dwr_ref[0] += lax.dot_general(xcd, dprc, tn_dims, preferred_element_type=F32)
        dwi_ref[0] += lax.dot_general(xcd, dpic, tn_dims, preferred_element_type=F32)
        dxc = dxc + jnp.dot(dprc, wrt_ref[0], preferred_element_type=F32) + jnp.dot(dpic, wit_ref[0],
                                                                                    preferred_element_type=F32)
        dcb_ref[...] += jnp.sum(dxc, axis=0, keepdims=True)
        for k in range(CONV_W):
            dcw_ref[pl.ds(k, 1), :] += jnp.sum(dxc * taps[k], axis=0, keepdims=True)
        head = head_ref[...]
        dxb = jnp.zeros_like(dxc)
        for sft in range(CONV_W):
            dxb = dxb + cw_ref[pl.ds(CONV_W - 1 - sft, 1), :] * _shift_up(dxc, head, sft)
        dx_ref[...] = dxb.astype(CD)
        head_ref[...] = dxc[0:SUBLANES, :]

    blk = pl.BlockSpec((tb, cbw), lambda n, t: (nt - 1 - t, n))
    tail = pl.BlockSpec((SUBLANES, cbw), lambda n, t: (jnp.maximum((nt - 1 - t) * groups - 1, 0), n))
    vec = pl.BlockSpec((1, cbw), lambda n, t: (0, n))
    cwb = pl.BlockSpec((CONV_W, cbw), lambda n, t: (0, n))
    wblk = pl.BlockSpec((1, cbw, cbw), lambda n, t: (n, 0, 0))
    vshape = jax.ShapeDtypeStruct((1, c), F32)
    wshape = jax.ShapeDtypeStruct((RG_BLOCKS, cbw, cbw), F32)
    return pl.pallas_call(
        body, name=name,
        out_shape=(jax.ShapeDtypeStruct((s, c), CD), jax.ShapeDtypeStruct((s, c), CD),
                   jax.ShapeDtypeStruct((CONV_W, c), F32), vshape, vshape, vshape, vshape, wshape, wshape),
        grid=(RG_BLOCKS, nt),
        in_specs=[blk, blk, blk, tail, blk, tail, cwb, vec, wblk, wblk, wblk, wblk, vec, vec, vec],
        out_specs=(blk, blk, cwb, vec, vec, vec, vec, wblk, wblk),
        scratch_shapes=[pltpu.VMEM((1, cbw), F32), pltpu.VMEM((SUBLANES, cbw), F32),
                        pltpu.VMEM((tb, cbw), F32), pltpu.VMEM((tb, cbw), F32), pltpu.VMEM((tb, cbw), F32)],
        compiler_params=_params("parallel", "arbitrary"),
    )(dy, gate_br, x_br, x_br, hs, hs, cw, cb, wr, wi, wrt, wit, br, bi, lam)


ATT_BLOCK = 256
ATT_Q_BLOCK = 1024
ATT_RATIO = ATT_Q_BLOCK // ATT_BLOCK
ATT_SCALE = 1.0 / math.sqrt(SB_HEAD_DIM)
N_PAIRS = SB_HEADS * SB_HEAD_DIM // LANES
NT_DIMS = (((1,), (1,)), ((), ()))
TN_DIMS = (((0,), (0,)), ((), ()))


LOG2E = 1.4426950408889634


def _neg_abs(x):
    bits = lax.bitcast_convert_type(x, jnp.uint32) | jnp.uint32(0x80000000)
    return lax.bitcast_convert_type(bits, F32)


def _qk(qx, kb):
    return lax.dot_general(qx, kb, NT_DIMS, preferred_element_type=F32)


def _sb_logits(qk, valid):
    z2 = qk * (ATT_SCALE * LOG2E)
    lb2 = jnp.minimum(z2, 0.0) - jnp.log2(1.0 + jnp.exp2(_neg_abs(z2)))
    l2 = lb2 - z2
    if valid is not None:
        l2 = jnp.where(valid, l2, 0.0)
    return lb2, l2


def _hi_lo(x):
    hi = x.astype(CD)
    lo = (x - hi.astype(F32)).astype(CD)
    return jnp.concatenate([hi, lo], axis=1)


def _tri(strict, stacked):
    r = lax.broadcasted_iota(jnp.int32, (ATT_BLOCK, ATT_BLOCK), 0)
    c = lax.broadcasted_iota(jnp.int32, (ATT_BLOCK, ATT_BLOCK), 1)
    m = (r > c if strict else r >= c).astype(CD)
    return jnp.concatenate([m, m], axis=0) if stacked else m


def _attn_fwd(qkv, name):
    _, s, _ = qkv.shape
    tq, t = ATT_Q_BLOCK, ATT_BLOCK
    nblk = s // tq

    def body(q_ref, k_ref, v_ref, o_ref, qk_scr, w_scr):
        i = pl.program_id(1)
        lane = lax.broadcasted_iota(jnp.int32, (1, LANES), 1)
        head_masks = (lane < SB_HEAD_DIM, lane >= SB_HEAD_DIM)
        q = q_ref[0]
        qs = [jnp.where(m, q, jnp.zeros_like(q)) for m in head_masks]
        tri = _tri(True, False)
        rr = lax.broadcasted_iota(jnp.int32, (tq, t), 0)
        cc = lax.broadcasted_iota(jnp.int32, (tq, t), 1)

        def rows_of(j):
            return pl.ds(pl.multiple_of(j * t, t), t)

        def tail(x, row0):
            return x if row0 == 0 else x[row0:]

        def start_logits(j, row0=0):
            kb = k_ref[0, rows_of(j), :]
            for hd in range(2):
                qk_scr[hd, row0:, :] = _qk(tail(qs[hd], row0), kb)

        def weights(run, diagonal=False, row0=0):
            new_run = []
            valid = (cc < rr)[:tq - row0] if diagonal else None
            for hd in range(2):
                lb2, l2 = _sb_logits(qk_scr[hd, row0:, :], valid)
                w = jnp.exp2(lb2 + (tail(run[hd], row0) + jnp.dot(l2.astype(CD), tri, preferred_element_type=F32)))
                if valid is not None:
                    w = jnp.where(valid, w, 0.0)
                w_scr[row0:, hd * t:(hd + 1) * t] = w.astype(CD)
                rowsum = jnp.sum(l2, axis=1, keepdims=True)
                if row0:
                    rowsum = jnp.concatenate([jnp.zeros((row0, 1), F32), rowsum], axis=0)
                new_run.append(run[hd] + rowsum)
            return tuple(new_run)

        def apply_weights(j, row0=0):
            vb = v_ref[0, rows_of(j), :]
            vcat = jnp.concatenate([jnp.where(m, vb, jnp.zeros_like(vb)) for m in head_masks], axis=0)
            inc = jnp.dot(w_scr[row0:, :], vcat, preferred_element_type=F32)
            return inc if row0 == 0 else jnp.concatenate([jnp.zeros((row0, LANES), F32), inc], axis=0)

        zero = jnp.zeros((tq, 1), F32)
        last = ATT_RATIO - 1
        start_logits(ATT_RATIO * i + last, last * t)
        run = weights((zero, zero), True, last * t)
        oacc = jnp.zeros((tq, LANES), F32)
        for d in reversed(range(last)):
            start_logits(ATT_RATIO * i + d, d * t)
            oacc = oacc + apply_weights(ATT_RATIO * i + d + 1, (d + 1) * t)
            run = weights(run, True, d * t)
        start_logits(jnp.maximum(ATT_RATIO * i - 1, 0))

        def step(jj, carry):
            run, oacc = carry
            b = ATT_RATIO * i - 1 - jj
            oacc = oacc + apply_weights(b + 1)
            run = weights(run)
            start_logits(jnp.maximum(b - 1, 0))
            return run, oacc

        run, oacc = lax.fori_loop(0, ATT_RATIO * i, step, (run, oacc))
        o_ref[0] = oacc + apply_weights(0)

    return pl.pallas_call(
        body, name=name, out_shape=jax.ShapeDtypeStruct((N_PAIRS, s, LANES), F32), grid=(N_PAIRS, nblk),
        in_specs=[pl.BlockSpec((1, tq, LANES), lambda p, i: (p, i, 0)),
                  pl.BlockSpec((1, s, LANES), lambda p, i: (N_PAIRS + p, 0, 0)),
                  pl.BlockSpec((1, s, LANES), lambda p, i: (2 * N_PAIRS + p, 0, 0))],
        out_specs=pl.BlockSpec((1, tq, LANES), lambda p, i: (p, i, 0)),
        scratch_shapes=[pltpu.VMEM((2, tq, t), F32), pltpu.VMEM((tq, 2 * t), CD)],
        compiler_params=_params("parallel", "arbitrary"),
    )(qkv, qkv, qkv)


def _attn_bwd(qkv, o, do, name):
    _, s, _ = qkv.shape
    tq, t = ATT_Q_BLOCK, ATT_BLOCK
    nblk = s // tq

    def body(q_ref, k_ref, v_ref, o_ref, do_ref, dq_ref, dk_ref, dv_ref, qk_scr, dw_scr, w_scr, dz_scr):
        i = pl.program_id(1)

        @pl.when(i == 0)
        def _():
            dk_ref[...] = jnp.zeros_like(dk_ref)
            dv_ref[...] = jnp.zeros_like(dv_ref)

        lane = lax.broadcasted_iota(jnp.int32, (1, LANES), 1)
        head_masks = (lane < SB_HEAD_DIM, lane >= SB_HEAD_DIM)
        q = q_ref[0]
        dov = do_ref[0]
        ov = o_ref[0]
        qs = [jnp.where(m, q, jnp.zeros_like(q)) for m in head_masks]
        q_scaled_t = jnp.concatenate([(qx.astype(F32) * ATT_SCALE).T for qx in qs], axis=1).astype(CD)
        docs = [jnp.where(m, dov, 0.0).astype(CD) for m in head_masks]
        docat_t = jnp.concatenate([jnp.where(m, dov, 0.0).T for m in head_masks], axis=1).astype(CD)
        totals = [jnp.sum(d.astype(F32) * ov, axis=1, keepdims=True) for d in docs]
        tri = _tri(True, False)
        tri_incl = _tri(False, True)
        rr = lax.broadcasted_iota(jnp.int32, (tq, t), 0)
        cc = lax.broadcasted_iota(jnp.int32, (tq, t), 1)

        def rows_of(j):
            return pl.ds(pl.multiple_of(j * t, t), t)

        def tail(x, row0):
            return x if row0 == 0 else x[row0:]

        def pad_rows(x, row0):
            return x if row0 == 0 else jnp.concatenate([jnp.zeros((row0, x.shape[1]), x.dtype), x], axis=0)

        def start_products(j, row0=0):
            kb = k_ref[0, rows_of(j), :]
            vb = v_ref[0, rows_of(j), :]
            for hd in range(2):
                qk_scr[hd, row0:, :] = _qk(tail(qs[hd], row0), kb)
                dw_scr[hd, row0:, :] = lax.dot_general(tail(docs[hd], row0), vb, NT_DIMS, preferred_element_type=F32)

        def logit_grads(run, erun, diagonal=False, row0=0):
            new_run, new_erun = [], []
            valid = (cc < rr)[:tq - row0] if diagonal else None
            for hd in range(2):
                lb2, l2 = _sb_logits(qk_scr[hd, row0:, :], valid)
                w = jnp.exp2(lb2 + (tail(run[hd], row0) + jnp.dot(l2.astype(CD), tri, preferred_element_type=F32)))
                if valid is not None:
                    w = jnp.where(valid, w, 0.0)
                wc = w.astype(CD)
                w_scr[hd * tq + row0:(hd + 1) * tq, :] = wc
                e = dw_scr[hd, row0:, :] * wc.astype(F32)
                prefix = (tail(totals[hd] - erun[hd], row0)
                          - jnp.dot(_hi_lo(e), tri_incl, preferred_element_type=F32))
                dz = e - jnp.exp2(lb2) * (e + prefix)
                if valid is not None:
                    dz = jnp.where(valid, dz, 0.0)
                dz_scr[hd * tq + row0:(hd + 1) * tq, :] = dz.astype(CD)
                new_run.append(run[hd] + pad_rows(jnp.sum(l2, axis=1, keepdims=True), row0))
                new_erun.append(erun[hd] + pad_rows(jnp.sum(e, axis=1, keepdims=True), row0))
            return tuple(new_run), tuple(new_erun)

        def apply_grads(j, row0=0):
            rows = rows_of(j)
            kb = k_ref[0, rows, :]
            kcat = jnp.concatenate([jnp.where(m, kb, jnp.zeros_like(kb)) for m in head_masks], axis=0)
            dz_heads = [dz_scr[hd * tq + row0:(hd + 1) * tq, :] for hd in range(2)]
            w_heads = [w_scr[hd * tq + row0:(hd + 1) * tq, :] for hd in range(2)]
            q_t = jnp.concatenate([q_scaled_t[:, hd * tq + row0:(hd + 1) * tq] for hd in range(2)], axis=1)
            do_t = jnp.concatenate([docat_t[:, hd * tq + row0:(hd + 1) * tq] for hd in range(2)], axis=1)
            dk_ref[0, :, rows] += jnp.dot(q_t, jnp.concatenate(dz_heads, axis=0), preferred_element_type=F32)
            dv_ref[0, :, rows] += jnp.dot(do_t, jnp.concatenate(w_heads, axis=0), preferred_element_type=F32)
            return pad_rows(jnp.dot(jnp.concatenate(dz_heads, axis=1), kcat, preferred_element_type=F32), row0)

        zero = jnp.zeros((tq, 1), F32)
        last = ATT_RATIO - 1
        start_products(ATT_RATIO * i + last, last * t)
        run, erun = logit_grads((zero, zero), (zero, zero), True, last * t)
        dqacc = jnp.zeros((tq, LANES), F32)
        for d in reversed(range(last)):
            start_products(ATT_RATIO * i + d, d * t)
            dqacc = dqacc + apply_grads(ATT_RATIO * i + d + 1, (d + 1) * t)
            run, erun = logit_grads(run, erun, True, d * t)
        start_products(jnp.maximum(ATT_RATIO * i - 1, 0))

        def step(jj, carry):
            run, erun, dqacc = carry
            b = ATT_RATIO * i - 1 - jj
            dqacc = dqacc + apply_grads(b + 1)
            run, erun = logit_grads(run, erun)
            start_products(jnp.maximum(b - 1, 0))
            return run, erun, dqacc

        run, erun, dqacc = lax.fori_loop(0, ATT_RATIO * i, step, (run, erun, dqacc))
        dq_ref[0] = (dqacc + apply_grads(0)) * ATT_SCALE

    qblk = pl.BlockSpec((1, tq, LANES), lambda p, i: (p, i, 0))
    full = pl.BlockSpec((1, LANES, s), lambda p, i: (p, 0, 0))
    shape = jax.ShapeDtypeStruct((N_PAIRS, s, LANES), F32)
    shape_t = jax.ShapeDtypeStruct((N_PAIRS, LANES, s), F32)
    dq, dk_t, dv_t = pl.pallas_call(
        body, name=name, out_shape=(shape, shape_t, shape_t), grid=(N_PAIRS, nblk),
        in_specs=[qblk,
                  pl.BlockSpec((1, s, LANES), lambda p, i: (N_PAIRS + p, 0, 0)),
                  pl.BlockSpec((1, s, LANES), lambda p, i: (2 * N_PAIRS + p, 0, 0)),
                  qblk, qblk],
        out_specs=(qblk, full, full),
        scratch_shapes=[pltpu.VMEM((2, tq, t), F32), pltpu.VMEM((2, tq, t), F32),
                        pltpu.VMEM((2 * tq, t), CD), pltpu.VMEM((2 * tq, t), CD)],
        compiler_params=_params("parallel", "arbitrary"),
    )(qkv, qkv, qkv, o, do)
    return dq, jnp.swapaxes(dk_t, 1, 2), jnp.swapaxes(dv_t, 1, 2)


def _adamw(w, g, m, v, name):
    shape = w.shape
    rows, cols = (shape[-2], shape[-1]) if len(shape) >= 2 else (1, shape[-1])
    lead = w.size // (rows * cols)
    tr = _pick(rows, (512, 256, 128, 64, 32, 16, 8))

    def body(w_ref, g_ref, m_ref, v_ref, d_ref, nm_ref, nv_ref):
        gv = g_ref[...]
        nm = ADAM_B1 * m_ref[...] + (1.0 - ADAM_B1) * gv
        nv = ADAM_B2 * v_ref[...] + (1.0 - ADAM_B2) * (gv * gv)
        m_hat = nm / (1.0 - ADAM_B1 ** ADAM_STEP)
        v_hat = nv / (1.0 - ADAM_B2 ** ADAM_STEP)
        d_ref[...] = -ADAM_LR * (m_hat / (jnp.sqrt(v_hat) + ADAM_EPS) + ADAM_WD * w_ref[...])
        nm_ref[...] = nm
        nv_ref[...] = nv

    blk = pl.BlockSpec((1, tr, cols), lambda l, i: (l, i, 0))
    out = jax.ShapeDtypeStruct((lead, rows, cols), F32)
    d, nm, nv = pl.pallas_call(
        body, name=name, out_shape=(out, out, out), grid=(lead, rows // tr),
        in_specs=[blk, blk, blk, blk], out_specs=(blk, blk, blk), compiler_params=_params("parallel", "parallel"),
    )(*[a.reshape(lead, rows, cols) for a in (w, g, m, v)])
    return d.reshape(shape), nm.reshape(shape), nv.reshape(shape)


HBM = pl.BlockSpec(memory_space=pltpu.HBM)


def _coords():
    return lax.axis_index("x"), lax.axis_index("y"), lax.axis_index("c")


def _other_chips(x, y):
    return [(1 - x, y), (x, 1 - y), (1 - x, 1 - y)]


def _allgather_chips(shard, name):
    r, cols = shard.shape
    half = r // 2
    quarter = half // 2

    def body(src_ref, out_ref, send_sems, recv_sems):
        x, y, c = _coords()
        sibling = (x, y, 1 - c)
        nx, ny, diag = (1 - x, y), (x, 1 - y), (1 - x, 1 - y)

        def piece(chip, core, lo, n):
            return out_ref.at[2 * chip[0] + chip[1], pl.ds(core * half + lo, n), :]

        def copy(k, dst, to, src=None):
            return pltpu.make_async_remote_copy(
                src_ref=dst if src is None else src, dst_ref=dst,
                send_sem=send_sems.at[k], recv_sem=recv_sems.at[k], device_id=to, device_id_type=MESH)

        me = (x, y)
        mine = src_ref.at[pl.ds(c * half, half), :]
        direct = [copy(0, piece(me, c, 0, half), (*nx, c), src=mine), copy(1, piece(me, c, 0, half), (*ny, c), src=mine)]
        for cp in direct:
            cp.start()
        arrivals = [piece(nx, c, 0, half), piece(ny, c, 0, half), piece(diag, c, 0, quarter),
                    piece(diag, c, quarter, quarter)]
        onward = [copy(2, piece(nx, c, 0, quarter), (*ny, c)), copy(3, piece(ny, c, quarter, quarter), (*nx, c))]
        to_sibling = [copy(4 + k, dst, sibling) for k, dst in enumerate(arrivals)]
        for k, dst in enumerate(arrivals):
            copy(k, dst, (x, y, c)).wait_recv()
            if k < 2:
                onward[k].start()
            to_sibling[k].start()
        from_sibling = [piece(nx, 1 - c, 0, half), piece(ny, 1 - c, 0, half), piece(diag, 1 - c, 0, quarter),
                        piece(diag, 1 - c, quarter, quarter)]
        for k, dst in enumerate(from_sibling):
            copy(4 + k, dst, (x, y, c)).wait_recv()
        for cp in direct + onward + to_sibling:
            cp.wait_send()

    return pl.pallas_call(
        body, name=name, out_shape=jax.ShapeDtypeStruct((N_CHIPS, r, cols), shard.dtype),
        in_specs=[HBM], out_specs=HBM,
        scratch_shapes=[pltpu.SemaphoreType.DMA((8,)), pltpu.SemaphoreType.DMA((8,))],
    )(shard)


def _exchange_sibling_halves(g, name):
    n, r, cols = g.shape
    half = r // 2

    def body(g_ref, out_ref, send_sem, recv_sem):
        x, y, c = _coords()
        cp = pltpu.make_async_remote_copy(
            src_ref=g_ref.at[:, pl.ds((1 - c) * half, half), :], dst_ref=out_ref,
            send_sem=send_sem, recv_sem=recv_sem, device_id=(x, y, 1 - c), device_id_type=MESH)
        cp.start()
        cp.wait()

    return pl.pallas_call(
        body, name=name, out_shape=jax.ShapeDtypeStruct((n, half, cols), g.dtype),
        in_specs=[HBM], out_specs=HBM,
        scratch_shapes=[pltpu.SemaphoreType.DMA, pltpu.SemaphoreType.DMA],
    )(g)


def _scatter_to_chips(p, name):
    n, h, cols = p.shape

    def body(p_ref, out_ref, send_sems, recv_sems):
        x, y, c = _coords()
        me = 2 * x + y
        sends = []
        for j, (px, py) in enumerate(_other_chips(x, y)):
            sends.append(pltpu.make_async_remote_copy(
                src_ref=p_ref.at[2 * px + py], dst_ref=out_ref.at[me],
                send_sem=send_sems.at[j], recv_sem=recv_sems.at[j], device_id=(px, py, c), device_id_type=MESH))
        for cp in sends:
            cp.start()
        for j, (px, py) in enumerate(_other_chips(x, y)):
            pltpu.make_async_remote_copy(
                src_ref=p_ref.at[me], dst_ref=out_ref.at[2 * px + py],
                send_sem=send_sems.at[j], recv_sem=recv_sems.at[j], device_id=(px, py, c),
                device_id_type=MESH).wait_recv()
        for cp in sends:
            cp.wait_send()

    return pl.pallas_call(
        body, name=name, out_shape=jax.ShapeDtypeStruct((n, h, cols), p.dtype),
        in_specs=[HBM], out_specs=HBM,
        scratch_shapes=[pltpu.SemaphoreType.DMA((3,)), pltpu.SemaphoreType.DMA((3,))],
    )(p)


def _sequencer(name, collective_id, scratch_types):
    return pl.kernel(mesh=plsc.ScalarSubcoreMesh(axis_name="sequencer", num_cores=1), name=name,
                     scratch_types=scratch_types, compiler_params=pltpu.CompilerParams(collective_id=collective_id))


def _handshake(peers):
    barrier = pltpu.get_barrier_semaphore()
    for peer in peers:
        pl.semaphore_signal(barrier, inc=1, device_id=peer, device_id_type=MESH)
    pl.semaphore_wait(barrier, len(peers))


def _exchange_sibling_halves_async(g, name, collective_id):
    n, r, cols = g.shape
    half = r // 2
    g_ref = jax.new_ref(g, memory_space=pltpu.MemorySpace.HBM)
    out_ref = jax.empty_ref(jax.ShapeDtypeStruct((n, half, cols), g.dtype), memory_space=pltpu.MemorySpace.HBM)

    @_sequencer(name, collective_id, (pltpu.SemaphoreType.DMA, pltpu.SemaphoreType.DMA))
    def launch(send_sem, recv_sem):
        x, y, c = _coords()
        _handshake([(x, y, 1 - c)])
        cp = pltpu.make_async_remote_copy(
            src_ref=g_ref.at[:, pl.ds((1 - c) * half, half), :], dst_ref=out_ref,
            send_sem=send_sem, recv_sem=recv_sem, device_id=(x, y, 1 - c), device_id_type=MESH)
        cp.start()
        cp.wait()

    launch()
    return out_ref[...]


def _share_halves_async(v, name, collective_id):
    h = v.shape[0] // 2
    v_ref = jax.new_ref(v, memory_space=pltpu.MemorySpace.HBM)

    @_sequencer(name, collective_id, (pltpu.SemaphoreType.DMA, pltpu.SemaphoreType.DMA))
    def launch(send_sem, recv_sem):
        x, y, c = _coords()
        _handshake([(x, y, 1 - c)])
        cp = pltpu.make_async_remote_copy(
            src_ref=v_ref.at[pl.ds(c * h, h), :], dst_ref=v_ref.at[pl.ds(c * h, h), :],
            send_sem=send_sem, recv_sem=recv_sem, device_id=(x, y, 1 - c), device_id_type=MESH)
        cp.start()
        pltpu.make_async_remote_copy(
            src_ref=v_ref.at[pl.ds(c * h, h), :], dst_ref=v_ref.at[pl.ds((1 - c) * h, h), :],
            send_sem=send_sem, recv_sem=recv_sem, device_id=(x, y, 1 - c), device_id_type=MESH).wait_recv()
        cp.wait_send()

    launch()
    return v_ref[...]


def _scatter_to_chips_async(p, name, collective_id):
    p_ref = jax.new_ref(p, memory_space=pltpu.MemorySpace.HBM)
    out_ref = jax.empty_ref(jax.ShapeDtypeStruct(p.shape, p.dtype), memory_space=pltpu.MemorySpace.HBM)

    @_sequencer(name, collective_id, (pltpu.SemaphoreType.DMA((3,)), pltpu.SemaphoreType.DMA((3,))))
    def launch(send_sems, recv_sems):
        x, y, c = _coords()
        me = 2 * x + y
        _handshake([(px, py, c) for px, py in _other_chips(x, y)])
        sends = []
        for j, (px, py) in enumerate(_other_chips(x, y)):
            sends.append(pltpu.make_async_remote_copy(
                src_ref=p_ref.at[2 * px + py], dst_ref=out_ref.at[me],
                send_sem=send_sems.at[j], recv_sem=recv_sems.at[j], device_id=(px, py, c), device_id_type=MESH))
        for cp in sends:
            cp.start()
        for j, (px, py) in enumerate(_other_chips(x, y)):
            pltpu.make_async_remote_copy(
                src_ref=p_ref.at[me], dst_ref=out_ref.at[2 * px + py],
                send_sem=send_sems.at[j], recv_sem=recv_sems.at[j], device_id=(px, py, c),
                device_id_type=MESH).wait_recv()
        for cp in sends:
            cp.wait_send()

    launch()
    return out_ref[...]


def _share_halves(v, name):
    h = v.shape[0] // 2

    def body(v_ref, out_ref, send_sem, recv_sem):
        x, y, c = _coords()
        cp = pltpu.make_async_remote_copy(
            src_ref=v_ref.at[pl.ds(c * h, h), :], dst_ref=out_ref.at[pl.ds(c * h, h), :],
            send_sem=send_sem, recv_sem=recv_sem, device_id=(x, y, 1 - c), device_id_type=MESH)
        cp.start()
        pltpu.make_async_remote_copy(
            src_ref=v_ref.at[pl.ds(c * h, h), :], dst_ref=out_ref.at[pl.ds((1 - c) * h, h), :],
            send_sem=send_sem, recv_sem=recv_sem, device_id=(x, y, 1 - c), device_id_type=MESH).wait_recv()
        cp.wait_send()

    return pl.pallas_call(
        body, name=name, out_shape=jax.ShapeDtypeStruct(v.shape, v.dtype),
        in_specs=[HBM], out_specs=HBM, input_output_aliases={0: 0},
        scratch_shapes=[pltpu.SemaphoreType.DMA, pltpu.SemaphoreType.DMA],
    )(v)


def _allreduce_small(v, name):
    r, cols = v.shape

    def body(v_ref, out_ref, buf_ref, send_sems, recv_sems):
        x, y, c = _coords()
        me = 4 * x + 2 * y + c
        buf_ref[me] = v_ref[...]
        sends = []
        for k in range(1, N_DEV):
            px = 1 - x if k & 4 else x
            py = 1 - y if k & 2 else y
            pc = 1 - c if k & 1 else c
            sends.append(pltpu.make_async_remote_copy(
                src_ref=v_ref, dst_ref=buf_ref.at[me], send_sem=send_sems.at[k - 1], recv_sem=recv_sems.at[k - 1],
                device_id=(px, py, pc), device_id_type=MESH))
        for cp in sends:
            cp.start()
        for cp in sends:
            cp.wait()
        acc = buf_ref[0]
        for d in range(1, N_DEV):
            acc = acc + buf_ref[d]
        out_ref[...] = acc

    return pl.pallas_call(
        body, name=name, out_shape=jax.ShapeDtypeStruct((r, cols), F32),
        in_specs=[pl.BlockSpec(memory_space=pltpu.VMEM)], out_specs=pl.BlockSpec(memory_space=pltpu.VMEM),
        scratch_shapes=[pltpu.VMEM((N_DEV, r, cols), F32), pltpu.SemaphoreType.DMA((N_DEV - 1,)),
                        pltpu.SemaphoreType.DMA((N_DEV - 1,))],
    )(v)


def _add_sibling(g, from_sibling, core, name):
    n, h, cols = from_sibling.shape
    tr = _row_tile(h)
    steps = h // tr

    def body(core_ref, a_ref, b_ref, o_ref):
        o_ref[...] = (a_ref[...] + b_ref[...]).astype(o_ref.dtype)

    return pl.pallas_call(
        body, name=name, out_shape=jax.ShapeDtypeStruct(from_sibling.shape, jnp.bfloat16),
        grid_spec=pltpu.PrefetchScalarGridSpec(
            num_scalar_prefetch=1, grid=(n, steps),
            in_specs=[pl.BlockSpec((1, tr, cols), lambda s, i, core_ref: (s, core_ref[0] * steps + i, 0)),
                      pl.BlockSpec((1, tr, cols), lambda s, i, core_ref: (s, i, 0))],
            out_specs=pl.BlockSpec((1, tr, cols), lambda s, i, core_ref: (s, i, 0))),
        compiler_params=_params("parallel", "parallel"),
    )(core.reshape(1).astype(jnp.int32), g, from_sibling)


def _sum_slots(p, own, chip, core, name):
    n, r, cols = p.shape
    tr = _row_tile(r)
    steps = r // tr

    def body(core_ref, chip_ref, p_ref, own_ref, o_ref):
        parts = [jnp.where(chip_ref[0] == s, own_ref[0], p_ref[s]).astype(F32) for s in range(n)]
        o_ref[...] = ((parts[0] + parts[1]) + parts[2]) + parts[3]

    return pl.pallas_call(
        body, name=name, out_shape=jax.ShapeDtypeStruct((2 * r, cols), F32),
        grid_spec=pltpu.PrefetchScalarGridSpec(
            num_scalar_prefetch=2, grid=(steps,),
            in_specs=[pl.BlockSpec((n, tr, cols), lambda i, core_ref, chip_ref: (0, i, 0)),
                      pl.BlockSpec((1, tr, cols), lambda i, core_ref, chip_ref: (chip_ref[0], i, 0))],
            out_specs=pl.BlockSpec((tr, cols), lambda i, core_ref, chip_ref: (core_ref[0] * steps + i, 0))),
        compiler_params=_params("parallel"),
    )(core.reshape(1).astype(jnp.int32), chip.reshape(1).astype(jnp.int32), p, own)


PACK_COLS = 1024


def _pack_shards(parts):
    return jnp.concatenate([p.reshape(-1, PACK_COLS) for p in parts], axis=0)


def _unpack_shards(buf, shapes):
    out, row = [], 0
    for shp in shapes:
        nrows = math.prod(shp) // PACK_COLS
        out.append(buf[..., row:row + nrows, :].reshape(buf.shape[:-2] + tuple(shp)))
        row += nrows
    return out


def _local_step(x, target, w):
    t = lambda a: a.T
    g = {}
    h0 = _rms_fwd(x, w["norm_mix_g"][0], "rms_mix0")
    w_in_g, w_in_x = w["a_w_in"][:, :D_RNN], w["a_w_in"][:, D_RNN:]
    gate_br = _matmul([(h0, w_in_g)], F32, "mm_a_gate")
    x_br = _matmul([(h0, w_in_x)], F32, "mm_a_xbr")
    y_a, hs = _rglru_fwd(gate_br, x_br, w["a_conv_w"], w["a_conv_b"], w["a_w_r"], w["a_w_i"], w["a_b_r"],
                         w["a_b_i"], w["a_lambda"], "rglru_fwd")
    x1, h1 = _matmul([(y_a, w["a_w_out"])], F32, "mm_a_out", addend=x, norm_gain=w["norm_ffn_g"][0])
    fg0, fu0, act0 = _ffn_up(h1, w["ffn_w_gate"][0], w["ffn_w_up"][0], "ffn0_up")
    x2, h2 = _matmul([(act0, w["ffn_w_down"][0])], F32, "mm_f0_down", addend=x1, norm_gain=w["norm_mix_g"][1])
    qkv = _matmul([(h2, w["b_w_qkv"])], CD, "mm_b_qkv", out_lbm=True, tn=1024)
    o = _attn_fwd(qkv, "attn_fwd")
    x3, h3 = _matmul([(o, w["b_w_out"])], F32, "mm_b_out", a_lbm=True, addend=x2, norm_gain=w["norm_ffn_g"][1])
    fg1, fu1, act1 = _ffn_up(h3, w["ffn_w_gate"][1], w["ffn_w_up"][1], "ffn1_up")
    x4 = _matmul([(act1, w["ffn_w_down"][1])], F32, "mm_f1_down", addend=x3)
    loss, dx4, dx4c, g["final_g"] = _loss_head(x4, w["final_g"], target, "loss_head")

    def ffn_bwd(dx_out, dxc, h, x_in, fg, fu, act, layer, tag):
        dg, du = _ffn_dact(dxc, t(w["ffn_w_down"][layer]), fg, fu, "ffn_" + tag + "_dact")
        dwd = _matmul([(act, dxc)], F32, "mm_" + tag + "_dwd", trans_a=True)
        dwg = _matmul([(h, dg)], F32, "mm_" + tag + "_dwg", trans_a=True)
        dwu = _matmul([(h, du)], F32, "mm_" + tag + "_dwu", trans_a=True)
        dx_in, dx_in_c, dgain = _matmul([(dg, t(w["ffn_w_gate"][layer])), (du, t(w["ffn_w_up"][layer]))], F32,
                                        "mm_" + tag + "_dh", norm_bwd=(x_in, w["norm_ffn_g"][layer], dx_out))
        return dx_in, dx_in_c, dgain, dwg, dwu, dwd

    dx3, dx3c, dgf1, dwg1, dwu1, dwd1 = ffn_bwd(dx4, dx4c, h3, x3, fg1, fu1, act1, 1, "f1")
    do = _matmul([(dx3c, t(w["b_w_out"]))], F32, "mm_b_do", out_lbm=True, tn=1024)
    g["b_w_out"] = _matmul([(o, dx3c)], F32, "mm_b_dwout", trans_a=True, a_lbm=True)
    dq, dk, dv = _attn_bwd(qkv, o, do, "attn_bwd")
    wq_t = t(w["b_w_qkv"])
    parts = (dq, dk, dv)
    g["b_w_qkv"] = jnp.concatenate(
        [_matmul([(h2, p)], F32, "mm_b_dwqkv%d" % n, trans_a=True, b_lbm=True) for n, p in enumerate(parts)], axis=1)
    dx2, dx2c, dgm1 = _matmul([(p, wq_t[n * D_MODEL:(n + 1) * D_MODEL]) for n, p in enumerate(parts)], F32, "mm_b_dh",
                              a_lbm=True, norm_bwd=(x2, w["norm_mix_g"][1], dx3))
    dx1, dx1c, dgf0, dwg0, dwu0, dwd0 = ffn_bwd(dx2, dx2c, h1, x1, fg0, fu0, act0, 0, "f0")
    dy_a = _matmul([(dx1c, t(w["a_w_out"]))], F32, "mm_a_dy")
    g["a_w_out"] = _matmul([(y_a, dx1c)], F32, "mm_a_dwout", trans_a=True)
    wrt = jnp.swapaxes(w["a_w_r"], 1, 2)
    wit = jnp.swapaxes(w["a_w_i"], 1, 2)
    (dgate, dxbr, g["a_conv_w"], g["a_conv_b"], g["a_b_r"], g["a_b_i"], g["a_lambda"], g["a_w_r"],
     g["a_w_i"]) = _rglru_bwd(dy_a, gate_br, x_br, hs, w["a_conv_w"], w["a_conv_b"], w["a_w_r"], w["a_w_i"], wrt, wit,
                              w["a_b_r"], w["a_b_i"], w["a_lambda"], "rglru_bwd")
    g["a_w_in"] = jnp.concatenate([_matmul([(h0, dgate)], F32, "mm_a_dwin_g", trans_a=True),
                                   _matmul([(h0, dxbr)], F32, "mm_a_dwin_x", trans_a=True)], axis=1)
    dx0, _, dgm0 = _matmul([(dgate, t(w_in_g)), (dxbr, t(w_in_x))], F32, "mm_a_dh",
                           norm_bwd=(x, w["norm_mix_g"][0], dx1))
    g["norm_mix_g"] = jnp.concatenate([dgm0, dgm1], axis=0)
    g["norm_ffn_g"] = jnp.concatenate([dgf0, dgf1], axis=0)
    g["ffn_w_gate"] = [dwg0, dwg1]
    g["ffn_w_up"] = [dwu0, dwu1]
    g["ffn_w_down"] = [dwd0, dwd1]
    return loss, dx0, g


WEIGHTS = ["norm_mix_g", "norm_ffn_g", "a_w_in", "a_conv_w", "a_conv_b", "a_w_r", "a_b_r", "a_w_i", "a_b_i",
           "a_lambda", "a_w_out", "b_w_qkv", "b_w_out", "ffn_w_gate", "ffn_w_up", "ffn_w_down", "final_g"]
BIG = [("a_w_in", 2), ("a_w_r", 2), ("a_w_i", 2), ("a_w_out", 1), ("b_w_qkv", 2), ("b_w_out", 1),
       ("ffn_w_gate", 2), ("ffn_w_up", 2), ("ffn_w_down", 1)]
LAYER1 = ["b_w_qkv", "b_w_out", "ffn_w_gate", "ffn_w_up", "ffn_w_down"]
LAYER0 = ["a_w_in", "a_w_r", "a_w_i", "a_w_out", "ffn_w_gate", "ffn_w_up", "ffn_w_down"]
RS_COLLECTIVE_IDS = {"chips1": 3, "chips0": 4, "sibling1": 5, "share1": 6}
SMALL = ["norm_mix_g", "norm_ffn_g", "a_conv_w", "a_conv_b", "a_b_r", "a_b_i", "a_lambda", "final_g"]


def _split_chips(full, axis):
    return jnp.stack(jnp.split(full, N_CHIPS, axis=axis))


def _step(x, target, weights, moments_m, moments_v):
    chip = 2 * lax.axis_index("x") + lax.axis_index("y")
    core = lax.axis_index("c")
    shard_shapes = [weights[n].shape for n, _ in BIG]
    packed = _pack_shards([weights[n].astype(CD) for n, _ in BIG])
    gathered = _allgather_chips(packed, "allgather_weights")
    full = {}
    for (n, axis), stack in zip(BIG, _unpack_shards(gathered, shard_shapes)):
        own = weights[n].astype(CD)
        joined = jnp.concatenate([jnp.where(chip == s, own, stack[s]) for s in range(N_CHIPS)], axis=axis)
        full[n] = joined[0] if joined.shape[0] == 1 else joined
    cw_rows = jnp.zeros((N_CHIPS, CONV_W, RG_BW), F32)
    cw_rows = lax.dynamic_update_slice(cw_rows, jnp.where(core == 0, weights["a_conv_w"], 0.0), (chip, 0, 0))
    cw_all = _allreduce_small(cw_rows.reshape(-1, LANES), "allgather_conv_w").reshape(N_CHIPS, CONV_W, RG_BW)
    full["a_conv_w"] = jnp.concatenate([cw_all[s] for s in range(N_CHIPS)], axis=1)
    for n in ("norm_mix_g", "norm_ffn_g", "final_g"):
        full[n] = weights[n]
    for n in ("a_conv_b", "a_b_r", "a_b_i", "a_lambda"):
        full[n] = weights[n]
    loss, dx, grads = _local_step(x[0], target[0], full)
    small_parts = [grads[n].reshape(-1) for n in SMALL] + [loss.reshape(-1)]
    sizes = [p.shape[0] for p in small_parts]
    small = _allreduce_small(jnp.concatenate(small_parts).reshape(-1, LANES), "allreduce_small").reshape(-1)
    red, pos = {}, 0
    for n, sz in zip(SMALL + ["loss"], sizes):
        red[n] = small[pos:pos + sz]
        pos += sz
    loss_out = red["loss"][0]
    g_out = {}
    for n in SMALL:
        if n == "a_conv_w":
            g_out[n] = lax.dynamic_slice(red[n].reshape(CONV_W, D_RNN), (0, chip * RG_BW), (CONV_W, RG_BW)).reshape(
                weights[n].shape)
        else:
            g_out[n] = red[n].reshape(weights[n].shape)
    axis_of = dict(BIG)
    pieces = {}
    for group, layer, tag in ((LAYER1, 1, "1"), (LAYER0, 0, "0")):
        stacks, shapes = [], []
        for n in group:
            per_layer = isinstance(grads[n], list)
            gfull = grads[n][layer] if per_layer else grads[n]
            shard_shape = weights[n].shape[1:]
            gfull = gfull.reshape((1,) + gfull.shape)
            stacks.append(_split_chips(gfull, axis_of[n]).reshape(N_CHIPS, -1, PACK_COLS))
            shapes.append((1,) + tuple(shard_shape))
        gbuf = jnp.concatenate(stacks, axis=1)
        if layer == 1:
            from_sibling = _exchange_sibling_halves_async(gbuf, "rs_sibling" + tag, RS_COLLECTIVE_IDS["sibling1"])
        else:
            from_sibling = _exchange_sibling_halves(gbuf, "rs_sibling" + tag)
        chip_partial = _add_sibling(gbuf, from_sibling, core, "rs_add" + tag)
        from_chips = _scatter_to_chips_async(chip_partial, "rs_chips" + tag, RS_COLLECTIVE_IDS["chips" + tag])
        halves = _sum_slots(from_chips, chip_partial, chip, core, "rs_sum" + tag)
        if layer == 1:
            reduced = _share_halves_async(halves, "rs_share" + tag, RS_COLLECTIVE_IDS["share1"])
        else:
            reduced = _share_halves(halves, "rs_share" + tag)
        for n, piece in zip(group, _unpack_shards(reduced, shapes)):
            pieces.setdefault(n, {})[layer] = piece
    for n, _ in BIG:
        layers = pieces[n]
        g_out[n] = jnp.concatenate([layers[k] for k in sorted(layers)], axis=0)
    outs_g, outs_d, outs_m, outs_v = [], [], [], []
    for n in WEIGHTS:
        d, nm, nv = _adamw(weights[n], g_out[n], moments_m[n], moments_v[n], "adamw_" + n)
        outs_g.append(g_out[n])
        outs_d.append(d)
        outs_m.append(nm)
        outs_v.append(nv)
    return (loss_out, dx[None], *outs_g, *outs_d, *outs_m, *outs_v)


def kernel(x, norm_mix_g, norm_ffn_g, a_w_in, a_conv_w, a_conv_b, a_w_r, a_b_r, a_w_i, a_b_i, a_lambda, a_w_out, b_w_qkv, b_w_out, ffn_w_gate, ffn_w_up, ffn_w_down, final_g, loss_target, m_norm_mix_g, m_norm_ffn_g, m_a_w_in, m_a_conv_w, m_a_conv_b, m_a_w_r, m_a_b_r, m_a_w_i, m_a_b_i, m_a_lambda, m_a_w_out, m_b_w_qkv, m_b_w_out, m_ffn_w_gate, m_ffn_w_up, m_ffn_w_down, m_final_g, v_norm_mix_g, v_norm_ffn_g, v_a_w_in, v_a_conv_w, v_a_conv_b, v_a_w_r, v_a_b_r, v_a_w_i, v_a_b_i, v_a_lambda, v_a_w_out, v_b_w_qkv, v_b_w_out, v_ffn_w_gate, v_ffn_w_up, v_ffn_w_down, v_final_g):
    ws = [norm_mix_g, norm_ffn_g, a_w_in, a_conv_w, a_conv_b, a_w_r, a_b_r, a_w_i, a_b_i, a_lambda, a_w_out, b_w_qkv,
          b_w_out, ffn_w_gate, ffn_w_up, ffn_w_down, final_g]
    ms = [m_norm_mix_g, m_norm_ffn_g, m_a_w_in, m_a_conv_w, m_a_conv_b, m_a_w_r, m_a_b_r, m_a_w_i, m_a_b_i, m_a_lambda,
          m_a_w_out, m_b_w_qkv, m_b_w_out, m_ffn_w_gate, m_ffn_w_up, m_ffn_w_down, m_final_g]
    vs = [v_norm_mix_g, v_norm_ffn_g, v_a_w_in, v_a_conv_w, v_a_conv_b, v_a_w_r, v_a_b_r, v_a_w_i, v_a_b_i, v_a_lambda,
          v_a_w_out, v_b_w_qkv, v_b_w_out, v_ffn_w_gate, v_ffn_w_up, v_ffn_w_down, v_final_g]
    return _step(x, loss_target, dict(zip(WEIGHTS, ws)), dict(zip(WEIGHTS, ms)), dict(zip(WEIGHTS, vs)))
```

```python
import functools
import math

import jax
import jax.numpy as jnp
from jax import lax
from jax.experimental import pallas as pl
from jax.experimental.pallas import tpu as pltpu
from jax.experimental.pallas import tpu_sc as plsc

F32 = jnp.float32
CD = jnp.bfloat16

D_MODEL = 1024
D_RNN = 1024
RG_BLOCKS = 4
RG_BW = 256
CONV_W = 4
RG_C = 8.0
SB_HEADS = 16
SB_HEAD_DIM = 64
D_FF = 2816
RMS_EPS = 1e-6
N_CHIPS = 4
N_DEV = 8

ADAM_LR = 0.001
ADAM_B1 = 0.9
ADAM_B2 = 0.999
ADAM_EPS = 1e-08
ADAM_WD = 0.01
ADAM_STEP = 10

LANES = 128
VMEM_LIMIT = 56 * 1024 * 1024
MESH = pl.DeviceIdType.MESH


def _params(*sem):
    return pltpu.CompilerParams(dimension_semantics=sem, vmem_limit_bytes=VMEM_LIMIT)


def _pick(n, prefs):
    for p in prefs:
        if n % p == 0:
            return p
    return n


def _row_tile(rows):
    return max(d for d in range(16, 1025, 16) if rows % d == 0)


def _matmul(pairs, out_dtype, name, *, trans_a=False, a_lbm=False, b_lbm=False, out_lbm=False, addend=None,
            tm=512, tn=None, tk=None, norm_gain=None, norm_bwd=None):
    a0, b0 = pairs[0]
    if trans_a:
        kdim = a0.shape[1] if a_lbm else a0.shape[0]
        m = a0.shape[0] * LANES if a_lbm else a0.shape[1]
    else:
        m = a0.shape[1] if a_lbm else a0.shape[0]
        kdim = a0.shape[0] * LANES if a_lbm else a0.shape[1]
    n = b0.shape[0] * LANES if b_lbm else b0.shape[1]
    tm = _pick(m, (tm, 1408, 256, 128))
    tn = tn or _pick(n, (1408, 1024, 768, 512, 256, 128))
    tk = tk or _pick(kdim, (1024, 1408, 512, 256, 128))
    nk = kdim // tk
    npair = len(pairs)

    def cat(ref):
        return jnp.concatenate([ref[p] for p in range(ref.shape[0])], axis=-1)

    def body(*refs):
        ins = refs[: 2 * npair]
        pos = 2 * npair
        add_ref = None
        if addend is not None:
            add_ref = refs[pos]
            pos += 1
        gain_ref = x_ref = dxin_ref = None
        if norm_gain is not None:
            gain_ref = refs[pos]
            pos += 1
        if norm_bwd is not None:
            x_ref, gain_ref, dxin_ref = refs[pos:pos + 3]
            pos += 3
        o_ref = refs[pos]
        extra_out = refs[pos + 1:-1]
        acc_ref = refs[-1]
        k = pl.program_id(2)

        @pl.when(k == 0)
        def _():
            acc_ref[...] = jnp.zeros_like(acc_ref)

        if norm_bwd is not None:
            @pl.when((k == 0) & (pl.program_id(0) == 0))
            def _():
                extra_out[1][...] = jnp.zeros_like(extra_out[1])

        acc = acc_ref[...]
        for p in range(npair):
            a = (cat(ins[2 * p]) if a_lbm else ins[2 * p][...]).astype(CD)
            b = (cat(ins[2 * p + 1]) if b_lbm else ins[2 * p + 1][...]).astype(CD)
            dims = (((0,), (0,)), ((), ())) if trans_a else (((1,), (0,)), ((), ()))
            acc = acc + lax.dot_general(a, b, dims, preferred_element_type=F32)
        acc_ref[...] = acc

        @pl.when(k == nk - 1)
        def _():
            res = acc_ref[...]
            if add_ref is not None:
                res = res + add_ref[...]
            if norm_gain is not None:
                rinv = lax.rsqrt(jnp.mean(res * res, axis=-1, keepdims=True) + RMS_EPS)
                extra_out[0][...] = (res * rinv * gain_ref[...]).astype(CD)
            if norm_bwd is not None:
                xv = x_ref[...]
                rinv = lax.rsqrt(jnp.mean(xv * xv, axis=-1, keepdims=True) + RMS_EPS)
                nrm = xv * rinv
                dn = res * gain_ref[...]
                extra_out[1][...] += jnp.sum(res * nrm, axis=0, keepdims=True)
                res = dxin_ref[...] + rinv * (dn - nrm * jnp.mean(dn * nrm, axis=-1, keepdims=True))
                extra_out[0][...] = res.astype(CD)
            res = res.astype(out_dtype)
            if out_lbm:
                for p in range(tn // LANES):
                    o_ref[p] = res[:, p * LANES:(p + 1) * LANES]
            else:
                o_ref[...] = res

    if trans_a:
        a_spec = (pl.BlockSpec((tm // LANES, tk, LANES), lambda i, j, k: (i, k, 0)) if a_lbm
                  else pl.BlockSpec((tk, tm), lambda i, j, k: (k, i)))
    else:
        a_spec = (pl.BlockSpec((tk // LANES, tm, LANES), lambda i, j, k: (k, i, 0)) if a_lbm
                  else pl.BlockSpec((tm, tk), lambda i, j, k: (i, k)))
    b_spec = (pl.BlockSpec((tn // LANES, tk, LANES), lambda i, j, k: (j, k, 0)) if b_lbm
              else pl.BlockSpec((tk, tn), lambda i, j, k: (k, j)))
    in_specs = []
    args = []
    for a, b in pairs:
        in_specs += [a_spec, b_spec]
        args += [a, b]
    if addend is not None:
        in_specs.append(pl.BlockSpec((tm, tn), lambda i, j, k: (i, j)))
        args.append(addend)
    tile = pl.BlockSpec((tm, tn), lambda i, j, k: (i, j))
    vec = pl.BlockSpec((1, tn), lambda i, j, k: (0, j))
    if out_lbm:
        out_shape = jax.ShapeDtypeStruct((n // LANES, m, LANES), out_dtype)
        out_spec = pl.BlockSpec((tn // LANES, tm, LANES), lambda i, j, k: (j, i, 0))
    else:
        out_shape = jax.ShapeDtypeStruct((m, n), out_dtype)
        out_spec = tile
    sem = ("parallel", "parallel", "arbitrary")
    if norm_gain is not None or norm_bwd is not None:
        assert tn == n and not out_lbm, "the norm needs whole rows in one tile"
        out_shape, out_spec = [out_shape, jax.ShapeDtypeStruct((m, n), CD)], [out_spec, tile]
    if norm_gain is not None:
        in_specs.append(vec)
        args.append(norm_gain.reshape(1, n))
    if norm_bwd is not None:
        x_in, gain, dx_in = norm_bwd
        in_specs += [tile, vec, tile]
        args += [x_in, gain.reshape(1, n), dx_in]
        out_shape.append(jax.ShapeDtypeStruct((1, n), F32))
        out_spec.append(vec)
        sem = ("arbitrary", "arbitrary", "arbitrary")
    return pl.pallas_call(
        body, name=name, out_shape=out_shape, grid=(m // tm, n // tn, nk),
        in_specs=in_specs, out_specs=out_spec,
        scratch_shapes=[pltpu.VMEM((tm, tn), F32)],
        compiler_params=_params(*sem),
    )(*args)


ROW_BLOCK = 256


def _rms_fwd(x, g, name):
    s, d = x.shape

    def body(x_ref, g_ref, h_ref):
        xv = x_ref[...]
        rinv = lax.rsqrt(jnp.mean(xv * xv, axis=-1, keepdims=True) + RMS_EPS)
        h_ref[...] = (xv * rinv * g_ref[...]).astype(CD)

    return pl.pallas_call(
        body, name=name, out_shape=jax.ShapeDtypeStruct((s, d), CD), grid=(s // ROW_BLOCK,),
        in_specs=[pl.BlockSpec((ROW_BLOCK, d), lambda i: (i, 0)), pl.BlockSpec((1, d), lambda i: (0, 0))],
        out_specs=pl.BlockSpec((ROW_BLOCK, d), lambda i: (i, 0)),
        compiler_params=_params("parallel"),
    )(x, g.reshape(1, d))


def _loss_head(x, g, target, name):
    s, d = x.shape

    def body(x_ref, g_ref, t_ref, loss_ref, dx_ref, dxc_ref, dg_ref):
        @pl.when(pl.program_id(0) == 0)
        def _():
            dg_ref[...] = jnp.zeros_like(dg_ref)
            loss_ref[...] = jnp.zeros_like(loss_ref)

        xv = x_ref[...]
        gv = g_ref[...]
        rinv = lax.rsqrt(jnp.mean(xv * xv, axis=-1, keepdims=True) + RMS_EPS)
        nrm = xv * rinv
        err = nrm * gv - t_ref[...]
        loss_ref[...] += 0.5 * jnp.sum(jnp.mean(err * err, axis=-1, keepdims=True), axis=0, keepdims=True)
        dy = err * (1.0 / d)
        dn = dy * gv
        dx = rinv * (dn - nrm * jnp.mean(dn * nrm, axis=-1, keepdims=True))
        dx_ref[...] = dx
        dxc_ref[...] = dx.astype(CD)
        dg_ref[...] += jnp.sum(dy * nrm, axis=0, keepdims=True)

    row = pl.BlockSpec((ROW_BLOCK, d), lambda i: (i, 0))
    vec = pl.BlockSpec((1, d), lambda i: (0, 0))
    return pl.pallas_call(
        body, name=name,
        out_shape=(jax.ShapeDtypeStruct((1, LANES), F32), jax.ShapeDtypeStruct((s, d), F32),
                   jax.ShapeDtypeStruct((s, d), CD), jax.ShapeDtypeStruct((1, d), F32)),
        grid=(s // ROW_BLOCK,), in_specs=[row, vec, row],
        out_specs=(pl.BlockSpec((1, LANES), lambda i: (0, 0)), row, row, vec),
        compiler_params=_params("arbitrary"),
    )(x, g.reshape(1, d), target)


def _sigmoid(z):
    return 1.0 / (1.0 + jnp.exp(-z))


FFN_TM = 512
FFN_TN = 1408


def _ffn_up(h, wg, wu, name):
    s, d = h.shape
    f = wg.shape[1]
    tm = _pick(s, (FFN_TM, 256))

    def body(h_ref, wg_ref, wu_ref, g_ref, u_ref, a_ref):
        hv = h_ref[...]
        gv = jnp.dot(hv, wg_ref[...], preferred_element_type=F32)
        uv = jnp.dot(hv, wu_ref[...], preferred_element_type=F32)
        g_ref[...] = gv
        u_ref[...] = uv
        a_ref[...] = (gv * _sigmoid(gv) * uv).astype(CD)

    a_spec = pl.BlockSpec((tm, d), lambda i, j: (i, 0))
    w_spec = pl.BlockSpec((d, FFN_TN), lambda i, j: (0, j))
    o_spec = pl.BlockSpec((tm, FFN_TN), lambda i, j: (i, j))
    return pl.pallas_call(
        body, name=name,
        out_shape=(jax.ShapeDtypeStruct((s, f), F32), jax.ShapeDtypeStruct((s, f), F32),
                   jax.ShapeDtypeStruct((s, f), CD)),
        grid=(s // tm, f // FFN_TN), in_specs=[a_spec, w_spec, w_spec], out_specs=(o_spec, o_spec, o_spec),
        compiler_params=_params("parallel", "parallel"),
    )(h, wg, wu)


def _ffn_dact(dxc, wd_t, g, u, name):
    s, d = dxc.shape
    f = wd_t.shape[1]
    tm = _pick(s, (FFN_TM, 256))

    def body(dx_ref, w_ref, g_ref, u_ref, dg_ref, du_ref):
        da = jnp.dot(dx_ref[...], w_ref[...], preferred_element_type=F32)
        gv = g_ref[...]
        sg = _sigmoid(gv)
        silu = gv * sg
        dg_ref[...] = (da * u_ref[...] * (sg + silu * (1.0 - sg))).astype(CD)
        du_ref[...] = (da * silu).astype(CD)

    a_spec = pl.BlockSpec((tm, d), lambda i, j: (i, 0))
    w_spec = pl.BlockSpec((d, FFN_TN), lambda i, j: (0, j))
    o_spec = pl.BlockSpec((tm, FFN_TN), lambda i, j: (i, j))
    return pl.pallas_call(
        body, name=name,
        out_shape=(jax.ShapeDtypeStruct((s, f), CD), jax.ShapeDtypeStruct((s, f), CD)),
        grid=(s // tm, f // FFN_TN), in_specs=[a_spec, w_spec, o_spec, o_spec], out_specs=(o_spec, o_spec),
        compiler_params=_params("parallel", "parallel"),
    )(dxc, wd_t, g, u)


TIME_BLOCK = 256
SUBLANES = 8
GELU_C = math.sqrt(2.0 / math.pi)
GELU_A = 0.044715


def _gelu(x):
    return 0.5 * x * (1.0 + jnp.tanh(GELU_C * (x + GELU_A * x * x * x)))


def _gelu_grad(x):
    t = jnp.tanh(GELU_C * (x + GELU_A * x * x * x))
    return 0.5 * (1.0 + t) + 0.5 * x * (1.0 - t * t) * GELU_C * (1.0 + 3.0 * GELU_A * x * x)


def _neg_expm1(x):
    series = -x * (1.0 + x * (0.5 + x * (1.0 / 6.0 + x * (1.0 / 24.0))))
    return jnp.where(x > -0.05, series, 1.0 - jnp.exp(x))


def _log_sigmoid(x):
    return jnp.minimum(x, 0.0) - jnp.log1p(jnp.exp(-jnp.abs(x)))


def _shift_down(x, tail, s):
    if s == 0:
        return x
    ext = jnp.concatenate([tail, x], axis=0)
    return pltpu.roll(ext, s, axis=0)[SUBLANES:]


def _shift_up(x, head, s):
    if s == 0:
        return x
    n = x.shape[0]
    ext = jnp.concatenate([x, head], axis=0)
    return pltpu.roll(ext, n + SUBLANES - s, axis=0)[:n]


def _rg_gates(xbr, tail, cw_ref, cb, wr, wi, br, bi, ls):
    taps = [_shift_down(xbr, tail, CONV_W - 1 - k) for k in range(CONV_W)]
    xc = cb
    for k in range(CONV_W):
        xc = xc + cw_ref[pl.ds(k, 1), :] * taps[k]
    xcd = xc.astype(CD)
    r = _sigmoid(jnp.dot(xcd, wr, preferred_element_type=F32) + br)
    i = _sigmoid(jnp.dot(xcd, wi, preferred_element_type=F32) + bi)
    log_a = RG_C * r * ls
    a = jnp.exp(log_a)
    mult = jnp.sqrt(jnp.maximum(_neg_expm1(2.0 * log_a), 0.0))
    return taps, xc, r, i, log_a, a, mult


def _scan8_fwd(a, u):
    row = lax.broadcasted_iota(jnp.int32, a.shape, 0)
    for d in (1, 2, 4):
        a_s = pltpu.roll(a, d, axis=0)
        u_s = pltpu.roll(u, d, axis=0)
        m = row >= d
        u = jnp.where(m, a * u_s + u, u)
        a = jnp.where(m, a * a_s, a)
    return a, u


def _scan8_bwd(b, u):
    row = lax.broadcasted_iota(jnp.int32, b.shape, 0)
    for d in (1, 2, 4):
        b_s = pltpu.roll(b, SUBLANES - d, axis=0)
        u_s = pltpu.roll(u, SUBLANES - d, axis=0)
        m = row < SUBLANES - d
        u = jnp.where(m, b * u_s + u, u)
        b = jnp.where(m, b * b_s, b)
    return b, u


def _rglru_fwd(gate_br, x_br, cw, cb, wr, wi, br, bi, lam, name):
    s, c = x_br.shape
    nt = s // TIME_BLOCK
    tb, cbw = TIME_BLOCK, RG_BW
    groups = tb // SUBLANES

    def body(g_ref, x_ref, tail_ref, cw_ref, cb_ref, wr_ref, wi_ref, br_ref, bi_ref, lam_ref,
             y_ref, hs_ref, carry_ref, a_scr, u_scr):
        t = pl.program_id(1)

        @pl.when(t == 0)
        def _():
            carry_ref[...] = jnp.zeros_like(carry_ref)

        tail = jnp.where(t > 0, tail_ref[...], 0.0)
        ls = _log_sigmoid(lam_ref[...])
        _, xc, _, i, _, a, mult = _rg_gates(x_ref[...], tail, cw_ref, cb_ref[...], wr_ref[0], wi_ref[0],
                                            br_ref[...], bi_ref[...], ls)
        a_scr[...] = a
        u_scr[...] = mult * (i * xc)
        carry = carry_ref[...]
        for gi in range(groups):
            rows = pl.ds(gi * SUBLANES, SUBLANES)
            pa, hl = _scan8_fwd(a_scr[rows, :], u_scr[rows, :])
            hs_ref[rows, :] = hl + pa * carry
            carry = hs_ref[pl.ds(gi * SUBLANES + SUBLANES - 1, 1), :]
        carry_ref[...] = carry
        y_ref[...] = (hs_ref[...] * _gelu(g_ref[...])).astype(CD)

    blk = pl.BlockSpec((tb, cbw), lambda n, t: (t, n))
    tail = pl.BlockSpec((SUBLANES, cbw), lambda n, t: (jnp.maximum(t * groups - 1, 0), n))
    vec = pl.BlockSpec((1, cbw), lambda n, t: (0, n))
    wblk = pl.BlockSpec((1, cbw, cbw), lambda n, t: (n, 0, 0))
    return pl.pallas_call(
        body, name=name,
        out_shape=(jax.ShapeDtypeStruct((s, c), CD), jax.ShapeDtypeStruct((s, c), F32)),
        grid=(RG_BLOCKS, nt),
        in_specs=[blk, blk, tail, pl.BlockSpec((CONV_W, cbw), lambda n, t: (0, n)), vec, wblk, wblk, vec, vec, vec],
        out_specs=(blk, blk),
        scratch_shapes=[pltpu.VMEM((1, cbw), F32), pltpu.VMEM((tb, cbw), F32), pltpu.VMEM((tb, cbw), F32)],
        compiler_params=_params("parallel", "arbitrary"),
    )(gate_br, x_br, x_br, cw, cb, wr, wi, br, bi, lam)


def _rglru_bwd(dy, gate_br, x_br, hs, cw, cb, wr, wi, wrt, wit, br, bi, lam, name):
    s, c = x_br.shape
    nt = s // TIME_BLOCK
    tb, cbw = TIME_BLOCK, RG_BW
    groups = tb // SUBLANES

    def body(dy_ref, g_ref, x_ref, tail_ref, hs_ref, hprev_ref, cw_ref, cb_ref, wr_ref, wi_ref, wrt_ref, wit_ref,
             br_ref, bi_ref, lam_ref,
             dg_ref, dx_ref, dcw_ref, dcb_ref, dbr_ref, dbi_ref, dlam_ref, dwr_ref, dwi_ref,
             carry_ref, head_ref, b_scr, u_scr, dh_scr):
        tr = pl.program_id(1)
        first_block = tr == nt - 1

        @pl.when(tr == 0)
        def _():
            carry_ref[...] = jnp.zeros_like(carry_ref)
            head_ref[...] = jnp.zeros_like(head_ref)
            for ref in (dcw_ref, dcb_ref, dbr_ref, dbi_ref, dlam_ref, dwr_ref, dwi_ref):
                ref[...] = jnp.zeros_like(ref)

        tail = jnp.where(first_block, 0.0, tail_ref[...])
        lam_v = lam_ref[...]
        ls = _log_sigmoid(lam_v)
        taps, xc, r, i, log_a, a, mult = _rg_gates(x_ref[...], tail, cw_ref, cb_ref[...], wr_ref[0], wi_ref[0],
                                                   br_ref[...], bi_ref[...], ls)
        gate_v = g_ref[...]
        dyv = dy_ref[...]
        hsv = hs_ref[...]
        dg_ref[...] = (dyv * hsv * _gelu_grad(gate_v)).astype(CD)

        row = lax.broadcasted_iota(jnp.int32, a.shape, 0)
        b_scr[...] = jnp.where(row == tb - 1, 1.0, pltpu.roll(a, tb - 1, axis=0))
        u_scr[...] = dyv * _gelu(gate_v)
        carry = carry_ref[...]
        for gi in reversed(range(groups)):
            rows = pl.ds(gi * SUBLANES, SUBLANES)
            pb, gl = _scan8_bwd(b_scr[rows, :], u_scr[rows, :])
            dh_scr[rows, :] = gl + pb * carry
            carry = dh_scr[pl.ds(gi * SUBLANES, 1), :]
        dh = dh_scr[...]
        carry_ref[...] = carry * jnp.sum(jnp.where(row == 0, a, 0.0), axis=0, keepdims=True)

        hprev_tail = jnp.where(first_block, 0.0, hprev_ref[...])
        h_prev = _shift_down(hsv, hprev_tail, 1)
        da = dh * h_prev
        ixc = i * xc
        dmult = dh * ixc
        di = dh * mult * xc
        dxc = dh * mult * i
        a2 = a * a
        dlog_a = da * a - dmult * a2 / mult
        dpre_r = (dlog_a * (RG_C * ls)) * r * (1.0 - r)
        dpre_i = di * i * (1.0 - i)
        dlam_ref[...] += jnp.sum(dlog_a * r, axis=0, keepdims=True) * (RG_C * _sigmoid(-lam_v))
        dbr_ref[...] += jnp.sum(dpre_r, axis=0, keepdims=True)
        dbi_ref[...] += jnp.sum(dpre_i, axis=0, keepdims=True)
        xcd = xc.astype(CD)
        dprc = dpre_r.astype(CD)
        dpic = dpre_i.astype(CD)
        tn_dims = (((0,), (0,)), ((), ()))
        dwr_ref[0] += lax.dot_general(xcd, dprc, tn_dims, preferred_element_type=F32)
        dwi_ref[0] += lax.dot_general(xcd, dpic, tn_dims, preferred_element_type=F32)
        dxc = dxc + jnp.dot(dprc, wrt_ref[0], preferred_element_type=F32) + jnp.dot(dpic, wit_ref[0],
                                                                                    preferred_element_type=F32)
        dcb_ref[...] += jnp.sum(dxc, axis=0, keepdims=True)
        for k in range(CONV_W):
            dcw_ref[pl.ds(k, 1), :] += jnp.sum(dxc * taps[k], axis=0, keepdims=True)
        head = head_ref[...]
        dxb = jnp.zeros_like(dxc)
        for sft in range(CONV_W):
            dxb = dxb + cw_ref[pl.ds(CONV_W - 1 - sft, 1), :] * _shift_up(dxc, head, sft)
        dx_ref[...] = dxb.astype(CD)
        head_ref[...] = dxc[0:SUBLANES, :]

    blk = pl.BlockSpec((tb, cbw), lambda n, t: (nt - 1 - t, n))
    tail = pl.BlockSpec((SUBLANES, cbw), lambda n, t: (jnp.maximum((nt - 1 - t) * groups - 1, 0), n))
    vec = pl.BlockSpec((1, cbw), lambda n, t: (0, n))
    cwb = pl.BlockSpec((CONV_W, cbw), lambda n, t: (0, n))
    wblk = pl.BlockSpec((1, cbw, cbw), lambda n, t: (n, 0, 0))
    vshape = jax.ShapeDtypeStruct((1, c), F32)
    wshape = jax.ShapeDtypeStruct((RG_BLOCKS, cbw, cbw), F32)
    return pl.pallas_call(
        body, name=name,
        out_shape=(jax.ShapeDtypeStruct((s, c), CD), jax.ShapeDtypeStruct((s, c), CD),
                   jax.ShapeDtypeStruct((CONV_W, c), F32), vshape, vshape, vshape, vshape, wshape, wshape),
        grid=(RG_BLOCKS, nt),
        in_specs=[blk, blk, blk, tail, blk, tail, cwb, vec, wblk, wblk, wblk, wblk, vec, vec, vec],
        out_specs=(blk, blk, cwb, vec, vec, vec, vec, wblk, wblk),
        scratch_shapes=[pltpu.VMEM((1, cbw), F32), pltpu.VMEM((SUBLANES, cbw), F32),
                        pltpu.VMEM((tb, cbw), F32), pltpu.VMEM((tb, cbw), F32), pltpu.VMEM((tb, cbw), F32)],
        compiler_params=_params("parallel", "arbitrary"),
    )(dy, gate_br, x_br, x_br, hs, hs, cw, cb, wr, wi, wrt, wit, br, bi, lam)


ATT_BLOCK = 256
ATT_Q_BLOCK = 1024
ATT_RATIO = ATT_Q_BLOCK // ATT_BLOCK
ATT_SCALE = 1.0 / math.sqrt(SB_HEAD_DIM)
N_PAIRS = SB_HEADS * SB_HEAD_DIM // LANES
NT_DIMS = (((1,), (1,)), ((), ()))
TN_DIMS = (((0,), (0,)), ((), ()))


LOG2E = 1.4426950408889634


def _neg_abs(x):
    bits = lax.bitcast_convert_type(x, jnp.uint32) | jnp.uint32(0x80000000)
    return lax.bitcast_convert_type(bits, F32)


def _qk(qx, kb):
    return lax.dot_general(qx, kb, NT_DIMS, preferred_element_type=F32)


def _sb_logits(qk, valid):
    z2 = qk * (ATT_SCALE * LOG2E)
    lb2 = jnp.minimum(z2, 0.0) - jnp.log2(1.0 + jnp.exp2(_neg_abs(z2)))
    l2 = lb2 - z2
    if valid is not None:
        l2 = jnp.where(valid, l2, 0.0)
    return lb2, l2


def _hi_lo(x):
    hi = x.astype(CD)
    lo = (x - hi.astype(F32)).astype(CD)
    return jnp.concatenate([hi, lo], axis=1)


def _tri(strict, stacked):
    r = lax.broadcasted_iota(jnp.int32, (ATT_BLOCK, ATT_BLOCK), 0)
    c = lax.broadcasted_iota(jnp.int32, (ATT_BLOCK, ATT_BLOCK), 1)
    m = (r > c if strict else r >= c).astype(CD)
    return jnp.concatenate([m, m], axis=0) if stacked else m


def _attn_fwd(qkv, name):
    _, s, _ = qkv.shape
    tq, t = ATT_Q_BLOCK, ATT_BLOCK
    nblk = s // tq

    def body(q_ref, k_ref, v_ref, o_ref, qk_scr, w_scr):
        i = pl.program_id(1)
        lane = lax.broadcasted_iota(jnp.int32, (1, LANES), 1)
        head_masks = (lane < SB_HEAD_DIM, lane >= SB_HEAD_DIM)
        q = q_ref[0]
        qs = [jnp.where(m, q, jnp.zeros_like(q)) for m in head_masks]
        tri = _tri(True, False)
        rr = lax.broadcasted_iota(jnp.int32, (tq, t), 0)
        cc = lax.broadcasted_iota(jnp.int32, (tq, t), 1)

        def rows_of(j):
            return pl.ds(pl.multiple_of(j * t, t), t)

        def tail(x, row0):
            return x if row0 == 0 else x[row0:]

        def start_logits(j, row0=0):
            kb = k_ref[0, rows_of(j), :]
            for hd in range(2):
                qk_scr[hd, row0:, :] = _qk(tail(qs[hd], row0), kb)

        def weights(run, diagonal=False, row0=0):
            new_run = []
            valid = (cc < rr)[:tq - row0] if diagonal else None
            for hd in range(2):
                lb2, l2 = _sb_logits(qk_scr[hd, row0:, :], valid)
                w = jnp.exp2(lb2 + (tail(run[hd], row0) + jnp.dot(l2.astype(CD), tri, preferred_element_type=F32)))
                if valid is not None:
                    w = jnp.where(valid, w, 0.0)
                w_scr[row0:, hd * t:(hd + 1) * t] = w.astype(CD)
                rowsum = jnp.sum(l2, axis=1, keepdims=True)
                if row0:
                    rowsum = jnp.concatenate([jnp.zeros((row0, 1), F32), rowsum], axis=0)
                new_run.append(run[hd] + rowsum)
            return tuple(new_run)

        def apply_weights(j, row0=0):
            vb = v_ref[0, rows_of(j), :]
            vcat = jnp.concatenate([jnp.where(m, vb, jnp.zeros_like(vb)) for m in head_masks], axis=0)
            inc = jnp.dot(w_scr[row0:, :], vcat, preferred_element_type=F32)
            return inc if row0 == 0 else jnp.concatenate([jnp.zeros((row0, LANES), F32), inc], axis=0)

        zero = jnp.zeros((tq, 1), F32)
        last = ATT_RATIO - 1
        start_logits(ATT_RATIO * i + last, last * t)
        run = weights((zero, zero), True, last * t)
        oacc = jnp.zeros((tq, LANES), F32)
        for d in reversed(range(last)):
            start_logits(ATT_RATIO * i + d, d * t)
            oacc = oacc + apply_weights(ATT_RATIO * i + d + 1, (d + 1) * t)
            run = weights(run, True, d * t)
        start_logits(jnp.maximum(ATT_RATIO * i - 1, 0))

        def step(jj, carry):
            run, oacc = carry
            b = ATT_RATIO * i - 1 - jj
            oacc = oacc + apply_weights(b + 1)
            run = weights(run)
            start_logits(jnp.maximum(b - 1, 0))
            return run, oacc

        run, oacc = lax.fori_loop(0, ATT_RATIO * i, step, (run, oacc))
        o_ref[0] = oacc + apply_weights(0)

    return pl.pallas_call(
        body, name=name, out_shape=jax.ShapeDtypeStruct((N_PAIRS, s, LANES), F32), grid=(N_PAIRS, nblk),
        in_specs=[pl.BlockSpec((1, tq, LANES), lambda p, i: (p, i, 0)),
                  pl.BlockSpec((1, s, LANES), lambda p, i: (N_PAIRS + p, 0, 0)),
                  pl.BlockSpec((1, s, LANES), lambda p, i: (2 * N_PAIRS + p, 0, 0))],
        out_specs=pl.BlockSpec((1, tq, LANES), lambda p, i: (p, i, 0)),
        scratch_shapes=[pltpu.VMEM((2, tq, t), F32), pltpu.VMEM((tq, 2 * t), CD)],
        compiler_params=_params("parallel", "arbitrary"),
    )(qkv, qkv, qkv)


def _attn_bwd(qkv, o, do, name):
    _, s, _ = qkv.shape
    tq, t = ATT_Q_BLOCK, ATT_BLOCK
    nblk = s // tq

    def body(q_ref, k_ref, v_ref, o_ref, do_ref, dq_ref, dk_ref, dv_ref, qk_scr, dw_scr, w_scr, dz_scr):
        i = pl.program_id(1)

        @pl.when(i == 0)
        def _():
            dk_ref[...] = jnp.zeros_like(dk_ref)
            dv_ref[...] = jnp.zeros_like(dv_ref)

        lane = lax.broadcasted_iota(jnp.int32, (1, LANES), 1)
        head_masks = (lane < SB_HEAD_DIM, lane >= SB_HEAD_DIM)
        q = q_ref[0]
        dov = do_ref[0]
        ov = o_ref[0]
        qs = [jnp.where(m, q, jnp.zeros_like(q)) for m in head_masks]
        q_scaled_t = jnp.concatenate([(qx.astype(F32) * ATT_SCALE).T for qx in qs], axis=1).astype(CD)
        docs = [jnp.where(m, dov, 0.0).astype(CD) for m in head_masks]
        docat_t = jnp.concatenate([jnp.where(m, dov, 0.0).T for m in head_masks], axis=1).astype(CD)
        totals = [jnp.sum(d.astype(F32) * ov, axis=1, keepdims=True) for d in docs]
        tri = _tri(True, False)
        tri_incl = _tri(False, True)
        rr = lax.broadcasted_iota(jnp.int32, (tq, t), 0)
        cc = lax.broadcasted_iota(jnp.int32, (tq, t), 1)

        def rows_of(j):
            return pl.ds(pl.multiple_of(j * t, t), t)

        def tail(x, row0):
            return x if row0 == 0 else x[row0:]

        def pad_rows(x, row0):
            return x if row0 == 0 else jnp.concatenate([jnp.zeros((row0, x.shape[1]), x.dtype), x], axis=0)

        def start_products(j, row0=0):
            kb = k_ref[0, rows_of(j), :]
            vb = v_ref[0, rows_of(j), :]
            for hd in range(2):
                qk_scr[hd, row0:, :] = _qk(tail(qs[hd], row0), kb)
                dw_scr[hd, row0:, :] = lax.dot_general(tail(docs[hd], row0), vb, NT_DIMS, preferred_element_type=F32)

        def logit_grads(run, erun, diagonal=False, row0=0):
            new_run, new_erun = [], []
            valid = (cc < rr)[:tq - row0] if diagonal else None
            for hd in range(2):
                lb2, l2 = _sb_logits(qk_scr[hd, row0:, :], valid)
                w = jnp.exp2(lb2 + (tail(run[hd], row0) + jnp.dot(l2.astype(CD), tri, preferred_element_type=F32)))
                if valid is not None:
                    w = jnp.where(valid, w, 0.0)
                wc = w.astype(CD)
                w_scr[hd * tq + row0:(hd + 1) * tq, :] = wc
                e = dw_scr[hd, row0:, :] * wc.astype(F32)
                prefix = (tail(totals[hd] - erun[hd], row0)
                          - jnp.dot(_hi_lo(e), tri_incl, preferred_element_type=F32))
                dz = e - jnp.exp2(lb2) * (e + prefix)
                if valid is not None:
                    dz = jnp.where(valid, dz, 0.0)
                dz_scr[hd * tq + row0:(hd + 1) * tq, :] = dz.astype(CD)
                new_run.append(run[hd] + pad_rows(jnp.sum(l2, axis=1, keepdims=True), row0))
                new_erun.append(erun[hd] + pad_rows(jnp.sum(e, axis=1, keepdims=True), row0))
            return tuple(new_run), tuple(new_erun)

        def apply_grads(j, row0=0):
            rows = rows_of(j)
            kb = k_ref[0, rows, :]
            kcat = jnp.concatenate([jnp.where(m, kb, jnp.zeros_like(kb)) for m in head_masks], axis=0)
            dz_heads = [dz_scr[hd * tq + row0:(hd + 1) * tq, :] for hd in range(2)]
            w_heads = [w_scr[hd * tq + row0:(hd + 1) * tq, :] for hd in range(2)]
            q_t = jnp.concatenate([q_scaled_t[:, hd * tq + row0:(hd + 1) * tq] for hd in range(2)], axis=1)
            do_t = jnp.concatenate([docat_t[:, hd * tq + row0:(hd + 1) * tq] for hd in range(2)], axis=1)
            dk_ref[0, :, rows] += jnp.dot(q_t, jnp.concatenate(dz_heads, axis=0), preferred_element_type=F32)
            dv_ref[0, :, rows] += jnp.dot(do_t, jnp.concatenate(w_heads, axis=0), preferred_element_type=F32)
            return pad_rows(jnp.dot(jnp.concatenate(dz_heads, axis=1), kcat, preferred_element_type=F32), row0)

        zero = jnp.zeros((tq, 1), F32)
        last = ATT_RATIO - 1
        start_products(ATT_RATIO * i + last, last * t)
        run, erun = logit_grads((zero, zero), (zero, zero), True, last * t)
        dqacc = jnp.zeros((tq, LANES), F32)
        for d in reversed(range(last)):
            start_products(ATT_RATIO * i + d, d * t)
            dqacc = dqacc + apply_grads(ATT_RATIO * i + d + 1, (d + 1) * t)
            run, erun = logit_grads(run, erun, True, d * t)
        start_products(jnp.maximum(ATT_RATIO * i - 1, 0))

        def step(jj, carry):
            run, erun, dqacc = carry
            b = ATT_RATIO * i - 1 - jj
            dqacc = dqacc + apply_grads(b + 1)
            run, erun = logit_grads(run, erun)
            start_products(jnp.maximum(b - 1, 0))
            return run, erun, dqacc

        run, erun, dqacc = lax.fori_loop(0, ATT_RATIO * i, step, (run, erun, dqacc))
        dq_ref[0] = (dqacc + apply_grads(0)) * ATT_SCALE

    qblk = pl.BlockSpec((1, tq, LANES), lambda p, i: (p, i, 0))
    full = pl.BlockSpec((1, LANES, s), lambda p, i: (p, 0, 0))
    shape = jax.ShapeDtypeStruct((N_PAIRS, s, LANES), F32)
    shape_t = jax.ShapeDtypeStruct((N_PAIRS, LANES, s), F32)
    dq, dk_t, dv_t = pl.pallas_call(
        body, name=name, out_shape=(shape, shape_t, shape_t), grid=(N_PAIRS, nblk),
        in_specs=[qblk,
                  pl.BlockSpec((1, s, LANES), lambda p, i: (N_PAIRS + p, 0, 0)),
                  pl.BlockSpec((1, s, LANES), lambda p, i: (2 * N_PAIRS + p, 0, 0)),
                  qblk, qblk],
        out_specs=(qblk, full, full),
        scratch_shapes=[pltpu.VMEM((2, tq, t), F32), pltpu.VMEM((2, tq, t), F32),
                        pltpu.VMEM((2 * tq, t), CD), pltpu.VMEM((2 * tq, t), CD)],
        compiler_params=_params("parallel", "arbitrary"),
    )(qkv, qkv, qkv, o, do)
    return dq, jnp.swapaxes(dk_t, 1, 2), jnp.swapaxes(dv_t, 1, 2)


def _adamw(w, g, m, v, name):
    shape = w.shape
    rows, cols = (shape[-2], shape[-1]) if len(shape) >= 2 else (1, shape[-1])
    lead = w.size // (rows * cols)
    tr = _pick(rows, (512, 256, 128, 64, 32, 16, 8))

    def body(w_ref, g_ref, m_ref, v_ref, d_ref, nm_ref, nv_ref):
        gv = g_ref[...]
        nm = ADAM_B1 * m_ref[...] + (1.0 - ADAM_B1) * gv
        nv = ADAM_B2 * v_ref[...] + (1.0 - ADAM_B2) * (gv * gv)
        m_hat = nm / (1.0 - ADAM_B1 ** ADAM_STEP)
        v_hat = nv / (1.0 - ADAM_B2 ** ADAM_STEP)
        d_ref[...] = -ADAM_LR * (m_hat / (jnp.sqrt(v_hat) + ADAM_EPS) + ADAM_WD * w_ref[...])
        nm_ref[...] = nm
        nv_ref[...] = nv

    blk = pl.BlockSpec((1, tr, cols), lambda l, i: (l, i, 0))
    out = jax.ShapeDtypeStruct((lead, rows, cols), F32)
    d, nm, nv = pl.pallas_call(
        body, name=name, out_shape=(out, out, out), grid=(lead, rows // tr),
        in_specs=[blk, blk, blk, blk], out_specs=(blk, blk, blk), compiler_params=_params("parallel", "parallel"),
    )(*[a.reshape(lead, rows, cols) for a in (w, g, m, v)])
    return d.reshape(shape), nm.reshape(shape), nv.reshape(shape)


HBM = pl.BlockSpec(memory_space=pltpu.HBM)


def _coords():
    return lax.axis_index("x"), lax.axis_index("y"), lax.axis_index("c")


def _other_chips(x, y):
    return [(1 - x, y), (x, 1 - y), (1 - x, 1 - y)]


def _allgather_chips(shard, name, collective_id=None):
    r, cols = shard.shape
    half = r // 2
    quarter = half // 2

    def body(src_ref, out_ref, send_sems, recv_sems):
        x, y, c = _coords()
        sibling = (x, y, 1 - c)
        nx, ny, diag = (1 - x, y), (x, 1 - y), (1 - x, 1 - y)

        def piece(chip, core, lo, n):
            return out_ref.at[2 * chip[0] + chip[1], pl.ds(core * half + lo, n), :]

        def copy(k, dst, to, src=None):
            return pltpu.make_async_remote_copy(
                src_ref=dst if src is None else src, dst_ref=dst,
                send_sem=send_sems.at[k], recv_sem=recv_sems.at[k], device_id=to, device_id_type=MESH)

        me = (x, y)
        mine = src_ref.at[pl.ds(c * half, half), :]
        direct = [copy(0, piece(me, c, 0, half), (*nx, c), src=mine), copy(1, piece(me, c, 0, half), (*ny, c), src=mine)]
        for cp in direct:
            cp.start()
        arrivals = [piece(nx, c, 0, half), piece(ny, c, 0, half), piece(diag, c, 0, quarter),
                    piece(diag, c, quarter, quarter)]
        onward = [copy(2, piece(nx, c, 0, quarter), (*ny, c)), copy(3, piece(ny, c, quarter, quarter), (*nx, c))]
        to_sibling = [copy(4 + k, dst, sibling) for k, dst in enumerate(arrivals)]
        for k, dst in enumerate(arrivals):
            copy(k, dst, (x, y, c)).wait_recv()
            if k < 2:
                onward[k].start()
            to_sibling[k].start()
        from_sibling = [piece(nx, 1 - c, 0, half), piece(ny, 1 - c, 0, half), piece(diag, 1 - c, 0, quarter),
                        piece(diag, 1 - c, quarter, quarter)]
        for k, dst in enumerate(from_sibling):
            copy(4 + k, dst, (x, y, c)).wait_recv()
        for cp in direct + onward + to_sibling:
            cp.wait_send()

    out_shape = jax.ShapeDtypeStruct((N_CHIPS, r, cols), shard.dtype)
    sems = (pltpu.SemaphoreType.DMA((8,)), pltpu.SemaphoreType.DMA((8,)))
    if collective_id is None:
        return pl.pallas_call(body, name=name, out_shape=out_shape, in_specs=[HBM], out_specs=HBM,
                              scratch_shapes=list(sems))(shard)
    shard_ref = jax.new_ref(shard, memory_space=pltpu.MemorySpace.HBM)
    gathered_ref = jax.empty_ref(out_shape, memory_space=pltpu.MemorySpace.HBM)

    @_sequencer(name, collective_id, sems)
    def launch(send_sems, recv_sems):
        x, y, c = _coords()
        _handshake([(1 - x, y, c), (x, 1 - y, c), (x, y, 1 - c)])
        body(shard_ref, gathered_ref, send_sems, recv_sems)

    launch()
    return gathered_ref[...]


def _exchange_sibling_halves(g, name):
    n, r, cols = g.shape
    half = r // 2

    def body(g_ref, out_ref, send_sem, recv_sem):
        x, y, c = _coords()
        cp = pltpu.make_async_remote_copy(
            src_ref=g_ref.at[:, pl.ds((1 - c) * half, half), :], dst_ref=out_ref,
            send_sem=send_sem, recv_sem=recv_sem, device_id=(x, y, 1 - c), device_id_type=MESH)
        cp.start()
        cp.wait()

    return pl.pallas_call(
        body, name=name, out_shape=jax.ShapeDtypeStruct((n, half, cols), g.dtype),
        in_specs=[HBM], out_specs=HBM,
        scratch_shapes=[pltpu.SemaphoreType.DMA, pltpu.SemaphoreType.DMA],
    )(g)


def _scatter_to_chips(p, name):
    n, h, cols = p.shape

    def body(p_ref, out_ref, send_sems, recv_sems):
        x, y, c = _coords()
        me = 2 * x + y
        sends = []
        for j, (px, py) in enumerate(_other_chips(x, y)):
            sends.append(pltpu.make_async_remote_copy(
                src_ref=p_ref.at[2 * px + py], dst_ref=out_ref.at[me],
                send_sem=send_sems.at[j], recv_sem=recv_sems.at[j], device_id=(px, py, c), device_id_type=MESH))
        for cp in sends:
            cp.start()
        for j, (px, py) in enumerate(_other_chips(x, y)):
            pltpu.make_async_remote_copy(
                src_ref=p_ref.at[me], dst_ref=out_ref.at[2 * px + py],
                send_sem=send_sems.at[j], recv_sem=recv_sems.at[j], device_id=(px, py, c),
                device_id_type=MESH).wait_recv()
        for cp in sends:
            cp.wait_send()

    return pl.pallas_call(
        body, name=name, out_shape=jax.ShapeDtypeStruct((n, h, cols), p.dtype),
        in_specs=[HBM], out_specs=HBM,
        scratch_shapes=[pltpu.SemaphoreType.DMA((3,)), pltpu.SemaphoreType.DMA((3,))],
    )(p)


def _sequencer(name, collective_id, scratch_types):
    return pl.kernel(mesh=plsc.ScalarSubcoreMesh(axis_name="sequencer", num_cores=1), name=name,
                     scratch_types=scratch_types, compiler_params=pltpu.CompilerParams(collective_id=collective_id))


def _handshake(peers):
    barrier = pltpu.get_barrier_semaphore()
    for peer in peers:
        pl.semaphore_signal(barrier, inc=1, device_id=peer, device_id_type=MESH)
    pl.semaphore_wait(barrier, len(peers))


def _exchange_sibling_halves_async(g, name, collective_id):
    n, r, cols = g.shape
    half = r // 2
    g_ref = jax.new_ref(g, memory_space=pltpu.MemorySpace.HBM)
    out_ref = jax.empty_ref(jax.ShapeDtypeStruct((n, half, cols), g.dtype), memory_space=pltpu.MemorySpace.HBM)

    @_sequencer(name, collective_id, (pltpu.SemaphoreType.DMA, pltpu.SemaphoreType.DMA))
    def launch(send_sem, recv_sem):
        x, y, c = _coords()
        _handshake([(x, y, 1 - c)])
        cp = pltpu.make_async_remote_copy(
            src_ref=g_ref.at[:, pl.ds((1 - c) * half, half), :], dst_ref=out_ref,
            send_sem=send_sem, recv_sem=recv_sem, device_id=(x, y, 1 - c), device_id_type=MESH)
        cp.start()
        cp.wait()

    launch()
    return out_ref[...]


def _share_halves_async(v, name, collective_id):
    h = v.shape[0] // 2
    v_ref = jax.new_ref(v, memory_space=pltpu.MemorySpace.HBM)

    @_sequencer(name, collective_id, (pltpu.SemaphoreType.DMA, pltpu.SemaphoreType.DMA))
    def launch(send_sem, recv_sem):
        x, y, c = _coords()
        _handshake([(x, y, 1 - c)])
        cp = pltpu.make_async_remote_copy(
            src_ref=v_ref.at[pl.ds(c * h, h), :], dst_ref=v_ref.at[pl.ds(c * h, h), :],
            send_sem=send_sem, recv_sem=recv_sem, device_id=(x, y, 1 - c), device_id_type=MESH)
        cp.start()
        pltpu.make_async_remote_copy(
            src_ref=v_ref.at[pl.ds(c * h, h), :], dst_ref=v_ref.at[pl.ds((1 - c) * h, h), :],
            send_sem=send_sem, recv_sem=recv_sem, device_id=(x, y, 1 - c), device_id_type=MESH).wait_recv()
        cp.wait_send()

    launch()
    return v_ref[...]


def _scatter_to_chips_async(p, name, collective_id):
    p_ref = jax.new_ref(p, memory_space=pltpu.MemorySpace.HBM)
    out_ref = jax.empty_ref(jax.ShapeDtypeStruct(p.shape, p.dtype), memory_space=pltpu.MemorySpace.HBM)

    @_sequencer(name, collective_id, (pltpu.SemaphoreType.DMA((3,)), pltpu.SemaphoreType.DMA((3,))))
    def launch(send_sems, recv_sems):
        x, y, c = _coords()
        me = 2 * x + y
        _handshake([(px, py, c) for px, py in _other_chips(x, y)])
        sends = []
        for j, (px, py) in enumerate(_other_chips(x, y)):
            sends.append(pltpu.make_async_remote_copy(
                src_ref=p_ref.at[2 * px + py], dst_ref=out_ref.at[me],
                send_sem=send_sems.at[j], recv_sem=recv_sems.at[j], device_id=(px, py, c), device_id_type=MESH))
        for cp in sends:
            cp.start()
        for j, (px, py) in enumerate(_other_chips(x, y)):
            pltpu.make_async_remote_copy(
                src_ref=p_ref.at[me], dst_ref=out_ref.at[2 * px + py],
                send_sem=send_sems.at[j], recv_sem=recv_sems.at[j], device_id=(px, py, c),
                device_id_type=MESH).wait_recv()
        for cp in sends:
            cp.wait_send()

    launch()
    return out_ref[...]


def _share_halves(v, name):
    h = v.shape[0] // 2

    def body(v_ref, out_ref, send_sem, recv_sem):
        x, y, c = _coords()
        cp = pltpu.make_async_remote_copy(
            src_ref=v_ref.at[pl.ds(c * h, h), :], dst_ref=out_ref.at[pl.ds(c * h, h), :],
            send_sem=send_sem, recv_sem=recv_sem, device_id=(x, y, 1 - c), device_id_type=MESH)
        cp.start()
        pltpu.make_async_remote_copy(
            src_ref=v_ref.at[pl.ds(c * h, h), :], dst_ref=out_ref.at[pl.ds((1 - c) * h, h), :],
            send_sem=send_sem, recv_sem=recv_sem, device_id=(x, y, 1 - c), device_id_type=MESH).wait_recv()
        cp.wait_send()

    return pl.pallas_call(
        body, name=name, out_shape=jax.ShapeDtypeStruct(v.shape, v.dtype),
        in_specs=[HBM], out_specs=HBM, input_output_aliases={0: 0},
        scratch_shapes=[pltpu.SemaphoreType.DMA, pltpu.SemaphoreType.DMA],
    )(v)


def _allreduce_small(v, name):
    r, cols = v.shape

    def body(v_ref, out_ref, buf_ref, send_sems, recv_sems):
        x, y, c = _coords()
        me = 4 * x + 2 * y + c
        buf_ref[me] = v_ref[...]
        sends = []
        for k in range(1, N_DEV):
            px = 1 - x if k & 4 else x
            py = 1 - y if k & 2 else y
            pc = 1 - c if k & 1 else c
            sends.append(pltpu.make_async_remote_copy(
                src_ref=v_ref, dst_ref=buf_ref.at[me], send_sem=send_sems.at[k - 1], recv_sem=recv_sems.at[k - 1],
                device_id=(px, py, pc), device_id_type=MESH))
        for cp in sends:
            cp.start()
        for cp in sends:
            cp.wait()
        acc = buf_ref[0]
        for d in range(1, N_DEV):
            acc = acc + buf_ref[d]
        out_ref[...] = acc

    return pl.pallas_call(
        body, name=name, out_shape=jax.ShapeDtypeStruct((r, cols), F32),
        in_specs=[pl.BlockSpec(memory_space=pltpu.VMEM)], out_specs=pl.BlockSpec(memory_space=pltpu.VMEM),
        scratch_shapes=[pltpu.VMEM((N_DEV, r, cols), F32), pltpu.SemaphoreType.DMA((N_DEV - 1,)),
                        pltpu.SemaphoreType.DMA((N_DEV - 1,))],
    )(v)


def _add_sibling(g, from_sibling, core, name):
    n, h, cols = from_sibling.shape
    tr = _row_tile(h)
    steps = h // tr

    def body(core_ref, a_ref, b_ref, o_ref):
        o_ref[...] = (a_ref[...] + b_ref[...]).astype(o_ref.dtype)

    return pl.pallas_call(
        body, name=name, out_shape=jax.ShapeDtypeStruct(from_sibling.shape, jnp.bfloat16),
        grid_spec=pltpu.PrefetchScalarGridSpec(
            num_scalar_prefetch=1, grid=(n, steps),
            in_specs=[pl.BlockSpec((1, tr, cols), lambda s, i, core_ref: (s, core_ref[0] * steps + i, 0)),
                      pl.BlockSpec((1, tr, cols), lambda s, i, core_ref: (s, i, 0))],
            out_specs=pl.BlockSpec((1, tr, cols), lambda s, i, core_ref: (s, i, 0))),
        compiler_params=_params("parallel", "parallel"),
    )(core.reshape(1).astype(jnp.int32), g, from_sibling)


def _sum_slots(p, own, chip, core, name):
    n, r, cols = p.shape
    tr = _row_tile(r)
    steps = r // tr

    def body(core_ref, chip_ref, p_ref, own_ref, o_ref):
        parts = [jnp.where(chip_ref[0] == s, own_ref[0], p_ref[s]).astype(F32) for s in range(n)]
        o_ref[...] = ((parts[0] + parts[1]) + parts[2]) + parts[3]

    return pl.pallas_call(
        body, name=name, out_shape=jax.ShapeDtypeStruct((2 * r, cols), F32),
        grid_spec=pltpu.PrefetchScalarGridSpec(
            num_scalar_prefetch=2, grid=(steps,),
            in_specs=[pl.BlockSpec((n, tr, cols), lambda i, core_ref, chip_ref: (0, i, 0)),
                      pl.BlockSpec((1, tr, cols), lambda i, core_ref, chip_ref: (chip_ref[0], i, 0))],
            out_specs=pl.BlockSpec((tr, cols), lambda i, core_ref, chip_ref: (core_ref[0] * steps + i, 0))),
        compiler_params=_params("parallel"),
    )(core.reshape(1).astype(jnp.int32), chip.reshape(1).astype(jnp.int32), p, own)


PACK_COLS = 1024


def _pack_shards(parts):
    return jnp.concatenate([p.reshape(-1, PACK_COLS) for p in parts], axis=0)


def _unpack_shards(buf, shapes):
    out, row = [], 0
    for shp in shapes:
        nrows = math.prod(shp) // PACK_COLS
        out.append(buf[..., row:row + nrows, :].reshape(buf.shape[:-2] + tuple(shp)))
        row += nrows
    return out


def _local_step(x, target, w):
    t = lambda a: a.T
    g = {}
    h0 = _rms_fwd(x, w["norm_mix_g"][0], "rms_mix0")
    w_in_g, w_in_x = w["a_w_in"][:, :D_RNN], w["a_w_in"][:, D_RNN:]
    gate_br = _matmul([(h0, w_in_g)], F32, "mm_a_gate")
    x_br = _matmul([(h0, w_in_x)], F32, "mm_a_xbr")
    y_a, hs = _rglru_fwd(gate_br, x_br, w["a_conv_w"], w["a_conv_b"], w["a_w_r"], w["a_w_i"], w["a_b_r"],
                         w["a_b_i"], w["a_lambda"], "rglru_fwd")
    x1, h1 = _matmul([(y_a, w["a_w_out"])], F32, "mm_a_out", addend=x, norm_gain=w["norm_ffn_g"][0])
    fg0, fu0, act0 = _ffn_up(h1, w["ffn_w_gate"][0], w["ffn_w_up"][0], "ffn0_up")
    x2, h2 = _matmul([(act0, w["ffn_w_down"][0])], F32, "mm_f0_down", addend=x1, norm_gain=w["norm_mix_g"][1])
    qkv = _matmul([(h2, w["b_w_qkv"])], CD, "mm_b_qkv", out_lbm=True, tn=1024)
    o = _attn_fwd(qkv, "attn_fwd")
    x3, h3 = _matmul([(o, w["b_w_out"])], F32, "mm_b_out", a_lbm=True, addend=x2, norm_gain=w["norm_ffn_g"][1])
    fg1, fu1, act1 = _ffn_up(h3, w["ffn_w_gate"][1], w["ffn_w_up"][1], "ffn1_up")
    x4 = _matmul([(act1, w["ffn_w_down"][1])], F32, "mm_f1_down", addend=x3)
    loss, dx4, dx4c, g["final_g"] = _loss_head(x4, w["final_g"], target, "loss_head")

    def ffn_bwd(dx_out, dxc, h, x_in, fg, fu, act, layer, tag):
        dg, du = _ffn_dact(dxc, t(w["ffn_w_down"][layer]), fg, fu, "ffn_" + tag + "_dact")
        dwd = _matmul([(act, dxc)], F32, "mm_" + tag + "_dwd", trans_a=True)
        dwg = _matmul([(h, dg)], F32, "mm_" + tag + "_dwg", trans_a=True)
        dwu = _matmul([(h, du)], F32, "mm_" + tag + "_dwu", trans_a=True)
        dx_in, dx_in_c, dgain = _matmul([(dg, t(w["ffn_w_gate"][layer])), (du, t(w["ffn_w_up"][layer]))], F32,
                                        "mm_" + tag + "_dh", norm_bwd=(x_in, w["norm_ffn_g"][layer], dx_out))
        return dx_in, dx_in_c, dgain, dwg, dwu, dwd

    dx3, dx3c, dgf1, dwg1, dwu1, dwd1 = ffn_bwd(dx4, dx4c, h3, x3, fg1, fu1, act1, 1, "f1")
    do = _matmul([(dx3c, t(w["b_w_out"]))], F32, "mm_b_do", out_lbm=True, tn=1024)
    g["b_w_out"] = _matmul([(o, dx3c)], F32, "mm_b_dwout", trans_a=True, a_lbm=True)
    dq, dk, dv = _attn_bwd(qkv, o, do, "attn_bwd")
    wq_t = t(w["b_w_qkv"])
    parts = (dq, dk, dv)
    g["b_w_qkv"] = jnp.concatenate(
        [_matmul([(h2, p)], F32, "mm_b_dwqkv%d" % n, trans_a=True, b_lbm=True) for n, p in enumerate(parts)], axis=1)
    dx2, dx2c, dgm1 = _matmul([(p, wq_t[n * D_MODEL:(n + 1) * D_MODEL]) for n, p in enumerate(parts)], F32, "mm_b_dh",
                              a_lbm=True, norm_bwd=(x2, w["norm_mix_g"][1], dx3))
    dx1, dx1c, dgf0, dwg0, dwu0, dwd0 = ffn_bwd(dx2, dx2c, h1, x1, fg0, fu0, act0, 0, "f0")
    dy_a = _matmul([(dx1c, t(w["a_w_out"]))], F32, "mm_a_dy")
    g["a_w_out"] = _matmul([(y_a, dx1c)], F32, "mm_a_dwout", trans_a=True)
    wrt = jnp.swapaxes(w["a_w_r"], 1, 2)
    wit = jnp.swapaxes(w["a_w_i"], 1, 2)
    (dgate, dxbr, g["a_conv_w"], g["a_conv_b"], g["a_b_r"], g["a_b_i"], g["a_lambda"], g["a_w_r"],
     g["a_w_i"]) = _rglru_bwd(dy_a, gate_br, x_br, hs, w["a_conv_w"], w["a_conv_b"], w["a_w_r"], w["a_w_i"], wrt, wit,
                              w["a_b_r"], w["a_b_i"], w["a_lambda"], "rglru_bwd")
    g["a_w_in"] = jnp.concatenate([_matmul([(h0, dgate)], F32, "mm_a_dwin_g", trans_a=True),
                                   _matmul([(h0, dxbr)], F32, "mm_a_dwin_x", trans_a=True)], axis=1)
    dx0, _, dgm0 = _matmul([(dgate, t(w_in_g)), (dxbr, t(w_in_x))], F32, "mm_a_dh",
                           norm_bwd=(x, w["norm_mix_g"][0], dx1))
    g["norm_mix_g"] = jnp.concatenate([dgm0, dgm1], axis=0)
    g["norm_ffn_g"] = jnp.concatenate([dgf0, dgf1], axis=0)
    g["ffn_w_gate"] = [dwg0, dwg1]
    g["ffn_w_up"] = [dwu0, dwu1]
    g["ffn_w_down"] = [dwd0, dwd1]
    return loss, dx0, g


WEIGHTS = ["norm_mix_g", "norm_ffn_g", "a_w_in", "a_conv_w", "a_conv_b", "a_w_r", "a_b_r", "a_w_i", "a_b_i",
           "a_lambda", "a_w_out", "b_w_qkv", "b_w_out", "ffn_w_gate", "ffn_w_up", "ffn_w_down", "final_g"]
BIG = [("a_w_in", 2), ("a_w_r", 2), ("a_w_i", 2), ("a_w_out", 1), ("b_w_qkv", 2), ("b_w_out", 1),
       ("ffn_w_gate", 2), ("ffn_w_up", 2), ("ffn_w_down", 1)]
LAYER1 = ["b_w_qkv", "b_w_out", "ffn_w_gate", "ffn_w_up", "ffn_w_down"]
LAYER0 = ["a_w_in", "a_w_r", "a_w_i", "a_w_out", "ffn_w_gate", "ffn_w_up", "ffn_w_down"]
RS_COLLECTIVE_IDS = {"chips1": 3, "chips0": 4, "sibling1": 5, "share1": 6}
GATHER_COLLECTIVE_ID = 7
SMALL = ["norm_mix_g", "norm_ffn_g", "a_conv_w", "a_conv_b", "a_b_r", "a_b_i", "a_lambda", "final_g"]


def _split_chips(full, axis):
    return jnp.stack(jnp.split(full, N_CHIPS, axis=axis))


def _step(x, target, weights, moments_m, moments_v):
    chip = 2 * lax.axis_index("x") + lax.axis_index("y")
    core = lax.axis_index("c")
    axis_of = dict(BIG)
    full = {}
    for group, layer, collective_id in ((LAYER0, 0, None), (LAYER1, 1, GATHER_COLLECTIVE_ID)):
        shards = [weights[n][layer % weights[n].shape[0]].astype(CD) for n in group]
        gathered = _allgather_chips(_pack_shards(shards), "allgather_weights%d" % layer, collective_id)
        for n, own, stack in zip(group, shards, _unpack_shards(gathered, [sh.shape for sh in shards])):
            joined = jnp.concatenate([jnp.where(chip == s, own, stack[s]) for s in range(N_CHIPS)],
                                     axis=axis_of[n] - 1)
            full.setdefault(n, {})[layer] = joined
    full = {n: (v[0] if n.startswith("a_") else v[1] if n.startswith("b_") else [v[0], v[1]]) for n, v in full.items()}
    cw_rows = jnp.zeros((N_CHIPS, CONV_W, RG_BW), F32)
    cw_rows = lax.dynamic_update_slice(cw_rows, jnp.where(core == 0, weights["a_conv_w"], 0.0), (chip, 0, 0))
    cw_all = _allreduce_small(cw_rows.reshape(-1, LANES), "allgather_conv_w").reshape(N_CHIPS, CONV_W, RG_BW)
    full["a_conv_w"] = jnp.concatenate([cw_all[s] for s in range(N_CHIPS)], axis=1)
    for n in ("norm_mix_g", "norm_ffn_g", "final_g"):
        full[n] = weights[n]
    for n in ("a_conv_b", "a_b_r", "a_b_i", "a_lambda"):
        full[n] = weights[n]
    loss, dx, grads = _local_step(x[0], target[0], full)
    small_parts = [grads[n].reshape(-1) for n in SMALL] + [loss.reshape(-1)]
    sizes = [p.shape[0] for p in small_parts]
    small = _allreduce_small(jnp.concatenate(small_parts).reshape(-1, LANES), "allreduce_small").reshape(-1)
    red, pos = {}, 0
    for n, sz in zip(SMALL + ["loss"], sizes):
        red[n] = small[pos:pos + sz]
        pos += sz
    loss_out = red["loss"][0]
    g_out = {}
    for n in SMALL:
        if n == "a_conv_w":
            g_out[n] = lax.dynamic_slice(red[n].reshape(CONV_W, D_RNN), (0, chip * RG_BW), (CONV_W, RG_BW)).reshape(
                weights[n].shape)
        else:
            g_out[n] = red[n].reshape(weights[n].shape)
    axis_of = dict(BIG)
    pieces = {}
    for group, layer, tag in ((LAYER1, 1, "1"), (LAYER0, 0, "0")):
        stacks, shapes = [], []
        for n in group:
            per_layer = isinstance(grads[n], list)
            gfull = grads[n][layer] if per_layer else grads[n]
            shard_shape = weights[n].shape[1:]
            gfull = gfull.reshape((1,) + gfull.shape)
            stacks.append(_split_chips(gfull, axis_of[n]).reshape(N_CHIPS, -1, PACK_COLS))
            shapes.append((1,) + tuple(shard_shape))
        gbuf = jnp.concatenate(stacks, axis=1)
        if layer == 1:
            from_sibling = _exchange_sibling_halves_async(gbuf, "rs_sibling" + tag, RS_COLLECTIVE_IDS["sibling1"])
        else:
            from_sibling = _exchange_sibling_halves(gbuf, "rs_sibling" + tag)
        chip_partial = _add_sibling(gbuf, from_sibling, core, "rs_add" + tag)
        from_chips = _scatter_to_chips_async(chip_partial, "rs_chips" + tag, RS_COLLECTIVE_IDS["chips" + tag])
        halves = _sum_slots(from_chips, chip_partial, chip, core, "rs_sum" + tag)
        if layer == 1:
            reduced = _share_halves_async(halves, "rs_share" + tag, RS_COLLECTIVE_IDS["share1"])
        else:
            reduced = _share_halves(halves, "rs_share" + tag)
        for n, piece in zip(group, _unpack_shards(reduced, shapes)):
            pieces.setdefault(n, {})[layer] = piece
    for n, _ in BIG:
        layers = pieces[n]
        g_out[n] = jnp.concatenate([layers[k] for k in sorted(layers)], axis=0)
    outs_g, outs_d, outs_m, outs_v = [], [], [], []
    for n in WEIGHTS:
        d, nm, nv = _adamw(weights[n], g_out[n], moments_m[n], moments_v[n], "adamw_" + n)
        outs_g.append(g_out[n])
        outs_d.append(d)
        outs_m.append(nm)
        outs_v.append(nv)
    return (loss_out, dx[None], *outs_g, *outs_d, *outs_m, *outs_v)


def kernel(x, norm_mix_g, norm_ffn_g, a_w_in, a_conv_w, a_conv_b, a_w_r, a_b_r, a_w_i, a_b_i, a_lambda, a_w_out, b_w_qkv, b_w_out, ffn_w_gate, ffn_w_up, ffn_w_down, final_g, loss_target, m_norm_mix_g, m_norm_ffn_g, m_a_w_in, m_a_conv_w, m_a_conv_b, m_a_w_r, m_a_b_r, m_a_w_i, m_a_b_i, m_a_lambda, m_a_w_out, m_b_w_qkv, m_b_w_out, m_ffn_w_gate, m_ffn_w_up, m_ffn_w_down, m_final_g, v_norm_mix_g, v_norm_ffn_g, v_a_w_in, v_a_conv_w, v_a_conv_b, v_a_w_r, v_a_b_r, v_a_w_i, v_a_b_i, v_a_lambda, v_a_w_out, v_b_w_qkv, v_b_w_out, v_ffn_w_gate, v_ffn_w_up, v_ffn_w_down, v_final_g):
    ws = [norm_mix_g, norm_ffn_g, a_w_in, a_conv_w, a_conv_b, a_w_r, a_b_r, a_w_i, a_b_i, a_lambda, a_w_out, b_w_qkv,
          b_w_out, ffn_w_gate, ffn_w_up, ffn_w_down, final_g]
    ms = [m_norm_mix_g, m_norm_ffn_g, m_a_w_in, m_a_conv_w, m_a_conv_b, m_a_w_r, m_a_b_r, m_a_w_i, m_a_b_i, m_a_lambda,
          m_a_w_out, m_b_w_qkv, m_b_w_out, m_ffn_w_gate, m_ffn_w_up, m_ffn_w_down, m_final_g]
    vs = [v_norm_mix_g, v_norm_ffn_g, v_a_w_in, v_a_conv_w, v_a_conv_b, v_a_w_r, v_a_b_r, v_a_w_i, v_a_b_i, v_a_lambda,
          v_a_w_out, v_b_w_qkv, v_b_w_out, v_ffn_w_gate, v_ffn_w_up, v_ffn_w_down, v_final_g]
    return _step(x, loss_target, dict(zip(WEIGHTS, ws)), dict(zip(WEIGHTS, ms)), dict(zip(WEIGHTS, vs)))
```

```python
import functools
import math

import jax
import jax.numpy as jnp
from jax import lax
from jax.experimental import pallas as pl
from jax.experimental.pallas import tpu as pltpu
from jax.experimental.pallas import tpu_sc as plsc

F32 = jnp.float32
CD = jnp.bfloat16

D_MODEL = 1024
D_RNN = 1024
RG_BLOCKS = 4
RG_BW = 256
CONV_W = 4
RG_C = 8.0
SB_HEADS = 16
SB_HEAD_DIM = 64
D_FF = 2816
RMS_EPS = 1e-6
N_CHIPS = 4
N_DEV = 8

ADAM_LR = 0.001
ADAM_B1 = 0.9
ADAM_B2 = 0.999
ADAM_EPS = 1e-08
ADAM_WD = 0.01
ADAM_STEP = 10

LANES = 128
VMEM_LIMIT = 56 * 1024 * 1024
MESH = pl.DeviceIdType.MESH


def _params(*sem):
    return pltpu.CompilerParams(dimension_semantics=sem, vmem_limit_bytes=VMEM_LIMIT)


def _pick(n, prefs):
    for p in prefs:
        if n % p == 0:
            return p
    return n


def _row_tile(rows):
    return max(d for d in range(16, 1025, 16) if rows % d == 0)


def _matmul(pairs, out_dtype, name, *, trans_a=False, a_lbm=False, b_lbm=False, out_lbm=False, addend=None,
            tm=512, tn=None, tk=None, norm_gain=None, norm_bwd=None):
    a0, b0 = pairs[0]
    if trans_a:
        kdim = a0.shape[1] if a_lbm else a0.shape[0]
        m = a0.shape[0] * LANES if a_lbm else a0.shape[1]
    else:
        m = a0.shape[1] if a_lbm else a0.shape[0]
        kdim = a0.shape[0] * LANES if a_lbm else a0.shape[1]
    n = b0.shape[0] * LANES if b_lbm else b0.shape[1]
    tm = _pick(m, (tm, 1408, 256, 128))
    tn = tn or _pick(n, (1408, 1024, 768, 512, 256, 128))
    tk = tk or _pick(kdim, (1024, 1408, 512, 256, 128))
    nk = kdim // tk
    npair = len(pairs)

    def cat(ref):
        return jnp.concatenate([ref[p] for p in range(ref.shape[0])], axis=-1)

    def body(*refs):
        ins = refs[: 2 * npair]
        pos = 2 * npair
        add_ref = None
        if addend is not None:
            add_ref = refs[pos]
            pos += 1
        gain_ref = x_ref = dxin_ref = None
        if norm_gain is not None:
            gain_ref = refs[pos]
            pos += 1
        if norm_bwd is not None:
            x_ref, gain_ref, dxin_ref = refs[pos:pos + 3]
            pos += 3
        o_ref = refs[pos]
        extra_out = refs[pos + 1:-1]
        acc_ref = refs[-1]
        k = pl.program_id(2)

        @pl.when(k == 0)
        def _():
            acc_ref[...] = jnp.zeros_like(acc_ref)

        if norm_bwd is not None:
            @pl.when((k == 0) & (pl.program_id(0) == 0))
            def _():
                extra_out[1][...] = jnp.zeros_like(extra_out[1])

        acc = acc_ref[...]
        for p in range(npair):
            a = (cat(ins[2 * p]) if a_lbm else ins[2 * p][...]).astype(CD)
            b = (cat(ins[2 * p + 1]) if b_lbm else ins[2 * p + 1][...]).astype(CD)
            dims = (((0,), (0,)), ((), ())) if trans_a else (((1,), (0,)), ((), ()))
            acc = acc + lax.dot_general(a, b, dims, preferred_element_type=F32)
        acc_ref[...] = acc

        @pl.when(k == nk - 1)
        def _():
            res = acc_ref[...]
            if add_ref is not None:
                res = res + add_ref[...]
            if norm_gain is not None:
                rinv = lax.rsqrt(jnp.mean(res * res, axis=-1, keepdims=True) + RMS_EPS)
                extra_out[0][...] = (res * rinv * gain_ref[...]).astype(CD)
            if norm_bwd is not None:
                xv = x_ref[...]
                rinv = lax.rsqrt(jnp.mean(xv * xv, axis=-1, keepdims=True) + RMS_EPS)
                nrm = xv * rinv
                dn = res * gain_ref[...]
                extra_out[1][...] += jnp.sum(res * nrm, axis=0, keepdims=True)
                res = dxin_ref[...] + rinv * (dn - nrm * jnp.mean(dn * nrm, axis=-1, keepdims=True))
                extra_out[0][...] = res.astype(CD)
            res = res.astype(out_dtype)
            if out_lbm:
                for p in range(tn // LANES):
                    o_ref[p] = res[:, p * LANES:(p + 1) * LANES]
            else:
                o_ref[...] = res

    if trans_a:
        a_spec = (pl.BlockSpec((tm // LANES, tk, LANES), lambda i, j, k: (i, k, 0)) if a_lbm
                  else pl.BlockSpec((tk, tm), lambda i, j, k: (k, i)))
    else:
        a_spec = (pl.BlockSpec((tk // LANES, tm, LANES), lambda i, j, k: (k, i, 0)) if a_lbm
                  else pl.BlockSpec((tm, tk), lambda i, j, k: (i, k)))
    b_spec = (pl.BlockSpec((tn // LANES, tk, LANES), lambda i, j, k: (j, k, 0)) if b_lbm
              else pl.BlockSpec((tk, tn), lambda i, j, k: (k, j)))
    in_specs = []
    args = []
    for a, b in pairs:
        in_specs += [a_spec, b_spec]
        args += [a, b]
    if addend is not None:
        in_specs.append(pl.BlockSpec((tm, tn), lambda i, j, k: (i, j)))
        args.append(addend)
    tile = pl.BlockSpec((tm, tn), lambda i, j, k: (i, j))
    vec = pl.BlockSpec((1, tn), lambda i, j, k: (0, j))
    if out_lbm:
        out_shape = jax.ShapeDtypeStruct((n // LANES, m, LANES), out_dtype)
        out_spec = pl.BlockSpec((tn // LANES, tm, LANES), lambda i, j, k: (j, i, 0))
    else:
        out_shape = jax.ShapeDtypeStruct((m, n), out_dtype)
        out_spec = tile
    sem = ("parallel", "parallel", "arbitrary")
    if norm_gain is not None or norm_bwd is not None:
        assert tn == n and not out_lbm, "the norm needs whole rows in one tile"
        out_shape, out_spec = [out_shape, jax.ShapeDtypeStruct((m, n), CD)], [out_spec, tile]
    if norm_gain is not None:
        in_specs.append(vec)
        args.append(norm_gain.reshape(1, n))
    if norm_bwd is not None:
        x_in, gain, dx_in = norm_bwd
        in_specs += [tile, vec, tile]
        args += [x_in, gain.reshape(1, n), dx_in]
        out_shape.append(jax.ShapeDtypeStruct((1, n), F32))
        out_spec.append(vec)
        sem = ("arbitrary", "arbitrary", "arbitrary")
    return pl.pallas_call(
        body, name=name, out_shape=out_shape, grid=(m // tm, n // tn, nk),
        in_specs=in_specs, out_specs=out_spec,
        scratch_shapes=[pltpu.VMEM((tm, tn), F32)],
        compiler_params=_params(*sem),
    )(*args)


ROW_BLOCK = 256


def _rms_fwd(x, g, name):
    s, d = x.shape

    def body(x_ref, g_ref, h_ref):
        xv = x_ref[...]
        rinv = lax.rsqrt(jnp.mean(xv * xv, axis=-1, keepdims=True) + RMS_EPS)
        h_ref[...] = (xv * rinv * g_ref[...]).astype(CD)

    return pl.pallas_call(
        body, name=name, out_shape=jax.ShapeDtypeStruct((s, d), CD), grid=(s // ROW_BLOCK,),
        in_specs=[pl.BlockSpec((ROW_BLOCK, d), lambda i: (i, 0)), pl.BlockSpec((1, d), lambda i: (0, 0))],
        out_specs=pl.BlockSpec((ROW_BLOCK, d), lambda i: (i, 0)),
        compiler_params=_params("parallel"),
    )(x, g.reshape(1, d))


def _loss_head(x, g, target, name):
    s, d = x.shape

    def body(x_ref, g_ref, t_ref, loss_ref, dx_ref, dxc_ref, dg_ref):
        @pl.when(pl.program_id(0) == 0)
        def _():
            dg_ref[...] = jnp.zeros_like(dg_ref)
            loss_ref[...] = jnp.zeros_like(loss_ref)

        xv = x_ref[...]
        gv = g_ref[...]
        rinv = lax.rsqrt(jnp.mean(xv * xv, axis=-1, keepdims=True) + RMS_EPS)
        nrm = xv * rinv
        err = nrm * gv - t_ref[...]
        loss_ref[...] += 0.5 * jnp.sum(jnp.mean(err * err, axis=-1, keepdims=True), axis=0, keepdims=True)
        dy = err * (1.0 / d)
        dn = dy * gv
        dx = rinv * (dn - nrm * jnp.mean(dn * nrm, axis=-1, keepdims=True))
        dx_ref[...] = dx
        dxc_ref[...] = dx.astype(CD)
        dg_ref[...] += jnp.sum(dy * nrm, axis=0, keepdims=True)

    row = pl.BlockSpec((ROW_BLOCK, d), lambda i: (i, 0))
    vec = pl.BlockSpec((1, d), lambda i: (0, 0))
    return pl.pallas_call(
        body, name=name,
        out_shape=(jax.ShapeDtypeStruct((1, LANES), F32), jax.ShapeDtypeStruct((s, d), F32),
                   jax.ShapeDtypeStruct((s, d), CD), jax.ShapeDtypeStruct((1, d), F32)),
        grid=(s // ROW_BLOCK,), in_specs=[row, vec, row],
        out_specs=(pl.BlockSpec((1, LANES), lambda i: (0, 0)), row, row, vec),
        compiler_params=_params("arbitrary"),
    )(x, g.reshape(1, d), target)


def _sigmoid(z):
    return 1.0 / (1.0 + jnp.exp(-z))


FFN_TM = 512
FFN_TN = 1408


def _ffn_up(h, wg, wu, name):
    s, d = h.shape
    f = wg.shape[1]
    tm = _pick(s, (FFN_TM, 256))

    def body(h_ref, wg_ref, wu_ref, g_ref, u_ref, a_ref):
        hv = h_ref[...]
        gv = jnp.dot(hv, wg_ref[...], preferred_element_type=F32)
        uv = jnp.dot(hv, wu_ref[...], preferred_element_type=F32)
        g_ref[...] = gv
        u_ref[...] = uv
        a_ref[...] = (gv * _sigmoid(gv) * uv).astype(CD)

    a_spec = pl.BlockSpec((tm, d), lambda i, j: (i, 0))
    w_spec = pl.BlockSpec((d, FFN_TN), lambda i, j: (0, j))
    o_spec = pl.BlockSpec((tm, FFN_TN), lambda i, j: (i, j))
    return pl.pallas_call(
        body, name=name,
        out_shape=(jax.ShapeDtypeStruct((s, f), F32), jax.ShapeDtypeStruct((s, f), F32),
                   jax.ShapeDtypeStruct((s, f), CD)),
        grid=(s // tm, f // FFN_TN), in_specs=[a_spec, w_spec, w_spec], out_specs=(o_spec, o_spec, o_spec),
        compiler_params=_params("parallel", "parallel"),
    )(h, wg, wu)


def _ffn_dact(dxc, wd_t, g, u, name):
    s, d = dxc.shape
    f = wd_t.shape[1]
    tm = _pick(s, (FFN_TM, 256))

    def body(dx_ref, w_ref, g_ref, u_ref, dg_ref, du_ref):
        da = jnp.dot(dx_ref[...], w_ref[...], preferred_element_type=F32)
        gv = g_ref[...]
        sg = _sigmoid(gv)
        silu = gv * sg
        dg_ref[...] = (da * u_ref[...] * (sg + silu * (1.0 - sg))).astype(CD)
        du_ref[...] = (da * silu).astype(CD)

    a_spec = pl.BlockSpec((tm, d), lambda i, j: (i, 0))
    w_spec = pl.BlockSpec((d, FFN_TN), lambda i, j: (0, j))
    o_spec = pl.BlockSpec((tm, FFN_TN), lambda i, j: (i, j))
    return pl.pallas_call(
        body, name=name,
        out_shape=(jax.ShapeDtypeStruct((s, f), CD), jax.ShapeDtypeStruct((s, f), CD)),
        grid=(s // tm, f // FFN_TN), in_specs=[a_spec, w_spec, o_spec, o_spec], out_specs=(o_spec, o_spec),
        compiler_params=_params("parallel", "parallel"),
    )(dxc, wd_t, g, u)


TIME_BLOCK = 256
SUBLANES = 8
GELU_C = math.sqrt(2.0 / math.pi)
GELU_A = 0.044715


def _gelu(x):
    return 0.5 * x * (1.0 + jnp.tanh(GELU_C * (x + GELU_A * x * x * x)))


def _gelu_grad(x):
    t = jnp.tanh(GELU_C * (x + GELU_A * x * x * x))
    return 0.5 * (1.0 + t) + 0.5 * x * (1.0 - t * t) * GELU_C * (1.0 + 3.0 * GELU_A * x * x)


def _neg_expm1(x):
    series = -x * (1.0 + x * (0.5 + x * (1.0 / 6.0 + x * (1.0 / 24.0))))
    return jnp.where(x > -0.05, series, 1.0 - jnp.exp(x))


def _log_sigmoid(x):
    return jnp.minimum(x, 0.0) - jnp.log1p(jnp.exp(-jnp.abs(x)))


def _shift_down(x, tail, s):
    if s == 0:
        return x
    ext = jnp.concatenate([tail, x], axis=0)
    return pltpu.roll(ext, s, axis=0)[SUBLANES:]


def _shift_up(x, head, s):
    if s == 0:
        return x
    n = x.shape[0]
    ext = jnp.concatenate([x, head], axis=0)
    return pltpu.roll(ext, n + SUBLANES - s, axis=0)[:n]


def _rg_gates(xbr, tail, cw_ref, cb, wr, wi, br, bi, ls):
    taps = [_shift_down(xbr, tail, CONV_W - 1 - k) for k in range(CONV_W)]
    xc = cb
    for k in range(CONV_W):
        xc = xc + cw_ref[pl.ds(k, 1), :] * taps[k]
    xcd = xc.astype(CD)
    r = _sigmoid(jnp.dot(xcd, wr, preferred_element_type=F32) + br)
    i = _sigmoid(jnp.dot(xcd, wi, preferred_element_type=F32) + bi)
    log_a = RG_C * r * ls
    a = jnp.exp(log_a)
    mult = jnp.sqrt(jnp.maximum(_neg_expm1(2.0 * log_a), 0.0))
    return taps, xc, r, i, log_a, a, mult


def _scan8_fwd(a, u):
    row = lax.broadcasted_iota(jnp.int32, a.shape, 0)
    for d in (1, 2, 4):
        a_s = pltpu.roll(a, d, axis=0)
        u_s = pltpu.roll(u, d, axis=0)
        m = row >= d
        u = jnp.where(m, a * u_s + u, u)
        a = jnp.where(m, a * a_s, a)
    return a, u


def _scan8_bwd(b, u):
    row = lax.broadcasted_iota(jnp.int32, b.shape, 0)
    for d in (1, 2, 4):
        b_s = pltpu.roll(b, SUBLANES - d, axis=0)
        u_s = pltpu.roll(u, SUBLANES - d, axis=0)
        m = row < SUBLANES - d
        u = jnp.where(m, b * u_s + u, u)
        b = jnp.where(m, b * b_s, b)
    return b, u


def _rglru_fwd(gate_br, x_br, cw, cb, wr, wi, br, bi, lam, name):
    s, c = x_br.shape
    nt = s // TIME_BLOCK
    tb, cbw = TIME_BLOCK, RG_BW
    groups = tb // SUBLANES

    def body(g_ref, x_ref, tail_ref, cw_ref, cb_ref, wr_ref, wi_ref, br_ref, bi_ref, lam_ref,
             y_ref, hs_ref, carry_ref, a_scr, u_scr):
        t = pl.program_id(1)

        @pl.when(t == 0)
        def _():
            carry_ref[...] = jnp.zeros_like(carry_ref)

        tail = jnp.where(t > 0, tail_ref[...], 0.0)
        ls = _log_sigmoid(lam_ref[...])
        _, xc, _, i, _, a, mult = _rg_gates(x_ref[...], tail, cw_ref, cb_ref[...], wr_ref[0], wi_ref[0],
                                            br_ref[...], bi_ref[...], ls)
        a_scr[...] = a
        u_scr[...] = mult * (i * xc)
        carry = carry_ref[...]
        for gi in range(groups):
            rows = pl.ds(gi * SUBLANES, SUBLANES)
            pa, hl = _scan8_fwd(a_scr[rows, :], u_scr[rows, :])
            hs_ref[rows, :] = hl + pa * carry
            carry = hs_ref[pl.ds(gi * SUBLANES + SUBLANES - 1, 1), :]
        carry_ref[...] = carry
        y_ref[...] = (hs_ref[...] * _gelu(g_ref[...])).astype(CD)

    blk = pl.BlockSpec((tb, cbw), lambda n, t: (t, n))
    tail = pl.BlockSpec((SUBLANES, cbw), lambda n, t: (jnp.maximum(t * groups - 1, 0), n))
    vec = pl.BlockSpec((1, cbw), lambda n, t: (0, n))
    wblk = pl.BlockSpec((1, cbw, cbw), lambda n, t: (n, 0, 0))
    return pl.pallas_call(
        body, name=name,
        out_shape=(jax.ShapeDtypeStruct((s, c), CD), jax.ShapeDtypeStruct((s, c), F32)),
        grid=(RG_BLOCKS, nt),
        in_specs=[blk, blk, tail, pl.BlockSpec((CONV_W, cbw), lambda n, t: (0, n)), vec, wblk, wblk, vec, vec, vec],
        out_specs=(blk, blk),
        scratch_shapes=[pltpu.VMEM((1, cbw), F32), pltpu.VMEM((tb, cbw), F32), pltpu.VMEM((tb, cbw), F32)],
        compiler_params=_params("parallel", "arbitrary"),
    )(gate_br, x_br, x_br, cw, cb, wr, wi, br, bi, lam)


def _rglru_bwd(dy, gate_br, x_br, hs, cw, cb, wr, wi, wrt, wit, br, bi, lam, name):
    s, c = x_br.shape
    nt = s // TIME_BLOCK
    tb, cbw = TIME_BLOCK, RG_BW
    groups = tb // SUBLANES

    def body(dy_ref, g_ref, x_ref, tail_ref, hs_ref, hprev_ref, cw_ref, cb_ref, wr_ref, wi_ref, wrt_ref, wit_ref,
             br_ref, bi_ref, lam_ref,
             dg_ref, dx_ref, dcw_ref, dcb_ref, dbr_ref, dbi_ref, dlam_ref, dwr_ref, dwi_ref,
             carry_ref, head_ref, b_scr, u_scr, dh_scr):
        tr = pl.program_id(1)
        first_block = tr == nt - 1

        @pl.when(tr == 0)
        def _():
            carry_ref[...] = jnp.zeros_like(carry_ref)
            head_ref[...] = jnp.zeros_like(head_ref)
            for ref in (dcw_ref, dcb_ref, dbr_ref, dbi_ref, dlam_ref, dwr_ref, dwi_ref):
                ref[...] = jnp.zeros_like(ref)

        tail = jnp.where(first_block, 0.0, tail_ref[...])
        lam_v = lam_ref[...]
        ls = _log_sigmoid(lam_v)
        taps, xc, r, i, log_a, a, mult = _rg_gates(x_ref[...], tail, cw_ref, cb_ref[...], wr_ref[0], wi_ref[0],
                                                   br_ref[...], bi_ref[...], ls)
        gate_v = g_ref[...]
        dyv = dy_ref[...]
        hsv = hs_ref[...]
        dg_ref[...] = (dyv * hsv * _gelu_grad(gate_v)).astype(CD)

        row = lax.broadcasted_iota(jnp.int32, a.shape, 0)
        b_scr[...] = jnp.where(row == tb - 1, 1.0, pltpu.roll(a, tb - 1, axis=0))
        u_scr[...] = dyv * _gelu(gate_v)
        carry = carry_ref[...]
        for gi in reversed(range(groups)):
            rows = pl.ds(gi * SUBLANES, SUBLANES)
            pb, gl = _scan8_bwd(b_scr[rows, :], u_scr[rows, :])
            dh_scr[rows, :] = gl + pb * carry
            carry = dh_scr[pl.ds(gi * SUBLANES, 1), :]
        dh = dh_scr[...]
        carry_ref[...] = carry * jnp.sum(jnp.where(row == 0, a, 0.0), axis=0, keepdims=True)

        hprev_tail = jnp.where(first_block, 0.0, hprev_ref[...])
        h_prev = _shift_down(hsv, hprev_tail, 1)
        da = dh * h_prev
        ixc = i * xc
        dmult = dh * ixc
        di = dh * mult * xc
        dxc = dh * mult * i
        a2 = a * a
        dlog_a = da * a - dmult * a2 / mult
        dpre_r = (dlog_a * (RG_C * ls)) * r * (1.0 - r)
        dpre_i = di * i * (1.0 - i)
        dlam_ref[...] += jnp.sum(dlog_a * r, axis=0, keepdims=True) * (RG_C * _sigmoid(-lam_v))
        dbr_ref[...] += jnp.sum(dpre_r, axis=0, keepdims=True)
        dbi_ref[...] += jnp.sum(dpre_i, axis=0, keepdims=True)
        xcd = xc.astype(CD)
        dprc = dpre_r.astype(CD)
        dpic = dpre_i.astype(CD)
        tn_dims = (((0,), (0,)), ((), ()))
        dwr_ref[0] += lax.dot_general(xcd, dprc, tn_dims, preferred_element_type=F32)
        dwi_ref[0] += lax.dot_general(xcd, dpic, tn_dims, preferred_element_type=F32)
        dxc = dxc + jnp.dot(dprc, wrt_ref[0], preferred_element_type=F32) + jnp.dot(dpic, wit_ref[0],
                                                                                    preferred_element_type=F32)
        dcb_ref[...] += jnp.sum(dxc, axis=0, keepdims=True)
        for k in range(CONV_W):
            dcw_ref[pl.ds(k, 1), :] += jnp.sum(dxc * taps[k], axis=0, keepdims=True)
        head = head_ref[...]
        dxb = jnp.zeros_like(dxc)
        for sft in range(CONV_W):
            dxb = dxb + cw_ref[pl.ds(CONV_W - 1 - sft, 1), :] * _shift_up(dxc, head, sft)
        dx_ref[...] = dxb.astype(CD)
        head_ref[...] = dxc[0:SUBLANES, :]

    blk = pl.BlockSpec((tb, cbw), lambda n, t: (nt - 1 - t, n))
    tail = pl.BlockSpec((SUBLANES, cbw), lambda n, t: (jnp.maximum((nt - 1 - t) * groups - 1, 0), n))
    vec = pl.BlockSpec((1, cbw), lambda n, t: (0, n))
    cwb = pl.BlockSpec((CONV_W, cbw), lambda n, t: (0, n))
    wblk = pl.BlockSpec((1, cbw, cbw), lambda n, t: (n, 0, 0))
    vshape = jax.ShapeDtypeStruct((1, c), F32)
    wshape = jax.ShapeDtypeStruct((RG_BLOCKS, cbw, cbw), F32)
    return pl.pallas_call(
        body, name=name,
        out_shape=(jax.ShapeDtypeStruct((s, c), CD), jax.ShapeDtypeStruct((s, c), CD),
                   jax.ShapeDtypeStruct((CONV_W, c), F32), vshape, vshape, vshape, vshape, wshape, wshape),
        grid=(RG_BLOCKS, nt),
        in_specs=[blk, blk, blk, tail, blk, tail, cwb, vec, wblk, wblk, wblk, wblk, vec, vec, vec],
        out_specs=(blk, blk, cwb, vec, vec, vec, vec, wblk, wblk),
        scratch_shapes=[pltpu.VMEM((1, cbw), F32), pltpu.VMEM((SUBLANES, cbw), F32),
                        pltpu.VMEM((tb, cbw), F32), pltpu.VMEM((tb, cbw), F32), pltpu.VMEM((tb, cbw), F32)],
        compiler_params=_params("parallel", "arbitrary"),
    )(dy, gate_br, x_br, x_br, hs, hs, cw, cb, wr, wi, wrt, wit, br, bi, lam)


ATT_BLOCK = 256
ATT_Q_BLOCK = 1024
ATT_RATIO = ATT_Q_BLOCK // ATT_BLOCK
ATT_SCALE = 1.0 / math.sqrt(SB_HEAD_DIM)
N_PAIRS = SB_HEADS * SB_HEAD_DIM // LANES
NT_DIMS = (((1,), (1,)), ((), ()))
TN_DIMS = (((0,), (0,)), ((), ()))


LOG2E = 1.4426950408889634


def _neg_abs(x):
    bits = lax.bitcast_convert_type(x, jnp.uint32) | jnp.uint32(0x80000000)
    return lax.bitcast_convert_type(bits, F32)


def _qk(qx, kb):
    return lax.dot_general(qx, kb, NT_DIMS, preferred_element_type=F32)


def _sb_logits(qk, valid):
    z2 = qk * (ATT_SCALE * LOG2E)
    lb2 = jnp.minimum(z2, 0.0) - jnp.log2(1.0 + jnp.exp2(_neg_abs(z2)))
    l2 = lb2 - z2
    if valid is not None:
        l2 = jnp.where(valid, l2, 0.0)
    return lb2, l2


def _hi_lo(x):
    hi = x.astype(CD)
    lo = (x - hi.astype(F32)).astype(CD)
    return jnp.concatenate([hi, lo], axis=1)


def _tri(strict, stacked):
    r = lax.broadcasted_iota(jnp.int32, (ATT_BLOCK, ATT_BLOCK), 0)
    c = lax.broadcasted_iota(jnp.int32, (ATT_BLOCK, ATT_BLOCK), 1)
    m = (r > c if strict else r >= c).astype(CD)
    return jnp.concatenate([m, m], axis=0) if stacked else m


def _attn_fwd(qkv, name):
    _, s, _ = qkv.shape
    tq, t = ATT_Q_BLOCK, ATT_BLOCK
    nblk = s // tq

    def body(q_ref, k_ref, v_ref, o_ref, qk_scr, w_scr):
        i = pl.program_id(1)
        lane = lax.broadcasted_iota(jnp.int32, (1, LANES), 1)
        head_masks = (lane < SB_HEAD_DIM, lane >= SB_HEAD_DIM)
        q = q_ref[0]
        qs = [jnp.where(m, q, jnp.zeros_like(q)) for m in head_masks]
        tri = _tri(True, False)
        rr = lax.broadcasted_iota(jnp.int32, (tq, t), 0)
        cc = lax.broadcasted_iota(jnp.int32, (tq, t), 1)

        def rows_of(j):
            return pl.ds(pl.multiple_of(j * t, t), t)

        def tail(x, row0):
            return x if row0 == 0 else x[row0:]

        def start_logits(j, row0=0):
            kb = k_ref[0, rows_of(j), :]
            for hd in range(2):
                qk_scr[hd, row0:, :] = _qk(tail(qs[hd], row0), kb)

        def weights(run, diagonal=False, row0=0):
            new_run = []
            valid = (cc < rr)[:tq - row0] if diagonal else None
            for hd in range(2):
                lb2, l2 = _sb_logits(qk_scr[hd, row0:, :], valid)
                w = jnp.exp2(lb2 + (tail(run[hd], row0) + jnp.dot(l2.astype(CD), tri, preferred_element_type=F32)))
                if valid is not None:
                    w = jnp.where(valid, w, 0.0)
                w_scr[row0:, hd * t:(hd + 1) * t] = w.astype(CD)
                rowsum = jnp.sum(l2, axis=1, keepdims=True)
                if row0:
                    rowsum = jnp.concatenate([jnp.zeros((row0, 1), F32), rowsum], axis=0)
                new_run.append(run[hd] + rowsum)
            return tuple(new_run)

        def apply_weights(j, row0=0):
            vb = v_ref[0, rows_of(j), :]
            vcat = jnp.concatenate([jnp.where(m, vb, jnp.zeros_like(vb)) for m in head_masks], axis=0)
            inc = jnp.dot(w_scr[row0:, :], vcat, preferred_element_type=F32)
            return inc if row0 == 0 else jnp.concatenate([jnp.zeros((row0, LANES), F32), inc], axis=0)

        zero = jnp.zeros((tq, 1), F32)
        last = ATT_RATIO - 1
        start_logits(ATT_RATIO * i + last, last * t)
        run = weights((zero, zero), True, last * t)
        oacc = jnp.zeros((tq, LANES), F32)
        for d in reversed(range(last)):
            start_logits(ATT_RATIO * i + d, d * t)
            oacc = oacc + apply_weights(ATT_RATIO * i + d + 1, (d + 1) * t)
            run = weights(run, True, d * t)
        start_logits(jnp.maximum(ATT_RATIO * i - 1, 0))

        def step(jj, carry):
            run, oacc = carry
            b = ATT_RATIO * i - 1 - jj
            oacc = oacc + apply_weights(b + 1)
            run = weights(run)
            start_logits(jnp.maximum(b - 1, 0))
            return run, oacc

        run, oacc = lax.fori_loop(0, ATT_RATIO * i, step, (run, oacc))
        o_ref[0] = oacc + apply_weights(0)

    return pl.pallas_call(
        body, name=name, out_shape=jax.ShapeDtypeStruct((N_PAIRS, s, LANES), F32), grid=(N_PAIRS, nblk),
        in_specs=[pl.BlockSpec((1, tq, LANES), lambda p, i: (p, i, 0)),
                  pl.BlockSpec((1, s, LANES), lambda p, i: (N_PAIRS + p, 0, 0)),
                  pl.BlockSpec((1, s, LANES), lambda p, i: (2 * N_PAIRS + p, 0, 0))],
        out_specs=pl.BlockSpec((1, tq, LANES), lambda p, i: (p, i, 0)),
        scratch_shapes=[pltpu.VMEM((2, tq, t), F32), pltpu.VMEM((tq, 2 * t), CD)],
        compiler_params=_params("parallel", "arbitrary"),
    )(qkv, qkv, qkv)


def _attn_bwd(qkv, o, do, name):
    _, s, _ = qkv.shape
    tq, t = ATT_Q_BLOCK, ATT_BLOCK
    nblk = s // tq

    def body(q_ref, k_ref, v_ref, o_ref, do_ref, dq_ref, dk_ref, dv_ref, qk_scr, dw_scr, w_scr, dz_scr):
        i = pl.program_id(1)

        @pl.when(i == 0)
        def _():
            dk_ref[...] = jnp.zeros_like(dk_ref)
            dv_ref[...] = jnp.zeros_like(dv_ref)

        lane = lax.broadcasted_iota(jnp.int32, (1, LANES), 1)
        head_masks = (lane < SB_HEAD_DIM, lane >= SB_HEAD_DIM)
        q = q_ref[0]
        dov = do_ref[0]
        ov = o_ref[0]
        qs = [jnp.where(m, q, jnp.zeros_like(q)) for m in head_masks]
        q_scaled_t = jnp.concatenate([(qx.astype(F32) * ATT_SCALE).T for qx in qs], axis=1).astype(CD)
        docs = [jnp.where(m, dov, 0.0).astype(CD) for m in head_masks]
        docat_t = jnp.concatenate([jnp.where(m, dov, 0.0).T for m in head_masks], axis=1).astype(CD)
        totals = [jnp.sum(d.astype(F32) * ov, axis=1, keepdims=True) for d in docs]
        tri = _tri(True, False)
        tri_incl = _tri(False, True)
        rr = lax.broadcasted_iota(jnp.int32, (tq, t), 0)
        cc = lax.broadcasted_iota(jnp.int32, (tq, t), 1)

        def rows_of(j):
            return pl.ds(pl.multiple_of(j * t, t), t)

        def tail(x, row0):
            return x if row0 == 0 else x[row0:]

        def pad_rows(x, row0):
            return x if row0 == 0 else jnp.concatenate([jnp.zeros((row0, x.shape[1]), x.dtype), x], axis=0)

        def start_products(j, row0=0):
            kb = k_ref[0, rows_of(j), :]
            vb = v_ref[0, rows_of(j), :]
            for hd in range(2):
                qk_scr[hd, row0:, :] = _qk(tail(qs[hd], row0), kb)
                dw_scr[hd, row0:, :] = lax.dot_general(tail(docs[hd], row0), vb, NT_DIMS, preferred_element_type=F32)

        def logit_grads(run, erun, diagonal=False, row0=0):
            new_run, new_erun = [], []
            valid = (cc < rr)[:tq - row0] if diagonal else None
            for hd in range(2):
                lb2, l2 = _sb_logits(qk_scr[hd, row0:, :], valid)
                w = jnp.exp2(lb2 + (tail(run[hd], row0) + jnp.dot(l2.astype(CD), tri, preferred_element_type=F32)))
                if valid is not None:
                    w = jnp.where(valid, w, 0.0)
                wc = w.astype(CD)
                w_scr[hd * tq + row0:(hd + 1) * tq, :] = wc
                e = dw_scr[hd, row0:, :] * wc.astype(F32)
                prefix = (tail(totals[hd] - erun[hd], row0)
                          - jnp.dot(_hi_lo(e), tri_incl, preferred_element_type=F32))
                dz = e - jnp.exp2(lb2) * (e + prefix)
                if valid is not None:
                    dz = jnp.where(valid, dz, 0.0)
                dz_scr[hd * tq + row0:(hd + 1) * tq, :] = dz.astype(CD)
                new_run.append(run[hd] + pad_rows(jnp.sum(l2, axis=1, keepdims=True), row0))
                new_erun.append(erun[hd] + pad_rows(jnp.sum(e, axis=1, keepdims=True), row0))
            return tuple(new_run), tuple(new_erun)

        def apply_grads(j, row0=0):
            rows = rows_of(j)
            kb = k_ref[0, rows, :]
            kcat = jnp.concatenate([jnp.where(m, kb, jnp.zeros_like(kb)) for m in head_masks], axis=0)
            dz_heads = [dz_scr[hd * tq + row0:(hd + 1) * tq, :] for hd in range(2)]
            w_heads = [w_scr[hd * tq + row0:(hd + 1) * tq, :] for hd in range(2)]
            q_t = jnp.concatenate([q_scaled_t[:, hd * tq + row0:(hd + 1) * tq] for hd in range(2)], axis=1)
            do_t = jnp.concatenate([docat_t[:, hd * tq + row0:(hd + 1) * tq] for hd in range(2)], axis=1)
            dk_ref[0, :, rows] += jnp.dot(q_t, jnp.concatenate(dz_heads, axis=0), preferred_element_type=F32)
            dv_ref[0, :, rows] += jnp.dot(do_t, jnp.concatenate(w_heads, axis=0), preferred_element_type=F32)
            return pad_rows(jnp.dot(jnp.concatenate(dz_heads, axis=1), kcat, preferred_element_type=F32), row0)

        zero = jnp.zeros((tq, 1), F32)
        last = ATT_RATIO - 1
        start_products(ATT_RATIO * i + last, last * t)
        run, erun = logit_grads((zero, zero), (zero, zero), True, last * t)
        dqacc = jnp.zeros((tq, LANES), F32)
        for d in reversed(range(last)):
            start_products(ATT_RATIO * i + d, d * t)
            dqacc = dqacc + apply_grads(ATT_RATIO * i + d + 1, (d + 1) * t)
            run, erun = logit_grads(run, erun, True, d * t)
        start_products(jnp.maximum(ATT_RATIO * i - 1, 0))

        def step(jj, carry):
            run, erun, dqacc = carry
            b = ATT_RATIO * i - 1 - jj
            dqacc = dqacc + apply_grads(b + 1)
            run, erun = logit_grads(run, erun)
            start_products(jnp.maximum(b - 1, 0))
            return run, erun, dqacc

        run, erun, dqacc = lax.fori_loop(0, ATT_RATIO * i, step, (run, erun, dqacc))
        dq_ref[0] = (dqacc + apply_grads(0)) * ATT_SCALE

    qblk = pl.BlockSpec((1, tq, LANES), lambda p, i: (p, i, 0))
    full = pl.BlockSpec((1, LANES, s), lambda p, i: (p, 0, 0))
    shape = jax.ShapeDtypeStruct((N_PAIRS, s, LANES), F32)
    shape_t = jax.ShapeDtypeStruct((N_PAIRS, LANES, s), F32)
    dq, dk_t, dv_t = pl.pallas_call(
        body, name=name, out_shape=(shape, shape_t, shape_t), grid=(N_PAIRS, nblk),
        in_specs=[qblk,
                  pl.BlockSpec((1, s, LANES), lambda p, i: (N_PAIRS + p, 0, 0)),
                  pl.BlockSpec((1, s, LANES), lambda p, i: (2 * N_PAIRS + p, 0, 0)),
                  qblk, qblk],
        out_specs=(qblk, full, full),
        scratch_shapes=[pltpu.VMEM((2, tq, t), F32), pltpu.VMEM((2, tq, t), F32),
                        pltpu.VMEM((2 * tq, t), CD), pltpu.VMEM((2 * tq, t), CD)],
        compiler_params=_params("parallel", "arbitrary"),
    )(qkv, qkv, qkv, o, do)
    return dq, jnp.swapaxes(dk_t, 1, 2), jnp.swapaxes(dv_t, 1, 2)


def _adamw(w, g, m, v, name):
    shape = w.shape
    rows, cols = (shape[-2], shape[-1]) if len(shape) >= 2 else (1, shape[-1])
    lead = w.size // (rows * cols)
    tr = _pick(rows, (512, 256, 128, 64, 32, 16, 8))

    def body(w_ref, g_ref, m_ref, v_ref, d_ref, nm_ref, nv_ref):
        gv = g_ref[...]
        nm = ADAM_B1 * m_ref[...] + (1.0 - ADAM_B1) * gv
        nv = ADAM_B2 * v_ref[...] + (1.0 - ADAM_B2) * (gv * gv)
        m_hat = nm / (1.0 - ADAM_B1 ** ADAM_STEP)
        v_hat = nv / (1.0 - ADAM_B2 ** ADAM_STEP)
        d_ref[...] = -ADAM_LR * (m_hat / (jnp.sqrt(v_hat) + ADAM_EPS) + ADAM_WD * w_ref[...])
        nm_ref[...] = nm
        nv_ref[...] = nv

    blk = pl.BlockSpec((1, tr, cols), lambda l, i: (l, i, 0))
    out = jax.ShapeDtypeStruct((lead, rows, cols), F32)
    d, nm, nv = pl.pallas_call(
        body, name=name, out_shape=(out, out, out), grid=(lead, rows // tr),
        in_specs=[blk, blk, blk, blk], out_specs=(blk, blk, blk), compiler_params=_params("parallel", "parallel"),
    )(*[a.reshape(lead, rows, cols) for a in (w, g, m, v)])
    return d.reshape(shape), nm.reshape(shape), nv.reshape(shape)


HBM = pl.BlockSpec(memory_space=pltpu.HBM)


def _coords():
    return lax.axis_index("x"), lax.axis_index("y"), lax.axis_index("c")


def _other_chips(x, y):
    return [(1 - x, y), (x, 1 - y), (1 - x, 1 - y)]


def _allgather_chips(shard, name, collective_id=None):
    r, cols = shard.shape
    half = r // 2
    quarter = half // 2

    def body(src_ref, out_ref, send_sems, recv_sems):
        x, y, c = _coords()
        sibling = (x, y, 1 - c)
        nx, ny, diag = (1 - x, y), (x, 1 - y), (1 - x, 1 - y)

        def piece(chip, core, lo, n):
            return out_ref.at[2 * chip[0] + chip[1], pl.ds(core * half + lo, n), :]

        def copy(k, dst, to, src=None):
            return pltpu.make_async_remote_copy(
                src_ref=dst if src is None else src, dst_ref=dst,
                send_sem=send_sems.at[k], recv_sem=recv_sems.at[k], device_id=to, device_id_type=MESH)

        me = (x, y)
        mine = src_ref.at[pl.ds(c * half, half), :]
        direct = [copy(0, piece(me, c, 0, half), (*nx, c), src=mine), copy(1, piece(me, c, 0, half), (*ny, c), src=mine)]
        for cp in direct:
            cp.start()
        arrivals = [piece(nx, c, 0, half), piece(ny, c, 0, half), piece(diag, c, 0, quarter),
                    piece(diag, c, quarter, quarter)]
        onward = [copy(2, piece(nx, c, 0, quarter), (*ny, c)), copy(3, piece(ny, c, quarter, quarter), (*nx, c))]
        to_sibling = [copy(4 + k, dst, sibling) for k, dst in enumerate(arrivals)]
        for k, dst in enumerate(arrivals):
            copy(k, dst, (x, y, c)).wait_recv()
            if k < 2:
                onward[k].start()
            to_sibling[k].start()
        from_sibling = [piece(nx, 1 - c, 0, half), piece(ny, 1 - c, 0, half), piece(diag, 1 - c, 0, quarter),
                        piece(diag, 1 - c, quarter, quarter)]
        for k, dst in enumerate(from_sibling):
            copy(4 + k, dst, (x, y, c)).wait_recv()
        for cp in direct + onward + to_sibling:
            cp.wait_send()

    out_shape = jax.ShapeDtypeStruct((N_CHIPS, r, cols), shard.dtype)
    sems = (pltpu.SemaphoreType.DMA((8,)), pltpu.SemaphoreType.DMA((8,)))
    if collective_id is None:
        return pl.pallas_call(body, name=name, out_shape=out_shape, in_specs=[HBM], out_specs=HBM,
                              scratch_shapes=list(sems))(shard)
    shard_ref = jax.new_ref(shard, memory_space=pltpu.MemorySpace.HBM)
    gathered_ref = jax.empty_ref(out_shape, memory_space=pltpu.MemorySpace.HBM)

    @_sequencer(name, collective_id, sems)
    def launch(send_sems, recv_sems):
        x, y, c = _coords()
        _handshake([(1 - x, y, c), (x, 1 - y, c), (x, y, 1 - c)])
        body(shard_ref, gathered_ref, send_sems, recv_sems)

    launch()
    return gathered_ref[...]


def _exchange_sibling_halves(g, name):
    n, r, cols = g.shape
    half = r // 2

    def body(g_ref, out_ref, send_sem, recv_sem):
        x, y, c = _coords()
        cp = pltpu.make_async_remote_copy(
            src_ref=g_ref.at[:, pl.ds((1 - c) * half, half), :], dst_ref=out_ref,
            send_sem=send_sem, recv_sem=recv_sem, device_id=(x, y, 1 - c), device_id_type=MESH)
        cp.start()
        cp.wait()

    return pl.pallas_call(
        body, name=name, out_shape=jax.ShapeDtypeStruct((n, half, cols), g.dtype),
        in_specs=[HBM], out_specs=HBM,
        scratch_shapes=[pltpu.SemaphoreType.DMA, pltpu.SemaphoreType.DMA],
    )(g)


def _sequencer(name, collective_id, scratch_types):
    return pl.kernel(mesh=plsc.ScalarSubcoreMesh(axis_name="sequencer", num_cores=1), name=name,
                     scratch_types=scratch_types, compiler_params=pltpu.CompilerParams(collective_id=collective_id))


def _handshake(peers):
    barrier = pltpu.get_barrier_semaphore()
    for peer in peers:
        pl.semaphore_signal(barrier, inc=1, device_id=peer, device_id_type=MESH)
    pl.semaphore_wait(barrier, len(peers))


def _exchange_sibling_halves_async(g, name, collective_id):
    n, r, cols = g.shape
    half = r // 2
    g_ref = jax.new_ref(g, memory_space=pltpu.MemorySpace.HBM)
    out_ref = jax.empty_ref(jax.ShapeDtypeStruct((n, half, cols), g.dtype), memory_space=pltpu.MemorySpace.HBM)

    @_sequencer(name, collective_id, (pltpu.SemaphoreType.DMA, pltpu.SemaphoreType.DMA))
    def launch(send_sem, recv_sem):
        x, y, c = _coords()
        _handshake([(x, y, 1 - c)])
        cp = pltpu.make_async_remote_copy(
            src_ref=g_ref.at[:, pl.ds((1 - c) * half, half), :], dst_ref=out_ref,
            send_sem=send_sem, recv_sem=recv_sem, device_id=(x, y, 1 - c), device_id_type=MESH)
        cp.start()
        cp.wait()

    launch()
    return out_ref[...]


def _share_halves_async(v, name, collective_id):
    h = v.shape[0] // 2
    v_ref = jax.new_ref(v, memory_space=pltpu.MemorySpace.HBM)

    @_sequencer(name, collective_id, (pltpu.SemaphoreType.DMA, pltpu.SemaphoreType.DMA))
    def launch(send_sem, recv_sem):
        x, y, c = _coords()
        _handshake([(x, y, 1 - c)])
        cp = pltpu.make_async_remote_copy(
            src_ref=v_ref.at[pl.ds(c * h, h), :], dst_ref=v_ref.at[pl.ds(c * h, h), :],
            send_sem=send_sem, recv_sem=recv_sem, device_id=(x, y, 1 - c), device_id_type=MESH)
        cp.start()
        pltpu.make_async_remote_copy(
            src_ref=v_ref.at[pl.ds(c * h, h), :], dst_ref=v_ref.at[pl.ds((1 - c) * h, h), :],
            send_sem=send_sem, recv_sem=recv_sem, device_id=(x, y, 1 - c), device_id_type=MESH).wait_recv()
        cp.wait_send()

    launch()
    return v_ref[...]


def _scatter_to_chips_async(p, name, collective_id):
    p_ref = jax.new_ref(p, memory_space=pltpu.MemorySpace.HBM)
    out_ref = jax.empty_ref(jax.ShapeDtypeStruct(p.shape, p.dtype), memory_space=pltpu.MemorySpace.HBM)

    @_sequencer(name, collective_id, (pltpu.SemaphoreType.DMA((3,)), pltpu.SemaphoreType.DMA((3,))))
    def launch(send_sems, recv_sems):
        x, y, c = _coords()
        me = 2 * x + y
        _handshake([(px, py, c) for px, py in _other_chips(x, y)])
        sends = []
        for j, (px, py) in enumerate(_other_chips(x, y)):
            sends.append(pltpu.make_async_remote_copy(
                src_ref=p_ref.at[2 * px + py], dst_ref=out_ref.at[me],
                send_sem=send_sems.at[j], recv_sem=recv_sems.at[j], device_id=(px, py, c), device_id_type=MESH))
        for cp in sends:
            cp.start()
        for j, (px, py) in enumerate(_other_chips(x, y)):
            pltpu.make_async_remote_copy(
                src_ref=p_ref.at[me], dst_ref=out_ref.at[2 * px + py],
                send_sem=send_sems.at[j], recv_sem=recv_sems.at[j], device_id=(px, py, c),
                device_id_type=MESH).wait_recv()
        for cp in sends:
            cp.wait_send()

    launch()
    return out_ref[...]


def _share_halves(v, name):
    h = v.shape[0] // 2

    def body(v_ref, out_ref, send_sem, recv_sem):
        x, y, c = _coords()
        cp = pltpu.make_async_remote_copy(
            src_ref=v_ref.at[pl.ds(c * h, h), :], dst_ref=out_ref.at[pl.ds(c * h, h), :],
            send_sem=send_sem, recv_sem=recv_sem, device_id=(x, y, 1 - c), device_id_type=MESH)
        cp.start()
        pltpu.make_async_remote_copy(
            src_ref=v_ref.at[pl.ds(c * h, h), :], dst_ref=out_ref.at[pl.ds((1 - c) * h, h), :],
            send_sem=send_sem, recv_sem=recv_sem, device_id=(x, y, 1 - c), device_id_type=MESH).wait_recv()
        cp.wait_send()

    return pl.pallas_call(
        body, name=name, out_shape=jax.ShapeDtypeStruct(v.shape, v.dtype),
        in_specs=[HBM], out_specs=HBM, input_output_aliases={0: 0},
        scratch_shapes=[pltpu.SemaphoreType.DMA, pltpu.SemaphoreType.DMA],
    )(v)


def _allreduce_small(v, name):
    r, cols = v.shape

    def body(v_ref, out_ref, buf_ref, send_sems, recv_sems):
        x, y, c = _coords()
        me = 4 * x + 2 * y + c
        buf_ref[me] = v_ref[...]
        sends = []
        for k in range(1, N_DEV):
            px = 1 - x if k & 4 else x
            py = 1 - y if k & 2 else y
            pc = 1 - c if k & 1 else c
            sends.append(pltpu.make_async_remote_copy(
                src_ref=v_ref, dst_ref=buf_ref.at[me], send_sem=send_sems.at[k - 1], recv_sem=recv_sems.at[k - 1],
                device_id=(px, py, pc), device_id_type=MESH))
        for cp in sends:
            cp.start()
        for cp in sends:
            cp.wait()
        acc = buf_ref[0]
        for d in range(1, N_DEV):
            acc = acc + buf_ref[d]
        out_ref[...] = acc

    return pl.pallas_call(
        body, name=name, out_shape=jax.ShapeDtypeStruct((r, cols), F32),
        in_specs=[pl.BlockSpec(memory_space=pltpu.VMEM)], out_specs=pl.BlockSpec(memory_space=pltpu.VMEM),
        scratch_shapes=[pltpu.VMEM((N_DEV, r, cols), F32), pltpu.SemaphoreType.DMA((N_DEV - 1,)),
                        pltpu.SemaphoreType.DMA((N_DEV - 1,))],
    )(v)


def _add_sibling(g, from_sibling, core, name):
    n, h, cols = from_sibling.shape
    tr = _row_tile(h)
    steps = h // tr

    def body(core_ref, a_ref, b_ref, o_ref):
        o_ref[...] = (a_ref[...] + b_ref[...]).astype(o_ref.dtype)

    return pl.pallas_call(
        body, name=name, out_shape=jax.ShapeDtypeStruct(from_sibling.shape, jnp.bfloat16),
        grid_spec=pltpu.PrefetchScalarGridSpec(
            num_scalar_prefetch=1, grid=(n, steps),
            in_specs=[pl.BlockSpec((1, tr, cols), lambda s, i, core_ref: (s, core_ref[0] * steps + i, 0)),
                      pl.BlockSpec((1, tr, cols), lambda s, i, core_ref: (s, i, 0))],
            out_specs=pl.BlockSpec((1, tr, cols), lambda s, i, core_ref: (s, i, 0))),
        compiler_params=_params("parallel", "parallel"),
    )(core.reshape(1).astype(jnp.int32), g, from_sibling)


def _sum_slots(p, own, chip, core, name):
    n, r, cols = p.shape
    tr = _row_tile(r)
    steps = r // tr

    def body(core_ref, chip_ref, p_ref, own_ref, o_ref):
        parts = [jnp.where(chip_ref[0] == s, own_ref[0], p_ref[s]).astype(F32) for s in range(n)]
        o_ref[...] = ((parts[0] + parts[1]) + parts[2]) + parts[3]

    return pl.pallas_call(
        body, name=name, out_shape=jax.ShapeDtypeStruct((2 * r, cols), F32),
        grid_spec=pltpu.PrefetchScalarGridSpec(
            num_scalar_prefetch=2, grid=(steps,),
            in_specs=[pl.BlockSpec((n, tr, cols), lambda i, core_ref, chip_ref: (0, i, 0)),
                      pl.BlockSpec((1, tr, cols), lambda i, core_ref, chip_ref: (chip_ref[0], i, 0))],
            out_specs=pl.BlockSpec((tr, cols), lambda i, core_ref, chip_ref: (core_ref[0] * steps + i, 0))),
        compiler_params=_params("parallel"),
    )(core.reshape(1).astype(jnp.int32), chip.reshape(1).astype(jnp.int32), p, own)


PACK_COLS = 1024


def _pack_shards(parts):
    return jnp.concatenate([p.reshape(-1, PACK_COLS) for p in parts], axis=0)


def _unpack_shards(buf, shapes):
    out, row = [], 0
    for shp in shapes:
        nrows = math.prod(shp) // PACK_COLS
        out.append(buf[..., row:row + nrows, :].reshape(buf.shape[:-2] + tuple(shp)))
        row += nrows
    return out


def _local_step(x, target, w):
    t = lambda a: a.T
    g = {}
    h0 = _rms_fwd(x, w["norm_mix_g"][0], "rms_mix0")
    w_in_g, w_in_x = w["a_w_in"][:, :D_RNN], w["a_w_in"][:, D_RNN:]
    gate_br = _matmul([(h0, w_in_g)], F32, "mm_a_gate")
    x_br = _matmul([(h0, w_in_x)], F32, "mm_a_xbr")
    y_a, hs = _rglru_fwd(gate_br, x_br, w["a_conv_w"], w["a_conv_b"], w["a_w_r"], w["a_w_i"], w["a_b_r"],
                         w["a_b_i"], w["a_lambda"], "rglru_fwd")
    x1, h1 = _matmul([(y_a, w["a_w_out"])], F32, "mm_a_out", addend=x, norm_gain=w["norm_ffn_g"][0])
    fg0, fu0, act0 = _ffn_up(h1, w["ffn_w_gate"][0], w["ffn_w_up"][0], "ffn0_up")
    x2, h2 = _matmul([(act0, w["ffn_w_down"][0])], F32, "mm_f0_down", addend=x1, norm_gain=w["norm_mix_g"][1])
    qkv = _matmul([(h2, w["b_w_qkv"])], CD, "mm_b_qkv", out_lbm=True, tn=1024)
    o = _attn_fwd(qkv, "attn_fwd")
    x3, h3 = _matmul([(o, w["b_w_out"])], F32, "mm_b_out", a_lbm=True, addend=x2, norm_gain=w["norm_ffn_g"][1])
    fg1, fu1, act1 = _ffn_up(h3, w["ffn_w_gate"][1], w["ffn_w_up"][1], "ffn1_up")
    x4 = _matmul([(act1, w["ffn_w_down"][1])], F32, "mm_f1_down", addend=x3)
    loss, dx4, dx4c, g["final_g"] = _loss_head(x4, w["final_g"], target, "loss_head")

    def ffn_bwd(dx_out, dxc, h, x_in, fg, fu, act, layer, tag):
        dg, du = _ffn_dact(dxc, t(w["ffn_w_down"][layer]), fg, fu, "ffn_" + tag + "_dact")
        dwd = _matmul([(act, dxc)], F32, "mm_" + tag + "_dwd", trans_a=True)
        dwg = _matmul([(h, dg)], F32, "mm_" + tag + "_dwg", trans_a=True)
        dwu = _matmul([(h, du)], F32, "mm_" + tag + "_dwu", trans_a=True)
        dx_in, dx_in_c, dgain = _matmul([(dg, t(w["ffn_w_gate"][layer])), (du, t(w["ffn_w_up"][layer]))], F32,
                                        "mm_" + tag + "_dh", norm_bwd=(x_in, w["norm_ffn_g"][layer], dx_out))
        return dx_in, dx_in_c, dgain, dwg, dwu, dwd

    dx3, dx3c, dgf1, dwg1, dwu1, dwd1 = ffn_bwd(dx4, dx4c, h3, x3, fg1, fu1, act1, 1, "f1")
    do = _matmul([(dx3c, t(w["b_w_out"]))], F32, "mm_b_do", out_lbm=True, tn=1024)
    g["b_w_out"] = _matmul([(o, dx3c)], F32, "mm_b_dwout", trans_a=True, a_lbm=True)
    dq, dk, dv = _attn_bwd(qkv, o, do, "attn_bwd")
    wq_t = t(w["b_w_qkv"])
    parts = (dq, dk, dv)
    g["b_w_qkv"] = jnp.concatenate(
        [_matmul([(h2, p)], F32, "mm_b_dwqkv%d" % n, trans_a=True, b_lbm=True) for n, p in enumerate(parts)], axis=1)
    dx2, dx2c, dgm1 = _matmul([(p, wq_t[n * D_MODEL:(n + 1) * D_MODEL]) for n, p in enumerate(parts)], F32, "mm_b_dh",
                              a_lbm=True, norm_bwd=(x2, w["norm_mix_g"][1], dx3))
    dx1, dx1c, dgf0, dwg0, dwu0, dwd0 = ffn_bwd(dx2, dx2c, h1, x1, fg0, fu0, act0, 0, "f0")
    dy_a = _matmul([(dx1c, t(w["a_w_out"]))], F32, "mm_a_dy")
    g["a_w_out"] = _matmul([(y_a, dx1c)], F32, "mm_a_dwout", trans_a=True)
    wrt = jnp.swapaxes(w["a_w_r"], 1, 2)
    wit = jnp.swapaxes(w["a_w_i"], 1, 2)
    (dgate, dxbr, g["a_conv_w"], g["a_conv_b"], g["a_b_r"], g["a_b_i"], g["a_lambda"], g["a_w_r"],
     g["a_w_i"]) = _rglru_bwd(dy_a, gate_br, x_br, hs, w["a_conv_w"], w["a_conv_b"], w["a_w_r"], w["a_w_i"], wrt, wit,
                              w["a_b_r"], w["a_b_i"], w["a_lambda"], "rglru_bwd")
    g["a_w_in"] = jnp.concatenate([_matmul([(h0, dgate)], F32, "mm_a_dwin_g", trans_a=True),
                                   _matmul([(h0, dxbr)], F32, "mm_a_dwin_x", trans_a=True)], axis=1)
    dx0, _, dgm0 = _matmul([(dgate, t(w_in_g)), (dxbr, t(w_in_x))], F32, "mm_a_dh",
                           norm_bwd=(x, w["norm_mix_g"][0], dx1))
    g["norm_mix_g"] = jnp.concatenate([dgm0, dgm1], axis=0)
    g["norm_ffn_g"] = jnp.concatenate([dgf0, dgf1], axis=0)
    g["ffn_w_gate"] = [dwg0, dwg1]
    g["ffn_w_up"] = [dwu0, dwu1]
    g["ffn_w_down"] = [dwd0, dwd1]
    return loss, dx0, g


WEIGHTS = ["norm_mix_g", "norm_ffn_g", "a_w_in", "a_conv_w", "a_conv_b", "a_w_r", "a_b_r", "a_w_i", "a_b_i",
           "a_lambda", "a_w_out", "b_w_qkv", "b_w_out", "ffn_w_gate", "ffn_w_up", "ffn_w_down", "final_g"]
BIG = [("a_w_in", 2), ("a_w_r", 2), ("a_w_i", 2), ("a_w_out", 1), ("b_w_qkv", 2), ("b_w_out", 1),
       ("ffn_w_gate", 2), ("ffn_w_up", 2), ("ffn_w_down", 1)]
LAYER1 = ["b_w_qkv", "b_w_out", "ffn_w_gate", "ffn_w_up", "ffn_w_down"]
LAYER0 = ["a_w_in", "a_w_r", "a_w_i", "a_w_out", "ffn_w_gate", "ffn_w_up", "ffn_w_down"]
RS_COLLECTIVE_IDS = {"chips1": 3, "chips0": 4, "sibling1": 5, "share1": 6}
GATHER_COLLECTIVE_IDS = (8, 7)
SMALL = ["norm_mix_g", "norm_ffn_g", "a_conv_w", "a_conv_b", "a_b_r", "a_b_i", "a_lambda", "final_g"]


def _split_chips(full, axis):
    return jnp.stack(jnp.split(full, N_CHIPS, axis=axis))


def _step(x, target, weights, moments_m, moments_v):
    chip = 2 * lax.axis_index("x") + lax.axis_index("y")
    core = lax.axis_index("c")
    axis_of = dict(BIG)
    full = {}
    for group, layer, tag, collective_id in ((LAYER0[:4], 0, "0a", None), (LAYER0[4:], 0, "0f", GATHER_COLLECTIVE_IDS[0]),
                                             (LAYER1, 1, "1", GATHER_COLLECTIVE_IDS[1])):
        shards = [weights[n][layer % weights[n].shape[0]].astype(CD) for n in group]
        gathered = _allgather_chips(_pack_shards(shards), "allgather_weights" + tag, collective_id)
        for n, own, stack in zip(group, shards, _unpack_shards(gathered, [sh.shape for sh in shards])):
            joined = jnp.concatenate([jnp.where(chip == s, own, stack[s]) for s in range(N_CHIPS)],
                                     axis=axis_of[n] - 1)
            full.setdefault(n, {})[layer] = joined
    full = {n: (v[0] if n.startswith("a_") else v[1] if n.startswith("b_") else [v[0], v[1]]) for n, v in full.items()}
    cw_rows = jnp.zeros((N_CHIPS, CONV_W, RG_BW), F32)
    cw_rows = lax.dynamic_update_slice(cw_rows, jnp.where(core == 0, weights["a_conv_w"], 0.0), (chip, 0, 0))
    cw_all = _allreduce_small(cw_rows.reshape(-1, LANES), "allgather_conv_w").reshape(N_CHIPS, CONV_W, RG_BW)
    full["a_conv_w"] = jnp.concatenate([cw_all[s] for s in range(N_CHIPS)], axis=1)
    for n in ("norm_mix_g", "norm_ffn_g", "final_g"):
        full[n] = weights[n]
    for n in ("a_conv_b", "a_b_r", "a_b_i", "a_lambda"):
        full[n] = weights[n]
    loss, dx, grads = _local_step(x[0], target[0], full)
    small_parts = [grads[n].reshape(-1) for n in SMALL] + [loss.reshape(-1)]
    sizes = [p.shape[0] for p in small_parts]
    small = _allreduce_small(jnp.concatenate(small_parts).reshape(-1, LANES), "allreduce_small").reshape(-1)
    red, pos = {}, 0
    for n, sz in zip(SMALL + ["loss"], sizes):
        red[n] = small[pos:pos + sz]
        pos += sz
    loss_out = red["loss"][0]
    g_out = {}
    for n in SMALL:
        if n == "a_conv_w":
            g_out[n] = lax.dynamic_slice(red[n].reshape(CONV_W, D_RNN), (0, chip * RG_BW), (CONV_W, RG_BW)).reshape(
                weights[n].shape)
        else:
            g_out[n] = red[n].reshape(weights[n].shape)
    axis_of = dict(BIG)
    pieces = {}
    for group, layer, tag in ((LAYER1, 1, "1"), (LAYER0, 0, "0")):
        stacks, shapes = [], []
        for n in group:
            per_layer = isinstance(grads[n], list)
            gfull = grads[n][layer] if per_layer else grads[n]
            shard_shape = weights[n].shape[1:]
            gfull = gfull.reshape((1,) + gfull.shape)
            stacks.append(_split_chips(gfull, axis_of[n]).reshape(N_CHIPS, -1, PACK_COLS))
            shapes.append((1,) + tuple(shard_shape))
        gbuf = jnp.concatenate(stacks, axis=1)
        if layer == 1:
            from_sibling = _exchange_sibling_halves_async(gbuf, "rs_sibling" + tag, RS_COLLECTIVE_IDS["sibling1"])
        else:
            from_sibling = _exchange_sibling_halves(gbuf, "rs_sibling" + tag)
        chip_partial = _add_sibling(gbuf, from_sibling, core, "rs_add" + tag)
        from_chips = _scatter_to_chips_async(chip_partial, "rs_chips" + tag, RS_COLLECTIVE_IDS["chips" + tag])
        halves = _sum_slots(from_chips, chip_partial, chip, core, "rs_sum" + tag)
        if layer == 1:
            reduced = _share_halves_async(halves, "rs_share" + tag, RS_COLLECTIVE_IDS["share1"])
        else:
            reduced = _share_halves(halves, "rs_share" + tag)
        for n, piece in zip(group, _unpack_shards(reduced, shapes)):
            pieces.setdefault(n, {})[layer] = piece
    for n, _ in BIG:
        layers = pieces[n]
        g_out[n] = jnp.concatenate([layers[k] for k in sorted(layers)], axis=0)
    outs_g, outs_d, outs_m, outs_v = [], [], [], []
    for n in WEIGHTS:
        d, nm, nv = _adamw(weights[n], g_out[n], moments_m[n], moments_v[n], "adamw_" + n)
        outs_g.append(g_out[n])
        outs_d.append(d)
        outs_m.append(nm)
        outs_v.append(nv)
    return (loss_out, dx[None], *outs_g, *outs_d, *outs_m, *outs_v)


def kernel(x, norm_mix_g, norm_ffn_g, a_w_in, a_conv_w, a_conv_b, a_w_r, a_b_r, a_w_i, a_b_i, a_lambda, a_w_out, b_w_qkv, b_w_out, ffn_w_gate, ffn_w_up, ffn_w_down, final_g, loss_target, m_norm_mix_g, m_norm_ffn_g, m_a_w_in, m_a_conv_w, m_a_conv_b, m_a_w_r, m_a_b_r, m_a_w_i, m_a_b_i, m_a_lambda, m_a_w_out, m_b_w_qkv, m_b_w_out, m_ffn_w_gate, m_ffn_w_up, m_ffn_w_down, m_final_g, v_norm_mix_g, v_norm_ffn_g, v_a_w_in, v_a_conv_w, v_a_conv_b, v_a_w_r, v_a_b_r, v_a_w_i, v_a_b_i, v_a_lambda, v_a_w_out, v_b_w_qkv, v_b_w_out, v_ffn_w_gate, v_ffn_w_up, v_ffn_w_down, v_final_g):
    ws = [norm_mix_g, norm_ffn_g, a_w_in, a_conv_w, a_conv_b, a_w_r, a_b_r, a_w_i, a_b_i, a_lambda, a_w_out, b_w_qkv,
          b_w_out, ffn_w_gate, ffn_w_up, ffn_w_down, final_g]
    ms = [m_norm_mix_g, m_norm_ffn_g, m_a_w_in, m_a_conv_w, m_a_conv_b, m_a_w_r, m_a_b_r, m_a_w_i, m_a_b_i, m_a_lambda,
          m_a_w_out, m_b_w_qkv, m_b_w_out, m_ffn_w_gate, m_ffn_w_up, m_ffn_w_down, m_final_g]
    vs = [v_norm_mix_g, v_norm_ffn_g, v_a_w_in, v_a_conv_w, v_a_conv_b, v_a_w_r, v_a_b_r, v_a_w_i, v_a_b_i, v_a_lambda,
          v_a_w_out, v_b_w_qkv, v_b_w_out, v_ffn_w_gate, v_ffn_w_up, v_ffn_w_down, v_final_g]
    return _step(x, loss_target, dict(zip(WEIGHTS, ws)), dict(zip(WEIGHTS, ms)), dict(zip(WEIGHTS, vs)))
```

```python
import functools
import math

import jax
import jax.numpy as jnp
from jax import lax
from jax.experimental import pallas as pl
from jax.experimental.pallas import tpu as pltpu
from jax.experimental.pallas import tpu_sc as plsc

F32 = jnp.float32
CD = jnp.bfloat16

D_MODEL = 1024
D_RNN = 1024
RG_BLOCKS = 4
RG_BW = 256
CONV_W = 4
RG_C = 8.0
SB_HEADS = 16
SB_HEAD_DIM = 64
D_FF = 2816
RMS_EPS = 1e-6
N_CHIPS = 4
N_DEV = 8

ADAM_LR = 0.001
ADAM_B1 = 0.9
ADAM_B2 = 0.999
ADAM_EPS = 1e-08
ADAM_WD = 0.01
ADAM_STEP = 10

LANES = 128
VMEM_LIMIT = 56 * 1024 * 1024
MESH = pl.DeviceIdType.MESH


def _params(*sem):
    return pltpu.CompilerParams(dimension_semantics=sem, vmem_limit_bytes=VMEM_LIMIT)


def _pick(n, prefs):
    for p in prefs:
        if n % p == 0:
            return p
    return n


def _row_tile(rows):
    return max(d for d in range(16, 1025, 16) if rows % d == 0)


def _matmul(pairs, out_dtype, name, *, trans_a=False, a_lbm=False, b_lbm=False, out_lbm=False, addend=None,
            tm=512, tn=None, tk=None, norm_gain=None, norm_bwd=None):
    a0, b0 = pairs[0]
    if trans_a:
        kdim = a0.shape[1] if a_lbm else a0.shape[0]
        m = a0.shape[0] * LANES if a_lbm else a0.shape[1]
    else:
        m = a0.shape[1] if a_lbm else a0.shape[0]
        kdim = a0.shape[0] * LANES if a_lbm else a0.shape[1]
    n = b0.shape[0] * LANES if b_lbm else b0.shape[1]
    tm = _pick(m, (tm, 1408, 256, 128))
    tn = tn or _pick(n, (1408, 1024, 768, 512, 256, 128))
    tk = tk or _pick(kdim, (1024, 1408, 512, 256, 128))
    nk = kdim // tk
    npair = len(pairs)

    def cat(ref):
        return jnp.concatenate([ref[p] for p in range(ref.shape[0])], axis=-1)

    def body(*refs):
        ins = refs[: 2 * npair]
        pos = 2 * npair
        add_ref = None
        if addend is not None:
            add_ref = refs[pos]
            pos += 1
        gain_ref = x_ref = dxin_ref = None
        if norm_gain is not None:
            gain_ref = refs[pos]
            pos += 1
        if norm_bwd is not None:
            x_ref, gain_ref, dxin_ref = refs[pos:pos + 3]
            pos += 3
        o_ref = refs[pos]
        extra_out = refs[pos + 1:-1]
        acc_ref = refs[-1]
        k = pl.program_id(2)

        @pl.when(k == 0)
        def _():
            acc_ref[...] = jnp.zeros_like(acc_ref)

        if norm_bwd is not None:
            @pl.when((k == 0) & (pl.program_id(0) == 0))
            def _():
                extra_out[1][...] = jnp.zeros_like(extra_out[1])

        acc = acc_ref[...]
        for p in range(npair):
            a = (cat(ins[2 * p]) if a_lbm else ins[2 * p][...]).astype(CD)
            b = (cat(ins[2 * p + 1]) if b_lbm else ins[2 * p + 1][...]).astype(CD)
            dims = (((0,), (0,)), ((), ())) if trans_a else (((1,), (0,)), ((), ()))
            acc = acc + lax.dot_general(a, b, dims, preferred_element_type=F32)
        acc_ref[...] = acc

        @pl.when(k == nk - 1)
        def _():
            res = acc_ref[...]
            if add_ref is not None:
                res = res + add_ref[...]
            if norm_gain is not None:
                rinv = lax.rsqrt(jnp.mean(res * res, axis=-1, keepdims=True) + RMS_EPS)
                extra_out[0][...] = (res * rinv * gain_ref[...]).astype(CD)
            if norm_bwd is not None:
                xv = x_ref[...]
                rinv = lax.rsqrt(jnp.mean(xv * xv, axis=-1, keepdims=True) + RMS_EPS)
                nrm = xv * rinv
                dn = res * gain_ref[...]
                extra_out[1][...] += jnp.sum(res * nrm, axis=0, keepdims=True)
                res = dxin_ref[...] + rinv * (dn - nrm * jnp.mean(dn * nrm, axis=-1, keepdims=True))
                extra_out[0][...] = res.astype(CD)
            res = res.astype(out_dtype)
            if out_lbm:
                for p in range(tn // LANES):
                    o_ref[p] = res[:, p * LANES:(p + 1) * LANES]
            else:
                o_ref[...] = res

    if trans_a:
        a_spec = (pl.BlockSpec((tm // LANES, tk, LANES), lambda i, j, k: (i, k, 0)) if a_lbm
                  else pl.BlockSpec((tk, tm), lambda i, j, k: (k, i)))
    else:
        a_spec = (pl.BlockSpec((tk // LANES, tm, LANES), lambda i, j, k: (k, i, 0)) if a_lbm
                  else pl.BlockSpec((tm, tk), lambda i, j, k: (i, k)))
    b_spec = (pl.BlockSpec((tn // LANES, tk, LANES), lambda i, j, k: (j, k, 0)) if b_lbm
              else pl.BlockSpec((tk, tn), lambda i, j, k: (k, j)))
    in_specs = []
    args = []
    for a, b in pairs:
        in_specs += [a_spec, b_spec]
        args += [a, b]
    if addend is not None:
        in_specs.append(pl.BlockSpec((tm, tn), lambda i, j, k: (i, j)))
        args.append(addend)
    tile = pl.BlockSpec((tm, tn), lambda i, j, k: (i, j))
    vec = pl.BlockSpec((1, tn), lambda i, j, k: (0, j))
    if out_lbm:
        out_shape = jax.ShapeDtypeStruct((n // LANES, m, LANES), out_dtype)
        out_spec = pl.BlockSpec((tn // LANES, tm, LANES), lambda i, j, k: (j, i, 0))
    else:
        out_shape = jax.ShapeDtypeStruct((m, n), out_dtype)
        out_spec = tile
    sem = ("parallel", "parallel", "arbitrary")
    if norm_gain is not None or norm_bwd is not None:
        assert tn == n and not out_lbm, "the norm needs whole rows in one tile"
        out_shape, out_spec = [out_shape, jax.ShapeDtypeStruct((m, n), CD)], [out_spec, tile]
    if norm_gain is not None:
        in_specs.append(vec)
        args.append(norm_gain.reshape(1, n))
    if norm_bwd is not None:
        x_in, gain, dx_in = norm_bwd
        in_specs += [tile, vec, tile]
        args += [x_in, gain.reshape(1, n), dx_in]
        out_shape.append(jax.ShapeDtypeStruct((1, n), F32))
        out_spec.append(vec)
        sem = ("arbitrary", "arbitrary", "arbitrary")
    return pl.pallas_call(
        body, name=name, out_shape=out_shape, grid=(m // tm, n // tn, nk),
        in_specs=in_specs, out_specs=out_spec,
        scratch_shapes=[pltpu.VMEM((tm, tn), F32)],
        compiler_params=_params(*sem),
    )(*args)


ROW_BLOCK = 256


def _rms_fwd(x, g, name):
    s, d = x.shape

    def body(x_ref, g_ref, h_ref):
        xv = x_ref[...]
        rinv = lax.rsqrt(jnp.mean(xv * xv, axis=-1, keepdims=True) + RMS_EPS)
        h_ref[...] = (xv * rinv * g_ref[...]).astype(CD)

    return pl.pallas_call(
        body, name=name, out_shape=jax.ShapeDtypeStruct((s, d), CD), grid=(s // ROW_BLOCK,),
        in_specs=[pl.BlockSpec((ROW_BLOCK, d), lambda i: (i, 0)), pl.BlockSpec((1, d), lambda i: (0, 0))],
        out_specs=pl.BlockSpec((ROW_BLOCK, d), lambda i: (i, 0)),
        compiler_params=_params("parallel"),
    )(x, g.reshape(1, d))


def _loss_head(x, g, target, name):
    s, d = x.shape

    def body(x_ref, g_ref, t_ref, loss_ref, dx_ref, dxc_ref, dg_ref):
        @pl.when(pl.program_id(0) == 0)
        def _():
            dg_ref[...] = jnp.zeros_like(dg_ref)
            loss_ref[...] = jnp.zeros_like(loss_ref)

        xv = x_ref[...]
        gv = g_ref[...]
        rinv = lax.rsqrt(jnp.mean(xv * xv, axis=-1, keepdims=True) + RMS_EPS)
        nrm = xv * rinv
        err = nrm * gv - t_ref[...]
        loss_ref[...] += 0.5 * jnp.sum(jnp.mean(err * err, axis=-1, keepdims=True), axis=0, keepdims=True)
        dy = err * (1.0 / d)
        dn = dy * gv
        dx = rinv * (dn - nrm * jnp.mean(dn * nrm, axis=-1, keepdims=True))
        dx_ref[...] = dx
        dxc_ref[...] = dx.astype(CD)
        dg_ref[...] += jnp.sum(dy * nrm, axis=0, keepdims=True)

    row = pl.BlockSpec((ROW_BLOCK, d), lambda i: (i, 0))
    vec = pl.BlockSpec((1, d), lambda i: (0, 0))
    return pl.pallas_call(
        body, name=name,
        out_shape=(jax.ShapeDtypeStruct((1, LANES), F32), jax.ShapeDtypeStruct((s, d), F32),
                   jax.ShapeDtypeStruct((s, d), CD), jax.ShapeDtypeStruct((1, d), F32)),
        grid=(s // ROW_BLOCK,), in_specs=[row, vec, row],
        out_specs=(pl.BlockSpec((1, LANES), lambda i: (0, 0)), row, row, vec),
        compiler_params=_params("arbitrary"),
    )(x, g.reshape(1, d), target)


def _sigmoid(z):
    return 1.0 / (1.0 + jnp.exp(-z))


FFN_TM = 512
FFN_TN = 1408


def _ffn_up(h, wg, wu, name):
    s, d = h.shape
    f = wg.shape[1]
    tm = _pick(s, (FFN_TM, 256))

    def body(h_ref, wg_ref, wu_ref, g_ref, u_ref, a_ref):
        hv = h_ref[...]
        gv = jnp.dot(hv, wg_ref[...], preferred_element_type=F32)
        uv = jnp.dot(hv, wu_ref[...], preferred_element_type=F32)
        g_ref[...] = gv
        u_ref[...] = uv
        a_ref[...] = (gv * _sigmoid(gv) * uv).astype(CD)

    a_spec = pl.BlockSpec((tm, d), lambda i, j: (i, 0))
    w_spec = pl.BlockSpec((d, FFN_TN), lambda i, j: (0, j))
    o_spec = pl.BlockSpec((tm, FFN_TN), lambda i, j: (i, j))
    return pl.pallas_call(
        body, name=name,
        out_shape=(jax.ShapeDtypeStruct((s, f), F32), jax.ShapeDtypeStruct((s, f), F32),
                   jax.ShapeDtypeStruct((s, f), CD)),
        grid=(s // tm, f // FFN_TN), in_specs=[a_spec, w_spec, w_spec], out_specs=(o_spec, o_spec, o_spec),
        compiler_params=_params("parallel", "parallel"),
    )(h, wg, wu)


def _ffn_dact(dxc, wd_t, g, u, name):
    s, d = dxc.shape
    f = wd_t.shape[1]
    tm = _pick(s, (FFN_TM, 256))

    def body(dx_ref, w_ref, g_ref, u_ref, dg_ref, du_ref):
        da = jnp.dot(dx_ref[...], w_ref[...], preferred_element_type=F32)
        gv = g_ref[...]
        sg = _sigmoid(gv)
        silu = gv * sg
        dg_ref[...] = (da * u_ref[...] * (sg + silu * (1.0 - sg))).astype(CD)
        du_ref[...] = (da * silu).astype(CD)

    a_spec = pl.BlockSpec((tm, d), lambda i, j: (i, 0))
    w_spec = pl.BlockSpec((d, FFN_TN), lambda i, j: (0, j))
    o_spec = pl.BlockSpec((tm, FFN_TN), lambda i, j: (i, j))
    return pl.pallas_call(
        body, name=name,
        out_shape=(jax.ShapeDtypeStruct((s, f), CD), jax.ShapeDtypeStruct((s, f), CD)),
        grid=(s // tm, f // FFN_TN), in_specs=[a_spec, w_spec, o_spec, o_spec], out_specs=(o_spec, o_spec),
        compiler_params=_params("parallel", "parallel"),
    )(dxc, wd_t, g, u)


TIME_BLOCK = 512
SUBLANES = 8
GELU_C = math.sqrt(2.0 / math.pi)
GELU_A = 0.044715


def _gelu(x):
    return 0.5 * x * (1.0 + jnp.tanh(GELU_C * (x + GELU_A * x * x * x)))


def _gelu_grad(x):
    t = jnp.tanh(GELU_C * (x + GELU_A * x * x * x))
    return 0.5 * (1.0 + t) + 0.5 * x * (1.0 - t * t) * GELU_C * (1.0 + 3.0 * GELU_A * x * x)


def _neg_expm1(x):
    series = -x * (1.0 + x * (0.5 + x * (1.0 / 6.0 + x * (1.0 / 24.0))))
    return jnp.where(x > -0.05, series, 1.0 - jnp.exp(x))


def _log_sigmoid(x):
    return jnp.minimum(x, 0.0) - jnp.log1p(jnp.exp(-jnp.abs(x)))


def _shift_down(x, tail, s):
    if s == 0:
        return x
    ext = jnp.concatenate([tail, x], axis=0)
    return pltpu.roll(ext, s, axis=0)[SUBLANES:]


def _shift_up(x, head, s):
    if s == 0:
        return x
    n = x.shape[0]
    ext = jnp.concatenate([x, head], axis=0)
    return pltpu.roll(ext, n + SUBLANES - s, axis=0)[:n]


def _rg_gates(xbr, tail, cw_ref, cb, wr, wi, br, bi, ls):
    taps = [_shift_down(xbr, tail, CONV_W - 1 - k) for k in range(CONV_W)]
    xc = cb
    for k in range(CONV_W):
        xc = xc + cw_ref[pl.ds(k, 1), :] * taps[k]
    xcd = xc.astype(CD)
    r = _sigmoid(jnp.dot(xcd, wr, preferred_element_type=F32) + br)
    i = _sigmoid(jnp.dot(xcd, wi, preferred_element_type=F32) + bi)
    log_a = RG_C * r * ls
    a = jnp.exp(log_a)
    mult = jnp.sqrt(jnp.maximum(_neg_expm1(2.0 * log_a), 0.0))
    return taps, xc, r, i, log_a, a, mult


def _scan8_fwd(a, u):
    row = lax.broadcasted_iota(jnp.int32, a.shape, 0)
    for d in (1, 2, 4):
        a_s = pltpu.roll(a, d, axis=0)
        u_s = pltpu.roll(u, d, axis=0)
        m = row >= d
        u = jnp.where(m, a * u_s + u, u)
        a = jnp.where(m, a * a_s, a)
    return a, u


def _scan8_bwd(b, u):
    row = lax.broadcasted_iota(jnp.int32, b.shape, 0)
    for d in (1, 2, 4):
        b_s = pltpu.roll(b, SUBLANES - d, axis=0)
        u_s = pltpu.roll(u, SUBLANES - d, axis=0)
        m = row < SUBLANES - d
        u = jnp.where(m, b * u_s + u, u)
        b = jnp.where(m, b * b_s, b)
    return b, u


def _rglru_fwd(gate_br, x_br, cw, cb, wr, wi, br, bi, lam, name):
    s, c = x_br.shape
    nt = s // TIME_BLOCK
    tb, cbw = TIME_BLOCK, RG_BW
    groups = tb // SUBLANES

    def body(g_ref, x_ref, tail_ref, cw_ref, cb_ref, wr_ref, wi_ref, br_ref, bi_ref, lam_ref,
             y_ref, hs_ref, carry_ref, a_scr, u_scr):
        t = pl.program_id(1)

        @pl.when(t == 0)
        def _():
            carry_ref[...] = jnp.zeros_like(carry_ref)

        tail = jnp.where(t > 0, tail_ref[...], 0.0)
        ls = _log_sigmoid(lam_ref[...])
        _, xc, _, i, _, a, mult = _rg_gates(x_ref[...], tail, cw_ref, cb_ref[...], wr_ref[0], wi_ref[0],
                                            br_ref[...], bi_ref[...], ls)
        a_scr[...] = a
        u_scr[...] = mult * (i * xc)
        carry = carry_ref[...]
        for gi in range(groups):
            rows = pl.ds(gi * SUBLANES, SUBLANES)
            pa, hl = _scan8_fwd(a_scr[rows, :], u_scr[rows, :])
            hs_ref[rows, :] = hl + pa * carry
            carry = hs_ref[pl.ds(gi * SUBLANES + SUBLANES - 1, 1), :]
        carry_ref[...] = carry
        y_ref[...] = (hs_ref[...] * _gelu(g_ref[...])).astype(CD)

    blk = pl.BlockSpec((tb, cbw), lambda n, t: (t, n))
    tail = pl.BlockSpec((SUBLANES, cbw), lambda n, t: (jnp.maximum(t * groups - 1, 0), n))
    vec = pl.BlockSpec((1, cbw), lambda n, t: (0, n))
    wblk = pl.BlockSpec((1, cbw, cbw), lambda n, t: (n, 0, 0))
    return pl.pallas_call(
        body, name=name,
        out_shape=(jax.ShapeDtypeStruct((s, c), CD), jax.ShapeDtypeStruct((s, c), F32)),
        grid=(RG_BLOCKS, nt),
        in_specs=[blk, blk, tail, pl.BlockSpec((CONV_W, cbw), lambda n, t: (0, n)), vec, wblk, wblk, vec, vec, vec],
        out_specs=(blk, blk),
        scratch_shapes=[pltpu.VMEM((1, cbw), F32), pltpu.VMEM((tb, cbw), F32), pltpu.VMEM((tb, cbw), F32)],
        compiler_params=_params("parallel", "arbitrary"),
    )(gate_br, x_br, x_br, cw, cb, wr, wi, br, bi, lam)


def _rglru_bwd(dy, gate_br, x_br, hs, cw, cb, wr, wi, wrt, wit, br, bi, lam, name):
    s, c = x_br.shape
    nt = s // TIME_BLOCK
    tb, cbw = TIME_BLOCK, RG_BW
    groups = tb // SUBLANES

    def body(dy_ref, g_ref, x_ref, tail_ref, hs_ref, hprev_ref, cw_ref, cb_ref, wr_ref, wi_ref, wrt_ref, wit_ref,
             br_ref, bi_ref, lam_ref,
             dg_ref, dx_ref, dcw_ref, dcb_ref, dbr_ref, dbi_ref, dlam_ref, dwr_ref, dwi_ref,
             carry_ref, head_ref, b_scr, u_scr, dh_scr):
        tr = pl.program_id(1)
        first_block = tr == nt - 1

        @pl.when(tr == 0)
        def _():
            carry_ref[...] = jnp.zeros_like(carry_ref)
            head_ref[...] = jnp.zeros_like(head_ref)
            for ref in (dcw_ref, dcb_ref, dbr_ref, dbi_ref, dlam_ref, dwr_ref, dwi_ref):
                ref[...] = jnp.zeros_like(ref)

        tail = jnp.where(first_block, 0.0, tail_ref[...])
        lam_v = lam_ref[...]
        ls = _log_sigmoid(lam_v)
        taps, xc, r, i, log_a, a, mult = _rg_gates(x_ref[...], tail, cw_ref, cb_ref[...], wr_ref[0], wi_ref[0],
                                                   br_ref[...], bi_ref[...], ls)
        gate_v = g_ref[...]
        dyv = dy_ref[...]
        hsv = hs_ref[...]
        dg_ref[...] = (dyv * hsv * _gelu_grad(gate_v)).astype(CD)

        row = lax.broadcasted_iota(jnp.int32, a.shape, 0)
        b_scr[...] = jnp.where(row == tb - 1, 1.0, pltpu.roll(a, tb - 1, axis=0))
        u_scr[...] = dyv * _gelu(gate_v)
        carry = carry_ref[...]
        for gi in reversed(range(groups)):
            rows = pl.ds(gi * SUBLANES, SUBLANES)
            pb, gl = _scan8_bwd(b_scr[rows, :], u_scr[rows, :])
            dh_scr[rows, :] = gl + pb * carry
            carry = dh_scr[pl.ds(gi * SUBLANES, 1), :]
        dh = dh_scr[...]
        carry_ref[...] = carry * jnp.sum(jnp.where(row == 0, a, 0.0), axis=0, keepdims=True)

        hprev_tail = jnp.where(first_block, 0.0, hprev_ref[...])
        h_prev = _shift_down(hsv, hprev_tail, 1)
        da = dh * h_prev
        ixc = i * xc
        dmult = dh * ixc
        di = dh * mult * xc
        dxc = dh * mult * i
        a2 = a * a
        dlog_a = da * a - dmult * a2 / mult
        dpre_r = (dlog_a * (RG_C * ls)) * r * (1.0 - r)
        dpre_i = di * i * (1.0 - i)
        dlam_ref[...] += jnp.sum(dlog_a * r, axis=0, keepdims=True) * (RG_C * _sigmoid(-lam_v))
        dbr_ref[...] += jnp.sum(dpre_r, axis=0, keepdims=True)
        dbi_ref[...] += jnp.sum(dpre_i, axis=0, keepdims=True)
        xcd = xc.astype(CD)
        dprc = dpre_r.astype(CD)
        dpic = dpre_i.astype(CD)
        tn_dims = (((0,), (0,)), ((), ()))
        dwr_ref[0] += lax.dot_general(xcd, dprc, tn_dims, preferred_element_type=F32)
        dwi_ref[0] += lax.dot_general(xcd, dpic, tn_dims, preferred_element_type=F32)
        dxc = dxc + jnp.dot(dprc, wrt_ref[0], preferred_element_type=F32) + jnp.dot(dpic, wit_ref[0],
                                                                                    preferred_element_type=F32)
        dcb_ref[...] += jnp.sum(dxc, axis=0, keepdims=True)
        for k in range(CONV_W):
            dcw_ref[pl.ds(k, 1), :] += jnp.sum(dxc * taps[k], axis=0, keepdims=True)
        head = head_ref[...]
        dxb = jnp.zeros_like(dxc)
        for sft in range(CONV_W):
            dxb = dxb + cw_ref[pl.ds(CONV_W - 1 - sft, 1), :] * _shift_up(dxc, head, sft)
        dx_ref[...] = dxb.astype(CD)
        head_ref[...] = dxc[0:SUBLANES, :]

    blk = pl.BlockSpec((tb, cbw), lambda n, t: (nt - 1 - t, n))
    tail = pl.BlockSpec((SUBLANES, cbw), lambda n, t: (jnp.maximum((nt - 1 - t) * groups - 1, 0), n))
    vec = pl.BlockSpec((1, cbw), lambda n, t: (0, n))
    cwb = pl.BlockSpec((CONV_W, cbw), lambda n, t: (0, n))
    wblk = pl.BlockSpec((1, cbw, cbw), lambda n, t: (n, 0, 0))
    vshape = jax.ShapeDtypeStruct((1, c), F32)
    wshape = jax.ShapeDtypeStruct((RG_BLOCKS, cbw, cbw), F32)
    return pl.pallas_call(
        body, name=name,
        out_shape=(jax.ShapeDtypeStruct((s, c), CD), jax.ShapeDtypeStruct((s, c), CD),
                   jax.ShapeDtypeStruct((CONV_W, c), F32), vshape, vshape, vshape, vshape, wshape, wshape),
        grid=(RG_BLOCKS, nt),
        in_specs=[blk, blk, blk, tail, blk, tail, cwb, vec, wblk, wblk, wblk, wblk, vec, vec, vec],
        out_specs=(blk, blk, cwb, vec, vec, vec, vec, wblk, wblk),
        scratch_shapes=[pltpu.VMEM((1, cbw), F32), pltpu.VMEM((SUBLANES, cbw), F32),
                        pltpu.VMEM((tb, cbw), F32), pltpu.VMEM((tb, cbw), F32), pltpu.VMEM((tb, cbw), F32)],
        compiler_params=_params("parallel", "arbitrary"),
    )(dy, gate_br, x_br, x_br, hs, hs, cw, cb, wr, wi, wrt, wit, br, bi, lam)


ATT_BLOCK = 256
ATT_Q_BLOCK = 1024
ATT_RATIO = ATT_Q_BLOCK // ATT_BLOCK
ATT_SCALE = 1.0 / math.sqrt(SB_HEAD_DIM)
N_PAIRS = SB_HEADS * SB_HEAD_DIM // LANES
NT_DIMS = (((1,), (1,)), ((), ()))
TN_DIMS = (((0,), (0,)), ((), ()))


LOG2E = 1.4426950408889634


def _neg_abs(x):
    bits = lax.bitcast_convert_type(x, jnp.uint32) | jnp.uint32(0x80000000)
    return lax.bitcast_convert_type(bits, F32)


def _qk(qx, kb):
    return lax.dot_general(qx, kb, NT_DIMS, preferred_element_type=F32)


def _sb_logits(qk, valid):
    z2 = qk * (ATT_SCALE * LOG2E)
    lb2 = jnp.minimum(z2, 0.0) - jnp.log2(1.0 + jnp.exp2(_neg_abs(z2)))
    l2 = lb2 - z2
    if valid is not None:
        l2 = jnp.where(valid, l2, 0.0)
    return lb2, l2


def _hi_lo(x):
    hi = x.astype(CD)
    lo = (x - hi.astype(F32)).astype(CD)
    return jnp.concatenate([hi, lo], axis=1)


def _tri(strict, stacked):
    r = lax.broadcasted_iota(jnp.int32, (ATT_BLOCK, ATT_BLOCK), 0)
    c = lax.broadcasted_iota(jnp.int32, (ATT_BLOCK, ATT_BLOCK), 1)
    m = (r > c if strict else r >= c).astype(CD)
    return jnp.concatenate([m, m], axis=0) if stacked else m


def _attn_fwd(qkv, name):
    _, s, _ = qkv.shape
    tq, t = ATT_Q_BLOCK, ATT_BLOCK
    nblk = s // tq

    def body(q_ref, k_ref, v_ref, o_ref, qk_scr, w_scr):
        i = pl.program_id(1)
        lane = lax.broadcasted_iota(jnp.int32, (1, LANES), 1)
        head_masks = (lane < SB_HEAD_DIM, lane >= SB_HEAD_DIM)
        q = q_ref[0]
        qs = [jnp.where(m, q, jnp.zeros_like(q)) for m in head_masks]
        tri = _tri(True, False)
        rr = lax.broadcasted_iota(jnp.int32, (tq, t), 0)
        cc = lax.broadcasted_iota(jnp.int32, (tq, t), 1)

        def rows_of(j):
            return pl.ds(pl.multiple_of(j * t, t), t)

        def tail(x, row0):
            return x if row0 == 0 else x[row0:]

        def start_logits(j, row0=0):
            kb = k_ref[0, rows_of(j), :]
            for hd in range(2):
                qk_scr[hd, row0:, :] = _qk(tail(qs[hd], row0), kb)

        def weights(run, diagonal=False, row0=0):
            new_run = []
            valid = (cc < rr)[:tq - row0] if diagonal else None
            for hd in range(2):
                lb2, l2 = _sb_logits(qk_scr[hd, row0:, :], valid)
                w = jnp.exp2(lb2 + (tail(run[hd], row0) + jnp.dot(l2.astype(CD), tri, preferred_element_type=F32)))
                if valid is not None:
                    w = jnp.where(valid, w, 0.0)
                w_scr[row0:, hd * t:(hd + 1) * t] = w.astype(CD)
                rowsum = jnp.sum(l2, axis=1, keepdims=True)
                if row0:
                    rowsum = jnp.concatenate([jnp.zeros((row0, 1), F32), rowsum], axis=0)
                new_run.append(run[hd] + rowsum)
            return tuple(new_run)

        def apply_weights(j, row0=0):
            vb = v_ref[0, rows_of(j), :]
            vcat = jnp.concatenate([jnp.where(m, vb, jnp.zeros_like(vb)) for m in head_masks], axis=0)
            inc = jnp.dot(w_scr[row0:, :], vcat, preferred_element_type=F32)
            return inc if row0 == 0 else jnp.concatenate([jnp.zeros((row0, LANES), F32), inc], axis=0)

        zero = jnp.zeros((tq, 1), F32)
        last = ATT_RATIO - 1
        start_logits(ATT_RATIO * i + last, last * t)
        run = weights((zero, zero), True, last * t)
        oacc = jnp.zeros((tq, LANES), F32)
        for d in reversed(range(last)):
            start_logits(ATT_RATIO * i + d, d * t)
            oacc = oacc + apply_weights(ATT_RATIO * i + d + 1, (d + 1) * t)
            run = weights(run, True, d * t)
        start_logits(jnp.maximum(ATT_RATIO * i - 1, 0))

        def step(jj, carry):
            run, oacc = carry
            b = ATT_RATIO * i - 1 - jj
            oacc = oacc + apply_weights(b + 1)
            run = weights(run)
            start_logits(jnp.maximum(b - 1, 0))
            return run, oacc

        run, oacc = lax.fori_loop(0, ATT_RATIO * i, step, (run, oacc))
        o_ref[0] = oacc + apply_weights(0)

    return pl.pallas_call(
        body, name=name, out_shape=jax.ShapeDtypeStruct((N_PAIRS, s, LANES), F32), grid=(N_PAIRS, nblk),
        in_specs=[pl.BlockSpec((1, tq, LANES), lambda p, i: (p, i, 0)),
                  pl.BlockSpec((1, s, LANES), lambda p, i: (N_PAIRS + p, 0, 0)),
                  pl.BlockSpec((1, s, LANES), lambda p, i: (2 * N_PAIRS + p, 0, 0))],
        out_specs=pl.BlockSpec((1, tq, LANES), lambda p, i: (p, i, 0)),
        scratch_shapes=[pltpu.VMEM((2, tq, t), F32), pltpu.VMEM((tq, 2 * t), CD)],
        compiler_params=_params("parallel", "arbitrary"),
    )(qkv, qkv, qkv)


def _attn_bwd(qkv, o, do, name):
    _, s, _ = qkv.shape
    tq, t = ATT_Q_BLOCK, ATT_BLOCK
    nblk = s // tq

    def body(q_ref, k_ref, v_ref, o_ref, do_ref, dq_ref, dk_ref, dv_ref, qk_scr, dw_scr, w_scr, dz_scr):
        i = pl.program_id(1)

        @pl.when(i == 0)
        def _():
            dk_ref[...] = jnp.zeros_like(dk_ref)
            dv_ref[...] = jnp.zeros_like(dv_ref)

        lane = lax.broadcasted_iota(jnp.int32, (1, LANES), 1)
        head_masks = (lane < SB_HEAD_DIM, lane >= SB_HEAD_DIM)
        q = q_ref[0]
        dov = do_ref[0]
        ov = o_ref[0]
        qs = [jnp.where(m, q, jnp.zeros_like(q)) for m in head_masks]
        q_scaled_t = jnp.concatenate([(qx.astype(F32) * ATT_SCALE).T for qx in qs], axis=1).astype(CD)
        docs = [jnp.where(m, dov, 0.0).astype(CD) for m in head_masks]
        docat_t = jnp.concatenate([jnp.where(m, dov, 0.0).T for m in head_masks], axis=1).astype(CD)
        totals = [jnp.sum(d.astype(F32) * ov, axis=1, keepdims=True) for d in docs]
        tri = _tri(True, False)
        tri_incl = _tri(False, True)
        rr = lax.broadcasted_iota(jnp.int32, (tq, t), 0)
        cc = lax.broadcasted_iota(jnp.int32, (tq, t), 1)

        def rows_of(j):
            return pl.ds(pl.multiple_of(j * t, t), t)

        def tail(x, row0):
            return x if row0 == 0 else x[row0:]

        def pad_rows(x, row0):
            return x if row0 == 0 else jnp.concatenate([jnp.zeros((row0, x.shape[1]), x.dtype), x], axis=0)

        def start_products(j, row0=0):
            kb = k_ref[0, rows_of(j), :]
            vb = v_ref[0, rows_of(j), :]
            for hd in range(2):
                qk_scr[hd, row0:, :] = _qk(tail(qs[hd], row0), kb)
                dw_scr[hd, row0:, :] = lax.dot_general(tail(docs[hd], row0), vb, NT_DIMS, preferred_element_type=F32)

        def logit_grads(run, erun, diagonal=False, row0=0):
            new_run, new_erun = [], []
            valid = (cc < rr)[:tq - row0] if diagonal else None
            for hd in range(2):
                lb2, l2 = _sb_logits(qk_scr[hd, row0:, :], valid)
                w = jnp.exp2(lb2 + (tail(run[hd], row0) + jnp.dot(l2.astype(CD), tri, preferred_element_type=F32)))
                if valid is not None:
                    w = jnp.where(valid, w, 0.0)
                wc = w.astype(CD)
                w_scr[hd * tq + row0:(hd + 1) * tq, :] = wc
                e = dw_scr[hd, row0:, :] * wc.astype(F32)
                prefix = (tail(totals[hd] - erun[hd], row0)
                          - jnp.dot(_hi_lo(e), tri_incl, preferred_element_type=F32))
                dz = e - jnp.exp2(lb2) * (e + prefix)
                if valid is not None:
                    dz = jnp.where(valid, dz, 0.0)
                dz_scr[hd * tq + row0:(hd + 1) * tq, :] = dz.astype(CD)
                new_run.append(run[hd] + pad_rows(jnp.sum(l2, axis=1, keepdims=True), row0))
                new_erun.append(erun[hd] + pad_rows(jnp.sum(e, axis=1, keepdims=True), row0))
            return tuple(new_run), tuple(new_erun)

        def apply_grads(j, row0=0):
            rows = rows_of(j)
            kb = k_ref[0, rows, :]
            kcat = jnp.concatenate([jnp.where(m, kb, jnp.zeros_like(kb)) for m in head_masks], axis=0)
            dz_heads = [dz_scr[hd * tq + row0:(hd + 1) * tq, :] for hd in range(2)]
            w_heads = [w_scr[hd * tq + row0:(hd + 1) * tq, :] for hd in range(2)]
            q_t = jnp.concatenate([q_scaled_t[:, hd * tq + row0:(hd + 1) * tq] for hd in range(2)], axis=1)
            do_t = jnp.concatenate([docat_t[:, hd * tq + row0:(hd + 1) * tq] for hd in range(2)], axis=1)
            dk_ref[0, :, rows] += jnp.dot(q_t, jnp.concatenate(dz_heads, axis=0), preferred_element_type=F32)
            dv_ref[0, :, rows] += jnp.dot(do_t, jnp.concatenate(w_heads, axis=0), preferred_element_type=F32)
            return pad_rows(jnp.dot(jnp.concatenate(dz_heads, axis=1), kcat, preferred_element_type=F32), row0)

        zero = jnp.zeros((tq, 1), F32)
        last = ATT_RATIO - 1
        start_products(ATT_RATIO * i + last, last * t)
        run, erun = logit_grads((zero, zero), (zero, zero), True, last * t)
        dqacc = jnp.zeros((tq, LANES), F32)
        for d in reversed(range(last)):
            start_products(ATT_RATIO * i + d, d * t)
            dqacc = dqacc + apply_grads(ATT_RATIO * i + d + 1, (d + 1) * t)
            run, erun = logit_grads(run, erun, True, d * t)
        start_products(jnp.maximum(ATT_RATIO * i - 1, 0))

        def step(jj, carry):
            run, erun, dqacc = carry
            b = ATT_RATIO * i - 1 - jj
            dqacc = dqacc + apply_grads(b + 1)
            run, erun = logit_grads(run, erun)
            start_products(jnp.maximum(b - 1, 0))
            return run, erun, dqacc

        run, erun, dqacc = lax.fori_loop(0, ATT_RATIO * i, step, (run, erun, dqacc))
        dq_ref[0] = (dqacc + apply_grads(0)) * ATT_SCALE

    qblk = pl.BlockSpec((1, tq, LANES), lambda p, i: (p, i, 0))
    full = pl.BlockSpec((1, LANES, s), lambda p, i: (p, 0, 0))
    shape = jax.ShapeDtypeStruct((N_PAIRS, s, LANES), F32)
    shape_t = jax.ShapeDtypeStruct((N_PAIRS, LANES, s), F32)
    dq, dk_t, dv_t = pl.pallas_call(
        body, name=name, out_shape=(shape, shape_t, shape_t), grid=(N_PAIRS, nblk),
        in_specs=[qblk,
                  pl.BlockSpec((1, s, LANES), lambda p, i: (N_PAIRS + p, 0, 0)),
                  pl.BlockSpec((1, s, LANES), lambda p, i: (2 * N_PAIRS + p, 0, 0)),
                  qblk, qblk],
        out_specs=(qblk, full, full),
        scratch_shapes=[pltpu.VMEM((2, tq, t), F32), pltpu.VMEM((2, tq, t), F32),
                        pltpu.VMEM((2 * tq, t), CD), pltpu.VMEM((2 * tq, t), CD)],
        compiler_params=_params("parallel", "arbitrary"),
    )(qkv, qkv, qkv, o, do)
    return dq, jnp.swapaxes(dk_t, 1, 2), jnp.swapaxes(dv_t, 1, 2)


def _adamw(w, g, m, v, name):
    shape = w.shape
    rows, cols = (shape[-2], shape[-1]) if len(shape) >= 2 else (1, shape[-1])
    lead = w.size // (rows * cols)
    tr = _pick(rows, (512, 256, 128, 64, 32, 16, 8))

    def body(w_ref, g_ref, m_ref, v_ref, d_ref, nm_ref, nv_ref):
        gv = g_ref[...]
        nm = ADAM_B1 * m_ref[...] + (1.0 - ADAM_B1) * gv
        nv = ADAM_B2 * v_ref[...] + (1.0 - ADAM_B2) * (gv * gv)
        m_hat = nm / (1.0 - ADAM_B1 ** ADAM_STEP)
        v_hat = nv / (1.0 - ADAM_B2 ** ADAM_STEP)
        d_ref[...] = -ADAM_LR * (m_hat / (jnp.sqrt(v_hat) + ADAM_EPS) + ADAM_WD * w_ref[...])
        nm_ref[...] = nm
        nv_ref[...] = nv

    blk = pl.BlockSpec((1, tr, cols), lambda l, i: (l, i, 0))
    out = jax.ShapeDtypeStruct((lead, rows, cols), F32)
    d, nm, nv = pl.pallas_call(
        body, name=name, out_shape=(out, out, out), grid=(lead, rows // tr),
        in_specs=[blk, blk, blk, blk], out_specs=(blk, blk, blk), compiler_params=_params("parallel", "parallel"),
    )(*[a.reshape(lead, rows, cols) for a in (w, g, m, v)])
    return d.reshape(shape), nm.reshape(shape), nv.reshape(shape)


HBM = pl.BlockSpec(memory_space=pltpu.HBM)


def _coords():
    return lax.axis_index("x"), lax.axis_index("y"), lax.axis_index("c")


def _other_chips(x, y):
    return [(1 - x, y), (x, 1 - y), (1 - x, 1 - y)]


def _allgather_chips(shard, name, collective_id=None):
    r, cols = shard.shape
    half = r // 2
    quarter = half // 2

    def body(src_ref, out_ref, send_sems, recv_sems):
        x, y, c = _coords()
        sibling = (x, y, 1 - c)
        nx, ny, diag = (1 - x, y), (x, 1 - y), (1 - x, 1 - y)

        def piece(chip, core, lo, n):
            return out_ref.at[2 * chip[0] + chip[1], pl.ds(core * half + lo, n), :]

        def copy(k, dst, to, src=None):
            return pltpu.make_async_remote_copy(
                src_ref=dst if src is None else src, dst_ref=dst,
                send_sem=send_sems.at[k], recv_sem=recv_sems.at[k], device_id=to, device_id_type=MESH)

        me = (x, y)
        mine = src_ref.at[pl.ds(c * half, half), :]
        direct = [copy(0, piece(me, c, 0, half), (*nx, c), src=mine), copy(1, piece(me, c, 0, half), (*ny, c), src=mine)]
        for cp in direct:
            cp.start()
        arrivals = [piece(nx, c, 0, half), piece(ny, c, 0, half), piece(diag, c, 0, quarter),
                    piece(diag, c, quarter, quarter)]
        onward = [copy(2, piece(nx, c, 0, quarter), (*ny, c)), copy(3, piece(ny, c, quarter, quarter), (*nx, c))]
        to_sibling = [copy(4 + k, dst, sibling) for k, dst in enumerate(arrivals)]
        for k, dst in enumerate(arrivals):
            copy(k, dst, (x, y, c)).wait_recv()
            if k < 2:
                onward[k].start()
            to_sibling[k].start()
        from_sibling = [piece(nx, 1 - c, 0, half), piece(ny, 1 - c, 0, half), piece(diag, 1 - c, 0, quarter),
                        piece(diag, 1 - c, quarter, quarter)]
        for k, dst in enumerate(from_sibling):
            copy(4 + k, dst, (x, y, c)).wait_recv()
        for cp in direct + onward + to_sibling:
            cp.wait_send()

    out_shape = jax.ShapeDtypeStruct((N_CHIPS, r, cols), shard.dtype)
    sems = (pltpu.SemaphoreType.DMA((8,)), pltpu.SemaphoreType.DMA((8,)))
    if collective_id is None:
        return pl.pallas_call(body, name=name, out_shape=out_shape, in_specs=[HBM], out_specs=HBM,
                              scratch_shapes=list(sems))(shard)
    shard_ref = jax.new_ref(shard, memory_space=pltpu.MemorySpace.HBM)
    gathered_ref = jax.empty_ref(out_shape, memory_space=pltpu.MemorySpace.HBM)

    @_sequencer(name, collective_id, sems)
    def launch(send_sems, recv_sems):
        x, y, c = _coords()
        _handshake([(1 - x, y, c), (x, 1 - y, c), (x, y, 1 - c)])
        body(shard_ref, gathered_ref, send_sems, recv_sems)

    launch()
    return gathered_ref[...]


def _exchange_sibling_halves(g, name):
    n, r, cols = g.shape
    half = r // 2

    def body(g_ref, out_ref, send_sem, recv_sem):
        x, y, c = _coords()
        cp = pltpu.make_async_remote_copy(
            src_ref=g_ref.at[:, pl.ds((1 - c) * half, half), :], dst_ref=out_ref,
            send_sem=send_sem, recv_sem=recv_sem, device_id=(x, y, 1 - c), device_id_type=MESH)
        cp.start()
        cp.wait()

    return pl.pallas_call(
        body, name=name, out_shape=jax.ShapeDtypeStruct((n, half, cols), g.dtype),
        in_specs=[HBM], out_specs=HBM,
        scratch_shapes=[pltpu.SemaphoreType.DMA, pltpu.SemaphoreType.DMA],
    )(g)


def _sequencer(name, collective_id, scratch_types):
    return pl.kernel(mesh=plsc.ScalarSubcoreMesh(axis_name="sequencer", num_cores=1), name=name,
                     scratch_types=scratch_types, compiler_params=pltpu.CompilerParams(collective_id=collective_id))


def _handshake(peers):
    barrier = pltpu.get_barrier_semaphore()
    for peer in peers:
        pl.semaphore_signal(barrier, inc=1, device_id=peer, device_id_type=MESH)
    pl.semaphore_wait(barrier, len(peers))


def _exchange_sibling_halves_async(g, name, collective_id):
    n, r, cols = g.shape
    half = r // 2
    g_ref = jax.new_ref(g, memory_space=pltpu.MemorySpace.HBM)
    out_ref = jax.empty_ref(jax.ShapeDtypeStruct((n, half, cols), g.dtype), memory_space=pltpu.MemorySpace.HBM)

    @_sequencer(name, collective_id, (pltpu.SemaphoreType.DMA, pltpu.SemaphoreType.DMA))
    def launch(send_sem, recv_sem):
        x, y, c = _coords()
        _handshake([(x, y, 1 - c)])
        cp = pltpu.make_async_remote_copy(
            src_ref=g_ref.at[:, pl.ds((1 - c) * half, half), :], dst_ref=out_ref,
            send_sem=send_sem, recv_sem=recv_sem, device_id=(x, y, 1 - c), device_id_type=MESH)
        cp.start()
        cp.wait()

    launch()
    return out_ref[...]


def _share_halves_async(v, name, collective_id):
    h = v.shape[0] // 2
    v_ref = jax.new_ref(v, memory_space=pltpu.MemorySpace.HBM)

    @_sequencer(name, collective_id, (pltpu.SemaphoreType.DMA, pltpu.SemaphoreType.DMA))
    def launch(send_sem, recv_sem):
        x, y, c = _coords()
        _handshake([(x, y, 1 - c)])
        cp = pltpu.make_async_remote_copy(
            src_ref=v_ref.at[pl.ds(c * h, h), :], dst_ref=v_ref.at[pl.ds(c * h, h), :],
            send_sem=send_sem, recv_sem=recv_sem, device_id=(x, y, 1 - c), device_id_type=MESH)
        cp.start()
        pltpu.make_async_remote_copy(
            src_ref=v_ref.at[pl.ds(c * h, h), :], dst_ref=v_ref.at[pl.ds((1 - c) * h, h), :],
            send_sem=send_sem, recv_sem=recv_sem, device_id=(x, y, 1 - c), device_id_type=MESH).wait_recv()
        cp.wait_send()

    launch()
    return v_ref[...]


def _scatter_to_chips_async(p, name, collective_id):
    p_ref = jax.new_ref(p, memory_space=pltpu.MemorySpace.HBM)
    out_ref = jax.empty_ref(jax.ShapeDtypeStruct(p.shape, p.dtype), memory_space=pltpu.MemorySpace.HBM)

    @_sequencer(name, collective_id, (pltpu.SemaphoreType.DMA((3,)), pltpu.SemaphoreType.DMA((3,))))
    def launch(send_sems, recv_sems):
        x, y, c = _coords()
        me = 2 * x + y
        _handshake([(px, py, c) for px, py in _other_chips(x, y)])
        sends = []
        for j, (px, py) in enumerate(_other_chips(x, y)):
            sends.append(pltpu.make_async_remote_copy(
                src_ref=p_ref.at[2 * px + py], dst_ref=out_ref.at[me],
                send_sem=send_sems.at[j], recv_sem=recv_sems.at[j], device_id=(px, py, c), device_id_type=MESH))
        for cp in sends:
            cp.start()
        for j, (px, py) in enumerate(_other_chips(x, y)):
            pltpu.make_async_remote_copy(
                src_ref=p_ref.at[me], dst_ref=out_ref.at[2 * px + py],
                send_sem=send_sems.at[j], recv_sem=recv_sems.at[j], device_id=(px, py, c),
                device_id_type=MESH).wait_recv()
        for cp in sends:
            cp.wait_send()

    launch()
    return out_ref[...]


def _share_halves(v, name):
    h = v.shape[0] // 2

    def body(v_ref, out_ref, send_sem, recv_sem):
        x, y, c = _coords()
        cp = pltpu.make_async_remote_copy(
            src_ref=v_ref.at[pl.ds(c * h, h), :], dst_ref=out_ref.at[pl.ds(c * h, h), :],
            send_sem=send_sem, recv_sem=recv_sem, device_id=(x, y, 1 - c), device_id_type=MESH)
        cp.start()
        pltpu.make_async_remote_copy(
            src_ref=v_ref.at[pl.ds(c * h, h), :], dst_ref=out_ref.at[pl.ds((1 - c) * h, h), :],
            send_sem=send_sem, recv_sem=recv_sem, device_id=(x, y, 1 - c), device_id_type=MESH).wait_recv()
        cp.wait_send()

    return pl.pallas_call(
        body, name=name, out_shape=jax.ShapeDtypeStruct(v.shape, v.dtype),
        in_specs=[HBM], out_specs=HBM, input_output_aliases={0: 0},
        scratch_shapes=[pltpu.SemaphoreType.DMA, pltpu.SemaphoreType.DMA],
    )(v)


def _allreduce_small(v, name):
    r, cols = v.shape

    def body(v_ref, out_ref, buf_ref, send_sems, recv_sems):
        x, y, c = _coords()
        me = 4 * x + 2 * y + c
        buf_ref[me] = v_ref[...]
        sends = []
        for k in range(1, N_DEV):
            px = 1 - x if k & 4 else x
            py = 1 - y if k & 2 else y
            pc = 1 - c if k & 1 else c
            sends.append(pltpu.make_async_remote_copy(
                src_ref=v_ref, dst_ref=buf_ref.at[me], send_sem=send_sems.at[k - 1], recv_sem=recv_sems.at[k - 1],
                device_id=(px, py, pc), device_id_type=MESH))
        for cp in sends:
            cp.start()
        for cp in sends:
            cp.wait()
        acc = buf_ref[0]
        for d in range(1, N_DEV):
            acc = acc + buf_ref[d]
        out_ref[...] = acc

    return pl.pallas_call(
        body, name=name, out_shape=jax.ShapeDtypeStruct((r, cols), F32),
        in_specs=[pl.BlockSpec(memory_space=pltpu.VMEM)], out_specs=pl.BlockSpec(memory_space=pltpu.VMEM),
        scratch_shapes=[pltpu.VMEM((N_DEV, r, cols), F32), pltpu.SemaphoreType.DMA((N_DEV - 1,)),
                        pltpu.SemaphoreType.DMA((N_DEV - 1,))],
    )(v)


def _add_sibling(g, from_sibling, core, name):
    n, h, cols = from_sibling.shape
    tr = _row_tile(h)
    steps = h // tr

    def body(core_ref, a_ref, b_ref, o_ref):
        o_ref[...] = (a_ref[...] + b_ref[...]).astype(o_ref.dtype)

    return pl.pallas_call(
        body, name=name, out_shape=jax.ShapeDtypeStruct(from_sibling.shape, jnp.bfloat16),
        grid_spec=pltpu.PrefetchScalarGridSpec(
            num_scalar_prefetch=1, grid=(n, steps),
            in_specs=[pl.BlockSpec((1, tr, cols), lambda s, i, core_ref: (s, core_ref[0] * steps + i, 0)),
                      pl.BlockSpec((1, tr, cols), lambda s, i, core_ref: (s, i, 0))],
            out_specs=pl.BlockSpec((1, tr, cols), lambda s, i, core_ref: (s, i, 0))),
        compiler_params=_params("parallel", "parallel"),
    )(core.reshape(1).astype(jnp.int32), g, from_sibling)


def _sum_slots(p, own, chip, core, name):
    n, r, cols = p.shape
    tr = _row_tile(r)
    steps = r // tr

    def body(core_ref, chip_ref, p_ref, own_ref, o_ref):
        parts = [jnp.where(chip_ref[0] == s, own_ref[0], p_ref[s]).astype(F32) for s in range(n)]
        o_ref[...] = ((parts[0] + parts[1]) + parts[2]) + parts[3]

    return pl.pallas_call(
        body, name=name, out_shape=jax.ShapeDtypeStruct((2 * r, cols), F32),
        grid_spec=pltpu.PrefetchScalarGridSpec(
            num_scalar_prefetch=2, grid=(steps,),
            in_specs=[pl.BlockSpec((n, tr, cols), lambda i, core_ref, chip_ref: (0, i, 0)),
                      pl.BlockSpec((1, tr, cols), lambda i, core_ref, chip_ref: (chip_ref[0], i, 0))],
            out_specs=pl.BlockSpec((tr, cols), lambda i, core_ref, chip_ref: (core_ref[0] * steps + i, 0))),
        compiler_params=_params("parallel"),
    )(core.reshape(1).astype(jnp.int32), chip.reshape(1).astype(jnp.int32), p, own)


PACK_COLS = 1024


def _pack_shards(parts):
    return jnp.concatenate([p.reshape(-1, PACK_COLS) for p in parts], axis=0)


def _unpack_shards(buf, shapes):
    out, row = [], 0
    for shp in shapes:
        nrows = math.prod(shp) // PACK_COLS
        out.append(buf[..., row:row + nrows, :].reshape(buf.shape[:-2] + tuple(shp)))
        row += nrows
    return out


def _local_step(x, target, w):
    t = lambda a: a.T
    g = {}
    h0 = _rms_fwd(x, w["norm_mix_g"][0], "rms_mix0")
    w_in_g, w_in_x = w["a_w_in"][:, :D_RNN], w["a_w_in"][:, D_RNN:]
    gate_br = _matmul([(h0, w_in_g)], F32, "mm_a_gate")
    x_br = _matmul([(h0, w_in_x)], F32, "mm_a_xbr")
    y_a, hs = _rglru_fwd(gate_br, x_br, w["a_conv_w"], w["a_conv_b"], w["a_w_r"], w["a_w_i"], w["a_b_r"],
                         w["a_b_i"], w["a_lambda"], "rglru_fwd")
    x1, h1 = _matmul([(y_a, w["a_w_out"])], F32, "mm_a_out", addend=x, norm_gain=w["norm_ffn_g"][0])
    fg0, fu0, act0 = _ffn_up(h1, w["ffn_w_gate"][0], w["ffn_w_up"][0], "ffn0_up")
    x2, h2 = _matmul([(act0, w["ffn_w_down"][0])], F32, "mm_f0_down", addend=x1, norm_gain=w["norm_mix_g"][1])
    qkv = _matmul([(h2, w["b_w_qkv"])], CD, "mm_b_qkv", out_lbm=True, tn=1024)
    o = _attn_fwd(qkv, "attn_fwd")
    x3, h3 = _matmul([(o, w["b_w_out"])], F32, "mm_b_out", a_lbm=True, addend=x2, norm_gain=w["norm_ffn_g"][1])
    fg1, fu1, act1 = _ffn_up(h3, w["ffn_w_gate"][1], w["ffn_w_up"][1], "ffn1_up")
    x4 = _matmul([(act1, w["ffn_w_down"][1])], F32, "mm_f1_down", addend=x3)
    loss, dx4, dx4c, g["final_g"] = _loss_head(x4, w["final_g"], target, "loss_head")

    def ffn_bwd(dx_out, dxc, h, x_in, fg, fu, act, layer, tag):
        dg, du = _ffn_dact(dxc, t(w["ffn_w_down"][layer]), fg, fu, "ffn_" + tag + "_dact")
        dwd = _matmul([(act, dxc)], F32, "mm_" + tag + "_dwd", trans_a=True)
        dwg = _matmul([(h, dg)], F32, "mm_" + tag + "_dwg", trans_a=True)
        dwu = _matmul([(h, du)], F32, "mm_" + tag + "_dwu", trans_a=True)
        dx_in, dx_in_c, dgain = _matmul([(dg, t(w["ffn_w_gate"][layer])), (du, t(w["ffn_w_up"][layer]))], F32,
                                        "mm_" + tag + "_dh", norm_bwd=(x_in, w["norm_ffn_g"][layer], dx_out))
        return dx_in, dx_in_c, dgain, dwg, dwu, dwd

    dx3, dx3c, dgf1, dwg1, dwu1, dwd1 = ffn_bwd(dx4, dx4c, h3, x3, fg1, fu1, act1, 1, "f1")
    do = _matmul([(dx3c, t(w["b_w_out"]))], F32, "mm_b_do", out_lbm=True, tn=1024)
    g["b_w_out"] = _matmul([(o, dx3c)], F32, "mm_b_dwout", trans_a=True, a_lbm=True)
    dq, dk, dv = _attn_bwd(qkv, o, do, "attn_bwd")
    wq_t = t(w["b_w_qkv"])
    parts = (dq, dk, dv)
    g["b_w_qkv"] = jnp.concatenate(
        [_matmul([(h2, p)], F32, "mm_b_dwqkv%d" % n, trans_a=True, b_lbm=True) for n, p in enumerate(parts)], axis=1)
    dx2, dx2c, dgm1 = _matmul([(p, wq_t[n * D_MODEL:(n + 1) * D_MODEL]) for n, p in enumerate(parts)], F32, "mm_b_dh",
                              a_lbm=True, norm_bwd=(x2, w["norm_mix_g"][1], dx3))
    dx1, dx1c, dgf0, dwg0, dwu0, dwd0 = ffn_bwd(dx2, dx2c, h1, x1, fg0, fu0, act0, 0, "f0")
    dy_a = _matmul([(dx1c, t(w["a_w_out"]))], F32, "mm_a_dy")
    g["a_w_out"] = _matmul([(y_a, dx1c)], F32, "mm_a_dwout", trans_a=True)
    wrt = jnp.swapaxes(w["a_w_r"], 1, 2)
    wit = jnp.swapaxes(w["a_w_i"], 1, 2)
    (dgate, dxbr, g["a_conv_w"], g["a_conv_b"], g["a_b_r"], g["a_b_i"], g["a_lambda"], g["a_w_r"],
     g["a_w_i"]) = _rglru_bwd(dy_a, gate_br, x_br, hs, w["a_conv_w"], w["a_conv_b"], w["a_w_r"], w["a_w_i"], wrt, wit,
                              w["a_b_r"], w["a_b_i"], w["a_lambda"], "rglru_bwd")
    g["a_w_in"] = jnp.concatenate([_matmul([(h0, dgate)], F32, "mm_a_dwin_g", trans_a=True),
                                   _matmul([(h0, dxbr)], F32, "mm_a_dwin_x", trans_a=True)], axis=1)
    dx0, _, dgm0 = _matmul([(dgate, t(w_in_g)), (dxbr, t(w_in_x))], F32, "mm_a_dh",
                           norm_bwd=(x, w["norm_mix_g"][0], dx1))
    g["norm_mix_g"] = jnp.concatenate([dgm0, dgm1], axis=0)
    g["norm_ffn_g"] = jnp.concatenate([dgf0, dgf1], axis=0)
    g["ffn_w_gate"] = [dwg0, dwg1]
    g["ffn_w_up"] = [dwu0, dwu1]
    g["ffn_w_down"] = [dwd0, dwd1]
    return loss, dx0, g


WEIGHTS = ["norm_mix_g", "norm_ffn_g", "a_w_in", "a_conv_w", "a_conv_b", "a_w_r", "a_b_r", "a_w_i", "a_b_i",
           "a_lambda", "a_w_out", "b_w_qkv", "b_w_out", "ffn_w_gate", "ffn_w_up", "ffn_w_down", "final_g"]
BIG = [("a_w_in", 2), ("a_w_r", 2), ("a_w_i", 2), ("a_w_out", 1), ("b_w_qkv", 2), ("b_w_out", 1),
       ("ffn_w_gate", 2), ("ffn_w_up", 2), ("ffn_w_down", 1)]
LAYER1 = ["b_w_qkv", "b_w_out", "ffn_w_gate", "ffn_w_up", "ffn_w_down"]
LAYER0 = ["a_w_in", "a_w_r", "a_w_i", "a_w_out", "ffn_w_gate", "ffn_w_up", "ffn_w_down"]
RS_COLLECTIVE_IDS = {"chips1": 3, "chips0": 4, "sibling1": 5, "share1": 6}
GATHER_COLLECTIVE_IDS = (8, 7)
SMALL = ["norm_mix_g", "norm_ffn_g", "a_conv_w", "a_conv_b", "a_b_r", "a_b_i", "a_lambda", "final_g"]


def _split_chips(full, axis):
    if axis == 1:
        return full.reshape((N_CHIPS, 1, full.shape[1] // N_CHIPS) + full.shape[2:])
    return jnp.stack(jnp.split(full, N_CHIPS, axis=axis))


def _step(x, target, weights, moments_m, moments_v):
    chip = 2 * lax.axis_index("x") + lax.axis_index("y")
    core = lax.axis_index("c")
    axis_of = dict(BIG)
    full = {}
    for group, layer, tag, collective_id in ((LAYER0[:4], 0, "0a", None), (LAYER0[4:], 0, "0f", GATHER_COLLECTIVE_IDS[0]),
                                             (LAYER1, 1, "1", GATHER_COLLECTIVE_IDS[1])):
        shards = [weights[n][layer % weights[n].shape[0]].astype(CD) for n in group]
        gathered = _allgather_chips(_pack_shards(shards), "allgather_weights" + tag, collective_id)
        for n, own, stack in zip(group, shards, _unpack_shards(gathered, [sh.shape for sh in shards])):
            joined = jnp.concatenate([jnp.where(chip == s, own, stack[s]) for s in range(N_CHIPS)],
                                     axis=axis_of[n] - 1)
            full.setdefault(n, {})[layer] = joined
    full = {n: (v[0] if n.startswith("a_") else v[1] if n.startswith("b_") else [v[0], v[1]]) for n, v in full.items()}
    cw_rows = jnp.zeros((N_CHIPS, CONV_W, RG_BW), F32)
    cw_rows = lax.dynamic_update_slice(cw_rows, jnp.where(core == 0, weights["a_conv_w"], 0.0), (chip, 0, 0))
    cw_all = _allreduce_small(cw_rows.reshape(-1, LANES), "allgather_conv_w").reshape(N_CHIPS, CONV_W, RG_BW)
    full["a_conv_w"] = jnp.concatenate([cw_all[s] for s in range(N_CHIPS)], axis=1)
    for n in ("norm_mix_g", "norm_ffn_g", "final_g"):
        full[n] = weights[n]
    for n in ("a_conv_b", "a_b_r", "a_b_i", "a_lambda"):
        full[n] = weights[n]
    loss, dx, grads = _local_step(x[0], target[0], full)
    small_parts = [grads[n].reshape(-1) for n in SMALL] + [loss.reshape(-1)]
    sizes = [p.shape[0] for p in small_parts]
    small = _allreduce_small(jnp.concatenate(small_parts).reshape(-1, LANES), "allreduce_small").reshape(-1)
    red, pos = {}, 0
    for n, sz in zip(SMALL + ["loss"], sizes):
        red[n] = small[pos:pos + sz]
        pos += sz
    loss_out = red["loss"][0]
    g_out = {}
    for n in SMALL:
        if n == "a_conv_w":
            g_out[n] = lax.dynamic_slice(red[n].reshape(CONV_W, D_RNN), (0, chip * RG_BW), (CONV_W, RG_BW)).reshape(
                weights[n].shape)
        else:
            g_out[n] = red[n].reshape(weights[n].shape)
    axis_of = dict(BIG)
    pieces = {}
    for group, layer, tag in ((LAYER1, 1, "1"), (LAYER0, 0, "0")):
        stacks, shapes = [], []
        for n in group:
            per_layer = isinstance(grads[n], list)
            gfull = grads[n][layer] if per_layer else grads[n]
            shard_shape = weights[n].shape[1:]
            gfull = gfull.reshape((1,) + gfull.shape)
            stacks.append(_split_chips(gfull, axis_of[n]).reshape(N_CHIPS, -1, PACK_COLS))
            shapes.append((1,) + tuple(shard_shape))
        gbuf = jnp.concatenate(stacks, axis=1)
        if layer == 1:
            from_sibling = _exchange_sibling_halves_async(gbuf, "rs_sibling" + tag, RS_COLLECTIVE_IDS["sibling1"])
        else:
            from_sibling = _exchange_sibling_halves(gbuf, "rs_sibling" + tag)
        chip_partial = _add_sibling(gbuf, from_sibling, core, "rs_add" + tag)
        from_chips = _scatter_to_chips_async(chip_partial, "rs_chips" + tag, RS_COLLECTIVE_IDS["chips" + tag])
        halves = _sum_slots(from_chips, chip_partial, chip, core, "rs_sum" + tag)
        if layer == 1:
            reduced = _share_halves_async(halves, "rs_share" + tag, RS_COLLECTIVE_IDS["share1"])
        else:
            reduced = _share_halves(halves, "rs_share" + tag)
        for n, piece in zip(group, _unpack_shards(reduced, shapes)):
            pieces.setdefault(n, {})[layer] = piece
    for n, _ in BIG:
        layers = pieces[n]
        g_out[n] = jnp.concatenate([layers[k] for k in sorted(layers)], axis=0)
    outs_g, outs_d, outs_m, outs_v = [], [], [], []
    for n in WEIGHTS:
        d, nm, nv = _adamw(weights[n], g_out[n], moments_m[n], moments_v[n], "adamw_" + n)
        outs_g.append(g_out[n])
        outs_d.append(d)
        outs_m.append(nm)
        outs_v.append(nv)
    return (loss_out, dx[None], *outs_g, *outs_d, *outs_m, *outs_v)


def kernel(x, norm_mix_g, norm_ffn_g, a_w_in, a_conv_w, a_conv_b, a_w_r, a_b_r, a_w_i, a_b_i, a_lambda, a_w_out, b_w_qkv, b_w_out, ffn_w_gate, ffn_w_up, ffn_w_down, final_g, loss_target, m_norm_mix_g, m_norm_ffn_g, m_a_w_in, m_a_conv_w, m_a_conv_b, m_a_w_r, m_a_b_r, m_a_w_i, m_a_b_i, m_a_lambda, m_a_w_out, m_b_w_qkv, m_b_w_out, m_ffn_w_gate, m_ffn_w_up, m_ffn_w_down, m_final_g, v_norm_mix_g, v_norm_ffn_g, v_a_w_in, v_a_conv_w, v_a_conv_b, v_a_w_r, v_a_b_r, v_a_w_i, v_a_b_i, v_a_lambda, v_a_w_out, v_b_w_qkv, v_b_w_out, v_ffn_w_gate, v_ffn_w_up, v_ffn_w_down, v_final_g):
    ws = [norm_mix_g, norm_ffn_g, a_w_in, a_conv_w, a_conv_b, a_w_r, a_b_r, a_w_i, a_b_i, a_lambda, a_w_out, b_w_qkv,
          b_w_out, ffn_w_gate, ffn_w_up, ffn_w_down, final_g]
    ms = [m_norm_mix_g, m_norm_ffn_g, m_a_w_in, m_a_conv_w, m_a_conv_b, m_a_w_r, m_a_b_r, m_a_w_i, m_a_b_i, m_a_lambda,
          m_a_w_out, m_b_w_qkv, m_b_w_out, m_ffn_w_gate, m_ffn_w_up, m_ffn_w_down, m_final_g]
    vs = [v_norm_mix_g, v_norm_ffn_g, v_a_w_in, v_a_conv_w, v_a_conv_b, v_a_w_r, v_a_b_r, v_a_w_i, v_a_b_i, v_a_lambda,
          v_a_w_out, v_b_w_qkv, v_b_w_out, v_ffn_w_gate, v_ffn_w_up, v_ffn_w_down, v_final_g]
    return _step(x, loss_target, dict(zip(WEIGHTS, ws)), dict(zip(WEIGHTS, ms)), dict(zip(WEIGHTS, vs)))
```

```python
import functools
import math

import jax
import jax.numpy as jnp
from jax import lax
from jax.experimental import pallas as pl
from jax.experimental.pallas import tpu as pltpu
from jax.experimental.pallas import tpu_sc as plsc

F32 = jnp.float32
CD = jnp.bfloat16

D_MODEL = 1024
D_RNN = 1024
RG_BLOCKS = 4
RG_BW = 256
CONV_W = 4
RG_C = 8.0
SB_HEADS = 16
SB_HEAD_DIM = 64
D_FF = 2816
RMS_EPS = 1e-6
N_CHIPS = 4
N_DEV = 8

ADAM_LR = 0.001
ADAM_B1 = 0.9
ADAM_B2 = 0.999
ADAM_EPS = 1e-08
ADAM_WD = 0.01
ADAM_STEP = 10

LANES = 128
VMEM_LIMIT = 56 * 1024 * 1024
MESH = pl.DeviceIdType.MESH


def _params(*sem):
    return pltpu.CompilerParams(dimension_semantics=sem, vmem_limit_bytes=VMEM_LIMIT)


def _pick(n, prefs):
    for p in prefs:
        if n % p == 0:
            return p
    return n


def _row_tile(rows):
    return max(d for d in range(16, 1025, 16) if rows % d == 0)


def _matmul(pairs, out_dtype, name, *, trans_a=False, a_lbm=False, b_lbm=False, out_lbm=False, addend=None,
            tm=512, tn=None, tk=None, norm_gain=None, norm_bwd=None, loss_head=None):
    a0, b0 = pairs[0]
    if trans_a:
        kdim = a0.shape[1] if a_lbm else a0.shape[0]
        m = a0.shape[0] * LANES if a_lbm else a0.shape[1]
    else:
        m = a0.shape[1] if a_lbm else a0.shape[0]
        kdim = a0.shape[0] * LANES if a_lbm else a0.shape[1]
    n = b0.shape[0] * LANES if b_lbm else b0.shape[1]
    tm = _pick(m, (tm, 1408, 256, 128))
    tn = tn or _pick(n, (1408, 1024, 768, 512, 256, 128))
    tk = tk or _pick(kdim, (1024, 1408, 512, 256, 128))
    nk = kdim // tk
    npair = len(pairs)

    def cat(ref):
        return jnp.concatenate([ref[p] for p in range(ref.shape[0])], axis=-1)

    def body(*refs):
        ins = refs[: 2 * npair]
        pos = 2 * npair
        add_ref = None
        if addend is not None:
            add_ref = refs[pos]
            pos += 1
        gain_ref = x_ref = dxin_ref = None
        if norm_gain is not None:
            gain_ref = refs[pos]
            pos += 1
        if norm_bwd is not None:
            x_ref, gain_ref, dxin_ref = refs[pos:pos + 3]
            pos += 3
        if loss_head is not None:
            gain_ref, target_ref = refs[pos:pos + 2]
            pos += 2
        o_ref = refs[pos]
        extra_out = refs[pos + 1:-1]
        acc_ref = refs[-1]
        k = pl.program_id(2)

        @pl.when(k == 0)
        def _():
            acc_ref[...] = jnp.zeros_like(acc_ref)

        if norm_bwd is not None or loss_head is not None:
            @pl.when((k == 0) & (pl.program_id(0) == 0))
            def _():
                for ref in extra_out[1:]:
                    ref[...] = jnp.zeros_like(ref)

        acc = acc_ref[...]
        for p in range(npair):
            a = (cat(ins[2 * p]) if a_lbm else ins[2 * p][...]).astype(CD)
            b = (cat(ins[2 * p + 1]) if b_lbm else ins[2 * p + 1][...]).astype(CD)
            dims = (((0,), (0,)), ((), ())) if trans_a else (((1,), (0,)), ((), ()))
            acc = acc + lax.dot_general(a, b, dims, preferred_element_type=F32)
        acc_ref[...] = acc

        @pl.when(k == nk - 1)
        def _():
            res = acc_ref[...]
            if add_ref is not None:
                res = res + add_ref[...]
            if norm_gain is not None:
                rinv = lax.rsqrt(jnp.mean(res * res, axis=-1, keepdims=True) + RMS_EPS)
                extra_out[0][...] = (res * rinv * gain_ref[...]).astype(CD)
            if norm_bwd is not None:
                xv = x_ref[...]
                rinv = lax.rsqrt(jnp.mean(xv * xv, axis=-1, keepdims=True) + RMS_EPS)
                nrm = xv * rinv
                dn = res * gain_ref[...]
                extra_out[1][...] += jnp.sum(res * nrm, axis=0, keepdims=True)
                res = dxin_ref[...] + rinv * (dn - nrm * jnp.mean(dn * nrm, axis=-1, keepdims=True))
                extra_out[0][...] = res.astype(CD)
            if loss_head is not None:
                gv = gain_ref[...]
                rinv = lax.rsqrt(jnp.mean(res * res, axis=-1, keepdims=True) + RMS_EPS)
                nrm = res * rinv
                err = nrm * gv - target_ref[...]
                extra_out[2][...] += 0.5 * jnp.sum(jnp.mean(err * err, axis=-1, keepdims=True), axis=0, keepdims=True)
                dy = err * (1.0 / n)
                dn = dy * gv
                extra_out[1][...] += jnp.sum(dy * nrm, axis=0, keepdims=True)
                res = rinv * (dn - nrm * jnp.mean(dn * nrm, axis=-1, keepdims=True))
                extra_out[0][...] = res.astype(CD)
            res = res.astype(out_dtype)
            if out_lbm:
                for p in range(tn // LANES):
                    o_ref[p] = res[:, p * LANES:(p + 1) * LANES]
            else:
                o_ref[...] = res

    if trans_a:
        a_spec = (pl.BlockSpec((tm // LANES, tk, LANES), lambda i, j, k: (i, k, 0)) if a_lbm
                  else pl.BlockSpec((tk, tm), lambda i, j, k: (k, i)))
    else:
        a_spec = (pl.BlockSpec((tk // LANES, tm, LANES), lambda i, j, k: (k, i, 0)) if a_lbm
                  else pl.BlockSpec((tm, tk), lambda i, j, k: (i, k)))
    b_spec = (pl.BlockSpec((tn // LANES, tk, LANES), lambda i, j, k: (j, k, 0)) if b_lbm
              else pl.BlockSpec((tk, tn), lambda i, j, k: (k, j)))
    in_specs = []
    args = []
    for a, b in pairs:
        in_specs += [a_spec, b_spec]
        args += [a, b]
    if addend is not None:
        in_specs.append(pl.BlockSpec((tm, tn), lambda i, j, k: (i, j)))
        args.append(addend)
    tile = pl.BlockSpec((tm, tn), lambda i, j, k: (i, j))
    vec = pl.BlockSpec((1, tn), lambda i, j, k: (0, j))
    if out_lbm:
        out_shape = jax.ShapeDtypeStruct((n // LANES, m, LANES), out_dtype)
        out_spec = pl.BlockSpec((tn // LANES, tm, LANES), lambda i, j, k: (j, i, 0))
    else:
        out_shape = jax.ShapeDtypeStruct((m, n), out_dtype)
        out_spec = tile
    sem = ("parallel", "parallel", "arbitrary")
    if norm_gain is not None or norm_bwd is not None or loss_head is not None:
        assert tn == n and not out_lbm, "the norm needs whole rows in one tile"
        out_shape, out_spec = [out_shape, jax.ShapeDtypeStruct((m, n), CD)], [out_spec, tile]
    if norm_gain is not None:
        in_specs.append(vec)
        args.append(norm_gain.reshape(1, n))
    if norm_bwd is not None:
        x_in, gain, dx_in = norm_bwd
        in_specs += [tile, vec, tile]
        args += [x_in, gain.reshape(1, n), dx_in]
        out_shape.append(jax.ShapeDtypeStruct((1, n), F32))
        out_spec.append(vec)
        sem = ("arbitrary", "arbitrary", "arbitrary")
    if loss_head is not None:
        gain, target = loss_head
        in_specs += [vec, tile]
        args += [gain.reshape(1, n), target]
        out_shape += [jax.ShapeDtypeStruct((1, n), F32), jax.ShapeDtypeStruct((1, LANES), F32)]
        out_spec += [vec, pl.BlockSpec((1, LANES), lambda i, j, k: (0, 0))]
        sem = ("arbitrary", "arbitrary", "arbitrary")
    return pl.pallas_call(
        body, name=name, out_shape=out_shape, grid=(m // tm, n // tn, nk),
        in_specs=in_specs, out_specs=out_spec,
        scratch_shapes=[pltpu.VMEM((tm, tn), F32)],
        compiler_params=_params(*sem),
    )(*args)


def _norm_and_project(x, g, w_a, w_b, name):
    s, d = x.shape
    n = w_a.shape[1]
    tm = _pick(s, (512, 256))

    def body(x_ref, g_ref, wa_ref, wb_ref, h_ref, a_ref, b_ref):
        xv = x_ref[...]
        rinv = lax.rsqrt(jnp.mean(xv * xv, axis=-1, keepdims=True) + RMS_EPS)
        h = (xv * rinv * g_ref[...]).astype(CD)
        h_ref[...] = h
        a_ref[...] = jnp.dot(h, wa_ref[...], preferred_element_type=F32)
        b_ref[...] = jnp.dot(h, wb_ref[...], preferred_element_type=F32)

    row = pl.BlockSpec((tm, d), lambda i: (i, 0))
    out = pl.BlockSpec((tm, n), lambda i: (i, 0))
    wspec = pl.BlockSpec((d, n), lambda i: (0, 0))
    return pl.pallas_call(
        body, name=name,
        out_shape=(jax.ShapeDtypeStruct((s, d), CD), jax.ShapeDtypeStruct((s, n), F32),
                   jax.ShapeDtypeStruct((s, n), F32)),
        grid=(s // tm,), in_specs=[row, pl.BlockSpec((1, d), lambda i: (0, 0)), wspec, wspec],
        out_specs=(row, out, out), compiler_params=_params("parallel"),
    )(x, g.reshape(1, d), w_a, w_b)


def _sigmoid(z):
    return 1.0 / (1.0 + jnp.exp(-z))


FFN_TM = 512
FFN_TN = 1408


def _ffn_up(h, wg, wu, name):
    s, d = h.shape
    f = wg.shape[1]
    tm = _pick(s, (FFN_TM, 256))

    def body(h_ref, wg_ref, wu_ref, g_ref, u_ref, a_ref):
        hv = h_ref[...]
        gv = jnp.dot(hv, wg_ref[...], preferred_element_type=F32)
        uv = jnp.dot(hv, wu_ref[...], preferred_element_type=F32)
        g_ref[...] = gv
        u_ref[...] = uv
        a_ref[...] = (gv * _sigmoid(gv) * uv).astype(CD)

    a_spec = pl.BlockSpec((tm, d), lambda i, j: (i, 0))
    w_spec = pl.BlockSpec((d, FFN_TN), lambda i, j: (0, j))
    o_spec = pl.BlockSpec((tm, FFN_TN), lambda i, j: (i, j))
    return pl.pallas_call(
        body, name=name,
        out_shape=(jax.ShapeDtypeStruct((s, f), F32), jax.ShapeDtypeStruct((s, f), F32),
                   jax.ShapeDtypeStruct((s, f), CD)),
        grid=(s // tm, f // FFN_TN), in_specs=[a_spec, w_spec, w_spec], out_specs=(o_spec, o_spec, o_spec),
        compiler_params=_params("parallel", "parallel"),
    )(h, wg, wu)


def _ffn_dact(dxc, wd_t, g, u, name):
    s, d = dxc.shape
    f = wd_t.shape[1]
    tm = _pick(s, (FFN_TM, 256))

    def body(dx_ref, w_ref, g_ref, u_ref, dg_ref, du_ref):
        da = jnp.dot(dx_ref[...], w_ref[...], preferred_element_type=F32)
        gv = g_ref[...]
        sg = _sigmoid(gv)
        silu = gv * sg
        dg_ref[...] = (da * u_ref[...] * (sg + silu * (1.0 - sg))).astype(CD)
        du_ref[...] = (da * silu).astype(CD)

    a_spec = pl.BlockSpec((tm, d), lambda i, j: (i, 0))
    w_spec = pl.BlockSpec((d, FFN_TN), lambda i, j: (0, j))
    o_spec = pl.BlockSpec((tm, FFN_TN), lambda i, j: (i, j))
    return pl.pallas_call(
        body, name=name,
        out_shape=(jax.ShapeDtypeStruct((s, f), CD), jax.ShapeDtypeStruct((s, f), CD)),
        grid=(s // tm, f // FFN_TN), in_specs=[a_spec, w_spec, o_spec, o_spec], out_specs=(o_spec, o_spec),
        compiler_params=_params("parallel", "parallel"),
    )(dxc, wd_t, g, u)


TIME_BLOCK = 512
SUBLANES = 8
GELU_C = math.sqrt(2.0 / math.pi)
GELU_A = 0.044715


def _gelu(x):
    return 0.5 * x * (1.0 + jnp.tanh(GELU_C * (x + GELU_A * x * x * x)))


def _gelu_grad(x):
    t = jnp.tanh(GELU_C * (x + GELU_A * x * x * x))
    return 0.5 * (1.0 + t) + 0.5 * x * (1.0 - t * t) * GELU_C * (1.0 + 3.0 * GELU_A * x * x)


def _neg_expm1(x):
    series = -x * (1.0 + x * (0.5 + x * (1.0 / 6.0 + x * (1.0 / 24.0))))
    return jnp.where(x > -0.05, series, 1.0 - jnp.exp(x))


def _log_sigmoid(x):
    return jnp.minimum(x, 0.0) - jnp.log1p(jnp.exp(-jnp.abs(x)))


def _shift_down(x, tail, s):
    if s == 0:
        return x
    ext = jnp.concatenate([tail, x], axis=0)
    return pltpu.roll(ext, s, axis=0)[SUBLANES:]


def _shift_up(x, head, s):
    if s == 0:
        return x
    n = x.shape[0]
    ext = jnp.concatenate([x, head], axis=0)
    return pltpu.roll(ext, n + SUBLANES - s, axis=0)[:n]


def _rg_gates(xbr, tail, cw_ref, cb, wr, wi, br, bi, ls):
    taps = [_shift_down(xbr, tail, CONV_W - 1 - k) for k in range(CONV_W)]
    xc = cb
    for k in range(CONV_W):
        xc = xc + cw_ref[pl.ds(k, 1), :] * taps[k]
    xcd = xc.astype(CD)
    r = _sigmoid(jnp.dot(xcd, wr, preferred_element_type=F32) + br)
    i = _sigmoid(jnp.dot(xcd, wi, preferred_element_type=F32) + bi)
    log_a = RG_C * r * ls
    a = jnp.exp(log_a)
    mult = jnp.sqrt(jnp.maximum(_neg_expm1(2.0 * log_a), 0.0))
    return taps, xc, r, i, log_a, a, mult


def _scan8_fwd(a, u):
    row = lax.broadcasted_iota(jnp.int32, a.shape, 0)
    for d in (1, 2, 4):
        a_s = pltpu.roll(a, d, axis=0)
        u_s = pltpu.roll(u, d, axis=0)
        m = row >= d
        u = jnp.where(m, a * u_s + u, u)
        a = jnp.where(m, a * a_s, a)
    return a, u


def _scan8_bwd(b, u):
    row = lax.broadcasted_iota(jnp.int32, b.shape, 0)
    for d in (1, 2, 4):
        b_s = pltpu.roll(b, SUBLANES - d, axis=0)
        u_s = pltpu.roll(u, SUBLANES - d, axis=0)
        m = row < SUBLANES - d
        u = jnp.where(m, b * u_s + u, u)
        b = jnp.where(m, b * b_s, b)
    return b, u


def _rglru_fwd(gate_br, x_br, cw, cb, wr, wi, br, bi, lam, name):
    s, c = x_br.shape
    nt = s // TIME_BLOCK
    tb, cbw = TIME_BLOCK, RG_BW
    groups = tb // SUBLANES

    def body(g_ref, x_ref, tail_ref, cw_ref, cb_ref, wr_ref, wi_ref, br_ref, bi_ref, lam_ref,
             y_ref, hs_ref, carry_ref, a_scr, u_scr):
        t = pl.program_id(1)

        @pl.when(t == 0)
        def _():
            carry_ref[...] = jnp.zeros_like(carry_ref)

        tail = jnp.where(t > 0, tail_ref[...], 0.0)
        ls = _log_sigmoid(lam_ref[...])
        _, xc, _, i, _, a, mult = _rg_gates(x_ref[...], tail, cw_ref, cb_ref[...], wr_ref[0], wi_ref[0],
                                            br_ref[...], bi_ref[...], ls)
        a_scr[...] = a
        u_scr[...] = mult * (i * xc)
        carry = carry_ref[...]
        for gi in range(groups):
            rows = pl.ds(gi * SUBLANES, SUBLANES)
            pa, hl = _scan8_fwd(a_scr[rows, :], u_scr[rows, :])
            hs_ref[rows, :] = hl + pa * carry
            carry = hs_ref[pl.ds(gi * SUBLANES + SUBLANES - 1, 1), :]
        carry_ref[...] = carry
        y_ref[...] = (hs_ref[...] * _gelu(g_ref[...])).astype(CD)

    blk = pl.BlockSpec((tb, cbw), lambda n, t: (t, n))
    tail = pl.BlockSpec((SUBLANES, cbw), lambda n, t: (jnp.maximum(t * groups - 1, 0), n))
    vec = pl.BlockSpec((1, cbw), lambda n, t: (0, n))
    wblk = pl.BlockSpec((1, cbw, cbw), lambda n, t: (n, 0, 0))
    return pl.pallas_call(
        body, name=name,
        out_shape=(jax.ShapeDtypeStruct((s, c), CD), jax.ShapeDtypeStruct((s, c), F32)),
        grid=(RG_BLOCKS, nt),
        in_specs=[blk, blk, tail, pl.BlockSpec((CONV_W, cbw), lambda n, t: (0, n)), vec, wblk, wblk, vec, vec, vec],
        out_specs=(blk, blk),
        scratch_shapes=[pltpu.VMEM((1, cbw), F32), pltpu.VMEM((tb, cbw), F32), pltpu.VMEM((tb, cbw), F32)],
        compiler_params=_params("parallel", "arbitrary"),
    )(gate_br, x_br, x_br, cw, cb, wr, wi, br, bi, lam)


def _rglru_bwd(dy, gate_br, x_br, hs, cw, cb, wr, wi, wrt, wit, br, bi, lam, name):
    s, c = x_br.shape
    nt = s // TIME_BLOCK
    tb, cbw = TIME_BLOCK, RG_BW
    groups = tb // SUBLANES

    def body(dy_ref, g_ref, x_ref, tail_ref, hs_ref, hprev_ref, cw_ref, cb_ref, wr_ref, wi_ref, wrt_ref, wit_ref,
             br_ref, bi_ref, lam_ref,
             dg_ref, dx_ref, dcw_ref, dcb_ref, dbr_ref, dbi_ref, dlam_ref, dwr_ref, dwi_ref,
             carry_ref, head_ref, b_scr, u_scr, dh_scr):
        tr = pl.program_id(1)
        first_block = tr == nt - 1

        @pl.when(tr == 0)
        def _():
            carry_ref[...] = jnp.zeros_like(carry_ref)
            head_ref[...] = jnp.zeros_like(head_ref)
            for ref in (dcw_ref, dcb_ref, dbr_ref, dbi_ref, dlam_ref, dwr_ref, dwi_ref):
                ref[...] = jnp.zeros_like(ref)

        tail = jnp.where(first_block, 0.0, tail_ref[...])
        lam_v = lam_ref[...]
        ls = _log_sigmoid(lam_v)
        taps, xc, r, i, log_a, a, mult = _rg_gates(x_ref[...], tail, cw_ref, cb_ref[...], wr_ref[0], wi_ref[0],
                                                   br_ref[...], bi_ref[...], ls)
        gate_v = g_ref[...]
        dyv = dy_ref[...]
        hsv = hs_ref[...]
        dg_ref[...] = (dyv * hsv * _gelu_grad(gate_v)).astype(CD)

        row = lax.broadcasted_iota(jnp.int32, a.shape, 0)
        b_scr[...] = jnp.where(row == tb - 1, 1.0, pltpu.roll(a, tb - 1, axis=0))
        u_scr[...] = dyv * _gelu(gate_v)
        carry = carry_ref[...]
        for gi in reversed(range(groups)):
            rows = pl.ds(gi * SUBLANES, SUBLANES)
            pb, gl = _scan8_bwd(b_scr[rows, :], u_scr[rows, :])
            dh_scr[rows, :] = gl + pb * carry
            carry = dh_scr[pl.ds(gi * SUBLANES, 1), :]
        dh = dh_scr[...]
        carry_ref[...] = carry * jnp.sum(jnp.where(row == 0, a, 0.0), axis=0, keepdims=True)

        hprev_tail = jnp.where(first_block, 0.0, hprev_ref[...])
        h_prev = _shift_down(hsv, hprev_tail, 1)
        da = dh * h_prev
        ixc = i * xc
        dmult = dh * ixc
        di = dh * mult * xc
        dxc = dh * mult * i
        a2 = a * a
        dlog_a = da * a - dmult * a2 / mult
        dpre_r = (dlog_a * (RG_C * ls)) * r * (1.0 - r)
        dpre_i = di * i * (1.0 - i)
        dlam_ref[...] += jnp.sum(dlog_a * r, axis=0, keepdims=True) * (RG_C * _sigmoid(-lam_v))
        dbr_ref[...] += jnp.sum(dpre_r, axis=0, keepdims=True)
        dbi_ref[...] += jnp.sum(dpre_i, axis=0, keepdims=True)
        xcd = xc.astype(CD)
        dprc = dpre_r.astype(CD)
        dpic = dpre_i.astype(CD)
        tn_dims = (((0,), (0,)), ((), ()))
        dwr_ref[0] += lax.dot_general(xcd, dprc, tn_dims, preferred_element_type=F32)
        dwi_ref[0] += lax.dot_general(xcd, dpic, tn_dims, preferred_element_type=F32)
        dxc = dxc + jnp.dot(dprc, wrt_ref[0], preferred_element_type=F32) + jnp.dot(dpic, wit_ref[0],
                                                                                    preferred_element_type=F32)
        dcb_ref[...] += jnp.sum(dxc, axis=0, keepdims=True)
        for k in range(CONV_W):
            dcw_ref[pl.ds(k, 1), :] += jnp.sum(dxc * taps[k], axis=0, keepdims=True)
        head = head_ref[...]
        dxb = jnp.zeros_like(dxc)
        for sft in range(CONV_W):
            dxb = dxb + cw_ref[pl.ds(CONV_W - 1 - sft, 1), :] * _shift_up(dxc, head, sft)
        dx_ref[...] = dxb.astype(CD)
        head_ref[...] = dxc[0:SUBLANES, :]

    blk = pl.BlockSpec((tb, cbw), lambda n, t: (nt - 1 - t, n))
    tail = pl.BlockSpec((SUBLANES, cbw), lambda n, t: (jnp.maximum((nt - 1 - t) * groups - 1, 0), n))
    vec = pl.BlockSpec((1, cbw), lambda n, t: (0, n))
    cwb = pl.BlockSpec((CONV_W, cbw), lambda n, t: (0, n))
    wblk = pl.BlockSpec((1, cbw, cbw), lambda n, t: (n, 0, 0))
    vshape = jax.ShapeDtypeStruct((1, c), F32)
    wshape = jax.ShapeDtypeStruct((RG_BLOCKS, cbw, cbw), F32)
    return pl.pallas_call(
        body, name=name,
        out_shape=(jax.ShapeDtypeStruct((s, c), CD), jax.ShapeDtypeStruct((s, c), CD),
                   jax.ShapeDtypeStruct((CONV_W, c), F32), vshape, vshape, vshape, vshape, wshape, wshape),
        grid=(RG_BLOCKS, nt),
        in_specs=[blk, blk, blk, tail, blk, tail, cwb, vec, wblk, wblk, wblk, wblk, vec, vec, vec],
        out_specs=(blk, blk, cwb, vec, vec, vec, vec, wblk, wblk),
        scratch_shapes=[pltpu.VMEM((1, cbw), F32), pltpu.VMEM((SUBLANES, cbw), F32),
                        pltpu.VMEM((tb, cbw), F32), pltpu.VMEM((tb, cbw), F32), pltpu.VMEM((tb, cbw), F32)],
        compiler_params=_params("parallel", "arbitrary"),
    )(dy, gate_br, x_br, x_br, hs, hs, cw, cb, wr, wi, wrt, wit, br, bi, lam)


ATT_BLOCK = 256
ATT_Q_BLOCK = 1024
ATT_RATIO = ATT_Q_BLOCK // ATT_BLOCK
ATT_SCALE = 1.0 / math.sqrt(SB_HEAD_DIM)
N_PAIRS = SB_HEADS * SB_HEAD_DIM // LANES
NT_DIMS = (((1,), (1,)), ((), ()))
TN_DIMS = (((0,), (0,)), ((), ()))


LOG2E = 1.4426950408889634


def _neg_abs(x):
    bits = lax.bitcast_convert_type(x, jnp.uint32) | jnp.uint32(0x80000000)
    return lax.bitcast_convert_type(bits, F32)


def _qk(qx, kb):
    return lax.dot_general(qx, kb, NT_DIMS, preferred_element_type=F32)


def _sb_logits(qk, valid):
    z2 = qk * (ATT_SCALE * LOG2E)
    lb2 = jnp.minimum(z2, 0.0) - jnp.log2(1.0 + jnp.exp2(_neg_abs(z2)))
    l2 = lb2 - z2
    if valid is not None:
        l2 = jnp.where(valid, l2, 0.0)
    return lb2, l2


def _hi_lo(x):
    hi = x.astype(CD)
    lo = (x - hi.astype(F32)).astype(CD)
    return jnp.concatenate([hi, lo], axis=1)


def _tri(strict, stacked):
    r = lax.broadcasted_iota(jnp.int32, (ATT_BLOCK, ATT_BLOCK), 0)
    c = lax.broadcasted_iota(jnp.int32, (ATT_BLOCK, ATT_BLOCK), 1)
    m = (r > c if strict else r >= c).astype(CD)
    return jnp.concatenate([m, m], axis=0) if stacked else m


def _attn_fwd(qkv, name):
    _, s, _ = qkv.shape
    tq, t = ATT_Q_BLOCK, ATT_BLOCK
    nblk = s // tq

    def body(q_ref, k_ref, v_ref, o_ref, qk_scr, w_scr):
        i = pl.program_id(1)
        lane = lax.broadcasted_iota(jnp.int32, (1, LANES), 1)
        head_masks = (lane < SB_HEAD_DIM, lane >= SB_HEAD_DIM)
        q = q_ref[0]
        qs = [jnp.where(m, q, jnp.zeros_like(q)) for m in head_masks]
        tri = _tri(True, False)
        rr = lax.broadcasted_iota(jnp.int32, (tq, t), 0)
        cc = lax.broadcasted_iota(jnp.int32, (tq, t), 1)

        def rows_of(j):
            return pl.ds(pl.multiple_of(j * t, t), t)

        def tail(x, row0):
            return x if row0 == 0 else x[row0:]

        def start_logits(j, row0=0):
            kb = k_ref[0, rows_of(j), :]
            for hd in range(2):
                qk_scr[hd, row0:, :] = _qk(tail(qs[hd], row0), kb)

        def weights(run, diagonal=False, row0=0):
            new_run = []
            valid = (cc < rr)[:tq - row0] if diagonal else None
            for hd in range(2):
                lb2, l2 = _sb_logits(qk_scr[hd, row0:, :], valid)
                w = jnp.exp2(lb2 + (tail(run[hd], row0) + jnp.dot(l2.astype(CD), tri, preferred_element_type=F32)))
                if valid is not None:
                    w = jnp.where(valid, w, 0.0)
                w_scr[row0:, hd * t:(hd + 1) * t] = w.astype(CD)
                rowsum = jnp.sum(l2, axis=1, keepdims=True)
                if row0:
                    rowsum = jnp.concatenate([jnp.zeros((row0, 1), F32), rowsum], axis=0)
                new_run.append(run[hd] + rowsum)
            return tuple(new_run)

        def apply_weights(j, row0=0):
            vb = v_ref[0, rows_of(j), :]
            vcat = jnp.concatenate([jnp.where(m, vb, jnp.zeros_like(vb)) for m in head_masks], axis=0)
            inc = jnp.dot(w_scr[row0:, :], vcat, preferred_element_type=F32)
            return inc if row0 == 0 else jnp.concatenate([jnp.zeros((row0, LANES), F32), inc], axis=0)

        zero = jnp.zeros((tq, 1), F32)
        last = ATT_RATIO - 1
        start_logits(ATT_RATIO * i + last, last * t)
        run = weights((zero, zero), True, last * t)
        oacc = jnp.zeros((tq, LANES), F32)
        for d in reversed(range(last)):
            start_logits(ATT_RATIO * i + d, d * t)
            oacc = oacc + apply_weights(ATT_RATIO * i + d + 1, (d + 1) * t)
            run = weights(run, True, d * t)
        start_logits(jnp.maximum(ATT_RATIO * i - 1, 0))

        def step(jj, carry):
            run, oacc = carry
            b = ATT_RATIO * i - 1 - jj
            oacc = oacc + apply_weights(b + 1)
            run = weights(run)
            start_logits(jnp.maximum(b - 1, 0))
            return run, oacc

        run, oacc = lax.fori_loop(0, ATT_RATIO * i, step, (run, oacc))
        o_ref[0] = oacc + apply_weights(0)

    return pl.pallas_call(
        body, name=name, out_shape=jax.ShapeDtypeStruct((N_PAIRS, s, LANES), F32), grid=(N_PAIRS, nblk),
        in_specs=[pl.BlockSpec((1, tq, LANES), lambda p, i: (p, i, 0)),
                  pl.BlockSpec((1, s, LANES), lambda p, i: (N_PAIRS + p, 0, 0)),
                  pl.BlockSpec((1, s, LANES), lambda p, i: (2 * N_PAIRS + p, 0, 0))],
        out_specs=pl.BlockSpec((1, tq, LANES), lambda p, i: (p, i, 0)),
        scratch_shapes=[pltpu.VMEM((2, tq, t), F32), pltpu.VMEM((tq, 2 * t), CD)],
        compiler_params=_params("parallel", "arbitrary"),
    )(qkv, qkv, qkv)


def _attn_bwd(qkv, o, do, name):
    _, s, _ = qkv.shape
    tq, t = ATT_Q_BLOCK, ATT_BLOCK
    nblk = s // tq

    def body(q_ref, k_ref, v_ref, o_ref, do_ref, dq_ref, dk_ref, dv_ref, qk_scr, dw_scr, w_scr, dz_scr):
        i = pl.program_id(1)

        @pl.when(i == 0)
        def _():
            dk_ref[...] = jnp.zeros_like(dk_ref)
            dv_ref[...] = jnp.zeros_like(dv_ref)

        lane = lax.broadcasted_iota(jnp.int32, (1, LANES), 1)
        head_masks = (lane < SB_HEAD_DIM, lane >= SB_HEAD_DIM)
        q = q_ref[0]
        dov = do_ref[0]
        ov = o_ref[0]
        qs = [jnp.where(m, q, jnp.zeros_like(q)) for m in head_masks]
        q_scaled_t = jnp.concatenate([(qx.astype(F32) * ATT_SCALE).T for qx in qs], axis=1).astype(CD)
        docs = [jnp.where(m, dov, 0.0).astype(CD) for m in head_masks]
        docat_t = jnp.concatenate([jnp.where(m, dov, 0.0).T for m in head_masks], axis=1).astype(CD)
        totals = [jnp.sum(d.astype(F32) * ov, axis=1, keepdims=True) for d in docs]
        tri = _tri(True, False)
        tri_incl = _tri(False, True)
        rr = lax.broadcasted_iota(jnp.int32, (tq, t), 0)
        cc = lax.broadcasted_iota(jnp.int32, (tq, t), 1)

        def rows_of(j):
            return pl.ds(pl.multiple_of(j * t, t), t)

        def tail(x, row0):
            return x if row0 == 0 else x[row0:]

        def pad_rows(x, row0):
            return x if row0 == 0 else jnp.concatenate([jnp.zeros((row0, x.shape[1]), x.dtype), x], axis=0)

        def start_products(j, row0=0):
            kb = k_ref[0, rows_of(j), :]
            vb = v_ref[0, rows_of(j), :]
            for hd in range(2):
                qk_scr[hd, row0:, :] = _qk(tail(qs[hd], row0), kb)
                dw_scr[hd, row0:, :] = lax.dot_general(tail(docs[hd], row0), vb, NT_DIMS, preferred_element_type=F32)

        def logit_grads(run, erun, diagonal=False, row0=0):
            new_run, new_erun = [], []
            valid = (cc < rr)[:tq - row0] if diagonal else None
            for hd in range(2):
                lb2, l2 = _sb_logits(qk_scr[hd, row0:, :], valid)
                w = jnp.exp2(lb2 + (tail(run[hd], row0) + jnp.dot(l2.astype(CD), tri, preferred_element_type=F32)))
                if valid is not None:
                    w = jnp.where(valid, w, 0.0)
                wc = w.astype(CD)
                w_scr[hd * tq + row0:(hd + 1) * tq, :] = wc
                e = dw_scr[hd, row0:, :] * wc.astype(F32)
                prefix = (tail(totals[hd] - erun[hd], row0)
                          - jnp.dot(_hi_lo(e), tri_incl, preferred_element_type=F32))
                dz = e - jnp.exp2(lb2) * (e + prefix)
                if valid is not None:
                    dz = jnp.where(valid, dz, 0.0)
                dz_scr[hd * tq + row0:(hd + 1) * tq, :] = dz.astype(CD)
                new_run.append(run[hd] + pad_rows(jnp.sum(l2, axis=1, keepdims=True), row0))
                new_erun.append(erun[hd] + pad_rows(jnp.sum(e, axis=1, keepdims=True), row0))
            return tuple(new_run), tuple(new_erun)

        def apply_grads(j, row0=0):
            rows = rows_of(j)
            kb = k_ref[0, rows, :]
            kcat = jnp.concatenate([jnp.where(m, kb, jnp.zeros_like(kb)) for m in head_masks], axis=0)
            dz_heads = [dz_scr[hd * tq + row0:(hd + 1) * tq, :] for hd in range(2)]
            w_heads = [w_scr[hd * tq + row0:(hd + 1) * tq, :] for hd in range(2)]
            q_t = jnp.concatenate([q_scaled_t[:, hd * tq + row0:(hd + 1) * tq] for hd in range(2)], axis=1)
            do_t = jnp.concatenate([docat_t[:, hd * tq + row0:(hd + 1) * tq] for hd in range(2)], axis=1)
            dk_ref[0, :, rows] += jnp.dot(q_t, jnp.concatenate(dz_heads, axis=0), preferred_element_type=F32)
            dv_ref[0, :, rows] += jnp.dot(do_t, jnp.concatenate(w_heads, axis=0), preferred_element_type=F32)
            return pad_rows(jnp.dot(jnp.concatenate(dz_heads, axis=1), kcat, preferred_element_type=F32), row0)

        zero = jnp.zeros((tq, 1), F32)
        last = ATT_RATIO - 1
        start_products(ATT_RATIO * i + last, last * t)
        run, erun = logit_grads((zero, zero), (zero, zero), True, last * t)
        dqacc = jnp.zeros((tq, LANES), F32)
        for d in reversed(range(last)):
            start_products(ATT_RATIO * i + d, d * t)
            dqacc = dqacc + apply_grads(ATT_RATIO * i + d + 1, (d + 1) * t)
            run, erun = logit_grads(run, erun, True, d * t)
        start_products(jnp.maximum(ATT_RATIO * i - 1, 0))

        def step(jj, carry):
            run, erun, dqacc = carry
            b = ATT_RATIO * i - 1 - jj
            dqacc = dqacc + apply_grads(b + 1)
            run, erun = logit_grads(run, erun)
            start_products(jnp.maximum(b - 1, 0))
            return run, erun, dqacc

        run, erun, dqacc = lax.fori_loop(0, ATT_RATIO * i, step, (run, erun, dqacc))
        dq_ref[0] = (dqacc + apply_grads(0)) * ATT_SCALE

    qblk = pl.BlockSpec((1, tq, LANES), lambda p, i: (p, i, 0))
    full = pl.BlockSpec((1, LANES, s), lambda p, i: (p, 0, 0))
    shape = jax.ShapeDtypeStruct((N_PAIRS, s, LANES), F32)
    shape_t = jax.ShapeDtypeStruct((N_PAIRS, LANES, s), F32)
    dq, dk_t, dv_t = pl.pallas_call(
        body, name=name, out_shape=(shape, shape_t, shape_t), grid=(N_PAIRS, nblk),
        in_specs=[qblk,
                  pl.BlockSpec((1, s, LANES), lambda p, i: (N_PAIRS + p, 0, 0)),
                  pl.BlockSpec((1, s, LANES), lambda p, i: (2 * N_PAIRS + p, 0, 0)),
                  qblk, qblk],
        out_specs=(qblk, full, full),
        scratch_shapes=[pltpu.VMEM((2, tq, t), F32), pltpu.VMEM((2, tq, t), F32),
                        pltpu.VMEM((2 * tq, t), CD), pltpu.VMEM((2 * tq, t), CD)],
        compiler_params=_params("parallel", "arbitrary"),
    )(qkv, qkv, qkv, o, do)
    return dq, jnp.swapaxes(dk_t, 1, 2), jnp.swapaxes(dv_t, 1, 2)


def _adamw(w, g, m, v, name):
    shape = w.shape
    rows, cols = (shape[-2], shape[-1]) if len(shape) >= 2 else (1, shape[-1])
    lead = w.size // (rows * cols)
    tr = _pick(rows, (512, 256, 128, 64, 32, 16, 8))

    def body(w_ref, g_ref, m_ref, v_ref, d_ref, nm_ref, nv_ref):
        gv = g_ref[...]
        nm = ADAM_B1 * m_ref[...] + (1.0 - ADAM_B1) * gv
        nv = ADAM_B2 * v_ref[...] + (1.0 - ADAM_B2) * (gv * gv)
        m_hat = nm / (1.0 - ADAM_B1 ** ADAM_STEP)
        v_hat = nv / (1.0 - ADAM_B2 ** ADAM_STEP)
        d_ref[...] = -ADAM_LR * (m_hat / (jnp.sqrt(v_hat) + ADAM_EPS) + ADAM_WD * w_ref[...])
        nm_ref[...] = nm
        nv_ref[...] = nv

    blk = pl.BlockSpec((1, tr, cols), lambda l, i: (l, i, 0))
    out = jax.ShapeDtypeStruct((lead, rows, cols), F32)
    d, nm, nv = pl.pallas_call(
        body, name=name, out_shape=(out, out, out), grid=(lead, rows // tr),
        in_specs=[blk, blk, blk, blk], out_specs=(blk, blk, blk), compiler_params=_params("parallel", "parallel"),
    )(*[a.reshape(lead, rows, cols) for a in (w, g, m, v)])
    return d.reshape(shape), nm.reshape(shape), nv.reshape(shape)


HBM = pl.BlockSpec(memory_space=pltpu.HBM)


def _coords():
    return lax.axis_index("x"), lax.axis_index("y"), lax.axis_index("c")


def _other_chips(x, y):
    return [(1 - x, y), (x, 1 - y), (1 - x, 1 - y)]


def _allgather_chips(shard, name, collective_id=None):
    r, cols = shard.shape
    half = r // 2
    quarter = half // 2

    def body(src_ref, out_ref, send_sems, recv_sems):
        x, y, c = _coords()
        sibling = (x, y, 1 - c)
        nx, ny, diag = (1 - x, y), (x, 1 - y), (1 - x, 1 - y)

        def piece(chip, core, lo, n):
            return out_ref.at[2 * chip[0] + chip[1], pl.ds(core * half + lo, n), :]

        def copy(k, dst, to, src=None):
            return pltpu.make_async_remote_copy(
                src_ref=dst if src is None else src, dst_ref=dst,
                send_sem=send_sems.at[k], recv_sem=recv_sems.at[k], device_id=to, device_id_type=MESH)

        me = (x, y)
        mine = src_ref.at[pl.ds(c * half, half), :]
        direct = [copy(0, piece(me, c, 0, half), (*nx, c), src=mine), copy(1, piece(me, c, 0, half), (*ny, c), src=mine)]
        for cp in direct:
            cp.start()
        arrivals = [piece(nx, c, 0, half), piece(ny, c, 0, half), piece(diag, c, 0, quarter),
                    piece(diag, c, quarter, quarter)]
        onward = [copy(2, piece(nx, c, 0, quarter), (*ny, c)), copy(3, piece(ny, c, quarter, quarter), (*nx, c))]
        to_sibling = [copy(4 + k, dst, sibling) for k, dst in enumerate(arrivals)]
        for k, dst in enumerate(arrivals):
            copy(k, dst, (x, y, c)).wait_recv()
            if k < 2:
                onward[k].start()
            to_sibling[k].start()
        from_sibling = [piece(nx, 1 - c, 0, half), piece(ny, 1 - c, 0, half), piece(diag, 1 - c, 0, quarter),
                        piece(diag, 1 - c, quarter, quarter)]
        for k, dst in enumerate(from_sibling):
            copy(4 + k, dst, (x, y, c)).wait_recv()
        for cp in direct + onward + to_sibling:
            cp.wait_send()

    out_shape = jax.ShapeDtypeStruct((N_CHIPS, r, cols), shard.dtype)
    sems = (pltpu.SemaphoreType.DMA((8,)), pltpu.SemaphoreType.DMA((8,)))
    if collective_id is None:
        return pl.pallas_call(body, name=name, out_shape=out_shape, in_specs=[HBM], out_specs=HBM,
                              scratch_shapes=list(sems))(shard)
    shard_ref = jax.new_ref(shard, memory_space=pltpu.MemorySpace.HBM)
    gathered_ref = jax.empty_ref(out_shape, memory_space=pltpu.MemorySpace.HBM)

    @_sequencer(name, collective_id, sems)
    def launch(send_sems, recv_sems):
        x, y, c = _coords()
        _handshake([(1 - x, y, c), (x, 1 - y, c), (x, y, 1 - c)])
        body(shard_ref, gathered_ref, send_sems, recv_sems)

    launch()
    return gathered_ref[...]


def _exchange_sibling_halves(g, name):
    n, r, cols = g.shape
    half = r // 2

    def body(g_ref, out_ref, send_sem, recv_sem):
        x, y, c = _coords()
        cp = pltpu.make_async_remote_copy(
            src_ref=g_ref.at[:, pl.ds((1 - c) * half, half), :], dst_ref=out_ref,
            send_sem=send_sem, recv_sem=recv_sem, device_id=(x, y, 1 - c), device_id_type=MESH)
        cp.start()
        cp.wait()

    return pl.pallas_call(
        body, name=name, out_shape=jax.ShapeDtypeStruct((n, half, cols), g.dtype),
        in_specs=[HBM], out_specs=HBM,
        scratch_shapes=[pltpu.SemaphoreType.DMA, pltpu.SemaphoreType.DMA],
    )(g)


def _sequencer(name, collective_id, scratch_types):
    return pl.kernel(mesh=plsc.ScalarSubcoreMesh(axis_name="sequencer", num_cores=1), name=name,
                     scratch_types=scratch_types, compiler_params=pltpu.CompilerParams(collective_id=collective_id))


def _handshake(peers):
    barrier = pltpu.get_barrier_semaphore()
    for peer in peers:
        pl.semaphore_signal(barrier, inc=1, device_id=peer, device_id_type=MESH)
    pl.semaphore_wait(barrier, len(peers))


def _exchange_sibling_halves_async(g, name, collective_id):
    n, r, cols = g.shape
    half = r // 2
    g_ref = jax.new_ref(g, memory_space=pltpu.MemorySpace.HBM)
    out_ref = jax.empty_ref(jax.ShapeDtypeStruct((n, half, cols), g.dtype), memory_space=pltpu.MemorySpace.HBM)

    @_sequencer(name, collective_id, (pltpu.SemaphoreType.DMA, pltpu.SemaphoreType.DMA))
    def launch(send_sem, recv_sem):
        x, y, c = _coords()
        _handshake([(x, y, 1 - c)])
        cp = pltpu.make_async_remote_copy(
            src_ref=g_ref.at[:, pl.ds((1 - c) * half, half), :], dst_ref=out_ref,
            send_sem=send_sem, recv_sem=recv_sem, device_id=(x, y, 1 - c), device_id_type=MESH)
        cp.start()
        cp.wait()

    launch()
    return out_ref[...]


def _share_halves_async(v, name, collective_id):
    h = v.shape[0] // 2
    v_ref = jax.new_ref(v, memory_space=pltpu.MemorySpace.HBM)

    @_sequencer(name, collective_id, (pltpu.SemaphoreType.DMA, pltpu.SemaphoreType.DMA))
    def launch(send_sem, recv_sem):
        x, y, c = _coords()
        _handshake([(x, y, 1 - c)])
        cp = pltpu.make_async_remote_copy(
            src_ref=v_ref.at[pl.ds(c * h, h), :], dst_ref=v_ref.at[pl.ds(c * h, h), :],
            send_sem=send_sem, recv_sem=recv_sem, device_id=(x, y, 1 - c), device_id_type=MESH)
        cp.start()
        pltpu.make_async_remote_copy(
            src_ref=v_ref.at[pl.ds(c * h, h), :], dst_ref=v_ref.at[pl.ds((1 - c) * h, h), :],
            send_sem=send_sem, recv_sem=recv_sem, device_id=(x, y, 1 - c), device_id_type=MESH).wait_recv()
        cp.wait_send()

    launch()
    return v_ref[...]


def _scatter_to_chips_async(p, name, collective_id):
    p_ref = jax.new_ref(p, memory_space=pltpu.MemorySpace.HBM)
    out_ref = jax.empty_ref(jax.ShapeDtypeStruct(p.shape, p.dtype), memory_space=pltpu.MemorySpace.HBM)

    @_sequencer(name, collective_id, (pltpu.SemaphoreType.DMA((3,)), pltpu.SemaphoreType.DMA((3,))))
    def launch(send_sems, recv_sems):
        x, y, c = _coords()
        me = 2 * x + y
        _handshake([(px, py, c) for px, py in _other_chips(x, y)])
        sends = []
        for j, (px, py) in enumerate(_other_chips(x, y)):
            sends.append(pltpu.make_async_remote_copy(
                src_ref=p_ref.at[2 * px + py], dst_ref=out_ref.at[me],
                send_sem=send_sems.at[j], recv_sem=recv_sems.at[j], device_id=(px, py, c), device_id_type=MESH))
        for cp in sends:
            cp.start()
        for j, (px, py) in enumerate(_other_chips(x, y)):
            pltpu.make_async_remote_copy(
                src_ref=p_ref.at[me], dst_ref=out_ref.at[2 * px + py],
                send_sem=send_sems.at[j], recv_sem=recv_sems.at[j], device_id=(px, py, c),
                device_id_type=MESH).wait_recv()
        for cp in sends:
            cp.wait_send()

    launch()
    return out_ref[...]


def _share_halves(v, name):
    h = v.shape[0] // 2

    def body(v_ref, out_ref, send_sem, recv_sem):
        x, y, c = _coords()
        cp = pltpu.make_async_remote_copy(
            src_ref=v_ref.at[pl.ds(c * h, h), :], dst_ref=out_ref.at[pl.ds(c * h, h), :],
            send_sem=send_sem, recv_sem=recv_sem, device_id=(x, y, 1 - c), device_id_type=MESH)
        cp.start()
        pltpu.make_async_remote_copy(
            src_ref=v_ref.at[pl.ds(c * h, h), :], dst_ref=out_ref.at[pl.ds((1 - c) * h, h), :],
            send_sem=send_sem, recv_sem=recv_sem, device_id=(x, y, 1 - c), device_id_type=MESH).wait_recv()
        cp.wait_send()

    return pl.pallas_call(
        body, name=name, out_shape=jax.ShapeDtypeStruct(v.shape, v.dtype),
        in_specs=[HBM], out_specs=HBM, input_output_aliases={0: 0},
        scratch_shapes=[pltpu.SemaphoreType.DMA, pltpu.SemaphoreType.DMA],
    )(v)


def _allreduce_small(v, name):
    r, cols = v.shape

    def body(v_ref, out_ref, buf_ref, send_sems, recv_sems):
        x, y, c = _coords()
        me = 4 * x + 2 * y + c
        buf_ref[me] = v_ref[...]
        sends = []
        for k in range(1, N_DEV):
            px = 1 - x if k & 4 else x
            py = 1 - y if k & 2 else y
            pc = 1 - c if k & 1 else c
            sends.append(pltpu.make_async_remote_copy(
                src_ref=v_ref, dst_ref=buf_ref.at[me], send_sem=send_sems.at[k - 1], recv_sem=recv_sems.at[k - 1],
                device_id=(px, py, pc), device_id_type=MESH))
        for cp in sends:
            cp.start()
        for cp in sends:
            cp.wait()
        acc = buf_ref[0]
        for d in range(1, N_DEV):
            acc = acc + buf_ref[d]
        out_ref[...] = acc

    return pl.pallas_call(
        body, name=name, out_shape=jax.ShapeDtypeStruct((r, cols), F32),
        in_specs=[pl.BlockSpec(memory_space=pltpu.VMEM)], out_specs=pl.BlockSpec(memory_space=pltpu.VMEM),
        scratch_shapes=[pltpu.VMEM((N_DEV, r, cols), F32), pltpu.SemaphoreType.DMA((N_DEV - 1,)),
                        pltpu.SemaphoreType.DMA((N_DEV - 1,))],
    )(v)


def _add_sibling(g, from_sibling, core, name):
    n, h, cols = from_sibling.shape
    tr = _row_tile(h)
    steps = h // tr

    def body(core_ref, a_ref, b_ref, o_ref):
        o_ref[...] = (a_ref[...] + b_ref[...]).astype(o_ref.dtype)

    return pl.pallas_call(
        body, name=name, out_shape=jax.ShapeDtypeStruct(from_sibling.shape, jnp.bfloat16),
        grid_spec=pltpu.PrefetchScalarGridSpec(
            num_scalar_prefetch=1, grid=(n, steps),
            in_specs=[pl.BlockSpec((1, tr, cols), lambda s, i, core_ref: (s, core_ref[0] * steps + i, 0)),
                      pl.BlockSpec((1, tr, cols), lambda s, i, core_ref: (s, i, 0))],
            out_specs=pl.BlockSpec((1, tr, cols), lambda s, i, core_ref: (s, i, 0))),
        compiler_params=_params("parallel", "parallel"),
    )(core.reshape(1).astype(jnp.int32), g, from_sibling)


def _sum_slots(p, own, chip, core, name):
    n, r, cols = p.shape
    tr = _row_tile(r)
    steps = r // tr

    def body(core_ref, chip_ref, p_ref, own_ref, o_ref):
        parts = [jnp.where(chip_ref[0] == s, own_ref[0], p_ref[s]).astype(F32) for s in range(n)]
        o_ref[...] = ((parts[0] + parts[1]) + parts[2]) + parts[3]

    return pl.pallas_call(
        body, name=name, out_shape=jax.ShapeDtypeStruct((2 * r, cols), F32),
        grid_spec=pltpu.PrefetchScalarGridSpec(
            num_scalar_prefetch=2, grid=(steps,),
            in_specs=[pl.BlockSpec((n, tr, cols), lambda i, core_ref, chip_ref: (0, i, 0)),
                      pl.BlockSpec((1, tr, cols), lambda i, core_ref, chip_ref: (chip_ref[0], i, 0))],
            out_specs=pl.BlockSpec((tr, cols), lambda i, core_ref, chip_ref: (core_ref[0] * steps + i, 0))),
        compiler_params=_params("parallel"),
    )(core.reshape(1).astype(jnp.int32), chip.reshape(1).astype(jnp.int32), p, own)


PACK_COLS = 1024


def _pack_shards(parts):
    return jnp.concatenate([p.reshape(-1, PACK_COLS) for p in parts], axis=0)


def _unpack_shards(buf, shapes):
    out, row = [], 0
    for shp in shapes:
        nrows = math.prod(shp) // PACK_COLS
        out.append(buf[..., row:row + nrows, :].reshape(buf.shape[:-2] + tuple(shp)))
        row += nrows
    return out


def _local_step(x, target, w):
    t = lambda a: a.T
    g = {}
    w_in_g, w_in_x = w["a_w_in"][:, :D_RNN], w["a_w_in"][:, D_RNN:]
    h0, gate_br, x_br = _norm_and_project(x, w["norm_mix_g"][0], w_in_g, w_in_x, "rglru_in")
    y_a, hs = _rglru_fwd(gate_br, x_br, w["a_conv_w"], w["a_conv_b"], w["a_w_r"], w["a_w_i"], w["a_b_r"],
                         w["a_b_i"], w["a_lambda"], "rglru_fwd")
    x1, h1 = _matmul([(y_a, w["a_w_out"])], F32, "mm_a_out", addend=x, norm_gain=w["norm_ffn_g"][0])
    fg0, fu0, act0 = _ffn_up(h1, w["ffn_w_gate"][0], w["ffn_w_up"][0], "ffn0_up")
    x2, h2 = _matmul([(act0, w["ffn_w_down"][0])], F32, "mm_f0_down", addend=x1, norm_gain=w["norm_mix_g"][1])
    qkv = _matmul([(h2, w["b_w_qkv"])], CD, "mm_b_qkv", out_lbm=True, tn=1024)
    o = _attn_fwd(qkv, "attn_fwd")
    x3, h3 = _matmul([(o, w["b_w_out"])], F32, "mm_b_out", a_lbm=True, addend=x2, norm_gain=w["norm_ffn_g"][1])
    fg1, fu1, act1 = _ffn_up(h3, w["ffn_w_gate"][1], w["ffn_w_up"][1], "ffn1_up")
    dx4, dx4c, g["final_g"], loss = _matmul([(act1, w["ffn_w_down"][1])], F32, "mm_f1_down", addend=x3,
                                            loss_head=(w["final_g"], target))

    def ffn_bwd(dx_out, dxc, h, x_in, fg, fu, act, layer, tag):
        dg, du = _ffn_dact(dxc, t(w["ffn_w_down"][layer]), fg, fu, "ffn_" + tag + "_dact")
        dwd = _matmul([(act, dxc)], F32, "mm_" + tag + "_dwd", trans_a=True)
        dwg = _matmul([(h, dg)], F32, "mm_" + tag + "_dwg", trans_a=True)
        dwu = _matmul([(h, du)], F32, "mm_" + tag + "_dwu", trans_a=True)
        dx_in, dx_in_c, dgain = _matmul([(dg, t(w["ffn_w_gate"][layer])), (du, t(w["ffn_w_up"][layer]))], F32,
                                        "mm_" + tag + "_dh", norm_bwd=(x_in, w["norm_ffn_g"][layer], dx_out))
        return dx_in, dx_in_c, dgain, dwg, dwu, dwd

    dx3, dx3c, dgf1, dwg1, dwu1, dwd1 = ffn_bwd(dx4, dx4c, h3, x3, fg1, fu1, act1, 1, "f1")
    do = _matmul([(dx3c, t(w["b_w_out"]))], F32, "mm_b_do", out_lbm=True, tn=1024)
    g["b_w_out"] = _matmul([(o, dx3c)], F32, "mm_b_dwout", trans_a=True, a_lbm=True)
    dq, dk, dv = _attn_bwd(qkv, o, do, "attn_bwd")
    wq_t = t(w["b_w_qkv"])
    parts = (dq, dk, dv)
    g["b_w_qkv"] = jnp.concatenate(
        [_matmul([(h2, p)], F32, "mm_b_dwqkv%d" % n, trans_a=True, b_lbm=True) for n, p in enumerate(parts)], axis=1)
    dx2, dx2c, dgm1 = _matmul([(p, wq_t[n * D_MODEL:(n + 1) * D_MODEL]) for n, p in enumerate(parts)], F32, "mm_b_dh",
                              a_lbm=True, norm_bwd=(x2, w["norm_mix_g"][1], dx3))
    dx1, dx1c, dgf0, dwg0, dwu0, dwd0 = ffn_bwd(dx2, dx2c, h1, x1, fg0, fu0, act0, 0, "f0")
    dy_a = _matmul([(dx1c, t(w["a_w_out"]))], F32, "mm_a_dy")
    g["a_w_out"] = _matmul([(y_a, dx1c)], F32, "mm_a_dwout", trans_a=True)
    wrt = jnp.swapaxes(w["a_w_r"], 1, 2)
    wit = jnp.swapaxes(w["a_w_i"], 1, 2)
    (dgate, dxbr, g["a_conv_w"], g["a_conv_b"], g["a_b_r"], g["a_b_i"], g["a_lambda"], g["a_w_r"],
     g["a_w_i"]) = _rglru_bwd(dy_a, gate_br, x_br, hs, w["a_conv_w"], w["a_conv_b"], w["a_w_r"], w["a_w_i"], wrt, wit,
                              w["a_b_r"], w["a_b_i"], w["a_lambda"], "rglru_bwd")
    g["a_w_in"] = jnp.concatenate([_matmul([(h0, dgate)], F32, "mm_a_dwin_g", trans_a=True),
                                   _matmul([(h0, dxbr)], F32, "mm_a_dwin_x", trans_a=True)], axis=1)
    dx0, _, dgm0 = _matmul([(dgate, t(w_in_g)), (dxbr, t(w_in_x))], F32, "mm_a_dh",
                           norm_bwd=(x, w["norm_mix_g"][0], dx1))
    g["norm_mix_g"] = jnp.concatenate([dgm0, dgm1], axis=0)
    g["norm_ffn_g"] = jnp.concatenate([dgf0, dgf1], axis=0)
    g["ffn_w_gate"] = [dwg0, dwg1]
    g["ffn_w_up"] = [dwu0, dwu1]
    g["ffn_w_down"] = [dwd0, dwd1]
    return loss, dx0, g


WEIGHTS = ["norm_mix_g", "norm_ffn_g", "a_w_in", "a_conv_w", "a_conv_b", "a_w_r", "a_b_r", "a_w_i", "a_b_i",
           "a_lambda", "a_w_out", "b_w_qkv", "b_w_out", "ffn_w_gate", "ffn_w_up", "ffn_w_down", "final_g"]
BIG = [("a_w_in", 2), ("a_w_r", 2), ("a_w_i", 2), ("a_w_out", 1), ("b_w_qkv", 2), ("b_w_out", 1),
       ("ffn_w_gate", 2), ("ffn_w_up", 2), ("ffn_w_down", 1)]
LAYER1 = ["b_w_qkv", "b_w_out", "ffn_w_gate", "ffn_w_up", "ffn_w_down"]
LAYER0 = ["a_w_in", "a_w_r", "a_w_i", "a_w_out", "ffn_w_gate", "ffn_w_up", "ffn_w_down"]
RS_COLLECTIVE_IDS = {"chips1": 3, "chips0": 4, "sibling1": 5, "share1": 6}
GATHER_COLLECTIVE_IDS = (8, 7)
SMALL = ["norm_mix_g", "norm_ffn_g", "a_conv_w", "a_conv_b", "a_b_r", "a_b_i", "a_lambda", "final_g"]


def _split_chips(full, axis):
    if axis == 1:
        return full.reshape((N_CHIPS, 1, full.shape[1] // N_CHIPS) + full.shape[2:])
    return jnp.stack(jnp.split(full, N_CHIPS, axis=axis))


def _step(x, target, weights, moments_m, moments_v):
    chip = 2 * lax.axis_index("x") + lax.axis_index("y")
    core = lax.axis_index("c")
    axis_of = dict(BIG)
    full = {}
    for group, layer, tag, collective_id in ((LAYER0[:4], 0, "0a", None), (LAYER0[4:], 0, "0f", GATHER_COLLECTIVE_IDS[0]),
                                             (LAYER1, 1, "1", GATHER_COLLECTIVE_IDS[1])):
        shards = [weights[n][layer % weights[n].shape[0]].astype(CD) for n in group]
        gathered = _allgather_chips(_pack_shards(shards), "allgather_weights" + tag, collective_id)
        for n, own, stack in zip(group, shards, _unpack_shards(gathered, [sh.shape for sh in shards])):
            joined = jnp.concatenate([jnp.where(chip == s, own, stack[s]) for s in range(N_CHIPS)],
                                     axis=axis_of[n] - 1)
            full.setdefault(n, {})[layer] = joined
    full = {n: (v[0] if n.startswith("a_") else v[1] if n.startswith("b_") else [v[0], v[1]]) for n, v in full.items()}
    cw_rows = jnp.zeros((N_CHIPS, CONV_W, RG_BW), F32)
    cw_rows = lax.dynamic_update_slice(cw_rows, jnp.where(core == 0, weights["a_conv_w"], 0.0), (chip, 0, 0))
    cw_all = _allreduce_small(cw_rows.reshape(-1, LANES), "allgather_conv_w").reshape(N_CHIPS, CONV_W, RG_BW)
    full["a_conv_w"] = jnp.concatenate([cw_all[s] for s in range(N_CHIPS)], axis=1)
    for n in ("norm_mix_g", "norm_ffn_g", "final_g"):
        full[n] = weights[n]
    for n in ("a_conv_b", "a_b_r", "a_b_i", "a_lambda"):
        full[n] = weights[n]
    loss, dx, grads = _local_step(x[0], target[0], full)
    small_parts = [grads[n].reshape(-1) for n in SMALL] + [loss.reshape(-1)]
    sizes = [p.shape[0] for p in small_parts]
    small = _allreduce_small(jnp.concatenate(small_parts).reshape(-1, LANES), "allreduce_small").reshape(-1)
    red, pos = {}, 0
    for n, sz in zip(SMALL + ["loss"], sizes):
        red[n] = small[pos:pos + sz]
        pos += sz
    loss_out = red["loss"][0]
    g_out = {}
    for n in SMALL:
        if n == "a_conv_w":
            g_out[n] = lax.dynamic_slice(red[n].reshape(CONV_W, D_RNN), (0, chip * RG_BW), (CONV_W, RG_BW)).reshape(
                weights[n].shape)
        else:
            g_out[n] = red[n].reshape(weights[n].shape)
    axis_of = dict(BIG)
    pieces = {}
    for group, layer, tag in ((LAYER1, 1, "1"), (LAYER0, 0, "0")):
        stacks, shapes = [], []
        for n in group:
            per_layer = isinstance(grads[n], list)
            gfull = grads[n][layer] if per_layer else grads[n]
            shard_shape = weights[n].shape[1:]
            gfull = gfull.reshape((1,) + gfull.shape)
            stacks.append(_split_chips(gfull, axis_of[n]).reshape(N_CHIPS, -1, PACK_COLS))
            shapes.append((1,) + tuple(shard_shape))
        gbuf = jnp.concatenate(stacks, axis=1)
        if layer == 1:
            from_sibling = _exchange_sibling_halves_async(gbuf, "rs_sibling" + tag, RS_COLLECTIVE_IDS["sibling1"])
        else:
            from_sibling = _exchange_sibling_halves(gbuf, "rs_sibling" + tag)
        chip_partial = _add_sibling(gbuf, from_sibling, core, "rs_add" + tag)
        from_chips = _scatter_to_chips_async(chip_partial, "rs_chips" + tag, RS_COLLECTIVE_IDS["chips" + tag])
        halves = _sum_slots(from_chips, chip_partial, chip, core, "rs_sum" + tag)
        if layer == 1:
            reduced = _share_halves_async(halves, "rs_share" + tag, RS_COLLECTIVE_IDS["share1"])
        else:
            reduced = _share_halves(halves, "rs_share" + tag)
        for n, piece in zip(group, _unpack_shards(reduced, shapes)):
            pieces.setdefault(n, {})[layer] = piece
    for n, _ in BIG:
        layers = pieces[n]
        g_out[n] = jnp.concatenate([layers[k] for k in sorted(layers)], axis=0)
    outs_g, outs_d, outs_m, outs_v = [], [], [], []
    for n in WEIGHTS:
        d, nm, nv = _adamw(weights[n], g_out[n], moments_m[n], moments_v[n], "adamw_" + n)
        outs_g.append(g_out[n])
        outs_d.append(d)
        outs_m.append(nm)
        outs_v.append(nv)
    return (loss_out, dx[None], *outs_g, *outs_d, *outs_m, *outs_v)


def kernel(x, norm_mix_g, norm_ffn_g, a_w_in, a_conv_w, a_conv_b, a_w_r, a_b_r, a_w_i, a_b_i, a_lambda, a_w_out, b_w_qkv, b_w_out, ffn_w_gate, ffn_w_up, ffn_w_down, final_g, loss_target, m_norm_mix_g, m_norm_ffn_g, m_a_w_in, m_a_conv_w, m_a_conv_b, m_a_w_r, m_a_b_r, m_a_w_i, m_a_b_i, m_a_lambda, m_a_w_out, m_b_w_qkv, m_b_w_out, m_ffn_w_gate, m_ffn_w_up, m_ffn_w_down, m_final_g, v_norm_mix_g, v_norm_ffn_g, v_a_w_in, v_a_conv_w, v_a_conv_b, v_a_w_r, v_a_b_r, v_a_w_i, v_a_b_i, v_a_lambda, v_a_w_out, v_b_w_qkv, v_b_w_out, v_ffn_w_gate, v_ffn_w_up, v_ffn_w_down, v_final_g):
    ws = [norm_mix_g, norm_ffn_g, a_w_in, a_conv_w, a_conv_b, a_w_r, a_b_r, a_w_i, a_b_i, a_lambda, a_w_out, b_w_qkv,
          b_w_out, ffn_w_gate, ffn_w_up, ffn_w_down, final_g]
    ms = [m_norm_mix_g, m_norm_ffn_g, m_a_w_in, m_a_conv_w, m_a_conv_b, m_a_w_r, m_a_b_r, m_a_w_i, m_a_b_i, m_a_lambda,
          m_a_w_out, m_b_w_qkv, m_b_w_out, m_ffn_w_gate, m_ffn_w_up, m_ffn_w_down, m_final_g]
    vs = [v_norm_mix_g, v_norm_ffn_g, v_a_w_in, v_a_conv_w, v_a_conv_b, v_a_w_r, v_a_b_r, v_a_w_i, v_a_b_i, v_a_lambda,
          v_a_w_out, v_b_w_qkv, v_b_w_out, v_ffn_w_gate, v_ffn_w_up, v_ffn_w_down, v_final_g]
    return _step(x, loss_target, dict(zip(WEIGHTS, ws)), dict(zip(WEIGHTS, ms)), dict(zip(WEIGHTS, vs)))
```

```python
import functools
import math

import jax
import jax.numpy as jnp
from jax import lax
from jax.experimental import pallas as pl
from jax.experimental.pallas import tpu as pltpu
from jax.experimental.pallas import tpu_sc as plsc

F32 = jnp.float32
CD = jnp.bfloat16

D_MODEL = 1024
D_RNN = 1024
RG_BLOCKS = 4
RG_BW = 256
CONV_W = 4
RG_C = 8.0
SB_HEADS = 16
SB_HEAD_DIM = 64
D_FF = 2816
RMS_EPS = 1e-6
N_CHIPS = 4
N_DEV = 8

ADAM_LR = 0.001
ADAM_B1 = 0.9
ADAM_B2 = 0.999
ADAM_EPS = 1e-08
ADAM_WD = 0.01
ADAM_STEP = 10

LANES = 128
VMEM_LIMIT = 56 * 1024 * 1024
MESH = pl.DeviceIdType.MESH


def _params(*sem):
    return pltpu.CompilerParams(dimension_semantics=sem, vmem_limit_bytes=VMEM_LIMIT)


def _pick(n, prefs):
    for p in prefs:
        if n % p == 0:
            return p
    return n


def _row_tile(rows):
    return max(d for d in range(16, 1025, 16) if rows % d == 0)


def _matmul(pairs, out_dtype, name, *, trans_a=False, a_lbm=False, b_lbm=False, out_lbm=False, addend=None,
            tm=512, tn=None, tk=None, norm_gain=None, norm_bwd=None, loss_head=None):
    a0, b0 = pairs[0]
    if trans_a:
        kdim = a0.shape[1] if a_lbm else a0.shape[0]
        m = a0.shape[0] * LANES if a_lbm else a0.shape[1]
    else:
        m = a0.shape[1] if a_lbm else a0.shape[0]
        kdim = a0.shape[0] * LANES if a_lbm else a0.shape[1]
    n = b0.shape[0] * LANES if b_lbm else b0.shape[1]
    tm = _pick(m, (tm, 1408, 256, 128))
    tn = tn or _pick(n, (1408, 1024, 768, 512, 256, 128))
    tk = tk or _pick(kdim, (1024, 1408, 512, 256, 128))
    nk = kdim // tk
    npair = len(pairs)

    def cat(ref):
        return jnp.concatenate([ref[p] for p in range(ref.shape[0])], axis=-1)

    def body(*refs):
        ins = refs[: 2 * npair]
        pos = 2 * npair
        add_ref = None
        if addend is not None:
            add_ref = refs[pos]
            pos += 1
        gain_ref = x_ref = dxin_ref = None
        if norm_gain is not None:
            gain_ref = refs[pos]
            pos += 1
        if norm_bwd is not None:
            x_ref, gain_ref, dxin_ref = refs[pos:pos + 3]
            pos += 3
        if loss_head is not None:
            gain_ref, target_ref = refs[pos:pos + 2]
            pos += 2
        o_ref = refs[pos]
        extra_out = refs[pos + 1:-1]
        acc_ref = refs[-1]
        k = pl.program_id(2)

        @pl.when(k == 0)
        def _():
            acc_ref[...] = jnp.zeros_like(acc_ref)

        if norm_bwd is not None or loss_head is not None:
            @pl.when((k == 0) & (pl.program_id(0) == 0))
            def _():
                for ref in extra_out[1:]:
                    ref[...] = jnp.zeros_like(ref)

        acc = acc_ref[...]
        for p in range(npair):
            a = (cat(ins[2 * p]) if a_lbm else ins[2 * p][...]).astype(CD)
            b = (cat(ins[2 * p + 1]) if b_lbm else ins[2 * p + 1][...]).astype(CD)
            dims = (((0,), (0,)), ((), ())) if trans_a else (((1,), (0,)), ((), ()))
            acc = acc + lax.dot_general(a, b, dims, preferred_element_type=F32)
        acc_ref[...] = acc

        @pl.when(k == nk - 1)
        def _():
            res = acc_ref[...]
            if add_ref is not None:
                res = res + add_ref[...]
            if norm_gain is not None:
                rinv = lax.rsqrt(jnp.mean(res * res, axis=-1, keepdims=True) + RMS_EPS)
                extra_out[0][...] = (res * rinv * gain_ref[...]).astype(CD)
            if norm_bwd is not None:
                xv = x_ref[...]
                rinv = lax.rsqrt(jnp.mean(xv * xv, axis=-1, keepdims=True) + RMS_EPS)
                nrm = xv * rinv
                dn = res * gain_ref[...]
                extra_out[1][...] += jnp.sum(res * nrm, axis=0, keepdims=True)
                res = dxin_ref[...] + rinv * (dn - nrm * jnp.mean(dn * nrm, axis=-1, keepdims=True))
                extra_out[0][...] = res.astype(CD)
            if loss_head is not None:
                gv = gain_ref[...]
                rinv = lax.rsqrt(jnp.mean(res * res, axis=-1, keepdims=True) + RMS_EPS)
                nrm = res * rinv
                err = nrm * gv - target_ref[...]
                extra_out[2][...] += 0.5 * jnp.sum(jnp.mean(err * err, axis=-1, keepdims=True), axis=0, keepdims=True)
                dy = err * (1.0 / n)
                dn = dy * gv
                extra_out[1][...] += jnp.sum(dy * nrm, axis=0, keepdims=True)
                res = rinv * (dn - nrm * jnp.mean(dn * nrm, axis=-1, keepdims=True))
                extra_out[0][...] = res.astype(CD)
            res = res.astype(out_dtype)
            if out_lbm:
                for p in range(tn // LANES):
                    o_ref[p] = res[:, p * LANES:(p + 1) * LANES]
            else:
                o_ref[...] = res

    if trans_a:
        a_spec = (pl.BlockSpec((tm // LANES, tk, LANES), lambda i, j, k: (i, k, 0)) if a_lbm
                  else pl.BlockSpec((tk, tm), lambda i, j, k: (k, i)))
    else:
        a_spec = (pl.BlockSpec((tk // LANES, tm, LANES), lambda i, j, k: (k, i, 0)) if a_lbm
                  else pl.BlockSpec((tm, tk), lambda i, j, k: (i, k)))
    b_spec = (pl.BlockSpec((tn // LANES, tk, LANES), lambda i, j, k: (j, k, 0)) if b_lbm
              else pl.BlockSpec((tk, tn), lambda i, j, k: (k, j)))
    in_specs = []
    args = []
    for a, b in pairs:
        in_specs += [a_spec, b_spec]
        args += [a, b]
    if addend is not None:
        in_specs.append(pl.BlockSpec((tm, tn), lambda i, j, k: (i, j)))
        args.append(addend)
    tile = pl.BlockSpec((tm, tn), lambda i, j, k: (i, j))
    vec = pl.BlockSpec((1, tn), lambda i, j, k: (0, j))
    if out_lbm:
        out_shape = jax.ShapeDtypeStruct((n // LANES, m, LANES), out_dtype)
        out_spec = pl.BlockSpec((tn // LANES, tm, LANES), lambda i, j, k: (j, i, 0))
    else:
        out_shape = jax.ShapeDtypeStruct((m, n), out_dtype)
        out_spec = tile
    sem = ("parallel", "parallel", "arbitrary")
    if norm_gain is not None or norm_bwd is not None or loss_head is not None:
        assert tn == n and not out_lbm, "the norm needs whole rows in one tile"
        out_shape, out_spec = [out_shape, jax.ShapeDtypeStruct((m, n), CD)], [out_spec, tile]
    if norm_gain is not None:
        in_specs.append(vec)
        args.append(norm_gain.reshape(1, n))
    if norm_bwd is not None:
        x_in, gain, dx_in = norm_bwd
        in_specs += [tile, vec, tile]
        args += [x_in, gain.reshape(1, n), dx_in]
        out_shape.append(jax.ShapeDtypeStruct((1, n), F32))
        out_spec.append(vec)
        sem = ("arbitrary", "arbitrary", "arbitrary")
    if loss_head is not None:
        gain, target = loss_head
        in_specs += [vec, tile]
        args += [gain.reshape(1, n), target]
        out_shape += [jax.ShapeDtypeStruct((1, n), F32), jax.ShapeDtypeStruct((1, LANES), F32)]
        out_spec += [vec, pl.BlockSpec((1, LANES), lambda i, j, k: (0, 0))]
        sem = ("arbitrary", "arbitrary", "arbitrary")
    return pl.pallas_call(
        body, name=name, out_shape=out_shape, grid=(m // tm, n // tn, nk),
        in_specs=in_specs, out_specs=out_spec,
        scratch_shapes=[pltpu.VMEM((tm, tn), F32)],
        compiler_params=_params(*sem),
    )(*args)


def _norm_and_project(x, g, w_a, w_b, name):
    s, d = x.shape
    n = w_a.shape[1]
    tm = _pick(s, (512, 256))

    def body(x_ref, g_ref, wa_ref, wb_ref, h_ref, a_ref, b_ref):
        xv = x_ref[...]
        rinv = lax.rsqrt(jnp.mean(xv * xv, axis=-1, keepdims=True) + RMS_EPS)
        h = (xv * rinv * g_ref[...]).astype(CD)
        h_ref[...] = h
        a_ref[...] = jnp.dot(h, wa_ref[...], preferred_element_type=F32)
        b_ref[...] = jnp.dot(h, wb_ref[...], preferred_element_type=F32)

    row = pl.BlockSpec((tm, d), lambda i: (i, 0))
    out = pl.BlockSpec((tm, n), lambda i: (i, 0))
    wspec = pl.BlockSpec((d, n), lambda i: (0, 0))
    return pl.pallas_call(
        body, name=name,
        out_shape=(jax.ShapeDtypeStruct((s, d), CD), jax.ShapeDtypeStruct((s, n), F32),
                   jax.ShapeDtypeStruct((s, n), F32)),
        grid=(s // tm,), in_specs=[row, pl.BlockSpec((1, d), lambda i: (0, 0)), wspec, wspec],
        out_specs=(row, out, out), compiler_params=_params("parallel"),
    )(x, g.reshape(1, d), w_a, w_b)


def _sigmoid(z):
    return 1.0 / (1.0 + jnp.exp(-z))


FFN_TM = 512
FFN_TN = 1408


def _ffn_up(h, wg, wu, name):
    s, d = h.shape
    f = wg.shape[1]
    tm = _pick(s, (FFN_TM, 256))

    def body(h_ref, wg_ref, wu_ref, g_ref, u_ref, a_ref):
        hv = h_ref[...]
        gv = jnp.dot(hv, wg_ref[...], preferred_element_type=F32)
        uv = jnp.dot(hv, wu_ref[...], preferred_element_type=F32)
        g_ref[...] = gv
        u_ref[...] = uv
        a_ref[...] = (gv * _sigmoid(gv) * uv).astype(CD)

    a_spec = pl.BlockSpec((tm, d), lambda i, j: (i, 0))
    w_spec = pl.BlockSpec((d, FFN_TN), lambda i, j: (0, j))
    o_spec = pl.BlockSpec((tm, FFN_TN), lambda i, j: (i, j))
    return pl.pallas_call(
        body, name=name,
        out_shape=(jax.ShapeDtypeStruct((s, f), F32), jax.ShapeDtypeStruct((s, f), F32),
                   jax.ShapeDtypeStruct((s, f), CD)),
        grid=(s // tm, f // FFN_TN), in_specs=[a_spec, w_spec, w_spec], out_specs=(o_spec, o_spec, o_spec),
        compiler_params=_params("parallel", "parallel"),
    )(h, wg, wu)


def _ffn_dact(dxc, wd_t, g, u, name):
    s, d = dxc.shape
    f = wd_t.shape[1]
    tm = _pick(s, (FFN_TM, 256))

    def body(dx_ref, w_ref, g_ref, u_ref, dg_ref, du_ref):
        da = jnp.dot(dx_ref[...], w_ref[...], preferred_element_type=F32)
        gv = g_ref[...]
        sg = _sigmoid(gv)
        silu = gv * sg
        dg_ref[...] = (da * u_ref[...] * (sg + silu * (1.0 - sg))).astype(CD)
        du_ref[...] = (da * silu).astype(CD)

    a_spec = pl.BlockSpec((tm, d), lambda i, j: (i, 0))
    w_spec = pl.BlockSpec((d, FFN_TN), lambda i, j: (0, j))
    o_spec = pl.BlockSpec((tm, FFN_TN), lambda i, j: (i, j))
    return pl.pallas_call(
        body, name=name,
        out_shape=(jax.ShapeDtypeStruct((s, f), CD), jax.ShapeDtypeStruct((s, f), CD)),
        grid=(s // tm, f // FFN_TN), in_specs=[a_spec, w_spec, o_spec, o_spec], out_specs=(o_spec, o_spec),
        compiler_params=_params("parallel", "parallel"),
    )(dxc, wd_t, g, u)


TIME_BLOCK = 1024
SUBLANES = 8
GELU_C = math.sqrt(2.0 / math.pi)
GELU_A = 0.044715


def _gelu(x):
    return 0.5 * x * (1.0 + jnp.tanh(GELU_C * (x + GELU_A * x * x * x)))


def _gelu_grad(x):
    t = jnp.tanh(GELU_C * (x + GELU_A * x * x * x))
    return 0.5 * (1.0 + t) + 0.5 * x * (1.0 - t * t) * GELU_C * (1.0 + 3.0 * GELU_A * x * x)


def _neg_expm1(x):
    series = -x * (1.0 + x * (0.5 + x * (1.0 / 6.0 + x * (1.0 / 24.0))))
    return jnp.where(x > -0.05, series, 1.0 - jnp.exp(x))


def _log_sigmoid(x):
    return jnp.minimum(x, 0.0) - jnp.log1p(jnp.exp(-jnp.abs(x)))


def _shift_down(x, tail, s):
    if s == 0:
        return x
    ext = jnp.concatenate([tail, x], axis=0)
    return pltpu.roll(ext, s, axis=0)[SUBLANES:]


def _shift_up(x, head, s):
    if s == 0:
        return x
    n = x.shape[0]
    ext = jnp.concatenate([x, head], axis=0)
    return pltpu.roll(ext, n + SUBLANES - s, axis=0)[:n]


def _rg_gates(xbr, tail, cw_ref, cb, wr, wi, br, bi, ls):
    taps = [_shift_down(xbr, tail, CONV_W - 1 - k) for k in range(CONV_W)]
    xc = cb
    for k in range(CONV_W):
        xc = xc + cw_ref[pl.ds(k, 1), :] * taps[k]
    xcd = xc.astype(CD)
    r = _sigmoid(jnp.dot(xcd, wr, preferred_element_type=F32) + br)
    i = _sigmoid(jnp.dot(xcd, wi, preferred_element_type=F32) + bi)
    log_a = RG_C * r * ls
    a = jnp.exp(log_a)
    mult = jnp.sqrt(jnp.maximum(_neg_expm1(2.0 * log_a), 0.0))
    return taps, xc, r, i, log_a, a, mult


def _scan8_fwd(a, u):
    row = lax.broadcasted_iota(jnp.int32, a.shape, 0)
    for d in (1, 2, 4):
        a_s = pltpu.roll(a, d, axis=0)
        u_s = pltpu.roll(u, d, axis=0)
        m = row >= d
        u = jnp.where(m, a * u_s + u, u)
        a = jnp.where(m, a * a_s, a)
    return a, u


def _scan8_bwd(b, u):
    row = lax.broadcasted_iota(jnp.int32, b.shape, 0)
    for d in (1, 2, 4):
        b_s = pltpu.roll(b, SUBLANES - d, axis=0)
        u_s = pltpu.roll(u, SUBLANES - d, axis=0)
        m = row < SUBLANES - d
        u = jnp.where(m, b * u_s + u, u)
        b = jnp.where(m, b * b_s, b)
    return b, u


def _rglru_fwd(gate_br, x_br, cw, cb, wr, wi, br, bi, lam, name):
    s, c = x_br.shape
    nt = s // TIME_BLOCK
    tb, cbw = TIME_BLOCK, RG_BW
    groups = tb // SUBLANES

    def body(g_ref, x_ref, tail_ref, cw_ref, cb_ref, wr_ref, wi_ref, br_ref, bi_ref, lam_ref,
             y_ref, hs_ref, carry_ref, a_scr, u_scr):
        t = pl.program_id(1)

        @pl.when(t == 0)
        def _():
            carry_ref[...] = jnp.zeros_like(carry_ref)

        tail = jnp.where(t > 0, tail_ref[...], 0.0)
        ls = _log_sigmoid(lam_ref[...])
        _, xc, _, i, _, a, mult = _rg_gates(x_ref[...], tail, cw_ref, cb_ref[...], wr_ref[0], wi_ref[0],
                                            br_ref[...], bi_ref[...], ls)
        a_scr[...] = a
        u_scr[...] = mult * (i * xc)
        carry = carry_ref[...]
        for gi in range(groups):
            rows = pl.ds(gi * SUBLANES, SUBLANES)
            pa, hl = _scan8_fwd(a_scr[rows, :], u_scr[rows, :])
            hs_ref[rows, :] = hl + pa * carry
            carry = hs_ref[pl.ds(gi * SUBLANES + SUBLANES - 1, 1), :]
        carry_ref[...] = carry
        y_ref[...] = (hs_ref[...] * _gelu(g_ref[...])).astype(CD)

    blk = pl.BlockSpec((tb, cbw), lambda n, t: (t, n))
    tail = pl.BlockSpec((SUBLANES, cbw), lambda n, t: (jnp.maximum(t * groups - 1, 0), n))
    vec = pl.BlockSpec((1, cbw), lambda n, t: (0, n))
    wblk = pl.BlockSpec((1, cbw, cbw), lambda n, t: (n, 0, 0))
    return pl.pallas_call(
        body, name=name,
        out_shape=(jax.ShapeDtypeStruct((s, c), CD), jax.ShapeDtypeStruct((s, c), F32)),
        grid=(RG_BLOCKS, nt),
        in_specs=[blk, blk, tail, pl.BlockSpec((CONV_W, cbw), lambda n, t: (0, n)), vec, wblk, wblk, vec, vec, vec],
        out_specs=(blk, blk),
        scratch_shapes=[pltpu.VMEM((1, cbw), F32), pltpu.VMEM((tb, cbw), F32), pltpu.VMEM((tb, cbw), F32)],
        compiler_params=_params("parallel", "arbitrary"),
    )(gate_br, x_br, x_br, cw, cb, wr, wi, br, bi, lam)


def _rglru_bwd(dy, gate_br, x_br, hs, cw, cb, wr, wi, wrt, wit, br, bi, lam, name):
    s, c = x_br.shape
    nt = s // TIME_BLOCK
    tb, cbw = TIME_BLOCK, RG_BW
    groups = tb // SUBLANES

    def body(dy_ref, g_ref, x_ref, tail_ref, hs_ref, hprev_ref, cw_ref, cb_ref, wr_ref, wi_ref, wrt_ref, wit_ref,
             br_ref, bi_ref, lam_ref,
             dg_ref, dx_ref, dcw_ref, dcb_ref, dbr_ref, dbi_ref, dlam_ref, dwr_ref, dwi_ref,
             carry_ref, head_ref, b_scr, u_scr, dh_scr):
        tr = pl.program_id(1)
        first_block = tr == nt - 1

        @pl.when(tr == 0)
        def _():
            carry_ref[...] = jnp.zeros_like(carry_ref)
            head_ref[...] = jnp.zeros_like(head_ref)
            for ref in (dcw_ref, dcb_ref, dbr_ref, dbi_ref, dlam_ref, dwr_ref, dwi_ref):
                ref[...] = jnp.zeros_like(ref)

        tail = jnp.where(first_block, 0.0, tail_ref[...])
        lam_v = lam_ref[...]
        ls = _log_sigmoid(lam_v)
        taps, xc, r, i, log_a, a, mult = _rg_gates(x_ref[...], tail, cw_ref, cb_ref[...], wr_ref[0], wi_ref[0],
                                                   br_ref[...], bi_ref[...], ls)
        gate_v = g_ref[...]
        dyv = dy_ref[...]
        hsv = hs_ref[...]
        dg_ref[...] = (dyv * hsv * _gelu_grad(gate_v)).astype(CD)

        row = lax.broadcasted_iota(jnp.int32, a.shape, 0)
        b_scr[...] = jnp.where(row == tb - 1, 1.0, pltpu.roll(a, tb - 1, axis=0))
        u_scr[...] = dyv * _gelu(gate_v)
        carry = carry_ref[...]
        for gi in reversed(range(groups)):
            rows = pl.ds(gi * SUBLANES, SUBLANES)
            pb, gl = _scan8_bwd(b_scr[rows, :], u_scr[rows, :])
            dh_scr[rows, :] = gl + pb * carry
            carry = dh_scr[pl.ds(gi * SUBLANES, 1), :]
        dh = dh_scr[...]
        carry_ref[...] = carry * jnp.sum(jnp.where(row == 0, a, 0.0), axis=0, keepdims=True)

        hprev_tail = jnp.where(first_block, 0.0, hprev_ref[...])
        h_prev = _shift_down(hsv, hprev_tail, 1)
        da = dh * h_prev
        ixc = i * xc
        dmult = dh * ixc
        di = dh * mult * xc
        dxc = dh * mult * i
        a2 = a * a
        dlog_a = da * a - dmult * a2 / mult
        dpre_r = (dlog_a * (RG_C * ls)) * r * (1.0 - r)
        dpre_i = di * i * (1.0 - i)
        dlam_ref[...] += jnp.sum(dlog_a * r, axis=0, keepdims=True) * (RG_C * _sigmoid(-lam_v))
        dbr_ref[...] += jnp.sum(dpre_r, axis=0, keepdims=True)
        dbi_ref[...] += jnp.sum(dpre_i, axis=0, keepdims=True)
        xcd = xc.astype(CD)
        dprc = dpre_r.astype(CD)
        dpic = dpre_i.astype(CD)
        tn_dims = (((0,), (0,)), ((), ()))
        dwr_ref[0] += lax.dot_general(xcd, dprc, tn_dims, preferred_element_type=F32)
        dwi_ref[0] += lax.dot_general(xcd, dpic, tn_dims, preferred_element_type=F32)
        dxc = dxc + jnp.dot(dprc, wrt_ref[0], preferred_element_type=F32) + jnp.dot(dpic, wit_ref[0],
                                                                                    preferred_element_type=F32)
        dcb_ref[...] += jnp.sum(dxc, axis=0, keepdims=True)
        for k in range(CONV_W):
            dcw_ref[pl.ds(k, 1), :] += jnp.sum(dxc * taps[k], axis=0, keepdims=True)
        head = head_ref[...]
        dxb = jnp.zeros_like(dxc)
        for sft in range(CONV_W):
            dxb = dxb + cw_ref[pl.ds(CONV_W - 1 - sft, 1), :] * _shift_up(dxc, head, sft)
        dx_ref[...] = dxb.astype(CD)
        head_ref[...] = dxc[0:SUBLANES, :]

    blk = pl.BlockSpec((tb, cbw), lambda n, t: (nt - 1 - t, n))
    tail = pl.BlockSpec((SUBLANES, cbw), lambda n, t: (jnp.maximum((nt - 1 - t) * groups - 1, 0), n))
    vec = pl.BlockSpec((1, cbw), lambda n, t: (0, n))
    cwb = pl.BlockSpec((CONV_W, cbw), lambda n, t: (0, n))
    wblk = pl.BlockSpec((1, cbw, cbw), lambda n, t: (n, 0, 0))
    vshape = jax.ShapeDtypeStruct((1, c), F32)
    wshape = jax.ShapeDtypeStruct((RG_BLOCKS, cbw, cbw), F32)
    return pl.pallas_call(
        body, name=name,
        out_shape=(jax.ShapeDtypeStruct((s, c), CD), jax.ShapeDtypeStruct((s, c), CD),
                   jax.ShapeDtypeStruct((CONV_W, c), F32), vshape, vshape, vshape, vshape, wshape, wshape),
        grid=(RG_BLOCKS, nt),
        in_specs=[blk, blk, blk, tail, blk, tail, cwb, vec, wblk, wblk, wblk, wblk, vec, vec, vec],
        out_specs=(blk, blk, cwb, vec, vec, vec, vec, wblk, wblk),
        scratch_shapes=[pltpu.VMEM((1, cbw), F32), pltpu.VMEM((SUBLANES, cbw), F32),
                        pltpu.VMEM((tb, cbw), F32), pltpu.VMEM((tb, cbw), F32), pltpu.VMEM((tb, cbw), F32)],
        compiler_params=_params("parallel", "arbitrary"),
    )(dy, gate_br, x_br, x_br, hs, hs, cw, cb, wr, wi, wrt, wit, br, bi, lam)


ATT_BLOCK = 256
ATT_Q_BLOCK = 1024
ATT_RATIO = ATT_Q_BLOCK // ATT_BLOCK
ATT_SCALE = 1.0 / math.sqrt(SB_HEAD_DIM)
N_PAIRS = SB_HEADS * SB_HEAD_DIM // LANES
NT_DIMS = (((1,), (1,)), ((), ()))
TN_DIMS = (((0,), (0,)), ((), ()))


LOG2E = 1.4426950408889634


def _neg_abs(x):
    bits = lax.bitcast_convert_type(x, jnp.uint32) | jnp.uint32(0x80000000)
    return lax.bitcast_convert_type(bits, F32)


def _qk(qx, kb):
    return lax.dot_general(qx, kb, NT_DIMS, preferred_element_type=F32)


def _sb_logits(qk, valid):
    z2 = qk * (ATT_SCALE * LOG2E)
    lb2 = jnp.minimum(z2, 0.0) - jnp.log2(1.0 + jnp.exp2(_neg_abs(z2)))
    l2 = lb2 - z2
    if valid is not None:
        l2 = jnp.where(valid, l2, 0.0)
    return lb2, l2


def _hi_lo(x):
    hi = x.astype(CD)
    lo = (x - hi.astype(F32)).astype(CD)
    return jnp.concatenate([hi, lo], axis=1)


def _tri(strict, stacked):
    r = lax.broadcasted_iota(jnp.int32, (ATT_BLOCK, ATT_BLOCK), 0)
    c = lax.broadcasted_iota(jnp.int32, (ATT_BLOCK, ATT_BLOCK), 1)
    m = (r > c if strict else r >= c).astype(CD)
    return jnp.concatenate([m, m], axis=0) if stacked else m


def _attn_fwd(qkv, name):
    _, s, _ = qkv.shape
    tq, t = ATT_Q_BLOCK, ATT_BLOCK
    nblk = s // tq

    def body(q_ref, k_ref, v_ref, o_ref, qk_scr, w_scr):
        i = pl.program_id(1)
        lane = lax.broadcasted_iota(jnp.int32, (1, LANES), 1)
        head_masks = (lane < SB_HEAD_DIM, lane >= SB_HEAD_DIM)
        q = q_ref[0]
        qs = [jnp.where(m, q, jnp.zeros_like(q)) for m in head_masks]
        tri = _tri(True, False)
        rr = lax.broadcasted_iota(jnp.int32, (tq, t), 0)
        cc = lax.broadcasted_iota(jnp.int32, (tq, t), 1)

        def rows_of(j):
            return pl.ds(pl.multiple_of(j * t, t), t)

        def tail(x, row0):
            return x if row0 == 0 else x[row0:]

        def start_logits(j, row0=0):
            kb = k_ref[0, rows_of(j), :]
            for hd in range(2):
                qk_scr[hd, row0:, :] = _qk(tail(qs[hd], row0), kb)

        def weights(run, diagonal=False, row0=0):
            new_run = []
            valid = (cc < rr)[:tq - row0] if diagonal else None
            for hd in range(2):
                lb2, l2 = _sb_logits(qk_scr[hd, row0:, :], valid)
                w = jnp.exp2(lb2 + (tail(run[hd], row0) + jnp.dot(l2.astype(CD), tri, preferred_element_type=F32)))
                if valid is not None:
                    w = jnp.where(valid, w, 0.0)
                w_scr[row0:, hd * t:(hd + 1) * t] = w.astype(CD)
                rowsum = jnp.sum(l2, axis=1, keepdims=True)
                if row0:
                    rowsum = jnp.concatenate([jnp.zeros((row0, 1), F32), rowsum], axis=0)
                new_run.append(run[hd] + rowsum)
            return tuple(new_run)

        def apply_weights(j, row0=0):
            vb = v_ref[0, rows_of(j), :]
            vcat = jnp.concatenate([jnp.where(m, vb, jnp.zeros_like(vb)) for m in head_masks], axis=0)
            inc = jnp.dot(w_scr[row0:, :], vcat, preferred_element_type=F32)
            return inc if row0 == 0 else jnp.concatenate([jnp.zeros((row0, LANES), F32), inc], axis=0)

        zero = jnp.zeros((tq, 1), F32)
        last = ATT_RATIO - 1
        start_logits(ATT_RATIO * i + last, last * t)
        run = weights((zero, zero), True, last * t)
        oacc = jnp.zeros((tq, LANES), F32)
        for d in reversed(range(last)):
            start_logits(ATT_RATIO * i + d, d * t)
            oacc = oacc + apply_weights(ATT_RATIO * i + d + 1, (d + 1) * t)
            run = weights(run, True, d * t)
        start_logits(jnp.maximum(ATT_RATIO * i - 1, 0))

        def step(jj, carry):
            run, oacc = carry
            b = ATT_RATIO * i - 1 - jj
            oacc = oacc + apply_weights(b + 1)
            run = weights(run)
            start_logits(jnp.maximum(b - 1, 0))
            return run, oacc

        run, oacc = lax.fori_loop(0, ATT_RATIO * i, step, (run, oacc))
        o_ref[0] = oacc + apply_weights(0)

    return pl.pallas_call(
        body, name=name, out_shape=jax.ShapeDtypeStruct((N_PAIRS, s, LANES), F32), grid=(N_PAIRS, nblk),
        in_specs=[pl.BlockSpec((1, tq, LANES), lambda p, i: (p, i, 0)),
                  pl.BlockSpec((1, s, LANES), lambda p, i: (N_PAIRS + p, 0, 0)),
                  pl.BlockSpec((1, s, LANES), lambda p, i: (2 * N_PAIRS + p, 0, 0))],
        out_specs=pl.BlockSpec((1, tq, LANES), lambda p, i: (p, i, 0)),
        scratch_shapes=[pltpu.VMEM((2, tq, t), F32), pltpu.VMEM((tq, 2 * t), CD)],
        compiler_params=_params("parallel", "arbitrary"),
    )(qkv, qkv, qkv)


def _attn_bwd(qkv, o, do, name):
    _, s, _ = qkv.shape
    tq, t = ATT_Q_BLOCK, ATT_BLOCK
    nblk = s // tq

    def body(q_ref, k_ref, v_ref, o_ref, do_ref, dq_ref, dk_ref, dv_ref, qk_scr, dw_scr, w_scr, dz_scr):
        i = pl.program_id(1)

        @pl.when(i == 0)
        def _():
            dk_ref[...] = jnp.zeros_like(dk_ref)
            dv_ref[...] = jnp.zeros_like(dv_ref)

        lane = lax.broadcasted_iota(jnp.int32, (1, LANES), 1)
        head_masks = (lane < SB_HEAD_DIM, lane >= SB_HEAD_DIM)
        q = q_ref[0]
        dov = do_ref[0]
        ov = o_ref[0]
        qs = [jnp.where(m, q, jnp.zeros_like(q)) for m in head_masks]
        q_scaled_t = jnp.concatenate([(qx.astype(F32) * ATT_SCALE).T for qx in qs], axis=1).astype(CD)
        docs = [jnp.where(m, dov, 0.0).astype(CD) for m in head_masks]
        docat_t = jnp.concatenate([jnp.where(m, dov, 0.0).T for m in head_masks], axis=1).astype(CD)
        totals = [jnp.sum(d.astype(F32) * ov, axis=1, keepdims=True) for d in docs]
        tri = _tri(True, False)
        tri_incl = _tri(False, True)
        rr = lax.broadcasted_iota(jnp.int32, (tq, t), 0)
        cc = lax.broadcasted_iota(jnp.int32, (tq, t), 1)

        def rows_of(j):
            return pl.ds(pl.multiple_of(j * t, t), t)

        def tail(x, row0):
            return x if row0 == 0 else x[row0:]

        def pad_rows(x, row0):
            return x if row0 == 0 else jnp.concatenate([jnp.zeros((row0, x.shape[1]), x.dtype), x], axis=0)

        def start_products(j, row0=0):
            kb = k_ref[0, rows_of(j), :]
            vb = v_ref[0, rows_of(j), :]
            for hd in range(2):
                qk_scr[hd, row0:, :] = _qk(tail(qs[hd], row0), kb)
                dw_scr[hd, row0:, :] = lax.dot_general(tail(docs[hd], row0), vb, NT_DIMS, preferred_element_type=F32)

        def logit_grads(run, erun, diagonal=False, row0=0):
            new_run, new_erun = [], []
            valid = (cc < rr)[:tq - row0] if diagonal else None
            for hd in range(2):
                lb2, l2 = _sb_logits(qk_scr[hd, row0:, :], valid)
                w = jnp.exp2(lb2 + (tail(run[hd], row0) + jnp.dot(l2.astype(CD), tri, preferred_element_type=F32)))
                if valid is not None:
                    w = jnp.where(valid, w, 0.0)
                wc = w.astype(CD)
                w_scr[hd * tq + row0:(hd + 1) * tq, :] = wc
                e = dw_scr[hd, row0:, :] * wc.astype(F32)
                prefix = (tail(totals[hd] - erun[hd], row0)
                          - jnp.dot(_hi_lo(e), tri_incl, preferred_element_type=F32))
                dz = e - jnp.exp2(lb2) * (e + prefix)
                if valid is not None:
                    dz = jnp.where(valid, dz, 0.0)
                dz_scr[hd * tq + row0:(hd + 1) * tq, :] = dz.astype(CD)
                new_run.append(run[hd] + pad_rows(jnp.sum(l2, axis=1, keepdims=True), row0))
                new_erun.append(erun[hd] + pad_rows(jnp.sum(e, axis=1, keepdims=True), row0))
            return tuple(new_run), tuple(new_erun)

        def apply_grads(j, row0=0):
            rows = rows_of(j)
            kb = k_ref[0, rows, :]
            kcat = jnp.concatenate([jnp.where(m, kb, jnp.zeros_like(kb)) for m in head_masks], axis=0)
            dz_heads = [dz_scr[hd * tq + row0:(hd + 1) * tq, :] for hd in range(2)]
            w_heads = [w_scr[hd * tq + row0:(hd + 1) * tq, :] for hd in range(2)]
            q_t = jnp.concatenate([q_scaled_t[:, hd * tq + row0:(hd + 1) * tq] for hd in range(2)], axis=1)
            do_t = jnp.concatenate([docat_t[:, hd * tq + row0:(hd + 1) * tq] for hd in range(2)], axis=1)
            dk_ref[0, :, rows] += jnp.dot(q_t, jnp.concatenate(dz_heads, axis=0), preferred_element_type=F32)
            dv_ref[0, :, rows] += jnp.dot(do_t, jnp.concatenate(w_heads, axis=0), preferred_element_type=F32)
            return pad_rows(jnp.dot(jnp.concatenate(dz_heads, axis=1), kcat, preferred_element_type=F32), row0)

        zero = jnp.zeros((tq, 1), F32)
        last = ATT_RATIO - 1
        start_products(ATT_RATIO * i + last, last * t)
        run, erun = logit_grads((zero, zero), (zero, zero), True, last * t)
        dqacc = jnp.zeros((tq, LANES), F32)
        for d in reversed(range(last)):
            start_products(ATT_RATIO * i + d, d * t)
            dqacc = dqacc + apply_grads(ATT_RATIO * i + d + 1, (d + 1) * t)
            run, erun = logit_grads(run, erun, True, d * t)
        start_products(jnp.maximum(ATT_RATIO * i - 1, 0))

        def step(jj, carry):
            run, erun, dqacc = carry
            b = ATT_RATIO * i - 1 - jj
            dqacc = dqacc + apply_grads(b + 1)
            run, erun = logit_grads(run, erun)
            start_products(jnp.maximum(b - 1, 0))
            return run, erun, dqacc

        run, erun, dqacc = lax.fori_loop(0, ATT_RATIO * i, step, (run, erun, dqacc))
        dq_ref[0] = (dqacc + apply_grads(0)) * ATT_SCALE

    qblk = pl.BlockSpec((1, tq, LANES), lambda p, i: (p, i, 0))
    full = pl.BlockSpec((1, LANES, s), lambda p, i: (p, 0, 0))
    shape = jax.ShapeDtypeStruct((N_PAIRS, s, LANES), F32)
    shape_t = jax.ShapeDtypeStruct((N_PAIRS, LANES, s), F32)
    dq, dk_t, dv_t = pl.pallas_call(
        body, name=name, out_shape=(shape, shape_t, shape_t), grid=(N_PAIRS, nblk),
        in_specs=[qblk,
                  pl.BlockSpec((1, s, LANES), lambda p, i: (N_PAIRS + p, 0, 0)),
                  pl.BlockSpec((1, s, LANES), lambda p, i: (2 * N_PAIRS + p, 0, 0)),
                  qblk, qblk],
        out_specs=(qblk, full, full),
        scratch_shapes=[pltpu.VMEM((2, tq, t), F32), pltpu.VMEM((2, tq, t), F32),
                        pltpu.VMEM((2 * tq, t), CD), pltpu.VMEM((2 * tq, t), CD)],
        compiler_params=_params("parallel", "arbitrary"),
    )(qkv, qkv, qkv, o, do)
    return dq, jnp.swapaxes(dk_t, 1, 2), jnp.swapaxes(dv_t, 1, 2)


def _adamw(w, g, m, v, name):
    shape = w.shape
    rows, cols = (shape[-2], shape[-1]) if len(shape) >= 2 else (1, shape[-1])
    lead = w.size // (rows * cols)
    tr = _pick(rows, (512, 256, 128, 64, 32, 16, 8))

    def body(w_ref, g_ref, m_ref, v_ref, d_ref, nm_ref, nv_ref):
        gv = g_ref[...]
        nm = ADAM_B1 * m_ref[...] + (1.0 - ADAM_B1) * gv
        nv = ADAM_B2 * v_ref[...] + (1.0 - ADAM_B2) * (gv * gv)
        m_hat = nm / (1.0 - ADAM_B1 ** ADAM_STEP)
        v_hat = nv / (1.0 - ADAM_B2 ** ADAM_STEP)
        d_ref[...] = -ADAM_LR * (m_hat / (jnp.sqrt(v_hat) + ADAM_EPS) + ADAM_WD * w_ref[...])
        nm_ref[...] = nm
        nv_ref[...] = nv

    blk = pl.BlockSpec((1, tr, cols), lambda l, i: (l, i, 0))
    out = jax.ShapeDtypeStruct((lead, rows, cols), F32)
    d, nm, nv = pl.pallas_call(
        body, name=name, out_shape=(out, out, out), grid=(lead, rows // tr),
        in_specs=[blk, blk, blk, blk], out_specs=(blk, blk, blk), compiler_params=_params("parallel", "parallel"),
    )(*[a.reshape(lead, rows, cols) for a in (w, g, m, v)])
    return d.reshape(shape), nm.reshape(shape), nv.reshape(shape)


HBM = pl.BlockSpec(memory_space=pltpu.HBM)


def _coords():
    return lax.axis_index("x"), lax.axis_index("y"), lax.axis_index("c")


def _other_chips(x, y):
    return [(1 - x, y), (x, 1 - y), (1 - x, 1 - y)]


def _allgather_chips(shard, name, collective_id=None):
    r, cols = shard.shape
    half = r // 2
    quarter = half // 2

    def body(src_ref, out_ref, send_sems, recv_sems):
        x, y, c = _coords()
        sibling = (x, y, 1 - c)
        nx, ny, diag = (1 - x, y), (x, 1 - y), (1 - x, 1 - y)

        def piece(chip, core, lo, n):
            return out_ref.at[2 * chip[0] + chip[1], pl.ds(core * half + lo, n), :]

        def copy(k, dst, to, src=None):
            return pltpu.make_async_remote_copy(
                src_ref=dst if src is None else src, dst_ref=dst,
                send_sem=send_sems.at[k], recv_sem=recv_sems.at[k], device_id=to, device_id_type=MESH)

        me = (x, y)
        mine = src_ref.at[pl.ds(c * half, half), :]
        direct = [copy(0, piece(me, c, 0, half), (*nx, c), src=mine), copy(1, piece(me, c, 0, half), (*ny, c), src=mine)]
        for cp in direct:
            cp.start()
        arrivals = [piece(nx, c, 0, half), piece(ny, c, 0, half), piece(diag, c, 0, quarter),
                    piece(diag, c, quarter, quarter)]
        onward = [copy(2, piece(nx, c, 0, quarter), (*ny, c)), copy(3, piece(ny, c, quarter, quarter), (*nx, c))]
        to_sibling = [copy(4 + k, dst, sibling) for k, dst in enumerate(arrivals)]
        for k, dst in enumerate(arrivals):
            copy(k, dst, (x, y, c)).wait_recv()
            if k < 2:
                onward[k].start()
            to_sibling[k].start()
        from_sibling = [piece(nx, 1 - c, 0, half), piece(ny, 1 - c, 0, half), piece(diag, 1 - c, 0, quarter),
                        piece(diag, 1 - c, quarter, quarter)]
        for k, dst in enumerate(from_sibling):
            copy(4 + k, dst, (x, y, c)).wait_recv()
        for cp in direct + onward + to_sibling:
            cp.wait_send()

    out_shape = jax.ShapeDtypeStruct((N_CHIPS, r, cols), shard.dtype)
    sems = (pltpu.SemaphoreType.DMA((8,)), pltpu.SemaphoreType.DMA((8,)))
    if collective_id is None:
        return pl.pallas_call(body, name=name, out_shape=out_shape, in_specs=[HBM], out_specs=HBM,
                              scratch_shapes=list(sems))(shard)
    shard_ref = jax.new_ref(shard, memory_space=pltpu.MemorySpace.HBM)
    gathered_ref = jax.empty_ref(out_shape, memory_space=pltpu.MemorySpace.HBM)

    @_sequencer(name, collective_id, sems)
    def launch(send_sems, recv_sems):
        x, y, c = _coords()
        _handshake([(1 - x, y, c), (x, 1 - y, c), (x, y, 1 - c)])
        body(shard_ref, gathered_ref, send_sems, recv_sems)

    launch()
    return gathered_ref[...]


def _exchange_sibling_halves(g, name):
    n, r, cols = g.shape
    half = r // 2

    def body(g_ref, out_ref, send_sem, recv_sem):
        x, y, c = _coords()
        cp = pltpu.make_async_remote_copy(
            src_ref=g_ref.at[:, pl.ds((1 - c) * half, half), :], dst_ref=out_ref,
            send_sem=send_sem, recv_sem=recv_sem, device_id=(x, y, 1 - c), device_id_type=MESH)
        cp.start()
        cp.wait()

    return pl.pallas_call(
        body, name=name, out_shape=jax.ShapeDtypeStruct((n, half, cols), g.dtype),
        in_specs=[HBM], out_specs=HBM,
        scratch_shapes=[pltpu.SemaphoreType.DMA, pltpu.SemaphoreType.DMA],
    )(g)


def _sequencer(name, collective_id, scratch_types):
    return pl.kernel(mesh=plsc.ScalarSubcoreMesh(axis_name="sequencer", num_cores=1), name=name,
                     scratch_types=scratch_types, compiler_params=pltpu.CompilerParams(collective_id=collective_id))


def _handshake(peers):
    barrier = pltpu.get_barrier_semaphore()
    for peer in peers:
        pl.semaphore_signal(barrier, inc=1, device_id=peer, device_id_type=MESH)
    pl.semaphore_wait(barrier, len(peers))


def _exchange_sibling_halves_async(g, name, collective_id):
    n, r, cols = g.shape
    half = r // 2
    g_ref = jax.new_ref(g, memory_space=pltpu.MemorySpace.HBM)
    out_ref = jax.empty_ref(jax.ShapeDtypeStruct((n, half, cols), g.dtype), memory_space=pltpu.MemorySpace.HBM)

    @_sequencer(name, collective_id, (pltpu.SemaphoreType.DMA, pltpu.SemaphoreType.DMA))
    def launch(send_sem, recv_sem):
        x, y, c = _coords()
        _handshake([(x, y, 1 - c)])
        cp = pltpu.make_async_remote_copy(
            src_ref=g_ref.at[:, pl.ds((1 - c) * half, half), :], dst_ref=out_ref,
            send_sem=send_sem, recv_sem=recv_sem, device_id=(x, y, 1 - c), device_id_type=MESH)
        cp.start()
        cp.wait()

    launch()
    return out_ref[...]


def _share_halves_async(v, name, collective_id):
    h = v.shape[0] // 2
    v_ref = jax.new_ref(v, memory_space=pltpu.MemorySpace.HBM)

    @_sequencer(name, collective_id, (pltpu.SemaphoreType.DMA, pltpu.SemaphoreType.DMA))
    def launch(send_sem, recv_sem):
        x, y, c = _coords()
        _handshake([(x, y, 1 - c)])
        cp = pltpu.make_async_remote_copy(
            src_ref=v_ref.at[pl.ds(c * h, h), :], dst_ref=v_ref.at[pl.ds(c * h, h), :],
            send_sem=send_sem, recv_sem=recv_sem, device_id=(x, y, 1 - c), device_id_type=MESH)
        cp.start()
        pltpu.make_async_remote_copy(
            src_ref=v_ref.at[pl.ds(c * h, h), :], dst_ref=v_ref.at[pl.ds((1 - c) * h, h), :],
            send_sem=send_sem, recv_sem=recv_sem, device_id=(x, y, 1 - c), device_id_type=MESH).wait_recv()
        cp.wait_send()

    launch()
    return v_ref[...]


def _scatter_to_chips_async(p, name, collective_id):
    p_ref = jax.new_ref(p, memory_space=pltpu.MemorySpace.HBM)
    out_ref = jax.empty_ref(jax.ShapeDtypeStruct(p.shape, p.dtype), memory_space=pltpu.MemorySpace.HBM)

    @_sequencer(name, collective_id, (pltpu.SemaphoreType.DMA((3,)), pltpu.SemaphoreType.DMA((3,))))
    def launch(send_sems, recv_sems):
        x, y, c = _coords()
        me = 2 * x + y
        _handshake([(px, py, c) for px, py in _other_chips(x, y)])
        sends = []
        for j, (px, py) in enumerate(_other_chips(x, y)):
            sends.append(pltpu.make_async_remote_copy(
                src_ref=p_ref.at[2 * px + py], dst_ref=out_ref.at[me],
                send_sem=send_sems.at[j], recv_sem=recv_sems.at[j], device_id=(px, py, c), device_id_type=MESH))
        for cp in sends:
            cp.start()
        for j, (px, py) in enumerate(_other_chips(x, y)):
            pltpu.make_async_remote_copy(
                src_ref=p_ref.at[me], dst_ref=out_ref.at[2 * px + py],
                send_sem=send_sems.at[j], recv_sem=recv_sems.at[j], device_id=(px, py, c),
                device_id_type=MESH).wait_recv()
        for cp in sends:
            cp.wait_send()

    launch()
    return out_ref[...]


def _share_halves(v, name):
    h = v.shape[0] // 2

    def body(v_ref, out_ref, send_sem, recv_sem):
        x, y, c = _coords()
        cp = pltpu.make_async_remote_copy(
            src_ref=v_ref.at[pl.ds(c * h, h), :], dst_ref=out_ref.at[pl.ds(c * h, h), :],
            send_sem=send_sem, recv_sem=recv_sem, device_id=(x, y, 1 - c), device_id_type=MESH)
        cp.start()
        pltpu.make_async_remote_copy(
            src_ref=v_ref.at[pl.ds(c * h, h), :], dst_ref=out_ref.at[pl.ds((1 - c) * h, h), :],
            send_sem=send_sem, recv_sem=recv_sem, device_id=(x, y, 1 - c), device_id_type=MESH).wait_recv()
        cp.wait_send()

    return pl.pallas_call(
        body, name=name, out_shape=jax.ShapeDtypeStruct(v.shape, v.dtype),
        in_specs=[HBM], out_specs=HBM, input_output_aliases={0: 0},
        scratch_shapes=[pltpu.SemaphoreType.DMA, pltpu.SemaphoreType.DMA],
    )(v)


def _allreduce_small(v, name):
    r, cols = v.shape

    def body(v_ref, out_ref, buf_ref, send_sems, recv_sems):
        x, y, c = _coords()
        me = 4 * x + 2 * y + c
        buf_ref[me] = v_ref[...]
        sends = []
        for k in range(1, N_DEV):
            px = 1 - x if k & 4 else x
            py = 1 - y if k & 2 else y
            pc = 1 - c if k & 1 else c
            sends.append(pltpu.make_async_remote_copy(
                src_ref=v_ref, dst_ref=buf_ref.at[me], send_sem=send_sems.at[k - 1], recv_sem=recv_sems.at[k - 1],
                device_id=(px, py, pc), device_id_type=MESH))
        for cp in sends:
            cp.start()
        for cp in sends:
            cp.wait()
        acc = buf_ref[0]
        for d in range(1, N_DEV):
            acc = acc + buf_ref[d]
        out_ref[...] = acc

    return pl.pallas_call(
        body, name=name, out_shape=jax.ShapeDtypeStruct((r, cols), F32),
        in_specs=[pl.BlockSpec(memory_space=pltpu.VMEM)], out_specs=pl.BlockSpec(memory_space=pltpu.VMEM),
        scratch_shapes=[pltpu.VMEM((N_DEV, r, cols), F32), pltpu.SemaphoreType.DMA((N_DEV - 1,)),
                        pltpu.SemaphoreType.DMA((N_DEV - 1,))],
    )(v)


def _add_sibling(g, from_sibling, core, name):
    n, h, cols = from_sibling.shape
    tr = _row_tile(h)
    steps = h // tr

    def body(core_ref, a_ref, b_ref, o_ref):
        o_ref[...] = (a_ref[...] + b_ref[...]).astype(o_ref.dtype)

    return pl.pallas_call(
        body, name=name, out_shape=jax.ShapeDtypeStruct(from_sibling.shape, jnp.bfloat16),
        grid_spec=pltpu.PrefetchScalarGridSpec(
            num_scalar_prefetch=1, grid=(n, steps),
            in_specs=[pl.BlockSpec((1, tr, cols), lambda s, i, core_ref: (s, core_ref[0] * steps + i, 0)),
                      pl.BlockSpec((1, tr, cols), lambda s, i, core_ref: (s, i, 0))],
            out_specs=pl.BlockSpec((1, tr, cols), lambda s, i, core_ref: (s, i, 0))),
        compiler_params=_params("parallel", "parallel"),
    )(core.reshape(1).astype(jnp.int32), g, from_sibling)


def _sum_slots(p, own, chip, core, name):
    n, r, cols = p.shape
    tr = _row_tile(r)
    steps = r // tr

    def body(core_ref, chip_ref, p_ref, own_ref, o_ref):
        parts = [jnp.where(chip_ref[0] == s, own_ref[0], p_ref[s]).astype(F32) for s in range(n)]
        o_ref[...] = ((parts[0] + parts[1]) + parts[2]) + parts[3]

    return pl.pallas_call(
        body, name=name, out_shape=jax.ShapeDtypeStruct((2 * r, cols), F32),
        grid_spec=pltpu.PrefetchScalarGridSpec(
            num_scalar_prefetch=2, grid=(steps,),
            in_specs=[pl.BlockSpec((n, tr, cols), lambda i, core_ref, chip_ref: (0, i, 0)),
                      pl.BlockSpec((1, tr, cols), lambda i, core_ref, chip_ref: (chip_ref[0], i, 0))],
            out_specs=pl.BlockSpec((tr, cols), lambda i, core_ref, chip_ref: (core_ref[0] * steps + i, 0))),
        compiler_params=_params("parallel"),
    )(core.reshape(1).astype(jnp.int32), chip.reshape(1).astype(jnp.int32), p, own)


PACK_COLS = 1024


def _pack_shards(parts):
    return jnp.concatenate([p.reshape(-1, PACK_COLS) for p in parts], axis=0)


def _unpack_shards(buf, shapes):
    out, row = [], 0
    for shp in shapes:
        nrows = math.prod(shp) // PACK_COLS
        out.append(buf[..., row:row + nrows, :].reshape(buf.shape[:-2] + tuple(shp)))
        row += nrows
    return out


def _local_step(x, target, w):
    t = lambda a: a.T
    g = {}
    w_in_g, w_in_x = w["a_w_in"][:, :D_RNN], w["a_w_in"][:, D_RNN:]
    h0, gate_br, x_br = _norm_and_project(x, w["norm_mix_g"][0], w_in_g, w_in_x, "rglru_in")
    y_a, hs = _rglru_fwd(gate_br, x_br, w["a_conv_w"], w["a_conv_b"], w["a_w_r"], w["a_w_i"], w["a_b_r"],
                         w["a_b_i"], w["a_lambda"], "rglru_fwd")
    x1, h1 = _matmul([(y_a, w["a_w_out"])], F32, "mm_a_out", addend=x, norm_gain=w["norm_ffn_g"][0])
    fg0, fu0, act0 = _ffn_up(h1, w["ffn_w_gate"][0], w["ffn_w_up"][0], "ffn0_up")
    x2, h2 = _matmul([(act0, w["ffn_w_down"][0])], F32, "mm_f0_down", addend=x1, norm_gain=w["norm_mix_g"][1])
    qkv = _matmul([(h2, w["b_w_qkv"])], CD, "mm_b_qkv", out_lbm=True, tn=1024)
    o = _attn_fwd(qkv, "attn_fwd")
    x3, h3 = _matmul([(o, w["b_w_out"])], F32, "mm_b_out", a_lbm=True, addend=x2, norm_gain=w["norm_ffn_g"][1])
    fg1, fu1, act1 = _ffn_up(h3, w["ffn_w_gate"][1], w["ffn_w_up"][1], "ffn1_up")
    dx4, dx4c, g["final_g"], loss = _matmul([(act1, w["ffn_w_down"][1])], F32, "mm_f1_down", addend=x3,
                                            loss_head=(w["final_g"], target))

    def ffn_bwd(dx_out, dxc, h, x_in, fg, fu, act, layer, tag):
        dg, du = _ffn_dact(dxc, t(w["ffn_w_down"][layer]), fg, fu, "ffn_" + tag + "_dact")
        dwd = _matmul([(act, dxc)], F32, "mm_" + tag + "_dwd", trans_a=True)
        dwg = _matmul([(h, dg)], F32, "mm_" + tag + "_dwg", trans_a=True)
        dwu = _matmul([(h, du)], F32, "mm_" + tag + "_dwu", trans_a=True)
        dx_in, dx_in_c, dgain = _matmul([(dg, t(w["ffn_w_gate"][layer])), (du, t(w["ffn_w_up"][layer]))], F32,
                                        "mm_" + tag + "_dh", norm_bwd=(x_in, w["norm_ffn_g"][layer], dx_out))
        return dx_in, dx_in_c, dgain, dwg, dwu, dwd

    dx3, dx3c, dgf1, dwg1, dwu1, dwd1 = ffn_bwd(dx4, dx4c, h3, x3, fg1, fu1, act1, 1, "f1")
    do = _matmul([(dx3c, t(w["b_w_out"]))], F32, "mm_b_do", out_lbm=True, tn=1024)
    g["b_w_out"] = _matmul([(o, dx3c)], F32, "mm_b_dwout", trans_a=True, a_lbm=True)
    dq, dk, dv = _attn_bwd(qkv, o, do, "attn_bwd")
    wq_t = t(w["b_w_qkv"])
    parts = (dq, dk, dv)
    g["b_w_qkv"] = jnp.concatenate(
        [_matmul([(h2, p)], F32, "mm_b_dwqkv%d" % n, trans_a=True, b_lbm=True) for n, p in enumerate(parts)], axis=1)
    dx2, dx2c, dgm1 = _matmul([(p, wq_t[n * D_MODEL:(n + 1) * D_MODEL]) for n, p in enumerate(parts)], F32, "mm_b_dh",
                              a_lbm=True, norm_bwd=(x2, w["norm_mix_g"][1], dx3))
    dx1, dx1c, dgf0, dwg0, dwu0, dwd0 = ffn_bwd(dx2, dx2c, h1, x1, fg0, fu0, act0, 0, "f0")
    dy_a = _matmul([(dx1c, t(w["a_w_out"]))], F32, "mm_a_dy")
    g["a_w_out"] = _matmul([(y_a, dx1c)], F32, "mm_a_dwout", trans_a=True)
    wrt = jnp.swapaxes(w["a_w_r"], 1, 2)
    wit = jnp.swapaxes(w["a_w_i"], 1, 2)
    (dgate, dxbr, g["a_conv_w"], g["a_conv_b"], g["a_b_r"], g["a_b_i"], g["a_lambda"], g["a_w_r"],
     g["a_w_i"]) = _rglru_bwd(dy_a, gate_br, x_br, hs, w["a_conv_w"], w["a_conv_b"], w["a_w_r"], w["a_w_i"], wrt, wit,
                              w["a_b_r"], w["a_b_i"], w["a_lambda"], "rglru_bwd")
    g["a_w_in"] = jnp.concatenate([_matmul([(h0, dgate)], F32, "mm_a_dwin_g", trans_a=True),
                                   _matmul([(h0, dxbr)], F32, "mm_a_dwin_x", trans_a=True)], axis=1)
    dx0, _, dgm0 = _matmul([(dgate, t(w_in_g)), (dxbr, t(w_in_x))], F32, "mm_a_dh",
                           norm_bwd=(x, w["norm_mix_g"][0], dx1))
    g["norm_mix_g"] = jnp.concatenate([dgm0, dgm1], axis=0)
    g["norm_ffn_g"] = jnp.concatenate([dgf0, dgf1], axis=0)
    g["ffn_w_gate"] = [dwg0, dwg1]
    g["ffn_w_up"] = [dwu0, dwu1]
    g["ffn_w_down"] = [dwd0, dwd1]
    return loss, dx0, g


WEIGHTS = ["norm_mix_g", "norm_ffn_g", "a_w_in", "a_conv_w", "a_conv_b", "a_w_r", "a_b_r", "a_w_i", "a_b_i",
           "a_lambda", "a_w_out", "b_w_qkv", "b_w_out", "ffn_w_gate", "ffn_w_up", "ffn_w_down", "final_g"]
BIG = [("a_w_in", 2), ("a_w_r", 2), ("a_w_i", 2), ("a_w_out", 1), ("b_w_qkv", 2), ("b_w_out", 1),
       ("ffn_w_gate", 2), ("ffn_w_up", 2), ("ffn_w_down", 1)]
LAYER1 = ["b_w_qkv", "b_w_out", "ffn_w_gate", "ffn_w_up", "ffn_w_down"]
LAYER0 = ["a_w_in", "a_w_r", "a_w_i", "a_w_out", "ffn_w_gate", "ffn_w_up", "ffn_w_down"]
RS_COLLECTIVE_IDS = {"chips1": 3, "chips0": 4, "sibling1": 5, "share1": 6}
GATHER_COLLECTIVE_IDS = (8, 7)
SMALL = ["norm_mix_g", "norm_ffn_g", "a_conv_w", "a_conv_b", "a_b_r", "a_b_i", "a_lambda", "final_g"]


def _split_chips(full, axis):
    if axis == 1:
        return full.reshape((N_CHIPS, 1, full.shape[1] // N_CHIPS) + full.shape[2:])
    return jnp.stack(jnp.split(full, N_CHIPS, axis=axis))


def _step(x, target, weights, moments_m, moments_v):
    chip = 2 * lax.axis_index("x") + lax.axis_index("y")
    core = lax.axis_index("c")
    axis_of = dict(BIG)
    full = {}
    for group, layer, tag, collective_id in ((LAYER0[:4], 0, "0a", None), (LAYER0[4:], 0, "0f", GATHER_COLLECTIVE_IDS[0]),
                                             (LAYER1, 1, "1", GATHER_COLLECTIVE_IDS[1])):
        shards = [weights[n][layer % weights[n].shape[0]].astype(CD) for n in group]
        packed = _pack_shards(shards)
        if collective_id is not None:
            packed, first_gathered = lax.optimization_barrier((packed, first_gathered))
        gathered = _allgather_chips(packed, "allgather_weights" + tag, collective_id)
        if collective_id is None:
            first_gathered = gathered
        for n, own, stack in zip(group, shards, _unpack_shards(gathered, [sh.shape for sh in shards])):
            joined = jnp.concatenate([jnp.where(chip == s, own, stack[s]) for s in range(N_CHIPS)],
                                     axis=axis_of[n] - 1)
            full.setdefault(n, {})[layer] = joined
    full = {n: (v[0] if n.startswith("a_") else v[1] if n.startswith("b_") else [v[0], v[1]]) for n, v in full.items()}
    cw_rows = jnp.zeros((N_CHIPS, CONV_W, RG_BW), F32)
    cw_rows = lax.dynamic_update_slice(cw_rows, jnp.where(core == 0, weights["a_conv_w"], 0.0), (chip, 0, 0))
    cw_all = _allreduce_small(cw_rows.reshape(-1, LANES), "allgather_conv_w").reshape(N_CHIPS, CONV_W, RG_BW)
    full["a_conv_w"] = jnp.concatenate([cw_all[s] for s in range(N_CHIPS)], axis=1)
    for n in ("norm_mix_g", "norm_ffn_g", "final_g"):
        full[n] = weights[n]
    for n in ("a_conv_b", "a_b_r", "a_b_i", "a_lambda"):
        full[n] = weights[n]
    loss, dx, grads = _local_step(x[0], target[0], full)
    small_parts = [grads[n].reshape(-1) for n in SMALL] + [loss.reshape(-1)]
    sizes = [p.shape[0] for p in small_parts]
    small = _allreduce_small(jnp.concatenate(small_parts).reshape(-1, LANES), "allreduce_small").reshape(-1)
    red, pos = {}, 0
    for n, sz in zip(SMALL + ["loss"], sizes):
        red[n] = small[pos:pos + sz]
        pos += sz
    loss_out = red["loss"][0]
    g_out = {}
    for n in SMALL:
        if n == "a_conv_w":
            g_out[n] = lax.dynamic_slice(red[n].reshape(CONV_W, D_RNN), (0, chip * RG_BW), (CONV_W, RG_BW)).reshape(
                weights[n].shape)
        else:
            g_out[n] = red[n].reshape(weights[n].shape)
    axis_of = dict(BIG)
    pieces = {}
    for group, layer, tag in ((LAYER1, 1, "1"), (LAYER0, 0, "0")):
        stacks, shapes = [], []
        for n in group:
            per_layer = isinstance(grads[n], list)
            gfull = grads[n][layer] if per_layer else grads[n]
            shard_shape = weights[n].shape[1:]
            gfull = gfull.reshape((1,) + gfull.shape)
            stacks.append(_split_chips(gfull, axis_of[n]).reshape(N_CHIPS, -1, PACK_COLS))
            shapes.append((1,) + tuple(shard_shape))
        gbuf = jnp.concatenate(stacks, axis=1)
        if layer == 1:
            from_sibling = _exchange_sibling_halves_async(gbuf, "rs_sibling" + tag, RS_COLLECTIVE_IDS["sibling1"])
        else:
            from_sibling = _exchange_sibling_halves(gbuf, "rs_sibling" + tag)
        chip_partial = _add_sibling(gbuf, from_sibling, core, "rs_add" + tag)
        from_chips = _scatter_to_chips_async(chip_partial, "rs_chips" + tag, RS_COLLECTIVE_IDS["chips" + tag])
        halves = _sum_slots(from_chips, chip_partial, chip, core, "rs_sum" + tag)
        if layer == 1:
            reduced = _share_halves_async(halves, "rs_share" + tag, RS_COLLECTIVE_IDS["share1"])
        else:
            reduced = _share_halves(halves, "rs_share" + tag)
        for n, piece in zip(group, _unpack_shards(reduced, shapes)):
            pieces.setdefault(n, {})[layer] = piece
    for n, _ in BIG:
        layers = pieces[n]
        g_out[n] = jnp.concatenate([layers[k] for k in sorted(layers)], axis=0)
    updates = {}
    for n, _ in BIG:
        updates[n] = _adamw(weights[n], g_out[n], moments_m[n], moments_v[n], "adamw_" + n)
    rows = lambda d: jnp.concatenate([d[n].reshape(-1, D_MODEL) for n in SMALL], axis=0)
    small_updates = _adamw(rows(weights), rows(g_out), rows(moments_m), rows(moments_v), "adamw_small")
    pos = 0
    for n in SMALL:
        nrows = weights[n].size // D_MODEL
        updates[n] = tuple(u[pos:pos + nrows].reshape(weights[n].shape) for u in small_updates)
        pos += nrows
    outs_g = [g_out[n] for n in WEIGHTS]
    outs_d, outs_m, outs_v = ([updates[n][k] for n in WEIGHTS] for k in range(3))
    return (loss_out, dx[None], *outs_g, *outs_d, *outs_m, *outs_v)


def kernel(x, norm_mix_g, norm_ffn_g, a_w_in, a_conv_w, a_conv_b, a_w_r, a_b_r, a_w_i, a_b_i, a_lambda, a_w_out, b_w_qkv, b_w_out, ffn_w_gate, ffn_w_up, ffn_w_down, final_g, loss_target, m_norm_mix_g, m_norm_ffn_g, m_a_w_in, m_a_conv_w, m_a_conv_b, m_a_w_r, m_a_b_r, m_a_w_i, m_a_b_i, m_a_lambda, m_a_w_out, m_b_w_qkv, m_b_w_out, m_ffn_w_gate, m_ffn_w_up, m_ffn_w_down, m_final_g, v_norm_mix_g, v_norm_ffn_g, v_a_w_in, v_a_conv_w, v_a_conv_b, v_a_w_r, v_a_b_r, v_a_w_i, v_a_b_i, v_a_lambda, v_a_w_out, v_b_w_qkv, v_b_w_out, v_ffn_w_gate, v_ffn_w_up, v_ffn_w_down, v_final_g):
    ws = [norm_mix_g, norm_ffn_g, a_w_in, a_conv_w, a_conv_b, a_w_r, a_b_r, a_w_i, a_b_i, a_lambda, a_w_out, b_w_qkv,
          b_w_out, ffn_w_gate, ffn_w_up, ffn_w_down, final_g]
    ms = [m_norm_mix_g, m_norm_ffn_g, m_a_w_in, m_a_conv_w, m_a_conv_b, m_a_w_r, m_a_b_r, m_a_w_i, m_a_b_i, m_a_lambda,
          m_a_w_out, m_b_w_qkv, m_b_w_out, m_ffn_w_gate, m_ffn_w_up, m_ffn_w_down, m_final_g]
    vs = [v_norm_mix_g, v_norm_ffn_g, v_a_w_in, v_a_conv_w, v_a_conv_b, v_a_w_r, v_a_b_r, v_a_w_i, v_a_b_i, v_a_lambda,
          v_a_w_out, v_b_w_qkv, v_b_w_out, v_ffn_w_gate, v_ffn_w_up, v_ffn_w_down, v_final_g]
    return _step(x, loss_target, dict(zip(WEIGHTS, ws)), dict(zip(WEIGHTS, ms)), dict(zip(WEIGHTS, vs)))
```

```python
import functools
import math

import jax
import jax.numpy as jnp
from jax import lax
from jax.experimental import pallas as pl
from jax.experimental.pallas import tpu as pltpu
from jax.experimental.pallas import tpu_sc as plsc

F32 = jnp.float32
CD = jnp.bfloat16

D_MODEL = 1024
D_RNN = 1024
RG_BLOCKS = 4
RG_BW = 256
CONV_W = 4
RG_C = 8.0
SB_HEADS = 16
SB_HEAD_DIM = 64
D_FF = 2816
RMS_EPS = 1e-6
N_CHIPS = 4
N_DEV = 8

ADAM_LR = 0.001
ADAM_B1 = 0.9
ADAM_B2 = 0.999
ADAM_EPS = 1e-08
ADAM_WD = 0.01
ADAM_STEP = 10

LANES = 128
VMEM_LIMIT = 56 * 1024 * 1024
MESH = pl.DeviceIdType.MESH


def _params(*sem):
    return pltpu.CompilerParams(dimension_semantics=sem, vmem_limit_bytes=VMEM_LIMIT)


def _pick(n, prefs):
    for p in prefs:
        if n % p == 0:
            return p
    return n


def _row_tile(rows):
    return max(d for d in range(16, 1025, 16) if rows % d == 0)


def _matmul(pairs, out_dtype, name, *, trans_a=False, a_lbm=False, b_lbm=False, out_lbm=False, addend=None,
            tm=512, tn=None, tk=None, norm_gain=None, norm_bwd=None, loss_head=None):
    a0, b0 = pairs[0]
    if trans_a:
        kdim = a0.shape[1] if a_lbm else a0.shape[0]
        m = a0.shape[0] * LANES if a_lbm else a0.shape[1]
    else:
        m = a0.shape[1] if a_lbm else a0.shape[0]
        kdim = a0.shape[0] * LANES if a_lbm else a0.shape[1]
    n = b0.shape[0] * LANES if b_lbm else b0.shape[1]
    tm = _pick(m, (tm, 1408, 256, 128))
    tn = tn or _pick(n, (1408, 1024, 768, 512, 256, 128))
    tk = tk or _pick(kdim, (1024, 1408, 512, 256, 128))
    nk = kdim // tk
    npair = len(pairs)

    def cat(ref):
        return jnp.concatenate([ref[p] for p in range(ref.shape[0])], axis=-1)

    def body(*refs):
        ins = refs[: 2 * npair]
        pos = 2 * npair
        add_ref = None
        if addend is not None:
            add_ref = refs[pos]
            pos += 1
        gain_ref = x_ref = dxin_ref = None
        if norm_gain is not None:
            gain_ref = refs[pos]
            pos += 1
        if norm_bwd is not None:
            x_ref, gain_ref, dxin_ref = refs[pos:pos + 3]
            pos += 3
        if loss_head is not None:
            gain_ref, target_ref = refs[pos:pos + 2]
            pos += 2
        o_ref = refs[pos]
        extra_out = refs[pos + 1:-1]
        acc_ref = refs[-1]
        k = pl.program_id(2)

        @pl.when(k == 0)
        def _():
            acc_ref[...] = jnp.zeros_like(acc_ref)

        if norm_bwd is not None or loss_head is not None:
            @pl.when((k == 0) & (pl.program_id(0) == 0))
            def _():
                for ref in extra_out[1:]:
                    ref[...] = jnp.zeros_like(ref)

        acc = acc_ref[...]
        for p in range(npair):
            a = (cat(ins[2 * p]) if a_lbm else ins[2 * p][...]).astype(CD)
            b = (cat(ins[2 * p + 1]) if b_lbm else ins[2 * p + 1][...]).astype(CD)
            dims = (((0,), (0,)), ((), ())) if trans_a else (((1,), (0,)), ((), ()))
            acc = acc + lax.dot_general(a, b, dims, preferred_element_type=F32)
        acc_ref[...] = acc

        @pl.when(k == nk - 1)
        def _():
            res = acc_ref[...]
            if add_ref is not None:
                res = res + add_ref[...]
            if norm_gain is not None:
                rinv = lax.rsqrt(jnp.mean(res * res, axis=-1, keepdims=True) + RMS_EPS)
                extra_out[0][...] = (res * rinv * gain_ref[...]).astype(CD)
            if norm_bwd is not None:
                xv = x_ref[...]
                rinv = lax.rsqrt(jnp.mean(xv * xv, axis=-1, keepdims=True) + RMS_EPS)
                nrm = xv * rinv
                dn = res * gain_ref[...]
                extra_out[1][...] += jnp.sum(res * nrm, axis=0, keepdims=True)
                res = dxin_ref[...] + rinv * (dn - nrm * jnp.mean(dn * nrm, axis=-1, keepdims=True))
                extra_out[0][...] = res.astype(CD)
            if loss_head is not None:
                gv = gain_ref[...]
                rinv = lax.rsqrt(jnp.mean(res * res, axis=-1, keepdims=True) + RMS_EPS)
                nrm = res * rinv
                err = nrm * gv - target_ref[...]
                extra_out[2][...] += 0.5 * jnp.sum(jnp.mean(err * err, axis=-1, keepdims=True), axis=0, keepdims=True)
                dy = err * (1.0 / n)
                dn = dy * gv
                extra_out[1][...] += jnp.sum(dy * nrm, axis=0, keepdims=True)
                res = rinv * (dn - nrm * jnp.mean(dn * nrm, axis=-1, keepdims=True))
                extra_out[0][...] = res.astype(CD)
            res = res.astype(out_dtype)
            if out_lbm:
                for p in range(tn // LANES):
                    o_ref[p] = res[:, p * LANES:(p + 1) * LANES]
            else:
                o_ref[...] = res

    if trans_a:
        a_spec = (pl.BlockSpec((tm // LANES, tk, LANES), lambda i, j, k: (i, k, 0)) if a_lbm
                  else pl.BlockSpec((tk, tm), lambda i, j, k: (k, i)))
    else:
        a_spec = (pl.BlockSpec((tk // LANES, tm, LANES), lambda i, j, k: (k, i, 0)) if a_lbm
                  else pl.BlockSpec((tm, tk), lambda i, j, k: (i, k)))
    b_spec = (pl.BlockSpec((tn // LANES, tk, LANES), lambda i, j, k: (j, k, 0)) if b_lbm
              else pl.BlockSpec((tk, tn), lambda i, j, k: (k, j)))
    in_specs = []
    args = []
    for a, b in pairs:
        in_specs += [a_spec, b_spec]
        args += [a, b]
    if addend is not None:
        in_specs.append(pl.BlockSpec((tm, tn), lambda i, j, k: (i, j)))
        args.append(addend)
    tile = pl.BlockSpec((tm, tn), lambda i, j, k: (i, j))
    vec = pl.BlockSpec((1, tn), lambda i, j, k: (0, j))
    if out_lbm:
        out_shape = jax.ShapeDtypeStruct((n // LANES, m, LANES), out_dtype)
        out_spec = pl.BlockSpec((tn // LANES, tm, LANES), lambda i, j, k: (j, i, 0))
    else:
        out_shape = jax.ShapeDtypeStruct((m, n), out_dtype)
        out_spec = tile
    sem = ("parallel", "parallel", "arbitrary")
    if norm_gain is not None or norm_bwd is not None or loss_head is not None:
        assert tn == n and not out_lbm, "the norm needs whole rows in one tile"
        out_shape, out_spec = [out_shape, jax.ShapeDtypeStruct((m, n), CD)], [out_spec, tile]
    if norm_gain is not None:
        in_specs.append(vec)
        args.append(norm_gain.reshape(1, n))
    if norm_bwd is not None:
        x_in, gain, dx_in = norm_bwd
        in_specs += [tile, vec, tile]
        args += [x_in, gain.reshape(1, n), dx_in]
        out_shape.append(jax.ShapeDtypeStruct((1, n), F32))
        out_spec.append(vec)
        sem = ("arbitrary", "arbitrary", "arbitrary")
    if loss_head is not None:
        gain, target = loss_head
        in_specs += [vec, tile]
        args += [gain.reshape(1, n), target]
        out_shape += [jax.ShapeDtypeStruct((1, n), F32), jax.ShapeDtypeStruct((1, LANES), F32)]
        out_spec += [vec, pl.BlockSpec((1, LANES), lambda i, j, k: (0, 0))]
        sem = ("arbitrary", "arbitrary", "arbitrary")
    return pl.pallas_call(
        body, name=name, out_shape=out_shape, grid=(m // tm, n // tn, nk),
        in_specs=in_specs, out_specs=out_spec,
        scratch_shapes=[pltpu.VMEM((tm, tn), F32)],
        compiler_params=_params(*sem),
    )(*args)


def _norm_and_project(x, g, w_a, w_b, name):
    s, d = x.shape
    n = w_a.shape[1]
    tm = _pick(s, (512, 256))

    def body(x_ref, g_ref, wa_ref, wb_ref, h_ref, a_ref, b_ref):
        xv = x_ref[...]
        rinv = lax.rsqrt(jnp.mean(xv * xv, axis=-1, keepdims=True) + RMS_EPS)
        h = (xv * rinv * g_ref[...]).astype(CD)
        h_ref[...] = h
        a_ref[...] = jnp.dot(h, wa_ref[...], preferred_element_type=F32)
        b_ref[...] = jnp.dot(h, wb_ref[...], preferred_element_type=F32)

    row = pl.BlockSpec((tm, d), lambda i: (i, 0))
    out = pl.BlockSpec((tm, n), lambda i: (i, 0))
    wspec = pl.BlockSpec((d, n), lambda i: (0, 0))
    return pl.pallas_call(
        body, name=name,
        out_shape=(jax.ShapeDtypeStruct((s, d), CD), jax.ShapeDtypeStruct((s, n), F32),
                   jax.ShapeDtypeStruct((s, n), F32)),
        grid=(s // tm,), in_specs=[row, pl.BlockSpec((1, d), lambda i: (0, 0)), wspec, wspec],
        out_specs=(row, out, out), compiler_params=_params("parallel"),
    )(x, g.reshape(1, d), w_a, w_b)


def _sigmoid(z):
    return 1.0 / (1.0 + jnp.exp(-z))


FFN_TM = 512
FFN_TN = 1408


def _ffn_up(h, wg, wu, name):
    s, d = h.shape
    f = wg.shape[1]
    tm = _pick(s, (FFN_TM, 256))

    def body(h_ref, wg_ref, wu_ref, g_ref, u_ref, a_ref):
        hv = h_ref[...]
        gv = jnp.dot(hv, wg_ref[...], preferred_element_type=F32)
        uv = jnp.dot(hv, wu_ref[...], preferred_element_type=F32)
        g_ref[...] = gv
        u_ref[...] = uv
        a_ref[...] = (gv * _sigmoid(gv) * uv).astype(CD)

    a_spec = pl.BlockSpec((tm, d), lambda i, j: (i, 0))
    w_spec = pl.BlockSpec((d, FFN_TN), lambda i, j: (0, j))
    o_spec = pl.BlockSpec((tm, FFN_TN), lambda i, j: (i, j))
    return pl.pallas_call(
        body, name=name,
        out_shape=(jax.ShapeDtypeStruct((s, f), F32), jax.ShapeDtypeStruct((s, f), F32),
                   jax.ShapeDtypeStruct((s, f), CD)),
        grid=(s // tm, f // FFN_TN), in_specs=[a_spec, w_spec, w_spec], out_specs=(o_spec, o_spec, o_spec),
        compiler_params=_params("parallel", "parallel"),
    )(h, wg, wu)


def _ffn_dact(dxc, wd_t, g, u, name):
    s, d = dxc.shape
    f = wd_t.shape[1]
    tm = _pick(s, (FFN_TM, 256))

    def body(dx_ref, w_ref, g_ref, u_ref, dg_ref, du_ref):
        da = jnp.dot(dx_ref[...], w_ref[...], preferred_element_type=F32)
        gv = g_ref[...]
        sg = _sigmoid(gv)
        silu = gv * sg
        dg_ref[...] = (da * u_ref[...] * (sg + silu * (1.0 - sg))).astype(CD)
        du_ref[...] = (da * silu).astype(CD)

    a_spec = pl.BlockSpec((tm, d), lambda i, j: (i, 0))
    w_spec = pl.BlockSpec((d, FFN_TN), lambda i, j: (0, j))
    o_spec = pl.BlockSpec((tm, FFN_TN), lambda i, j: (i, j))
    return pl.pallas_call(
        body, name=name,
        out_shape=(jax.ShapeDtypeStruct((s, f), CD), jax.ShapeDtypeStruct((s, f), CD)),
        grid=(s // tm, f // FFN_TN), in_specs=[a_spec, w_spec, o_spec, o_spec], out_specs=(o_spec, o_spec),
        compiler_params=_params("parallel", "parallel"),
    )(dxc, wd_t, g, u)


TIME_BLOCK = 1024
SUBLANES = 8
GELU_C = math.sqrt(2.0 / math.pi)
GELU_A = 0.044715


def _gelu(x):
    return 0.5 * x * (1.0 + jnp.tanh(GELU_C * (x + GELU_A * x * x * x)))


def _gelu_grad(x):
    t = jnp.tanh(GELU_C * (x + GELU_A * x * x * x))
    return 0.5 * (1.0 + t) + 0.5 * x * (1.0 - t * t) * GELU_C * (1.0 + 3.0 * GELU_A * x * x)


def _neg_expm1(x):
    series = -x * (1.0 + x * (0.5 + x * (1.0 / 6.0 + x * (1.0 / 24.0))))
    return jnp.where(x > -0.05, series, 1.0 - jnp.exp(x))


def _log_sigmoid(x):
    return jnp.minimum(x, 0.0) - jnp.log1p(jnp.exp(-jnp.abs(x)))


def _shift_down(x, tail, s):
    if s == 0:
        return x
    ext = jnp.concatenate([tail, x], axis=0)
    return pltpu.roll(ext, s, axis=0)[SUBLANES:]


def _shift_up(x, head, s):
    if s == 0:
        return x
    n = x.shape[0]
    ext = jnp.concatenate([x, head], axis=0)
    return pltpu.roll(ext, n + SUBLANES - s, axis=0)[:n]


def _rg_gates(xbr, tail, cw_ref, cb, wr, wi, br, bi, ls):
    taps = [_shift_down(xbr, tail, CONV_W - 1 - k) for k in range(CONV_W)]
    xc = cb
    for k in range(CONV_W):
        xc = xc + cw_ref[pl.ds(k, 1), :] * taps[k]
    xcd = xc.astype(CD)
    r = _sigmoid(jnp.dot(xcd, wr, preferred_element_type=F32) + br)
    i = _sigmoid(jnp.dot(xcd, wi, preferred_element_type=F32) + bi)
    log_a = RG_C * r * ls
    a = jnp.exp(log_a)
    mult = jnp.sqrt(jnp.maximum(_neg_expm1(2.0 * log_a), 0.0))
    return taps, xc, r, i, log_a, a, mult


def _scan8_fwd(a, u):
    row = lax.broadcasted_iota(jnp.int32, a.shape, 0)
    for d in (1, 2, 4):
        a_s = pltpu.roll(a, d, axis=0)
        u_s = pltpu.roll(u, d, axis=0)
        m = row >= d
        u = jnp.where(m, a * u_s + u, u)
        a = jnp.where(m, a * a_s, a)
    return a, u


def _scan8_bwd(b, u):
    row = lax.broadcasted_iota(jnp.int32, b.shape, 0)
    for d in (1, 2, 4):
        b_s = pltpu.roll(b, SUBLANES - d, axis=0)
        u_s = pltpu.roll(u, SUBLANES - d, axis=0)
        m = row < SUBLANES - d
        u = jnp.where(m, b * u_s + u, u)
        b = jnp.where(m, b * b_s, b)
    return b, u


def _rglru_fwd(gate_br, x_br, cw, cb, wr, wi, br, bi, lam, name):
    s, c = x_br.shape
    nt = s // TIME_BLOCK
    tb, cbw = TIME_BLOCK, RG_BW
    groups = tb // SUBLANES

    def body(g_ref, x_ref, tail_ref, cw_ref, cb_ref, wr_ref, wi_ref, br_ref, bi_ref, lam_ref,
             y_ref, hs_ref, carry_ref, a_scr, u_scr):
        t = pl.program_id(1)

        @pl.when(t == 0)
        def _():
            carry_ref[...] = jnp.zeros_like(carry_ref)

        tail = jnp.where(t > 0, tail_ref[...], 0.0)
        ls = _log_sigmoid(lam_ref[...])
        _, xc, _, i, _, a, mult = _rg_gates(x_ref[...], tail, cw_ref, cb_ref[...], wr_ref[0], wi_ref[0],
                                            br_ref[...], bi_ref[...], ls)
        a_scr[...] = a
        u_scr[...] = mult * (i * xc)
        carry = carry_ref[...]
        for gi in range(groups):
            rows = pl.ds(gi * SUBLANES, SUBLANES)
            pa, hl = _scan8_fwd(a_scr[rows, :], u_scr[rows, :])
            hs_ref[rows, :] = hl + pa * carry
            carry = hs_ref[pl.ds(gi * SUBLANES + SUBLANES - 1, 1), :]
        carry_ref[...] = carry
        y_ref[...] = (hs_ref[...] * _gelu(g_ref[...])).astype(CD)

    blk = pl.BlockSpec((tb, cbw), lambda n, t: (t, n))
    tail = pl.BlockSpec((SUBLANES, cbw), lambda n, t: (jnp.maximum(t * groups - 1, 0), n))
    vec = pl.BlockSpec((1, cbw), lambda n, t: (0, n))
    wblk = pl.BlockSpec((1, cbw, cbw), lambda n, t: (n, 0, 0))
    return pl.pallas_call(
        body, name=name,
        out_shape=(jax.ShapeDtypeStruct((s, c), CD), jax.ShapeDtypeStruct((s, c), F32)),
        grid=(RG_BLOCKS, nt),
        in_specs=[blk, blk, tail, pl.BlockSpec((CONV_W, cbw), lambda n, t: (0, n)), vec, wblk, wblk, vec, vec, vec],
        out_specs=(blk, blk),
        scratch_shapes=[pltpu.VMEM((1, cbw), F32), pltpu.VMEM((tb, cbw), F32), pltpu.VMEM((tb, cbw), F32)],
        compiler_params=_params("parallel", "arbitrary"),
    )(gate_br, x_br, x_br, cw, cb, wr, wi, br, bi, lam)


def _rglru_bwd(dy, gate_br, x_br, hs, cw, cb, wr, wi, wrt, wit, br, bi, lam, name):
    s, c = x_br.shape
    nt = s // TIME_BLOCK
    tb, cbw = TIME_BLOCK, RG_BW
    groups = tb // SUBLANES

    def body(dy_ref, g_ref, x_ref, tail_ref, hs_ref, hprev_ref, cw_ref, cb_ref, wr_ref, wi_ref, wrt_ref, wit_ref,
             br_ref, bi_ref, lam_ref,
             dg_ref, dx_ref, dcw_ref, dcb_ref, dbr_ref, dbi_ref, dlam_ref, dwr_ref, dwi_ref,
             carry_ref, head_ref, b_scr, u_scr, dh_scr):
        tr = pl.program_id(1)
        first_block = tr == nt - 1

        @pl.when(tr == 0)
        def _():
            carry_ref[...] = jnp.zeros_like(carry_ref)
            head_ref[...] = jnp.zeros_like(head_ref)
            for ref in (dcw_ref, dcb_ref, dbr_ref, dbi_ref, dlam_ref, dwr_ref, dwi_ref):
                ref[...] = jnp.zeros_like(ref)

        tail = jnp.where(first_block, 0.0, tail_ref[...])
        lam_v = lam_ref[...]
        ls = _log_sigmoid(lam_v)
        taps, xc, r, i, log_a, a, mult = _rg_gates(x_ref[...], tail, cw_ref, cb_ref[...], wr_ref[0], wi_ref[0],
                                                   br_ref[...], bi_ref[...], ls)
        gate_v = g_ref[...]
        dyv = dy_ref[...]
        hsv = hs_ref[...]
        dg_ref[...] = (dyv * hsv * _gelu_grad(gate_v)).astype(CD)

        row = lax.broadcasted_iota(jnp.int32, a.shape, 0)
        b_scr[...] = jnp.where(row == tb - 1, 1.0, pltpu.roll(a, tb - 1, axis=0))
        u_scr[...] = dyv * _gelu(gate_v)
        carry = carry_ref[...]
        for gi in reversed(range(groups)):
            rows = pl.ds(gi * SUBLANES, SUBLANES)
            pb, gl = _scan8_bwd(b_scr[rows, :], u_scr[rows, :])
            dh_scr[rows, :] = gl + pb * carry
            carry = dh_scr[pl.ds(gi * SUBLANES, 1), :]
        dh = dh_scr[...]
        carry_ref[...] = carry * jnp.sum(jnp.where(row == 0, a, 0.0), axis=0, keepdims=True)

        hprev_tail = jnp.where(first_block, 0.0, hprev_ref[...])
        h_prev = _shift_down(hsv, hprev_tail, 1)
        da = dh * h_prev
        ixc = i * xc
        dmult = dh * ixc
        di = dh * mult * xc
        dxc = dh * mult * i
        a2 = a * a
        dlog_a = da * a - dmult * a2 / mult
        dpre_r = (dlog_a * (RG_C * ls)) * r * (1.0 - r)
        dpre_i = di * i * (1.0 - i)
        dlam_ref[...] += jnp.sum(dlog_a * r, axis=0, keepdims=True) * (RG_C * _sigmoid(-lam_v))
        dbr_ref[...] += jnp.sum(dpre_r, axis=0, keepdims=True)
        dbi_ref[...] += jnp.sum(dpre_i, axis=0, keepdims=True)
        xcd = xc.astype(CD)
        dprc = dpre_r.astype(CD)
        dpic = dpre_i.astype(CD)
        tn_dims = (((0,), (0,)), ((), ()))
        dwr_ref[0] += lax.dot_general(xcd, dprc, tn_dims, preferred_element_type=F32)
        dwi_ref[0] += lax.dot_general(xcd, dpic, tn_dims, preferred_element_type=F32)
        dxc = dxc + jnp.dot(dprc, wrt_ref[0], preferred_element_type=F32) + jnp.dot(dpic, wit_ref[0],
                                                                                    preferred_element_type=F32)
        dcb_ref[...] += jnp.sum(dxc, axis=0, keepdims=True)
        for k in range(CONV_W):
            dcw_ref[pl.ds(k, 1), :] += jnp.sum(dxc * taps[k], axis=0, keepdims=True)
        head = head_ref[...]
        dxb = jnp.zeros_like(dxc)
        for sft in range(CONV_W):
            dxb = dxb + cw_ref[pl.ds(CONV_W - 1 - sft, 1), :] * _shift_up(dxc, head, sft)
        dx_ref[...] = dxb.astype(CD)
        head_ref[...] = dxc[0:SUBLANES, :]

    blk = pl.BlockSpec((tb, cbw), lambda n, t: (nt - 1 - t, n))
    tail = pl.BlockSpec((SUBLANES, cbw), lambda n, t: (jnp.maximum((nt - 1 - t) * groups - 1, 0), n))
    vec = pl.BlockSpec((1, cbw), lambda n, t: (0, n))
    cwb = pl.BlockSpec((CONV_W, cbw), lambda n, t: (0, n))
    wblk = pl.BlockSpec((1, cbw, cbw), lambda n, t: (n, 0, 0))
    vshape = jax.ShapeDtypeStruct((1, c), F32)
    wshape = jax.ShapeDtypeStruct((RG_BLOCKS, cbw, cbw), F32)
    return pl.pallas_call(
        body, name=name,
        out_shape=(jax.ShapeDtypeStruct((s, c), CD), jax.ShapeDtypeStruct((s, c), CD),
                   jax.ShapeDtypeStruct((CONV_W, c), F32), vshape, vshape, vshape, vshape, wshape, wshape),
        grid=(RG_BLOCKS, nt),
        in_specs=[blk, blk, blk, tail, blk, tail, cwb, vec, wblk, wblk, wblk, wblk, vec, vec, vec],
        out_specs=(blk, blk, cwb, vec, vec, vec, vec, wblk, wblk),
        scratch_shapes=[pltpu.VMEM((1, cbw), F32), pltpu.VMEM((SUBLANES, cbw), F32),
                        pltpu.VMEM((tb, cbw), F32), pltpu.VMEM((tb, cbw), F32), pltpu.VMEM((tb, cbw), F32)],
        compiler_params=_params("parallel", "arbitrary"),
    )(dy, gate_br, x_br, x_br, hs, hs, cw, cb, wr, wi, wrt, wit, br, bi, lam)


ATT_BLOCK = 256
ATT_Q_BLOCK = 1024
ATT_RATIO = ATT_Q_BLOCK // ATT_BLOCK
ATT_SCALE = 1.0 / math.sqrt(SB_HEAD_DIM)
N_PAIRS = SB_HEADS * SB_HEAD_DIM // LANES
NT_DIMS = (((1,), (1,)), ((), ()))
TN_DIMS = (((0,), (0,)), ((), ()))


LOG2E = 1.4426950408889634


def _neg_abs(x):
    bits = lax.bitcast_convert_type(x, jnp.uint32) | jnp.uint32(0x80000000)
    return lax.bitcast_convert_type(bits, F32)


def _qk(qx, kb):
    return lax.dot_general(qx, kb, NT_DIMS, preferred_element_type=F32)


def _sb_logits(qk, valid):
    z2 = qk * (ATT_SCALE * LOG2E)
    lb2 = jnp.minimum(z2, 0.0) - jnp.log2(1.0 + jnp.exp2(_neg_abs(z2)))
    l2 = lb2 - z2
    if valid is not None:
        l2 = jnp.where(valid, l2, 0.0)
    return lb2, l2


def _hi_lo(x):
    hi = x.astype(CD)
    lo = (x - hi.astype(F32)).astype(CD)
    return jnp.concatenate([hi, lo], axis=1)


def _tri(strict, stacked):
    r = lax.broadcasted_iota(jnp.int32, (ATT_BLOCK, ATT_BLOCK), 0)
    c = lax.broadcasted_iota(jnp.int32, (ATT_BLOCK, ATT_BLOCK), 1)
    m = (r > c if strict else r >= c).astype(CD)
    return jnp.concatenate([m, m], axis=0) if stacked else m


def _attn_fwd(qkv, name):
    _, s, _ = qkv.shape
    tq, t = ATT_Q_BLOCK, ATT_BLOCK
    nblk = s // tq

    def body(q_ref, k_ref, v_ref, o_ref, qk_scr, w_scr):
        i = pl.program_id(1)
        lane = lax.broadcasted_iota(jnp.int32, (1, LANES), 1)
        head_masks = (lane < SB_HEAD_DIM, lane >= SB_HEAD_DIM)
        q = q_ref[0]
        qs = [jnp.where(m, q, jnp.zeros_like(q)) for m in head_masks]
        tri = _tri(True, False)
        rr = lax.broadcasted_iota(jnp.int32, (tq, t), 0)
        cc = lax.broadcasted_iota(jnp.int32, (tq, t), 1)

        def rows_of(j):
            return pl.ds(pl.multiple_of(j * t, t), t)

        def tail(x, row0):
            return x if row0 == 0 else x[row0:]

        def start_logits(j, row0=0):
            kb = k_ref[0, rows_of(j), :]
            for hd in range(2):
                qk_scr[hd, row0:, :] = _qk(tail(qs[hd], row0), kb)

        def weights(run, diagonal=False, row0=0):
            new_run = []
            valid = (cc < rr)[:tq - row0] if diagonal else None
            for hd in range(2):
                lb2, l2 = _sb_logits(qk_scr[hd, row0:, :], valid)
                w = jnp.exp2(lb2 + (tail(run[hd], row0) + jnp.dot(l2.astype(CD), tri, preferred_element_type=F32)))
                if valid is not None:
                    w = jnp.where(valid, w, 0.0)
                w_scr[row0:, hd * t:(hd + 1) * t] = w.astype(CD)
                rowsum = jnp.sum(l2, axis=1, keepdims=True)
                if row0:
                    rowsum = jnp.concatenate([jnp.zeros((row0, 1), F32), rowsum], axis=0)
                new_run.append(run[hd] + rowsum)
            return tuple(new_run)

        def apply_weights(j, row0=0):
            vb = v_ref[0, rows_of(j), :]
            vcat = jnp.concatenate([jnp.where(m, vb, jnp.zeros_like(vb)) for m in head_masks], axis=0)
            inc = jnp.dot(w_scr[row0:, :], vcat, preferred_element_type=F32)
            return inc if row0 == 0 else jnp.concatenate([jnp.zeros((row0, LANES), F32), inc], axis=0)

        zero = jnp.zeros((tq, 1), F32)
        last = ATT_RATIO - 1
        start_logits(ATT_RATIO * i + last, last * t)
        run = weights((zero, zero), True, last * t)
        oacc = jnp.zeros((tq, LANES), F32)
        for d in reversed(range(last)):
            start_logits(ATT_RATIO * i + d, d * t)
            oacc = oacc + apply_weights(ATT_RATIO * i + d + 1, (d + 1) * t)
            run = weights(run, True, d * t)
        start_logits(jnp.maximum(ATT_RATIO * i - 1, 0))

        def step(jj, carry):
            run, oacc = carry
            b = ATT_RATIO * i - 1 - jj
            oacc = oacc + apply_weights(b + 1)
            run = weights(run)
            start_logits(jnp.maximum(b - 1, 0))
            return run, oacc

        run, oacc = lax.fori_loop(0, ATT_RATIO * i, step, (run, oacc))
        o_ref[0] = oacc + apply_weights(0)

    return pl.pallas_call(
        body, name=name, out_shape=jax.ShapeDtypeStruct((N_PAIRS, s, LANES), F32), grid=(N_PAIRS, nblk),
        in_specs=[pl.BlockSpec((1, tq, LANES), lambda p, i: (p, i, 0)),
                  pl.BlockSpec((1, s, LANES), lambda p, i: (N_PAIRS + p, 0, 0)),
                  pl.BlockSpec((1, s, LANES), lambda p, i: (2 * N_PAIRS + p, 0, 0))],
        out_specs=pl.BlockSpec((1, tq, LANES), lambda p, i: (p, i, 0)),
        scratch_shapes=[pltpu.VMEM((2, tq, t), F32), pltpu.VMEM((tq, 2 * t), CD)],
        compiler_params=_params("parallel", "arbitrary"),
    )(qkv, qkv, qkv)


def _attn_bwd(qkv, o, do, name):
    _, s, _ = qkv.shape
    tq, t = ATT_Q_BLOCK, ATT_BLOCK
    nblk = s // tq

    def body(q_ref, k_ref, v_ref, o_ref, do_ref, dq_ref, dk_ref, dv_ref, qk_scr, dw_scr, w_scr, dz_scr):
        i = pl.program_id(1)

        @pl.when(i == 0)
        def _():
            dk_ref[...] = jnp.zeros_like(dk_ref)
            dv_ref[...] = jnp.zeros_like(dv_ref)

        lane = lax.broadcasted_iota(jnp.int32, (1, LANES), 1)
        head_masks = (lane < SB_HEAD_DIM, lane >= SB_HEAD_DIM)
        q = q_ref[0]
        dov = do_ref[0]
        ov = o_ref[0]
        qs = [jnp.where(m, q, jnp.zeros_like(q)) for m in head_masks]
        q_scaled_t = jnp.concatenate([(qx.astype(F32) * ATT_SCALE).T for qx in qs], axis=1).astype(CD)
        docs = [jnp.where(m, dov, 0.0).astype(CD) for m in head_masks]
        docat_t = jnp.concatenate([jnp.where(m, dov, 0.0).T for m in head_masks], axis=1).astype(CD)
        totals = [jnp.sum(d.astype(F32) * ov, axis=1, keepdims=True) for d in docs]
        tri = _tri(True, False)
        tri_incl = _tri(False, True)
        rr = lax.broadcasted_iota(jnp.int32, (tq, t), 0)
        cc = lax.broadcasted_iota(jnp.int32, (tq, t), 1)

        def rows_of(j):
            return pl.ds(pl.multiple_of(j * t, t), t)

        def tail(x, row0):
            return x if row0 == 0 else x[row0:]

        def pad_rows(x, row0):
            return x if row0 == 0 else jnp.concatenate([jnp.zeros((row0, x.shape[1]), x.dtype), x], axis=0)

        def start_products(j, row0=0):
            kb = k_ref[0, rows_of(j), :]
            vb = v_ref[0, rows_of(j), :]
            for hd in range(2):
                qk_scr[hd, row0:, :] = _qk(tail(qs[hd], row0), kb)
                dw_scr[hd, row0:, :] = lax.dot_general(tail(docs[hd], row0), vb, NT_DIMS, preferred_element_type=F32)

        def logit_grads(run, erun, diagonal=False, row0=0):
            new_run, new_erun = [], []
            valid = (cc < rr)[:tq - row0] if diagonal else None
            for hd in range(2):
                lb2, l2 = _sb_logits(qk_scr[hd, row0:, :], valid)
                w = jnp.exp2(lb2 + (tail(run[hd], row0) + jnp.dot(l2.astype(CD), tri, preferred_element_type=F32)))
                if valid is not None:
                    w = jnp.where(valid, w, 0.0)
                wc = w.astype(CD)
                w_scr[hd * tq + row0:(hd + 1) * tq, :] = wc
                e = dw_scr[hd, row0:, :] * wc.astype(F32)
                prefix = (tail(totals[hd] - erun[hd], row0)
                          - jnp.dot(_hi_lo(e), tri_incl, preferred_element_type=F32))
                dz = e - jnp.exp2(lb2) * (e + prefix)
                if valid is not None:
                    dz = jnp.where(valid, dz, 0.0)
                dz_scr[hd * tq + row0:(hd + 1) * tq, :] = dz.astype(CD)
                new_run.append(run[hd] + pad_rows(jnp.sum(l2, axis=1, keepdims=True), row0))
                new_erun.append(erun[hd] + pad_rows(jnp.sum(e, axis=1, keepdims=True), row0))
            return tuple(new_run), tuple(new_erun)

        def apply_grads(j, row0=0):
            rows = rows_of(j)
            kb = k_ref[0, rows, :]
            kcat = jnp.concatenate([jnp.where(m, kb, jnp.zeros_like(kb)) for m in head_masks], axis=0)
            dz_heads = [dz_scr[hd * tq + row0:(hd + 1) * tq, :] for hd in range(2)]
            w_heads = [w_scr[hd * tq + row0:(hd + 1) * tq, :] for hd in range(2)]
            q_t = jnp.concatenate([q_scaled_t[:, hd * tq + row0:(hd + 1) * tq] for hd in range(2)], axis=1)
            do_t = jnp.concatenate([docat_t[:, hd * tq + row0:(hd + 1) * tq] for hd in range(2)], axis=1)
            dk_ref[0, rows, :] += jnp.dot(q_t, jnp.concatenate(dz_heads, axis=0), preferred_element_type=F32).T
            dv_ref[0, rows, :] += jnp.dot(do_t, jnp.concatenate(w_heads, axis=0), preferred_element_type=F32).T
            return pad_rows(jnp.dot(jnp.concatenate(dz_heads, axis=1), kcat, preferred_element_type=F32), row0)

        zero = jnp.zeros((tq, 1), F32)
        last = ATT_RATIO - 1
        start_products(ATT_RATIO * i + last, last * t)
        run, erun = logit_grads((zero, zero), (zero, zero), True, last * t)
        dqacc = jnp.zeros((tq, LANES), F32)
        for d in reversed(range(last)):
            start_products(ATT_RATIO * i + d, d * t)
            dqacc = dqacc + apply_grads(ATT_RATIO * i + d + 1, (d + 1) * t)
            run, erun = logit_grads(run, erun, True, d * t)
        start_products(jnp.maximum(ATT_RATIO * i - 1, 0))

        def step(jj, carry):
            run, erun, dqacc = carry
            b = ATT_RATIO * i - 1 - jj
            dqacc = dqacc + apply_grads(b + 1)
            run, erun = logit_grads(run, erun)
            start_products(jnp.maximum(b - 1, 0))
            return run, erun, dqacc

        run, erun, dqacc = lax.fori_loop(0, ATT_RATIO * i, step, (run, erun, dqacc))
        dq_ref[0] = (dqacc + apply_grads(0)) * ATT_SCALE

    qblk = pl.BlockSpec((1, tq, LANES), lambda p, i: (p, i, 0))
    full = pl.BlockSpec((1, s, LANES), lambda p, i: (p, 0, 0))
    shape = jax.ShapeDtypeStruct((N_PAIRS, s, LANES), F32)
    return pl.pallas_call(
        body, name=name, out_shape=(shape, shape, shape), grid=(N_PAIRS, nblk),
        in_specs=[qblk,
                  pl.BlockSpec((1, s, LANES), lambda p, i: (N_PAIRS + p, 0, 0)),
                  pl.BlockSpec((1, s, LANES), lambda p, i: (2 * N_PAIRS + p, 0, 0)),
                  qblk, qblk],
        out_specs=(qblk, full, full),
        scratch_shapes=[pltpu.VMEM((2, tq, t), F32), pltpu.VMEM((2, tq, t), F32),
                        pltpu.VMEM((2 * tq, t), CD), pltpu.VMEM((2 * tq, t), CD)],
        compiler_params=_params("parallel", "arbitrary"),
    )(qkv, qkv, qkv, o, do)


def _adamw(w, g, m, v, name):
    shape = w.shape
    rows, cols = (shape[-2], shape[-1]) if len(shape) >= 2 else (1, shape[-1])
    lead = w.size // (rows * cols)
    tr = _pick(rows, (512, 256, 128, 64, 32, 16, 8))

    def body(w_ref, g_ref, m_ref, v_ref, d_ref, nm_ref, nv_ref):
        gv = g_ref[...]
        nm = ADAM_B1 * m_ref[...] + (1.0 - ADAM_B1) * gv
        nv = ADAM_B2 * v_ref[...] + (1.0 - ADAM_B2) * (gv * gv)
        m_hat = nm / (1.0 - ADAM_B1 ** ADAM_STEP)
        v_hat = nv / (1.0 - ADAM_B2 ** ADAM_STEP)
        d_ref[...] = -ADAM_LR * (m_hat / (jnp.sqrt(v_hat) + ADAM_EPS) + ADAM_WD * w_ref[...])
        nm_ref[...] = nm
        nv_ref[...] = nv

    blk = pl.BlockSpec((1, tr, cols), lambda l, i: (l, i, 0))
    out = jax.ShapeDtypeStruct((lead, rows, cols), F32)
    d, nm, nv = pl.pallas_call(
        body, name=name, out_shape=(out, out, out), grid=(lead, rows // tr),
        in_specs=[blk, blk, blk, blk], out_specs=(blk, blk, blk), compiler_params=_params("parallel", "parallel"),
    )(*[a.reshape(lead, rows, cols) for a in (w, g, m, v)])
    return d.reshape(shape), nm.reshape(shape), nv.reshape(shape)


HBM = pl.BlockSpec(memory_space=pltpu.HBM)


def _coords():
    return lax.axis_index("x"), lax.axis_index("y"), lax.axis_index("c")


def _other_chips(x, y):
    return [(1 - x, y), (x, 1 - y), (1 - x, 1 - y)]


def _allgather_chips(shard, name, collective_id=None):
    r, cols = shard.shape
    half = r // 2
    quarter = half // 2

    def body(src_ref, out_ref, send_sems, recv_sems):
        x, y, c = _coords()
        sibling = (x, y, 1 - c)
        nx, ny, diag = (1 - x, y), (x, 1 - y), (1 - x, 1 - y)

        def piece(chip, core, lo, n):
            return out_ref.at[2 * chip[0] + chip[1], pl.ds(core * half + lo, n), :]

        def copy(k, dst, to, src=None):
            return pltpu.make_async_remote_copy(
                src_ref=dst if src is None else src, dst_ref=dst,
                send_sem=send_sems.at[k], recv_sem=recv_sems.at[k], device_id=to, device_id_type=MESH)

        me = (x, y)
        mine = src_ref.at[pl.ds(c * half, half), :]
        direct = [copy(0, piece(me, c, 0, half), (*nx, c), src=mine), copy(1, piece(me, c, 0, half), (*ny, c), src=mine)]
        for cp in direct:
            cp.start()
        arrivals = [piece(nx, c, 0, half), piece(ny, c, 0, half), piece(diag, c, 0, quarter),
                    piece(diag, c, quarter, quarter)]
        onward = [copy(2, piece(nx, c, 0, quarter), (*ny, c)), copy(3, piece(ny, c, quarter, quarter), (*nx, c))]
        to_sibling = [copy(4 + k, dst, sibling) for k, dst in enumerate(arrivals)]
        for k, dst in enumerate(arrivals):
            copy(k, dst, (x, y, c)).wait_recv()
            if k < 2:
                onward[k].start()
            to_sibling[k].start()
        from_sibling = [piece(nx, 1 - c, 0, half), piece(ny, 1 - c, 0, half), piece(diag, 1 - c, 0, quarter),
                        piece(diag, 1 - c, quarter, quarter)]
        for k, dst in enumerate(from_sibling):
            copy(4 + k, dst, (x, y, c)).wait_recv()
        for cp in direct + onward + to_sibling:
            cp.wait_send()

    out_shape = jax.ShapeDtypeStruct((N_CHIPS, r, cols), shard.dtype)
    sems = (pltpu.SemaphoreType.DMA((8,)), pltpu.SemaphoreType.DMA((8,)))
    if collective_id is None:
        return pl.pallas_call(body, name=name, out_shape=out_shape, in_specs=[HBM], out_specs=HBM,
                              scratch_shapes=list(sems))(shard)
    shard_ref = jax.new_ref(shard, memory_space=pltpu.MemorySpace.HBM)
    gathered_ref = jax.empty_ref(out_shape, memory_space=pltpu.MemorySpace.HBM)

    @_sequencer(name, collective_id, sems)
    def launch(send_sems, recv_sems):
        x, y, c = _coords()
        _handshake([(1 - x, y, c), (x, 1 - y, c), (x, y, 1 - c)])
        body(shard_ref, gathered_ref, send_sems, recv_sems)

    launch()
    return gathered_ref[...]


def _exchange_sibling_halves(g, name):
    n, r, cols = g.shape
    half = r // 2

    def body(g_ref, out_ref, send_sem, recv_sem):
        x, y, c = _coords()
        cp = pltpu.make_async_remote_copy(
            src_ref=g_ref.at[:, pl.ds((1 - c) * half, half), :], dst_ref=out_ref,
            send_sem=send_sem, recv_sem=recv_sem, device_id=(x, y, 1 - c), device_id_type=MESH)
        cp.start()
        cp.wait()

    return pl.pallas_call(
        body, name=name, out_shape=jax.ShapeDtypeStruct((n, half, cols), g.dtype),
        in_specs=[HBM], out_specs=HBM,
        scratch_shapes=[pltpu.SemaphoreType.DMA, pltpu.SemaphoreType.DMA],
    )(g)


def _sequencer(name, collective_id, scratch_types):
    return pl.kernel(mesh=plsc.ScalarSubcoreMesh(axis_name="sequencer", num_cores=1), name=name,
                     scratch_types=scratch_types, compiler_params=pltpu.CompilerParams(collective_id=collective_id))


def _handshake(peers):
    barrier = pltpu.get_barrier_semaphore()
    for peer in peers:
        pl.semaphore_signal(barrier, inc=1, device_id=peer, device_id_type=MESH)
    pl.semaphore_wait(barrier, len(peers))


def _exchange_sibling_halves_async(g, name, collective_id):
    n, r, cols = g.shape
    half = r // 2
    g_ref = jax.new_ref(g, memory_space=pltpu.MemorySpace.HBM)
    out_ref = jax.empty_ref(jax.ShapeDtypeStruct((n, half, cols), g.dtype), memory_space=pltpu.MemorySpace.HBM)

    @_sequencer(name, collective_id, (pltpu.SemaphoreType.DMA, pltpu.SemaphoreType.DMA))
    def launch(send_sem, recv_sem):
        x, y, c = _coords()
        _handshake([(x, y, 1 - c)])
        cp = pltpu.make_async_remote_copy(
            src_ref=g_ref.at[:, pl.ds((1 - c) * half, half), :], dst_ref=out_ref,
            send_sem=send_sem, recv_sem=recv_sem, device_id=(x, y, 1 - c), device_id_type=MESH)
        cp.start()
        cp.wait()

    launch()
    return out_ref[...]


def _share_halves_async(v, name, collective_id):
    h = v.shape[0] // 2
    v_ref = jax.new_ref(v, memory_space=pltpu.MemorySpace.HBM)

    @_sequencer(name, collective_id, (pltpu.SemaphoreType.DMA, pltpu.SemaphoreType.DMA))
    def launch(send_sem, recv_sem):
        x, y, c = _coords()
        _handshake([(x, y, 1 - c)])
        cp = pltpu.make_async_remote_copy(
            src_ref=v_ref.at[pl.ds(c * h, h), :], dst_ref=v_ref.at[pl.ds(c * h, h), :],
            send_sem=send_sem, recv_sem=recv_sem, device_id=(x, y, 1 - c), device_id_type=MESH)
        cp.start()
        pltpu.make_async_remote_copy(
            src_ref=v_ref.at[pl.ds(c * h, h), :], dst_ref=v_ref.at[pl.ds((1 - c) * h, h), :],
            send_sem=send_sem, recv_sem=recv_sem, device_id=(x, y, 1 - c), device_id_type=MESH).wait_recv()
        cp.wait_send()

    launch()
    return v_ref[...]


def _scatter_to_chips_async(p, name, collective_id):
    p_ref = jax.new_ref(p, memory_space=pltpu.MemorySpace.HBM)
    out_ref = jax.empty_ref(jax.ShapeDtypeStruct(p.shape, p.dtype), memory_space=pltpu.MemorySpace.HBM)

    @_sequencer(name, collective_id, (pltpu.SemaphoreType.DMA((3,)), pltpu.SemaphoreType.DMA((3,))))
    def launch(send_sems, recv_sems):
        x, y, c = _coords()
        me = 2 * x + y
        _handshake([(px, py, c) for px, py in _other_chips(x, y)])
        sends = []
        for j, (px, py) in enumerate(_other_chips(x, y)):
            sends.append(pltpu.make_async_remote_copy(
                src_ref=p_ref.at[2 * px + py], dst_ref=out_ref.at[me],
                send_sem=send_sems.at[j], recv_sem=recv_sems.at[j], device_id=(px, py, c), device_id_type=MESH))
        for cp in sends:
            cp.start()
        for j, (px, py) in enumerate(_other_chips(x, y)):
            pltpu.make_async_remote_copy(
                src_ref=p_ref.at[me], dst_ref=out_ref.at[2 * px + py],
                send_sem=send_sems.at[j], recv_sem=recv_sems.at[j], device_id=(px, py, c),
                device_id_type=MESH).wait_recv()
        for cp in sends:
            cp.wait_send()

    launch()
    return out_ref[...]


def _share_halves(v, name):
    h = v.shape[0] // 2

    def body(v_ref, out_ref, send_sem, recv_sem):
        x, y, c = _coords()
        cp = pltpu.make_async_remote_copy(
            src_ref=v_ref.at[pl.ds(c * h, h), :], dst_ref=out_ref.at[pl.ds(c * h, h), :],
            send_sem=send_sem, recv_sem=recv_sem, device_id=(x, y, 1 - c), device_id_type=MESH)
        cp.start()
        pltpu.make_async_remote_copy(
            src_ref=v_ref.at[pl.ds(c * h, h), :], dst_ref=out_ref.at[pl.ds((1 - c) * h, h), :],
            send_sem=send_sem, recv_sem=recv_sem, device_id=(x, y, 1 - c), device_id_type=MESH).wait_recv()
        cp.wait_send()

    return pl.pallas_call(
        body, name=name, out_shape=jax.ShapeDtypeStruct(v.shape, v.dtype),
        in_specs=[HBM], out_specs=HBM, input_output_aliases={0: 0},
        scratch_shapes=[pltpu.SemaphoreType.DMA, pltpu.SemaphoreType.DMA],
    )(v)


def _allreduce_small(v, name):
    r, cols = v.shape

    def body(v_ref, out_ref, buf_ref, send_sems, recv_sems):
        x, y, c = _coords()
        me = 4 * x + 2 * y + c
        buf_ref[me] = v_ref[...]
        sends = []
        for k in range(1, N_DEV):
            px = 1 - x if k & 4 else x
            py = 1 - y if k & 2 else y
            pc = 1 - c if k & 1 else c
            sends.append(pltpu.make_async_remote_copy(
                src_ref=v_ref, dst_ref=buf_ref.at[me], send_sem=send_sems.at[k - 1], recv_sem=recv_sems.at[k - 1],
                device_id=(px, py, pc), device_id_type=MESH))
        for cp in sends:
            cp.start()
        for cp in sends:
            cp.wait()
        acc = buf_ref[0]
        for d in range(1, N_DEV):
            acc = acc + buf_ref[d]
        out_ref[...] = acc

    return pl.pallas_call(
        body, name=name, out_shape=jax.ShapeDtypeStruct((r, cols), F32),
        in_specs=[pl.BlockSpec(memory_space=pltpu.VMEM)], out_specs=pl.BlockSpec(memory_space=pltpu.VMEM),
        scratch_shapes=[pltpu.VMEM((N_DEV, r, cols), F32), pltpu.SemaphoreType.DMA((N_DEV - 1,)),
                        pltpu.SemaphoreType.DMA((N_DEV - 1,))],
    )(v)


def _add_sibling(g, from_sibling, core, name):
    n, h, cols = from_sibling.shape
    tr = _row_tile(h)
    steps = h // tr

    def body(core_ref, a_ref, b_ref, o_ref):
        o_ref[...] = (a_ref[...] + b_ref[...]).astype(o_ref.dtype)

    return pl.pallas_call(
        body, name=name, out_shape=jax.ShapeDtypeStruct(from_sibling.shape, jnp.bfloat16),
        grid_spec=pltpu.PrefetchScalarGridSpec(
            num_scalar_prefetch=1, grid=(n, steps),
            in_specs=[pl.BlockSpec((1, tr, cols), lambda s, i, core_ref: (s, core_ref[0] * steps + i, 0)),
                      pl.BlockSpec((1, tr, cols), lambda s, i, core_ref: (s, i, 0))],
            out_specs=pl.BlockSpec((1, tr, cols), lambda s, i, core_ref: (s, i, 0))),
        compiler_params=_params("parallel", "parallel"),
    )(core.reshape(1).astype(jnp.int32), g, from_sibling)


def _sum_slots(p, own, chip, core, name):
    n, r, cols = p.shape
    tr = _row_tile(r)
    steps = r // tr

    def body(core_ref, chip_ref, p_ref, own_ref, o_ref):
        parts = [jnp.where(chip_ref[0] == s, own_ref[0], p_ref[s]).astype(F32) for s in range(n)]
        o_ref[...] = ((parts[0] + parts[1]) + parts[2]) + parts[3]

    return pl.pallas_call(
        body, name=name, out_shape=jax.ShapeDtypeStruct((2 * r, cols), F32),
        grid_spec=pltpu.PrefetchScalarGridSpec(
            num_scalar_prefetch=2, grid=(steps,),
            in_specs=[pl.BlockSpec((n, tr, cols), lambda i, core_ref, chip_ref: (0, i, 0)),
                      pl.BlockSpec((1, tr, cols), lambda i, core_ref, chip_ref: (chip_ref[0], i, 0))],
            out_specs=pl.BlockSpec((tr, cols), lambda i, core_ref, chip_ref: (core_ref[0] * steps + i, 0))),
        compiler_params=_params("parallel"),
    )(core.reshape(1).astype(jnp.int32), chip.reshape(1).astype(jnp.int32), p, own)


PACK_COLS = 1024


def _pack_shards(parts):
    return jnp.concatenate([p.reshape(-1, PACK_COLS) for p in parts], axis=0)


def _unpack_shards(buf, shapes):
    out, row = [], 0
    for shp in shapes:
        nrows = math.prod(shp) // PACK_COLS
        out.append(buf[..., row:row + nrows, :].reshape(buf.shape[:-2] + tuple(shp)))
        row += nrows
    return out


def _local_step(x, target, w):
    t = lambda a: a.T
    g = {}
    w_in_g, w_in_x = w["a_w_in"][:, :D_RNN], w["a_w_in"][:, D_RNN:]
    h0, gate_br, x_br = _norm_and_project(x, w["norm_mix_g"][0], w_in_g, w_in_x, "rglru_in")
    y_a, hs = _rglru_fwd(gate_br, x_br, w["a_conv_w"], w["a_conv_b"], w["a_w_r"], w["a_w_i"], w["a_b_r"],
                         w["a_b_i"], w["a_lambda"], "rglru_fwd")
    x1, h1 = _matmul([(y_a, w["a_w_out"])], F32, "mm_a_out", addend=x, norm_gain=w["norm_ffn_g"][0])
    fg0, fu0, act0 = _ffn_up(h1, w["ffn_w_gate"][0], w["ffn_w_up"][0], "ffn0_up")
    x2, h2 = _matmul([(act0, w["ffn_w_down"][0])], F32, "mm_f0_down", addend=x1, norm_gain=w["norm_mix_g"][1])
    qkv = _matmul([(h2, w["b_w_qkv"])], CD, "mm_b_qkv", out_lbm=True, tn=1024)
    o = _attn_fwd(qkv, "attn_fwd")
    x3, h3 = _matmul([(o, w["b_w_out"])], F32, "mm_b_out", a_lbm=True, addend=x2, norm_gain=w["norm_ffn_g"][1])
    fg1, fu1, act1 = _ffn_up(h3, w["ffn_w_gate"][1], w["ffn_w_up"][1], "ffn1_up")
    dx4, dx4c, g["final_g"], loss = _matmul([(act1, w["ffn_w_down"][1])], F32, "mm_f1_down", addend=x3,
                                            loss_head=(w["final_g"], target))

    def ffn_bwd(dx_out, dxc, h, x_in, fg, fu, act, layer, tag):
        dg, du = _ffn_dact(dxc, t(w["ffn_w_down"][layer]), fg, fu, "ffn_" + tag + "_dact")
        dwd = _matmul([(act, dxc)], F32, "mm_" + tag + "_dwd", trans_a=True)
        dwg = _matmul([(h, dg)], F32, "mm_" + tag + "_dwg", trans_a=True)
        dwu = _matmul([(h, du)], F32, "mm_" + tag + "_dwu", trans_a=True)
        dx_in, dx_in_c, dgain = _matmul([(dg, t(w["ffn_w_gate"][layer])), (du, t(w["ffn_w_up"][layer]))], F32,
                                        "mm_" + tag + "_dh", norm_bwd=(x_in, w["norm_ffn_g"][layer], dx_out))
        return dx_in, dx_in_c, dgain, dwg, dwu, dwd

    dx3, dx3c, dgf1, dwg1, dwu1, dwd1 = ffn_bwd(dx4, dx4c, h3, x3, fg1, fu1, act1, 1, "f1")
    do = _matmul([(dx3c, t(w["b_w_out"]))], F32, "mm_b_do", out_lbm=True, tn=1024)
    g["b_w_out"] = _matmul([(o, dx3c)], F32, "mm_b_dwout", trans_a=True, a_lbm=True)
    dq, dk, dv = _attn_bwd(qkv, o, do, "attn_bwd")
    wq_t = t(w["b_w_qkv"])
    parts = (dq, dk, dv)
    g["b_w_qkv"] = jnp.concatenate(
        [_matmul([(h2, p)], F32, "mm_b_dwqkv%d" % n, trans_a=True, b_lbm=True) for n, p in enumerate(parts)], axis=1)
    dx2, dx2c, dgm1 = _matmul([(p, wq_t[n * D_MODEL:(n + 1) * D_MODEL]) for n, p in enumerate(parts)], F32, "mm_b_dh",
                              a_lbm=True, norm_bwd=(x2, w["norm_mix_g"][1], dx3))
    dx1, dx1c, dgf0, dwg0, dwu0, dwd0 = ffn_bwd(dx2, dx2c, h1, x1, fg0, fu0, act0, 0, "f0")
    dy_a = _matmul([(dx1c, t(w["a_w_out"]))], F32, "mm_a_dy")
    g["a_w_out"] = _matmul([(y_a, dx1c)], F32, "mm_a_dwout", trans_a=True)
    wrt = jnp.swapaxes(w["a_w_r"], 1, 2)
    wit = jnp.swapaxes(w["a_w_i"], 1, 2)
    (dgate, dxbr, g["a_conv_w"], g["a_conv_b"], g["a_b_r"], g["a_b_i"], g["a_lambda"], g["a_w_r"],
     g["a_w_i"]) = _rglru_bwd(dy_a, gate_br, x_br, hs, w["a_conv_w"], w["a_conv_b"], w["a_w_r"], w["a_w_i"], wrt, wit,
                              w["a_b_r"], w["a_b_i"], w["a_lambda"], "rglru_bwd")
    g["a_w_in"] = jnp.concatenate([_matmul([(h0, dgate)], F32, "mm_a_dwin_g", trans_a=True),
                                   _matmul([(h0, dxbr)], F32, "mm_a_dwin_x", trans_a=True)], axis=1)
    dx0, _, dgm0 = _matmul([(dgate, t(w_in_g)), (dxbr, t(w_in_x))], F32, "mm_a_dh",
                           norm_bwd=(x, w["norm_mix_g"][0], dx1))
    g["norm_mix_g"] = jnp.concatenate([dgm0, dgm1], axis=0)
    g["norm_ffn_g"] = jnp.concatenate([dgf0, dgf1], axis=0)
    g["ffn_w_gate"] = [dwg0, dwg1]
    g["ffn_w_up"] = [dwu0, dwu1]
    g["ffn_w_down"] = [dwd0, dwd1]
    return loss, dx0, g


WEIGHTS = ["norm_mix_g", "norm_ffn_g", "a_w_in", "a_conv_w", "a_conv_b", "a_w_r", "a_b_r", "a_w_i", "a_b_i",
           "a_lambda", "a_w_out", "b_w_qkv", "b_w_out", "ffn_w_gate", "ffn_w_up", "ffn_w_down", "final_g"]
BIG = [("a_w_in", 2), ("a_w_r", 2), ("a_w_i", 2), ("a_w_out", 1), ("b_w_qkv", 2), ("b_w_out", 1),
       ("ffn_w_gate", 2), ("ffn_w_up", 2), ("ffn_w_down", 1)]
LAYER1 = ["b_w_qkv", "b_w_out", "ffn_w_gate", "ffn_w_up", "ffn_w_down"]
LAYER0 = ["a_w_in", "a_w_r", "a_w_i", "a_w_out", "ffn_w_gate", "ffn_w_up", "ffn_w_down"]
RS_COLLECTIVE_IDS = {"chips1": 3, "chips0": 4, "sibling1": 5, "share1": 6}
GATHER_COLLECTIVE_IDS = (8, 7)
SMALL = ["norm_mix_g", "norm_ffn_g", "a_conv_w", "a_conv_b", "a_b_r", "a_b_i", "a_lambda", "final_g"]


def _split_chips(full, axis):
    if axis == 1:
        return full.reshape((N_CHIPS, 1, full.shape[1] // N_CHIPS) + full.shape[2:])
    return jnp.stack(jnp.split(full, N_CHIPS, axis=axis))


def _step(x, target, weights, moments_m, moments_v):
    chip = 2 * lax.axis_index("x") + lax.axis_index("y")
    core = lax.axis_index("c")
    axis_of = dict(BIG)
    full = {}
    for group, layer, tag, collective_id in ((LAYER0[:4], 0, "0a", None), (LAYER0[4:], 0, "0f", GATHER_COLLECTIVE_IDS[0]),
                                             (LAYER1, 1, "1", GATHER_COLLECTIVE_IDS[1])):
        shards = [weights[n][layer % weights[n].shape[0]].astype(CD) for n in group]
        packed = _pack_shards(shards)
        if collective_id is not None:
            packed, first_gathered = lax.optimization_barrier((packed, first_gathered))
        gathered = _allgather_chips(packed, "allgather_weights" + tag, collective_id)
        if collective_id is None:
            first_gathered = gathered
        for n, own, stack in zip(group, shards, _unpack_shards(gathered, [sh.shape for sh in shards])):
            joined = jnp.concatenate([jnp.where(chip == s, own, stack[s]) for s in range(N_CHIPS)],
                                     axis=axis_of[n] - 1)
            full.setdefault(n, {})[layer] = joined
    full = {n: (v[0] if n.startswith("a_") else v[1] if n.startswith("b_") else [v[0], v[1]]) for n, v in full.items()}
    cw_rows = jnp.zeros((N_CHIPS, CONV_W, RG_BW), F32)
    cw_rows = lax.dynamic_update_slice(cw_rows, jnp.where(core == 0, weights["a_conv_w"], 0.0), (chip, 0, 0))
    cw_all = _allreduce_small(cw_rows.reshape(-1, LANES), "allgather_conv_w").reshape(N_CHIPS, CONV_W, RG_BW)
    full["a_conv_w"] = jnp.concatenate([cw_all[s] for s in range(N_CHIPS)], axis=1)
    for n in ("norm_mix_g", "norm_ffn_g", "final_g"):
        full[n] = weights[n]
    for n in ("a_conv_b", "a_b_r", "a_b_i", "a_lambda"):
        full[n] = weights[n]
    loss, dx, grads = _local_step(x[0], target[0], full)
    small_parts = [grads[n].reshape(-1) for n in SMALL] + [loss.reshape(-1)]
    sizes = [p.shape[0] for p in small_parts]
    small = _allreduce_small(jnp.concatenate(small_parts).reshape(-1, LANES), "allreduce_small").reshape(-1)
    red, pos = {}, 0
    for n, sz in zip(SMALL + ["loss"], sizes):
        red[n] = small[pos:pos + sz]
        pos += sz
    loss_out = red["loss"][0]
    g_out = {}
    for n in SMALL:
        if n == "a_conv_w":
            g_out[n] = lax.dynamic_slice(red[n].reshape(CONV_W, D_RNN), (0, chip * RG_BW), (CONV_W, RG_BW)).reshape(
                weights[n].shape)
        else:
            g_out[n] = red[n].reshape(weights[n].shape)
    axis_of = dict(BIG)
    pieces = {}
    for group, layer, tag in ((LAYER1, 1, "1"), (LAYER0, 0, "0")):
        stacks, shapes = [], []
        for n in group:
            per_layer = isinstance(grads[n], list)
            gfull = grads[n][layer] if per_layer else grads[n]
            shard_shape = weights[n].shape[1:]
            gfull = gfull.reshape((1,) + gfull.shape)
            stacks.append(_split_chips(gfull, axis_of[n]).reshape(N_CHIPS, -1, PACK_COLS))
            shapes.append((1,) + tuple(shard_shape))
        gbuf = jnp.concatenate(stacks, axis=1)
        if layer == 1:
            from_sibling = _exchange_sibling_halves_async(gbuf, "rs_sibling" + tag, RS_COLLECTIVE_IDS["sibling1"])
        else:
            from_sibling = _exchange_sibling_halves(gbuf, "rs_sibling" + tag)
        chip_partial = _add_sibling(gbuf, from_sibling, core, "rs_add" + tag)
        from_chips = _scatter_to_chips_async(chip_partial, "rs_chips" + tag, RS_COLLECTIVE_IDS["chips" + tag])
        halves = _sum_slots(from_chips, chip_partial, chip, core, "rs_sum" + tag)
        if layer == 1:
            reduced = _share_halves_async(halves, "rs_share" + tag, RS_COLLECTIVE_IDS["share1"])
        else:
            reduced = _share_halves(halves, "rs_share" + tag)
        for n, piece in zip(group, _unpack_shards(reduced, shapes)):
            pieces.setdefault(n, {})[layer] = piece
    for n, _ in BIG:
        layers = pieces[n]
        g_out[n] = jnp.concatenate([layers[k] for k in sorted(layers)], axis=0)
    updates = {}
    for n, _ in BIG:
        updates[n] = _adamw(weights[n], g_out[n], moments_m[n], moments_v[n], "adamw_" + n)
    rows = lambda d: jnp.concatenate([d[n].reshape(-1, D_MODEL) for n in SMALL], axis=0)
    small_updates = _adamw(rows(weights), rows(g_out), rows(moments_m), rows(moments_v), "adamw_small")
    pos = 0
    for n in SMALL:
        nrows = weights[n].size // D_MODEL
        updates[n] = tuple(u[pos:pos + nrows].reshape(weights[n].shape) for u in small_updates)
        pos += nrows
    outs_g = [g_out[n] for n in WEIGHTS]
    outs_d, outs_m, outs_v = ([updates[n][k] for n in WEIGHTS] for k in range(3))
    return (loss_out, dx[None], *outs_g, *outs_d, *outs_m, *outs_v)


def kernel(x, norm_mix_g, norm_ffn_g, a_w_in, a_conv_w, a_conv_b, a_w_r, a_b_r, a_w_i, a_b_i, a_lambda, a_w_out, b_w_qkv, b_w_out, ffn_w_gate, ffn_w_up, ffn_w_down, final_g, loss_target, m_norm_mix_g, m_norm_ffn_g, m_a_w_in, m_a_conv_w, m_a_conv_b, m_a_w_r, m_a_b_r, m_a_w_i, m_a_b_i, m_a_lambda, m_a_w_out, m_b_w_qkv, m_b_w_out, m_ffn_w_gate, m_ffn_w_up, m_ffn_w_down, m_final_g, v_norm_mix_g, v_norm_ffn_g, v_a_w_in, v_a_conv_w, v_a_conv_b, v_a_w_r, v_a_b_r, v_a_w_i, v_a_b_i, v_a_lambda, v_a_w_out, v_b_w_qkv, v_b_w_out, v_ffn_w_gate, v_ffn_w_up, v_ffn_w_down, v_final_g):
    ws = [norm_mix_g, norm_ffn_g, a_w_in, a_conv_w, a_conv_b, a_w_r, a_b_r, a_w_i, a_b_i, a_lambda, a_w_out, b_w_qkv,
          b_w_out, ffn_w_gate, ffn_w_up, ffn_w_down, final_g]
    ms = [m_norm_mix_g, m_norm_ffn_g, m_a_w_in, m_a_conv_w, m_a_conv_b, m_a_w_r, m_a_b_r, m_a_w_i, m_a_b_i, m_a_lambda,
          m_a_w_out, m_b_w_qkv, m_b_w_out, m_ffn_w_gate, m_ffn_w_up, m_ffn_w_down, m_final_g]
    vs = [v_norm_mix_g, v_norm_ffn_g, v_a_w_in, v_a_conv_w, v_a_conv_b, v_a_w_r, v_a_b_r, v_a_w_i, v_a_b_i, v_a_lambda,
          v_a_w_out, v_b_w_qkv, v_b_w_out, v_ffn_w_gate, v_ffn_w_up, v_ffn_w_down, v_final_g]
    return _step(x, loss_target, dict(zip(WEIGHTS, ws)), dict(zip(WEIGHTS, ms)), dict(zip(WEIGHTS, vs)))
```

```python
import functools
import math

import jax
import jax.numpy as jnp
from jax import lax
from jax.experimental import pallas as pl
from jax.experimental.pallas import tpu as pltpu
from jax.experimental.pallas import tpu_sc as plsc

F32 = jnp.float32
CD = jnp.bfloat16

D_MODEL = 1024
D_RNN = 1024
RG_BLOCKS = 4
RG_BW = 256
CONV_W = 4
RG_C = 8.0
SB_HEADS = 16
SB_HEAD_DIM = 64
D_FF = 2816
RMS_EPS = 1e-6
N_CHIPS = 4
N_DEV = 8

ADAM_LR = 0.001
ADAM_B1 = 0.9
ADAM_B2 = 0.999
ADAM_EPS = 1e-08
ADAM_WD = 0.01
ADAM_STEP = 10

LANES = 128
VMEM_LIMIT = 56 * 1024 * 1024
MESH = pl.DeviceIdType.MESH


def _params(*sem):
    return pltpu.CompilerParams(dimension_semantics=sem, vmem_limit_bytes=VMEM_LIMIT)


def _pick(n, prefs):
    for p in prefs:
        if n % p == 0:
            return p
    return n


def _row_tile(rows):
    return max(d for d in range(16, 1025, 16) if rows % d == 0)


def _matmul(pairs, out_dtype, name, *, trans_a=False, a_lbm=False, b_lbm=False, out_lbm=False, addend=None,
            tm=512, tn=None, tk=None, norm_gain=None, norm_bwd=None, loss_head=None):
    a0, b0 = pairs[0]
    if trans_a:
        kdim = a0.shape[1] if a_lbm else a0.shape[0]
        m = a0.shape[0] * LANES if a_lbm else a0.shape[1]
    else:
        m = a0.shape[1] if a_lbm else a0.shape[0]
        kdim = a0.shape[0] * LANES if a_lbm else a0.shape[1]
    n = b0.shape[0] * LANES if b_lbm else b0.shape[1]
    tm = _pick(m, (tm, 1408, 256, 128))
    tn = tn or _pick(n, (1408, 1024, 768, 512, 256, 128))
    tk = tk or _pick(kdim, (1024, 1408, 512, 256, 128))
    nk = kdim // tk
    npair = len(pairs)

    def cat(ref):
        return jnp.concatenate([ref[p] for p in range(ref.shape[0])], axis=-1)

    def body(*refs):
        ins = refs[: 2 * npair]
        pos = 2 * npair
        add_ref = None
        if addend is not None:
            add_ref = refs[pos]
            pos += 1
        gain_ref = x_ref = dxin_ref = None
        if norm_gain is not None:
            gain_ref = refs[pos]
            pos += 1
        if norm_bwd is not None:
            x_ref, gain_ref, dxin_ref = refs[pos:pos + 3]
            pos += 3
        if loss_head is not None:
            gain_ref, target_ref = refs[pos:pos + 2]
            pos += 2
        o_ref = refs[pos]
        extra_out = refs[pos + 1:-1]
        acc_ref = refs[-1]
        k = pl.program_id(2)

        @pl.when(k == 0)
        def _():
            acc_ref[...] = jnp.zeros_like(acc_ref)

        if norm_bwd is not None or loss_head is not None:
            @pl.when((k == 0) & (pl.program_id(0) == 0))
            def _():
                for ref in extra_out[1:]:
                    ref[...] = jnp.zeros_like(ref)

        acc = acc_ref[...]
        for p in range(npair):
            a = (cat(ins[2 * p]) if a_lbm else ins[2 * p][...]).astype(CD)
            b = (cat(ins[2 * p + 1]) if b_lbm else ins[2 * p + 1][...]).astype(CD)
            dims = (((0,), (0,)), ((), ())) if trans_a else (((1,), (0,)), ((), ()))
            acc = acc + lax.dot_general(a, b, dims, preferred_element_type=F32)
        acc_ref[...] = acc

        @pl.when(k == nk - 1)
        def _():
            res = acc_ref[...]
            if add_ref is not None:
                res = res + add_ref[...]
            if norm_gain is not None:
                rinv = lax.rsqrt(jnp.mean(res * res, axis=-1, keepdims=True) + RMS_EPS)
                extra_out[0][...] = (res * rinv * gain_ref[...]).astype(CD)
            if norm_bwd is not None:
                xv = x_ref[...]
                rinv = lax.rsqrt(jnp.mean(xv * xv, axis=-1, keepdims=True) + RMS_EPS)
                nrm = xv * rinv
                dn = res * gain_ref[...]
                extra_out[1][...] += jnp.sum(res * nrm, axis=0, keepdims=True)
                res = dxin_ref[...] + rinv * (dn - nrm * jnp.mean(dn * nrm, axis=-1, keepdims=True))
                extra_out[0][...] = res.astype(CD)
            if loss_head is not None:
                gv = gain_ref[...]
                rinv = lax.rsqrt(jnp.mean(res * res, axis=-1, keepdims=True) + RMS_EPS)
                nrm = res * rinv
                err = nrm * gv - target_ref[...]
                extra_out[2][...] += 0.5 * jnp.sum(jnp.mean(err * err, axis=-1, keepdims=True), axis=0, keepdims=True)
                dy = err * (1.0 / n)
                dn = dy * gv
                extra_out[1][...] += jnp.sum(dy * nrm, axis=0, keepdims=True)
                res = rinv * (dn - nrm * jnp.mean(dn * nrm, axis=-1, keepdims=True))
                extra_out[0][...] = res.astype(CD)
            res = res.astype(out_dtype)
            if out_lbm:
                for p in range(tn // LANES):
                    o_ref[p] = res[:, p * LANES:(p + 1) * LANES]
            else:
                o_ref[...] = res

    if trans_a:
        a_spec = (pl.BlockSpec((tm // LANES, tk, LANES), lambda i, j, k: (i, k, 0)) if a_lbm
                  else pl.BlockSpec((tk, tm), lambda i, j, k: (k, i)))
    else:
        a_spec = (pl.BlockSpec((tk // LANES, tm, LANES), lambda i, j, k: (k, i, 0)) if a_lbm
                  else pl.BlockSpec((tm, tk), lambda i, j, k: (i, k)))
    b_spec = (pl.BlockSpec((tn // LANES, tk, LANES), lambda i, j, k: (j, k, 0)) if b_lbm
              else pl.BlockSpec((tk, tn), lambda i, j, k: (k, j)))
    in_specs = []
    args = []
    for a, b in pairs:
        in_specs += [a_spec, b_spec]
        args += [a, b]
    if addend is not None:
        in_specs.append(pl.BlockSpec((tm, tn), lambda i, j, k: (i, j)))
        args.append(addend)
    tile = pl.BlockSpec((tm, tn), lambda i, j, k: (i, j))
    vec = pl.BlockSpec((1, tn), lambda i, j, k: (0, j))
    if out_lbm:
        out_shape = jax.ShapeDtypeStruct((n // LANES, m, LANES), out_dtype)
        out_spec = pl.BlockSpec((tn // LANES, tm, LANES), lambda i, j, k: (j, i, 0))
    else:
        out_shape = jax.ShapeDtypeStruct((m, n), out_dtype)
        out_spec = tile
    sem = ("parallel", "parallel", "arbitrary")
    if norm_gain is not None or norm_bwd is not None or loss_head is not None:
        assert tn == n and not out_lbm, "the norm needs whole rows in one tile"
        out_shape, out_spec = [out_shape, jax.ShapeDtypeStruct((m, n), CD)], [out_spec, tile]
    if norm_gain is not None:
        in_specs.append(vec)
        args.append(norm_gain.reshape(1, n))
    if norm_bwd is not None:
        x_in, gain, dx_in = norm_bwd
        in_specs += [tile, vec, tile]
        args += [x_in, gain.reshape(1, n), dx_in]
        out_shape.append(jax.ShapeDtypeStruct((1, n), F32))
        out_spec.append(vec)
        sem = ("arbitrary", "arbitrary", "arbitrary")
    if loss_head is not None:
        gain, target = loss_head
        in_specs += [vec, tile]
        args += [gain.reshape(1, n), target]
        out_shape += [jax.ShapeDtypeStruct((1, n), F32), jax.ShapeDtypeStruct((1, LANES), F32)]
        out_spec += [vec, pl.BlockSpec((1, LANES), lambda i, j, k: (0, 0))]
        sem = ("arbitrary", "arbitrary", "arbitrary")
    return pl.pallas_call(
        body, name=name, out_shape=out_shape, grid=(m // tm, n // tn, nk),
        in_specs=in_specs, out_specs=out_spec,
        scratch_shapes=[pltpu.VMEM((tm, tn), F32)],
        compiler_params=_params(*sem),
    )(*args)


def _norm_and_project(x, g, w_a, w_b, name):
    s, d = x.shape
    n = w_a.shape[1]
    tm = _pick(s, (512, 256))

    def body(x_ref, g_ref, wa_ref, wb_ref, h_ref, a_ref, b_ref):
        xv = x_ref[...]
        rinv = lax.rsqrt(jnp.mean(xv * xv, axis=-1, keepdims=True) + RMS_EPS)
        h = (xv * rinv * g_ref[...]).astype(CD)
        h_ref[...] = h
        a_ref[...] = jnp.dot(h, wa_ref[...], preferred_element_type=F32)
        b_ref[...] = jnp.dot(h, wb_ref[...], preferred_element_type=F32)

    row = pl.BlockSpec((tm, d), lambda i: (i, 0))
    out = pl.BlockSpec((tm, n), lambda i: (i, 0))
    wspec = pl.BlockSpec((d, n), lambda i: (0, 0))
    return pl.pallas_call(
        body, name=name,
        out_shape=(jax.ShapeDtypeStruct((s, d), CD), jax.ShapeDtypeStruct((s, n), F32),
                   jax.ShapeDtypeStruct((s, n), F32)),
        grid=(s // tm,), in_specs=[row, pl.BlockSpec((1, d), lambda i: (0, 0)), wspec, wspec],
        out_specs=(row, out, out), compiler_params=_params("parallel"),
    )(x, g.reshape(1, d), w_a, w_b)


def _sigmoid(z):
    return 1.0 / (1.0 + jnp.exp(-z))


FFN_TM = 512
FFN_TN = 1408


def _ffn_up(h, wg, wu, name):
    s, d = h.shape
    f = wg.shape[1]
    tm = _pick(s, (FFN_TM, 256))

    def body(h_ref, wg_ref, wu_ref, g_ref, u_ref, a_ref):
        hv = h_ref[...]
        gv = jnp.dot(hv, wg_ref[...], preferred_element_type=F32)
        uv = jnp.dot(hv, wu_ref[...], preferred_element_type=F32)
        g_ref[...] = gv
        u_ref[...] = uv
        a_ref[...] = (gv * _sigmoid(gv) * uv).astype(CD)

    a_spec = pl.BlockSpec((tm, d), lambda j, i: (i, 0))
    w_spec = pl.BlockSpec((d, FFN_TN), lambda j, i: (0, j))
    o_spec = pl.BlockSpec((tm, FFN_TN), lambda j, i: (i, j))
    return pl.pallas_call(
        body, name=name,
        out_shape=(jax.ShapeDtypeStruct((s, f), F32), jax.ShapeDtypeStruct((s, f), F32),
                   jax.ShapeDtypeStruct((s, f), CD)),
        grid=(f // FFN_TN, s // tm), in_specs=[a_spec, w_spec, w_spec], out_specs=(o_spec, o_spec, o_spec),
        compiler_params=_params("parallel", "parallel"),
    )(h, wg, wu)


def _ffn_dact(dxc, wd_t, g, u, name):
    s, d = dxc.shape
    f = wd_t.shape[1]
    tm = _pick(s, (FFN_TM, 256))

    def body(dx_ref, w_ref, g_ref, u_ref, dg_ref, du_ref):
        da = jnp.dot(dx_ref[...], w_ref[...], preferred_element_type=F32)
        gv = g_ref[...]
        sg = _sigmoid(gv)
        silu = gv * sg
        dg_ref[...] = (da * u_ref[...] * (sg + silu * (1.0 - sg))).astype(CD)
        du_ref[...] = (da * silu).astype(CD)

    a_spec = pl.BlockSpec((tm, d), lambda j, i: (i, 0))
    w_spec = pl.BlockSpec((d, FFN_TN), lambda j, i: (0, j))
    o_spec = pl.BlockSpec((tm, FFN_TN), lambda j, i: (i, j))
    return pl.pallas_call(
        body, name=name,
        out_shape=(jax.ShapeDtypeStruct((s, f), CD), jax.ShapeDtypeStruct((s, f), CD)),
        grid=(f // FFN_TN, s // tm), in_specs=[a_spec, w_spec, o_spec, o_spec], out_specs=(o_spec, o_spec),
        compiler_params=_params("parallel", "parallel"),
    )(dxc, wd_t, g, u)


TIME_BLOCK = 1024
SUBLANES = 8
GELU_C = math.sqrt(2.0 / math.pi)
GELU_A = 0.044715


def _gelu(x):
    return 0.5 * x * (1.0 + jnp.tanh(GELU_C * (x + GELU_A * x * x * x)))


def _gelu_grad(x):
    t = jnp.tanh(GELU_C * (x + GELU_A * x * x * x))
    return 0.5 * (1.0 + t) + 0.5 * x * (1.0 - t * t) * GELU_C * (1.0 + 3.0 * GELU_A * x * x)


def _neg_expm1(x):
    series = -x * (1.0 + x * (0.5 + x * (1.0 / 6.0 + x * (1.0 / 24.0))))
    return jnp.where(x > -0.05, series, 1.0 - jnp.exp(x))


def _log_sigmoid(x):
    return jnp.minimum(x, 0.0) - jnp.log1p(jnp.exp(-jnp.abs(x)))


def _shift_down(x, tail, s):
    if s == 0:
        return x
    ext = jnp.concatenate([tail, x], axis=0)
    return pltpu.roll(ext, s, axis=0)[SUBLANES:]


def _shift_up(x, head, s):
    if s == 0:
        return x
    n = x.shape[0]
    ext = jnp.concatenate([x, head], axis=0)
    return pltpu.roll(ext, n + SUBLANES - s, axis=0)[:n]


def _rg_gates(xbr, tail, cw_ref, cb, wr, wi, br, bi, ls):
    taps = [_shift_down(xbr, tail, CONV_W - 1 - k) for k in range(CONV_W)]
    xc = cb
    for k in range(CONV_W):
        xc = xc + cw_ref[pl.ds(k, 1), :] * taps[k]
    xcd = xc.astype(CD)
    r = _sigmoid(jnp.dot(xcd, wr, preferred_element_type=F32) + br)
    i = _sigmoid(jnp.dot(xcd, wi, preferred_element_type=F32) + bi)
    log_a = RG_C * r * ls
    a = jnp.exp(log_a)
    mult = jnp.sqrt(jnp.maximum(_neg_expm1(2.0 * log_a), 0.0))
    return taps, xc, r, i, log_a, a, mult


def _scan8_fwd(a, u):
    row = lax.broadcasted_iota(jnp.int32, a.shape, 0)
    for d in (1, 2, 4):
        a_s = pltpu.roll(a, d, axis=0)
        u_s = pltpu.roll(u, d, axis=0)
        m = row >= d
        u = jnp.where(m, a * u_s + u, u)
        a = jnp.where(m, a * a_s, a)
    return a, u


def _scan8_bwd(b, u):
    row = lax.broadcasted_iota(jnp.int32, b.shape, 0)
    for d in (1, 2, 4):
        b_s = pltpu.roll(b, SUBLANES - d, axis=0)
        u_s = pltpu.roll(u, SUBLANES - d, axis=0)
        m = row < SUBLANES - d
        u = jnp.where(m, b * u_s + u, u)
        b = jnp.where(m, b * b_s, b)
    return b, u


def _rglru_fwd(gate_br, x_br, cw, cb, wr, wi, br, bi, lam, name):
    s, c = x_br.shape
    nt = s // TIME_BLOCK
    tb, cbw = TIME_BLOCK, RG_BW
    groups = tb // SUBLANES

    def body(g_ref, x_ref, tail_ref, cw_ref, cb_ref, wr_ref, wi_ref, br_ref, bi_ref, lam_ref,
             y_ref, hs_ref, carry_ref, a_scr, u_scr):
        t = pl.program_id(1)

        @pl.when(t == 0)
        def _():
            carry_ref[...] = jnp.zeros_like(carry_ref)

        tail = jnp.where(t > 0, tail_ref[...], 0.0)
        ls = _log_sigmoid(lam_ref[...])
        _, xc, _, i, _, a, mult = _rg_gates(x_ref[...], tail, cw_ref, cb_ref[...], wr_ref[0], wi_ref[0],
                                            br_ref[...], bi_ref[...], ls)
        a_scr[...] = a
        u_scr[...] = mult * (i * xc)
        carry = carry_ref[...]
        for gi in range(groups):
            rows = pl.ds(gi * SUBLANES, SUBLANES)
            pa, hl = _scan8_fwd(a_scr[rows, :], u_scr[rows, :])
            hs_ref[rows, :] = hl + pa * carry
            carry = hs_ref[pl.ds(gi * SUBLANES + SUBLANES - 1, 1), :]
        carry_ref[...] = carry
        y_ref[...] = (hs_ref[...] * _gelu(g_ref[...])).astype(CD)

    blk = pl.BlockSpec((tb, cbw), lambda n, t: (t, n))
    tail = pl.BlockSpec((SUBLANES, cbw), lambda n, t: (jnp.maximum(t * groups - 1, 0), n))
    vec = pl.BlockSpec((1, cbw), lambda n, t: (0, n))
    wblk = pl.BlockSpec((1, cbw, cbw), lambda n, t: (n, 0, 0))
    return pl.pallas_call(
        body, name=name,
        out_shape=(jax.ShapeDtypeStruct((s, c), CD), jax.ShapeDtypeStruct((s, c), F32)),
        grid=(RG_BLOCKS, nt),
        in_specs=[blk, blk, tail, pl.BlockSpec((CONV_W, cbw), lambda n, t: (0, n)), vec, wblk, wblk, vec, vec, vec],
        out_specs=(blk, blk),
        scratch_shapes=[pltpu.VMEM((1, cbw), F32), pltpu.VMEM((tb, cbw), F32), pltpu.VMEM((tb, cbw), F32)],
        compiler_params=_params("parallel", "arbitrary"),
    )(gate_br, x_br, x_br, cw, cb, wr, wi, br, bi, lam)


def _rglru_bwd(dy, gate_br, x_br, hs, cw, cb, wr, wi, wrt, wit, br, bi, lam, name):
    s, c = x_br.shape
    nt = s // TIME_BLOCK
    tb, cbw = TIME_BLOCK, RG_BW
    groups = tb // SUBLANES

    def body(dy_ref, g_ref, x_ref, tail_ref, hs_ref, hprev_ref, cw_ref, cb_ref, wr_ref, wi_ref, wrt_ref, wit_ref,
             br_ref, bi_ref, lam_ref,
             dg_ref, dx_ref, dcw_ref, dcb_ref, dbr_ref, dbi_ref, dlam_ref, dwr_ref, dwi_ref,
             carry_ref, head_ref, b_scr, u_scr, dh_scr):
        tr = pl.program_id(1)
        first_block = tr == nt - 1

        @pl.when(tr == 0)
        def _():
            carry_ref[...] = jnp.zeros_like(carry_ref)
            head_ref[...] = jnp.zeros_like(head_ref)
            for ref in (dcw_ref, dcb_ref, dbr_ref, dbi_ref, dlam_ref, dwr_ref, dwi_ref):
                ref[...] = jnp.zeros_like(ref)

        tail = jnp.where(first_block, 0.0, tail_ref[...])
        lam_v = lam_ref[...]
        ls = _log_sigmoid(lam_v)
        taps, xc, r, i, log_a, a, mult = _rg_gates(x_ref[...], tail, cw_ref, cb_ref[...], wr_ref[0], wi_ref[0],
                                                   br_ref[...], bi_ref[...], ls)
        gate_v = g_ref[...]
        dyv = dy_ref[...]
        hsv = hs_ref[...]
        dg_ref[...] = (dyv * hsv * _gelu_grad(gate_v)).astype(CD)

        row = lax.broadcasted_iota(jnp.int32, a.shape, 0)
        b_scr[...] = jnp.where(row == tb - 1, 1.0, pltpu.roll(a, tb - 1, axis=0))
        u_scr[...] = dyv * _gelu(gate_v)
        carry = carry_ref[...]
        for gi in reversed(range(groups)):
            rows = pl.ds(gi * SUBLANES, SUBLANES)
            pb, gl = _scan8_bwd(b_scr[rows, :], u_scr[rows, :])
            dh_scr[rows, :] = gl + pb * carry
            carry = dh_scr[pl.ds(gi * SUBLANES, 1), :]
        dh = dh_scr[...]
        carry_ref[...] = carry * jnp.sum(jnp.where(row == 0, a, 0.0), axis=0, keepdims=True)

        hprev_tail = jnp.where(first_block, 0.0, hprev_ref[...])
        h_prev = _shift_down(hsv, hprev_tail, 1)
        da = dh * h_prev
        ixc = i * xc
        dmult = dh * ixc
        di = dh * mult * xc
        dxc = dh * mult * i
        a2 = a * a
        dlog_a = da * a - dmult * a2 / mult
        dpre_r = (dlog_a * (RG_C * ls)) * r * (1.0 - r)
        dpre_i = di * i * (1.0 - i)
        dlam_ref[...] += jnp.sum(dlog_a * r, axis=0, keepdims=True) * (RG_C * _sigmoid(-lam_v))
        dbr_ref[...] += jnp.sum(dpre_r, axis=0, keepdims=True)
        dbi_ref[...] += jnp.sum(dpre_i, axis=0, keepdims=True)
        xcd = xc.astype(CD)
        dprc = dpre_r.astype(CD)
        dpic = dpre_i.astype(CD)
        tn_dims = (((0,), (0,)), ((), ()))
        dwr_ref[0] += lax.dot_general(xcd, dprc, tn_dims, preferred_element_type=F32)
        dwi_ref[0] += lax.dot_general(xcd, dpic, tn_dims, preferred_element_type=F32)
        dxc = dxc + jnp.dot(dprc, wrt_ref[0], preferred_element_type=F32) + jnp.dot(dpic, wit_ref[0],
                                                                                    preferred_element_type=F32)
        dcb_ref[...] += jnp.sum(dxc, axis=0, keepdims=True)
        for k in range(CONV_W):
            dcw_ref[pl.ds(k, 1), :] += jnp.sum(dxc * taps[k], axis=0, keepdims=True)
        head = head_ref[...]
        dxb = jnp.zeros_like(dxc)
        for sft in range(CONV_W):
            dxb = dxb + cw_ref[pl.ds(CONV_W - 1 - sft, 1), :] * _shift_up(dxc, head, sft)
        dx_ref[...] = dxb.astype(CD)
        head_ref[...] = dxc[0:SUBLANES, :]

    blk = pl.BlockSpec((tb, cbw), lambda n, t: (nt - 1 - t, n))
    tail = pl.BlockSpec((SUBLANES, cbw), lambda n, t: (jnp.maximum((nt - 1 - t) * groups - 1, 0), n))
    vec = pl.BlockSpec((1, cbw), lambda n, t: (0, n))
    cwb = pl.BlockSpec((CONV_W, cbw), lambda n, t: (0, n))
    wblk = pl.BlockSpec((1, cbw, cbw), lambda n, t: (n, 0, 0))
    vshape = jax.ShapeDtypeStruct((1, c), F32)
    wshape = jax.ShapeDtypeStruct((RG_BLOCKS, cbw, cbw), F32)
    return pl.pallas_call(
        body, name=name,
        out_shape=(jax.ShapeDtypeStruct((s, c), CD), jax.ShapeDtypeStruct((s, c), CD),
                   jax.ShapeDtypeStruct((CONV_W, c), F32), vshape, vshape, vshape, vshape, wshape, wshape),
        grid=(RG_BLOCKS, nt),
        in_specs=[blk, blk, blk, tail, blk, tail, cwb, vec, wblk, wblk, wblk, wblk, vec, vec, vec],
        out_specs=(blk, blk, cwb, vec, vec, vec, vec, wblk, wblk),
        scratch_shapes=[pltpu.VMEM((1, cbw), F32), pltpu.VMEM((SUBLANES, cbw), F32),
                        pltpu.VMEM((tb, cbw), F32), pltpu.VMEM((tb, cbw), F32), pltpu.VMEM((tb, cbw), F32)],
        compiler_params=_params("parallel", "arbitrary"),
    )(dy, gate_br, x_br, x_br, hs, hs, cw, cb, wr, wi, wrt, wit, br, bi, lam)


ATT_BLOCK = 256
ATT_Q_BLOCK = 1024
ATT_RATIO = ATT_Q_BLOCK // ATT_BLOCK
ATT_SCALE = 1.0 / math.sqrt(SB_HEAD_DIM)
N_PAIRS = SB_HEADS * SB_HEAD_DIM // LANES
NT_DIMS = (((1,), (1,)), ((), ()))
TN_DIMS = (((0,), (0,)), ((), ()))


LOG2E = 1.4426950408889634


def _neg_abs(x):
    bits = lax.bitcast_convert_type(x, jnp.uint32) | jnp.uint32(0x80000000)
    return lax.bitcast_convert_type(bits, F32)


def _qk(qx, kb):
    return lax.dot_general(qx, kb, NT_DIMS, preferred_element_type=F32)


def _sb_logits(qk, valid):
    z2 = qk * (ATT_SCALE * LOG2E)
    lb2 = jnp.minimum(z2, 0.0) - jnp.log2(1.0 + jnp.exp2(_neg_abs(z2)))
    l2 = lb2 - z2
    if valid is not None:
        l2 = jnp.where(valid, l2, 0.0)
    return lb2, l2


def _hi_lo(x):
    hi = x.astype(CD)
    lo = (x - hi.astype(F32)).astype(CD)
    return jnp.concatenate([hi, lo], axis=1)


def _tri(strict, stacked):
    r = lax.broadcasted_iota(jnp.int32, (ATT_BLOCK, ATT_BLOCK), 0)
    c = lax.broadcasted_iota(jnp.int32, (ATT_BLOCK, ATT_BLOCK), 1)
    m = (r > c if strict else r >= c).astype(CD)
    return jnp.concatenate([m, m], axis=0) if stacked else m


def _attn_fwd(qkv, name):
    _, s, _ = qkv.shape
    tq, t = ATT_Q_BLOCK, ATT_BLOCK
    nblk = s // tq

    def body(q_ref, k_ref, v_ref, o_ref, qk_scr, w_scr):
        i = pl.program_id(1)
        lane = lax.broadcasted_iota(jnp.int32, (1, LANES), 1)
        head_masks = (lane < SB_HEAD_DIM, lane >= SB_HEAD_DIM)
        q = q_ref[0]
        qs = [jnp.where(m, q, jnp.zeros_like(q)) for m in head_masks]
        tri = _tri(True, False)
        rr = lax.broadcasted_iota(jnp.int32, (tq, t), 0)
        cc = lax.broadcasted_iota(jnp.int32, (tq, t), 1)

        def rows_of(j):
            return pl.ds(pl.multiple_of(j * t, t), t)

        def tail(x, row0):
            return x if row0 == 0 else x[row0:]

        def start_logits(j, row0=0):
            kb = k_ref[0, rows_of(j), :]
            for hd in range(2):
                qk_scr[hd, row0:, :] = _qk(tail(qs[hd], row0), kb)

        def weights(run, diagonal=False, row0=0):
            new_run = []
            valid = (cc < rr)[:tq - row0] if diagonal else None
            for hd in range(2):
                lb2, l2 = _sb_logits(qk_scr[hd, row0:, :], valid)
                w = jnp.exp2(lb2 + (tail(run[hd], row0) + jnp.dot(l2.astype(CD), tri, preferred_element_type=F32)))
                if valid is not None:
                    w = jnp.where(valid, w, 0.0)
                w_scr[row0:, hd * t:(hd + 1) * t] = w.astype(CD)
                rowsum = jnp.sum(l2, axis=1, keepdims=True)
                if row0:
                    rowsum = jnp.concatenate([jnp.zeros((row0, 1), F32), rowsum], axis=0)
                new_run.append(run[hd] + rowsum)
            return tuple(new_run)

        def apply_weights(j, row0=0):
            vb = v_ref[0, rows_of(j), :]
            vcat = jnp.concatenate([jnp.where(m, vb, jnp.zeros_like(vb)) for m in head_masks], axis=0)
            inc = jnp.dot(w_scr[row0:, :], vcat, preferred_element_type=F32)
            return inc if row0 == 0 else jnp.concatenate([jnp.zeros((row0, LANES), F32), inc], axis=0)

        zero = jnp.zeros((tq, 1), F32)
        last = ATT_RATIO - 1
        start_logits(ATT_RATIO * i + last, last * t)
        run = weights((zero, zero), True, last * t)
        oacc = jnp.zeros((tq, LANES), F32)
        for d in reversed(range(last)):
            start_logits(ATT_RATIO * i + d, d * t)
            oacc = oacc + apply_weights(ATT_RATIO * i + d + 1, (d + 1) * t)
            run = weights(run, True, d * t)
        start_logits(jnp.maximum(ATT_RATIO * i - 1, 0))

        def step(jj, carry):
            run, oacc = carry
            b = ATT_RATIO * i - 1 - jj
            oacc = oacc + apply_weights(b + 1)
            run = weights(run)
            start_logits(jnp.maximum(b - 1, 0))
            return run, oacc

        run, oacc = lax.fori_loop(0, ATT_RATIO * i, step, (run, oacc))
        o_ref[0] = oacc + apply_weights(0)

    return pl.pallas_call(
        body, name=name, out_shape=jax.ShapeDtypeStruct((N_PAIRS, s, LANES), F32), grid=(N_PAIRS, nblk),
        in_specs=[pl.BlockSpec((1, tq, LANES), lambda p, i: (p, i, 0)),
                  pl.BlockSpec((1, s, LANES), lambda p, i: (N_PAIRS + p, 0, 0)),
                  pl.BlockSpec((1, s, LANES), lambda p, i: (2 * N_PAIRS + p, 0, 0))],
        out_specs=pl.BlockSpec((1, tq, LANES), lambda p, i: (p, i, 0)),
        scratch_shapes=[pltpu.VMEM((2, tq, t), F32), pltpu.VMEM((tq, 2 * t), CD)],
        compiler_params=_params("parallel", "arbitrary"),
    )(qkv, qkv, qkv)


def _attn_bwd(qkv, o, do, name):
    _, s, _ = qkv.shape
    tq, t = ATT_Q_BLOCK, ATT_BLOCK
    nblk = s // tq

    def body(q_ref, k_ref, v_ref, o_ref, do_ref, dq_ref, dk_ref, dv_ref, qk_scr, dw_scr, w_scr, dz_scr):
        i = pl.program_id(1)

        @pl.when(i == 0)
        def _():
            dk_ref[...] = jnp.zeros_like(dk_ref)
            dv_ref[...] = jnp.zeros_like(dv_ref)

        lane = lax.broadcasted_iota(jnp.int32, (1, LANES), 1)
        head_masks = (lane < SB_HEAD_DIM, lane >= SB_HEAD_DIM)
        q = q_ref[0]
        dov = do_ref[0]
        ov = o_ref[0]
        qs = [jnp.where(m, q, jnp.zeros_like(q)) for m in head_masks]
        q_scaled_t = jnp.concatenate([(qx.astype(F32) * ATT_SCALE).T for qx in qs], axis=1).astype(CD)
        docs = [jnp.where(m, dov, 0.0).astype(CD) for m in head_masks]
        docat_t = jnp.concatenate([jnp.where(m, dov, 0.0).T for m in head_masks], axis=1).astype(CD)
        totals = [jnp.sum(d.astype(F32) * ov, axis=1, keepdims=True) for d in docs]
        tri = _tri(True, False)
        tri_incl = _tri(False, True)
        rr = lax.broadcasted_iota(jnp.int32, (tq, t), 0)
        cc = lax.broadcasted_iota(jnp.int32, (tq, t), 1)

        def rows_of(j):
            return pl.ds(pl.multiple_of(j * t, t), t)

        def tail(x, row0):
            return x if row0 == 0 else x[row0:]

        def pad_rows(x, row0):
            return x if row0 == 0 else jnp.concatenate([jnp.zeros((row0, x.shape[1]), x.dtype), x], axis=0)

        def start_products(j, row0=0):
            kb = k_ref[0, rows_of(j), :]
            vb = v_ref[0, rows_of(j), :]
            for hd in range(2):
                qk_scr[hd, row0:, :] = _qk(tail(qs[hd], row0), kb)
                dw_scr[hd, row0:, :] = lax.dot_general(tail(docs[hd], row0), vb, NT_DIMS, preferred_element_type=F32)

        def logit_grads(run, erun, diagonal=False, row0=0):
            new_run, new_erun = [], []
            valid = (cc < rr)[:tq - row0] if diagonal else None
            for hd in range(2):
                lb2, l2 = _sb_logits(qk_scr[hd, row0:, :], valid)
                w = jnp.exp2(lb2 + (tail(run[hd], row0) + jnp.dot(l2.astype(CD), tri, preferred_element_type=F32)))
                if valid is not None:
                    w = jnp.where(valid, w, 0.0)
                wc = w.astype(CD)
                w_scr[hd * tq + row0:(hd + 1) * tq, :] = wc
                e = dw_scr[hd, row0:, :] * wc.astype(F32)
                prefix = (tail(totals[hd] - erun[hd], row0)
                          - jnp.dot(_hi_lo(e), tri_incl, preferred_element_type=F32))
                dz = e - jnp.exp2(lb2) * (e + prefix)
                if valid is not None:
                    dz = jnp.where(valid, dz, 0.0)
                dz_scr[hd * tq + row0:(hd + 1) * tq, :] = dz.astype(CD)
                new_run.append(run[hd] + pad_rows(jnp.sum(l2, axis=1, keepdims=True), row0))
                new_erun.append(erun[hd] + pad_rows(jnp.sum(e, axis=1, keepdims=True), row0))
            return tuple(new_run), tuple(new_erun)

        def apply_grads(j, row0=0):
            rows = rows_of(j)
            kb = k_ref[0, rows, :]
            kcat = jnp.concatenate([jnp.where(m, kb, jnp.zeros_like(kb)) for m in head_masks], axis=0)
            dz_heads = [dz_scr[hd * tq + row0:(hd + 1) * tq, :] for hd in range(2)]
            w_heads = [w_scr[hd * tq + row0:(hd + 1) * tq, :] for hd in range(2)]
            q_t = jnp.concatenate([q_scaled_t[:, hd * tq + row0:(hd + 1) * tq] for hd in range(2)], axis=1)
            do_t = jnp.concatenate([docat_t[:, hd * tq + row0:(hd + 1) * tq] for hd in range(2)], axis=1)
            dk_ref[0, :, rows] += jnp.dot(q_t, jnp.concatenate(dz_heads, axis=0), preferred_element_type=F32)
            dv_ref[0, :, rows] += jnp.dot(do_t, jnp.concatenate(w_heads, axis=0), preferred_element_type=F32)
            return pad_rows(jnp.dot(jnp.concatenate(dz_heads, axis=1), kcat, preferred_element_type=F32), row0)

        zero = jnp.zeros((tq, 1), F32)
        last = ATT_RATIO - 1
        start_products(ATT_RATIO * i + last, last * t)
        run, erun = logit_grads((zero, zero), (zero, zero), True, last * t)
        dqacc = jnp.zeros((tq, LANES), F32)
        for d in reversed(range(last)):
            start_products(ATT_RATIO * i + d, d * t)
            dqacc = dqacc + apply_grads(ATT_RATIO * i + d + 1, (d + 1) * t)
            run, erun = logit_grads(run, erun, True, d * t)
        start_products(jnp.maximum(ATT_RATIO * i - 1, 0))

        def step(jj, carry):
            run, erun, dqacc = carry
            b = ATT_RATIO * i - 1 - jj
            dqacc = dqacc + apply_grads(b + 1)
            run, erun = logit_grads(run, erun)
            start_products(jnp.maximum(b - 1, 0))
            return run, erun, dqacc

        run, erun, dqacc = lax.fori_loop(0, ATT_RATIO * i, step, (run, erun, dqacc))
        dq_ref[0] = (dqacc + apply_grads(0)) * ATT_SCALE

    qblk = pl.BlockSpec((1, tq, LANES), lambda p, i: (p, i, 0))
    full = pl.BlockSpec((1, LANES, s), lambda p, i: (p, 0, 0))
    shape = jax.ShapeDtypeStruct((N_PAIRS, s, LANES), F32)
    shape_t = jax.ShapeDtypeStruct((N_PAIRS, LANES, s), F32)
    dq, dk_t, dv_t = pl.pallas_call(
        body, name=name, out_shape=(shape, shape_t, shape_t), grid=(N_PAIRS, nblk),
        in_specs=[qblk,
                  pl.BlockSpec((1, s, LANES), lambda p, i: (N_PAIRS + p, 0, 0)),
                  pl.BlockSpec((1, s, LANES), lambda p, i: (2 * N_PAIRS + p, 0, 0)),
                  qblk, qblk],
        out_specs=(qblk, full, full),
        scratch_shapes=[pltpu.VMEM((2, tq, t), F32), pltpu.VMEM((2, tq, t), F32),
                        pltpu.VMEM((2 * tq, t), CD), pltpu.VMEM((2 * tq, t), CD)],
        compiler_params=_params("parallel", "arbitrary"),
    )(qkv, qkv, qkv, o, do)
    return dq, jnp.swapaxes(dk_t, 1, 2), jnp.swapaxes(dv_t, 1, 2)


def _adamw(w, g, m, v, name):
    shape = w.shape
    rows, cols = (shape[-2], shape[-1]) if len(shape) >= 2 else (1, shape[-1])
    lead = w.size // (rows * cols)
    tr = _pick(rows, (512, 256, 128, 64, 32, 16, 8))

    def body(w_ref, g_ref, m_ref, v_ref, d_ref, nm_ref, nv_ref):
        gv = g_ref[...]
        nm = ADAM_B1 * m_ref[...] + (1.0 - ADAM_B1) * gv
        nv = ADAM_B2 * v_ref[...] + (1.0 - ADAM_B2) * (gv * gv)
        m_hat = nm / (1.0 - ADAM_B1 ** ADAM_STEP)
        v_hat = nv / (1.0 - ADAM_B2 ** ADAM_STEP)
        d_ref[...] = -ADAM_LR * (m_hat / (jnp.sqrt(v_hat) + ADAM_EPS) + ADAM_WD * w_ref[...])
        nm_ref[...] = nm
        nv_ref[...] = nv

    blk = pl.BlockSpec((1, tr, cols), lambda l, i: (l, i, 0))
    out = jax.ShapeDtypeStruct((lead, rows, cols), F32)
    d, nm, nv = pl.pallas_call(
        body, name=name, out_shape=(out, out, out), grid=(lead, rows // tr),
        in_specs=[blk, blk, blk, blk], out_specs=(blk, blk, blk), compiler_params=_params("parallel", "parallel"),
    )(*[a.reshape(lead, rows, cols) for a in (w, g, m, v)])
    return d.reshape(shape), nm.reshape(shape), nv.reshape(shape)


HBM = pl.BlockSpec(memory_space=pltpu.HBM)


def _coords():
    return lax.axis_index("x"), lax.axis_index("y"), lax.axis_index("c")


def _other_chips(x, y):
    return [(1 - x, y), (x, 1 - y), (1 - x, 1 - y)]


def _allgather_chips(shard, name, collective_id=None):
    r, cols = shard.shape
    half = r // 2
    quarter = half // 2

    def body(src_ref, out_ref, send_sems, recv_sems):
        x, y, c = _coords()
        sibling = (x, y, 1 - c)
        nx, ny, diag = (1 - x, y), (x, 1 - y), (1 - x, 1 - y)

        def piece(chip, core, lo, n):
            return out_ref.at[2 * chip[0] + chip[1], pl.ds(core * half + lo, n), :]

        def copy(k, dst, to, src=None):
            return pltpu.make_async_remote_copy(
                src_ref=dst if src is None else src, dst_ref=dst,
                send_sem=send_sems.at[k], recv_sem=recv_sems.at[k], device_id=to, device_id_type=MESH)

        me = (x, y)
        mine = src_ref.at[pl.ds(c * half, half), :]
        direct = [copy(0, piece(me, c, 0, half), (*nx, c), src=mine), copy(1, piece(me, c, 0, half), (*ny, c), src=mine)]
        for cp in direct:
            cp.start()
        arrivals = [piece(nx, c, 0, half), piece(ny, c, 0, half), piece(diag, c, 0, quarter),
                    piece(diag, c, quarter, quarter)]
        onward = [copy(2, piece(nx, c, 0, quarter), (*ny, c)), copy(3, piece(ny, c, quarter, quarter), (*nx, c))]
        to_sibling = [copy(4 + k, dst, sibling) for k, dst in enumerate(arrivals)]
        for k, dst in enumerate(arrivals):
            copy(k, dst, (x, y, c)).wait_recv()
            if k < 2:
                onward[k].start()
            to_sibling[k].start()
        from_sibling = [piece(nx, 1 - c, 0, half), piece(ny, 1 - c, 0, half), piece(diag, 1 - c, 0, quarter),
                        piece(diag, 1 - c, quarter, quarter)]
        for k, dst in enumerate(from_sibling):
            copy(4 + k, dst, (x, y, c)).wait_recv()
        for cp in direct + onward + to_sibling:
            cp.wait_send()

    out_shape = jax.ShapeDtypeStruct((N_CHIPS, r, cols), shard.dtype)
    sems = (pltpu.SemaphoreType.DMA((8,)), pltpu.SemaphoreType.DMA((8,)))
    if collective_id is None:
        return pl.pallas_call(body, name=name, out_shape=out_shape, in_specs=[HBM], out_specs=HBM,
                              scratch_shapes=list(sems))(shard)
    shard_ref = jax.new_ref(shard, memory_space=pltpu.MemorySpace.HBM)
    gathered_ref = jax.empty_ref(out_shape, memory_space=pltpu.MemorySpace.HBM)

    @_sequencer(name, collective_id, sems)
    def launch(send_sems, recv_sems):
        x, y, c = _coords()
        _handshake([(1 - x, y, c), (x, 1 - y, c), (x, y, 1 - c)])
        body(shard_ref, gathered_ref, send_sems, recv_sems)

    launch()
    return gathered_ref[...]


def _exchange_sibling_halves(g, name):
    n, r, cols = g.shape
    half = r // 2

    def body(g_ref, out_ref, send_sem, recv_sem):
        x, y, c = _coords()
        cp = pltpu.make_async_remote_copy(
            src_ref=g_ref.at[:, pl.ds((1 - c) * half, half), :], dst_ref=out_ref,
            send_sem=send_sem, recv_sem=recv_sem, device_id=(x, y, 1 - c), device_id_type=MESH)
        cp.start()
        cp.wait()

    return pl.pallas_call(
        body, name=name, out_shape=jax.ShapeDtypeStruct((n, half, cols), g.dtype),
        in_specs=[HBM], out_specs=HBM,
        scratch_shapes=[pltpu.SemaphoreType.DMA, pltpu.SemaphoreType.DMA],
    )(g)


def _sequencer(name, collective_id, scratch_types):
    return pl.kernel(mesh=plsc.ScalarSubcoreMesh(axis_name="sequencer", num_cores=1), name=name,
                     scratch_types=scratch_types, compiler_params=pltpu.CompilerParams(collective_id=collective_id))


def _handshake(peers):
    barrier = pltpu.get_barrier_semaphore()
    for peer in peers:
        pl.semaphore_signal(barrier, inc=1, device_id=peer, device_id_type=MESH)
    pl.semaphore_wait(barrier, len(peers))


def _exchange_sibling_halves_async(g, name, collective_id):
    n, r, cols = g.shape
    half = r // 2
    g_ref = jax.new_ref(g, memory_space=pltpu.MemorySpace.HBM)
    out_ref = jax.empty_ref(jax.ShapeDtypeStruct((n, half, cols), g.dtype), memory_space=pltpu.MemorySpace.HBM)

    @_sequencer(name, collective_id, (pltpu.SemaphoreType.DMA, pltpu.SemaphoreType.DMA))
    def launch(send_sem, recv_sem):
        x, y, c = _coords()
        _handshake([(x, y, 1 - c)])
        cp = pltpu.make_async_remote_copy(
            src_ref=g_ref.at[:, pl.ds((1 - c) * half, half), :], dst_ref=out_ref,
            send_sem=send_sem, recv_sem=recv_sem, device_id=(x, y, 1 - c), device_id_type=MESH)
        cp.start()
        cp.wait()

    launch()
    return out_ref[...]


def _share_halves_async(v, name, collective_id):
    h = v.shape[0] // 2
    v_ref = jax.new_ref(v, memory_space=pltpu.MemorySpace.HBM)

    @_sequencer(name, collective_id, (pltpu.SemaphoreType.DMA, pltpu.SemaphoreType.DMA))
    def launch(send_sem, recv_sem):
        x, y, c = _coords()
        _handshake([(x, y, 1 - c)])
        cp = pltpu.make_async_remote_copy(
            src_ref=v_ref.at[pl.ds(c * h, h), :], dst_ref=v_ref.at[pl.ds(c * h, h), :],
            send_sem=send_sem, recv_sem=recv_sem, device_id=(x, y, 1 - c), device_id_type=MESH)
        cp.start()
        pltpu.make_async_remote_copy(
            src_ref=v_ref.at[pl.ds(c * h, h), :], dst_ref=v_ref.at[pl.ds((1 - c) * h, h), :],
            send_sem=send_sem, recv_sem=recv_sem, device_id=(x, y, 1 - c), device_id_type=MESH).wait_recv()
        cp.wait_send()

    launch()
    return v_ref[...]


def _scatter_to_chips_async(p, name, collective_id):
    p_ref = jax.new_ref(p, memory_space=pltpu.MemorySpace.HBM)
    out_ref = jax.empty_ref(jax.ShapeDtypeStruct(p.shape, p.dtype), memory_space=pltpu.MemorySpace.HBM)

    @_sequencer(name, collective_id, (pltpu.SemaphoreType.DMA((3,)), pltpu.SemaphoreType.DMA((3,))))
    def launch(send_sems, recv_sems):
        x, y, c = _coords()
        me = 2 * x + y
        _handshake([(px, py, c) for px, py in _other_chips(x, y)])
        sends = []
        for j, (px, py) in enumerate(_other_chips(x, y)):
            sends.append(pltpu.make_async_remote_copy(
                src_ref=p_ref.at[2 * px + py], dst_ref=out_ref.at[me],
                send_sem=send_sems.at[j], recv_sem=recv_sems.at[j], device_id=(px, py, c), device_id_type=MESH))
        for cp in sends:
            cp.start()
        for j, (px, py) in enumerate(_other_chips(x, y)):
            pltpu.make_async_remote_copy(
                src_ref=p_ref.at[me], dst_ref=out_ref.at[2 * px + py],
                send_sem=send_sems.at[j], recv_sem=recv_sems.at[j], device_id=(px, py, c),
                device_id_type=MESH).wait_recv()
        for cp in sends:
            cp.wait_send()

    launch()
    return out_ref[...]


def _share_halves(v, name):
    h = v.shape[0] // 2

    def body(v_ref, out_ref, send_sem, recv_sem):
        x, y, c = _coords()
        cp = pltpu.make_async_remote_copy(
            src_ref=v_ref.at[pl.ds(c * h, h), :], dst_ref=out_ref.at[pl.ds(c * h, h), :],
            send_sem=send_sem, recv_sem=recv_sem, device_id=(x, y, 1 - c), device_id_type=MESH)
        cp.start()
        pltpu.make_async_remote_copy(
            src_ref=v_ref.at[pl.ds(c * h, h), :], dst_ref=out_ref.at[pl.ds((1 - c) * h, h), :],
            send_sem=send_sem, recv_sem=recv_sem, device_id=(x, y, 1 - c), device_id_type=MESH).wait_recv()
        cp.wait_send()

    return pl.pallas_call(
        body, name=name, out_shape=jax.ShapeDtypeStruct(v.shape, v.dtype),
        in_specs=[HBM], out_specs=HBM, input_output_aliases={0: 0},
        scratch_shapes=[pltpu.SemaphoreType.DMA, pltpu.SemaphoreType.DMA],
    )(v)


def _allreduce_small(v, name):
    r, cols = v.shape

    def body(v_ref, out_ref, buf_ref, send_sems, recv_sems):
        x, y, c = _coords()
        me = 4 * x + 2 * y + c
        buf_ref[me] = v_ref[...]
        sends = []
        for k in range(1, N_DEV):
            px = 1 - x if k & 4 else x
            py = 1 - y if k & 2 else y
            pc = 1 - c if k & 1 else c
            sends.append(pltpu.make_async_remote_copy(
                src_ref=v_ref, dst_ref=buf_ref.at[me], send_sem=send_sems.at[k - 1], recv_sem=recv_sems.at[k - 1],
                device_id=(px, py, pc), device_id_type=MESH))
        for cp in sends:
            cp.start()
        for cp in sends:
            cp.wait()
        acc = buf_ref[0]
        for d in range(1, N_DEV):
            acc = acc + buf_ref[d]
        out_ref[...] = acc

    return pl.pallas_call(
        body, name=name, out_shape=jax.ShapeDtypeStruct((r, cols), F32),
        in_specs=[pl.BlockSpec(memory_space=pltpu.VMEM)], out_specs=pl.BlockSpec(memory_space=pltpu.VMEM),
        scratch_shapes=[pltpu.VMEM((N_DEV, r, cols), F32), pltpu.SemaphoreType.DMA((N_DEV - 1,)),
                        pltpu.SemaphoreType.DMA((N_DEV - 1,))],
    )(v)


def _add_sibling(g, from_sibling, core, name):
    n, h, cols = from_sibling.shape
    tr = _row_tile(h)
    steps = h // tr

    def body(core_ref, a_ref, b_ref, o_ref):
        o_ref[...] = (a_ref[...] + b_ref[...]).astype(o_ref.dtype)

    return pl.pallas_call(
        body, name=name, out_shape=jax.ShapeDtypeStruct(from_sibling.shape, jnp.bfloat16),
        grid_spec=pltpu.PrefetchScalarGridSpec(
            num_scalar_prefetch=1, grid=(n, steps),
            in_specs=[pl.BlockSpec((1, tr, cols), lambda s, i, core_ref: (s, core_ref[0] * steps + i, 0)),
                      pl.BlockSpec((1, tr, cols), lambda s, i, core_ref: (s, i, 0))],
            out_specs=pl.BlockSpec((1, tr, cols), lambda s, i, core_ref: (s, i, 0))),
        compiler_params=_params("parallel", "parallel"),
    )(core.reshape(1).astype(jnp.int32), g, from_sibling)


def _sum_slots(p, own, chip, core, name):
    n, r, cols = p.shape
    tr = _row_tile(r)
    steps = r // tr

    def body(core_ref, chip_ref, p_ref, own_ref, o_ref):
        parts = [jnp.where(chip_ref[0] == s, own_ref[0], p_ref[s]).astype(F32) for s in range(n)]
        o_ref[...] = ((parts[0] + parts[1]) + parts[2]) + parts[3]

    return pl.pallas_call(
        body, name=name, out_shape=jax.ShapeDtypeStruct((2 * r, cols), F32),
        grid_spec=pltpu.PrefetchScalarGridSpec(
            num_scalar_prefetch=2, grid=(steps,),
            in_specs=[pl.BlockSpec((n, tr, cols), lambda i, core_ref, chip_ref: (0, i, 0)),
                      pl.BlockSpec((1, tr, cols), lambda i, core_ref, chip_ref: (chip_ref[0], i, 0))],
            out_specs=pl.BlockSpec((tr, cols), lambda i, core_ref, chip_ref: (core_ref[0] * steps + i, 0))),
        compiler_params=_params("parallel"),
    )(core.reshape(1).astype(jnp.int32), chip.reshape(1).astype(jnp.int32), p, own)


PACK_COLS = 1024


def _pack_shards(parts):
    return jnp.concatenate([p.reshape(-1, PACK_COLS) for p in parts], axis=0)


def _unpack_shards(buf, shapes):
    out, row = [], 0
    for shp in shapes:
        nrows = math.prod(shp) // PACK_COLS
        out.append(buf[..., row:row + nrows, :].reshape(buf.shape[:-2] + tuple(shp)))
        row += nrows
    return out


def _local_step(x, target, w):
    t = lambda a: a.T
    g = {}
    w_in_g, w_in_x = w["a_w_in"][:, :D_RNN], w["a_w_in"][:, D_RNN:]
    h0, gate_br, x_br = _norm_and_project(x, w["norm_mix_g"][0], w_in_g, w_in_x, "rglru_in")
    y_a, hs = _rglru_fwd(gate_br, x_br, w["a_conv_w"], w["a_conv_b"], w["a_w_r"], w["a_w_i"], w["a_b_r"],
                         w["a_b_i"], w["a_lambda"], "rglru_fwd")
    x1, h1 = _matmul([(y_a, w["a_w_out"])], F32, "mm_a_out", addend=x, norm_gain=w["norm_ffn_g"][0])
    fg0, fu0, act0 = _ffn_up(h1, w["ffn_w_gate"][0], w["ffn_w_up"][0], "ffn0_up")
    x2, h2 = _matmul([(act0, w["ffn_w_down"][0])], F32, "mm_f0_down", addend=x1, norm_gain=w["norm_mix_g"][1],
                     tk=D_FF)
    qkv = _matmul([(h2, w["b_w_qkv"])], CD, "mm_b_qkv", out_lbm=True, tn=3 * D_MODEL)
    o = _attn_fwd(qkv, "attn_fwd")
    x3, h3 = _matmul([(o, w["b_w_out"])], F32, "mm_b_out", a_lbm=True, addend=x2, norm_gain=w["norm_ffn_g"][1])
    fg1, fu1, act1 = _ffn_up(h3, w["ffn_w_gate"][1], w["ffn_w_up"][1], "ffn1_up")
    dx4, dx4c, g["final_g"], loss = _matmul([(act1, w["ffn_w_down"][1])], F32, "mm_f1_down", addend=x3,
                                            loss_head=(w["final_g"], target), tk=D_FF)

    def ffn_bwd(dx_out, dxc, h, x_in, fg, fu, act, layer, tag):
        dg, du = _ffn_dact(dxc, t(w["ffn_w_down"][layer]), fg, fu, "ffn_" + tag + "_dact")
        dwd = _matmul([(act, dxc)], F32, "mm_" + tag + "_dwd", trans_a=True)
        dwg = _matmul([(h, dg)], F32, "mm_" + tag + "_dwg", trans_a=True, tm=D_MODEL)
        dwu = _matmul([(h, du)], F32, "mm_" + tag + "_dwu", trans_a=True, tm=D_MODEL)
        dx_in, dx_in_c, dgain = _matmul([(dg, t(w["ffn_w_gate"][layer])), (du, t(w["ffn_w_up"][layer]))], F32,
                                        "mm_" + tag + "_dh", norm_bwd=(x_in, w["norm_ffn_g"][layer], dx_out),
                                        tk=D_FF, tm=256)
        return dx_in, dx_in_c, dgain, dwg, dwu, dwd

    dx3, dx3c, dgf1, dwg1, dwu1, dwd1 = ffn_bwd(dx4, dx4c, h3, x3, fg1, fu1, act1, 1, "f1")
    do = _matmul([(dx3c, t(w["b_w_out"]))], F32, "mm_b_do", out_lbm=True, tn=1024)
    g["b_w_out"] = _matmul([(o, dx3c)], F32, "mm_b_dwout", trans_a=True, a_lbm=True)
    dq, dk, dv = _attn_bwd(qkv, o, do, "attn_bwd")
    wq_t = t(w["b_w_qkv"])
    parts = (dq, dk, dv)
    g["b_w_qkv"] = jnp.concatenate(
        [_matmul([(h2, p)], F32, "mm_b_dwqkv%d" % n, trans_a=True, b_lbm=True) for n, p in enumerate(parts)], axis=1)
    dx2, dx2c, dgm1 = _matmul([(p, wq_t[n * D_MODEL:(n + 1) * D_MODEL]) for n, p in enumerate(parts)], F32, "mm_b_dh",
                              a_lbm=True, norm_bwd=(x2, w["norm_mix_g"][1], dx3))
    dx1, dx1c, dgf0, dwg0, dwu0, dwd0 = ffn_bwd(dx2, dx2c, h1, x1, fg0, fu0, act0, 0, "f0")
    dy_a = _matmul([(dx1c, t(w["a_w_out"]))], F32, "mm_a_dy")
    g["a_w_out"] = _matmul([(y_a, dx1c)], F32, "mm_a_dwout", trans_a=True)
    wrt = jnp.swapaxes(w["a_w_r"], 1, 2)
    wit = jnp.swapaxes(w["a_w_i"], 1, 2)
    (dgate, dxbr, g["a_conv_w"], g["a_conv_b"], g["a_b_r"], g["a_b_i"], g["a_lambda"], g["a_w_r"],
     g["a_w_i"]) = _rglru_bwd(dy_a, gate_br, x_br, hs, w["a_conv_w"], w["a_conv_b"], w["a_w_r"], w["a_w_i"], wrt, wit,
                              w["a_b_r"], w["a_b_i"], w["a_lambda"], "rglru_bwd")
    g["a_w_in"] = jnp.concatenate([_matmul([(h0, dgate)], F32, "mm_a_dwin_g", trans_a=True),
                                   _matmul([(h0, dxbr)], F32, "mm_a_dwin_x", trans_a=True)], axis=1)
    dx0, _, dgm0 = _matmul([(dgate, t(w_in_g)), (dxbr, t(w_in_x))], F32, "mm_a_dh",
                           norm_bwd=(x, w["norm_mix_g"][0], dx1))
    g["norm_mix_g"] = jnp.concatenate([dgm0, dgm1], axis=0)
    g["norm_ffn_g"] = jnp.concatenate([dgf0, dgf1], axis=0)
    g["ffn_w_gate"] = [dwg0, dwg1]
    g["ffn_w_up"] = [dwu0, dwu1]
    g["ffn_w_down"] = [dwd0, dwd1]
    return loss, dx0, g


WEIGHTS = ["norm_mix_g", "norm_ffn_g", "a_w_in", "a_conv_w", "a_conv_b", "a_w_r", "a_b_r", "a_w_i", "a_b_i",
           "a_lambda", "a_w_out", "b_w_qkv", "b_w_out", "ffn_w_gate", "ffn_w_up", "ffn_w_down", "final_g"]
BIG = [("a_w_in", 2), ("a_w_r", 2), ("a_w_i", 2), ("a_w_out", 1), ("b_w_qkv", 2), ("b_w_out", 1),
       ("ffn_w_gate", 2), ("ffn_w_up", 2), ("ffn_w_down", 1)]
LAYER1 = ["b_w_qkv", "b_w_out", "ffn_w_gate", "ffn_w_up", "ffn_w_down"]
LAYER0 = ["a_w_in", "a_w_r", "a_w_i", "a_w_out", "ffn_w_gate", "ffn_w_up", "ffn_w_down"]
RS_COLLECTIVE_IDS = {"chips1": 3, "chips0": 4, "sibling1": 5, "share1": 6}
GATHER_COLLECTIVE_IDS = (8, 7)
SMALL = ["norm_mix_g", "norm_ffn_g", "a_conv_w", "a_conv_b", "a_b_r", "a_b_i", "a_lambda", "final_g"]


def _split_chips(full, axis):
    if axis == 1:
        return full.reshape((N_CHIPS, 1, full.shape[1] // N_CHIPS) + full.shape[2:])
    return jnp.stack(jnp.split(full, N_CHIPS, axis=axis))


def _step(x, target, weights, moments_m, moments_v):
    chip = 2 * lax.axis_index("x") + lax.axis_index("y")
    core = lax.axis_index("c")
    axis_of = dict(BIG)
    full = {}
    for group, layer, tag, collective_id in ((LAYER0[:4], 0, "0a", None), (LAYER0[4:], 0, "0f", GATHER_COLLECTIVE_IDS[0]),
                                             (LAYER1, 1, "1", GATHER_COLLECTIVE_IDS[1])):
        shards = [weights[n][layer % weights[n].shape[0]].astype(CD) for n in group]
        packed = _pack_shards(shards)
        if collective_id is not None:
            packed, first_gathered = lax.optimization_barrier((packed, first_gathered))
        gathered = _allgather_chips(packed, "allgather_weights" + tag, collective_id)
        if collective_id is None:
            first_gathered = gathered
        for n, own, stack in zip(group, shards, _unpack_shards(gathered, [sh.shape for sh in shards])):
            joined = jnp.concatenate([jnp.where(chip == s, own, stack[s]) for s in range(N_CHIPS)],
                                     axis=axis_of[n] - 1)
            full.setdefault(n, {})[layer] = joined
    full = {n: (v[0] if n.startswith("a_") else v[1] if n.startswith("b_") else [v[0], v[1]]) for n, v in full.items()}
    cw_rows = jnp.zeros((N_CHIPS, CONV_W, RG_BW), F32)
    cw_rows = lax.dynamic_update_slice(cw_rows, jnp.where(core == 0, weights["a_conv_w"], 0.0), (chip, 0, 0))
    cw_all = _allreduce_small(cw_rows.reshape(-1, LANES), "allgather_conv_w").reshape(N_CHIPS, CONV_W, RG_BW)
    full["a_conv_w"] = jnp.concatenate([cw_all[s] for s in range(N_CHIPS)], axis=1)
    for n in ("norm_mix_g", "norm_ffn_g", "final_g"):
        full[n] = weights[n]
    for n in ("a_conv_b", "a_b_r", "a_b_i", "a_lambda"):
        full[n] = weights[n]
    loss, dx, grads = _local_step(x[0], target[0], full)
    small_parts = [grads[n].reshape(-1) for n in SMALL] + [loss.reshape(-1)]
    sizes = [p.shape[0] for p in small_parts]
    small = _allreduce_small(jnp.concatenate(small_parts).reshape(-1, LANES), "allreduce_small").reshape(-1)
    red, pos = {}, 0
    for n, sz in zip(SMALL + ["loss"], sizes):
        red[n] = small[pos:pos + sz]
        pos += sz
    loss_out = red["loss"][0]
    g_out = {}
    for n in SMALL:
        if n == "a_conv_w":
            g_out[n] = lax.dynamic_slice(red[n].reshape(CONV_W, D_RNN), (0, chip * RG_BW), (CONV_W, RG_BW)).reshape(
                weights[n].shape)
        else:
            g_out[n] = red[n].reshape(weights[n].shape)
    axis_of = dict(BIG)
    pieces = {}
    for group, layer, tag in ((LAYER1, 1, "1"), (LAYER0, 0, "0")):
        stacks, shapes = [], []
        for n in group:
            per_layer = isinstance(grads[n], list)
            gfull = grads[n][layer] if per_layer else grads[n]
            shard_shape = weights[n].shape[1:]
            gfull = gfull.reshape((1,) + gfull.shape)
            stacks.append(_split_chips(gfull, axis_of[n]).reshape(N_CHIPS, -1, PACK_COLS))
            shapes.append((1,) + tuple(shard_shape))
        gbuf = jnp.concatenate(stacks, axis=1)
        if layer == 1:
            from_sibling = _exchange_sibling_halves_async(gbuf, "rs_sibling" + tag, RS_COLLECTIVE_IDS["sibling1"])
        else:
            from_sibling = _exchange_sibling_halves(gbuf, "rs_sibling" + tag)
        chip_partial = _add_sibling(gbuf, from_sibling, core, "rs_add" + tag)
        from_chips = _scatter_to_chips_async(chip_partial, "rs_chips" + tag, RS_COLLECTIVE_IDS["chips" + tag])
        halves = _sum_slots(from_chips, chip_partial, chip, core, "rs_sum" + tag)
        if layer == 1:
            reduced = _share_halves_async(halves, "rs_share" + tag, RS_COLLECTIVE_IDS["share1"])
        else:
            reduced = _share_halves(halves, "rs_share" + tag)
        for n, piece in zip(group, _unpack_shards(reduced, shapes)):
            pieces.setdefault(n, {})[layer] = piece
    for n, _ in BIG:
        layers = pieces[n]
        g_out[n] = jnp.concatenate([layers[k] for k in sorted(layers)], axis=0)
    updates = {}
    for n, _ in BIG:
        updates[n] = _adamw(weights[n], g_out[n], moments_m[n], moments_v[n], "adamw_" + n)
    rows = lambda d: jnp.concatenate([d[n].reshape(-1, D_MODEL) for n in SMALL], axis=0)
    small_updates = _adamw(rows(weights), rows(g_out), rows(moments_m), rows(moments_v), "adamw_small")
    pos = 0
    for n in SMALL:
        nrows = weights[n].size // D_MODEL
        updates[n] = tuple(u[pos:pos + nrows].reshape(weights[n].shape) for u in small_updates)
        pos += nrows
    outs_g = [g_out[n] for n in WEIGHTS]
    outs_d, outs_m, outs_v = ([updates[n][k] for n in WEIGHTS] for k in range(3))
    return (loss_out, dx[None], *outs_g, *outs_d, *outs_m, *outs_v)


def kernel(x, norm_mix_g, norm_ffn_g, a_w_in, a_conv_w, a_conv_b, a_w_r, a_b_r, a_w_i, a_b_i, a_lambda, a_w_out, b_w_qkv, b_w_out, ffn_w_gate, ffn_w_up, ffn_w_down, final_g, loss_target, m_norm_mix_g, m_norm_ffn_g, m_a_w_in, m_a_conv_w, m_a_conv_b, m_a_w_r, m_a_b_r, m_a_w_i, m_a_b_i, m_a_lambda, m_a_w_out, m_b_w_qkv, m_b_w_out, m_ffn_w_gate, m_ffn_w_up, m_ffn_w_down, m_final_g, v_norm_mix_g, v_norm_ffn_g, v_a_w_in, v_a_conv_w, v_a_conv_b, v_a_w_r, v_a_b_r, v_a_w_i, v_a_b_i, v_a_lambda, v_a_w_out, v_b_w_qkv, v_b_w_out, v_ffn_w_gate, v_ffn_w_up, v_ffn_w_down, v_final_g):
    ws = [norm_mix_g, norm_ffn_g, a_w_in, a_conv_w, a_conv_b, a_w_r, a_b_r, a_w_i, a_b_i, a_lambda, a_w_out, b_w_qkv,
          b_w_out, ffn_w_gate, ffn_w_up, ffn_w_down, final_g]
    ms = [m_norm_mix_g, m_norm_ffn_g, m_a_w_in, m_a_conv_w, m_a_conv_b, m_a_w_r, m_a_b_r, m_a_w_i, m_a_b_i, m_a_lambda,
          m_a_w_out, m_b_w_qkv, m_b_w_out, m_ffn_w_gate, m_ffn_w_up, m_ffn_w_down, m_final_g]
    vs = [v_norm_mix_g, v_norm_ffn_g, v_a_w_in, v_a_conv_w, v_a_conv_b, v_a_w_r, v_a_b_r, v_a_w_i, v_a_b_i, v_a_lambda,
          v_a_w_out, v_b_w_qkv, v_b_w_out, v_ffn_w_gate, v_ffn_w_up, v_ffn_w_down, v_final_g]
    return _step(x, loss_target, dict(zip(WEIGHTS, ws)), dict(zip(WEIGHTS, ms)), dict(zip(WEIGHTS, vs)))
```

```python
import functools
import math

import jax
import jax.numpy as jnp
from jax import lax
from jax.experimental import pallas as pl
from jax.experimental.pallas import tpu as pltpu
from jax.experimental.pallas import tpu_sc as plsc

F32 = jnp.float32
CD = jnp.bfloat16

D_MODEL = 1024
D_RNN = 1024
RG_BLOCKS = 4
RG_BW = 256
CONV_W = 4
RG_C = 8.0
SB_HEADS = 16
SB_HEAD_DIM = 64
D_FF = 2816
RMS_EPS = 1e-6
N_CHIPS = 4
N_DEV = 8

ADAM_LR = 0.001
ADAM_B1 = 0.9
ADAM_B2 = 0.999
ADAM_EPS = 1e-08
ADAM_WD = 0.01
ADAM_STEP = 10

LANES = 128
VMEM_LIMIT = 56 * 1024 * 1024
MESH = pl.DeviceIdType.MESH


def _params(*sem):
    return pltpu.CompilerParams(dimension_semantics=sem, vmem_limit_bytes=VMEM_LIMIT)


def _pick(n, prefs):
    for p in prefs:
        if n % p == 0:
            return p
    return n


def _row_tile(rows):
    return max(d for d in range(16, 1025, 16) if rows % d == 0)


def _matmul(pairs, out_dtype, name, *, trans_a=False, a_lbm=False, b_lbm=False, out_lbm=False, addend=None,
            tm=512, tn=None, tk=None, norm_gain=None, norm_bwd=None, loss_head=None):
    a0, b0 = pairs[0]
    if trans_a:
        kdim = a0.shape[1] if a_lbm else a0.shape[0]
        m = a0.shape[0] * LANES if a_lbm else a0.shape[1]
    else:
        m = a0.shape[1] if a_lbm else a0.shape[0]
        kdim = a0.shape[0] * LANES if a_lbm else a0.shape[1]
    n = b0.shape[0] * LANES if b_lbm else b0.shape[1]
    tm = _pick(m, (tm, 1408, 256, 128))
    tn = tn or _pick(n, (1408, 1024, 768, 512, 256, 128))
    tk = tk or _pick(kdim, (1024, 1408, 512, 256, 128))
    nk = kdim // tk
    npair = len(pairs)

    def cat(ref):
        return jnp.concatenate([ref[p] for p in range(ref.shape[0])], axis=-1)

    def body(*refs):
        ins = refs[: 2 * npair]
        pos = 2 * npair
        add_ref = None
        if addend is not None:
            add_ref = refs[pos]
            pos += 1
        gain_ref = x_ref = dxin_ref = None
        if norm_gain is not None:
            gain_ref = refs[pos]
            pos += 1
        if norm_bwd is not None:
            x_ref, gain_ref, dxin_ref = refs[pos:pos + 3]
            pos += 3
        if loss_head is not None:
            gain_ref, target_ref = refs[pos:pos + 2]
            pos += 2
        o_ref = refs[pos]
        extra_out = refs[pos + 1:-1]
        acc_ref = refs[-1]
        k = pl.program_id(2)

        @pl.when(k == 0)
        def _():
            acc_ref[...] = jnp.zeros_like(acc_ref)

        if norm_bwd is not None or loss_head is not None:
            @pl.when((k == 0) & (pl.program_id(0) == 0))
            def _():
                for ref in extra_out[1:]:
                    ref[...] = jnp.zeros_like(ref)

        acc = acc_ref[...]
        for p in range(npair):
            a = (cat(ins[2 * p]) if a_lbm else ins[2 * p][...]).astype(CD)
            b = (cat(ins[2 * p + 1]) if b_lbm else ins[2 * p + 1][...]).astype(CD)
            dims = (((0,), (0,)), ((), ())) if trans_a else (((1,), (0,)), ((), ()))
            acc = acc + lax.dot_general(a, b, dims, preferred_element_type=F32)
        acc_ref[...] = acc

        @pl.when(k == nk - 1)
        def _():
            res = acc_ref[...]
            if add_ref is not None:
                res = res + add_ref[...]
            if norm_gain is not None:
                rinv = lax.rsqrt(jnp.mean(res * res, axis=-1, keepdims=True) + RMS_EPS)
                extra_out[0][...] = (res * rinv * gain_ref[...]).astype(CD)
            if norm_bwd is not None:
                xv = x_ref[...]
                rinv = lax.rsqrt(jnp.mean(xv * xv, axis=-1, keepdims=True) + RMS_EPS)
                nrm = xv * rinv
                dn = res * gain_ref[...]
                extra_out[1][...] += jnp.sum(res * nrm, axis=0, keepdims=True)
                res = dxin_ref[...] + rinv * (dn - nrm * jnp.mean(dn * nrm, axis=-1, keepdims=True))
                extra_out[0][...] = res.astype(CD)
            if loss_head is not None:
                gv = gain_ref[...]
                rinv = lax.rsqrt(jnp.mean(res * res, axis=-1, keepdims=True) + RMS_EPS)
                nrm = res * rinv
                err = nrm * gv - target_ref[...]
                extra_out[2][...] += 0.5 * jnp.sum(jnp.mean(err * err, axis=-1, keepdims=True), axis=0, keepdims=True)
                dy = err * (1.0 / n)
                dn = dy * gv
                extra_out[1][...] += jnp.sum(dy * nrm, axis=0, keepdims=True)
                res = rinv * (dn - nrm * jnp.mean(dn * nrm, axis=-1, keepdims=True))
                extra_out[0][...] = res.astype(CD)
            res = res.astype(out_dtype)
            if out_lbm:
                for p in range(tn // LANES):
                    o_ref[p] = res[:, p * LANES:(p + 1) * LANES]
            else:
                o_ref[...] = res

    if trans_a:
        a_spec = (pl.BlockSpec((tm // LANES, tk, LANES), lambda i, j, k: (i, k, 0)) if a_lbm
                  else pl.BlockSpec((tk, tm), lambda i, j, k: (k, i)))
    else:
        a_spec = (pl.BlockSpec((tk // LANES, tm, LANES), lambda i, j, k: (k, i, 0)) if a_lbm
                  else pl.BlockSpec((tm, tk), lambda i, j, k: (i, k)))
    b_spec = (pl.BlockSpec((tn // LANES, tk, LANES), lambda i, j, k: (j, k, 0)) if b_lbm
              else pl.BlockSpec((tk, tn), lambda i, j, k: (k, j)))
    in_specs = []
    args = []
    for a, b in pairs:
        in_specs += [a_spec, b_spec]
        args += [a, b]
    if addend is not None:
        in_specs.append(pl.BlockSpec((tm, tn), lambda i, j, k: (i, j)))
        args.append(addend)
    tile = pl.BlockSpec((tm, tn), lambda i, j, k: (i, j))
    vec = pl.BlockSpec((1, tn), lambda i, j, k: (0, j))
    if out_lbm:
        out_shape = jax.ShapeDtypeStruct((n // LANES, m, LANES), out_dtype)
        out_spec = pl.BlockSpec((tn // LANES, tm, LANES), lambda i, j, k: (j, i, 0))
    else:
        out_shape = jax.ShapeDtypeStruct((m, n), out_dtype)
        out_spec = tile
    sem = ("parallel", "parallel", "arbitrary")
    if norm_gain is not None or norm_bwd is not None or loss_head is not None:
        assert tn == n and not out_lbm, "the norm needs whole rows in one tile"
        out_shape, out_spec = [out_shape, jax.ShapeDtypeStruct((m, n), CD)], [out_spec, tile]
    if norm_gain is not None:
        in_specs.append(vec)
        args.append(norm_gain.reshape(1, n))
    if norm_bwd is not None:
        x_in, gain, dx_in = norm_bwd
        in_specs += [tile, vec, tile]
        args += [x_in, gain.reshape(1, n), dx_in]
        out_shape.append(jax.ShapeDtypeStruct((1, n), F32))
        out_spec.append(vec)
        sem = ("arbitrary", "arbitrary", "arbitrary")
    if loss_head is not None:
        gain, target = loss_head
        in_specs += [vec, tile]
        args += [gain.reshape(1, n), target]
        out_shape += [jax.ShapeDtypeStruct((1, n), F32), jax.ShapeDtypeStruct((1, LANES), F32)]
        out_spec += [vec, pl.BlockSpec((1, LANES), lambda i, j, k: (0, 0))]
        sem = ("arbitrary", "arbitrary", "arbitrary")
    return pl.pallas_call(
        body, name=name, out_shape=out_shape, grid=(m // tm, n // tn, nk),
        in_specs=in_specs, out_specs=out_spec,
        scratch_shapes=[pltpu.VMEM((tm, tn), F32)],
        compiler_params=_params(*sem),
    )(*args)


def _norm_and_project(x, g, w_a, w_b, name):
    s, d = x.shape
    n = w_a.shape[1]
    tm = _pick(s, (512, 256))

    def body(x_ref, g_ref, wa_ref, wb_ref, h_ref, a_ref, b_ref):
        xv = x_ref[...]
        rinv = lax.rsqrt(jnp.mean(xv * xv, axis=-1, keepdims=True) + RMS_EPS)
        h = (xv * rinv * g_ref[...]).astype(CD)
        h_ref[...] = h
        a_ref[...] = jnp.dot(h, wa_ref[...], preferred_element_type=F32)
        b_ref[...] = jnp.dot(h, wb_ref[...], preferred_element_type=F32)

    row = pl.BlockSpec((tm, d), lambda i: (i, 0))
    out = pl.BlockSpec((tm, n), lambda i: (i, 0))
    wspec = pl.BlockSpec((d, n), lambda i: (0, 0))
    return pl.pallas_call(
        body, name=name,
        out_shape=(jax.ShapeDtypeStruct((s, d), CD), jax.ShapeDtypeStruct((s, n), F32),
                   jax.ShapeDtypeStruct((s, n), F32)),
        grid=(s // tm,), in_specs=[row, pl.BlockSpec((1, d), lambda i: (0, 0)), wspec, wspec],
        out_specs=(row, out, out), compiler_params=_params("parallel"),
    )(x, g.reshape(1, d), w_a, w_b)


def _sigmoid(z):
    return 1.0 / (1.0 + jnp.exp(-z))


FFN_TM = 512
FFN_TN = 1408


def _ffn_up(h, wg, wu, name):
    s, d = h.shape
    f = wg.shape[1]
    tm = _pick(s, (FFN_TM, 256))

    def body(h_ref, wg_ref, wu_ref, g_ref, u_ref, a_ref):
        hv = h_ref[...]
        gv = jnp.dot(hv, wg_ref[...], preferred_element_type=F32)
        uv = jnp.dot(hv, wu_ref[...], preferred_element_type=F32)
        g_ref[...] = gv
        u_ref[...] = uv
        a_ref[...] = (gv * _sigmoid(gv) * uv).astype(CD)

    a_spec = pl.BlockSpec((tm, d), lambda j, i: (i, 0))
    w_spec = pl.BlockSpec((d, FFN_TN), lambda j, i: (0, j))
    o_spec = pl.BlockSpec((tm, FFN_TN), lambda j, i: (i, j))
    return pl.pallas_call(
        body, name=name,
        out_shape=(jax.ShapeDtypeStruct((s, f), F32), jax.ShapeDtypeStruct((s, f), F32),
                   jax.ShapeDtypeStruct((s, f), CD)),
        grid=(f // FFN_TN, s // tm), in_specs=[a_spec, w_spec, w_spec], out_specs=(o_spec, o_spec, o_spec),
        compiler_params=_params("parallel", "parallel"),
    )(h, wg, wu)


def _ffn_dact(dxc, wd_t, g, u, name):
    s, d = dxc.shape
    f = wd_t.shape[1]
    tm = _pick(s, (FFN_TM, 256))

    def body(dx_ref, w_ref, g_ref, u_ref, dg_ref, du_ref):
        da = jnp.dot(dx_ref[...], w_ref[...], preferred_element_type=F32)
        gv = g_ref[...]
        sg = _sigmoid(gv)
        silu = gv * sg
        dg_ref[...] = (da * u_ref[...] * (sg + silu * (1.0 - sg))).astype(CD)
        du_ref[...] = (da * silu).astype(CD)

    a_spec = pl.BlockSpec((tm, d), lambda j, i: (i, 0))
    w_spec = pl.BlockSpec((d, FFN_TN), lambda j, i: (0, j))
    o_spec = pl.BlockSpec((tm, FFN_TN), lambda j, i: (i, j))
    return pl.pallas_call(
        body, name=name,
        out_shape=(jax.ShapeDtypeStruct((s, f), CD), jax.ShapeDtypeStruct((s, f), CD)),
        grid=(f // FFN_TN, s // tm), in_specs=[a_spec, w_spec, o_spec, o_spec], out_specs=(o_spec, o_spec),
        compiler_params=_params("parallel", "parallel"),
    )(dxc, wd_t, g, u)


TIME_BLOCK = 1024
SUBLANES = 8
GELU_C = math.sqrt(2.0 / math.pi)
GELU_A = 0.044715


def _gelu(x):
    return 0.5 * x * (1.0 + jnp.tanh(GELU_C * (x + GELU_A * x * x * x)))


def _gelu_grad(x):
    t = jnp.tanh(GELU_C * (x + GELU_A * x * x * x))
    return 0.5 * (1.0 + t) + 0.5 * x * (1.0 - t * t) * GELU_C * (1.0 + 3.0 * GELU_A * x * x)


def _neg_expm1(x):
    series = -x * (1.0 + x * (0.5 + x * (1.0 / 6.0 + x * (1.0 / 24.0))))
    return jnp.where(x > -0.05, series, 1.0 - jnp.exp(x))


def _log_sigmoid(x):
    return jnp.minimum(x, 0.0) - jnp.log1p(jnp.exp(-jnp.abs(x)))


def _shift_down(x, tail, s):
    if s == 0:
        return x
    ext = jnp.concatenate([tail, x], axis=0)
    return pltpu.roll(ext, s, axis=0)[SUBLANES:]


def _shift_up(x, head, s):
    if s == 0:
        return x
    n = x.shape[0]
    ext = jnp.concatenate([x, head], axis=0)
    return pltpu.roll(ext, n + SUBLANES - s, axis=0)[:n]


def _rg_gates(xbr, tail, cw_ref, cb, wr, wi, br, bi, ls):
    taps = [_shift_down(xbr, tail, CONV_W - 1 - k) for k in range(CONV_W)]
    xc = cb
    for k in range(CONV_W):
        xc = xc + cw_ref[pl.ds(k, 1), :] * taps[k]
    xcd = xc.astype(CD)
    r = _sigmoid(jnp.dot(xcd, wr, preferred_element_type=F32) + br)
    i = _sigmoid(jnp.dot(xcd, wi, preferred_element_type=F32) + bi)
    log_a = RG_C * r * ls
    a = jnp.exp(log_a)
    mult = jnp.sqrt(jnp.maximum(_neg_expm1(2.0 * log_a), 0.0))
    return taps, xc, r, i, log_a, a, mult


def _scan8_fwd(a, u):
    row = lax.broadcasted_iota(jnp.int32, a.shape, 0)
    for d in (1, 2, 4):
        a_s = pltpu.roll(a, d, axis=0)
        u_s = pltpu.roll(u, d, axis=0)
        m = row >= d
        u = jnp.where(m, a * u_s + u, u)
        a = jnp.where(m, a * a_s, a)
    return a, u


def _scan8_bwd(b, u):
    row = lax.broadcasted_iota(jnp.int32, b.shape, 0)
    for d in (1, 2, 4):
        b_s = pltpu.roll(b, SUBLANES - d, axis=0)
        u_s = pltpu.roll(u, SUBLANES - d, axis=0)
        m = row < SUBLANES - d
        u = jnp.where(m, b * u_s + u, u)
        b = jnp.where(m, b * b_s, b)
    return b, u


def _rglru_fwd(gate_br, x_br, cw, cb, wr, wi, br, bi, lam, name):
    s, c = x_br.shape
    nt = s // TIME_BLOCK
    tb, cbw = TIME_BLOCK, RG_BW
    groups = tb // SUBLANES

    def body(g_ref, x_ref, tail_ref, cw_ref, cb_ref, wr_ref, wi_ref, br_ref, bi_ref, lam_ref,
             y_ref, hs_ref, carry_ref, a_scr, u_scr):
        t = pl.program_id(1)

        @pl.when(t == 0)
        def _():
            carry_ref[...] = jnp.zeros_like(carry_ref)

        tail = jnp.where(t > 0, tail_ref[...], 0.0)
        ls = _log_sigmoid(lam_ref[...])
        _, xc, _, i, _, a, mult = _rg_gates(x_ref[...], tail, cw_ref, cb_ref[...], wr_ref[0], wi_ref[0],
                                            br_ref[...], bi_ref[...], ls)
        a_scr[...] = a
        u_scr[...] = mult * (i * xc)
        carry = carry_ref[...]
        for gi in range(groups):
            rows = pl.ds(gi * SUBLANES, SUBLANES)
            pa, hl = _scan8_fwd(a_scr[rows, :], u_scr[rows, :])
            hs_ref[rows, :] = hl + pa * carry
            carry = hs_ref[pl.ds(gi * SUBLANES + SUBLANES - 1, 1), :]
        carry_ref[...] = carry
        y_ref[...] = (hs_ref[...] * _gelu(g_ref[...])).astype(CD)

    blk = pl.BlockSpec((tb, cbw), lambda n, t: (t, n))
    tail = pl.BlockSpec((SUBLANES, cbw), lambda n, t: (jnp.maximum(t * groups - 1, 0), n))
    vec = pl.BlockSpec((1, cbw), lambda n, t: (0, n))
    wblk = pl.BlockSpec((1, cbw, cbw), lambda n, t: (n, 0, 0))
    return pl.pallas_call(
        body, name=name,
        out_shape=(jax.ShapeDtypeStruct((s, c), CD), jax.ShapeDtypeStruct((s, c), F32)),
        grid=(RG_BLOCKS, nt),
        in_specs=[blk, blk, tail, pl.BlockSpec((CONV_W, cbw), lambda n, t: (0, n)), vec, wblk, wblk, vec, vec, vec],
        out_specs=(blk, blk),
        scratch_shapes=[pltpu.VMEM((1, cbw), F32), pltpu.VMEM((tb, cbw), F32), pltpu.VMEM((tb, cbw), F32)],
        compiler_params=_params("parallel", "arbitrary"),
    )(gate_br, x_br, x_br, cw, cb, wr, wi, br, bi, lam)


def _rglru_bwd(dy, gate_br, x_br, hs, cw, cb, wr, wi, wrt, wit, br, bi, lam, name):
    s, c = x_br.shape
    nt = s // TIME_BLOCK
    tb, cbw = TIME_BLOCK, RG_BW
    groups = tb // SUBLANES

    def body(dy_ref, g_ref, x_ref, tail_ref, hs_ref, hprev_ref, cw_ref, cb_ref, wr_ref, wi_ref, wrt_ref, wit_ref,
             br_ref, bi_ref, lam_ref,
             dg_ref, dx_ref, dcw_ref, dcb_ref, dbr_ref, dbi_ref, dlam_ref, dwr_ref, dwi_ref,
             carry_ref, head_ref, b_scr, u_scr, dh_scr):
        tr = pl.program_id(1)
        first_block = tr == nt - 1

        @pl.when(tr == 0)
        def _():
            carry_ref[...] = jnp.zeros_like(carry_ref)
            head_ref[...] = jnp.zeros_like(head_ref)
            for ref in (dcw_ref, dcb_ref, dbr_ref, dbi_ref, dlam_ref, dwr_ref, dwi_ref):
                ref[...] = jnp.zeros_like(ref)

        tail = jnp.where(first_block, 0.0, tail_ref[...])
        lam_v = lam_ref[...]
        ls = _log_sigmoid(lam_v)
        taps, xc, r, i, log_a, a, mult = _rg_gates(x_ref[...], tail, cw_ref, cb_ref[...], wr_ref[0], wi_ref[0],
                                                   br_ref[...], bi_ref[...], ls)
        gate_v = g_ref[...]
        dyv = dy_ref[...]
        hsv = hs_ref[...]
        dg_ref[...] = (dyv * hsv * _gelu_grad(gate_v)).astype(CD)

        row = lax.broadcasted_iota(jnp.int32, a.shape, 0)
        b_scr[...] = jnp.where(row == tb - 1, 1.0, pltpu.roll(a, tb - 1, axis=0))
        u_scr[...] = dyv * _gelu(gate_v)
        carry = carry_ref[...]
        for gi in reversed(range(groups)):
            rows = pl.ds(gi * SUBLANES, SUBLANES)
            pb, gl = _scan8_bwd(b_scr[rows, :], u_scr[rows, :])
            dh_scr[rows, :] = gl + pb * carry
            carry = dh_scr[pl.ds(gi * SUBLANES, 1), :]
        dh = dh_scr[...]
        carry_ref[...] = carry * jnp.sum(jnp.where(row == 0, a, 0.0), axis=0, keepdims=True)

        hprev_tail = jnp.where(first_block, 0.0, hprev_ref[...])
        h_prev = _shift_down(hsv, hprev_tail, 1)
        da = dh * h_prev
        ixc = i * xc
        dmult = dh * ixc
        di = dh * mult * xc
        dxc = dh * mult * i
        a2 = a * a
        dlog_a = da * a - dmult * a2 / mult
        dpre_r = (dlog_a * (RG_C * ls)) * r * (1.0 - r)
        dpre_i = di * i * (1.0 - i)
        dlam_ref[...] += jnp.sum(dlog_a * r, axis=0, keepdims=True) * (RG_C * _sigmoid(-lam_v))
        dbr_ref[...] += jnp.sum(dpre_r, axis=0, keepdims=True)
        dbi_ref[...] += jnp.sum(dpre_i, axis=0, keepdims=True)
        xcd = xc.astype(CD)
        dprc = dpre_r.astype(CD)
        dpic = dpre_i.astype(CD)
        tn_dims = (((0,), (0,)), ((), ()))
        dwr_ref[0] += lax.dot_general(xcd, dprc, tn_dims, preferred_element_type=F32)
        dwi_ref[0] += lax.dot_general(xcd, dpic, tn_dims, preferred_element_type=F32)
        dxc = dxc + jnp.dot(dprc, wrt_ref[0], preferred_element_type=F32) + jnp.dot(dpic, wit_ref[0],
                                                                                    preferred_element_type=F32)
        dcb_ref[...] += jnp.sum(dxc, axis=0, keepdims=True)
        for k in range(CONV_W):
            dcw_ref[pl.ds(k, 1), :] += jnp.sum(dxc * taps[k], axis=0, keepdims=True)
        head = head_ref[...]
        dxb = jnp.zeros_like(dxc)
        for sft in range(CONV_W):
            dxb = dxb + cw_ref[pl.ds(CONV_W - 1 - sft, 1), :] * _shift_up(dxc, head, sft)
        dx_ref[...] = dxb.astype(CD)
        head_ref[...] = dxc[0:SUBLANES, :]

    blk = pl.BlockSpec((tb, cbw), lambda n, t: (nt - 1 - t, n))
    tail = pl.BlockSpec((SUBLANES, cbw), lambda n, t: (jnp.maximum((nt - 1 - t) * groups - 1, 0), n))
    vec = pl.BlockSpec((1, cbw), lambda n, t: (0, n))
    cwb = pl.BlockSpec((CONV_W, cbw), lambda n, t: (0, n))
    wblk = pl.BlockSpec((1, cbw, cbw), lambda n, t: (n, 0, 0))
    vshape = jax.ShapeDtypeStruct((1, c), F32)
    wshape = jax.ShapeDtypeStruct((RG_BLOCKS, cbw, cbw), F32)
    return pl.pallas_call(
        body, name=name,
        out_shape=(jax.ShapeDtypeStruct((s, c), CD), jax.ShapeDtypeStruct((s, c), CD),
                   jax.ShapeDtypeStruct((CONV_W, c), F32), vshape, vshape, vshape, vshape, wshape, wshape),
        grid=(RG_BLOCKS, nt),
        in_specs=[blk, blk, blk, tail, blk, tail, cwb, vec, wblk, wblk, wblk, wblk, vec, vec, vec],
        out_specs=(blk, blk, cwb, vec, vec, vec, vec, wblk, wblk),
        scratch_shapes=[pltpu.VMEM((1, cbw), F32), pltpu.VMEM((SUBLANES, cbw), F32),
                        pltpu.VMEM((tb, cbw), F32), pltpu.VMEM((tb, cbw), F32), pltpu.VMEM((tb, cbw), F32)],
        compiler_params=_params("parallel", "arbitrary"),
    )(dy, gate_br, x_br, x_br, hs, hs, cw, cb, wr, wi, wrt, wit, br, bi, lam)


ATT_BLOCK = 256
ATT_Q_BLOCK = 1024
ATT_RATIO = ATT_Q_BLOCK // ATT_BLOCK
ATT_SCALE = 1.0 / math.sqrt(SB_HEAD_DIM)
N_PAIRS = SB_HEADS * SB_HEAD_DIM // LANES
NT_DIMS = (((1,), (1,)), ((), ()))
TN_DIMS = (((0,), (0,)), ((), ()))


LOG2E = 1.4426950408889634


def _neg_abs(x):
    bits = lax.bitcast_convert_type(x, jnp.uint32) | jnp.uint32(0x80000000)
    return lax.bitcast_convert_type(bits, F32)


def _qk(qx, kb):
    return lax.dot_general(qx, kb, NT_DIMS, preferred_element_type=F32)


def _sb_logits(qk, valid):
    z2 = qk * (ATT_SCALE * LOG2E)
    lb2 = jnp.minimum(z2, 0.0) - jnp.log2(1.0 + jnp.exp2(_neg_abs(z2)))
    l2 = lb2 - z2
    if valid is not None:
        l2 = jnp.where(valid, l2, 0.0)
    return lb2, l2


def _hi_lo(x):
    hi = x.astype(CD)
    lo = (x - hi.astype(F32)).astype(CD)
    return jnp.concatenate([hi, lo], axis=1)


def _tri(strict, stacked):
    r = lax.broadcasted_iota(jnp.int32, (ATT_BLOCK, ATT_BLOCK), 0)
    c = lax.broadcasted_iota(jnp.int32, (ATT_BLOCK, ATT_BLOCK), 1)
    m = (r > c if strict else r >= c).astype(CD)
    return jnp.concatenate([m, m], axis=0) if stacked else m


def _attn_fwd(qkv, name):
    _, s, _ = qkv.shape
    tq, t = ATT_Q_BLOCK, ATT_BLOCK
    nblk = s // tq

    def body(q_ref, k_ref, v_ref, o_ref, qk_scr, w_scr):
        i = pl.program_id(1)
        lane = lax.broadcasted_iota(jnp.int32, (1, LANES), 1)
        head_masks = (lane < SB_HEAD_DIM, lane >= SB_HEAD_DIM)
        q = q_ref[0]
        qs = [jnp.where(m, q, jnp.zeros_like(q)) for m in head_masks]
        tri = _tri(True, False)
        rr = lax.broadcasted_iota(jnp.int32, (tq, t), 0)
        cc = lax.broadcasted_iota(jnp.int32, (tq, t), 1)

        def rows_of(j):
            return pl.ds(pl.multiple_of(j * t, t), t)

        def tail(x, row0):
            return x if row0 == 0 else x[row0:]

        def start_logits(j, row0=0):
            kb = k_ref[0, rows_of(j), :]
            for hd in range(2):
                qk_scr[hd, row0:, :] = _qk(tail(qs[hd], row0), kb)

        def weights(run, diagonal=False, row0=0):
            new_run = []
            valid = (cc < rr)[:tq - row0] if diagonal else None
            for hd in range(2):
                lb2, l2 = _sb_logits(qk_scr[hd, row0:, :], valid)
                w = jnp.exp2(lb2 + (tail(run[hd], row0) + jnp.dot(l2.astype(CD), tri, preferred_element_type=F32)))
                if valid is not None:
                    w = jnp.where(valid, w, 0.0)
                w_scr[row0:, hd * t:(hd + 1) * t] = w.astype(CD)
                rowsum = jnp.sum(l2, axis=1, keepdims=True)
                if row0:
                    rowsum = jnp.concatenate([jnp.zeros((row0, 1), F32), rowsum], axis=0)
                new_run.append(run[hd] + rowsum)
            return tuple(new_run)

        def apply_weights(j, row0=0):
            vb = v_ref[0, rows_of(j), :]
            vcat = jnp.concatenate([jnp.where(m, vb, jnp.zeros_like(vb)) for m in head_masks], axis=0)
            inc = jnp.dot(w_scr[row0:, :], vcat, preferred_element_type=F32)
            return inc if row0 == 0 else jnp.concatenate([jnp.zeros((row0, LANES), F32), inc], axis=0)

        zero = jnp.zeros((tq, 1), F32)
        last = ATT_RATIO - 1
        start_logits(ATT_RATIO * i + last, last * t)
        run = weights((zero, zero), True, last * t)
        oacc = jnp.zeros((tq, LANES), F32)
        for d in reversed(range(last)):
            start_logits(ATT_RATIO * i + d, d * t)
            oacc = oacc + apply_weights(ATT_RATIO * i + d + 1, (d + 1) * t)
            run = weights(run, True, d * t)
        start_logits(jnp.maximum(ATT_RATIO * i - 1, 0))

        def step(jj, carry):
            run, oacc = carry
            b = ATT_RATIO * i - 1 - jj
            oacc = oacc + apply_weights(b + 1)
            run = weights(run)
            start_logits(jnp.maximum(b - 1, 0))
            return run, oacc

        run, oacc = lax.fori_loop(0, ATT_RATIO * i, step, (run, oacc))
        o_ref[0] = oacc + apply_weights(0)

    return pl.pallas_call(
        body, name=name, out_shape=jax.ShapeDtypeStruct((N_PAIRS, s, LANES), F32), grid=(N_PAIRS, nblk),
        in_specs=[pl.BlockSpec((1, tq, LANES), lambda p, i: (p, i, 0)),
                  pl.BlockSpec((1, s, LANES), lambda p, i: (N_PAIRS + p, 0, 0)),
                  pl.BlockSpec((1, s, LANES), lambda p, i: (2 * N_PAIRS + p, 0, 0))],
        out_specs=pl.BlockSpec((1, tq, LANES), lambda p, i: (p, i, 0)),
        scratch_shapes=[pltpu.VMEM((2, tq, t), F32), pltpu.VMEM((tq, 2 * t), CD)],
        compiler_params=_params("parallel", "arbitrary"),
    )(qkv, qkv, qkv)


def _attn_bwd(qkv, o, do, name):
    _, s, _ = qkv.shape
    tq, t = ATT_Q_BLOCK, ATT_BLOCK
    nblk = s // tq

    def body(q_ref, k_ref, v_ref, o_ref, do_ref, dq_ref, dk_ref, dv_ref, qk_scr, dw_scr, w_scr, dz_scr):
        i = pl.program_id(1)

        @pl.when(i == 0)
        def _():
            dk_ref[...] = jnp.zeros_like(dk_ref)
            dv_ref[...] = jnp.zeros_like(dv_ref)

        lane = lax.broadcasted_iota(jnp.int32, (1, LANES), 1)
        head_masks = (lane < SB_HEAD_DIM, lane >= SB_HEAD_DIM)
        q = q_ref[0]
        dov = do_ref[0]
        ov = o_ref[0]
        qs = [jnp.where(m, q, jnp.zeros_like(q)) for m in head_masks]
        q_scaled_t = jnp.concatenate([(qx.astype(F32) * ATT_SCALE).T for qx in qs], axis=1).astype(CD)
        docs = [jnp.where(m, dov, jnp.zeros_like(dov)) for m in head_masks]
        docat_t = jnp.concatenate([d.astype(F32).T for d in docs], axis=1).astype(CD)
        totals = [jnp.sum(d.astype(F32) * ov, axis=1, keepdims=True) for d in docs]
        tri = _tri(True, False)
        tri_incl = _tri(False, True)
        rr = lax.broadcasted_iota(jnp.int32, (tq, t), 0)
        cc = lax.broadcasted_iota(jnp.int32, (tq, t), 1)

        def rows_of(j):
            return pl.ds(pl.multiple_of(j * t, t), t)

        def tail(x, row0):
            return x if row0 == 0 else x[row0:]

        def pad_rows(x, row0):
            return x if row0 == 0 else jnp.concatenate([jnp.zeros((row0, x.shape[1]), x.dtype), x], axis=0)

        def start_products(j, row0=0):
            kb = k_ref[0, rows_of(j), :]
            vb = v_ref[0, rows_of(j), :]
            for hd in range(2):
                qk_scr[hd, row0:, :] = _qk(tail(qs[hd], row0), kb)
                dw_scr[hd, row0:, :] = lax.dot_general(tail(docs[hd], row0), vb, NT_DIMS, preferred_element_type=F32)

        def logit_grads(run, erun, diagonal=False, row0=0):
            new_run, new_erun = [], []
            valid = (cc < rr)[:tq - row0] if diagonal else None
            for hd in range(2):
                lb2, l2 = _sb_logits(qk_scr[hd, row0:, :], valid)
                w = jnp.exp2(lb2 + (tail(run[hd], row0) + jnp.dot(l2.astype(CD), tri, preferred_element_type=F32)))
                if valid is not None:
                    w = jnp.where(valid, w, 0.0)
                wc = w.astype(CD)
                w_scr[hd * tq + row0:(hd + 1) * tq, :] = wc
                e = dw_scr[hd, row0:, :] * wc.astype(F32)
                prefix = (tail(totals[hd] - erun[hd], row0)
                          - jnp.dot(_hi_lo(e), tri_incl, preferred_element_type=F32))
                dz = e - jnp.exp2(lb2) * (e + prefix)
                if valid is not None:
                    dz = jnp.where(valid, dz, 0.0)
                dz_scr[hd * tq + row0:(hd + 1) * tq, :] = dz.astype(CD)
                new_run.append(run[hd] + pad_rows(jnp.sum(l2, axis=1, keepdims=True), row0))
                new_erun.append(erun[hd] + pad_rows(jnp.sum(e, axis=1, keepdims=True), row0))
            return tuple(new_run), tuple(new_erun)

        def apply_grads(j, row0=0):
            rows = rows_of(j)
            kb = k_ref[0, rows, :]
            kcat = jnp.concatenate([jnp.where(m, kb, jnp.zeros_like(kb)) for m in head_masks], axis=0)
            dz_heads = [dz_scr[hd * tq + row0:(hd + 1) * tq, :] for hd in range(2)]
            w_heads = [w_scr[hd * tq + row0:(hd + 1) * tq, :] for hd in range(2)]
            q_t = jnp.concatenate([q_scaled_t[:, hd * tq + row0:(hd + 1) * tq] for hd in range(2)], axis=1)
            do_t = jnp.concatenate([docat_t[:, hd * tq + row0:(hd + 1) * tq] for hd in range(2)], axis=1)
            dk_ref[0, :, rows] += jnp.dot(q_t, jnp.concatenate(dz_heads, axis=0), preferred_element_type=F32)
            dv_ref[0, :, rows] += jnp.dot(do_t, jnp.concatenate(w_heads, axis=0), preferred_element_type=F32)
            return pad_rows(jnp.dot(jnp.concatenate(dz_heads, axis=1), kcat, preferred_element_type=F32), row0)

        zero = jnp.zeros((tq, 1), F32)
        last = ATT_RATIO - 1
        start_products(ATT_RATIO * i + last, last * t)
        run, erun = logit_grads((zero, zero), (zero, zero), True, last * t)
        dqacc = jnp.zeros((tq, LANES), F32)
        for d in reversed(range(last)):
            start_products(ATT_RATIO * i + d, d * t)
            dqacc = dqacc + apply_grads(ATT_RATIO * i + d + 1, (d + 1) * t)
            run, erun = logit_grads(run, erun, True, d * t)
        start_products(jnp.maximum(ATT_RATIO * i - 1, 0))

        def step(jj, carry):
            run, erun, dqacc = carry
            b = ATT_RATIO * i - 1 - jj
            dqacc = dqacc + apply_grads(b + 1)
            run, erun = logit_grads(run, erun)
            start_products(jnp.maximum(b - 1, 0))
            return run, erun, dqacc

        run, erun, dqacc = lax.fori_loop(0, ATT_RATIO * i, step, (run, erun, dqacc))
        dq_ref[0] = ((dqacc + apply_grads(0)) * ATT_SCALE).astype(CD)

    qblk = pl.BlockSpec((1, tq, LANES), lambda p, i: (p, i, 0))
    full = pl.BlockSpec((1, LANES, s), lambda p, i: (p, 0, 0))
    shape = jax.ShapeDtypeStruct((N_PAIRS, s, LANES), F32)
    shape_t = jax.ShapeDtypeStruct((N_PAIRS, LANES, s), F32)
    dq, dk_t, dv_t = pl.pallas_call(
        body, name=name, out_shape=(jax.ShapeDtypeStruct(shape.shape, CD), shape_t, shape_t), grid=(N_PAIRS, nblk),
        in_specs=[qblk,
                  pl.BlockSpec((1, s, LANES), lambda p, i: (N_PAIRS + p, 0, 0)),
                  pl.BlockSpec((1, s, LANES), lambda p, i: (2 * N_PAIRS + p, 0, 0)),
                  qblk, qblk],
        out_specs=(qblk, full, full),
        scratch_shapes=[pltpu.VMEM((2, tq, t), F32), pltpu.VMEM((2, tq, t), F32),
                        pltpu.VMEM((2 * tq, t), CD), pltpu.VMEM((2 * tq, t), CD)],
        compiler_params=_params("parallel", "arbitrary"),
    )(qkv, qkv, qkv, o, do)
    return dq, jnp.swapaxes(dk_t, 1, 2).astype(CD), jnp.swapaxes(dv_t, 1, 2).astype(CD)


def _adamw(w, g, m, v, name):
    shape = w.shape
    rows, cols = (shape[-2], shape[-1]) if len(shape) >= 2 else (1, shape[-1])
    lead = w.size // (rows * cols)
    tr = _pick(rows, (512, 256, 128, 64, 32, 16, 8))

    def body(w_ref, g_ref, m_ref, v_ref, d_ref, nm_ref, nv_ref):
        gv = g_ref[...]
        nm = ADAM_B1 * m_ref[...] + (1.0 - ADAM_B1) * gv
        nv = ADAM_B2 * v_ref[...] + (1.0 - ADAM_B2) * (gv * gv)
        m_hat = nm / (1.0 - ADAM_B1 ** ADAM_STEP)
        v_hat = nv / (1.0 - ADAM_B2 ** ADAM_STEP)
        d_ref[...] = -ADAM_LR * (m_hat / (jnp.sqrt(v_hat) + ADAM_EPS) + ADAM_WD * w_ref[...])
        nm_ref[...] = nm
        nv_ref[...] = nv

    blk = pl.BlockSpec((1, tr, cols), lambda l, i: (l, i, 0))
    out = jax.ShapeDtypeStruct((lead, rows, cols), F32)
    d, nm, nv = pl.pallas_call(
        body, name=name, out_shape=(out, out, out), grid=(lead, rows // tr),
        in_specs=[blk, blk, blk, blk], out_specs=(blk, blk, blk), compiler_params=_params("parallel", "parallel"),
    )(*[a.reshape(lead, rows, cols) for a in (w, g, m, v)])
    return d.reshape(shape), nm.reshape(shape), nv.reshape(shape)


HBM = pl.BlockSpec(memory_space=pltpu.HBM)


def _coords():
    return lax.axis_index("x"), lax.axis_index("y"), lax.axis_index("c")


def _other_chips(x, y):
    return [(1 - x, y), (x, 1 - y), (1 - x, 1 - y)]


def _allgather_chips(shard, name, collective_id=None):
    r, cols = shard.shape
    half = r // 2
    quarter = half // 2

    def body(src_ref, out_ref, send_sems, recv_sems):
        x, y, c = _coords()
        sibling = (x, y, 1 - c)
        nx, ny, diag = (1 - x, y), (x, 1 - y), (1 - x, 1 - y)

        def piece(chip, core, lo, n):
            return out_ref.at[2 * chip[0] + chip[1], pl.ds(core * half + lo, n), :]

        def copy(k, dst, to, src=None):
            return pltpu.make_async_remote_copy(
                src_ref=dst if src is None else src, dst_ref=dst,
                send_sem=send_sems.at[k], recv_sem=recv_sems.at[k], device_id=to, device_id_type=MESH)

        me = (x, y)
        mine = src_ref.at[pl.ds(c * half, half), :]
        direct = [copy(0, piece(me, c, 0, half), (*nx, c), src=mine), copy(1, piece(me, c, 0, half), (*ny, c), src=mine)]
        for cp in direct:
            cp.start()
        arrivals = [piece(nx, c, 0, half), piece(ny, c, 0, half), piece(diag, c, 0, quarter),
                    piece(diag, c, quarter, quarter)]
        onward = [copy(2, piece(nx, c, 0, quarter), (*ny, c)), copy(3, piece(ny, c, quarter, quarter), (*nx, c))]
        to_sibling = [copy(4 + k, dst, sibling) for k, dst in enumerate(arrivals)]
        for k, dst in enumerate(arrivals):
            copy(k, dst, (x, y, c)).wait_recv()
            if k < 2:
                onward[k].start()
            to_sibling[k].start()
        from_sibling = [piece(nx, 1 - c, 0, half), piece(ny, 1 - c, 0, half), piece(diag, 1 - c, 0, quarter),
                        piece(diag, 1 - c, quarter, quarter)]
        for k, dst in enumerate(from_sibling):
            copy(4 + k, dst, (x, y, c)).wait_recv()
        for cp in direct + onward + to_sibling:
            cp.wait_send()

    out_shape = jax.ShapeDtypeStruct((N_CHIPS, r, cols), shard.dtype)
    sems = (pltpu.SemaphoreType.DMA((8,)), pltpu.SemaphoreType.DMA((8,)))
    if collective_id is None:
        return pl.pallas_call(body, name=name, out_shape=out_shape, in_specs=[HBM], out_specs=HBM,
                              scratch_shapes=list(sems))(shard)
    shard_ref = jax.new_ref(shard, memory_space=pltpu.MemorySpace.HBM)
    gathered_ref = jax.empty_ref(out_shape, memory_space=pltpu.MemorySpace.HBM)

    @_sequencer(name, collective_id, sems)
    def launch(send_sems, recv_sems):
        x, y, c = _coords()
        _handshake([(1 - x, y, c), (x, 1 - y, c), (x, y, 1 - c)])
        body(shard_ref, gathered_ref, send_sems, recv_sems)

    launch()
    return gathered_ref[...]


def _exchange_sibling_halves(g, name):
    n, r, cols = g.shape
    half = r // 2

    def body(g_ref, out_ref, send_sem, recv_sem):
        x, y, c = _coords()
        cp = pltpu.make_async_remote_copy(
            src_ref=g_ref.at[:, pl.ds((1 - c) * half, half), :], dst_ref=out_ref,
            send_sem=send_sem, recv_sem=recv_sem, device_id=(x, y, 1 - c), device_id_type=MESH)
        cp.start()
        cp.wait()

    return pl.pallas_call(
        body, name=name, out_shape=jax.ShapeDtypeStruct((n, half, cols), g.dtype),
        in_specs=[HBM], out_specs=HBM,
        scratch_shapes=[pltpu.SemaphoreType.DMA, pltpu.SemaphoreType.DMA],
    )(g)


def _sequencer(name, collective_id, scratch_types):
    return pl.kernel(mesh=plsc.ScalarSubcoreMesh(axis_name="sequencer", num_cores=1), name=name,
                     scratch_types=scratch_types, compiler_params=pltpu.CompilerParams(collective_id=collective_id))


def _handshake(peers):
    barrier = pltpu.get_barrier_semaphore()
    for peer in peers:
        pl.semaphore_signal(barrier, inc=1, device_id=peer, device_id_type=MESH)
    pl.semaphore_wait(barrier, len(peers))


def _exchange_sibling_halves_async(g, name, collective_id):
    n, r, cols = g.shape
    half = r // 2
    g_ref = jax.new_ref(g, memory_space=pltpu.MemorySpace.HBM)
    out_ref = jax.empty_ref(jax.ShapeDtypeStruct((n, half, cols), g.dtype), memory_space=pltpu.MemorySpace.HBM)

    @_sequencer(name, collective_id, (pltpu.SemaphoreType.DMA, pltpu.SemaphoreType.DMA))
    def launch(send_sem, recv_sem):
        x, y, c = _coords()
        _handshake([(x, y, 1 - c)])
        cp = pltpu.make_async_remote_copy(
            src_ref=g_ref.at[:, pl.ds((1 - c) * half, half), :], dst_ref=out_ref,
            send_sem=send_sem, recv_sem=recv_sem, device_id=(x, y, 1 - c), device_id_type=MESH)
        cp.start()
        cp.wait()

    launch()
    return out_ref[...]


def _share_halves_async(v, name, collective_id):
    h = v.shape[0] // 2
    v_ref = jax.new_ref(v, memory_space=pltpu.MemorySpace.HBM)

    @_sequencer(name, collective_id, (pltpu.SemaphoreType.DMA, pltpu.SemaphoreType.DMA))
    def launch(send_sem, recv_sem):
        x, y, c = _coords()
        _handshake([(x, y, 1 - c)])
        cp = pltpu.make_async_remote_copy(
            src_ref=v_ref.at[pl.ds(c * h, h), :], dst_ref=v_ref.at[pl.ds(c * h, h), :],
            send_sem=send_sem, recv_sem=recv_sem, device_id=(x, y, 1 - c), device_id_type=MESH)
        cp.start()
        pltpu.make_async_remote_copy(
            src_ref=v_ref.at[pl.ds(c * h, h), :], dst_ref=v_ref.at[pl.ds((1 - c) * h, h), :],
            send_sem=send_sem, recv_sem=recv_sem, device_id=(x, y, 1 - c), device_id_type=MESH).wait_recv()
        cp.wait_send()

    launch()
    return v_ref[...]


def _scatter_to_chips_async(p, name, collective_id):
    p_ref = jax.new_ref(p, memory_space=pltpu.MemorySpace.HBM)
    out_ref = jax.empty_ref(jax.ShapeDtypeStruct(p.shape, p.dtype), memory_space=pltpu.MemorySpace.HBM)

    @_sequencer(name, collective_id, (pltpu.SemaphoreType.DMA((3,)), pltpu.SemaphoreType.DMA((3,))))
    def launch(send_sems, recv_sems):
        x, y, c = _coords()
        me = 2 * x + y
        _handshake([(px, py, c) for px, py in _other_chips(x, y)])
        sends = []
        for j, (px, py) in enumerate(_other_chips(x, y)):
            sends.append(pltpu.make_async_remote_copy(
                src_ref=p_ref.at[2 * px + py], dst_ref=out_ref.at[me],
                send_sem=send_sems.at[j], recv_sem=recv_sems.at[j], device_id=(px, py, c), device_id_type=MESH))
        for cp in sends:
            cp.start()
        for j, (px, py) in enumerate(_other_chips(x, y)):
            pltpu.make_async_remote_copy(
                src_ref=p_ref.at[me], dst_ref=out_ref.at[2 * px + py],
                send_sem=send_sems.at[j], recv_sem=recv_sems.at[j], device_id=(px, py, c),
                device_id_type=MESH).wait_recv()
        for cp in sends:
            cp.wait_send()

    launch()
    return out_ref[...]


def _share_halves(v, name):
    h = v.shape[0] // 2

    def body(v_ref, out_ref, send_sem, recv_sem):
        x, y, c = _coords()
        cp = pltpu.make_async_remote_copy(
            src_ref=v_ref.at[pl.ds(c * h, h), :], dst_ref=out_ref.at[pl.ds(c * h, h), :],
            send_sem=send_sem, recv_sem=recv_sem, device_id=(x, y, 1 - c), device_id_type=MESH)
        cp.start()
        pltpu.make_async_remote_copy(
            src_ref=v_ref.at[pl.ds(c * h, h), :], dst_ref=out_ref.at[pl.ds((1 - c) * h, h), :],
            send_sem=send_sem, recv_sem=recv_sem, device_id=(x, y, 1 - c), device_id_type=MESH).wait_recv()
        cp.wait_send()

    return pl.pallas_call(
        body, name=name, out_shape=jax.ShapeDtypeStruct(v.shape, v.dtype),
        in_specs=[HBM], out_specs=HBM, input_output_aliases={0: 0},
        scratch_shapes=[pltpu.SemaphoreType.DMA, pltpu.SemaphoreType.DMA],
    )(v)


def _allreduce_small(v, name):
    r, cols = v.shape

    def body(v_ref, out_ref, buf_ref, send_sems, recv_sems):
        x, y, c = _coords()
        me = 4 * x + 2 * y + c
        buf_ref[me] = v_ref[...]
        sends = []
        for k in range(1, N_DEV):
            px = 1 - x if k & 4 else x
            py = 1 - y if k & 2 else y
            pc = 1 - c if k & 1 else c
            sends.append(pltpu.make_async_remote_copy(
                src_ref=v_ref, dst_ref=buf_ref.at[me], send_sem=send_sems.at[k - 1], recv_sem=recv_sems.at[k - 1],
                device_id=(px, py, pc), device_id_type=MESH))
        for cp in sends:
            cp.start()
        for cp in sends:
            cp.wait()
        acc = buf_ref[0]
        for d in range(1, N_DEV):
            acc = acc + buf_ref[d]
        out_ref[...] = acc

    return pl.pallas_call(
        body, name=name, out_shape=jax.ShapeDtypeStruct((r, cols), F32),
        in_specs=[pl.BlockSpec(memory_space=pltpu.VMEM)], out_specs=pl.BlockSpec(memory_space=pltpu.VMEM),
        scratch_shapes=[pltpu.VMEM((N_DEV, r, cols), F32), pltpu.SemaphoreType.DMA((N_DEV - 1,)),
                        pltpu.SemaphoreType.DMA((N_DEV - 1,))],
    )(v)


def _add_sibling(g, from_sibling, core, name):
    n, h, cols = from_sibling.shape
    tr = _row_tile(h)
    steps = h // tr

    def body(core_ref, a_ref, b_ref, o_ref):
        o_ref[...] = (a_ref[...] + b_ref[...]).astype(o_ref.dtype)

    return pl.pallas_call(
        body, name=name, out_shape=jax.ShapeDtypeStruct(from_sibling.shape, jnp.bfloat16),
        grid_spec=pltpu.PrefetchScalarGridSpec(
            num_scalar_prefetch=1, grid=(n, steps),
            in_specs=[pl.BlockSpec((1, tr, cols), lambda s, i, core_ref: (s, core_ref[0] * steps + i, 0)),
                      pl.BlockSpec((1, tr, cols), lambda s, i, core_ref: (s, i, 0))],
            out_specs=pl.BlockSpec((1, tr, cols), lambda s, i, core_ref: (s, i, 0))),
        compiler_params=_params("parallel", "parallel"),
    )(core.reshape(1).astype(jnp.int32), g, from_sibling)


def _sum_slots(p, own, chip, core, name):
    n, r, cols = p.shape
    tr = _row_tile(r)
    steps = r // tr

    def body(core_ref, chip_ref, p_ref, own_ref, o_ref):
        parts = [jnp.where(chip_ref[0] == s, own_ref[0], p_ref[s]).astype(F32) for s in range(n)]
        o_ref[...] = ((parts[0] + parts[1]) + parts[2]) + parts[3]

    return pl.pallas_call(
        body, name=name, out_shape=jax.ShapeDtypeStruct((2 * r, cols), F32),
        grid_spec=pltpu.PrefetchScalarGridSpec(
            num_scalar_prefetch=2, grid=(steps,),
            in_specs=[pl.BlockSpec((n, tr, cols), lambda i, core_ref, chip_ref: (0, i, 0)),
                      pl.BlockSpec((1, tr, cols), lambda i, core_ref, chip_ref: (chip_ref[0], i, 0))],
            out_specs=pl.BlockSpec((tr, cols), lambda i, core_ref, chip_ref: (core_ref[0] * steps + i, 0))),
        compiler_params=_params("parallel"),
    )(core.reshape(1).astype(jnp.int32), chip.reshape(1).astype(jnp.int32), p, own)


PACK_COLS = 1024


def _pack_shards(parts):
    return jnp.concatenate([p.reshape(-1, PACK_COLS) for p in parts], axis=0)


def _unpack_shards(buf, shapes):
    out, row = [], 0
    for shp in shapes:
        nrows = math.prod(shp) // PACK_COLS
        out.append(buf[..., row:row + nrows, :].reshape(buf.shape[:-2] + tuple(shp)))
        row += nrows
    return out


def _local_step(x, target, w):
    t = lambda a: a.T
    g = {}
    w_in_g, w_in_x = w["a_w_in"][:, :D_RNN], w["a_w_in"][:, D_RNN:]
    h0, gate_br, x_br = _norm_and_project(x, w["norm_mix_g"][0], w_in_g, w_in_x, "rglru_in")
    y_a, hs = _rglru_fwd(gate_br, x_br, w["a_conv_w"], w["a_conv_b"], w["a_w_r"], w["a_w_i"], w["a_b_r"],
                         w["a_b_i"], w["a_lambda"], "rglru_fwd")
    x1, h1 = _matmul([(y_a, w["a_w_out"])], F32, "mm_a_out", addend=x, norm_gain=w["norm_ffn_g"][0])
    fg0, fu0, act0 = _ffn_up(h1, w["ffn_w_gate"][0], w["ffn_w_up"][0], "ffn0_up")
    x2, h2 = _matmul([(act0, w["ffn_w_down"][0])], F32, "mm_f0_down", addend=x1, norm_gain=w["norm_mix_g"][1],
                     tk=D_FF)
    qkv = _matmul([(h2, w["b_w_qkv"])], CD, "mm_b_qkv", out_lbm=True, tn=3 * D_MODEL)
    o = _attn_fwd(qkv, "attn_fwd")
    x3, h3 = _matmul([(o, w["b_w_out"])], F32, "mm_b_out", a_lbm=True, addend=x2, norm_gain=w["norm_ffn_g"][1])
    fg1, fu1, act1 = _ffn_up(h3, w["ffn_w_gate"][1], w["ffn_w_up"][1], "ffn1_up")
    dx4, dx4c, g["final_g"], loss = _matmul([(act1, w["ffn_w_down"][1])], F32, "mm_f1_down", addend=x3,
                                            loss_head=(w["final_g"], target), tk=D_FF)

    def ffn_bwd(dx_out, dxc, h, x_in, fg, fu, act, layer, tag):
        dg, du = _ffn_dact(dxc, t(w["ffn_w_down"][layer]), fg, fu, "ffn_" + tag + "_dact")
        dwd = _matmul([(act, dxc)], F32, "mm_" + tag + "_dwd", trans_a=True)
        dwg = _matmul([(h, dg)], F32, "mm_" + tag + "_dwg", trans_a=True, tm=D_MODEL)
        dwu = _matmul([(h, du)], F32, "mm_" + tag + "_dwu", trans_a=True, tm=D_MODEL)
        dx_in, dx_in_c, dgain = _matmul([(dg, t(w["ffn_w_gate"][layer])), (du, t(w["ffn_w_up"][layer]))], F32,
                                        "mm_" + tag + "_dh", norm_bwd=(x_in, w["norm_ffn_g"][layer], dx_out),
                                        tk=D_FF, tm=256)
        return dx_in, dx_in_c, dgain, dwg, dwu, dwd

    dx3, dx3c, dgf1, dwg1, dwu1, dwd1 = ffn_bwd(dx4, dx4c, h3, x3, fg1, fu1, act1, 1, "f1")
    do = _matmul([(dx3c, t(w["b_w_out"]))], CD, "mm_b_do", out_lbm=True, tn=1024)
    g["b_w_out"] = _matmul([(o, dx3c)], F32, "mm_b_dwout", trans_a=True, a_lbm=True)
    dq, dk, dv = _attn_bwd(qkv, o, do, "attn_bwd")
    wq_t = t(w["b_w_qkv"])
    parts = (dq, dk, dv)
    g["b_w_qkv"] = jnp.concatenate(
        [_matmul([(h2, p)], F32, "mm_b_dwqkv%d" % n, trans_a=True, b_lbm=True) for n, p in enumerate(parts)], axis=1)
    dx2, dx2c, dgm1 = _matmul([(p, wq_t[n * D_MODEL:(n + 1) * D_MODEL]) for n, p in enumerate(parts)], F32, "mm_b_dh",
                              a_lbm=True, norm_bwd=(x2, w["norm_mix_g"][1], dx3))
    dx1, dx1c, dgf0, dwg0, dwu0, dwd0 = ffn_bwd(dx2, dx2c, h1, x1, fg0, fu0, act0, 0, "f0")
    dy_a = _matmul([(dx1c, t(w["a_w_out"]))], F32, "mm_a_dy")
    g["a_w_out"] = _matmul([(y_a, dx1c)], F32, "mm_a_dwout", trans_a=True)
    wrt = jnp.swapaxes(w["a_w_r"], 1, 2)
    wit = jnp.swapaxes(w["a_w_i"], 1, 2)
    (dgate, dxbr, g["a_conv_w"], g["a_conv_b"], g["a_b_r"], g["a_b_i"], g["a_lambda"], g["a_w_r"],
     g["a_w_i"]) = _rglru_bwd(dy_a, gate_br, x_br, hs, w["a_conv_w"], w["a_conv_b"], w["a_w_r"], w["a_w_i"], wrt, wit,
                              w["a_b_r"], w["a_b_i"], w["a_lambda"], "rglru_bwd")
    g["a_w_in"] = jnp.concatenate([_matmul([(h0, dgate)], F32, "mm_a_dwin_g", trans_a=True),
                                   _matmul([(h0, dxbr)], F32, "mm_a_dwin_x", trans_a=True)], axis=1)
    dx0, _, dgm0 = _matmul([(dgate, t(w_in_g)), (dxbr, t(w_in_x))], F32, "mm_a_dh",
                           norm_bwd=(x, w["norm_mix_g"][0], dx1))
    g["norm_mix_g"] = jnp.concatenate([dgm0, dgm1], axis=0)
    g["norm_ffn_g"] = jnp.concatenate([dgf0, dgf1], axis=0)
    g["ffn_w_gate"] = [dwg0, dwg1]
    g["ffn_w_up"] = [dwu0, dwu1]
    g["ffn_w_down"] = [dwd0, dwd1]
    return loss, dx0, g


WEIGHTS = ["norm_mix_g", "norm_ffn_g", "a_w_in", "a_conv_w", "a_conv_b", "a_w_r", "a_b_r", "a_w_i", "a_b_i",
           "a_lambda", "a_w_out", "b_w_qkv", "b_w_out", "ffn_w_gate", "ffn_w_up", "ffn_w_down", "final_g"]
BIG = [("a_w_in", 2), ("a_w_r", 2), ("a_w_i", 2), ("a_w_out", 1), ("b_w_qkv", 2), ("b_w_out", 1),
       ("ffn_w_gate", 2), ("ffn_w_up", 2), ("ffn_w_down", 1)]
LAYER1 = ["b_w_qkv", "b_w_out", "ffn_w_gate", "ffn_w_up", "ffn_w_down"]
LAYER0 = ["a_w_in", "a_w_r", "a_w_i", "a_w_out", "ffn_w_gate", "ffn_w_up", "ffn_w_down"]
RS_COLLECTIVE_IDS = {"chips1": 3, "chips0": 4, "sibling1": 5, "share1": 6}
GATHER_COLLECTIVE_IDS = (8, 7)
SMALL = ["norm_mix_g", "norm_ffn_g", "a_conv_w", "a_conv_b", "a_b_r", "a_b_i", "a_lambda", "final_g"]


def _split_chips(full, axis):
    if axis == 1:
        return full.reshape((N_CHIPS, 1, full.shape[1] // N_CHIPS) + full.shape[2:])
    return jnp.stack(jnp.split(full, N_CHIPS, axis=axis))


def _step(x, target, weights, moments_m, moments_v):
    chip = 2 * lax.axis_index("x") + lax.axis_index("y")
    core = lax.axis_index("c")
    axis_of = dict(BIG)
    full = {}
    for group, layer, tag, collective_id in ((LAYER0[:4], 0, "0a", None), (LAYER0[4:], 0, "0f", GATHER_COLLECTIVE_IDS[0]),
                                             (LAYER1, 1, "1", GATHER_COLLECTIVE_IDS[1])):
        shards = [weights[n][layer % weights[n].shape[0]].astype(CD) for n in group]
        packed = _pack_shards(shards)
        if collective_id is not None:
            packed, first_gathered = lax.optimization_barrier((packed, first_gathered))
        gathered = _allgather_chips(packed, "allgather_weights" + tag, collective_id)
        if collective_id is None:
            first_gathered = gathered
        for n, own, stack in zip(group, shards, _unpack_shards(gathered, [sh.shape for sh in shards])):
            joined = jnp.concatenate([jnp.where(chip == s, own, stack[s]) for s in range(N_CHIPS)],
                                     axis=axis_of[n] - 1)
            full.setdefault(n, {})[layer] = joined
    full = {n: (v[0] if n.startswith("a_") else v[1] if n.startswith("b_") else [v[0], v[1]]) for n, v in full.items()}
    cw_rows = jnp.zeros((N_CHIPS, CONV_W, RG_BW), F32)
    cw_rows = lax.dynamic_update_slice(cw_rows, jnp.where(core == 0, weights["a_conv_w"], 0.0), (chip, 0, 0))
    cw_all = _allreduce_small(cw_rows.reshape(-1, LANES), "allgather_conv_w").reshape(N_CHIPS, CONV_W, RG_BW)
    full["a_conv_w"] = jnp.concatenate([cw_all[s] for s in range(N_CHIPS)], axis=1)
    for n in ("norm_mix_g", "norm_ffn_g", "final_g"):
        full[n] = weights[n]
    for n in ("a_conv_b", "a_b_r", "a_b_i", "a_lambda"):
        full[n] = weights[n]
    loss, dx, grads = _local_step(x[0], target[0], full)
    small_parts = [grads[n].reshape(-1) for n in SMALL] + [loss.reshape(-1)]
    sizes = [p.shape[0] for p in small_parts]
    small = _allreduce_small(jnp.concatenate(small_parts).reshape(-1, LANES), "allreduce_small").reshape(-1)
    red, pos = {}, 0
    for n, sz in zip(SMALL + ["loss"], sizes):
        red[n] = small[pos:pos + sz]
        pos += sz
    loss_out = red["loss"][0]
    g_out = {}
    for n in SMALL:
        if n == "a_conv_w":
            g_out[n] = lax.dynamic_slice(red[n].reshape(CONV_W, D_RNN), (0, chip * RG_BW), (CONV_W, RG_BW)).reshape(
                weights[n].shape)
        else:
            g_out[n] = red[n].reshape(weights[n].shape)
    axis_of = dict(BIG)
    pieces = {}
    for group, layer, tag in ((LAYER1, 1, "1"), (LAYER0, 0, "0")):
        stacks, shapes = [], []
        for n in group:
            per_layer = isinstance(grads[n], list)
            gfull = grads[n][layer] if per_layer else grads[n]
            shard_shape = weights[n].shape[1:]
            gfull = gfull.reshape((1,) + gfull.shape)
            stacks.append(_split_chips(gfull, axis_of[n]).reshape(N_CHIPS, -1, PACK_COLS))
            shapes.append((1,) + tuple(shard_shape))
        gbuf = jnp.concatenate(stacks, axis=1)
        if layer == 1:
            from_sibling = _exchange_sibling_halves_async(gbuf, "rs_sibling" + tag, RS_COLLECTIVE_IDS["sibling1"])
        else:
            from_sibling = _exchange_sibling_halves(gbuf, "rs_sibling" + tag)
        chip_partial = _add_sibling(gbuf, from_sibling, core, "rs_add" + tag)
        from_chips = _scatter_to_chips_async(chip_partial, "rs_chips" + tag, RS_COLLECTIVE_IDS["chips" + tag])
        halves = _sum_slots(from_chips, chip_partial, chip, core, "rs_sum" + tag)
        if layer == 1:
            reduced = _share_halves_async(halves, "rs_share" + tag, RS_COLLECTIVE_IDS["share1"])
        else:
            reduced = _share_halves(halves, "rs_share" + tag)
        for n, piece in zip(group, _unpack_shards(reduced, shapes)):
            pieces.setdefault(n, {})[layer] = piece
    for n, _ in BIG:
        layers = pieces[n]
        g_out[n] = jnp.concatenate([layers[k] for k in sorted(layers)], axis=0)
    updates = {}
    for n, _ in BIG:
        updates[n] = _adamw(weights[n], g_out[n], moments_m[n], moments_v[n], "adamw_" + n)
    rows = lambda d: jnp.concatenate([d[n].reshape(-1, D_MODEL) for n in SMALL], axis=0)
    small_updates = _adamw(rows(weights), rows(g_out), rows(moments_m), rows(moments_v), "adamw_small")
    pos = 0
    for n in SMALL:
        nrows = weights[n].size // D_MODEL
        updates[n] = tuple(u[pos:pos + nrows].reshape(weights[n].shape) for u in small_updates)
        pos += nrows
    outs_g = [g_out[n] for n in WEIGHTS]
    outs_d, outs_m, outs_v = ([updates[n][k] for n in WEIGHTS] for k in range(3))
    return (loss_out, dx[None], *outs_g, *outs_d, *outs_m, *outs_v)


def kernel(x, norm_mix_g, norm_ffn_g, a_w_in, a_conv_w, a_conv_b, a_w_r, a_b_r, a_w_i, a_b_i, a_lambda, a_w_out, b_w_qkv, b_w_out, ffn_w_gate, ffn_w_up, ffn_w_down, final_g, loss_target, m_norm_mix_g, m_norm_ffn_g, m_a_w_in, m_a_conv_w, m_a_conv_b, m_a_w_r, m_a_b_r, m_a_w_i, m_a_b_i, m_a_lambda, m_a_w_out, m_b_w_qkv, m_b_w_out, m_ffn_w_gate, m_ffn_w_up, m_ffn_w_down, m_final_g, v_norm_mix_g, v_norm_ffn_g, v_a_w_in, v_a_conv_w, v_a_conv_b, v_a_w_r, v_a_b_r, v_a_w_i, v_a_b_i, v_a_lambda, v_a_w_out, v_b_w_qkv, v_b_w_out, v_ffn_w_gate, v_ffn_w_up, v_ffn_w_down, v_final_g):
    ws = [norm_mix_g, norm_ffn_g, a_w_in, a_conv_w, a_conv_b, a_w_r, a_b_r, a_w_i, a_b_i, a_lambda, a_w_out, b_w_qkv,
          b_w_out, ffn_w_gate, ffn_w_up, ffn_w_down, final_g]
    ms = [m_norm_mix_g, m_norm_ffn_g, m_a_w_in, m_a_conv_w, m_a_conv_b, m_a_w_r, m_a_b_r, m_a_w_i, m_a_b_i, m_a_lambda,
          m_a_w_out, m_b_w_qkv, m_b_w_out, m_ffn_w_gate, m_ffn_w_up, m_ffn_w_down, m_final_g]
    vs = [v_norm_mix_g, v_norm_ffn_g, v_a_w_in, v_a_conv_w, v_a_conv_b, v_a_w_r, v_a_b_r, v_a_w_i, v_a_b_i, v_a_lambda,
          v_a_w_out, v_b_w_qkv, v_b_w_out, v_ffn_w_gate, v_ffn_w_up, v_ffn_w_down, v_final_g]
    return _step(x, loss_target, dict(zip(WEIGHTS, ws)), dict(zip(WEIGHTS, ms)), dict(zip(WEIGHTS, vs)))
```

```python
import math

import jax
import jax.numpy as jnp
from jax import lax
from jax.experimental import pallas as pl
from jax.experimental.pallas import tpu as pltpu
from jax.experimental.pallas import tpu_sc as plsc

F32 = jnp.float32
CD = jnp.bfloat16

D_MODEL = 1024
D_RNN = 1024
RG_BLOCKS = 4
RG_BW = 256
CONV_W = 4
RG_C = 8.0
SB_HEADS = 16
SB_HEAD_DIM = 64
D_FF = 2816
RMS_EPS = 1e-6
N_CHIPS = 4
N_DEV = 8

ADAM_LR = 0.001
ADAM_B1 = 0.9
ADAM_B2 = 0.999
ADAM_EPS = 1e-08
ADAM_WD = 0.01
ADAM_STEP = 10

LANES = 128
VMEM_LIMIT = 56 * 1024 * 1024
MESH = pl.DeviceIdType.MESH


def _params(*sem):
    return pltpu.CompilerParams(dimension_semantics=sem, vmem_limit_bytes=VMEM_LIMIT)


def _pick(n, prefs):
    for p in prefs:
        if n % p == 0:
            return p
    return n


def _row_tile(rows):
    return max(d for d in range(16, 1025, 16) if rows % d == 0)


def _matmul(pairs, out_dtype, name, *, trans_a=False, a_lbm=False, b_lbm=False, out_lbm=False, addend=None,
            tm=512, tn=None, tk=None, norm_gain=None, norm_bwd=None, loss_head=None):
    a0, b0 = pairs[0]
    if trans_a:
        kdim = a0.shape[1] if a_lbm else a0.shape[0]
        m = a0.shape[0] * LANES if a_lbm else a0.shape[1]
    else:
        m = a0.shape[1] if a_lbm else a0.shape[0]
        kdim = a0.shape[0] * LANES if a_lbm else a0.shape[1]
    n = b0.shape[0] * LANES if b_lbm else b0.shape[1]
    if trans_a and m <= 1024:
        tm = m
    tm = _pick(m, (tm, 1408, 256, 128))
    tn = tn or _pick(n, (1408, 1024, 768, 512, 256, 128))
    tk = tk or _pick(kdim, (1024, 1408, 512, 256, 128))
    nk = kdim // tk
    npair = len(pairs)

    def cat(ref):
        return jnp.concatenate([ref[p] for p in range(ref.shape[0])], axis=-1)

    def body(*refs):
        ins = refs[: 2 * npair]
        pos = 2 * npair
        add_ref = None
        if addend is not None:
            add_ref = refs[pos]
            pos += 1
        gain_ref = x_ref = dxin_ref = None
        if norm_gain is not None:
            gain_ref = refs[pos]
            pos += 1
        if norm_bwd is not None:
            x_ref, gain_ref, dxin_ref = refs[pos:pos + 3]
            pos += 3
        if loss_head is not None:
            gain_ref, target_ref = refs[pos:pos + 2]
            pos += 2
        o_ref = refs[pos]
        extra_out = refs[pos + 1:-1]
        acc_ref = refs[-1]
        k = pl.program_id(2)

        @pl.when(k == 0)
        def _():
            acc_ref[...] = jnp.zeros_like(acc_ref)

        if norm_bwd is not None or loss_head is not None:
            @pl.when((k == 0) & (pl.program_id(0) == 0))
            def _():
                for ref in extra_out[1:]:
                    ref[...] = jnp.zeros_like(ref)

        acc = acc_ref[...]
        for p in range(npair):
            a = (cat(ins[2 * p]) if a_lbm else ins[2 * p][...]).astype(CD)
            b = (cat(ins[2 * p + 1]) if b_lbm else ins[2 * p + 1][...]).astype(CD)
            dims = (((0,), (0,)), ((), ())) if trans_a else (((1,), (0,)), ((), ()))
            acc = acc + lax.dot_general(a, b, dims, preferred_element_type=F32)
        acc_ref[...] = acc

        @pl.when(k == nk - 1)
        def _():
            res = acc_ref[...]
            if add_ref is not None:
                res = res + add_ref[...]
            if norm_gain is not None:
                rinv = lax.rsqrt(jnp.mean(res * res, axis=-1, keepdims=True) + RMS_EPS)
                extra_out[0][...] = (res * rinv * gain_ref[...]).astype(CD)
            if norm_bwd is not None:
                xv = x_ref[...]
                rinv = lax.rsqrt(jnp.mean(xv * xv, axis=-1, keepdims=True) + RMS_EPS)
                nrm = xv * rinv
                dn = res * gain_ref[...]
                extra_out[1][...] += jnp.sum(res * nrm, axis=0, keepdims=True)
                res = dxin_ref[...] + rinv * (dn - nrm * jnp.mean(dn * nrm, axis=-1, keepdims=True))
                extra_out[0][...] = res.astype(CD)
            if loss_head is not None:
                gv = gain_ref[...]
                rinv = lax.rsqrt(jnp.mean(res * res, axis=-1, keepdims=True) + RMS_EPS)
                nrm = res * rinv
                err = nrm * gv - target_ref[...]
                extra_out[2][...] += 0.5 * jnp.sum(jnp.mean(err * err, axis=-1, keepdims=True), axis=0, keepdims=True)
                dy = err * (1.0 / n)
                dn = dy * gv
                extra_out[1][...] += jnp.sum(dy * nrm, axis=0, keepdims=True)
                res = rinv * (dn - nrm * jnp.mean(dn * nrm, axis=-1, keepdims=True))
                extra_out[0][...] = res.astype(CD)
            res = res.astype(out_dtype)
            if out_lbm:
                for p in range(tn // LANES):
                    o_ref[p] = res[:, p * LANES:(p + 1) * LANES]
            else:
                o_ref[...] = res

    if trans_a:
        a_spec = (pl.BlockSpec((tm // LANES, tk, LANES), lambda i, j, k: (i, k, 0)) if a_lbm
                  else pl.BlockSpec((tk, tm), lambda i, j, k: (k, i)))
    else:
        a_spec = (pl.BlockSpec((tk // LANES, tm, LANES), lambda i, j, k: (k, i, 0)) if a_lbm
                  else pl.BlockSpec((tm, tk), lambda i, j, k: (i, k)))
    b_spec = (pl.BlockSpec((tn // LANES, tk, LANES), lambda i, j, k: (j, k, 0)) if b_lbm
              else pl.BlockSpec((tk, tn), lambda i, j, k: (k, j)))
    in_specs = []
    args = []
    for a, b in pairs:
        in_specs += [a_spec, b_spec]
        args += [a, b]
    if addend is not None:
        in_specs.append(pl.BlockSpec((tm, tn), lambda i, j, k: (i, j)))
        args.append(addend)
    tile = pl.BlockSpec((tm, tn), lambda i, j, k: (i, j))
    vec = pl.BlockSpec((1, tn), lambda i, j, k: (0, j))
    if out_lbm:
        out_shape = jax.ShapeDtypeStruct((n // LANES, m, LANES), out_dtype)
        out_spec = pl.BlockSpec((tn // LANES, tm, LANES), lambda i, j, k: (j, i, 0))
    else:
        out_shape = jax.ShapeDtypeStruct((m, n), out_dtype)
        out_spec = tile
    sem = ("parallel", "parallel", "arbitrary")
    if norm_gain is not None or norm_bwd is not None or loss_head is not None:
        assert tn == n and not out_lbm, "the norm needs whole rows in one tile"
        out_shape, out_spec = [out_shape, jax.ShapeDtypeStruct((m, n), CD)], [out_spec, tile]
    if norm_gain is not None:
        in_specs.append(vec)
        args.append(norm_gain.reshape(1, n))
    if norm_bwd is not None:
        x_in, gain, dx_in = norm_bwd
        in_specs += [tile, vec, tile]
        args += [x_in, gain.reshape(1, n), dx_in]
        out_shape.append(jax.ShapeDtypeStruct((1, n), F32))
        out_spec.append(vec)
        sem = ("arbitrary", "arbitrary", "arbitrary")
    if loss_head is not None:
        gain, target = loss_head
        in_specs += [vec, tile]
        args += [gain.reshape(1, n), target]
        out_shape += [jax.ShapeDtypeStruct((1, n), F32), jax.ShapeDtypeStruct((1, LANES), F32)]
        out_spec += [vec, pl.BlockSpec((1, LANES), lambda i, j, k: (0, 0))]
        sem = ("arbitrary", "arbitrary", "arbitrary")
    return pl.pallas_call(
        body, name=name, out_shape=out_shape, grid=(m // tm, n // tn, nk),
        in_specs=in_specs, out_specs=out_spec,
        scratch_shapes=[pltpu.VMEM((tm, tn), F32)],
        compiler_params=_params(*sem),
    )(*args)


def _norm_and_project(x, g, w_a, w_b, name):
    s, d = x.shape
    n = w_a.shape[1]
    tm = _pick(s, (512, 256))

    def body(x_ref, g_ref, wa_ref, wb_ref, h_ref, a_ref, b_ref):
        xv = x_ref[...]
        rinv = lax.rsqrt(jnp.mean(xv * xv, axis=-1, keepdims=True) + RMS_EPS)
        h = (xv * rinv * g_ref[...]).astype(CD)
        h_ref[...] = h
        a_ref[...] = jnp.dot(h, wa_ref[...], preferred_element_type=F32)
        b_ref[...] = jnp.dot(h, wb_ref[...], preferred_element_type=F32)

    row = pl.BlockSpec((tm, d), lambda i: (i, 0))
    out = pl.BlockSpec((tm, n), lambda i: (i, 0))
    wspec = pl.BlockSpec((d, n), lambda i: (0, 0))
    return pl.pallas_call(
        body, name=name,
        out_shape=(jax.ShapeDtypeStruct((s, d), CD), jax.ShapeDtypeStruct((s, n), F32),
                   jax.ShapeDtypeStruct((s, n), F32)),
        grid=(s // tm,), in_specs=[row, pl.BlockSpec((1, d), lambda i: (0, 0)), wspec, wspec],
        out_specs=(row, out, out), compiler_params=_params("parallel"),
    )(x, g.reshape(1, d), w_a, w_b)


def _sigmoid(z):
    return 1.0 / (1.0 + jnp.exp(-z))


FFN_TM = 512
FFN_TN = 1408


def _ffn_up(h, wg, wu, name):
    s, d = h.shape
    f = wg.shape[1]
    tm = _pick(s, (FFN_TM, 256))

    def body(h_ref, wg_ref, wu_ref, g_ref, u_ref, a_ref):
        hv = h_ref[...]
        gv = jnp.dot(hv, wg_ref[...], preferred_element_type=F32)
        uv = jnp.dot(hv, wu_ref[...], preferred_element_type=F32)
        g_ref[...] = gv
        u_ref[...] = uv
        a_ref[...] = (gv * _sigmoid(gv) * uv).astype(CD)

    a_spec = pl.BlockSpec((tm, d), lambda j, i: (i, 0))
    w_spec = pl.BlockSpec((d, FFN_TN), lambda j, i: (0, j))
    o_spec = pl.BlockSpec((tm, FFN_TN), lambda j, i: (i, j))
    return pl.pallas_call(
        body, name=name,
        out_shape=(jax.ShapeDtypeStruct((s, f), F32), jax.ShapeDtypeStruct((s, f), F32),
                   jax.ShapeDtypeStruct((s, f), CD)),
        grid=(f // FFN_TN, s // tm), in_specs=[a_spec, w_spec, w_spec], out_specs=(o_spec, o_spec, o_spec),
        compiler_params=_params("parallel", "parallel"),
    )(h, wg, wu)


def _ffn_dact(dxc, wd_t, g, u, name):
    s, d = dxc.shape
    f = wd_t.shape[1]
    tm = _pick(s, (FFN_TM, 256))

    def body(dx_ref, w_ref, g_ref, u_ref, dg_ref, du_ref):
        da = jnp.dot(dx_ref[...], w_ref[...], preferred_element_type=F32)
        gv = g_ref[...]
        sg = _sigmoid(gv)
        silu = gv * sg
        dg_ref[...] = (da * u_ref[...] * (sg + silu * (1.0 - sg))).astype(CD)
        du_ref[...] = (da * silu).astype(CD)

    a_spec = pl.BlockSpec((tm, d), lambda j, i: (i, 0))
    w_spec = pl.BlockSpec((d, FFN_TN), lambda j, i: (0, j))
    o_spec = pl.BlockSpec((tm, FFN_TN), lambda j, i: (i, j))
    return pl.pallas_call(
        body, name=name,
        out_shape=(jax.ShapeDtypeStruct((s, f), CD), jax.ShapeDtypeStruct((s, f), CD)),
        grid=(f // FFN_TN, s // tm), in_specs=[a_spec, w_spec, o_spec, o_spec], out_specs=(o_spec, o_spec),
        compiler_params=_params("parallel", "parallel"),
    )(dxc, wd_t, g, u)


TIME_BLOCK = 1024
SUBLANES = 8
GELU_C = math.sqrt(2.0 / math.pi)
GELU_A = 0.044715


def _gelu(x):
    return 0.5 * x * (1.0 + jnp.tanh(GELU_C * (x + GELU_A * x * x * x)))


def _gelu_grad(x):
    t = jnp.tanh(GELU_C * (x + GELU_A * x * x * x))
    return 0.5 * (1.0 + t) + 0.5 * x * (1.0 - t * t) * GELU_C * (1.0 + 3.0 * GELU_A * x * x)


def _neg_expm1(x):
    series = -x * (1.0 + x * (0.5 + x * (1.0 / 6.0 + x * (1.0 / 24.0))))
    return jnp.where(x > -0.05, series, 1.0 - jnp.exp(x))


def _log_sigmoid(x):
    return jnp.minimum(x, 0.0) - jnp.log1p(jnp.exp(-jnp.abs(x)))


def _shift_down(x, tail, s):
    if s == 0:
        return x
    ext = jnp.concatenate([tail, x], axis=0)
    return pltpu.roll(ext, s, axis=0)[SUBLANES:]


def _shift_up(x, head, s):
    if s == 0:
        return x
    n = x.shape[0]
    ext = jnp.concatenate([x, head], axis=0)
    return pltpu.roll(ext, n + SUBLANES - s, axis=0)[:n]


def _rg_gates(xbr, tail, cw_ref, cb, wr, wi, br, bi, ls):
    taps = [_shift_down(xbr, tail, CONV_W - 1 - k) for k in range(CONV_W)]
    xc = cb
    for k in range(CONV_W):
        xc = xc + cw_ref[pl.ds(k, 1), :] * taps[k]
    xcd = xc.astype(CD)
    r = _sigmoid(jnp.dot(xcd, wr, preferred_element_type=F32) + br)
    i = _sigmoid(jnp.dot(xcd, wi, preferred_element_type=F32) + bi)
    log_a = RG_C * r * ls
    a = jnp.exp(log_a)
    mult = jnp.sqrt(jnp.maximum(_neg_expm1(2.0 * log_a), 0.0))
    return taps, xc, r, i, log_a, a, mult


def _scan8_fwd(a, u):
    row = lax.broadcasted_iota(jnp.int32, a.shape, 0)
    for d in (1, 2, 4):
        a_s = pltpu.roll(a, d, axis=0)
        u_s = pltpu.roll(u, d, axis=0)
        m = row >= d
        u = jnp.where(m, a * u_s + u, u)
        a = jnp.where(m, a * a_s, a)
    return a, u


def _scan8_bwd(b, u):
    row = lax.broadcasted_iota(jnp.int32, b.shape, 0)
    for d in (1, 2, 4):
        b_s = pltpu.roll(b, SUBLANES - d, axis=0)
        u_s = pltpu.roll(u, SUBLANES - d, axis=0)
        m = row < SUBLANES - d
        u = jnp.where(m, b * u_s + u, u)
        b = jnp.where(m, b * b_s, b)
    return b, u


def _rglru_fwd(gate_br, x_br, cw, cb, wr, wi, br, bi, lam, name):
    s, c = x_br.shape
    nt = s // TIME_BLOCK
    tb, cbw = TIME_BLOCK, RG_BW
    groups = tb // SUBLANES

    def body(g_ref, x_ref, tail_ref, cw_ref, cb_ref, wr_ref, wi_ref, br_ref, bi_ref, lam_ref,
             y_ref, hs_ref, carry_ref, a_scr, u_scr):
        t = pl.program_id(1)

        @pl.when(t == 0)
        def _():
            carry_ref[...] = jnp.zeros_like(carry_ref)

        tail = jnp.where(t > 0, tail_ref[...], 0.0)
        ls = _log_sigmoid(lam_ref[...])
        _, xc, _, i, _, a, mult = _rg_gates(x_ref[...], tail, cw_ref, cb_ref[...], wr_ref[0], wi_ref[0],
                                            br_ref[...], bi_ref[...], ls)
        a_scr[...] = a
        u_scr[...] = mult * (i * xc)
        carry = carry_ref[...]
        for gi in range(groups):
            rows = pl.ds(gi * SUBLANES, SUBLANES)
            pa, hl = _scan8_fwd(a_scr[rows, :], u_scr[rows, :])
            hs_ref[rows, :] = hl + pa * carry
            carry = hs_ref[pl.ds(gi * SUBLANES + SUBLANES - 1, 1), :]
        carry_ref[...] = carry
        y_ref[...] = (hs_ref[...] * _gelu(g_ref[...])).astype(CD)

    blk = pl.BlockSpec((tb, cbw), lambda n, t: (t, n))
    tail = pl.BlockSpec((SUBLANES, cbw), lambda n, t: (jnp.maximum(t * groups - 1, 0), n))
    vec = pl.BlockSpec((1, cbw), lambda n, t: (0, n))
    wblk = pl.BlockSpec((1, cbw, cbw), lambda n, t: (n, 0, 0))
    return pl.pallas_call(
        body, name=name,
        out_shape=(jax.ShapeDtypeStruct((s, c), CD), jax.ShapeDtypeStruct((s, c), F32)),
        grid=(RG_BLOCKS, nt),
        in_specs=[blk, blk, tail, pl.BlockSpec((CONV_W, cbw), lambda n, t: (0, n)), vec, wblk, wblk, vec, vec, vec],
        out_specs=(blk, blk),
        scratch_shapes=[pltpu.VMEM((1, cbw), F32), pltpu.VMEM((tb, cbw), F32), pltpu.VMEM((tb, cbw), F32)],
        compiler_params=_params("parallel", "arbitrary"),
    )(gate_br, x_br, x_br, cw, cb, wr, wi, br, bi, lam)


def _rglru_bwd(dy, gate_br, x_br, hs, cw, cb, wr, wi, wrt, wit, br, bi, lam, name):
    s, c = x_br.shape
    nt = s // TIME_BLOCK
    tb, cbw = TIME_BLOCK, RG_BW
    groups = tb // SUBLANES

    def body(dy_ref, g_ref, x_ref, tail_ref, hs_ref, hprev_ref, cw_ref, cb_ref, wr_ref, wi_ref, wrt_ref, wit_ref,
             br_ref, bi_ref, lam_ref,
             dg_ref, dx_ref, dcw_ref, dcb_ref, dbr_ref, dbi_ref, dlam_ref, dwr_ref, dwi_ref,
             carry_ref, head_ref, b_scr, u_scr, dh_scr):
        tr = pl.program_id(1)
        first_block = tr == nt - 1

        @pl.when(tr == 0)
        def _():
            carry_ref[...] = jnp.zeros_like(carry_ref)
            head_ref[...] = jnp.zeros_like(head_ref)
            for ref in (dcw_ref, dcb_ref, dbr_ref, dbi_ref, dlam_ref, dwr_ref, dwi_ref):
                ref[...] = jnp.zeros_like(ref)

        tail = jnp.where(first_block, 0.0, tail_ref[...])
        lam_v = lam_ref[...]
        ls = _log_sigmoid(lam_v)
        taps, xc, r, i, log_a, a, mult = _rg_gates(x_ref[...], tail, cw_ref, cb_ref[...], wr_ref[0], wi_ref[0],
                                                   br_ref[...], bi_ref[...], ls)
        gate_v = g_ref[...]
        dyv = dy_ref[...]
        hsv = hs_ref[...]
        dg_ref[...] = (dyv * hsv * _gelu_grad(gate_v)).astype(CD)

        row = lax.broadcasted_iota(jnp.int32, a.shape, 0)
        b_scr[...] = jnp.where(row == tb - 1, 1.0, pltpu.roll(a, tb - 1, axis=0))
        u_scr[...] = dyv * _gelu(gate_v)
        carry = carry_ref[...]
        for gi in reversed(range(groups)):
            rows = pl.ds(gi * SUBLANES, SUBLANES)
            pb, gl = _scan8_bwd(b_scr[rows, :], u_scr[rows, :])
            dh_scr[rows, :] = gl + pb * carry
            carry = dh_scr[pl.ds(gi * SUBLANES, 1), :]
        dh = dh_scr[...]
        carry_ref[...] = carry * jnp.sum(jnp.where(row == 0, a, 0.0), axis=0, keepdims=True)

        hprev_tail = jnp.where(first_block, 0.0, hprev_ref[...])
        h_prev = _shift_down(hsv, hprev_tail, 1)
        da = dh * h_prev
        ixc = i * xc
        dmult = dh * ixc
        di = dh * mult * xc
        dxc = dh * mult * i
        a2 = a * a
        dlog_a = da * a - dmult * a2 / mult
        dpre_r = (dlog_a * (RG_C * ls)) * r * (1.0 - r)
        dpre_i = di * i * (1.0 - i)
        dlam_ref[...] += jnp.sum(dlog_a * r, axis=0, keepdims=True) * (RG_C * _sigmoid(-lam_v))
        dbr_ref[...] += jnp.sum(dpre_r, axis=0, keepdims=True)
        dbi_ref[...] += jnp.sum(dpre_i, axis=0, keepdims=True)
        xcd = xc.astype(CD)
        dprc = dpre_r.astype(CD)
        dpic = dpre_i.astype(CD)
        tn_dims = (((0,), (0,)), ((), ()))
        dwr_ref[0] += lax.dot_general(xcd, dprc, tn_dims, preferred_element_type=F32)
        dwi_ref[0] += lax.dot_general(xcd, dpic, tn_dims, preferred_element_type=F32)
        dxc = dxc + jnp.dot(dprc, wrt_ref[0], preferred_element_type=F32) + jnp.dot(dpic, wit_ref[0],
                                                                                    preferred_element_type=F32)
        dcb_ref[...] += jnp.sum(dxc, axis=0, keepdims=True)
        for k in range(CONV_W):
            dcw_ref[pl.ds(k, 1), :] += jnp.sum(dxc * taps[k], axis=0, keepdims=True)
        head = head_ref[...]
        dxb = jnp.zeros_like(dxc)
        for sft in range(CONV_W):
            dxb = dxb + cw_ref[pl.ds(CONV_W - 1 - sft, 1), :] * _shift_up(dxc, head, sft)
        dx_ref[...] = dxb.astype(CD)
        head_ref[...] = dxc[0:SUBLANES, :]

    blk = pl.BlockSpec((tb, cbw), lambda n, t: (nt - 1 - t, n))
    tail = pl.BlockSpec((SUBLANES, cbw), lambda n, t: (jnp.maximum((nt - 1 - t) * groups - 1, 0), n))
    vec = pl.BlockSpec((1, cbw), lambda n, t: (0, n))
    cwb = pl.BlockSpec((CONV_W, cbw), lambda n, t: (0, n))
    wblk = pl.BlockSpec((1, cbw, cbw), lambda n, t: (n, 0, 0))
    vshape = jax.ShapeDtypeStruct((1, c), F32)
    wshape = jax.ShapeDtypeStruct((RG_BLOCKS, cbw, cbw), F32)
    return pl.pallas_call(
        body, name=name,
        out_shape=(jax.ShapeDtypeStruct((s, c), CD), jax.ShapeDtypeStruct((s, c), CD),
                   jax.ShapeDtypeStruct((CONV_W, c), F32), vshape, vshape, vshape, vshape, wshape, wshape),
        grid=(RG_BLOCKS, nt),
        in_specs=[blk, blk, blk, tail, blk, tail, cwb, vec, wblk, wblk, wblk, wblk, vec, vec, vec],
        out_specs=(blk, blk, cwb, vec, vec, vec, vec, wblk, wblk),
        scratch_shapes=[pltpu.VMEM((1, cbw), F32), pltpu.VMEM((SUBLANES, cbw), F32),
                        pltpu.VMEM((tb, cbw), F32), pltpu.VMEM((tb, cbw), F32), pltpu.VMEM((tb, cbw), F32)],
        compiler_params=_params("parallel", "arbitrary"),
    )(dy, gate_br, x_br, x_br, hs, hs, cw, cb, wr, wi, wrt, wit, br, bi, lam)


ATT_BLOCK = 256
ATT_Q_BLOCK = 1024
ATT_RATIO = ATT_Q_BLOCK // ATT_BLOCK
ATT_SCALE = 1.0 / math.sqrt(SB_HEAD_DIM)
N_PAIRS = SB_HEADS * SB_HEAD_DIM // LANES
NT_DIMS = (((1,), (1,)), ((), ()))
TN_DIMS = (((0,), (0,)), ((), ()))


LOG2E = 1.4426950408889634


def _neg_abs(x):
    bits = lax.bitcast_convert_type(x, jnp.uint32) | jnp.uint32(0x80000000)
    return lax.bitcast_convert_type(bits, F32)


def _qk(qx, kb):
    return lax.dot_general(qx, kb, NT_DIMS, preferred_element_type=F32)


def _sb_logits(qk, valid):
    z2 = qk * (ATT_SCALE * LOG2E)
    lb2 = jnp.minimum(z2, 0.0) - jnp.log2(1.0 + jnp.exp2(_neg_abs(z2)))
    l2 = lb2 - z2
    if valid is not None:
        l2 = jnp.where(valid, l2, 0.0)
    return lb2, l2


def _hi_lo(x):
    hi = x.astype(CD)
    lo = (x - hi.astype(F32)).astype(CD)
    return jnp.concatenate([hi, lo], axis=1)


def _tri(strict, stacked):
    r = lax.broadcasted_iota(jnp.int32, (ATT_BLOCK, ATT_BLOCK), 0)
    c = lax.broadcasted_iota(jnp.int32, (ATT_BLOCK, ATT_BLOCK), 1)
    m = (r > c if strict else r >= c).astype(CD)
    return jnp.concatenate([m, m], axis=0) if stacked else m


def _attn_fwd(qkv, name):
    _, s, _ = qkv.shape
    tq, t = ATT_Q_BLOCK, ATT_BLOCK
    nblk = s // tq

    def body(q_ref, k_ref, v_ref, o_ref, qk_scr, w_scr):
        i = pl.program_id(1)
        lane = lax.broadcasted_iota(jnp.int32, (1, LANES), 1)
        head_masks = (lane < SB_HEAD_DIM, lane >= SB_HEAD_DIM)
        q = q_ref[0]
        qs = [jnp.where(m, q, jnp.zeros_like(q)) for m in head_masks]
        tri = _tri(True, False)
        rr = lax.broadcasted_iota(jnp.int32, (tq, t), 0)
        cc = lax.broadcasted_iota(jnp.int32, (tq, t), 1)

        def rows_of(j):
            return pl.ds(pl.multiple_of(j * t, t), t)

        def tail(x, row0):
            return x if row0 == 0 else x[row0:]

        def start_logits(j, row0=0):
            kb = k_ref[0, rows_of(j), :]
            for hd in range(2):
                qk_scr[hd, row0:, :] = _qk(tail(qs[hd], row0), kb)

        def weights(run, diagonal=False, row0=0):
            new_run = []
            valid = (cc < rr)[:tq - row0] if diagonal else None
            for hd in range(2):
                lb2, l2 = _sb_logits(qk_scr[hd, row0:, :], valid)
                w = jnp.exp2(lb2 + (tail(run[hd], row0) + jnp.dot(l2.astype(CD), tri, preferred_element_type=F32)))
                if valid is not None:
                    w = jnp.where(valid, w, 0.0)
                w_scr[row0:, hd * t:(hd + 1) * t] = w.astype(CD)
                rowsum = jnp.sum(l2, axis=1, keepdims=True)
                if row0:
                    rowsum = jnp.concatenate([jnp.zeros((row0, 1), F32), rowsum], axis=0)
                new_run.append(run[hd] + rowsum)
            return tuple(new_run)

        def apply_weights(j, row0=0):
            vb = v_ref[0, rows_of(j), :]
            vcat = jnp.concatenate([jnp.where(m, vb, jnp.zeros_like(vb)) for m in head_masks], axis=0)
            inc = jnp.dot(w_scr[row0:, :], vcat, preferred_element_type=F32)
            return inc if row0 == 0 else jnp.concatenate([jnp.zeros((row0, LANES), F32), inc], axis=0)

        zero = jnp.zeros((tq, 1), F32)
        last = ATT_RATIO - 1
        start_logits(ATT_RATIO * i + last, last * t)
        run = weights((zero, zero), True, last * t)
        oacc = jnp.zeros((tq, LANES), F32)
        for d in reversed(range(last)):
            start_logits(ATT_RATIO * i + d, d * t)
            oacc = oacc + apply_weights(ATT_RATIO * i + d + 1, (d + 1) * t)
            run = weights(run, True, d * t)
        start_logits(jnp.maximum(ATT_RATIO * i - 1, 0))

        def step(jj, carry):
            run, oacc = carry
            b = ATT_RATIO * i - 1 - jj
            oacc = oacc + apply_weights(b + 1)
            run = weights(run)
            start_logits(jnp.maximum(b - 1, 0))
            return run, oacc

        run, oacc = lax.fori_loop(0, ATT_RATIO * i, step, (run, oacc))
        o_ref[0] = oacc + apply_weights(0)

    return pl.pallas_call(
        body, name=name, out_shape=jax.ShapeDtypeStruct((N_PAIRS, s, LANES), F32), grid=(N_PAIRS, nblk),
        in_specs=[pl.BlockSpec((1, tq, LANES), lambda p, i: (p, i, 0)),
                  pl.BlockSpec((1, s, LANES), lambda p, i: (N_PAIRS + p, 0, 0)),
                  pl.BlockSpec((1, s, LANES), lambda p, i: (2 * N_PAIRS + p, 0, 0))],
        out_specs=pl.BlockSpec((1, tq, LANES), lambda p, i: (p, i, 0)),
        scratch_shapes=[pltpu.VMEM((2, tq, t), F32), pltpu.VMEM((tq, 2 * t), CD)],
        compiler_params=_params("parallel", "arbitrary"),
    )(qkv, qkv, qkv)


def _attn_bwd(qkv, o, do, name):
    _, s, _ = qkv.shape
    tq, t = ATT_Q_BLOCK, ATT_BLOCK
    nblk = s // tq

    def body(q_ref, k_ref, v_ref, o_ref, do_ref, dq_ref, dk_ref, dv_ref, qk_scr, dw_scr, w_scr, dz_scr):
        i = pl.program_id(1)

        @pl.when(i == 0)
        def _():
            dk_ref[...] = jnp.zeros_like(dk_ref)
            dv_ref[...] = jnp.zeros_like(dv_ref)

        lane = lax.broadcasted_iota(jnp.int32, (1, LANES), 1)
        head_masks = (lane < SB_HEAD_DIM, lane >= SB_HEAD_DIM)
        q = q_ref[0]
        dov = do_ref[0]
        ov = o_ref[0]
        qs = [jnp.where(m, q, jnp.zeros_like(q)) for m in head_masks]
        q_scaled_t = jnp.concatenate([(qx.astype(F32) * ATT_SCALE).T for qx in qs], axis=1).astype(CD)
        docs = [jnp.where(m, dov, jnp.zeros_like(dov)) for m in head_masks]
        docat_t = jnp.concatenate([d.astype(F32).T for d in docs], axis=1).astype(CD)
        totals = [jnp.sum(d.astype(F32) * ov, axis=1, keepdims=True) for d in docs]
        tri = _tri(True, False)
        tri_incl = _tri(False, True)
        rr = lax.broadcasted_iota(jnp.int32, (tq, t), 0)
        cc = lax.broadcasted_iota(jnp.int32, (tq, t), 1)

        def rows_of(j):
            return pl.ds(pl.multiple_of(j * t, t), t)

        def tail(x, row0):
            return x if row0 == 0 else x[row0:]

        def pad_rows(x, row0):
            return x if row0 == 0 else jnp.concatenate([jnp.zeros((row0, x.shape[1]), x.dtype), x], axis=0)

        def start_products(j, row0=0):
            kb = k_ref[0, rows_of(j), :]
            vb = v_ref[0, rows_of(j), :]
            for hd in range(2):
                qk_scr[hd, row0:, :] = _qk(tail(qs[hd], row0), kb)
                dw_scr[hd, row0:, :] = lax.dot_general(tail(docs[hd], row0), vb, NT_DIMS, preferred_element_type=F32)

        def logit_grads(run, erun, diagonal=False, row0=0):
            new_run, new_erun = [], []
            valid = (cc < rr)[:tq - row0] if diagonal else None
            for hd in range(2):
                lb2, l2 = _sb_logits(qk_scr[hd, row0:, :], valid)
                w = jnp.exp2(lb2 + (tail(run[hd], row0) + jnp.dot(l2.astype(CD), tri, preferred_element_type=F32)))
                if valid is not None:
                    w = jnp.where(valid, w, 0.0)
                wc = w.astype(CD)
                w_scr[hd * tq + row0:(hd + 1) * tq, :] = wc
                e = dw_scr[hd, row0:, :] * wc.astype(F32)
                prefix = (tail(totals[hd] - erun[hd], row0)
                          - jnp.dot(_hi_lo(e), tri_incl, preferred_element_type=F32))
                dz = e - jnp.exp2(lb2) * (e + prefix)
                if valid is not None:
                    dz = jnp.where(valid, dz, 0.0)
                dz_scr[hd * tq + row0:(hd + 1) * tq, :] = dz.astype(CD)
                new_run.append(run[hd] + pad_rows(jnp.sum(l2, axis=1, keepdims=True), row0))
                new_erun.append(erun[hd] + pad_rows(jnp.sum(e, axis=1, keepdims=True), row0))
            return tuple(new_run), tuple(new_erun)

        def apply_grads(j, row0=0):
            rows = rows_of(j)
            kb = k_ref[0, rows, :]
            kcat = jnp.concatenate([jnp.where(m, kb, jnp.zeros_like(kb)) for m in head_masks], axis=0)
            dz_heads = [dz_scr[hd * tq + row0:(hd + 1) * tq, :] for hd in range(2)]
            w_heads = [w_scr[hd * tq + row0:(hd + 1) * tq, :] for hd in range(2)]
            q_t = jnp.concatenate([q_scaled_t[:, hd * tq + row0:(hd + 1) * tq] for hd in range(2)], axis=1)
            do_t = jnp.concatenate([docat_t[:, hd * tq + row0:(hd + 1) * tq] for hd in range(2)], axis=1)
            dk_ref[0, :, rows] += jnp.dot(q_t, jnp.concatenate(dz_heads, axis=0), preferred_element_type=F32)
            dv_ref[0, :, rows] += jnp.dot(do_t, jnp.concatenate(w_heads, axis=0), preferred_element_type=F32)
            return pad_rows(jnp.dot(jnp.concatenate(dz_heads, axis=1), kcat, preferred_element_type=F32), row0)

        zero = jnp.zeros((tq, 1), F32)
        last = ATT_RATIO - 1
        start_products(ATT_RATIO * i + last, last * t)
        run, erun = logit_grads((zero, zero), (zero, zero), True, last * t)
        dqacc = jnp.zeros((tq, LANES), F32)
        for d in reversed(range(last)):
            start_products(ATT_RATIO * i + d, d * t)
            dqacc = dqacc + apply_grads(ATT_RATIO * i + d + 1, (d + 1) * t)
            run, erun = logit_grads(run, erun, True, d * t)
        start_products(jnp.maximum(ATT_RATIO * i - 1, 0))

        def step(jj, carry):
            run, erun, dqacc = carry
            b = ATT_RATIO * i - 1 - jj
            dqacc = dqacc + apply_grads(b + 1)
            run, erun = logit_grads(run, erun)
            start_products(jnp.maximum(b - 1, 0))
            return run, erun, dqacc

        run, erun, dqacc = lax.fori_loop(0, ATT_RATIO * i, step, (run, erun, dqacc))
        dq_ref[0] = ((dqacc + apply_grads(0)) * ATT_SCALE).astype(CD)

    qblk = pl.BlockSpec((1, tq, LANES), lambda p, i: (p, i, 0))
    full = pl.BlockSpec((1, LANES, s), lambda p, i: (p, 0, 0))
    shape = jax.ShapeDtypeStruct((N_PAIRS, s, LANES), F32)
    shape_t = jax.ShapeDtypeStruct((N_PAIRS, LANES, s), F32)
    dq, dk_t, dv_t = pl.pallas_call(
        body, name=name, out_shape=(jax.ShapeDtypeStruct(shape.shape, CD), shape_t, shape_t), grid=(N_PAIRS, nblk),
        in_specs=[qblk,
                  pl.BlockSpec((1, s, LANES), lambda p, i: (N_PAIRS + p, 0, 0)),
                  pl.BlockSpec((1, s, LANES), lambda p, i: (2 * N_PAIRS + p, 0, 0)),
                  qblk, qblk],
        out_specs=(qblk, full, full),
        scratch_shapes=[pltpu.VMEM((2, tq, t), F32), pltpu.VMEM((2, tq, t), F32),
                        pltpu.VMEM((2 * tq, t), CD), pltpu.VMEM((2 * tq, t), CD)],
        compiler_params=_params("parallel", "arbitrary"),
    )(qkv, qkv, qkv, o, do)
    return dq, jnp.swapaxes(dk_t, 1, 2).astype(CD), jnp.swapaxes(dv_t, 1, 2).astype(CD)


def _adamw(w, g, m, v, name):
    shape = w.shape
    rows, cols = (shape[-2], shape[-1]) if len(shape) >= 2 else (1, shape[-1])
    lead = w.size // (rows * cols)
    tr = _pick(rows, (512, 256, 128, 64, 32, 16, 8))

    def body(w_ref, g_ref, m_ref, v_ref, d_ref, nm_ref, nv_ref):
        gv = g_ref[...]
        nm = ADAM_B1 * m_ref[...] + (1.0 - ADAM_B1) * gv
        nv = ADAM_B2 * v_ref[...] + (1.0 - ADAM_B2) * (gv * gv)
        m_hat = nm / (1.0 - ADAM_B1 ** ADAM_STEP)
        v_hat = nv / (1.0 - ADAM_B2 ** ADAM_STEP)
        d_ref[...] = -ADAM_LR * (m_hat / (jnp.sqrt(v_hat) + ADAM_EPS) + ADAM_WD * w_ref[...])
        nm_ref[...] = nm
        nv_ref[...] = nv

    blk = pl.BlockSpec((1, tr, cols), lambda l, i: (l, i, 0))
    out = jax.ShapeDtypeStruct((lead, rows, cols), F32)
    d, nm, nv = pl.pallas_call(
        body, name=name, out_shape=(out, out, out), grid=(lead, rows // tr),
        in_specs=[blk, blk, blk, blk], out_specs=(blk, blk, blk), compiler_params=_params("parallel", "parallel"),
    )(*[a.reshape(lead, rows, cols) for a in (w, g, m, v)])
    return d.reshape(shape), nm.reshape(shape), nv.reshape(shape)


HBM = pl.BlockSpec(memory_space=pltpu.HBM)


def _coords():
    return lax.axis_index("x"), lax.axis_index("y"), lax.axis_index("c")


def _other_chips(x, y):
    return [(1 - x, y), (x, 1 - y), (1 - x, 1 - y)]


def _allgather_chips(shard, name, collective_id=None):
    r, cols = shard.shape
    half = r // 2
    quarter = half // 2

    def body(src_ref, out_ref, send_sems, recv_sems):
        x, y, c = _coords()
        sibling = (x, y, 1 - c)
        nx, ny, diag = (1 - x, y), (x, 1 - y), (1 - x, 1 - y)

        def piece(chip, core, lo, n):
            return out_ref.at[2 * chip[0] + chip[1], pl.ds(core * half + lo, n), :]

        def copy(k, dst, to, src=None):
            return pltpu.make_async_remote_copy(
                src_ref=dst if src is None else src, dst_ref=dst,
                send_sem=send_sems.at[k], recv_sem=recv_sems.at[k], device_id=to, device_id_type=MESH)

        me = (x, y)
        mine = src_ref.at[pl.ds(c * half, half), :]
        direct = [copy(0, piece(me, c, 0, half), (*nx, c), src=mine), copy(1, piece(me, c, 0, half), (*ny, c), src=mine)]
        for cp in direct:
            cp.start()
        arrivals = [piece(nx, c, 0, half), piece(ny, c, 0, half), piece(diag, c, 0, quarter),
                    piece(diag, c, quarter, quarter)]
        onward = [copy(2, piece(nx, c, 0, quarter), (*ny, c)), copy(3, piece(ny, c, quarter, quarter), (*nx, c))]
        to_sibling = [copy(4 + k, dst, sibling) for k, dst in enumerate(arrivals)]
        for k, dst in enumerate(arrivals):
            copy(k, dst, (x, y, c)).wait_recv()
            if k < 2:
                onward[k].start()
            to_sibling[k].start()
        from_sibling = [piece(nx, 1 - c, 0, half), piece(ny, 1 - c, 0, half), piece(diag, 1 - c, 0, quarter),
                        piece(diag, 1 - c, quarter, quarter)]
        for k, dst in enumerate(from_sibling):
            copy(4 + k, dst, (x, y, c)).wait_recv()
        for cp in direct + onward + to_sibling:
            cp.wait_send()

    out_shape = jax.ShapeDtypeStruct((N_CHIPS, r, cols), shard.dtype)
    sems = (pltpu.SemaphoreType.DMA((8,)), pltpu.SemaphoreType.DMA((8,)))
    if collective_id is None:
        return pl.pallas_call(body, name=name, out_shape=out_shape, in_specs=[HBM], out_specs=HBM,
                              scratch_shapes=list(sems))(shard)
    shard_ref = jax.new_ref(shard, memory_space=pltpu.MemorySpace.HBM)
    gathered_ref = jax.empty_ref(out_shape, memory_space=pltpu.MemorySpace.HBM)

    @_sequencer(name, collective_id, sems)
    def launch(send_sems, recv_sems):
        x, y, c = _coords()
        _handshake([(1 - x, y, c), (x, 1 - y, c), (x, y, 1 - c)])
        body(shard_ref, gathered_ref, send_sems, recv_sems)

    launch()
    return gathered_ref[...]


def _exchange_sibling_halves(g, name):
    n, r, cols = g.shape
    half = r // 2

    def body(g_ref, out_ref, send_sem, recv_sem):
        x, y, c = _coords()
        cp = pltpu.make_async_remote_copy(
            src_ref=g_ref.at[:, pl.ds((1 - c) * half, half), :], dst_ref=out_ref,
            send_sem=send_sem, recv_sem=recv_sem, device_id=(x, y, 1 - c), device_id_type=MESH)
        cp.start()
        cp.wait()

    return pl.pallas_call(
        body, name=name, out_shape=jax.ShapeDtypeStruct((n, half, cols), g.dtype),
        in_specs=[HBM], out_specs=HBM,
        scratch_shapes=[pltpu.SemaphoreType.DMA, pltpu.SemaphoreType.DMA],
    )(g)


def _sequencer(name, collective_id, scratch_types):
    return pl.kernel(mesh=plsc.ScalarSubcoreMesh(axis_name="sequencer", num_cores=1), name=name,
                     scratch_types=scratch_types, compiler_params=pltpu.CompilerParams(collective_id=collective_id))


def _handshake(peers):
    barrier = pltpu.get_barrier_semaphore()
    for peer in peers:
        pl.semaphore_signal(barrier, inc=1, device_id=peer, device_id_type=MESH)
    pl.semaphore_wait(barrier, len(peers))


def _exchange_sibling_halves_async(g, name, collective_id):
    n, r, cols = g.shape
    half = r // 2
    g_ref = jax.new_ref(g, memory_space=pltpu.MemorySpace.HBM)
    out_ref = jax.empty_ref(jax.ShapeDtypeStruct((n, half, cols), g.dtype), memory_space=pltpu.MemorySpace.HBM)

    @_sequencer(name, collective_id, (pltpu.SemaphoreType.DMA, pltpu.SemaphoreType.DMA))
    def launch(send_sem, recv_sem):
        x, y, c = _coords()
        _handshake([(x, y, 1 - c)])
        cp = pltpu.make_async_remote_copy(
            src_ref=g_ref.at[:, pl.ds((1 - c) * half, half), :], dst_ref=out_ref,
            send_sem=send_sem, recv_sem=recv_sem, device_id=(x, y, 1 - c), device_id_type=MESH)
        cp.start()
        cp.wait()

    launch()
    return out_ref[...]


def _share_halves_async(v, name, collective_id):
    h = v.shape[0] // 2
    v_ref = jax.new_ref(v, memory_space=pltpu.MemorySpace.HBM)

    @_sequencer(name, collective_id, (pltpu.SemaphoreType.DMA, pltpu.SemaphoreType.DMA))
    def launch(send_sem, recv_sem):
        x, y, c = _coords()
        _handshake([(x, y, 1 - c)])
        cp = pltpu.make_async_remote_copy(
            src_ref=v_ref.at[pl.ds(c * h, h), :], dst_ref=v_ref.at[pl.ds(c * h, h), :],
            send_sem=send_sem, recv_sem=recv_sem, device_id=(x, y, 1 - c), device_id_type=MESH)
        cp.start()
        pltpu.make_async_remote_copy(
            src_ref=v_ref.at[pl.ds(c * h, h), :], dst_ref=v_ref.at[pl.ds((1 - c) * h, h), :],
            send_sem=send_sem, recv_sem=recv_sem, device_id=(x, y, 1 - c), device_id_type=MESH).wait_recv()
        cp.wait_send()

    launch()
    return v_ref[...]


def _scatter_to_chips_async(p, name, collective_id):
    p_ref = jax.new_ref(p, memory_space=pltpu.MemorySpace.HBM)
    out_ref = jax.empty_ref(jax.ShapeDtypeStruct(p.shape, p.dtype), memory_space=pltpu.MemorySpace.HBM)

    @_sequencer(name, collective_id, (pltpu.SemaphoreType.DMA((3,)), pltpu.SemaphoreType.DMA((3,))))
    def launch(send_sems, recv_sems):
        x, y, c = _coords()
        me = 2 * x + y
        _handshake([(px, py, c) for px, py in _other_chips(x, y)])
        sends = []
        for j, (px, py) in enumerate(_other_chips(x, y)):
            sends.append(pltpu.make_async_remote_copy(
                src_ref=p_ref.at[2 * px + py], dst_ref=out_ref.at[me],
                send_sem=send_sems.at[j], recv_sem=recv_sems.at[j], device_id=(px, py, c), device_id_type=MESH))
        for cp in sends:
            cp.start()
        for j, (px, py) in enumerate(_other_chips(x, y)):
            pltpu.make_async_remote_copy(
                src_ref=p_ref.at[me], dst_ref=out_ref.at[2 * px + py],
                send_sem=send_sems.at[j], recv_sem=recv_sems.at[j], device_id=(px, py, c),
                device_id_type=MESH).wait_recv()
        for cp in sends:
            cp.wait_send()

    launch()
    return out_ref[...]


def _share_halves(v, name):
    h = v.shape[0] // 2

    def body(v_ref, out_ref, send_sem, recv_sem):
        x, y, c = _coords()
        cp = pltpu.make_async_remote_copy(
            src_ref=v_ref.at[pl.ds(c * h, h), :], dst_ref=out_ref.at[pl.ds(c * h, h), :],
            send_sem=send_sem, recv_sem=recv_sem, device_id=(x, y, 1 - c), device_id_type=MESH)
        cp.start()
        pltpu.make_async_remote_copy(
            src_ref=v_ref.at[pl.ds(c * h, h), :], dst_ref=out_ref.at[pl.ds((1 - c) * h, h), :],
            send_sem=send_sem, recv_sem=recv_sem, device_id=(x, y, 1 - c), device_id_type=MESH).wait_recv()
        cp.wait_send()

    return pl.pallas_call(
        body, name=name, out_shape=jax.ShapeDtypeStruct(v.shape, v.dtype),
        in_specs=[HBM], out_specs=HBM, input_output_aliases={0: 0},
        scratch_shapes=[pltpu.SemaphoreType.DMA, pltpu.SemaphoreType.DMA],
    )(v)


def _allreduce_small(v, name):
    r, cols = v.shape

    def body(v_ref, out_ref, buf_ref, send_sems, recv_sems):
        x, y, c = _coords()
        me = 4 * x + 2 * y + c
        buf_ref[me] = v_ref[...]
        sends = []
        for k in range(1, N_DEV):
            px = 1 - x if k & 4 else x
            py = 1 - y if k & 2 else y
            pc = 1 - c if k & 1 else c
            sends.append(pltpu.make_async_remote_copy(
                src_ref=v_ref, dst_ref=buf_ref.at[me], send_sem=send_sems.at[k - 1], recv_sem=recv_sems.at[k - 1],
                device_id=(px, py, pc), device_id_type=MESH))
        for cp in sends:
            cp.start()
        for cp in sends:
            cp.wait()
        acc = buf_ref[0]
        for d in range(1, N_DEV):
            acc = acc + buf_ref[d]
        out_ref[...] = acc

    return pl.pallas_call(
        body, name=name, out_shape=jax.ShapeDtypeStruct((r, cols), F32),
        in_specs=[pl.BlockSpec(memory_space=pltpu.VMEM)], out_specs=pl.BlockSpec(memory_space=pltpu.VMEM),
        scratch_shapes=[pltpu.VMEM((N_DEV, r, cols), F32), pltpu.SemaphoreType.DMA((N_DEV - 1,)),
                        pltpu.SemaphoreType.DMA((N_DEV - 1,))],
    )(v)


def _add_sibling(g, from_sibling, core, name):
    n, h, cols = from_sibling.shape
    tr = _row_tile(h)
    steps = h // tr

    def body(core_ref, a_ref, b_ref, o_ref):
        o_ref[...] = (a_ref[...] + b_ref[...]).astype(o_ref.dtype)

    return pl.pallas_call(
        body, name=name, out_shape=jax.ShapeDtypeStruct(from_sibling.shape, jnp.bfloat16),
        grid_spec=pltpu.PrefetchScalarGridSpec(
            num_scalar_prefetch=1, grid=(n, steps),
            in_specs=[pl.BlockSpec((1, tr, cols), lambda s, i, core_ref: (s, core_ref[0] * steps + i, 0)),
                      pl.BlockSpec((1, tr, cols), lambda s, i, core_ref: (s, i, 0))],
            out_specs=pl.BlockSpec((1, tr, cols), lambda s, i, core_ref: (s, i, 0))),
        compiler_params=_params("parallel", "parallel"),
    )(core.reshape(1).astype(jnp.int32), g, from_sibling)


def _sum_slots(p, own, chip, core, name):
    n, r, cols = p.shape
    tr = _row_tile(r)
    steps = r // tr

    def body(core_ref, chip_ref, p_ref, own_ref, o_ref):
        parts = [jnp.where(chip_ref[0] == s, own_ref[0], p_ref[s]).astype(F32) for s in range(n)]
        o_ref[...] = ((parts[0] + parts[1]) + parts[2]) + parts[3]

    return pl.pallas_call(
        body, name=name, out_shape=jax.ShapeDtypeStruct((2 * r, cols), F32),
        grid_spec=pltpu.PrefetchScalarGridSpec(
            num_scalar_prefetch=2, grid=(steps,),
            in_specs=[pl.BlockSpec((n, tr, cols), lambda i, core_ref, chip_ref: (0, i, 0)),
                      pl.BlockSpec((1, tr, cols), lambda i, core_ref, chip_ref: (chip_ref[0], i, 0))],
            out_specs=pl.BlockSpec((tr, cols), lambda i, core_ref, chip_ref: (core_ref[0] * steps + i, 0))),
        compiler_params=_params("parallel"),
    )(core.reshape(1).astype(jnp.int32), chip.reshape(1).astype(jnp.int32), p, own)


PACK_COLS = 1024


def _pack_shards(parts):
    return jnp.concatenate([p.reshape(-1, PACK_COLS) for p in parts], axis=0)


def _unpack_shards(buf, shapes):
    out, row = [], 0
    for shp in shapes:
        nrows = math.prod(shp) // PACK_COLS
        out.append(buf[..., row:row + nrows, :].reshape(buf.shape[:-2] + tuple(shp)))
        row += nrows
    return out


def _local_step(x, target, w):
    t = lambda a: a.T
    g = {}
    w_in_g, w_in_x = w["a_w_in"][:, :D_RNN], w["a_w_in"][:, D_RNN:]
    h0, gate_br, x_br = _norm_and_project(x, w["norm_mix_g"][0], w_in_g, w_in_x, "rglru_in")
    y_a, hs = _rglru_fwd(gate_br, x_br, w["a_conv_w"], w["a_conv_b"], w["a_w_r"], w["a_w_i"], w["a_b_r"],
                         w["a_b_i"], w["a_lambda"], "rglru_fwd")
    x1, h1 = _matmul([(y_a, w["a_w_out"])], F32, "mm_a_out", addend=x, norm_gain=w["norm_ffn_g"][0])
    fg0, fu0, act0 = _ffn_up(h1, w["ffn_w_gate"][0], w["ffn_w_up"][0], "ffn0_up")
    x2, h2 = _matmul([(act0, w["ffn_w_down"][0])], F32, "mm_f0_down", addend=x1, norm_gain=w["norm_mix_g"][1],
                     tk=D_FF)
    qkv = _matmul([(h2, w["b_w_qkv"])], CD, "mm_b_qkv", out_lbm=True, tn=3 * D_MODEL)
    o = _attn_fwd(qkv, "attn_fwd")
    x3, h3 = _matmul([(o, w["b_w_out"])], F32, "mm_b_out", a_lbm=True, addend=x2, norm_gain=w["norm_ffn_g"][1])
    fg1, fu1, act1 = _ffn_up(h3, w["ffn_w_gate"][1], w["ffn_w_up"][1], "ffn1_up")
    dx4, dx4c, g["final_g"], loss = _matmul([(act1, w["ffn_w_down"][1])], F32, "mm_f1_down", addend=x3,
                                            loss_head=(w["final_g"], target), tk=D_FF)

    def ffn_bwd(dx_out, dxc, h, x_in, fg, fu, act, layer, tag):
        dg, du = _ffn_dact(dxc, t(w["ffn_w_down"][layer]), fg, fu, "ffn_" + tag + "_dact")
        dwd = _matmul([(act, dxc)], F32, "mm_" + tag + "_dwd", trans_a=True)
        dwg = _matmul([(h, dg)], F32, "mm_" + tag + "_dwg", trans_a=True)
        dwu = _matmul([(h, du)], F32, "mm_" + tag + "_dwu", trans_a=True)
        dx_in, dx_in_c, dgain = _matmul([(dg, t(w["ffn_w_gate"][layer])), (du, t(w["ffn_w_up"][layer]))], F32,
                                        "mm_" + tag + "_dh", norm_bwd=(x_in, w["norm_ffn_g"][layer], dx_out),
                                        tk=D_FF, tm=256)
        return dx_in, dx_in_c, dgain, dwg, dwu, dwd

    dx3, dx3c, dgf1, dwg1, dwu1, dwd1 = ffn_bwd(dx4, dx4c, h3, x3, fg1, fu1, act1, 1, "f1")
    do = _matmul([(dx3c, t(w["b_w_out"]))], CD, "mm_b_do", out_lbm=True, tn=1024)
    g["b_w_out"] = _matmul([(o, dx3c)], F32, "mm_b_dwout", trans_a=True, a_lbm=True)
    dq, dk, dv = _attn_bwd(qkv, o, do, "attn_bwd")
    wq_t = t(w["b_w_qkv"])
    parts = (dq, dk, dv)
    g["b_w_qkv"] = jnp.concatenate(
        [_matmul([(h2, p)], F32, "mm_b_dwqkv%d" % n, trans_a=True, b_lbm=True) for n, p in enumerate(parts)], axis=1)
    dx2, dx2c, dgm1 = _matmul([(p, wq_t[n * D_MODEL:(n + 1) * D_MODEL]) for n, p in enumerate(parts)], F32, "mm_b_dh",
                              a_lbm=True, norm_bwd=(x2, w["norm_mix_g"][1], dx3))
    dx1, dx1c, dgf0, dwg0, dwu0, dwd0 = ffn_bwd(dx2, dx2c, h1, x1, fg0, fu0, act0, 0, "f0")
    dy_a = _matmul([(dx1c, t(w["a_w_out"]))], F32, "mm_a_dy")
    g["a_w_out"] = _matmul([(y_a, dx1c)], F32, "mm_a_dwout", trans_a=True)
    wrt = jnp.swapaxes(w["a_w_r"], 1, 2)
    wit = jnp.swapaxes(w["a_w_i"], 1, 2)
    (dgate, dxbr, g["a_conv_w"], g["a_conv_b"], g["a_b_r"], g["a_b_i"], g["a_lambda"], g["a_w_r"],
     g["a_w_i"]) = _rglru_bwd(dy_a, gate_br, x_br, hs, w["a_conv_w"], w["a_conv_b"], w["a_w_r"], w["a_w_i"], wrt, wit,
                              w["a_b_r"], w["a_b_i"], w["a_lambda"], "rglru_bwd")
    g["a_w_in"] = jnp.concatenate([_matmul([(h0, dgate)], F32, "mm_a_dwin_g", trans_a=True),
                                   _matmul([(h0, dxbr)], F32, "mm_a_dwin_x", trans_a=True)], axis=1)
    dx0, _, dgm0 = _matmul([(dgate, t(w_in_g)), (dxbr, t(w_in_x))], F32, "mm_a_dh",
                           norm_bwd=(x, w["norm_mix_g"][0], dx1))
    g["norm_mix_g"] = jnp.concatenate([dgm0, dgm1], axis=0)
    g["norm_ffn_g"] = jnp.concatenate([dgf0, dgf1], axis=0)
    g["ffn_w_gate"] = [dwg0, dwg1]
    g["ffn_w_up"] = [dwu0, dwu1]
    g["ffn_w_down"] = [dwd0, dwd1]
    return loss, dx0, g


WEIGHTS = ["norm_mix_g", "norm_ffn_g", "a_w_in", "a_conv_w", "a_conv_b", "a_w_r", "a_b_r", "a_w_i", "a_b_i",
           "a_lambda", "a_w_out", "b_w_qkv", "b_w_out", "ffn_w_gate", "ffn_w_up", "ffn_w_down", "final_g"]
BIG = [("a_w_in", 2), ("a_w_r", 2), ("a_w_i", 2), ("a_w_out", 1), ("b_w_qkv", 2), ("b_w_out", 1),
       ("ffn_w_gate", 2), ("ffn_w_up", 2), ("ffn_w_down", 1)]
LAYER1 = ["b_w_qkv", "b_w_out", "ffn_w_gate", "ffn_w_up", "ffn_w_down"]
LAYER0 = ["a_w_in", "a_w_r", "a_w_i", "a_w_out", "ffn_w_gate", "ffn_w_up", "ffn_w_down"]
RS_COLLECTIVE_IDS = {"chips1": 3, "chips0": 4, "sibling1": 5, "share1": 6}
GATHER_COLLECTIVE_IDS = (8, 7)
SMALL = ["norm_mix_g", "norm_ffn_g", "a_conv_w", "a_conv_b", "a_b_r", "a_b_i", "a_lambda", "final_g"]


def _split_chips(full, axis):
    if axis == 1:
        return full.reshape((N_CHIPS, 1, full.shape[1] // N_CHIPS) + full.shape[2:])
    return jnp.stack(jnp.split(full, N_CHIPS, axis=axis))


def _step(x, target, weights, moments_m, moments_v):
    chip = 2 * lax.axis_index("x") + lax.axis_index("y")
    core = lax.axis_index("c")
    axis_of = dict(BIG)
    full = {}
    for group, layer, tag, collective_id in ((LAYER0[:4], 0, "0a", None), (LAYER0[4:], 0, "0f", GATHER_COLLECTIVE_IDS[0]),
                                             (LAYER1, 1, "1", GATHER_COLLECTIVE_IDS[1])):
        shards = [weights[n][layer % weights[n].shape[0]].astype(CD) for n in group]
        packed = _pack_shards(shards)
        if collective_id is not None:
            packed, first_gathered = lax.optimization_barrier((packed, first_gathered))
        gathered = _allgather_chips(packed, "allgather_weights" + tag, collective_id)
        if collective_id is None:
            first_gathered = gathered
        for n, own, stack in zip(group, shards, _unpack_shards(gathered, [sh.shape for sh in shards])):
            joined = jnp.concatenate([jnp.where(chip == s, own, stack[s]) for s in range(N_CHIPS)],
                                     axis=axis_of[n] - 1)
            full.setdefault(n, {})[layer] = joined
    full = {n: (v[0] if n.startswith("a_") else v[1] if n.startswith("b_") else [v[0], v[1]]) for n, v in full.items()}
    cw_rows = jnp.zeros((N_CHIPS, CONV_W, RG_BW), F32)
    cw_rows = lax.dynamic_update_slice(cw_rows, jnp.where(core == 0, weights["a_conv_w"], 0.0), (chip, 0, 0))
    cw_all = _allreduce_small(cw_rows.reshape(-1, LANES), "allgather_conv_w").reshape(N_CHIPS, CONV_W, RG_BW)
    full["a_conv_w"] = jnp.concatenate([cw_all[s] for s in range(N_CHIPS)], axis=1)
    for n in ("norm_mix_g", "norm_ffn_g", "final_g"):
        full[n] = weights[n]
    for n in ("a_conv_b", "a_b_r", "a_b_i", "a_lambda"):
        full[n] = weights[n]
    loss, dx, grads = _local_step(x[0], target[0], full)
    small_parts = [grads[n].reshape(-1) for n in SMALL] + [loss.reshape(-1)]
    sizes = [p.shape[0] for p in small_parts]
    small = _allreduce_small(jnp.concatenate(small_parts).reshape(-1, LANES), "allreduce_small").reshape(-1)
    red, pos = {}, 0
    for n, sz in zip(SMALL + ["loss"], sizes):
        red[n] = small[pos:pos + sz]
        pos += sz
    loss_out = red["loss"][0]
    g_out = {}
    for n in SMALL:
        if n == "a_conv_w":
            g_out[n] = lax.dynamic_slice(red[n].reshape(CONV_W, D_RNN), (0, chip * RG_BW), (CONV_W, RG_BW)).reshape(
                weights[n].shape)
        else:
            g_out[n] = red[n].reshape(weights[n].shape)
    axis_of = dict(BIG)
    pieces = {}
    for group, layer, tag in ((LAYER1, 1, "1"), (LAYER0, 0, "0")):
        stacks, shapes = [], []
        for n in group:
            per_layer = isinstance(grads[n], list)
            gfull = grads[n][layer] if per_layer else grads[n]
            shard_shape = weights[n].shape[1:]
            gfull = gfull.reshape((1,) + gfull.shape)
            stacks.append(_split_chips(gfull, axis_of[n]).reshape(N_CHIPS, -1, PACK_COLS))
            shapes.append((1,) + tuple(shard_shape))
        gbuf = jnp.concatenate(stacks, axis=1)
        if layer == 1:
            from_sibling = _exchange_sibling_halves_async(gbuf, "rs_sibling" + tag, RS_COLLECTIVE_IDS["sibling1"])
        else:
            from_sibling = _exchange_sibling_halves(gbuf, "rs_sibling" + tag)
        chip_partial = _add_sibling(gbuf, from_sibling, core, "rs_add" + tag)
        from_chips = _scatter_to_chips_async(chip_partial, "rs_chips" + tag, RS_COLLECTIVE_IDS["chips" + tag])
        halves = _sum_slots(from_chips, chip_partial, chip, core, "rs_sum" + tag)
        if layer == 1:
            reduced = _share_halves_async(halves, "rs_share" + tag, RS_COLLECTIVE_IDS["share1"])
        else:
            reduced = _share_halves(halves, "rs_share" + tag)
        for n, piece in zip(group, _unpack_shards(reduced, shapes)):
            pieces.setdefault(n, {})[layer] = piece
    for n, _ in BIG:
        layers = pieces[n]
        g_out[n] = jnp.concatenate([layers[k] for k in sorted(layers)], axis=0)
    updates = {}
    for n, _ in BIG:
        updates[n] = _adamw(weights[n], g_out[n], moments_m[n], moments_v[n], "adamw_" + n)
    rows = lambda d: jnp.concatenate([d[n].reshape(-1, D_MODEL) for n in SMALL], axis=0)
    small_updates = _adamw(rows(weights), rows(g_out), rows(moments_m), rows(moments_v), "adamw_small")
    pos = 0
    for n in SMALL:
        nrows = weights[n].size // D_MODEL
        updates[n] = tuple(u[pos:pos + nrows].reshape(weights[n].shape) for u in small_updates)
        pos += nrows
    outs_g = [g_out[n] for n in WEIGHTS]
    outs_d, outs_m, outs_v = ([updates[n][k] for n in WEIGHTS] for k in range(3))
    return (loss_out, dx[None], *outs_g, *outs_d, *outs_m, *outs_v)


def kernel(x, norm_mix_g, norm_ffn_g, a_w_in, a_conv_w, a_conv_b, a_w_r, a_b_r, a_w_i, a_b_i, a_lambda, a_w_out, b_w_qkv, b_w_out, ffn_w_gate, ffn_w_up, ffn_w_down, final_g, loss_target, m_norm_mix_g, m_norm_ffn_g, m_a_w_in, m_a_conv_w, m_a_conv_b, m_a_w_r, m_a_b_r, m_a_w_i, m_a_b_i, m_a_lambda, m_a_w_out, m_b_w_qkv, m_b_w_out, m_ffn_w_gate, m_ffn_w_up, m_ffn_w_down, m_final_g, v_norm_mix_g, v_norm_ffn_g, v_a_w_in, v_a_conv_w, v_a_conv_b, v_a_w_r, v_a_b_r, v_a_w_i, v_a_b_i, v_a_lambda, v_a_w_out, v_b_w_qkv, v_b_w_out, v_ffn_w_gate, v_ffn_w_up, v_ffn_w_down, v_final_g):
    ws = [norm_mix_g, norm_ffn_g, a_w_in, a_conv_w, a_conv_b, a_w_r, a_b_r, a_w_i, a_b_i, a_lambda, a_w_out, b_w_qkv,
          b_w_out, ffn_w_gate, ffn_w_up, ffn_w_down, final_g]
    ms = [m_norm_mix_g, m_norm_ffn_g, m_a_w_in, m_a_conv_w, m_a_conv_b, m_a_w_r, m_a_b_r, m_a_w_i, m_a_b_i, m_a_lambda,
          m_a_w_out, m_b_w_qkv, m_b_w_out, m_ffn_w_gate, m_ffn_w_up, m_ffn_w_down, m_final_g]
    vs = [v_norm_mix_g, v_norm_ffn_g, v_a_w_in, v_a_conv_w, v_a_conv_b, v_a_w_r, v_a_b_r, v_a_w_i, v_a_b_i, v_a_lambda,
          v_a_w_out, v_b_w_qkv, v_b_w_out, v_ffn_w_gate, v_ffn_w_up, v_ffn_w_down, v_final_g]
    return _step(x, loss_target, dict(zip(WEIGHTS, ws)), dict(zip(WEIGHTS, ms)), dict(zip(WEIGHTS, vs)))
```

```python
import math

import jax
import jax.numpy as jnp
from jax import lax
from jax.experimental import pallas as pl
from jax.experimental.pallas import tpu as pltpu
from jax.experimental.pallas import tpu_sc as plsc

F32 = jnp.float32
CD = jnp.bfloat16

D_MODEL = 1024
D_RNN = 1024
RG_BLOCKS = 4
RG_BW = 256
CONV_W = 4
RG_C = 8.0
SB_HEADS = 16
SB_HEAD_DIM = 64
D_FF = 2816
RMS_EPS = 1e-6
N_CHIPS = 4
N_DEV = 8

ADAM_LR = 0.001
ADAM_B1 = 0.9
ADAM_B2 = 0.999
ADAM_EPS = 1e-08
ADAM_WD = 0.01
ADAM_STEP = 10

LANES = 128
VMEM_LIMIT = 56 * 1024 * 1024
MESH = pl.DeviceIdType.MESH


def _params(*sem):
    return pltpu.CompilerParams(dimension_semantics=sem, vmem_limit_bytes=VMEM_LIMIT)


def _pick(n, prefs):
    for p in prefs:
        if n % p == 0:
            return p
    return n


def _row_tile(rows):
    return max(d for d in range(16, 1025, 16) if rows % d == 0)


def _matmul(pairs, out_dtype, name, *, trans_a=False, a_lbm=False, b_lbm=False, out_lbm=False, addend=None,
            tm=512, tn=None, tk=None, norm_gain=None, norm_bwd=None, loss_head=None):
    a0, b0 = pairs[0]
    if trans_a:
        kdim = a0.shape[1] if a_lbm else a0.shape[0]
        m = a0.shape[0] * LANES if a_lbm else a0.shape[1]
    else:
        m = a0.shape[1] if a_lbm else a0.shape[0]
        kdim = a0.shape[0] * LANES if a_lbm else a0.shape[1]
    n = b0.shape[0] * LANES if b_lbm else b0.shape[1]
    if trans_a and m <= 1024:
        tm = m
    tm = _pick(m, (tm, 1408, 256, 128))
    tn = tn or _pick(n, (1408, 1024, 768, 512, 256, 128))
    tk = tk or _pick(kdim, (1024, 1408, 512, 256, 128))
    nk = kdim // tk
    npair = len(pairs)

    def cat(ref):
        return jnp.concatenate([ref[p] for p in range(ref.shape[0])], axis=-1)

    def body(*refs):
        ins = refs[: 2 * npair]
        pos = 2 * npair
        add_ref = None
        if addend is not None:
            add_ref = refs[pos]
            pos += 1
        gain_ref = x_ref = dxin_ref = None
        if norm_gain is not None:
            gain_ref = refs[pos]
            pos += 1
        if norm_bwd is not None:
            x_ref, gain_ref, dxin_ref = refs[pos:pos + 3]
            pos += 3
        if loss_head is not None:
            gain_ref, target_ref = refs[pos:pos + 2]
            pos += 2
        o_ref = refs[pos]
        extra_out = refs[pos + 1:-1]
        acc_ref = refs[-1]
        k = pl.program_id(2)

        @pl.when(k == 0)
        def _():
            acc_ref[...] = jnp.zeros_like(acc_ref)

        if norm_bwd is not None or loss_head is not None:
            @pl.when((k == 0) & (pl.program_id(0) == 0))
            def _():
                for ref in extra_out[1:]:
                    ref[...] = jnp.zeros_like(ref)

        acc = acc_ref[...]
        for p in range(npair):
            a = (cat(ins[2 * p]) if a_lbm else ins[2 * p][...]).astype(CD)
            b = (cat(ins[2 * p + 1]) if b_lbm else ins[2 * p + 1][...]).astype(CD)
            dims = (((0,), (0,)), ((), ())) if trans_a else (((1,), (0,)), ((), ()))
            acc = acc + lax.dot_general(a, b, dims, preferred_element_type=F32)
        acc_ref[...] = acc

        @pl.when(k == nk - 1)
        def _():
            res = acc_ref[...]
            if add_ref is not None:
                res = res + add_ref[...]
            if norm_gain is not None:
                rinv = lax.rsqrt(jnp.mean(res * res, axis=-1, keepdims=True) + RMS_EPS)
                extra_out[0][...] = (res * rinv * gain_ref[...]).astype(CD)
            if norm_bwd is not None:
                xv = x_ref[...]
                rinv = lax.rsqrt(jnp.mean(xv * xv, axis=-1, keepdims=True) + RMS_EPS)
                nrm = xv * rinv
                dn = res * gain_ref[...]
                extra_out[1][...] += jnp.sum(res * nrm, axis=0, keepdims=True)
                res = dxin_ref[...] + rinv * (dn - nrm * jnp.mean(dn * nrm, axis=-1, keepdims=True))
                extra_out[0][...] = res.astype(CD)
            if loss_head is not None:
                gv = gain_ref[...]
                rinv = lax.rsqrt(jnp.mean(res * res, axis=-1, keepdims=True) + RMS_EPS)
                nrm = res * rinv
                err = nrm * gv - target_ref[...]
                extra_out[2][...] += 0.5 * jnp.sum(jnp.mean(err * err, axis=-1, keepdims=True), axis=0, keepdims=True)
                dy = err * (1.0 / n)
                dn = dy * gv
                extra_out[1][...] += jnp.sum(dy * nrm, axis=0, keepdims=True)
                res = rinv * (dn - nrm * jnp.mean(dn * nrm, axis=-1, keepdims=True))
                extra_out[0][...] = res.astype(CD)
            res = res.astype(out_dtype)
            if out_lbm:
                for p in range(tn // LANES):
                    o_ref[p] = res[:, p * LANES:(p + 1) * LANES]
            else:
                o_ref[...] = res

    if trans_a:
        a_spec = (pl.BlockSpec((tm // LANES, tk, LANES), lambda i, j, k: (i, k, 0)) if a_lbm
                  else pl.BlockSpec((tk, tm), lambda i, j, k: (k, i)))
    else:
        a_spec = (pl.BlockSpec((tk // LANES, tm, LANES), lambda i, j, k: (k, i, 0)) if a_lbm
                  else pl.BlockSpec((tm, tk), lambda i, j, k: (i, k)))
    b_spec = (pl.BlockSpec((tn // LANES, tk, LANES), lambda i, j, k: (j, k, 0)) if b_lbm
              else pl.BlockSpec((tk, tn), lambda i, j, k: (k, j)))
    in_specs = []
    args = []
    for a, b in pairs:
        in_specs += [a_spec, b_spec]
        args += [a, b]
    if addend is not None:
        in_specs.append(pl.BlockSpec((tm, tn), lambda i, j, k: (i, j)))
        args.append(addend)
    tile = pl.BlockSpec((tm, tn), lambda i, j, k: (i, j))
    vec = pl.BlockSpec((1, tn), lambda i, j, k: (0, j))
    if out_lbm:
        out_shape = jax.ShapeDtypeStruct((n // LANES, m, LANES), out_dtype)
        out_spec = pl.BlockSpec((tn // LANES, tm, LANES), lambda i, j, k: (j, i, 0))
    else:
        out_shape = jax.ShapeDtypeStruct((m, n), out_dtype)
        out_spec = tile
    sem = ("parallel", "parallel", "arbitrary")
    if norm_gain is not None or norm_bwd is not None or loss_head is not None:
        assert tn == n and not out_lbm, "the norm needs whole rows in one tile"
        out_shape, out_spec = [out_shape, jax.ShapeDtypeStruct((m, n), CD)], [out_spec, tile]
    if norm_gain is not None:
        in_specs.append(vec)
        args.append(norm_gain.reshape(1, n))
    if norm_bwd is not None:
        x_in, gain, dx_in = norm_bwd
        in_specs += [tile, vec, tile]
        args += [x_in, gain.reshape(1, n), dx_in]
        out_shape.append(jax.ShapeDtypeStruct((1, n), F32))
        out_spec.append(vec)
        sem = ("arbitrary", "arbitrary", "arbitrary")
    if loss_head is not None:
        gain, target = loss_head
        in_specs += [vec, tile]
        args += [gain.reshape(1, n), target]
        out_shape += [jax.ShapeDtypeStruct((1, n), F32), jax.ShapeDtypeStruct((1, LANES), F32)]
        out_spec += [vec, pl.BlockSpec((1, LANES), lambda i, j, k: (0, 0))]
        sem = ("arbitrary", "arbitrary", "arbitrary")
    return pl.pallas_call(
        body, name=name, out_shape=out_shape, grid=(m // tm, n // tn, nk),
        in_specs=in_specs, out_specs=out_spec,
        scratch_shapes=[pltpu.VMEM((tm, tn), F32)],
        compiler_params=_params(*sem),
    )(*args)


def _norm_and_project(x, g, w_a, w_b, name):
    s, d = x.shape
    n = w_a.shape[1]
    tm = _pick(s, (512, 256))

    def body(x_ref, g_ref, wa_ref, wb_ref, h_ref, a_ref, b_ref):
        xv = x_ref[...]
        rinv = lax.rsqrt(jnp.mean(xv * xv, axis=-1, keepdims=True) + RMS_EPS)
        h = (xv * rinv * g_ref[...]).astype(CD)
        h_ref[...] = h
        a_ref[...] = jnp.dot(h, wa_ref[...], preferred_element_type=F32)
        b_ref[...] = jnp.dot(h, wb_ref[...], preferred_element_type=F32)

    row = pl.BlockSpec((tm, d), lambda i: (i, 0))
    out = pl.BlockSpec((tm, n), lambda i: (i, 0))
    wspec = pl.BlockSpec((d, n), lambda i: (0, 0))
    return pl.pallas_call(
        body, name=name,
        out_shape=(jax.ShapeDtypeStruct((s, d), CD), jax.ShapeDtypeStruct((s, n), F32),
                   jax.ShapeDtypeStruct((s, n), F32)),
        grid=(s // tm,), in_specs=[row, pl.BlockSpec((1, d), lambda i: (0, 0)), wspec, wspec],
        out_specs=(row, out, out), compiler_params=_params("parallel"),
    )(x, g.reshape(1, d), w_a, w_b)


def _sigmoid(z):
    return 1.0 / (1.0 + jnp.exp(-z))


FFN_TM = 512
FFN_TN = 1408


def _ffn_up(h, wg, wu, name):
    s, d = h.shape
    f = wg.shape[1]
    tm = _pick(s, (FFN_TM, 256))

    def body(h_ref, wg_ref, wu_ref, g_ref, u_ref, a_ref):
        hv = h_ref[...]
        gv = jnp.dot(hv, wg_ref[...], preferred_element_type=F32)
        uv = jnp.dot(hv, wu_ref[...], preferred_element_type=F32)
        g_ref[...] = gv
        u_ref[...] = uv
        a_ref[...] = (gv * _sigmoid(gv) * uv).astype(CD)

    a_spec = pl.BlockSpec((tm, d), lambda j, i: (i, 0))
    w_spec = pl.BlockSpec((d, FFN_TN), lambda j, i: (0, j))
    o_spec = pl.BlockSpec((tm, FFN_TN), lambda j, i: (i, j))
    return pl.pallas_call(
        body, name=name,
        out_shape=(jax.ShapeDtypeStruct((s, f), F32), jax.ShapeDtypeStruct((s, f), F32),
                   jax.ShapeDtypeStruct((s, f), CD)),
        grid=(f // FFN_TN, s // tm), in_specs=[a_spec, w_spec, w_spec], out_specs=(o_spec, o_spec, o_spec),
        compiler_params=_params("parallel", "parallel"),
    )(h, wg, wu)


def _ffn_dact(dxc, wd_t, g, u, name):
    s, d = dxc.shape
    f = wd_t.shape[1]
    tm = _pick(s, (FFN_TM, 256))

    def body(dx_ref, w_ref, g_ref, u_ref, dg_ref, du_ref):
        da = jnp.dot(dx_ref[...], w_ref[...], preferred_element_type=F32)
        gv = g_ref[...]
        sg = _sigmoid(gv)
        silu = gv * sg
        dg_ref[...] = (da * u_ref[...] * (sg + silu * (1.0 - sg))).astype(CD)
        du_ref[...] = (da * silu).astype(CD)

    a_spec = pl.BlockSpec((tm, d), lambda j, i: (i, 0))
    w_spec = pl.BlockSpec((d, FFN_TN), lambda j, i: (0, j))
    o_spec = pl.BlockSpec((tm, FFN_TN), lambda j, i: (i, j))
    return pl.pallas_call(
        body, name=name,
        out_shape=(jax.ShapeDtypeStruct((s, f), CD), jax.ShapeDtypeStruct((s, f), CD)),
        grid=(f // FFN_TN, s // tm), in_specs=[a_spec, w_spec, o_spec, o_spec], out_specs=(o_spec, o_spec),
        compiler_params=_params("parallel", "parallel"),
    )(dxc, wd_t, g, u)


TIME_BLOCK = 1024
SUBLANES = 8
GELU_C = math.sqrt(2.0 / math.pi)
GELU_A = 0.044715


def _gelu(x):
    return 0.5 * x * (1.0 + jnp.tanh(GELU_C * (x + GELU_A * x * x * x)))


def _gelu_grad(x):
    t = jnp.tanh(GELU_C * (x + GELU_A * x * x * x))
    return 0.5 * (1.0 + t) + 0.5 * x * (1.0 - t * t) * GELU_C * (1.0 + 3.0 * GELU_A * x * x)


def _neg_expm1(x):
    series = -x * (1.0 + x * (0.5 + x * (1.0 / 6.0 + x * (1.0 / 24.0))))
    return jnp.where(x > -0.05, series, 1.0 - jnp.exp(x))


def _log_sigmoid(x):
    return jnp.minimum(x, 0.0) - jnp.log1p(jnp.exp(-jnp.abs(x)))


def _shift_down(x, tail, s):
    if s == 0:
        return x
    ext = jnp.concatenate([tail, x], axis=0)
    return pltpu.roll(ext, s, axis=0)[SUBLANES:]


def _shift_up(x, head, s):
    if s == 0:
        return x
    n = x.shape[0]
    ext = jnp.concatenate([x, head], axis=0)
    return pltpu.roll(ext, n + SUBLANES - s, axis=0)[:n]


def _rg_gates(xbr, tail, cw_ref, cb, wr, wi, br, bi, ls):
    taps = [_shift_down(xbr, tail, CONV_W - 1 - k) for k in range(CONV_W)]
    xc = cb
    for k in range(CONV_W):
        xc = xc + cw_ref[pl.ds(k, 1), :] * taps[k]
    xcd = xc.astype(CD)
    r = _sigmoid(jnp.dot(xcd, wr, preferred_element_type=F32) + br)
    i = _sigmoid(jnp.dot(xcd, wi, preferred_element_type=F32) + bi)
    log_a = RG_C * r * ls
    a = jnp.exp(log_a)
    mult = jnp.sqrt(jnp.maximum(_neg_expm1(2.0 * log_a), 0.0))
    return taps, xc, r, i, log_a, a, mult


def _scan8_fwd(a, u):
    row = lax.broadcasted_iota(jnp.int32, a.shape, 0)
    for d in (1, 2, 4):
        a_s = pltpu.roll(a, d, axis=0)
        u_s = pltpu.roll(u, d, axis=0)
        m = row >= d
        u = jnp.where(m, a * u_s + u, u)
        a = jnp.where(m, a * a_s, a)
    return a, u


def _scan8_bwd(b, u):
    row = lax.broadcasted_iota(jnp.int32, b.shape, 0)
    for d in (1, 2, 4):
        b_s = pltpu.roll(b, SUBLANES - d, axis=0)
        u_s = pltpu.roll(u, SUBLANES - d, axis=0)
        m = row < SUBLANES - d
        u = jnp.where(m, b * u_s + u, u)
        b = jnp.where(m, b * b_s, b)
    return b, u


def _rglru_fwd(gate_br, x_br, cw, cb, wr, wi, br, bi, lam, name):
    s, c = x_br.shape
    nt = s // TIME_BLOCK
    tb, cbw = TIME_BLOCK, RG_BW
    groups = tb // SUBLANES

    def body(g_ref, x_ref, tail_ref, cw_ref, cb_ref, wr_ref, wi_ref, br_ref, bi_ref, lam_ref,
             y_ref, hs_ref, carry_ref, a_scr, u_scr):
        t = pl.program_id(1)

        @pl.when(t == 0)
        def _():
            carry_ref[...] = jnp.zeros_like(carry_ref)

        tail = jnp.where(t > 0, tail_ref[...], 0.0)
        ls = _log_sigmoid(lam_ref[...])
        _, xc, _, i, _, a, mult = _rg_gates(x_ref[...], tail, cw_ref, cb_ref[...], wr_ref[0], wi_ref[0],
                                            br_ref[...], bi_ref[...], ls)
        a_scr[...] = a
        u_scr[...] = mult * (i * xc)
        carry = carry_ref[...]
        for gi in range(groups):
            rows = pl.ds(gi * SUBLANES, SUBLANES)
            pa, hl = _scan8_fwd(a_scr[rows, :], u_scr[rows, :])
            hs_ref[rows, :] = hl + pa * carry
            carry = hs_ref[pl.ds(gi * SUBLANES + SUBLANES - 1, 1), :]
        carry_ref[...] = carry
        y_ref[...] = (hs_ref[...] * _gelu(g_ref[...])).astype(CD)

    blk = pl.BlockSpec((tb, cbw), lambda n, t: (t, n))
    tail = pl.BlockSpec((SUBLANES, cbw), lambda n, t: (jnp.maximum(t * groups - 1, 0), n))
    vec = pl.BlockSpec((1, cbw), lambda n, t: (0, n))
    wblk = pl.BlockSpec((1, cbw, cbw), lambda n, t: (n, 0, 0))
    return pl.pallas_call(
        body, name=name,
        out_shape=(jax.ShapeDtypeStruct((s, c), CD), jax.ShapeDtypeStruct((s, c), F32)),
        grid=(RG_BLOCKS, nt),
        in_specs=[blk, blk, tail, pl.BlockSpec((CONV_W, cbw), lambda n, t: (0, n)), vec, wblk, wblk, vec, vec, vec],
        out_specs=(blk, blk),
        scratch_shapes=[pltpu.VMEM((1, cbw), F32), pltpu.VMEM((tb, cbw), F32), pltpu.VMEM((tb, cbw), F32)],
        compiler_params=_params("parallel", "arbitrary"),
    )(gate_br, x_br, x_br, cw, cb, wr, wi, br, bi, lam)


def _rglru_bwd(dy, gate_br, x_br, hs, cw, cb, wr, wi, wrt, wit, br, bi, lam, name):
    s, c = x_br.shape
    nt = s // TIME_BLOCK
    tb, cbw = TIME_BLOCK, RG_BW
    groups = tb // SUBLANES

    def body(dy_ref, g_ref, x_ref, tail_ref, hs_ref, hprev_ref, cw_ref, cb_ref, wr_ref, wi_ref, wrt_ref, wit_ref,
             br_ref, bi_ref, lam_ref,
             dg_ref, dx_ref, dcw_ref, dcb_ref, dbr_ref, dbi_ref, dlam_ref, dwr_ref, dwi_ref,
             carry_ref, head_ref, b_scr, u_scr, dh_scr):
        tr = pl.program_id(1)
        first_block = tr == nt - 1

        @pl.when(tr == 0)
        def _():
            carry_ref[...] = jnp.zeros_like(carry_ref)
            head_ref[...] = jnp.zeros_like(head_ref)
            for ref in (dcw_ref, dcb_ref, dbr_ref, dbi_ref, dlam_ref, dwr_ref, dwi_ref):
                ref[...] = jnp.zeros_like(ref)

        tail = jnp.where(first_block, 0.0, tail_ref[...])
        lam_v = lam_ref[...]
        ls = _log_sigmoid(lam_v)
        taps, xc, r, i, log_a, a, mult = _rg_gates(x_ref[...], tail, cw_ref, cb_ref[...], wr_ref[0], wi_ref[0],
                                                   br_ref[...], bi_ref[...], ls)
        gate_v = g_ref[...]
        dyv = dy_ref[...]
        hsv = hs_ref[...]
        dg_ref[...] = (dyv * hsv * _gelu_grad(gate_v)).astype(CD)

        row = lax.broadcasted_iota(jnp.int32, a.shape, 0)
        b_scr[...] = jnp.where(row == tb - 1, 1.0, pltpu.roll(a, tb - 1, axis=0))
        u_scr[...] = dyv * _gelu(gate_v)
        carry = carry_ref[...]
        for gi in reversed(range(groups)):
            rows = pl.ds(gi * SUBLANES, SUBLANES)
            pb, gl = _scan8_bwd(b_scr[rows, :], u_scr[rows, :])
            dh_scr[rows, :] = gl + pb * carry
            carry = dh_scr[pl.ds(gi * SUBLANES, 1), :]
        dh = dh_scr[...]
        carry_ref[...] = carry * jnp.sum(jnp.where(row == 0, a, 0.0), axis=0, keepdims=True)

        hprev_tail = jnp.where(first_block, 0.0, hprev_ref[...])
        h_prev = _shift_down(hsv, hprev_tail, 1)
        da = dh * h_prev
        ixc = i * xc
        dmult = dh * ixc
        di = dh * mult * xc
        dxc = dh * mult * i
        a2 = a * a
        dlog_a = da * a - dmult * a2 / mult
        dpre_r = (dlog_a * (RG_C * ls)) * r * (1.0 - r)
        dpre_i = di * i * (1.0 - i)
        dlam_ref[...] += jnp.sum(dlog_a * r, axis=0, keepdims=True) * (RG_C * _sigmoid(-lam_v))
        dbr_ref[...] += jnp.sum(dpre_r, axis=0, keepdims=True)
        dbi_ref[...] += jnp.sum(dpre_i, axis=0, keepdims=True)
        xcd = xc.astype(CD)
        dprc = dpre_r.astype(CD)
        dpic = dpre_i.astype(CD)
        tn_dims = (((0,), (0,)), ((), ()))
        dwr_ref[0] += lax.dot_general(xcd, dprc, tn_dims, preferred_element_type=F32)
        dwi_ref[0] += lax.dot_general(xcd, dpic, tn_dims, preferred_element_type=F32)
        dxc = dxc + jnp.dot(dprc, wrt_ref[0], preferred_element_type=F32) + jnp.dot(dpic, wit_ref[0],
                                                                                    preferred_element_type=F32)
        dcb_ref[...] += jnp.sum(dxc, axis=0, keepdims=True)
        for k in range(CONV_W):
            dcw_ref[pl.ds(k, 1), :] += jnp.sum(dxc * taps[k], axis=0, keepdims=True)
        head = head_ref[...]
        dxb = jnp.zeros_like(dxc)
        for sft in range(CONV_W):
            dxb = dxb + cw_ref[pl.ds(CONV_W - 1 - sft, 1), :] * _shift_up(dxc, head, sft)
        dx_ref[...] = dxb.astype(CD)
        head_ref[...] = dxc[0:SUBLANES, :]

    blk = pl.BlockSpec((tb, cbw), lambda n, t: (nt - 1 - t, n))
    tail = pl.BlockSpec((SUBLANES, cbw), lambda n, t: (jnp.maximum((nt - 1 - t) * groups - 1, 0), n))
    vec = pl.BlockSpec((1, cbw), lambda n, t: (0, n))
    cwb = pl.BlockSpec((CONV_W, cbw), lambda n, t: (0, n))
    wblk = pl.BlockSpec((1, cbw, cbw), lambda n, t: (n, 0, 0))
    vshape = jax.ShapeDtypeStruct((1, c), F32)
    wshape = jax.ShapeDtypeStruct((RG_BLOCKS, cbw, cbw), F32)
    return pl.pallas_call(
        body, name=name,
        out_shape=(jax.ShapeDtypeStruct((s, c), CD), jax.ShapeDtypeStruct((s, c), CD),
                   jax.ShapeDtypeStruct((CONV_W, c), F32), vshape, vshape, vshape, vshape, wshape, wshape),
        grid=(RG_BLOCKS, nt),
        in_specs=[blk, blk, blk, tail, blk, tail, cwb, vec, wblk, wblk, wblk, wblk, vec, vec, vec],
        out_specs=(blk, blk, cwb, vec, vec, vec, vec, wblk, wblk),
        scratch_shapes=[pltpu.VMEM((1, cbw), F32), pltpu.VMEM((SUBLANES, cbw), F32),
                        pltpu.VMEM((tb, cbw), F32), pltpu.VMEM((tb, cbw), F32), pltpu.VMEM((tb, cbw), F32)],
        compiler_params=_params("parallel", "arbitrary"),
    )(dy, gate_br, x_br, x_br, hs, hs, cw, cb, wr, wi, wrt, wit, br, bi, lam)


ATT_BLOCK = 256
ATT_Q_BLOCK = 1024
ATT_RATIO = ATT_Q_BLOCK // ATT_BLOCK
ATT_SCALE = 1.0 / math.sqrt(SB_HEAD_DIM)
N_PAIRS = SB_HEADS * SB_HEAD_DIM // LANES
NT_DIMS = (((1,), (1,)), ((), ()))
TN_DIMS = (((0,), (0,)), ((), ()))


LOG2E = 1.4426950408889634


def _neg_abs(x):
    bits = lax.bitcast_convert_type(x, jnp.uint32) | jnp.uint32(0x80000000)
    return lax.bitcast_convert_type(bits, F32)


def _qk(qx, kb):
    return lax.dot_general(qx, kb, NT_DIMS, preferred_element_type=F32)


def _sb_logits(qk, valid):
    z2 = qk * (ATT_SCALE * LOG2E)
    lb2 = jnp.minimum(z2, 0.0) - jnp.log2(1.0 + jnp.exp2(_neg_abs(z2)))
    l2 = lb2 - z2
    if valid is not None:
        l2 = jnp.where(valid, l2, 0.0)
    return lb2, l2


def _hi_lo(x):
    hi = x.astype(CD)
    lo = (x - hi.astype(F32)).astype(CD)
    return jnp.concatenate([hi, lo], axis=1)


def _tri(strict, stacked):
    r = lax.broadcasted_iota(jnp.int32, (ATT_BLOCK, ATT_BLOCK), 0)
    c = lax.broadcasted_iota(jnp.int32, (ATT_BLOCK, ATT_BLOCK), 1)
    m = (r > c if strict else r >= c).astype(CD)
    return jnp.concatenate([m, m], axis=0) if stacked else m


def _attn_fwd(qkv, name):
    _, s, _ = qkv.shape
    tq, t = ATT_Q_BLOCK, ATT_BLOCK
    nblk = s // tq

    def body(q_ref, k_ref, v_ref, o_ref, qk_scr, w_scr):
        i = pl.program_id(1)
        lane = lax.broadcasted_iota(jnp.int32, (1, LANES), 1)
        head_masks = (lane < SB_HEAD_DIM, lane >= SB_HEAD_DIM)
        q = q_ref[0]
        qs = [jnp.where(m, q, jnp.zeros_like(q)) for m in head_masks]
        tri = _tri(True, False)
        rr = lax.broadcasted_iota(jnp.int32, (tq, t), 0)
        cc = lax.broadcasted_iota(jnp.int32, (tq, t), 1)

        def rows_of(j):
            return pl.ds(pl.multiple_of(j * t, t), t)

        def tail(x, row0):
            return x if row0 == 0 else x[row0:]

        def start_logits(j, row0=0):
            kb = k_ref[0, rows_of(j), :]
            for hd in range(2):
                qk_scr[hd, row0:, :] = _qk(tail(qs[hd], row0), kb)

        def weights(run, diagonal=False, row0=0):
            new_run = []
            valid = (cc < rr)[:tq - row0] if diagonal else None
            for hd in range(2):
                lb2, l2 = _sb_logits(qk_scr[hd, row0:, :], valid)
                w = jnp.exp2(lb2 + (tail(run[hd], row0) + jnp.dot(l2.astype(CD), tri, preferred_element_type=F32)))
                if valid is not None:
                    w = jnp.where(valid, w, 0.0)
                w_scr[row0:, hd * t:(hd + 1) * t] = w.astype(CD)
                rowsum = jnp.sum(l2, axis=1, keepdims=True)
                if row0:
                    rowsum = jnp.concatenate([jnp.zeros((row0, 1), F32), rowsum], axis=0)
                new_run.append(run[hd] + rowsum)
            return tuple(new_run)

        def apply_weights(j, row0=0):
            vb = v_ref[0, rows_of(j), :]
            vcat = jnp.concatenate([jnp.where(m, vb, jnp.zeros_like(vb)) for m in head_masks], axis=0)
            inc = jnp.dot(w_scr[row0:, :], vcat, preferred_element_type=F32)
            return inc if row0 == 0 else jnp.concatenate([jnp.zeros((row0, LANES), F32), inc], axis=0)

        zero = jnp.zeros((tq, 1), F32)
        last = ATT_RATIO - 1
        start_logits(ATT_RATIO * i + last, last * t)
        run = weights((zero, zero), True, last * t)
        oacc = jnp.zeros((tq, LANES), F32)
        for d in reversed(range(last)):
            start_logits(ATT_RATIO * i + d, d * t)
            oacc = oacc + apply_weights(ATT_RATIO * i + d + 1, (d + 1) * t)
            run = weights(run, True, d * t)
        start_logits(jnp.maximum(ATT_RATIO * i - 1, 0))

        def step(jj, carry):
            run, oacc = carry
            b = ATT_RATIO * i - 1 - jj
            oacc = oacc + apply_weights(b + 1)
            run = weights(run)
            start_logits(jnp.maximum(b - 1, 0))
            return run, oacc

        run, oacc = lax.fori_loop(0, ATT_RATIO * i, step, (run, oacc))
        o_ref[0] = oacc + apply_weights(0)

    return pl.pallas_call(
        body, name=name, out_shape=jax.ShapeDtypeStruct((N_PAIRS, s, LANES), F32), grid=(N_PAIRS, nblk),
        in_specs=[pl.BlockSpec((1, tq, LANES), lambda p, i: (p, i, 0)),
                  pl.BlockSpec((1, s, LANES), lambda p, i: (N_PAIRS + p, 0, 0)),
                  pl.BlockSpec((1, s, LANES), lambda p, i: (2 * N_PAIRS + p, 0, 0))],
        out_specs=pl.BlockSpec((1, tq, LANES), lambda p, i: (p, i, 0)),
        scratch_shapes=[pltpu.VMEM((2, tq, t), F32), pltpu.VMEM((tq, 2 * t), CD)],
        compiler_params=_params("parallel", "arbitrary"),
    )(qkv, qkv, qkv)


def _attn_bwd(qkv, o, do, name):
    _, s, _ = qkv.shape
    tq, t = ATT_Q_BLOCK, ATT_BLOCK
    nblk = s // tq

    def body(q_ref, k_ref, v_ref, o_ref, do_ref, dq_ref, dk_ref, dv_ref, qk_scr, dw_scr, w_scr, dz_scr):
        i = pl.program_id(1)

        @pl.when(i == 0)
        def _():
            dk_ref[...] = jnp.zeros_like(dk_ref)
            dv_ref[...] = jnp.zeros_like(dv_ref)

        lane = lax.broadcasted_iota(jnp.int32, (1, LANES), 1)
        head_masks = (lane < SB_HEAD_DIM, lane >= SB_HEAD_DIM)
        q = q_ref[0]
        dov = do_ref[0]
        ov = o_ref[0]
        qs = [jnp.where(m, q, jnp.zeros_like(q)) for m in head_masks]
        q_scaled_t = jnp.concatenate([(qx.astype(F32) * ATT_SCALE).T for qx in qs], axis=1).astype(CD)
        docs = [jnp.where(m, dov, jnp.zeros_like(dov)) for m in head_masks]
        docat_t = jnp.concatenate([d.astype(F32).T for d in docs], axis=1).astype(CD)
        totals = [jnp.sum(d.astype(F32) * ov, axis=1, keepdims=True) for d in docs]
        tri = _tri(True, False)
        tri_incl = _tri(False, True)
        rr = lax.broadcasted_iota(jnp.int32, (tq, t), 0)
        cc = lax.broadcasted_iota(jnp.int32, (tq, t), 1)

        def rows_of(j):
            return pl.ds(pl.multiple_of(j * t, t), t)

        def tail(x, row0):
            return x if row0 == 0 else x[row0:]

        def pad_rows(x, row0):
            return x if row0 == 0 else jnp.concatenate([jnp.zeros((row0, x.shape[1]), x.dtype), x], axis=0)

        def start_products(j, row0=0):
            kb = k_ref[0, rows_of(j), :]
            vb = v_ref[0, rows_of(j), :]
            for hd in range(2):
                qk_scr[hd, row0:, :] = _qk(tail(qs[hd], row0), kb)
                dw_scr[hd, row0:, :] = lax.dot_general(tail(docs[hd], row0), vb, NT_DIMS, preferred_element_type=F32)

        def logit_grads(run, erun, diagonal=False, row0=0):
            new_run, new_erun = [], []
            valid = (cc < rr)[:tq - row0] if diagonal else None
            for hd in range(2):
                lb2, l2 = _sb_logits(qk_scr[hd, row0:, :], valid)
                w = jnp.exp2(lb2 + (tail(run[hd], row0) + jnp.dot(l2.astype(CD), tri, preferred_element_type=F32)))
                if valid is not None:
                    w = jnp.where(valid, w, 0.0)
                wc = w.astype(CD)
                w_scr[hd * tq + row0:(hd + 1) * tq, :] = wc
                e = dw_scr[hd, row0:, :] * wc.astype(F32)
                prefix = (tail(totals[hd] - erun[hd], row0)
                          - jnp.dot(_hi_lo(e), tri_incl, preferred_element_type=F32))
                dz = e - jnp.exp2(lb2) * (e + prefix)
                if valid is not None:
                    dz = jnp.where(valid, dz, 0.0)
                dz_scr[hd * tq + row0:(hd + 1) * tq, :] = dz.astype(CD)
                new_run.append(run[hd] + pad_rows(jnp.sum(l2, axis=1, keepdims=True), row0))
                new_erun.append(erun[hd] + pad_rows(jnp.sum(e, axis=1, keepdims=True), row0))
            return tuple(new_run), tuple(new_erun)

        def apply_grads(j, row0=0):
            rows = rows_of(j)
            kb = k_ref[0, rows, :]
            kcat = jnp.concatenate([jnp.where(m, kb, jnp.zeros_like(kb)) for m in head_masks], axis=0)
            dz_heads = [dz_scr[hd * tq + row0:(hd + 1) * tq, :] for hd in range(2)]
            w_heads = [w_scr[hd * tq + row0:(hd + 1) * tq, :] for hd in range(2)]
            q_t = jnp.concatenate([q_scaled_t[:, hd * tq + row0:(hd + 1) * tq] for hd in range(2)], axis=1)
            do_t = jnp.concatenate([docat_t[:, hd * tq + row0:(hd + 1) * tq] for hd in range(2)], axis=1)
            dk_ref[0, :, rows] += jnp.dot(q_t, jnp.concatenate(dz_heads, axis=0), preferred_element_type=F32)
            dv_ref[0, :, rows] += jnp.dot(do_t, jnp.concatenate(w_heads, axis=0), preferred_element_type=F32)
            return pad_rows(jnp.dot(jnp.concatenate(dz_heads, axis=1), kcat, preferred_element_type=F32), row0)

        zero = jnp.zeros((tq, 1), F32)
        last = ATT_RATIO - 1
        start_products(ATT_RATIO * i + last, last * t)
        run, erun = logit_grads((zero, zero), (zero, zero), True, last * t)
        dqacc = jnp.zeros((tq, LANES), F32)
        for d in reversed(range(last)):
            start_products(ATT_RATIO * i + d, d * t)
            dqacc = dqacc + apply_grads(ATT_RATIO * i + d + 1, (d + 1) * t)
            run, erun = logit_grads(run, erun, True, d * t)
        start_products(jnp.maximum(ATT_RATIO * i - 1, 0))

        def step(jj, carry):
            run, erun, dqacc = carry
            b = ATT_RATIO * i - 1 - jj
            dqacc = dqacc + apply_grads(b + 1)
            run, erun = logit_grads(run, erun)
            start_products(jnp.maximum(b - 1, 0))
            return run, erun, dqacc

        run, erun, dqacc = lax.fori_loop(0, ATT_RATIO * i, step, (run, erun, dqacc))
        dq_ref[0] = ((dqacc + apply_grads(0)) * ATT_SCALE).astype(CD)

    qblk = pl.BlockSpec((1, tq, LANES), lambda p, i: (p, i, 0))
    full = pl.BlockSpec((1, LANES, s), lambda p, i: (p, 0, 0))
    shape = jax.ShapeDtypeStruct((N_PAIRS, s, LANES), F32)
    shape_t = jax.ShapeDtypeStruct((N_PAIRS, LANES, s), F32)
    dq, dk_t, dv_t = pl.pallas_call(
        body, name=name, out_shape=(jax.ShapeDtypeStruct(shape.shape, CD), shape_t, shape_t), grid=(N_PAIRS, nblk),
        in_specs=[qblk,
                  pl.BlockSpec((1, s, LANES), lambda p, i: (N_PAIRS + p, 0, 0)),
                  pl.BlockSpec((1, s, LANES), lambda p, i: (2 * N_PAIRS + p, 0, 0)),
                  qblk, qblk],
        out_specs=(qblk, full, full),
        scratch_shapes=[pltpu.VMEM((2, tq, t), F32), pltpu.VMEM((2, tq, t), F32),
                        pltpu.VMEM((2 * tq, t), CD), pltpu.VMEM((2 * tq, t), CD)],
        compiler_params=_params("parallel", "arbitrary"),
    )(qkv, qkv, qkv, o, do)
    return dq, jnp.swapaxes(dk_t, 1, 2).astype(CD), jnp.swapaxes(dv_t, 1, 2).astype(CD)


def _adamw(w, g, m, v, name):
    shape = w.shape
    rows, cols = (shape[-2], shape[-1]) if len(shape) >= 2 else (1, shape[-1])
    lead = w.size // (rows * cols)
    tr = _pick(rows, (512, 256, 128, 64, 32, 16, 8))

    def body(w_ref, g_ref, m_ref, v_ref, d_ref, nm_ref, nv_ref):
        gv = g_ref[...]
        nm = ADAM_B1 * m_ref[...] + (1.0 - ADAM_B1) * gv
        nv = ADAM_B2 * v_ref[...] + (1.0 - ADAM_B2) * (gv * gv)
        m_hat = nm / (1.0 - ADAM_B1 ** ADAM_STEP)
        v_hat = nv / (1.0 - ADAM_B2 ** ADAM_STEP)
        d_ref[...] = -ADAM_LR * (m_hat / (jnp.sqrt(v_hat) + ADAM_EPS) + ADAM_WD * w_ref[...])
        nm_ref[...] = nm
        nv_ref[...] = nv

    blk = pl.BlockSpec((1, tr, cols), lambda l, i: (l, i, 0))
    out = jax.ShapeDtypeStruct((lead, rows, cols), F32)
    d, nm, nv = pl.pallas_call(
        body, name=name, out_shape=(out, out, out), grid=(lead, rows // tr),
        in_specs=[blk, blk, blk, blk], out_specs=(blk, blk, blk), compiler_params=_params("parallel", "parallel"),
    )(*[a.reshape(lead, rows, cols) for a in (w, g, m, v)])
    return d.reshape(shape), nm.reshape(shape), nv.reshape(shape)


HBM = pl.BlockSpec(memory_space=pltpu.HBM)


def _coords():
    return lax.axis_index("x"), lax.axis_index("y"), lax.axis_index("c")


def _other_chips(x, y):
    return [(1 - x, y), (x, 1 - y), (1 - x, 1 - y)]


def _allgather_chips(shard, name, collective_id=None):
    r, cols = shard.shape
    half = r // 2
    quarter = half // 2

    def body(src_ref, out_ref, send_sems, recv_sems):
        x, y, c = _coords()
        sibling = (x, y, 1 - c)
        nx, ny, diag = (1 - x, y), (x, 1 - y), (1 - x, 1 - y)

        def piece(chip, core, lo, n):
            return out_ref.at[2 * chip[0] + chip[1], pl.ds(core * half + lo, n), :]

        def copy(k, dst, to, src=None):
            return pltpu.make_async_remote_copy(
                src_ref=dst if src is None else src, dst_ref=dst,
                send_sem=send_sems.at[k], recv_sem=recv_sems.at[k], device_id=to, device_id_type=MESH)

        me = (x, y)
        mine = src_ref.at[pl.ds(c * half, half), :]
        direct = [copy(0, piece(me, c, 0, half), (*nx, c), src=mine), copy(1, piece(me, c, 0, half), (*ny, c), src=mine)]
        for cp in direct:
            cp.start()
        arrivals = [piece(nx, c, 0, half), piece(ny, c, 0, half), piece(diag, c, 0, quarter),
                    piece(diag, c, quarter, quarter)]
        onward = [copy(2, piece(nx, c, 0, quarter), (*ny, c)), copy(3, piece(ny, c, quarter, quarter), (*nx, c))]
        to_sibling = [copy(4 + k, dst, sibling) for k, dst in enumerate(arrivals)]
        for k, dst in enumerate(arrivals):
            copy(k, dst, (x, y, c)).wait_recv()
            if k < 2:
                onward[k].start()
            to_sibling[k].start()
        from_sibling = [piece(nx, 1 - c, 0, half), piece(ny, 1 - c, 0, half), piece(diag, 1 - c, 0, quarter),
                        piece(diag, 1 - c, quarter, quarter)]
        for k, dst in enumerate(from_sibling):
            copy(4 + k, dst, (x, y, c)).wait_recv()
        for cp in direct + onward + to_sibling:
            cp.wait_send()

    out_shape = jax.ShapeDtypeStruct((N_CHIPS, r, cols), shard.dtype)
    sems = (pltpu.SemaphoreType.DMA((8,)), pltpu.SemaphoreType.DMA((8,)))
    if collective_id is None:
        return pl.pallas_call(body, name=name, out_shape=out_shape, in_specs=[HBM], out_specs=HBM,
                              scratch_shapes=list(sems))(shard)
    shard_ref = jax.new_ref(shard, memory_space=pltpu.MemorySpace.HBM)
    gathered_ref = jax.empty_ref(out_shape, memory_space=pltpu.MemorySpace.HBM)

    @_sequencer(name, collective_id, sems)
    def launch(send_sems, recv_sems):
        x, y, c = _coords()
        _handshake([(1 - x, y, c), (x, 1 - y, c), (x, y, 1 - c)])
        body(shard_ref, gathered_ref, send_sems, recv_sems)

    launch()
    return gathered_ref[...]


def _sequencer(name, collective_id, scratch_types):
    return pl.kernel(mesh=plsc.ScalarSubcoreMesh(axis_name="sequencer", num_cores=1), name=name,
                     scratch_types=scratch_types, compiler_params=pltpu.CompilerParams(collective_id=collective_id))


def _handshake(peers):
    barrier = pltpu.get_barrier_semaphore()
    for peer in peers:
        pl.semaphore_signal(barrier, inc=1, device_id=peer, device_id_type=MESH)
    pl.semaphore_wait(barrier, len(peers))


def _exchange_sibling_halves_async(g, name, collective_id):
    n, r, cols = g.shape
    half = r // 2
    g_ref = jax.new_ref(g, memory_space=pltpu.MemorySpace.HBM)
    out_ref = jax.empty_ref(jax.ShapeDtypeStruct((n, half, cols), g.dtype), memory_space=pltpu.MemorySpace.HBM)

    @_sequencer(name, collective_id, (pltpu.SemaphoreType.DMA, pltpu.SemaphoreType.DMA))
    def launch(send_sem, recv_sem):
        x, y, c = _coords()
        _handshake([(x, y, 1 - c)])
        cp = pltpu.make_async_remote_copy(
            src_ref=g_ref.at[:, pl.ds((1 - c) * half, half), :], dst_ref=out_ref,
            send_sem=send_sem, recv_sem=recv_sem, device_id=(x, y, 1 - c), device_id_type=MESH)
        cp.start()
        cp.wait()

    launch()
    return out_ref[...]


def _share_halves_async(v, name, collective_id):
    h = v.shape[0] // 2
    v_ref = jax.new_ref(v, memory_space=pltpu.MemorySpace.HBM)

    @_sequencer(name, collective_id, (pltpu.SemaphoreType.DMA, pltpu.SemaphoreType.DMA))
    def launch(send_sem, recv_sem):
        x, y, c = _coords()
        _handshake([(x, y, 1 - c)])
        cp = pltpu.make_async_remote_copy(
            src_ref=v_ref.at[pl.ds(c * h, h), :], dst_ref=v_ref.at[pl.ds(c * h, h), :],
            send_sem=send_sem, recv_sem=recv_sem, device_id=(x, y, 1 - c), device_id_type=MESH)
        cp.start()
        pltpu.make_async_remote_copy(
            src_ref=v_ref.at[pl.ds(c * h, h), :], dst_ref=v_ref.at[pl.ds((1 - c) * h, h), :],
            send_sem=send_sem, recv_sem=recv_sem, device_id=(x, y, 1 - c), device_id_type=MESH).wait_recv()
        cp.wait_send()

    launch()
    return v_ref[...]


def _scatter_to_chips_async(p, name, collective_id):
    p_ref = jax.new_ref(p, memory_space=pltpu.MemorySpace.HBM)
    out_ref = jax.empty_ref(jax.ShapeDtypeStruct(p.shape, p.dtype), memory_space=pltpu.MemorySpace.HBM)

    @_sequencer(name, collective_id, (pltpu.SemaphoreType.DMA((3,)), pltpu.SemaphoreType.DMA((3,))))
    def launch(send_sems, recv_sems):
        x, y, c = _coords()
        me = 2 * x + y
        _handshake([(px, py, c) for px, py in _other_chips(x, y)])
        sends = []
        for j, (px, py) in enumerate(_other_chips(x, y)):
            sends.append(pltpu.make_async_remote_copy(
                src_ref=p_ref.at[2 * px + py], dst_ref=out_ref.at[me],
                send_sem=send_sems.at[j], recv_sem=recv_sems.at[j], device_id=(px, py, c), device_id_type=MESH))
        for cp in sends:
            cp.start()
        for j, (px, py) in enumerate(_other_chips(x, y)):
            pltpu.make_async_remote_copy(
                src_ref=p_ref.at[me], dst_ref=out_ref.at[2 * px + py],
                send_sem=send_sems.at[j], recv_sem=recv_sems.at[j], device_id=(px, py, c),
                device_id_type=MESH).wait_recv()
        for cp in sends:
            cp.wait_send()

    launch()
    return out_ref[...]


def _share_halves(v, name):
    h = v.shape[0] // 2

    def body(v_ref, out_ref, send_sem, recv_sem):
        x, y, c = _coords()
        cp = pltpu.make_async_remote_copy(
            src_ref=v_ref.at[pl.ds(c * h, h), :], dst_ref=out_ref.at[pl.ds(c * h, h), :],
            send_sem=send_sem, recv_sem=recv_sem, device_id=(x, y, 1 - c), device_id_type=MESH)
        cp.start()
        pltpu.make_async_remote_copy(
            src_ref=v_ref.at[pl.ds(c * h, h), :], dst_ref=out_ref.at[pl.ds((1 - c) * h, h), :],
            send_sem=send_sem, recv_sem=recv_sem, device_id=(x, y, 1 - c), device_id_type=MESH).wait_recv()
        cp.wait_send()

    return pl.pallas_call(
        body, name=name, out_shape=jax.ShapeDtypeStruct(v.shape, v.dtype),
        in_specs=[HBM], out_specs=HBM, input_output_aliases={0: 0},
        scratch_shapes=[pltpu.SemaphoreType.DMA, pltpu.SemaphoreType.DMA],
    )(v)


def _allreduce_small(v, name):
    r, cols = v.shape

    def body(v_ref, out_ref, buf_ref, send_sems, recv_sems):
        x, y, c = _coords()
        me = 4 * x + 2 * y + c
        buf_ref[me] = v_ref[...]
        sends = []
        for k in range(1, N_DEV):
            px = 1 - x if k & 4 else x
            py = 1 - y if k & 2 else y
            pc = 1 - c if k & 1 else c
            sends.append(pltpu.make_async_remote_copy(
                src_ref=v_ref, dst_ref=buf_ref.at[me], send_sem=send_sems.at[k - 1], recv_sem=recv_sems.at[k - 1],
                device_id=(px, py, pc), device_id_type=MESH))
        for cp in sends:
            cp.start()
        for cp in sends:
            cp.wait()
        acc = buf_ref[0]
        for d in range(1, N_DEV):
            acc = acc + buf_ref[d]
        out_ref[...] = acc

    return pl.pallas_call(
        body, name=name, out_shape=jax.ShapeDtypeStruct((r, cols), F32),
        in_specs=[pl.BlockSpec(memory_space=pltpu.VMEM)], out_specs=pl.BlockSpec(memory_space=pltpu.VMEM),
        scratch_shapes=[pltpu.VMEM((N_DEV, r, cols), F32), pltpu.SemaphoreType.DMA((N_DEV - 1,)),
                        pltpu.SemaphoreType.DMA((N_DEV - 1,))],
    )(v)


def _add_sibling(g, from_sibling, core, name):
    n, h, cols = from_sibling.shape
    tr = _row_tile(h)
    steps = h // tr

    def body(core_ref, a_ref, b_ref, o_ref):
        o_ref[...] = (a_ref[...] + b_ref[...]).astype(o_ref.dtype)

    return pl.pallas_call(
        body, name=name, out_shape=jax.ShapeDtypeStruct(from_sibling.shape, jnp.bfloat16),
        grid_spec=pltpu.PrefetchScalarGridSpec(
            num_scalar_prefetch=1, grid=(n, steps),
            in_specs=[pl.BlockSpec((1, tr, cols), lambda s, i, core_ref: (s, core_ref[0] * steps + i, 0)),
                      pl.BlockSpec((1, tr, cols), lambda s, i, core_ref: (s, i, 0))],
            out_specs=pl.BlockSpec((1, tr, cols), lambda s, i, core_ref: (s, i, 0))),
        compiler_params=_params("parallel", "parallel"),
    )(core.reshape(1).astype(jnp.int32), g, from_sibling)


def _sum_slots(p, own, chip, core, name):
    n, r, cols = p.shape
    tr = _row_tile(r)
    steps = r // tr

    def body(core_ref, chip_ref, p_ref, own_ref, o_ref):
        parts = [jnp.where(chip_ref[0] == s, own_ref[0], p_ref[s]).astype(F32) for s in range(n)]
        o_ref[...] = ((parts[0] + parts[1]) + parts[2]) + parts[3]

    return pl.pallas_call(
        body, name=name, out_shape=jax.ShapeDtypeStruct((2 * r, cols), F32),
        grid_spec=pltpu.PrefetchScalarGridSpec(
            num_scalar_prefetch=2, grid=(steps,),
            in_specs=[pl.BlockSpec((n, tr, cols), lambda i, core_ref, chip_ref: (0, i, 0)),
                      pl.BlockSpec((1, tr, cols), lambda i, core_ref, chip_ref: (chip_ref[0], i, 0))],
            out_specs=pl.BlockSpec((tr, cols), lambda i, core_ref, chip_ref: (core_ref[0] * steps + i, 0))),
        compiler_params=_params("parallel"),
    )(core.reshape(1).astype(jnp.int32), chip.reshape(1).astype(jnp.int32), p, own)


PACK_COLS = 1024


def _pack_shards(parts):
    return jnp.concatenate([p.reshape(-1, PACK_COLS) for p in parts], axis=0)


def _unpack_shards(buf, shapes):
    out, row = [], 0
    for shp in shapes:
        nrows = math.prod(shp) // PACK_COLS
        out.append(buf[..., row:row + nrows, :].reshape(buf.shape[:-2] + tuple(shp)))
        row += nrows
    return out


def _local_step(x, target, w):
    t = lambda a: a.T
    g = {}
    w_in_g, w_in_x = w["a_w_in"][:, :D_RNN], w["a_w_in"][:, D_RNN:]
    h0, gate_br, x_br = _norm_and_project(x, w["norm_mix_g"][0], w_in_g, w_in_x, "rglru_in")
    y_a, hs = _rglru_fwd(gate_br, x_br, w["a_conv_w"], w["a_conv_b"], w["a_w_r"], w["a_w_i"], w["a_b_r"],
                         w["a_b_i"], w["a_lambda"], "rglru_fwd")
    x1, h1 = _matmul([(y_a, w["a_w_out"])], F32, "mm_a_out", addend=x, norm_gain=w["norm_ffn_g"][0])
    fg0, fu0, act0 = _ffn_up(h1, w["ffn_w_gate"][0], w["ffn_w_up"][0], "ffn0_up")
    x2, h2 = _matmul([(act0, w["ffn_w_down"][0])], F32, "mm_f0_down", addend=x1, norm_gain=w["norm_mix_g"][1],
                     tk=D_FF)
    qkv = _matmul([(h2, w["b_w_qkv"])], CD, "mm_b_qkv", out_lbm=True, tn=3 * D_MODEL)
    o = _attn_fwd(qkv, "attn_fwd")
    x3, h3 = _matmul([(o, w["b_w_out"])], F32, "mm_b_out", a_lbm=True, addend=x2, norm_gain=w["norm_ffn_g"][1])
    fg1, fu1, act1 = _ffn_up(h3, w["ffn_w_gate"][1], w["ffn_w_up"][1], "ffn1_up")
    dx4, dx4c, g["final_g"], loss = _matmul([(act1, w["ffn_w_down"][1])], F32, "mm_f1_down", addend=x3,
                                            loss_head=(w["final_g"], target), tk=D_FF)

    def ffn_bwd(dx_out, dxc, h, x_in, fg, fu, act, layer, tag):
        dg, du = _ffn_dact(dxc, t(w["ffn_w_down"][layer]), fg, fu, "ffn_" + tag + "_dact")
        dwd = _matmul([(act, dxc)], F32, "mm_" + tag + "_dwd", trans_a=True)
        dwg = _matmul([(h, dg)], F32, "mm_" + tag + "_dwg", trans_a=True)
        dwu = _matmul([(h, du)], F32, "mm_" + tag + "_dwu", trans_a=True)
        dx_in, dx_in_c, dgain = _matmul([(dg, t(w["ffn_w_gate"][layer])), (du, t(w["ffn_w_up"][layer]))], F32,
                                        "mm_" + tag + "_dh", norm_bwd=(x_in, w["norm_ffn_g"][layer], dx_out),
                                        tk=D_FF, tm=256)
        return dx_in, dx_in_c, dgain, dwg, dwu, dwd

    dx3, dx3c, dgf1, dwg1, dwu1, dwd1 = ffn_bwd(dx4, dx4c, h3, x3, fg1, fu1, act1, 1, "f1")
    do = _matmul([(dx3c, t(w["b_w_out"]))], CD, "mm_b_do", out_lbm=True, tn=1024)
    g["b_w_out"] = _matmul([(o, dx3c)], F32, "mm_b_dwout", trans_a=True, a_lbm=True)
    dq, dk, dv = _attn_bwd(qkv, o, do, "attn_bwd")
    wq_t = t(w["b_w_qkv"])
    parts = (dq, dk, dv)
    g["b_w_qkv"] = jnp.concatenate(
        [_matmul([(h2, p)], F32, "mm_b_dwqkv%d" % n, trans_a=True, b_lbm=True) for n, p in enumerate(parts)], axis=1)
    dx2, dx2c, dgm1 = _matmul([(p, wq_t[n * D_MODEL:(n + 1) * D_MODEL]) for n, p in enumerate(parts)], F32, "mm_b_dh",
                              a_lbm=True, norm_bwd=(x2, w["norm_mix_g"][1], dx3))
    dx1, dx1c, dgf0, dwg0, dwu0, dwd0 = ffn_bwd(dx2, dx2c, h1, x1, fg0, fu0, act0, 0, "f0")
    dy_a = _matmul([(dx1c, t(w["a_w_out"]))], F32, "mm_a_dy")
    g["a_w_out"] = _matmul([(y_a, dx1c)], F32, "mm_a_dwout", trans_a=True)
    wrt = jnp.swapaxes(w["a_w_r"], 1, 2)
    wit = jnp.swapaxes(w["a_w_i"], 1, 2)
    (dgate, dxbr, g["a_conv_w"], g["a_conv_b"], g["a_b_r"], g["a_b_i"], g["a_lambda"], g["a_w_r"],
     g["a_w_i"]) = _rglru_bwd(dy_a, gate_br, x_br, hs, w["a_conv_w"], w["a_conv_b"], w["a_w_r"], w["a_w_i"], wrt, wit,
                              w["a_b_r"], w["a_b_i"], w["a_lambda"], "rglru_bwd")
    g["a_w_in"] = jnp.concatenate([_matmul([(h0, dgate)], F32, "mm_a_dwin_g", trans_a=True),
                                   _matmul([(h0, dxbr)], F32, "mm_a_dwin_x", trans_a=True)], axis=1)
    dx0, _, dgm0 = _matmul([(dgate, t(w_in_g)), (dxbr, t(w_in_x))], F32, "mm_a_dh",
                           norm_bwd=(x, w["norm_mix_g"][0], dx1))
    g["norm_mix_g"] = jnp.concatenate([dgm0, dgm1], axis=0)
    g["norm_ffn_g"] = jnp.concatenate([dgf0, dgf1], axis=0)
    g["ffn_w_gate"] = [dwg0, dwg1]
    g["ffn_w_up"] = [dwu0, dwu1]
    g["ffn_w_down"] = [dwd0, dwd1]
    return loss, dx0, g


WEIGHTS = ["norm_mix_g", "norm_ffn_g", "a_w_in", "a_conv_w", "a_conv_b", "a_w_r", "a_b_r", "a_w_i", "a_b_i",
           "a_lambda", "a_w_out", "b_w_qkv", "b_w_out", "ffn_w_gate", "ffn_w_up", "ffn_w_down", "final_g"]
BIG = [("a_w_in", 2), ("a_w_r", 2), ("a_w_i", 2), ("a_w_out", 1), ("b_w_qkv", 2), ("b_w_out", 1),
       ("ffn_w_gate", 2), ("ffn_w_up", 2), ("ffn_w_down", 1)]
LAYER1 = ["b_w_qkv", "b_w_out", "ffn_w_gate", "ffn_w_up", "ffn_w_down"]
LAYER0 = ["a_w_in", "a_w_r", "a_w_i", "a_w_out", "ffn_w_gate", "ffn_w_up", "ffn_w_down"]
RS_COLLECTIVE_IDS = {"chips1": 3, "chips0": 4, "sibling1": 5, "share1": 6, "sibling0": 9}
GATHER_COLLECTIVE_IDS = (8, 7)
SMALL = ["norm_mix_g", "norm_ffn_g", "a_conv_w", "a_conv_b", "a_b_r", "a_b_i", "a_lambda", "final_g"]


def _split_chips(full, axis):
    if axis == 1:
        return full.reshape((N_CHIPS, 1, full.shape[1] // N_CHIPS) + full.shape[2:])
    return jnp.stack(jnp.split(full, N_CHIPS, axis=axis))


def _step(x, target, weights, moments_m, moments_v):
    chip = 2 * lax.axis_index("x") + lax.axis_index("y")
    core = lax.axis_index("c")
    axis_of = dict(BIG)
    full = {}
    for group, layer, tag, collective_id in ((LAYER0[:4], 0, "0a", None), (LAYER0[4:], 0, "0f", GATHER_COLLECTIVE_IDS[0]),
                                             (LAYER1, 1, "1", GATHER_COLLECTIVE_IDS[1])):
        shards = [weights[n][layer % weights[n].shape[0]].astype(CD) for n in group]
        packed = _pack_shards(shards)
        if collective_id is not None:
            packed, first_gathered = lax.optimization_barrier((packed, first_gathered))
        gathered = _allgather_chips(packed, "allgather_weights" + tag, collective_id)
        if collective_id is None:
            first_gathered = gathered
        for n, own, stack in zip(group, shards, _unpack_shards(gathered, [sh.shape for sh in shards])):
            joined = jnp.concatenate([jnp.where(chip == s, own, stack[s]) for s in range(N_CHIPS)],
                                     axis=axis_of[n] - 1)
            full.setdefault(n, {})[layer] = joined
    full = {n: (v[0] if n.startswith("a_") else v[1] if n.startswith("b_") else [v[0], v[1]]) for n, v in full.items()}
    cw_rows = jnp.zeros((N_CHIPS, CONV_W, RG_BW), F32)
    cw_rows = lax.dynamic_update_slice(cw_rows, jnp.where(core == 0, weights["a_conv_w"], 0.0), (chip, 0, 0))
    cw_all = _allreduce_small(cw_rows.reshape(-1, LANES), "allgather_conv_w").reshape(N_CHIPS, CONV_W, RG_BW)
    full["a_conv_w"] = jnp.concatenate([cw_all[s] for s in range(N_CHIPS)], axis=1)
    for n in ("norm_mix_g", "norm_ffn_g", "final_g"):
        full[n] = weights[n]
    for n in ("a_conv_b", "a_b_r", "a_b_i", "a_lambda"):
        full[n] = weights[n]
    loss, dx, grads = _local_step(x[0], target[0], full)
    small_parts = [grads[n].reshape(-1) for n in SMALL] + [loss.reshape(-1)]
    sizes = [p.shape[0] for p in small_parts]
    small = _allreduce_small(jnp.concatenate(small_parts).reshape(-1, LANES), "allreduce_small").reshape(-1)
    red, pos = {}, 0
    for n, sz in zip(SMALL + ["loss"], sizes):
        red[n] = small[pos:pos + sz]
        pos += sz
    loss_out = red["loss"][0]
    g_out = {}
    for n in SMALL:
        if n == "a_conv_w":
            g_out[n] = lax.dynamic_slice(red[n].reshape(CONV_W, D_RNN), (0, chip * RG_BW), (CONV_W, RG_BW)).reshape(
                weights[n].shape)
        else:
            g_out[n] = red[n].reshape(weights[n].shape)
    axis_of = dict(BIG)
    pieces = {}
    for group, layer, tag in ((LAYER1, 1, "1"), (LAYER0, 0, "0")):
        stacks, shapes = [], []
        for n in group:
            per_layer = isinstance(grads[n], list)
            gfull = grads[n][layer] if per_layer else grads[n]
            shard_shape = weights[n].shape[1:]
            gfull = gfull.reshape((1,) + gfull.shape)
            stacks.append(_split_chips(gfull, axis_of[n]).reshape(N_CHIPS, -1, PACK_COLS))
            shapes.append((1,) + tuple(shard_shape))
        gbuf = jnp.concatenate(stacks, axis=1)
        from_sibling = _exchange_sibling_halves_async(gbuf, "rs_sibling" + tag, RS_COLLECTIVE_IDS["sibling" + tag])
        chip_partial = _add_sibling(gbuf, from_sibling, core, "rs_add" + tag)
        from_chips = _scatter_to_chips_async(chip_partial, "rs_chips" + tag, RS_COLLECTIVE_IDS["chips" + tag])
        halves = _sum_slots(from_chips, chip_partial, chip, core, "rs_sum" + tag)
        if layer == 1:
            reduced = _share_halves_async(halves, "rs_share" + tag, RS_COLLECTIVE_IDS["share1"])
        else:
            reduced = _share_halves(halves, "rs_share" + tag)
        for n, piece in zip(group, _unpack_shards(reduced, shapes)):
            pieces.setdefault(n, {})[layer] = piece
    for n, _ in BIG:
        layers = pieces[n]
        g_out[n] = jnp.concatenate([layers[k] for k in sorted(layers)], axis=0)
    updates = {}
    for n, _ in BIG:
        updates[n] = _adamw(weights[n], g_out[n], moments_m[n], moments_v[n], "adamw_" + n)
    rows = lambda d: jnp.concatenate([d[n].reshape(-1, D_MODEL) for n in SMALL], axis=0)
    small_updates = _adamw(rows(weights), rows(g_out), rows(moments_m), rows(moments_v), "adamw_small")
    pos = 0
    for n in SMALL:
        nrows = weights[n].size // D_MODEL
        updates[n] = tuple(u[pos:pos + nrows].reshape(weights[n].shape) for u in small_updates)
        pos += nrows
    outs_g = [g_out[n] for n in WEIGHTS]
    outs_d, outs_m, outs_v = ([updates[n][k] for n in WEIGHTS] for k in range(3))
    return (loss_out, dx[None], *outs_g, *outs_d, *outs_m, *outs_v)


def kernel(x, norm_mix_g, norm_ffn_g, a_w_in, a_conv_w, a_conv_b, a_w_r, a_b_r, a_w_i, a_b_i, a_lambda, a_w_out, b_w_qkv, b_w_out, ffn_w_gate, ffn_w_up, ffn_w_down, final_g, loss_target, m_norm_mix_g, m_norm_ffn_g, m_a_w_in, m_a_conv_w, m_a_conv_b, m_a_w_r, m_a_b_r, m_a_w_i, m_a_b_i, m_a_lambda, m_a_w_out, m_b_w_qkv, m_b_w_out, m_ffn_w_gate, m_ffn_w_up, m_ffn_w_down, m_final_g, v_norm_mix_g, v_norm_ffn_g, v_a_w_in, v_a_conv_w, v_a_conv_b, v_a_w_r, v_a_b_r, v_a_w_i, v_a_b_i, v_a_lambda, v_a_w_out, v_b_w_qkv, v_b_w_out, v_ffn_w_gate, v_ffn_w_up, v_ffn_w_down, v_final_g):
    ws = [norm_mix_g, norm_ffn_g, a_w_in, a_conv_w, a_conv_b, a_w_r, a_b_r, a_w_i, a_b_i, a_lambda, a_w_out, b_w_qkv,
          b_w_out, ffn_w_gate, ffn_w_up, ffn_w_down, final_g]
    ms = [m_norm_mix_g, m_norm_ffn_g, m_a_w_in, m_a_conv_w, m_a_conv_b, m_a_w_r, m_a_b_r, m_a_w_i, m_a_b_i, m_a_lambda,
          m_a_w_out, m_b_w_qkv, m_b_w_out, m_ffn_w_gate, m_ffn_w_up, m_ffn_w_down, m_final_g]
    vs = [v_norm_mix_g, v_norm_ffn_g, v_a_w_in, v_a_conv_w, v_a_conv_b, v_a_w_r, v_a_b_r, v_a_w_i, v_a_b_i, v_a_lambda,
          v_a_w_out, v_b_w_qkv, v_b_w_out, v_ffn_w_gate, v_ffn_w_up, v_ffn_w_down, v_final_g]
    return _step(x, loss_target, dict(zip(WEIGHTS, ws)), dict(zip(WEIGHTS, ms)), dict(zip(WEIGHTS, vs)))
```

```python
import math

import jax
import jax.numpy as jnp
from jax import lax
from jax.experimental import pallas as pl
from jax.experimental.pallas import tpu as pltpu
from jax.experimental.pallas import tpu_sc as plsc

F32 = jnp.float32
CD = jnp.bfloat16

D_MODEL = 1024
D_RNN = 1024
RG_BLOCKS = 4
RG_BW = 256
CONV_W = 4
RG_C = 8.0
SB_HEADS = 16
SB_HEAD_DIM = 64
D_FF = 2816
RMS_EPS = 1e-6
N_CHIPS = 4
N_DEV = 8

ADAM_LR = 0.001
ADAM_B1 = 0.9
ADAM_B2 = 0.999
ADAM_EPS = 1e-08
ADAM_WD = 0.01
ADAM_STEP = 10

LANES = 128
VMEM_LIMIT = 56 * 1024 * 1024
MESH = pl.DeviceIdType.MESH


def _params(*sem):
    return pltpu.CompilerParams(dimension_semantics=sem, vmem_limit_bytes=VMEM_LIMIT)


def _pick(n, prefs):
    for p in prefs:
        if n % p == 0:
            return p
    return n


def _row_tile(rows):
    return max(d for d in range(16, 1025, 16) if rows % d == 0)


def _matmul(pairs, out_dtype, name, *, trans_a=False, a_lbm=False, b_lbm=False, out_lbm=False, addend=None,
            tm=512, tn=None, tk=None, norm_gain=None, norm_bwd=None, loss_head=None):
    a0, b0 = pairs[0]
    if trans_a:
        kdim = a0.shape[1] if a_lbm else a0.shape[0]
        m = a0.shape[0] * LANES if a_lbm else a0.shape[1]
    else:
        m = a0.shape[1] if a_lbm else a0.shape[0]
        kdim = a0.shape[0] * LANES if a_lbm else a0.shape[1]
    n = b0.shape[0] * LANES if b_lbm else b0.shape[1]
    if trans_a and m <= 1024:
        tm = m
    tm = _pick(m, (tm, 1408, 256, 128))
    tn = tn or _pick(n, (1408, 1024, 768, 512, 256, 128))
    tk = tk or _pick(kdim, (1024, 1408, 512, 256, 128))
    nk = kdim // tk
    npair = len(pairs)

    def cat(ref):
        return jnp.concatenate([ref[p] for p in range(ref.shape[0])], axis=-1)

    def body(*refs):
        ins = refs[: 2 * npair]
        pos = 2 * npair
        add_ref = None
        if addend is not None:
            add_ref = refs[pos]
            pos += 1
        gain_ref = x_ref = dxin_ref = None
        if norm_gain is not None:
            gain_ref = refs[pos]
            pos += 1
        if norm_bwd is not None:
            x_ref, gain_ref, dxin_ref = refs[pos:pos + 3]
            pos += 3
        if loss_head is not None:
            gain_ref, target_ref = refs[pos:pos + 2]
            pos += 2
        o_ref = refs[pos]
        extra_out = refs[pos + 1:-1]
        acc_ref = refs[-1]
        k = pl.program_id(2)

        @pl.when(k == 0)
        def _():
            acc_ref[...] = jnp.zeros_like(acc_ref)

        if norm_bwd is not None or loss_head is not None:
            @pl.when((k == 0) & (pl.program_id(0) == 0))
            def _():
                for ref in extra_out[1:]:
                    ref[...] = jnp.zeros_like(ref)

        acc = acc_ref[...]
        for p in range(npair):
            a = (cat(ins[2 * p]) if a_lbm else ins[2 * p][...]).astype(CD)
            b = (cat(ins[2 * p + 1]) if b_lbm else ins[2 * p + 1][...]).astype(CD)
            dims = (((0,), (0,)), ((), ())) if trans_a else (((1,), (0,)), ((), ()))
            acc = acc + lax.dot_general(a, b, dims, preferred_element_type=F32)
        acc_ref[...] = acc

        @pl.when(k == nk - 1)
        def _():
            res = acc_ref[...]
            if add_ref is not None:
                res = res + add_ref[...]
            if norm_gain is not None:
                rinv = lax.rsqrt(jnp.mean(res * res, axis=-1, keepdims=True) + RMS_EPS)
                extra_out[0][...] = (res * rinv * gain_ref[...]).astype(CD)
            if norm_bwd is not None:
                xv = x_ref[...]
                rinv = lax.rsqrt(jnp.mean(xv * xv, axis=-1, keepdims=True) + RMS_EPS)
                nrm = xv * rinv
                dn = res * gain_ref[...]
                extra_out[1][...] += jnp.sum(res * nrm, axis=0, keepdims=True)
                res = dxin_ref[...] + rinv * (dn - nrm * jnp.mean(dn * nrm, axis=-1, keepdims=True))
                extra_out[0][...] = res.astype(CD)
            if loss_head is not None:
                gv = gain_ref[...]
                rinv = lax.rsqrt(jnp.mean(res * res, axis=-1, keepdims=True) + RMS_EPS)
                nrm = res * rinv
                err = nrm * gv - target_ref[...]
                extra_out[2][...] += 0.5 * jnp.sum(jnp.mean(err * err, axis=-1, keepdims=True), axis=0, keepdims=True)
                dy = err * (1.0 / n)
                dn = dy * gv
                extra_out[1][...] += jnp.sum(dy * nrm, axis=0, keepdims=True)
                res = rinv * (dn - nrm * jnp.mean(dn * nrm, axis=-1, keepdims=True))
                extra_out[0][...] = res.astype(CD)
            res = res.astype(out_dtype)
            if out_lbm:
                for p in range(tn // LANES):
                    o_ref[p] = res[:, p * LANES:(p + 1) * LANES]
            else:
                o_ref[...] = res

    if trans_a:
        a_spec = (pl.BlockSpec((tm // LANES, tk, LANES), lambda i, j, k: (i, k, 0)) if a_lbm
                  else pl.BlockSpec((tk, tm), lambda i, j, k: (k, i)))
    else:
        a_spec = (pl.BlockSpec((tk // LANES, tm, LANES), lambda i, j, k: (k, i, 0)) if a_lbm
                  else pl.BlockSpec((tm, tk), lambda i, j, k: (i, k)))
    b_spec = (pl.BlockSpec((tn // LANES, tk, LANES), lambda i, j, k: (j, k, 0)) if b_lbm
              else pl.BlockSpec((tk, tn), lambda i, j, k: (k, j)))
    in_specs = []
    args = []
    for a, b in pairs:
        in_specs += [a_spec, b_spec]
        args += [a, b]
    if addend is not None:
        in_specs.append(pl.BlockSpec((tm, tn), lambda i, j, k: (i, j)))
        args.append(addend)
    tile = pl.BlockSpec((tm, tn), lambda i, j, k: (i, j))
    vec = pl.BlockSpec((1, tn), lambda i, j, k: (0, j))
    if out_lbm:
        out_shape = jax.ShapeDtypeStruct((n // LANES, m, LANES), out_dtype)
        out_spec = pl.BlockSpec((tn // LANES, tm, LANES), lambda i, j, k: (j, i, 0))
    else:
        out_shape = jax.ShapeDtypeStruct((m, n), out_dtype)
        out_spec = tile
    sem = ("parallel", "parallel", "arbitrary")
    if norm_gain is not None or norm_bwd is not None or loss_head is not None:
        assert tn == n and not out_lbm, "the norm needs whole rows in one tile"
        out_shape, out_spec = [out_shape, jax.ShapeDtypeStruct((m, n), CD)], [out_spec, tile]
    if norm_gain is not None:
        in_specs.append(vec)
        args.append(norm_gain.reshape(1, n))
    if norm_bwd is not None:
        x_in, gain, dx_in = norm_bwd
        in_specs += [tile, vec, tile]
        args += [x_in, gain.reshape(1, n), dx_in]
        out_shape.append(jax.ShapeDtypeStruct((1, n), F32))
        out_spec.append(vec)
        sem = ("arbitrary", "arbitrary", "arbitrary")
    if loss_head is not None:
        gain, target = loss_head
        in_specs += [vec, tile]
        args += [gain.reshape(1, n), target]
        out_shape += [jax.ShapeDtypeStruct((1, n), F32), jax.ShapeDtypeStruct((1, LANES), F32)]
        out_spec += [vec, pl.BlockSpec((1, LANES), lambda i, j, k: (0, 0))]
        sem = ("arbitrary", "arbitrary", "arbitrary")
    return pl.pallas_call(
        body, name=name, out_shape=out_shape, grid=(m // tm, n // tn, nk),
        in_specs=in_specs, out_specs=out_spec,
        scratch_shapes=[pltpu.VMEM((tm, tn), F32)],
        compiler_params=_params(*sem),
    )(*args)


def _norm_and_project(x, g, w_a, w_b, name):
    s, d = x.shape
    n = w_a.shape[1]
    tm = _pick(s, (512, 256))

    def body(x_ref, g_ref, wa_ref, wb_ref, h_ref, a_ref, b_ref):
        xv = x_ref[...]
        rinv = lax.rsqrt(jnp.mean(xv * xv, axis=-1, keepdims=True) + RMS_EPS)
        h = (xv * rinv * g_ref[...]).astype(CD)
        h_ref[...] = h
        a_ref[...] = jnp.dot(h, wa_ref[...], preferred_element_type=F32)
        b_ref[...] = jnp.dot(h, wb_ref[...], preferred_element_type=F32)

    row = pl.BlockSpec((tm, d), lambda i: (i, 0))
    out = pl.BlockSpec((tm, n), lambda i: (i, 0))
    wspec = pl.BlockSpec((d, n), lambda i: (0, 0))
    return pl.pallas_call(
        body, name=name,
        out_shape=(jax.ShapeDtypeStruct((s, d), CD), jax.ShapeDtypeStruct((s, n), F32),
                   jax.ShapeDtypeStruct((s, n), F32)),
        grid=(s // tm,), in_specs=[row, pl.BlockSpec((1, d), lambda i: (0, 0)), wspec, wspec],
        out_specs=(row, out, out), compiler_params=_params("parallel"),
    )(x, g.reshape(1, d), w_a, w_b)


def _sigmoid(z):
    return 1.0 / (1.0 + jnp.exp(-z))


FFN_TM = 512
FFN_TN = 1408


def _ffn_up(h, wg, wu, name):
    s, d = h.shape
    f = wg.shape[1]
    tm = _pick(s, (FFN_TM, 256))

    def body(h_ref, wg_ref, wu_ref, g_ref, u_ref, a_ref):
        hv = h_ref[...]
        gv = jnp.dot(hv, wg_ref[...], preferred_element_type=F32)
        uv = jnp.dot(hv, wu_ref[...], preferred_element_type=F32)
        g_ref[...] = gv
        u_ref[...] = uv
        a_ref[...] = (gv * _sigmoid(gv) * uv).astype(CD)

    a_spec = pl.BlockSpec((tm, d), lambda j, i: (i, 0))
    w_spec = pl.BlockSpec((d, FFN_TN), lambda j, i: (0, j))
    o_spec = pl.BlockSpec((tm, FFN_TN), lambda j, i: (i, j))
    return pl.pallas_call(
        body, name=name,
        out_shape=(jax.ShapeDtypeStruct((s, f), F32), jax.ShapeDtypeStruct((s, f), F32),
                   jax.ShapeDtypeStruct((s, f), CD)),
        grid=(f // FFN_TN, s // tm), in_specs=[a_spec, w_spec, w_spec], out_specs=(o_spec, o_spec, o_spec),
        compiler_params=_params("parallel", "parallel"),
    )(h, wg, wu)


def _ffn_dact(dxc, wd_t, g, u, name):
    s, d = dxc.shape
    f = wd_t.shape[1]
    tm = _pick(s, (FFN_TM, 256))

    def body(dx_ref, w_ref, g_ref, u_ref, dg_ref, du_ref):
        da = jnp.dot(dx_ref[...], w_ref[...], preferred_element_type=F32)
        gv = g_ref[...]
        sg = _sigmoid(gv)
        silu = gv * sg
        dg_ref[...] = (da * u_ref[...] * (sg + silu * (1.0 - sg))).astype(CD)
        du_ref[...] = (da * silu).astype(CD)

    a_spec = pl.BlockSpec((tm, d), lambda j, i: (i, 0))
    w_spec = pl.BlockSpec((d, FFN_TN), lambda j, i: (0, j))
    o_spec = pl.BlockSpec((tm, FFN_TN), lambda j, i: (i, j))
    return pl.pallas_call(
        body, name=name,
        out_shape=(jax.ShapeDtypeStruct((s, f), CD), jax.ShapeDtypeStruct((s, f), CD)),
        grid=(f // FFN_TN, s // tm), in_specs=[a_spec, w_spec, o_spec, o_spec], out_specs=(o_spec, o_spec),
        compiler_params=_params("parallel", "parallel"),
    )(dxc, wd_t, g, u)


TIME_BLOCK = 1024
SUBLANES = 8
GELU_C = math.sqrt(2.0 / math.pi)
GELU_A = 0.044715


def _gelu(x):
    return 0.5 * x * (1.0 + jnp.tanh(GELU_C * (x + GELU_A * x * x * x)))


def _gelu_grad(x):
    t = jnp.tanh(GELU_C * (x + GELU_A * x * x * x))
    return 0.5 * (1.0 + t) + 0.5 * x * (1.0 - t * t) * GELU_C * (1.0 + 3.0 * GELU_A * x * x)


def _neg_expm1(x):
    series = -x * (1.0 + x * (0.5 + x * (1.0 / 6.0 + x * (1.0 / 24.0))))
    return jnp.where(x > -0.05, series, 1.0 - jnp.exp(x))


def _log_sigmoid(x):
    return jnp.minimum(x, 0.0) - jnp.log1p(jnp.exp(-jnp.abs(x)))


def _shift_down(x, tail, s):
    if s == 0:
        return x
    ext = jnp.concatenate([tail, x], axis=0)
    return pltpu.roll(ext, s, axis=0)[SUBLANES:]


def _shift_up(x, head, s):
    if s == 0:
        return x
    n = x.shape[0]
    ext = jnp.concatenate([x, head], axis=0)
    return pltpu.roll(ext, n + SUBLANES - s, axis=0)[:n]


def _rg_gates(xbr, tail, cw_ref, cb, wr, wi, br, bi, ls):
    taps = [_shift_down(xbr, tail, CONV_W - 1 - k) for k in range(CONV_W)]
    xc = cb
    for k in range(CONV_W):
        xc = xc + cw_ref[pl.ds(k, 1), :] * taps[k]
    xcd = xc.astype(CD)
    r = _sigmoid(jnp.dot(xcd, wr, preferred_element_type=F32) + br)
    i = _sigmoid(jnp.dot(xcd, wi, preferred_element_type=F32) + bi)
    log_a = RG_C * r * ls
    a = jnp.exp(log_a)
    mult = jnp.sqrt(jnp.maximum(_neg_expm1(2.0 * log_a), 0.0))
    return taps, xc, r, i, log_a, a, mult


def _scan8_fwd(a, u):
    row = lax.broadcasted_iota(jnp.int32, a.shape, 0)
    for d in (1, 2, 4):
        a_s = pltpu.roll(a, d, axis=0)
        u_s = pltpu.roll(u, d, axis=0)
        m = row >= d
        u = jnp.where(m, a * u_s + u, u)
        a = jnp.where(m, a * a_s, a)
    return a, u


def _scan8_bwd(b, u):
    row = lax.broadcasted_iota(jnp.int32, b.shape, 0)
    for d in (1, 2, 4):
        b_s = pltpu.roll(b, SUBLANES - d, axis=0)
        u_s = pltpu.roll(u, SUBLANES - d, axis=0)
        m = row < SUBLANES - d
        u = jnp.where(m, b * u_s + u, u)
        b = jnp.where(m, b * b_s, b)
    return b, u


def _rglru_fwd(gate_br, x_br, cw, cb, wr, wi, br, bi, lam, name):
    s, c = x_br.shape
    nt = s // TIME_BLOCK
    tb, cbw = TIME_BLOCK, RG_BW
    groups = tb // SUBLANES

    def body(g_ref, x_ref, tail_ref, cw_ref, cb_ref, wr_ref, wi_ref, br_ref, bi_ref, lam_ref,
             y_ref, hs_ref, carry_ref, a_scr, u_scr):
        t = pl.program_id(1)

        @pl.when(t == 0)
        def _():
            carry_ref[...] = jnp.zeros_like(carry_ref)

        tail = jnp.where(t > 0, tail_ref[...], 0.0)
        ls = _log_sigmoid(lam_ref[...])
        _, xc, _, i, _, a, mult = _rg_gates(x_ref[...], tail, cw_ref, cb_ref[...], wr_ref[0], wi_ref[0],
                                            br_ref[...], bi_ref[...], ls)
        a_scr[...] = a
        u_scr[...] = mult * (i * xc)
        carry = carry_ref[...]
        for gi in range(groups):
            rows = pl.ds(gi * SUBLANES, SUBLANES)
            pa, hl = _scan8_fwd(a_scr[rows, :], u_scr[rows, :])
            hs_ref[rows, :] = hl + pa * carry
            carry = hs_ref[pl.ds(gi * SUBLANES + SUBLANES - 1, 1), :]
        carry_ref[...] = carry
        y_ref[...] = (hs_ref[...] * _gelu(g_ref[...])).astype(CD)

    blk = pl.BlockSpec((tb, cbw), lambda n, t: (t, n))
    tail = pl.BlockSpec((SUBLANES, cbw), lambda n, t: (jnp.maximum(t * groups - 1, 0), n))
    vec = pl.BlockSpec((1, cbw), lambda n, t: (0, n))
    wblk = pl.BlockSpec((1, cbw, cbw), lambda n, t: (n, 0, 0))
    return pl.pallas_call(
        body, name=name,
        out_shape=(jax.ShapeDtypeStruct((s, c), CD), jax.ShapeDtypeStruct((s, c), F32)),
        grid=(RG_BLOCKS, nt),
        in_specs=[blk, blk, tail, pl.BlockSpec((CONV_W, cbw), lambda n, t: (0, n)), vec, wblk, wblk, vec, vec, vec],
        out_specs=(blk, blk),
        scratch_shapes=[pltpu.VMEM((1, cbw), F32), pltpu.VMEM((tb, cbw), F32), pltpu.VMEM((tb, cbw), F32)],
        compiler_params=_params("parallel", "arbitrary"),
    )(gate_br, x_br, x_br, cw, cb, wr, wi, br, bi, lam)


def _rglru_bwd(dy, gate_br, x_br, hs, cw, cb, wr, wi, wrt, wit, br, bi, lam, name):
    s, c = x_br.shape
    nt = s // TIME_BLOCK
    tb, cbw = TIME_BLOCK, RG_BW
    groups = tb // SUBLANES

    def body(dy_ref, g_ref, x_ref, tail_ref, hs_ref, hprev_ref, cw_ref, cb_ref, wr_ref, wi_ref, wrt_ref, wit_ref,
             br_ref, bi_ref, lam_ref,
             dg_ref, dx_ref, dcw_ref, dcb_ref, dbr_ref, dbi_ref, dlam_ref, dwr_ref, dwi_ref,
             carry_ref, head_ref, b_scr, u_scr, dh_scr):
        tr = pl.program_id(1)
        first_block = tr == nt - 1

        @pl.when(tr == 0)
        def _():
            carry_ref[...] = jnp.zeros_like(carry_ref)
            head_ref[...] = jnp.zeros_like(head_ref)
            for ref in (dcw_ref, dcb_ref, dbr_ref, dbi_ref, dlam_ref, dwr_ref, dwi_ref):
                ref[...] = jnp.zeros_like(ref)

        tail = jnp.where(first_block, 0.0, tail_ref[...])
        lam_v = lam_ref[...]
        ls = _log_sigmoid(lam_v)
        taps, xc, r, i, log_a, a, mult = _rg_gates(x_ref[...], tail, cw_ref, cb_ref[...], wr_ref[0], wi_ref[0],
                                                   br_ref[...], bi_ref[...], ls)
        gate_v = g_ref[...]
        dyv = dy_ref[...]
        hsv = hs_ref[...]
        dg_ref[...] = (dyv * hsv * _gelu_grad(gate_v)).astype(CD)

        row = lax.broadcasted_iota(jnp.int32, a.shape, 0)
        b_scr[...] = jnp.where(row == tb - 1, 1.0, pltpu.roll(a, tb - 1, axis=0))
        u_scr[...] = dyv * _gelu(gate_v)
        carry = carry_ref[...]
        for gi in reversed(range(groups)):
            rows = pl.ds(gi * SUBLANES, SUBLANES)
            pb, gl = _scan8_bwd(b_scr[rows, :], u_scr[rows, :])
            dh_scr[rows, :] = gl + pb * carry
            carry = dh_scr[pl.ds(gi * SUBLANES, 1), :]
        dh = dh_scr[...]
        carry_ref[...] = carry * jnp.sum(jnp.where(row == 0, a, 0.0), axis=0, keepdims=True)

        hprev_tail = jnp.where(first_block, 0.0, hprev_ref[...])
        h_prev = _shift_down(hsv, hprev_tail, 1)
        da = dh * h_prev
        ixc = i * xc
        dmult = dh * ixc
        di = dh * mult * xc
        dxc = dh * mult * i
        a2 = a * a
        dlog_a = da * a - dmult * a2 / mult
        dpre_r = (dlog_a * (RG_C * ls)) * r * (1.0 - r)
        dpre_i = di * i * (1.0 - i)
        dlam_ref[...] += jnp.sum(dlog_a * r, axis=0, keepdims=True) * (RG_C * _sigmoid(-lam_v))
        dbr_ref[...] += jnp.sum(dpre_r, axis=0, keepdims=True)
        dbi_ref[...] += jnp.sum(dpre_i, axis=0, keepdims=True)
        xcd = xc.astype(CD)
        dprc = dpre_r.astype(CD)
        dpic = dpre_i.astype(CD)
        tn_dims = (((0,), (0,)), ((), ()))
        dwr_ref[0] += lax.dot_general(xcd, dprc, tn_dims, preferred_element_type=F32)
        dwi_ref[0] += lax.dot_general(xcd, dpic, tn_dims, preferred_element_type=F32)
        dxc = dxc + jnp.dot(dprc, wrt_ref[0], preferred_element_type=F32) + jnp.dot(dpic, wit_ref[0],
                                                                                    preferred_element_type=F32)
        dcb_ref[...] += jnp.sum(dxc, axis=0, keepdims=True)
        for k in range(CONV_W):
            dcw_ref[pl.ds(k, 1), :] += jnp.sum(dxc * taps[k], axis=0, keepdims=True)
        head = head_ref[...]
        dxb = jnp.zeros_like(dxc)
        for sft in range(CONV_W):
            dxb = dxb + cw_ref[pl.ds(CONV_W - 1 - sft, 1), :] * _shift_up(dxc, head, sft)
        dx_ref[...] = dxb.astype(CD)
        head_ref[...] = dxc[0:SUBLANES, :]

    blk = pl.BlockSpec((tb, cbw), lambda n, t: (nt - 1 - t, n))
    tail = pl.BlockSpec((SUBLANES, cbw), lambda n, t: (jnp.maximum((nt - 1 - t) * groups - 1, 0), n))
    vec = pl.BlockSpec((1, cbw), lambda n, t: (0, n))
    cwb = pl.BlockSpec((CONV_W, cbw), lambda n, t: (0, n))
    wblk = pl.BlockSpec((1, cbw, cbw), lambda n, t: (n, 0, 0))
    vshape = jax.ShapeDtypeStruct((1, c), F32)
    wshape = jax.ShapeDtypeStruct((RG_BLOCKS, cbw, cbw), F32)
    return pl.pallas_call(
        body, name=name,
        out_shape=(jax.ShapeDtypeStruct((s, c), CD), jax.ShapeDtypeStruct((s, c), CD),
                   jax.ShapeDtypeStruct((CONV_W, c), F32), vshape, vshape, vshape, vshape, wshape, wshape),
        grid=(RG_BLOCKS, nt),
        in_specs=[blk, blk, blk, tail, blk, tail, cwb, vec, wblk, wblk, wblk, wblk, vec, vec, vec],
        out_specs=(blk, blk, cwb, vec, vec, vec, vec, wblk, wblk),
        scratch_shapes=[pltpu.VMEM((1, cbw), F32), pltpu.VMEM((SUBLANES, cbw), F32),
                        pltpu.VMEM((tb, cbw), F32), pltpu.VMEM((tb, cbw), F32), pltpu.VMEM((tb, cbw), F32)],
        compiler_params=_params("parallel", "arbitrary"),
    )(dy, gate_br, x_br, x_br, hs, hs, cw, cb, wr, wi, wrt, wit, br, bi, lam)


ATT_BLOCK = 256
ATT_Q_BLOCK = 1024
ATT_Q_BLOCK_FWD = 2048
ATT_RATIO = ATT_Q_BLOCK // ATT_BLOCK
ATT_SCALE = 1.0 / math.sqrt(SB_HEAD_DIM)
N_PAIRS = SB_HEADS * SB_HEAD_DIM // LANES
NT_DIMS = (((1,), (1,)), ((), ()))
TN_DIMS = (((0,), (0,)), ((), ()))


LOG2E = 1.4426950408889634


def _neg_abs(x):
    bits = lax.bitcast_convert_type(x, jnp.uint32) | jnp.uint32(0x80000000)
    return lax.bitcast_convert_type(bits, F32)


def _qk(qx, kb):
    return lax.dot_general(qx, kb, NT_DIMS, preferred_element_type=F32)


def _sb_logits(qk, valid):
    z2 = qk * (ATT_SCALE * LOG2E)
    lb2 = jnp.minimum(z2, 0.0) - jnp.log2(1.0 + jnp.exp2(_neg_abs(z2)))
    l2 = lb2 - z2
    if valid is not None:
        l2 = jnp.where(valid, l2, 0.0)
    return lb2, l2


def _hi_lo(x):
    hi = x.astype(CD)
    lo = (x - hi.astype(F32)).astype(CD)
    return jnp.concatenate([hi, lo], axis=1)


def _tri(strict, stacked):
    r = lax.broadcasted_iota(jnp.int32, (ATT_BLOCK, ATT_BLOCK), 0)
    c = lax.broadcasted_iota(jnp.int32, (ATT_BLOCK, ATT_BLOCK), 1)
    m = (r > c if strict else r >= c).astype(CD)
    return jnp.concatenate([m, m], axis=0) if stacked else m


def _attn_fwd(qkv, name):
    _, s, _ = qkv.shape
    tq, t = _pick(s, (ATT_Q_BLOCK_FWD, ATT_Q_BLOCK)), ATT_BLOCK
    ratio = tq // t
    nblk = s // tq

    def body(q_ref, k_ref, v_ref, o_ref, qk_scr, w_scr):
        i = pl.program_id(1)
        lane = lax.broadcasted_iota(jnp.int32, (1, LANES), 1)
        head_masks = (lane < SB_HEAD_DIM, lane >= SB_HEAD_DIM)
        q = q_ref[0]
        qs = [jnp.where(m, q, jnp.zeros_like(q)) for m in head_masks]
        tri = _tri(True, False)
        rr = lax.broadcasted_iota(jnp.int32, (tq, t), 0)
        cc = lax.broadcasted_iota(jnp.int32, (tq, t), 1)

        def rows_of(j):
            return pl.ds(pl.multiple_of(j * t, t), t)

        def tail(x, row0):
            return x if row0 == 0 else x[row0:]

        def start_logits(j, row0=0):
            kb = k_ref[0, rows_of(j), :]
            for hd in range(2):
                qk_scr[hd, row0:, :] = _qk(tail(qs[hd], row0), kb)

        def weights(run, diagonal=False, row0=0):
            new_run = []
            valid = (cc < rr)[:tq - row0] if diagonal else None
            for hd in range(2):
                lb2, l2 = _sb_logits(qk_scr[hd, row0:, :], valid)
                w = jnp.exp2(lb2 + (tail(run[hd], row0) + jnp.dot(l2.astype(CD), tri, preferred_element_type=F32)))
                if valid is not None:
                    w = jnp.where(valid, w, 0.0)
                w_scr[row0:, hd * t:(hd + 1) * t] = w.astype(CD)
                rowsum = jnp.sum(l2, axis=1, keepdims=True)
                if row0:
                    rowsum = jnp.concatenate([jnp.zeros((row0, 1), F32), rowsum], axis=0)
                new_run.append(run[hd] + rowsum)
            return tuple(new_run)

        def apply_weights(j, row0=0):
            vb = v_ref[0, rows_of(j), :]
            vcat = jnp.concatenate([jnp.where(m, vb, jnp.zeros_like(vb)) for m in head_masks], axis=0)
            inc = jnp.dot(w_scr[row0:, :], vcat, preferred_element_type=F32)
            return inc if row0 == 0 else jnp.concatenate([jnp.zeros((row0, LANES), F32), inc], axis=0)

        zero = jnp.zeros((tq, 1), F32)
        last = ratio - 1
        start_logits(ratio * i + last, last * t)
        run = weights((zero, zero), True, last * t)
        oacc = jnp.zeros((tq, LANES), F32)
        for d in reversed(range(last)):
            start_logits(ratio * i + d, d * t)
            oacc = oacc + apply_weights(ratio * i + d + 1, (d + 1) * t)
            run = weights(run, True, d * t)
        start_logits(jnp.maximum(ratio * i - 1, 0))

        def step(jj, carry):
            run, oacc = carry
            b = ratio * i - 1 - jj
            oacc = oacc + apply_weights(b + 1)
            run = weights(run)
            start_logits(jnp.maximum(b - 1, 0))
            return run, oacc

        run, oacc = lax.fori_loop(0, ratio * i, step, (run, oacc))
        o_ref[0] = oacc + apply_weights(0)

    return pl.pallas_call(
        body, name=name, out_shape=jax.ShapeDtypeStruct((N_PAIRS, s, LANES), F32), grid=(N_PAIRS, nblk),
        in_specs=[pl.BlockSpec((1, tq, LANES), lambda p, i: (p, i, 0)),
                  pl.BlockSpec((1, s, LANES), lambda p, i: (N_PAIRS + p, 0, 0)),
                  pl.BlockSpec((1, s, LANES), lambda p, i: (2 * N_PAIRS + p, 0, 0))],
        out_specs=pl.BlockSpec((1, tq, LANES), lambda p, i: (p, i, 0)),
        scratch_shapes=[pltpu.VMEM((2, tq, t), F32), pltpu.VMEM((tq, 2 * t), CD)],
        compiler_params=_params("parallel", "arbitrary"),
    )(qkv, qkv, qkv)


def _attn_bwd(qkv, o, do, name):
    _, s, _ = qkv.shape
    tq, t = ATT_Q_BLOCK, ATT_BLOCK
    nblk = s // tq

    def body(q_ref, k_ref, v_ref, o_ref, do_ref, dq_ref, dk_ref, dv_ref, qk_scr, dw_scr, w_scr, dz_scr):
        i = pl.program_id(1)

        @pl.when(i == 0)
        def _():
            dk_ref[...] = jnp.zeros_like(dk_ref)
            dv_ref[...] = jnp.zeros_like(dv_ref)

        lane = lax.broadcasted_iota(jnp.int32, (1, LANES), 1)
        head_masks = (lane < SB_HEAD_DIM, lane >= SB_HEAD_DIM)
        q = q_ref[0]
        dov = do_ref[0]
        ov = o_ref[0]
        qs = [jnp.where(m, q, jnp.zeros_like(q)) for m in head_masks]
        q_scaled_t = jnp.concatenate([(qx.astype(F32) * ATT_SCALE).T for qx in qs], axis=1).astype(CD)
        docs = [jnp.where(m, dov, jnp.zeros_like(dov)) for m in head_masks]
        docat_t = jnp.concatenate([d.astype(F32).T for d in docs], axis=1).astype(CD)
        totals = [jnp.sum(d.astype(F32) * ov, axis=1, keepdims=True) for d in docs]
        tri = _tri(True, False)
        tri_incl = _tri(False, True)
        rr = lax.broadcasted_iota(jnp.int32, (tq, t), 0)
        cc = lax.broadcasted_iota(jnp.int32, (tq, t), 1)

        def rows_of(j):
            return pl.ds(pl.multiple_of(j * t, t), t)

        def tail(x, row0):
            return x if row0 == 0 else x[row0:]

        def pad_rows(x, row0):
            return x if row0 == 0 else jnp.concatenate([jnp.zeros((row0, x.shape[1]), x.dtype), x], axis=0)

        def start_products(j, row0=0):
            kb = k_ref[0, rows_of(j), :]
            vb = v_ref[0, rows_of(j), :]
            for hd in range(2):
                qk_scr[hd, row0:, :] = _qk(tail(qs[hd], row0), kb)
                dw_scr[hd, row0:, :] = lax.dot_general(tail(docs[hd], row0), vb, NT_DIMS, preferred_element_type=F32)

        def logit_grads(run, erun, diagonal=False, row0=0):
            new_run, new_erun = [], []
            valid = (cc < rr)[:tq - row0] if diagonal else None
            for hd in range(2):
                lb2, l2 = _sb_logits(qk_scr[hd, row0:, :], valid)
                w = jnp.exp2(lb2 + (tail(run[hd], row0) + jnp.dot(l2.astype(CD), tri, preferred_element_type=F32)))
                if valid is not None:
                    w = jnp.where(valid, w, 0.0)
                wc = w.astype(CD)
                w_scr[hd * tq + row0:(hd + 1) * tq, :] = wc
                e = dw_scr[hd, row0:, :] * wc.astype(F32)
                prefix = (tail(totals[hd] - erun[hd], row0)
                          - jnp.dot(_hi_lo(e), tri_incl, preferred_element_type=F32))
                dz = e - jnp.exp2(lb2) * (e + prefix)
                if valid is not None:
                    dz = jnp.where(valid, dz, 0.0)
                dz_scr[hd * tq + row0:(hd + 1) * tq, :] = dz.astype(CD)
                new_run.append(run[hd] + pad_rows(jnp.sum(l2, axis=1, keepdims=True), row0))
                new_erun.append(erun[hd] + pad_rows(jnp.sum(e, axis=1, keepdims=True), row0))
            return tuple(new_run), tuple(new_erun)

        def apply_grads(j, row0=0):
            rows = rows_of(j)
            kb = k_ref[0, rows, :]
            kcat = jnp.concatenate([jnp.where(m, kb, jnp.zeros_like(kb)) for m in head_masks], axis=0)
            dz_heads = [dz_scr[hd * tq + row0:(hd + 1) * tq, :] for hd in range(2)]
            w_heads = [w_scr[hd * tq + row0:(hd + 1) * tq, :] for hd in range(2)]
            q_t = jnp.concatenate([q_scaled_t[:, hd * tq + row0:(hd + 1) * tq] for hd in range(2)], axis=1)
            do_t = jnp.concatenate([docat_t[:, hd * tq + row0:(hd + 1) * tq] for hd in range(2)], axis=1)
            dk_ref[0, :, rows] += jnp.dot(q_t, jnp.concatenate(dz_heads, axis=0), preferred_element_type=F32)
            dv_ref[0, :, rows] += jnp.dot(do_t, jnp.concatenate(w_heads, axis=0), preferred_element_type=F32)
            return pad_rows(jnp.dot(jnp.concatenate(dz_heads, axis=1), kcat, preferred_element_type=F32), row0)

        zero = jnp.zeros((tq, 1), F32)
        last = ATT_RATIO - 1
        start_products(ATT_RATIO * i + last, last * t)
        run, erun = logit_grads((zero, zero), (zero, zero), True, last * t)
        dqacc = jnp.zeros((tq, LANES), F32)
        for d in reversed(range(last)):
            start_products(ATT_RATIO * i + d, d * t)
            dqacc = dqacc + apply_grads(ATT_RATIO * i + d + 1, (d + 1) * t)
            run, erun = logit_grads(run, erun, True, d * t)
        start_products(jnp.maximum(ATT_RATIO * i - 1, 0))

        def step(jj, carry):
            run, erun, dqacc = carry
            b = ATT_RATIO * i - 1 - jj
            dqacc = dqacc + apply_grads(b + 1)
            run, erun = logit_grads(run, erun)
            start_products(jnp.maximum(b - 1, 0))
            return run, erun, dqacc

        run, erun, dqacc = lax.fori_loop(0, ATT_RATIO * i, step, (run, erun, dqacc))
        dq_ref[0] = ((dqacc + apply_grads(0)) * ATT_SCALE).astype(CD)

    qblk = pl.BlockSpec((1, tq, LANES), lambda p, i: (p, i, 0))
    full = pl.BlockSpec((1, LANES, s), lambda p, i: (p, 0, 0))
    shape = jax.ShapeDtypeStruct((N_PAIRS, s, LANES), F32)
    shape_t = jax.ShapeDtypeStruct((N_PAIRS, LANES, s), F32)
    dq, dk_t, dv_t = pl.pallas_call(
        body, name=name, out_shape=(jax.ShapeDtypeStruct(shape.shape, CD), shape_t, shape_t), grid=(N_PAIRS, nblk),
        in_specs=[qblk,
                  pl.BlockSpec((1, s, LANES), lambda p, i: (N_PAIRS + p, 0, 0)),
                  pl.BlockSpec((1, s, LANES), lambda p, i: (2 * N_PAIRS + p, 0, 0)),
                  qblk, qblk],
        out_specs=(qblk, full, full),
        scratch_shapes=[pltpu.VMEM((2, tq, t), F32), pltpu.VMEM((2, tq, t), F32),
                        pltpu.VMEM((2 * tq, t), CD), pltpu.VMEM((2 * tq, t), CD)],
        compiler_params=_params("parallel", "arbitrary"),
    )(qkv, qkv, qkv, o, do)
    return dq, jnp.swapaxes(dk_t, 1, 2).astype(CD), jnp.swapaxes(dv_t, 1, 2).astype(CD)


def _adamw(w, g, m, v, name):
    shape = w.shape
    rows, cols = (shape[-2], shape[-1]) if len(shape) >= 2 else (1, shape[-1])
    lead = w.size // (rows * cols)
    tr = _pick(rows, (512, 256, 128, 64, 32, 16, 8))

    def body(w_ref, g_ref, m_ref, v_ref, d_ref, nm_ref, nv_ref):
        gv = g_ref[...]
        nm = ADAM_B1 * m_ref[...] + (1.0 - ADAM_B1) * gv
        nv = ADAM_B2 * v_ref[...] + (1.0 - ADAM_B2) * (gv * gv)
        m_hat = nm / (1.0 - ADAM_B1 ** ADAM_STEP)
        v_hat = nv / (1.0 - ADAM_B2 ** ADAM_STEP)
        d_ref[...] = -ADAM_LR * (m_hat / (jnp.sqrt(v_hat) + ADAM_EPS) + ADAM_WD * w_ref[...])
        nm_ref[...] = nm
        nv_ref[...] = nv

    blk = pl.BlockSpec((1, tr, cols), lambda l, i: (l, i, 0))
    out = jax.ShapeDtypeStruct((lead, rows, cols), F32)
    d, nm, nv = pl.pallas_call(
        body, name=name, out_shape=(out, out, out), grid=(lead, rows // tr),
        in_specs=[blk, blk, blk, blk], out_specs=(blk, blk, blk), compiler_params=_params("parallel", "parallel"),
    )(*[a.reshape(lead, rows, cols) for a in (w, g, m, v)])
    return d.reshape(shape), nm.reshape(shape), nv.reshape(shape)


HBM = pl.BlockSpec(memory_space=pltpu.HBM)


def _coords():
    return lax.axis_index("x"), lax.axis_index("y"), lax.axis_index("c")


def _other_chips(x, y):
    return [(1 - x, y), (x, 1 - y), (1 - x, 1 - y)]


def _allgather_chips(shard, name, collective_id=None):
    r, cols = shard.shape
    half = r // 2
    quarter = half // 2

    def body(src_ref, out_ref, send_sems, recv_sems):
        x, y, c = _coords()
        sibling = (x, y, 1 - c)
        nx, ny, diag = (1 - x, y), (x, 1 - y), (1 - x, 1 - y)

        def piece(chip, core, lo, n):
            return out_ref.at[2 * chip[0] + chip[1], pl.ds(core * half + lo, n), :]

        def copy(k, dst, to, src=None):
            return pltpu.make_async_remote_copy(
                src_ref=dst if src is None else src, dst_ref=dst,
                send_sem=send_sems.at[k], recv_sem=recv_sems.at[k], device_id=to, device_id_type=MESH)

        me = (x, y)
        mine = src_ref.at[pl.ds(c * half, half), :]
        direct = [copy(0, piece(me, c, 0, half), (*nx, c), src=mine), copy(1, piece(me, c, 0, half), (*ny, c), src=mine)]
        for cp in direct:
            cp.start()
        arrivals = [piece(nx, c, 0, half), piece(ny, c, 0, half), piece(diag, c, 0, quarter),
                    piece(diag, c, quarter, quarter)]
        onward = [copy(2, piece(nx, c, 0, quarter), (*ny, c)), copy(3, piece(ny, c, quarter, quarter), (*nx, c))]
        to_sibling = [copy(4 + k, dst, sibling) for k, dst in enumerate(arrivals)]
        for k, dst in enumerate(arrivals):
            copy(k, dst, (x, y, c)).wait_recv()
            if k < 2:
                onward[k].start()
            to_sibling[k].start()
        from_sibling = [piece(nx, 1 - c, 0, half), piece(ny, 1 - c, 0, half), piece(diag, 1 - c, 0, quarter),
                        piece(diag, 1 - c, quarter, quarter)]
        for k, dst in enumerate(from_sibling):
            copy(4 + k, dst, (x, y, c)).wait_recv()
        for cp in direct + onward + to_sibling:
            cp.wait_send()

    out_shape = jax.ShapeDtypeStruct((N_CHIPS, r, cols), shard.dtype)
    sems = (pltpu.SemaphoreType.DMA((8,)), pltpu.SemaphoreType.DMA((8,)))
    if collective_id is None:
        return pl.pallas_call(body, name=name, out_shape=out_shape, in_specs=[HBM], out_specs=HBM,
                              scratch_shapes=list(sems))(shard)
    shard_ref = jax.new_ref(shard, memory_space=pltpu.MemorySpace.HBM)
    gathered_ref = jax.empty_ref(out_shape, memory_space=pltpu.MemorySpace.HBM)

    @_sequencer(name, collective_id, sems)
    def launch(send_sems, recv_sems):
        x, y, c = _coords()
        _handshake([(1 - x, y, c), (x, 1 - y, c), (x, y, 1 - c)])
        body(shard_ref, gathered_ref, send_sems, recv_sems)

    launch()
    return gathered_ref[...]


def _exchange_sibling_halves(g, name):
    n, r, cols = g.shape
    half = r // 2

    def body(g_ref, out_ref, send_sem, recv_sem):
        x, y, c = _coords()
        cp = pltpu.make_async_remote_copy(
            src_ref=g_ref.at[:, pl.ds((1 - c) * half, half), :], dst_ref=out_ref,
            send_sem=send_sem, recv_sem=recv_sem, device_id=(x, y, 1 - c), device_id_type=MESH)
        cp.start()
        cp.wait()

    return pl.pallas_call(
        body, name=name, out_shape=jax.ShapeDtypeStruct((n, half, cols), g.dtype),
        in_specs=[HBM], out_specs=HBM,
        scratch_shapes=[pltpu.SemaphoreType.DMA, pltpu.SemaphoreType.DMA],
    )(g)


def _sequencer(name, collective_id, scratch_types):
    return pl.kernel(mesh=plsc.ScalarSubcoreMesh(axis_name="sequencer", num_cores=1), name=name,
                     scratch_types=scratch_types, compiler_params=pltpu.CompilerParams(collective_id=collective_id))


def _handshake(peers):
    barrier = pltpu.get_barrier_semaphore()
    for peer in peers:
        pl.semaphore_signal(barrier, inc=1, device_id=peer, device_id_type=MESH)
    pl.semaphore_wait(barrier, len(peers))


def _exchange_sibling_halves_async(g, name, collective_id):
    n, r, cols = g.shape
    half = r // 2
    g_ref = jax.new_ref(g, memory_space=pltpu.MemorySpace.HBM)
    out_ref = jax.empty_ref(jax.ShapeDtypeStruct((n, half, cols), g.dtype), memory_space=pltpu.MemorySpace.HBM)

    @_sequencer(name, collective_id, (pltpu.SemaphoreType.DMA, pltpu.SemaphoreType.DMA))
    def launch(send_sem, recv_sem):
        x, y, c = _coords()
        _handshake([(x, y, 1 - c)])
        cp = pltpu.make_async_remote_copy(
            src_ref=g_ref.at[:, pl.ds((1 - c) * half, half), :], dst_ref=out_ref,
            send_sem=send_sem, recv_sem=recv_sem, device_id=(x, y, 1 - c), device_id_type=MESH)
        cp.start()
        cp.wait()

    launch()
    return out_ref[...]


def _share_halves_async(v, name, collective_id):
    h = v.shape[0] // 2
    v_ref = jax.new_ref(v, memory_space=pltpu.MemorySpace.HBM)

    @_sequencer(name, collective_id, (pltpu.SemaphoreType.DMA, pltpu.SemaphoreType.DMA))
    def launch(send_sem, recv_sem):
        x, y, c = _coords()
        _handshake([(x, y, 1 - c)])
        cp = pltpu.make_async_remote_copy(
            src_ref=v_ref.at[pl.ds(c * h, h), :], dst_ref=v_ref.at[pl.ds(c * h, h), :],
            send_sem=send_sem, recv_sem=recv_sem, device_id=(x, y, 1 - c), device_id_type=MESH)
        cp.start()
        pltpu.make_async_remote_copy(
            src_ref=v_ref.at[pl.ds(c * h, h), :], dst_ref=v_ref.at[pl.ds((1 - c) * h, h), :],
            send_sem=send_sem, recv_sem=recv_sem, device_id=(x, y, 1 - c), device_id_type=MESH).wait_recv()
        cp.wait_send()

    launch()
    return v_ref[...]


def _scatter_to_chips_async(p, name, collective_id):
    p_ref = jax.new_ref(p, memory_space=pltpu.MemorySpace.HBM)
    out_ref = jax.empty_ref(jax.ShapeDtypeStruct(p.shape, p.dtype), memory_space=pltpu.MemorySpace.HBM)

    @_sequencer(name, collective_id, (pltpu.SemaphoreType.DMA((3,)), pltpu.SemaphoreType.DMA((3,))))
    def launch(send_sems, recv_sems):
        x, y, c = _coords()
        me = 2 * x + y
        _handshake([(px, py, c) for px, py in _other_chips(x, y)])
        sends = []
        for j, (px, py) in enumerate(_other_chips(x, y)):
            sends.append(pltpu.make_async_remote_copy(
                src_ref=p_ref.at[2 * px + py], dst_ref=out_ref.at[me],
                send_sem=send_sems.at[j], recv_sem=recv_sems.at[j], device_id=(px, py, c), device_id_type=MESH))
        for cp in sends:
            cp.start()
        for j, (px, py) in enumerate(_other_chips(x, y)):
            pltpu.make_async_remote_copy(
                src_ref=p_ref.at[me], dst_ref=out_ref.at[2 * px + py],
                send_sem=send_sems.at[j], recv_sem=recv_sems.at[j], device_id=(px, py, c),
                device_id_type=MESH).wait_recv()
        for cp in sends:
            cp.wait_send()

    launch()
    return out_ref[...]


def _share_halves(v, name):
    h = v.shape[0] // 2

    def body(v_ref, out_ref, send_sem, recv_sem):
        x, y, c = _coords()
        cp = pltpu.make_async_remote_copy(
            src_ref=v_ref.at[pl.ds(c * h, h), :], dst_ref=out_ref.at[pl.ds(c * h, h), :],
            send_sem=send_sem, recv_sem=recv_sem, device_id=(x, y, 1 - c), device_id_type=MESH)
        cp.start()
        pltpu.make_async_remote_copy(
            src_ref=v_ref.at[pl.ds(c * h, h), :], dst_ref=out_ref.at[pl.ds((1 - c) * h, h), :],
            send_sem=send_sem, recv_sem=recv_sem, device_id=(x, y, 1 - c), device_id_type=MESH).wait_recv()
        cp.wait_send()

    return pl.pallas_call(
        body, name=name, out_shape=jax.ShapeDtypeStruct(v.shape, v.dtype),
        in_specs=[HBM], out_specs=HBM, input_output_aliases={0: 0},
        scratch_shapes=[pltpu.SemaphoreType.DMA, pltpu.SemaphoreType.DMA],
    )(v)


def _allreduce_small(v, name):
    r, cols = v.shape

    def body(v_ref, out_ref, buf_ref, send_sems, recv_sems):
        x, y, c = _coords()
        me = 4 * x + 2 * y + c
        buf_ref[me] = v_ref[...]
        sends = []
        for k in range(1, N_DEV):
            px = 1 - x if k & 4 else x
            py = 1 - y if k & 2 else y
            pc = 1 - c if k & 1 else c
            sends.append(pltpu.make_async_remote_copy(
                src_ref=v_ref, dst_ref=buf_ref.at[me], send_sem=send_sems.at[k - 1], recv_sem=recv_sems.at[k - 1],
                device_id=(px, py, pc), device_id_type=MESH))
        for cp in sends:
            cp.start()
        for cp in sends:
            cp.wait()
        acc = buf_ref[0]
        for d in range(1, N_DEV):
            acc = acc + buf_ref[d]
        out_ref[...] = acc

    return pl.pallas_call(
        body, name=name, out_shape=jax.ShapeDtypeStruct((r, cols), F32),
        in_specs=[pl.BlockSpec(memory_space=pltpu.VMEM)], out_specs=pl.BlockSpec(memory_space=pltpu.VMEM),
        scratch_shapes=[pltpu.VMEM((N_DEV, r, cols), F32), pltpu.SemaphoreType.DMA((N_DEV - 1,)),
                        pltpu.SemaphoreType.DMA((N_DEV - 1,))],
    )(v)


def _add_sibling(g, from_sibling, core, name):
    n, h, cols = from_sibling.shape
    tr = _row_tile(h)
    steps = h // tr

    def body(core_ref, a_ref, b_ref, o_ref):
        o_ref[...] = (a_ref[...] + b_ref[...]).astype(o_ref.dtype)

    return pl.pallas_call(
        body, name=name, out_shape=jax.ShapeDtypeStruct(from_sibling.shape, jnp.bfloat16),
        grid_spec=pltpu.PrefetchScalarGridSpec(
            num_scalar_prefetch=1, grid=(n, steps),
            in_specs=[pl.BlockSpec((1, tr, cols), lambda s, i, core_ref: (s, core_ref[0] * steps + i, 0)),
                      pl.BlockSpec((1, tr, cols), lambda s, i, core_ref: (s, i, 0))],
            out_specs=pl.BlockSpec((1, tr, cols), lambda s, i, core_ref: (s, i, 0))),
        compiler_params=_params("parallel", "parallel"),
    )(core.reshape(1).astype(jnp.int32), g, from_sibling)


def _sum_slots(p, own, chip, core, name):
    n, r, cols = p.shape
    tr = _row_tile(r)
    steps = r // tr

    def body(core_ref, chip_ref, p_ref, own_ref, o_ref):
        parts = [jnp.where(chip_ref[0] == s, own_ref[0], p_ref[s]).astype(F32) for s in range(n)]
        o_ref[...] = ((parts[0] + parts[1]) + parts[2]) + parts[3]

    return pl.pallas_call(
        body, name=name, out_shape=jax.ShapeDtypeStruct((2 * r, cols), F32),
        grid_spec=pltpu.PrefetchScalarGridSpec(
            num_scalar_prefetch=2, grid=(steps,),
            in_specs=[pl.BlockSpec((n, tr, cols), lambda i, core_ref, chip_ref: (0, i, 0)),
                      pl.BlockSpec((1, tr, cols), lambda i, core_ref, chip_ref: (chip_ref[0], i, 0))],
            out_specs=pl.BlockSpec((tr, cols), lambda i, core_ref, chip_ref: (core_ref[0] * steps + i, 0))),
        compiler_params=_params("parallel"),
    )(core.reshape(1).astype(jnp.int32), chip.reshape(1).astype(jnp.int32), p, own)


PACK_COLS = 1024


def _pack_shards(parts):
    return jnp.concatenate([p.reshape(-1, PACK_COLS) for p in parts], axis=0)


def _unpack_shards(buf, shapes):
    out, row = [], 0
    for shp in shapes:
        nrows = math.prod(shp) // PACK_COLS
        out.append(buf[..., row:row + nrows, :].reshape(buf.shape[:-2] + tuple(shp)))
        row += nrows
    return out


def _local_step(x, target, w):
    t = lambda a: a.T
    g = {}
    w_in_g, w_in_x = w["a_w_in"][:, :D_RNN], w["a_w_in"][:, D_RNN:]
    h0, gate_br, x_br = _norm_and_project(x, w["norm_mix_g"][0], w_in_g, w_in_x, "rglru_in")
    y_a, hs = _rglru_fwd(gate_br, x_br, w["a_conv_w"], w["a_conv_b"], w["a_w_r"], w["a_w_i"], w["a_b_r"],
                         w["a_b_i"], w["a_lambda"], "rglru_fwd")
    x1, h1 = _matmul([(y_a, w["a_w_out"])], F32, "mm_a_out", addend=x, norm_gain=w["norm_ffn_g"][0])
    fg0, fu0, act0 = _ffn_up(h1, w["ffn_w_gate"][0], w["ffn_w_up"][0], "ffn0_up")
    x2, h2 = _matmul([(act0, w["ffn_w_down"][0])], F32, "mm_f0_down", addend=x1, norm_gain=w["norm_mix_g"][1],
                     tk=D_FF)
    qkv = _matmul([(h2, w["b_w_qkv"])], CD, "mm_b_qkv", out_lbm=True, tn=3 * D_MODEL)
    o = _attn_fwd(qkv, "attn_fwd")
    x3, h3 = _matmul([(o, w["b_w_out"])], F32, "mm_b_out", a_lbm=True, addend=x2, norm_gain=w["norm_ffn_g"][1])
    fg1, fu1, act1 = _ffn_up(h3, w["ffn_w_gate"][1], w["ffn_w_up"][1], "ffn1_up")
    dx4, dx4c, g["final_g"], loss = _matmul([(act1, w["ffn_w_down"][1])], F32, "mm_f1_down", addend=x3,
                                            loss_head=(w["final_g"], target), tk=D_FF)

    def ffn_bwd(dx_out, dxc, h, x_in, fg, fu, act, layer, tag):
        dg, du = _ffn_dact(dxc, t(w["ffn_w_down"][layer]), fg, fu, "ffn_" + tag + "_dact")
        dwd = _matmul([(act, dxc)], F32, "mm_" + tag + "_dwd", trans_a=True)
        dwg = _matmul([(h, dg)], F32, "mm_" + tag + "_dwg", trans_a=True)
        dwu = _matmul([(h, du)], F32, "mm_" + tag + "_dwu", trans_a=True)
        dx_in, dx_in_c, dgain = _matmul([(dg, t(w["ffn_w_gate"][layer])), (du, t(w["ffn_w_up"][layer]))], F32,
                                        "mm_" + tag + "_dh", norm_bwd=(x_in, w["norm_ffn_g"][layer], dx_out),
                                        tk=D_FF, tm=256)
        return dx_in, dx_in_c, dgain, dwg, dwu, dwd

    dx3, dx3c, dgf1, dwg1, dwu1, dwd1 = ffn_bwd(dx4, dx4c, h3, x3, fg1, fu1, act1, 1, "f1")
    do = _matmul([(dx3c, t(w["b_w_out"]))], CD, "mm_b_do", out_lbm=True, tn=1024)
    g["b_w_out"] = _matmul([(o, dx3c)], F32, "mm_b_dwout", trans_a=True, a_lbm=True)
    dq, dk, dv = _attn_bwd(qkv, o, do, "attn_bwd")
    wq_t = t(w["b_w_qkv"])
    parts = (dq, dk, dv)
    g["b_w_qkv"] = jnp.concatenate(
        [_matmul([(h2, p)], F32, "mm_b_dwqkv%d" % n, trans_a=True, b_lbm=True) for n, p in enumerate(parts)], axis=1)
    dx2, dx2c, dgm1 = _matmul([(p, wq_t[n * D_MODEL:(n + 1) * D_MODEL]) for n, p in enumerate(parts)], F32, "mm_b_dh",
                              a_lbm=True, norm_bwd=(x2, w["norm_mix_g"][1], dx3))
    dx1, dx1c, dgf0, dwg0, dwu0, dwd0 = ffn_bwd(dx2, dx2c, h1, x1, fg0, fu0, act0, 0, "f0")
    dy_a = _matmul([(dx1c, t(w["a_w_out"]))], F32, "mm_a_dy")
    g["a_w_out"] = _matmul([(y_a, dx1c)], F32, "mm_a_dwout", trans_a=True)
    wrt = jnp.swapaxes(w["a_w_r"], 1, 2)
    wit = jnp.swapaxes(w["a_w_i"], 1, 2)
    (dgate, dxbr, g["a_conv_w"], g["a_conv_b"], g["a_b_r"], g["a_b_i"], g["a_lambda"], g["a_w_r"],
     g["a_w_i"]) = _rglru_bwd(dy_a, gate_br, x_br, hs, w["a_conv_w"], w["a_conv_b"], w["a_w_r"], w["a_w_i"], wrt, wit,
                              w["a_b_r"], w["a_b_i"], w["a_lambda"], "rglru_bwd")
    g["a_w_in"] = jnp.concatenate([_matmul([(h0, dgate)], F32, "mm_a_dwin_g", trans_a=True),
                                   _matmul([(h0, dxbr)], F32, "mm_a_dwin_x", trans_a=True)], axis=1)
    dx0, _, dgm0 = _matmul([(dgate, t(w_in_g)), (dxbr, t(w_in_x))], F32, "mm_a_dh",
                           norm_bwd=(x, w["norm_mix_g"][0], dx1))
    g["norm_mix_g"] = jnp.concatenate([dgm0, dgm1], axis=0)
    g["norm_ffn_g"] = jnp.concatenate([dgf0, dgf1], axis=0)
    g["ffn_w_gate"] = [dwg0, dwg1]
    g["ffn_w_up"] = [dwu0, dwu1]
    g["ffn_w_down"] = [dwd0, dwd1]
    return loss, dx0, g


WEIGHTS = ["norm_mix_g", "norm_ffn_g", "a_w_in", "a_conv_w", "a_conv_b", "a_w_r", "a_b_r", "a_w_i", "a_b_i",
           "a_lambda", "a_w_out", "b_w_qkv", "b_w_out", "ffn_w_gate", "ffn_w_up", "ffn_w_down", "final_g"]
BIG = [("a_w_in", 2), ("a_w_r", 2), ("a_w_i", 2), ("a_w_out", 1), ("b_w_qkv", 2), ("b_w_out", 1),
       ("ffn_w_gate", 2), ("ffn_w_up", 2), ("ffn_w_down", 1)]
LAYER1 = ["b_w_qkv", "b_w_out", "ffn_w_gate", "ffn_w_up", "ffn_w_down"]
LAYER0 = ["a_w_in", "a_w_r", "a_w_i", "a_w_out", "ffn_w_gate", "ffn_w_up", "ffn_w_down"]
RS_COLLECTIVE_IDS = {"chips1": 3, "chips0": 4, "sibling1": 5, "share1": 6}
GATHER_COLLECTIVE_IDS = (8, 7)
SMALL = ["norm_mix_g", "norm_ffn_g", "a_conv_w", "a_conv_b", "a_b_r", "a_b_i", "a_lambda", "final_g"]


def _split_chips(full, axis):
    if axis == 1:
        return full.reshape((N_CHIPS, 1, full.shape[1] // N_CHIPS) + full.shape[2:])
    return jnp.stack(jnp.split(full, N_CHIPS, axis=axis))


def _step(x, target, weights, moments_m, moments_v):
    chip = 2 * lax.axis_index("x") + lax.axis_index("y")
    core = lax.axis_index("c")
    axis_of = dict(BIG)
    full = {}
    for group, layer, tag, collective_id in ((LAYER0[:4], 0, "0a", None), (LAYER0[4:], 0, "0f", GATHER_COLLECTIVE_IDS[0]),
                                             (LAYER1, 1, "1", GATHER_COLLECTIVE_IDS[1])):
        shards = [weights[n][layer % weights[n].shape[0]].astype(CD) for n in group]
        packed = _pack_shards(shards)
        if collective_id is not None:
            packed, first_gathered = lax.optimization_barrier((packed, first_gathered))
        gathered = _allgather_chips(packed, "allgather_weights" + tag, collective_id)
        if collective_id is None:
            first_gathered = gathered
        for n, own, stack in zip(group, shards, _unpack_shards(gathered, [sh.shape for sh in shards])):
            joined = jnp.concatenate([jnp.where(chip == s, own, stack[s]) for s in range(N_CHIPS)],
                                     axis=axis_of[n] - 1)
            full.setdefault(n, {})[layer] = joined
    full = {n: (v[0] if n.startswith("a_") else v[1] if n.startswith("b_") else [v[0], v[1]]) for n, v in full.items()}
    cw_rows = jnp.zeros((N_CHIPS, CONV_W, RG_BW), F32)
    cw_rows = lax.dynamic_update_slice(cw_rows, jnp.where(core == 0, weights["a_conv_w"], 0.0), (chip, 0, 0))
    cw_all = _allreduce_small(cw_rows.reshape(-1, LANES), "allgather_conv_w").reshape(N_CHIPS, CONV_W, RG_BW)
    full["a_conv_w"] = jnp.concatenate([cw_all[s] for s in range(N_CHIPS)], axis=1)
    for n in ("norm_mix_g", "norm_ffn_g", "final_g"):
        full[n] = weights[n]
    for n in ("a_conv_b", "a_b_r", "a_b_i", "a_lambda"):
        full[n] = weights[n]
    loss, dx, grads = _local_step(x[0], target[0], full)
    small_parts = [grads[n].reshape(-1) for n in SMALL] + [loss.reshape(-1)]
    sizes = [p.shape[0] for p in small_parts]
    small = _allreduce_small(jnp.concatenate(small_parts).reshape(-1, LANES), "allreduce_small").reshape(-1)
    red, pos = {}, 0
    for n, sz in zip(SMALL + ["loss"], sizes):
        red[n] = small[pos:pos + sz]
        pos += sz
    loss_out = red["loss"][0]
    g_out = {}
    for n in SMALL:
        if n == "a_conv_w":
            g_out[n] = lax.dynamic_slice(red[n].reshape(CONV_W, D_RNN), (0, chip * RG_BW), (CONV_W, RG_BW)).reshape(
                weights[n].shape)
        else:
            g_out[n] = red[n].reshape(weights[n].shape)
    axis_of = dict(BIG)
    pieces = {}
    for group, layer, tag in ((LAYER1, 1, "1"), (LAYER0, 0, "0")):
        stacks, shapes = [], []
        for n in group:
            per_layer = isinstance(grads[n], list)
            gfull = grads[n][layer] if per_layer else grads[n]
            shard_shape = weights[n].shape[1:]
            gfull = gfull.reshape((1,) + gfull.shape)
            stacks.append(_split_chips(gfull, axis_of[n]).reshape(N_CHIPS, -1, PACK_COLS))
            shapes.append((1,) + tuple(shard_shape))
        gbuf = jnp.concatenate(stacks, axis=1)
        if layer == 1:
            from_sibling = _exchange_sibling_halves_async(gbuf, "rs_sibling" + tag, RS_COLLECTIVE_IDS["sibling1"])
        else:
            from_sibling = _exchange_sibling_halves(gbuf, "rs_sibling" + tag)
        chip_partial = _add_sibling(gbuf, from_sibling, core, "rs_add" + tag)
        from_chips = _scatter_to_chips_async(chip_partial, "rs_chips" + tag, RS_COLLECTIVE_IDS["chips" + tag])
        halves = _sum_slots(from_chips, chip_partial, chip, core, "rs_sum" + tag)
        if layer == 1:
            reduced = _share_halves_async(halves, "rs_share" + tag, RS_COLLECTIVE_IDS["share1"])
        else:
            reduced = _share_halves(halves, "rs_share" + tag)
        for n, piece in zip(group, _unpack_shards(reduced, shapes)):
            pieces.setdefault(n, {})[layer] = piece
    for n, _ in BIG:
        layers = pieces[n]
        g_out[n] = jnp.concatenate([layers[k] for k in sorted(layers)], axis=0)
    updates = {}
    for n, _ in BIG:
        updates[n] = _adamw(weights[n], g_out[n], moments_m[n], moments_v[n], "adamw_" + n)
    rows = lambda d: jnp.concatenate([d[n].reshape(-1, D_MODEL) for n in SMALL], axis=0)
    small_updates = _adamw(rows(weights), rows(g_out), rows(moments_m), rows(moments_v), "adamw_small")
    pos = 0
    for n in SMALL:
        nrows = weights[n].size // D_MODEL
        updates[n] = tuple(u[pos:pos + nrows].reshape(weights[n].shape) for u in small_updates)
        pos += nrows
    outs_g = [g_out[n] for n in WEIGHTS]
    outs_d, outs_m, outs_v = ([updates[n][k] for n in WEIGHTS] for k in range(3))
    return (loss_out, dx[None], *outs_g, *outs_d, *outs_m, *outs_v)


def kernel(x, norm_mix_g, norm_ffn_g, a_w_in, a_conv_w, a_conv_b, a_w_r, a_b_r, a_w_i, a_b_i, a_lambda, a_w_out, b_w_qkv, b_w_out, ffn_w_gate, ffn_w_up, ffn_w_down, final_g, loss_target, m_norm_mix_g, m_norm_ffn_g, m_a_w_in, m_a_conv_w, m_a_conv_b, m_a_w_r, m_a_b_r, m_a_w_i, m_a_b_i, m_a_lambda, m_a_w_out, m_b_w_qkv, m_b_w_out, m_ffn_w_gate, m_ffn_w_up, m_ffn_w_down, m_final_g, v_norm_mix_g, v_norm_ffn_g, v_a_w_in, v_a_conv_w, v_a_conv_b, v_a_w_r, v_a_b_r, v_a_w_i, v_a_b_i, v_a_lambda, v_a_w_out, v_b_w_qkv, v_b_w_out, v_ffn_w_gate, v_ffn_w_up, v_ffn_w_down, v_final_g):
    ws = [norm_mix_g, norm_ffn_g, a_w_in, a_conv_w, a_conv_b, a_w_r, a_b_r, a_w_i, a_b_i, a_lambda, a_w_out, b_w_qkv,
          b_w_out, ffn_w_gate, ffn_w_up, ffn_w_down, final_g]
    ms = [m_norm_mix_g, m_norm_ffn_g, m_a_w_in, m_a_conv_w, m_a_conv_b, m_a_w_r, m_a_b_r, m_a_w_i, m_a_b_i, m_a_lambda,
          m_a_w_out, m_b_w_qkv, m_b_w_out, m_ffn_w_gate, m_ffn_w_up, m_ffn_w_down, m_final_g]
    vs = [v_norm_mix_g, v_norm_ffn_g, v_a_w_in, v_a_conv_w, v_a_conv_b, v_a_w_r, v_a_b_r, v_a_w_i, v_a_b_i, v_a_lambda,
          v_a_w_out, v_b_w_qkv, v_b_w_out, v_ffn_w_gate, v_ffn_w_up, v_ffn_w_down, v_final_g]
    return _step(x, loss_target, dict(zip(WEIGHTS, ws)), dict(zip(WEIGHTS, ms)), dict(zip(WEIGHTS, vs)))
```

```python
import math

import jax
import jax.numpy as jnp
from jax import lax
from jax.experimental import pallas as pl
from jax.experimental.pallas import tpu as pltpu
from jax.experimental.pallas import tpu_sc as plsc

F32 = jnp.float32
CD = jnp.bfloat16

D_MODEL = 1024
D_RNN = 1024
RG_BLOCKS = 4
RG_BW = 256
CONV_W = 4
RG_C = 8.0
SB_HEADS = 16
SB_HEAD_DIM = 64
D_FF = 2816
RMS_EPS = 1e-6
N_CHIPS = 4
N_DEV = 8

ADAM_LR = 0.001
ADAM_B1 = 0.9
ADAM_B2 = 0.999
ADAM_EPS = 1e-08
ADAM_WD = 0.01
ADAM_STEP = 10

LANES = 128
VMEM_LIMIT = 56 * 1024 * 1024
MESH = pl.DeviceIdType.MESH


def _params(*sem):
    return pltpu.CompilerParams(dimension_semantics=sem, vmem_limit_bytes=VMEM_LIMIT)


def _pick(n, prefs):
    for p in prefs:
        if n % p == 0:
            return p
    return n


def _row_tile(rows):
    return max(d for d in range(16, 1025, 16) if rows % d == 0)


def _matmul(pairs, out_dtype, name, *, trans_a=False, a_lbm=False, b_lbm=False, out_lbm=False, addend=None,
            tm=512, tn=None, tk=None, norm_gain=None, norm_bwd=None, loss_head=None):
    a0, b0 = pairs[0]
    if trans_a:
        kdim = a0.shape[1] if a_lbm else a0.shape[0]
        m = a0.shape[0] * LANES if a_lbm else a0.shape[1]
    else:
        m = a0.shape[1] if a_lbm else a0.shape[0]
        kdim = a0.shape[0] * LANES if a_lbm else a0.shape[1]
    n = b0.shape[0] * LANES if b_lbm else b0.shape[1]
    if trans_a and m <= 1024:
        tm = m
    tm = _pick(m, (tm, 1408, 256, 128))
    tn = tn or _pick(n, (1408, 1024, 768, 512, 256, 128))
    tk = tk or _pick(kdim, (1024, 1408, 512, 256, 128))
    nk = kdim // tk
    npair = len(pairs)

    def cat(ref):
        return jnp.concatenate([ref[p] for p in range(ref.shape[0])], axis=-1)

    def body(*refs):
        ins = refs[: 2 * npair]
        pos = 2 * npair
        add_ref = None
        if addend is not None:
            add_ref = refs[pos]
            pos += 1
        gain_ref = x_ref = dxin_ref = None
        if norm_gain is not None:
            gain_ref = refs[pos]
            pos += 1
        if norm_bwd is not None:
            x_ref, gain_ref, dxin_ref = refs[pos:pos + 3]
            pos += 3
        if loss_head is not None:
            gain_ref, target_ref = refs[pos:pos + 2]
            pos += 2
        o_ref = refs[pos]
        extra_out = refs[pos + 1:-1]
        acc_ref = refs[-1]
        k = pl.program_id(2)

        @pl.when(k == 0)
        def _():
            acc_ref[...] = jnp.zeros_like(acc_ref)

        if norm_bwd is not None or loss_head is not None:
            @pl.when((k == 0) & (pl.program_id(0) == 0))
            def _():
                for ref in extra_out[1:]:
                    ref[...] = jnp.zeros_like(ref)

        acc = acc_ref[...]
        for p in range(npair):
            a = (cat(ins[2 * p]) if a_lbm else ins[2 * p][...]).astype(CD)
            b = (cat(ins[2 * p + 1]) if b_lbm else ins[2 * p + 1][...]).astype(CD)
            dims = (((0,), (0,)), ((), ())) if trans_a else (((1,), (0,)), ((), ()))
            acc = acc + lax.dot_general(a, b, dims, preferred_element_type=F32)
        acc_ref[...] = acc

        @pl.when(k == nk - 1)
        def _():
            res = acc_ref[...]
            if add_ref is not None:
                res = res + add_ref[...]
            if norm_gain is not None:
                rinv = lax.rsqrt(jnp.mean(res * res, axis=-1, keepdims=True) + RMS_EPS)
                extra_out[0][...] = (res * rinv * gain_ref[...]).astype(CD)
            if norm_bwd is not None:
                xv = x_ref[...]
                rinv = lax.rsqrt(jnp.mean(xv * xv, axis=-1, keepdims=True) + RMS_EPS)
                nrm = xv * rinv
                dn = res * gain_ref[...]
                extra_out[1][...] += jnp.sum(res * nrm, axis=0, keepdims=True)
                res = dxin_ref[...] + rinv * (dn - nrm * jnp.mean(dn * nrm, axis=-1, keepdims=True))
                extra_out[0][...] = res.astype(CD)
            if loss_head is not None:
                gv = gain_ref[...]
                rinv = lax.rsqrt(jnp.mean(res * res, axis=-1, keepdims=True) + RMS_EPS)
                nrm = res * rinv
                err = nrm * gv - target_ref[...]
                extra_out[2][...] += 0.5 * jnp.sum(jnp.mean(err * err, axis=-1, keepdims=True), axis=0, keepdims=True)
                dy = err * (1.0 / n)
                dn = dy * gv
                extra_out[1][...] += jnp.sum(dy * nrm, axis=0, keepdims=True)
                res = rinv * (dn - nrm * jnp.mean(dn * nrm, axis=-1, keepdims=True))
                extra_out[0][...] = res.astype(CD)
            res = res.astype(out_dtype)
            if out_lbm:
                for p in range(tn // LANES):
                    o_ref[p] = res[:, p * LANES:(p + 1) * LANES]
            else:
                o_ref[...] = res

    if trans_a:
        a_spec = (pl.BlockSpec((tm // LANES, tk, LANES), lambda i, j, k: (i, k, 0)) if a_lbm
                  else pl.BlockSpec((tk, tm), lambda i, j, k: (k, i)))
    else:
        a_spec = (pl.BlockSpec((tk // LANES, tm, LANES), lambda i, j, k: (k, i, 0)) if a_lbm
                  else pl.BlockSpec((tm, tk), lambda i, j, k: (i, k)))
    b_spec = (pl.BlockSpec((tn // LANES, tk, LANES), lambda i, j, k: (j, k, 0)) if b_lbm
              else pl.BlockSpec((tk, tn), lambda i, j, k: (k, j)))
    in_specs = []
    args = []
    for a, b in pairs:
        in_specs += [a_spec, b_spec]
        args += [a, b]
    if addend is not None:
        in_specs.append(pl.BlockSpec((tm, tn), lambda i, j, k: (i, j)))
        args.append(addend)
    tile = pl.BlockSpec((tm, tn), lambda i, j, k: (i, j))
    vec = pl.BlockSpec((1, tn), lambda i, j, k: (0, j))
    if out_lbm:
        out_shape = jax.ShapeDtypeStruct((n // LANES, m, LANES), out_dtype)
        out_spec = pl.BlockSpec((tn // LANES, tm, LANES), lambda i, j, k: (j, i, 0))
    else:
        out_shape = jax.ShapeDtypeStruct((m, n), out_dtype)
        out_spec = tile
    sem = ("parallel", "parallel", "arbitrary")
    if norm_gain is not None or norm_bwd is not None or loss_head is not None:
        assert tn == n and not out_lbm, "the norm needs whole rows in one tile"
        out_shape, out_spec = [out_shape, jax.ShapeDtypeStruct((m, n), CD)], [out_spec, tile]
    if norm_gain is not None:
        in_specs.append(vec)
        args.append(norm_gain.reshape(1, n))
    if norm_bwd is not None:
        x_in, gain, dx_in = norm_bwd
        in_specs += [tile, vec, tile]
        args += [x_in, gain.reshape(1, n), dx_in]
        out_shape.append(jax.ShapeDtypeStruct((1, n), F32))
        out_spec.append(vec)
        sem = ("arbitrary", "arbitrary", "arbitrary")
    if loss_head is not None:
        gain, target = loss_head
        in_specs += [vec, tile]
        args += [gain.reshape(1, n), target]
        out_shape += [jax.ShapeDtypeStruct((1, n), F32), jax.ShapeDtypeStruct((1, LANES), F32)]
        out_spec += [vec, pl.BlockSpec((1, LANES), lambda i, j, k: (0, 0))]
        sem = ("arbitrary", "arbitrary", "arbitrary")
    return pl.pallas_call(
        body, name=name, out_shape=out_shape, grid=(m // tm, n // tn, nk),
        in_specs=in_specs, out_specs=out_spec,
        scratch_shapes=[pltpu.VMEM((tm, tn), F32)],
        compiler_params=_params(*sem),
    )(*args)


def _norm_and_project(x, g, w_a, w_b, name):
    s, d = x.shape
    n = w_a.shape[1]
    tm = _pick(s, (512, 256))

    def body(x_ref, g_ref, wa_ref, wb_ref, h_ref, a_ref, b_ref):
        xv = x_ref[...]
        rinv = lax.rsqrt(jnp.mean(xv * xv, axis=-1, keepdims=True) + RMS_EPS)
        h = (xv * rinv * g_ref[...]).astype(CD)
        h_ref[...] = h
        a_ref[...] = jnp.dot(h, wa_ref[...], preferred_element_type=F32)
        b_ref[...] = jnp.dot(h, wb_ref[...], preferred_element_type=F32)

    row = pl.BlockSpec((tm, d), lambda i: (i, 0))
    out = pl.BlockSpec((tm, n), lambda i: (i, 0))
    wspec = pl.BlockSpec((d, n), lambda i: (0, 0))
    return pl.pallas_call(
        body, name=name,
        out_shape=(jax.ShapeDtypeStruct((s, d), CD), jax.ShapeDtypeStruct((s, n), F32),
                   jax.ShapeDtypeStruct((s, n), F32)),
        grid=(s // tm,), in_specs=[row, pl.BlockSpec((1, d), lambda i: (0, 0)), wspec, wspec],
        out_specs=(row, out, out), compiler_params=_params("parallel"),
    )(x, g.reshape(1, d), w_a, w_b)


def _sigmoid(z):
    return 1.0 / (1.0 + jnp.exp(-z))


FFN_TM = 512
FFN_TN = 1408


def _ffn_up(h, wg, wu, name):
    s, d = h.shape
    f = wg.shape[1]
    tm = _pick(s, (FFN_TM, 256))

    def body(h_ref, wg_ref, wu_ref, g_ref, u_ref, a_ref):
        hv = h_ref[...]
        gv = jnp.dot(hv, wg_ref[...], preferred_element_type=F32)
        uv = jnp.dot(hv, wu_ref[...], preferred_element_type=F32)
        g_ref[...] = gv
        u_ref[...] = uv
        a_ref[...] = (gv * _sigmoid(gv) * uv).astype(CD)

    a_spec = pl.BlockSpec((tm, d), lambda j, i: (i, 0))
    w_spec = pl.BlockSpec((d, FFN_TN), lambda j, i: (0, j))
    o_spec = pl.BlockSpec((tm, FFN_TN), lambda j, i: (i, j))
    return pl.pallas_call(
        body, name=name,
        out_shape=(jax.ShapeDtypeStruct((s, f), F32), jax.ShapeDtypeStruct((s, f), F32),
                   jax.ShapeDtypeStruct((s, f), CD)),
        grid=(f // FFN_TN, s // tm), in_specs=[a_spec, w_spec, w_spec], out_specs=(o_spec, o_spec, o_spec),
        compiler_params=_params("parallel", "parallel"),
    )(h, wg, wu)


def _ffn_dact(dxc, wd_t, g, u, name):
    s, d = dxc.shape
    f = wd_t.shape[1]
    tm = _pick(s, (FFN_TM, 256))

    def body(dx_ref, w_ref, g_ref, u_ref, dg_ref, du_ref):
        da = jnp.dot(dx_ref[...], w_ref[...], preferred_element_type=F32)
        gv = g_ref[...]
        sg = _sigmoid(gv)
        silu = gv * sg
        dg_ref[...] = (da * u_ref[...] * (sg + silu * (1.0 - sg))).astype(CD)
        du_ref[...] = (da * silu).astype(CD)

    a_spec = pl.BlockSpec((tm, d), lambda j, i: (i, 0))
    w_spec = pl.BlockSpec((d, FFN_TN), lambda j, i: (0, j))
    o_spec = pl.BlockSpec((tm, FFN_TN), lambda j, i: (i, j))
    return pl.pallas_call(
        body, name=name,
        out_shape=(jax.ShapeDtypeStruct((s, f), CD), jax.ShapeDtypeStruct((s, f), CD)),
        grid=(f // FFN_TN, s // tm), in_specs=[a_spec, w_spec, o_spec, o_spec], out_specs=(o_spec, o_spec),
        compiler_params=_params("parallel", "parallel"),
    )(dxc, wd_t, g, u)


TIME_BLOCK = 1024
SUBLANES = 8
GELU_C = math.sqrt(2.0 / math.pi)
GELU_A = 0.044715


def _gelu(x):
    return 0.5 * x * (1.0 + jnp.tanh(GELU_C * (x + GELU_A * x * x * x)))


def _gelu_grad(x):
    t = jnp.tanh(GELU_C * (x + GELU_A * x * x * x))
    return 0.5 * (1.0 + t) + 0.5 * x * (1.0 - t * t) * GELU_C * (1.0 + 3.0 * GELU_A * x * x)


def _neg_expm1(x):
    series = -x * (1.0 + x * (0.5 + x * (1.0 / 6.0 + x * (1.0 / 24.0))))
    return jnp.where(x > -0.05, series, 1.0 - jnp.exp(x))


def _log_sigmoid(x):
    return jnp.minimum(x, 0.0) - jnp.log1p(jnp.exp(-jnp.abs(x)))


def _shift_down(x, tail, s):
    if s == 0:
        return x
    ext = jnp.concatenate([tail, x], axis=0)
    return pltpu.roll(ext, s, axis=0)[SUBLANES:]


def _shift_up(x, head, s):
    if s == 0:
        return x
    n = x.shape[0]
    ext = jnp.concatenate([x, head], axis=0)
    return pltpu.roll(ext, n + SUBLANES - s, axis=0)[:n]


def _rg_gates(xbr, tail, cw_ref, cb, wr, wi, br, bi, ls):
    taps = [_shift_down(xbr, tail, CONV_W - 1 - k) for k in range(CONV_W)]
    xc = cb
    for k in range(CONV_W):
        xc = xc + cw_ref[pl.ds(k, 1), :] * taps[k]
    xcd = xc.astype(CD)
    r = _sigmoid(jnp.dot(xcd, wr, preferred_element_type=F32) + br)
    i = _sigmoid(jnp.dot(xcd, wi, preferred_element_type=F32) + bi)
    log_a = RG_C * r * ls
    a = jnp.exp(log_a)
    mult = jnp.sqrt(jnp.maximum(_neg_expm1(2.0 * log_a), 0.0))
    return taps, xc, r, i, log_a, a, mult


def _scan8_fwd(a, u):
    row = lax.broadcasted_iota(jnp.int32, a.shape, 0)
    for d in (1, 2, 4):
        a_s = pltpu.roll(a, d, axis=0)
        u_s = pltpu.roll(u, d, axis=0)
        m = row >= d
        u = jnp.where(m, a * u_s + u, u)
        a = jnp.where(m, a * a_s, a)
    return a, u


def _scan8_bwd(b, u):
    row = lax.broadcasted_iota(jnp.int32, b.shape, 0)
    for d in (1, 2, 4):
        b_s = pltpu.roll(b, SUBLANES - d, axis=0)
        u_s = pltpu.roll(u, SUBLANES - d, axis=0)
        m = row < SUBLANES - d
        u = jnp.where(m, b * u_s + u, u)
        b = jnp.where(m, b * b_s, b)
    return b, u


def _rglru_fwd(gate_br, x_br, cw, cb, wr, wi, br, bi, lam, name):
    s, c = x_br.shape
    nt = s // TIME_BLOCK
    tb, cbw = TIME_BLOCK, RG_BW
    groups = tb // SUBLANES

    def body(g_ref, x_ref, tail_ref, cw_ref, cb_ref, wr_ref, wi_ref, br_ref, bi_ref, lam_ref,
             y_ref, hs_ref, carry_ref, a_scr, u_scr):
        t = pl.program_id(1)

        @pl.when(t == 0)
        def _():
            carry_ref[...] = jnp.zeros_like(carry_ref)

        tail = jnp.where(t > 0, tail_ref[...], 0.0)
        ls = _log_sigmoid(lam_ref[...])
        _, xc, _, i, _, a, mult = _rg_gates(x_ref[...], tail, cw_ref, cb_ref[...], wr_ref[0], wi_ref[0],
                                            br_ref[...], bi_ref[...], ls)
        a_scr[...] = a
        u_scr[...] = mult * (i * xc)
        carry = carry_ref[...]
        for gi in range(groups):
            rows = pl.ds(gi * SUBLANES, SUBLANES)
            pa, hl = _scan8_fwd(a_scr[rows, :], u_scr[rows, :])
            hs_ref[rows, :] = hl + pa * carry
            carry = hs_ref[pl.ds(gi * SUBLANES + SUBLANES - 1, 1), :]
        carry_ref[...] = carry
        y_ref[...] = (hs_ref[...] * _gelu(g_ref[...])).astype(CD)

    blk = pl.BlockSpec((tb, cbw), lambda n, t: (t, n))
    tail = pl.BlockSpec((SUBLANES, cbw), lambda n, t: (jnp.maximum(t * groups - 1, 0), n))
    vec = pl.BlockSpec((1, cbw), lambda n, t: (0, n))
    wblk = pl.BlockSpec((1, cbw, cbw), lambda n, t: (n, 0, 0))
    return pl.pallas_call(
        body, name=name,
        out_shape=(jax.ShapeDtypeStruct((s, c), CD), jax.ShapeDtypeStruct((s, c), F32)),
        grid=(RG_BLOCKS, nt),
        in_specs=[blk, blk, tail, pl.BlockSpec((CONV_W, cbw), lambda n, t: (0, n)), vec, wblk, wblk, vec, vec, vec],
        out_specs=(blk, blk),
        scratch_shapes=[pltpu.VMEM((1, cbw), F32), pltpu.VMEM((tb, cbw), F32), pltpu.VMEM((tb, cbw), F32)],
        compiler_params=_params("parallel", "arbitrary"),
    )(gate_br, x_br, x_br, cw, cb, wr, wi, br, bi, lam)


def _rglru_bwd(dy, gate_br, x_br, hs, cw, cb, wr, wi, wrt, wit, br, bi, lam, name):
    s, c = x_br.shape
    nt = s // TIME_BLOCK
    tb, cbw = TIME_BLOCK, RG_BW
    groups = tb // SUBLANES

    def body(dy_ref, g_ref, x_ref, tail_ref, hs_ref, hprev_ref, cw_ref, cb_ref, wr_ref, wi_ref, wrt_ref, wit_ref,
             br_ref, bi_ref, lam_ref,
             dg_ref, dx_ref, dcw_ref, dcb_ref, dbr_ref, dbi_ref, dlam_ref, dwr_ref, dwi_ref,
             carry_ref, head_ref, b_scr, u_scr, dh_scr):
        tr = pl.program_id(1)
        first_block = tr == nt - 1

        @pl.when(tr == 0)
        def _():
            carry_ref[...] = jnp.zeros_like(carry_ref)
            head_ref[...] = jnp.zeros_like(head_ref)
            for ref in (dcw_ref, dcb_ref, dbr_ref, dbi_ref, dlam_ref, dwr_ref, dwi_ref):
                ref[...] = jnp.zeros_like(ref)

        tail = jnp.where(first_block, 0.0, tail_ref[...])
        lam_v = lam_ref[...]
        ls = _log_sigmoid(lam_v)
        taps, xc, r, i, log_a, a, mult = _rg_gates(x_ref[...], tail, cw_ref, cb_ref[...], wr_ref[0], wi_ref[0],
                                                   br_ref[...], bi_ref[...], ls)
        gate_v = g_ref[...]
        dyv = dy_ref[...]
        hsv = hs_ref[...]
        dg_ref[...] = (dyv * hsv * _gelu_grad(gate_v)).astype(CD)

        row = lax.broadcasted_iota(jnp.int32, a.shape, 0)
        b_scr[...] = jnp.where(row == tb - 1, 1.0, pltpu.roll(a, tb - 1, axis=0))
        u_scr[...] = dyv * _gelu(gate_v)
        carry = carry_ref[...]
        for gi in reversed(range(groups)):
            rows = pl.ds(gi * SUBLANES, SUBLANES)
            pb, gl = _scan8_bwd(b_scr[rows, :], u_scr[rows, :])
            dh_scr[rows, :] = gl + pb * carry
            carry = dh_scr[pl.ds(gi * SUBLANES, 1), :]
        dh = dh_scr[...]
        carry_ref[...] = carry * jnp.sum(jnp.where(row == 0, a, 0.0), axis=0, keepdims=True)

        hprev_tail = jnp.where(first_block, 0.0, hprev_ref[...])
        h_prev = _shift_down(hsv, hprev_tail, 1)
        da = dh * h_prev
        ixc = i * xc
        dmult = dh * ixc
        di = dh * mult * xc
        dxc = dh * mult * i
        a2 = a * a
        dlog_a = da * a - dmult * a2 / mult
        dpre_r = (dlog_a * (RG_C * ls)) * r * (1.0 - r)
        dpre_i = di * i * (1.0 - i)
        dlam_ref[...] += jnp.sum(dlog_a * r, axis=0, keepdims=True) * (RG_C * _sigmoid(-lam_v))
        dbr_ref[...] += jnp.sum(dpre_r, axis=0, keepdims=True)
        dbi_ref[...] += jnp.sum(dpre_i, axis=0, keepdims=True)
        xcd = xc.astype(CD)
        dprc = dpre_r.astype(CD)
        dpic = dpre_i.astype(CD)
        tn_dims = (((0,), (0,)), ((), ()))
        dwr_ref[0] += lax.dot_general(xcd, dprc, tn_dims, preferred_element_type=F32)
        dwi_ref[0] += lax.dot_general(xcd, dpic, tn_dims, preferred_element_type=F32)
        dxc = dxc + jnp.dot(dprc, wrt_ref[0], preferred_element_type=F32) + jnp.dot(dpic, wit_ref[0],
                                                                                    preferred_element_type=F32)
        dcb_ref[...] += jnp.sum(dxc, axis=0, keepdims=True)
        for k in range(CONV_W):
            dcw_ref[pl.ds(k, 1), :] += jnp.sum(dxc * taps[k], axis=0, keepdims=True)
        head = head_ref[...]
        dxb = jnp.zeros_like(dxc)
        for sft in range(CONV_W):
            dxb = dxb + cw_ref[pl.ds(CONV_W - 1 - sft, 1), :] * _shift_up(dxc, head, sft)
        dx_ref[...] = dxb.astype(CD)
        head_ref[...] = dxc[0:SUBLANES, :]

    blk = pl.BlockSpec((tb, cbw), lambda n, t: (nt - 1 - t, n))
    tail = pl.BlockSpec((SUBLANES, cbw), lambda n, t: (jnp.maximum((nt - 1 - t) * groups - 1, 0), n))
    vec = pl.BlockSpec((1, cbw), lambda n, t: (0, n))
    cwb = pl.BlockSpec((CONV_W, cbw), lambda n, t: (0, n))
    wblk = pl.BlockSpec((1, cbw, cbw), lambda n, t: (n, 0, 0))
    vshape = jax.ShapeDtypeStruct((1, c), F32)
    wshape = jax.ShapeDtypeStruct((RG_BLOCKS, cbw, cbw), F32)
    return pl.pallas_call(
        body, name=name,
        out_shape=(jax.ShapeDtypeStruct((s, c), CD), jax.ShapeDtypeStruct((s, c), CD),
                   jax.ShapeDtypeStruct((CONV_W, c), F32), vshape, vshape, vshape, vshape, wshape, wshape),
        grid=(RG_BLOCKS, nt),
        in_specs=[blk, blk, blk, tail, blk, tail, cwb, vec, wblk, wblk, wblk, wblk, vec, vec, vec],
        out_specs=(blk, blk, cwb, vec, vec, vec, vec, wblk, wblk),
        scratch_shapes=[pltpu.VMEM((1, cbw), F32), pltpu.VMEM((SUBLANES, cbw), F32),
                        pltpu.VMEM((tb, cbw), F32), pltpu.VMEM((tb, cbw), F32), pltpu.VMEM((tb, cbw), F32)],
        compiler_params=_params("parallel", "arbitrary"),
    )(dy, gate_br, x_br, x_br, hs, hs, cw, cb, wr, wi, wrt, wit, br, bi, lam)


ATT_BLOCK = 256
ATT_Q_BLOCK = 1024
ATT_Q_BLOCK_FWD = 2048
ATT_RATIO = ATT_Q_BLOCK // ATT_BLOCK
ATT_SCALE = 1.0 / math.sqrt(SB_HEAD_DIM)
N_PAIRS = SB_HEADS * SB_HEAD_DIM // LANES
NT_DIMS = (((1,), (1,)), ((), ()))
TN_DIMS = (((0,), (0,)), ((), ()))


LOG2E = 1.4426950408889634


def _neg_abs(x):
    bits = lax.bitcast_convert_type(x, jnp.uint32) | jnp.uint32(0x80000000)
    return lax.bitcast_convert_type(bits, F32)


def _qk(qx, kb):
    return lax.dot_general(qx, kb, NT_DIMS, preferred_element_type=F32)


def _sb_logits(qk, valid):
    z2 = qk * (ATT_SCALE * LOG2E)
    lb2 = jnp.minimum(z2, 0.0) - jnp.log2(1.0 + jnp.exp2(_neg_abs(z2)))
    l2 = lb2 - z2
    if valid is not None:
        l2 = jnp.where(valid, l2, 0.0)
    return lb2, l2


def _hi_lo(x):
    hi = x.astype(CD)
    lo = (x - hi.astype(F32)).astype(CD)
    return jnp.concatenate([hi, lo], axis=1)


def _tri(strict, stacked):
    r = lax.broadcasted_iota(jnp.int32, (ATT_BLOCK, ATT_BLOCK), 0)
    c = lax.broadcasted_iota(jnp.int32, (ATT_BLOCK, ATT_BLOCK), 1)
    m = (r > c if strict else r >= c).astype(CD)
    return jnp.concatenate([m, m], axis=0) if stacked else m


def _attn_fwd(qkv, name):
    _, s, _ = qkv.shape
    tq, t = _pick(s, (ATT_Q_BLOCK_FWD, ATT_Q_BLOCK)), ATT_BLOCK
    ratio = tq // t
    nblk = s // tq

    def body(q_ref, k_ref, v_ref, o_ref, qk_scr, w_scr):
        i = pl.program_id(1)
        lane = lax.broadcasted_iota(jnp.int32, (1, LANES), 1)
        head_masks = (lane < SB_HEAD_DIM, lane >= SB_HEAD_DIM)
        q = q_ref[0]
        qs = [jnp.where(m, q, jnp.zeros_like(q)) for m in head_masks]
        tri = _tri(True, False)
        rr = lax.broadcasted_iota(jnp.int32, (tq, t), 0)
        cc = lax.broadcasted_iota(jnp.int32, (tq, t), 1)

        def rows_of(j):
            return pl.ds(pl.multiple_of(j * t, t), t)

        def tail(x, row0):
            return x if row0 == 0 else x[row0:]

        def start_logits(j, row0=0):
            kb = k_ref[0, rows_of(j), :]
            for hd in range(2):
                qk_scr[hd, row0:, :] = _qk(tail(qs[hd], row0), kb)

        def weights(run, diagonal=False, row0=0):
            new_run = []
            valid = (cc < rr)[:tq - row0] if diagonal else None
            for hd in range(2):
                lb2, l2 = _sb_logits(qk_scr[hd, row0:, :], valid)
                w = jnp.exp2(lb2 + (tail(run[hd], row0) + jnp.dot(l2.astype(CD), tri, preferred_element_type=F32)))
                if valid is not None:
                    w = jnp.where(valid, w, 0.0)
                w_scr[row0:, hd * t:(hd + 1) * t] = w.astype(CD)
                rowsum = jnp.sum(l2, axis=1, keepdims=True)
                if row0:
                    rowsum = jnp.concatenate([jnp.zeros((row0, 1), F32), rowsum], axis=0)
                new_run.append(run[hd] + rowsum)
            return tuple(new_run)

        def apply_weights(j, row0=0):
            vb = v_ref[0, rows_of(j), :]
            vcat = jnp.concatenate([jnp.where(m, vb, jnp.zeros_like(vb)) for m in head_masks], axis=0)
            inc = jnp.dot(w_scr[row0:, :], vcat, preferred_element_type=F32)
            return inc if row0 == 0 else jnp.concatenate([jnp.zeros((row0, LANES), F32), inc], axis=0)

        zero = jnp.zeros((tq, 1), F32)
        last = ratio - 1
        start_logits(ratio * i + last, last * t)
        run = weights((zero, zero), True, last * t)
        oacc = jnp.zeros((tq, LANES), F32)
        for d in reversed(range(last)):
            start_logits(ratio * i + d, d * t)
            oacc = oacc + apply_weights(ratio * i + d + 1, (d + 1) * t)
            run = weights(run, True, d * t)
        start_logits(jnp.maximum(ratio * i - 1, 0))

        def step(jj, carry):
            run, oacc = carry
            b = ratio * i - 1 - jj
            oacc = oacc + apply_weights(b + 1)
            run = weights(run)
            start_logits(jnp.maximum(b - 1, 0))
            return run, oacc

        run, oacc = lax.fori_loop(0, ratio * i, step, (run, oacc))
        o_ref[0] = oacc + apply_weights(0)

    return pl.pallas_call(
        body, name=name, out_shape=jax.ShapeDtypeStruct((N_PAIRS, s, LANES), F32), grid=(N_PAIRS, nblk),
        in_specs=[pl.BlockSpec((1, tq, LANES), lambda p, i: (p, i, 0)),
                  pl.BlockSpec((1, s, LANES), lambda p, i: (N_PAIRS + p, 0, 0)),
                  pl.BlockSpec((1, s, LANES), lambda p, i: (2 * N_PAIRS + p, 0, 0))],
        out_specs=pl.BlockSpec((1, tq, LANES), lambda p, i: (p, i, 0)),
        scratch_shapes=[pltpu.VMEM((2, tq, t), F32), pltpu.VMEM((tq, 2 * t), CD)],
        compiler_params=_params("parallel", "arbitrary"),
    )(qkv, qkv, qkv)


def _attn_bwd(qkv, o, do, name):
    _, s, _ = qkv.shape
    tq, t = ATT_Q_BLOCK, ATT_BLOCK
    nblk = s // tq

    def body(q_ref, k_ref, v_ref, o_ref, do_ref, dq_ref, dk_ref, dv_ref, qk_scr, dw_scr, w_scr, dz_scr):
        i = pl.program_id(1)

        @pl.when(i == 0)
        def _():
            dk_ref[...] = jnp.zeros_like(dk_ref)
            dv_ref[...] = jnp.zeros_like(dv_ref)

        lane = lax.broadcasted_iota(jnp.int32, (1, LANES), 1)
        head_masks = (lane < SB_HEAD_DIM, lane >= SB_HEAD_DIM)
        q = q_ref[0]
        dov = do_ref[0]
        ov = o_ref[0]
        qs = [jnp.where(m, q, jnp.zeros_like(q)) for m in head_masks]
        q_scaled_t = jnp.concatenate([(qx.astype(F32) * ATT_SCALE).T for qx in qs], axis=1).astype(CD)
        docs = [jnp.where(m, dov, jnp.zeros_like(dov)) for m in head_masks]
        docat_t = jnp.concatenate([d.astype(F32).T for d in docs], axis=1).astype(CD)
        totals = [jnp.sum(d.astype(F32) * ov, axis=1, keepdims=True) for d in docs]
        tri = _tri(True, False)
        tri_incl = _tri(False, True)
        rr = lax.broadcasted_iota(jnp.int32, (tq, t), 0)
        cc = lax.broadcasted_iota(jnp.int32, (tq, t), 1)

        def rows_of(j):
            return pl.ds(pl.multiple_of(j * t, t), t)

        def tail(x, row0):
            return x if row0 == 0 else x[row0:]

        def pad_rows(x, row0):
            return x if row0 == 0 else jnp.concatenate([jnp.zeros((row0, x.shape[1]), x.dtype), x], axis=0)

        def start_products(j, row0=0):
            kb = k_ref[0, rows_of(j), :]
            vb = v_ref[0, rows_of(j), :]
            for hd in range(2):
                qk_scr[hd, row0:, :] = _qk(tail(qs[hd], row0), kb)
                dw_scr[hd, row0:, :] = lax.dot_general(tail(docs[hd], row0), vb, NT_DIMS, preferred_element_type=F32)

        def logit_grads(run, erun, diagonal=False, row0=0):
            new_run, new_erun = [], []
            valid = (cc < rr)[:tq - row0] if diagonal else None
            for hd in range(2):
                lb2, l2 = _sb_logits(qk_scr[hd, row0:, :], valid)
                w = jnp.exp2(lb2 + (tail(run[hd], row0) + jnp.dot(l2.astype(CD), tri, preferred_element_type=F32)))
                if valid is not None:
                    w = jnp.where(valid, w, 0.0)
                wc = w.astype(CD)
                w_scr[hd * tq + row0:(hd + 1) * tq, :] = wc
                e = dw_scr[hd, row0:, :] * wc.astype(F32)
                prefix = (tail(totals[hd] - erun[hd], row0)
                          - jnp.dot(_hi_lo(e), tri_incl, preferred_element_type=F32))
                dz = e - jnp.exp2(lb2) * (e + prefix)
                if valid is not None:
                    dz = jnp.where(valid, dz, 0.0)
                dz_scr[hd * tq + row0:(hd + 1) * tq, :] = dz.astype(CD)
                new_run.append(run[hd] + pad_rows(jnp.sum(l2, axis=1, keepdims=True), row0))
                new_erun.append(erun[hd] + pad_rows(jnp.sum(e, axis=1, keepdims=True), row0))
            return tuple(new_run), tuple(new_erun)

        def apply_grads(j, row0=0):
            rows = rows_of(j)
            kb = k_ref[0, rows, :]
            kcat = jnp.concatenate([jnp.where(m, kb, jnp.zeros_like(kb)) for m in head_masks], axis=0)
            dz_heads = [dz_scr[hd * tq + row0:(hd + 1) * tq, :] for hd in range(2)]
            w_heads = [w_scr[hd * tq + row0:(hd + 1) * tq, :] for hd in range(2)]
            q_t = jnp.concatenate([q_scaled_t[:, hd * tq + row0:(hd + 1) * tq] for hd in range(2)], axis=1)
            do_t = jnp.concatenate([docat_t[:, hd * tq + row0:(hd + 1) * tq] for hd in range(2)], axis=1)
            dk_ref[0, :, rows] += jnp.dot(q_t, jnp.concatenate(dz_heads, axis=0), preferred_element_type=F32)
            dv_ref[0, :, rows] += jnp.dot(do_t, jnp.concatenate(w_heads, axis=0), preferred_element_type=F32)
            return pad_rows(jnp.dot(jnp.concatenate(dz_heads, axis=1), kcat, preferred_element_type=F32), row0)

        zero = jnp.zeros((tq, 1), F32)
        last = ATT_RATIO - 1
        start_products(ATT_RATIO * i + last, last * t)
        run, erun = logit_grads((zero, zero), (zero, zero), True, last * t)
        dqacc = jnp.zeros((tq, LANES), F32)
        for d in reversed(range(last)):
            start_products(ATT_RATIO * i + d, d * t)
            dqacc = dqacc + apply_grads(ATT_RATIO * i + d + 1, (d + 1) * t)
            run, erun = logit_grads(run, erun, True, d * t)
        start_products(jnp.maximum(ATT_RATIO * i - 1, 0))

        def step(jj, carry):
            run, erun, dqacc = carry
            b = ATT_RATIO * i - 1 - jj
            dqacc = dqacc + apply_grads(b + 1)
            run, erun = logit_grads(run, erun)
            start_products(jnp.maximum(b - 1, 0))
            return run, erun, dqacc

        run, erun, dqacc = lax.fori_loop(0, ATT_RATIO * i, step, (run, erun, dqacc))
        dq_ref[0] = ((dqacc + apply_grads(0)) * ATT_SCALE).astype(CD)

    qblk = pl.BlockSpec((1, tq, LANES), lambda p, i: (p, i, 0))
    full = pl.BlockSpec((1, LANES, s), lambda p, i: (p, 0, 0))
    shape = jax.ShapeDtypeStruct((N_PAIRS, s, LANES), F32)
    shape_t = jax.ShapeDtypeStruct((N_PAIRS, LANES, s), F32)
    dq, dk_t, dv_t = pl.pallas_call(
        body, name=name, out_shape=(jax.ShapeDtypeStruct(shape.shape, CD), shape_t, shape_t), grid=(N_PAIRS, nblk),
        in_specs=[qblk,
                  pl.BlockSpec((1, s, LANES), lambda p, i: (N_PAIRS + p, 0, 0)),
                  pl.BlockSpec((1, s, LANES), lambda p, i: (2 * N_PAIRS + p, 0, 0)),
                  qblk, qblk],
        out_specs=(qblk, full, full),
        scratch_shapes=[pltpu.VMEM((2, tq, t), F32), pltpu.VMEM((2, tq, t), F32),
                        pltpu.VMEM((2 * tq, t), CD), pltpu.VMEM((2 * tq, t), CD)],
        compiler_params=_params("parallel", "arbitrary"),
    )(qkv, qkv, qkv, o, do)
    return dq, jnp.swapaxes(dk_t, 1, 2).astype(CD), jnp.swapaxes(dv_t, 1, 2).astype(CD)


def _adamw(w, g, m, v, name):
    shape = w.shape
    rows, cols = (shape[-2], shape[-1]) if len(shape) >= 2 else (1, shape[-1])
    lead = w.size // (rows * cols)
    tr = _pick(rows, (512, 256, 128, 64, 32, 16, 8))

    def body(w_ref, g_ref, m_ref, v_ref, d_ref, nm_ref, nv_ref):
        gv = g_ref[...]
        nm = ADAM_B1 * m_ref[...] + (1.0 - ADAM_B1) * gv
        nv = ADAM_B2 * v_ref[...] + (1.0 - ADAM_B2) * (gv * gv)
        m_hat = nm / (1.0 - ADAM_B1 ** ADAM_STEP)
        v_hat = nv / (1.0 - ADAM_B2 ** ADAM_STEP)
        d_ref[...] = -ADAM_LR * (m_hat / (jnp.sqrt(v_hat) + ADAM_EPS) + ADAM_WD * w_ref[...])
        nm_ref[...] = nm
        nv_ref[...] = nv

    blk = pl.BlockSpec((1, tr, cols), lambda l, i: (l, i, 0))
    out = jax.ShapeDtypeStruct((lead, rows, cols), F32)
    d, nm, nv = pl.pallas_call(
        body, name=name, out_shape=(out, out, out), grid=(lead, rows // tr),
        in_specs=[blk, blk, blk, blk], out_specs=(blk, blk, blk), compiler_params=_params("parallel", "parallel"),
    )(*[a.reshape(lead, rows, cols) for a in (w, g, m, v)])
    return d.reshape(shape), nm.reshape(shape), nv.reshape(shape)


SC_TILES = 32
SC_VECTOR = 16


def _adamw_on_tiles(w, g, m, v, name):
    shape = w.shape
    per_tile = w.size // SC_TILES

    def body(w_hbm, g_hbm, m_hbm, v_hbm, d_hbm, nm_hbm, nv_hbm, wb, gb, mb, vb):
        tile = lax.axis_index("subcore") * 2 + lax.axis_index("sparse_core")
        mine = pl.ds(tile * per_tile, per_tile)
        pltpu.sync_copy(w_hbm.at[mine], wb)
        pltpu.sync_copy(g_hbm.at[mine], gb)
        pltpu.sync_copy(m_hbm.at[mine], mb)
        pltpu.sync_copy(v_hbm.at[mine], vb)

        @pl.loop(0, per_tile, step=SC_VECTOR)
        def _(i):
            sl = pl.ds(i, SC_VECTOR)
            gv = gb[sl]
            nm = ADAM_B1 * mb[sl] + (1.0 - ADAM_B1) * gv
            nv = ADAM_B2 * vb[sl] + (1.0 - ADAM_B2) * (gv * gv)
            m_hat = nm / (1.0 - ADAM_B1 ** ADAM_STEP)
            v_hat = nv / (1.0 - ADAM_B2 ** ADAM_STEP)
            wb[sl] = -ADAM_LR * (m_hat / (jnp.sqrt(v_hat) + ADAM_EPS) + ADAM_WD * wb[sl])
            mb[sl] = nm
            vb[sl] = nv

        pltpu.sync_copy(wb, d_hbm.at[mine])
        pltpu.sync_copy(mb, nm_hbm.at[mine])
        pltpu.sync_copy(vb, nv_hbm.at[mine])

    flat = jax.ShapeDtypeStruct((w.size,), F32)
    d, nm, nv = pl.kernel(
        body, name=name, out_type=(flat, flat, flat),
        mesh=plsc.VectorSubcoreMesh(core_axis_name="sparse_core", subcore_axis_name="subcore"),
        scratch_types=[pltpu.VMEM((per_tile,), F32)] * 4,
    )(*[a.reshape(-1) for a in (w, g, m, v)])
    return d.reshape(shape), nm.reshape(shape), nv.reshape(shape)


HBM = pl.BlockSpec(memory_space=pltpu.HBM)


def _coords():
    return lax.axis_index("x"), lax.axis_index("y"), lax.axis_index("c")


def _other_chips(x, y):
    return [(1 - x, y), (x, 1 - y), (1 - x, 1 - y)]


def _allgather_chips(shard, name, collective_id=None):
    r, cols = shard.shape
    half = r // 2
    quarter = half // 2

    def body(src_ref, out_ref, send_sems, recv_sems):
        x, y, c = _coords()
        sibling = (x, y, 1 - c)
        nx, ny, diag = (1 - x, y), (x, 1 - y), (1 - x, 1 - y)

        def piece(chip, core, lo, n):
            return out_ref.at[2 * chip[0] + chip[1], pl.ds(core * half + lo, n), :]

        def copy(k, dst, to, src=None):
            return pltpu.make_async_remote_copy(
                src_ref=dst if src is None else src, dst_ref=dst,
                send_sem=send_sems.at[k], recv_sem=recv_sems.at[k], device_id=to, device_id_type=MESH)

        me = (x, y)
        mine = src_ref.at[pl.ds(c * half, half), :]
        direct = [copy(0, piece(me, c, 0, half), (*nx, c), src=mine), copy(1, piece(me, c, 0, half), (*ny, c), src=mine)]
        for cp in direct:
            cp.start()
        arrivals = [piece(nx, c, 0, half), piece(ny, c, 0, half), piece(diag, c, 0, quarter),
                    piece(diag, c, quarter, quarter)]
        onward = [copy(2, piece(nx, c, 0, quarter), (*ny, c)), copy(3, piece(ny, c, quarter, quarter), (*nx, c))]
        to_sibling = [copy(4 + k, dst, sibling) for k, dst in enumerate(arrivals)]
        for k, dst in enumerate(arrivals):
            copy(k, dst, (x, y, c)).wait_recv()
            if k < 2:
                onward[k].start()
            to_sibling[k].start()
        from_sibling = [piece(nx, 1 - c, 0, half), piece(ny, 1 - c, 0, half), piece(diag, 1 - c, 0, quarter),
                        piece(diag, 1 - c, quarter, quarter)]
        for k, dst in enumerate(from_sibling):
            copy(4 + k, dst, (x, y, c)).wait_recv()
        for cp in direct + onward + to_sibling:
            cp.wait_send()

    out_shape = jax.ShapeDtypeStruct((N_CHIPS, r, cols), shard.dtype)
    sems = (pltpu.SemaphoreType.DMA((8,)), pltpu.SemaphoreType.DMA((8,)))
    if collective_id is None:
        return pl.pallas_call(body, name=name, out_shape=out_shape, in_specs=[HBM], out_specs=HBM,
                              scratch_shapes=list(sems))(shard)
    shard_ref = jax.new_ref(shard, memory_space=pltpu.MemorySpace.HBM)
    gathered_ref = jax.empty_ref(out_shape, memory_space=pltpu.MemorySpace.HBM)

    @_sequencer(name, collective_id, sems)
    def launch(send_sems, recv_sems):
        x, y, c = _coords()
        _handshake([(1 - x, y, c), (x, 1 - y, c), (x, y, 1 - c)])
        body(shard_ref, gathered_ref, send_sems, recv_sems)

    launch()
    return gathered_ref[...]


def _exchange_sibling_halves(g, name):
    n, r, cols = g.shape
    half = r // 2

    def body(g_ref, out_ref, send_sem, recv_sem):
        x, y, c = _coords()
        cp = pltpu.make_async_remote_copy(
            src_ref=g_ref.at[:, pl.ds((1 - c) * half, half), :], dst_ref=out_ref,
            send_sem=send_sem, recv_sem=recv_sem, device_id=(x, y, 1 - c), device_id_type=MESH)
        cp.start()
        cp.wait()

    return pl.pallas_call(
        body, name=name, out_shape=jax.ShapeDtypeStruct((n, half, cols), g.dtype),
        in_specs=[HBM], out_specs=HBM,
        scratch_shapes=[pltpu.SemaphoreType.DMA, pltpu.SemaphoreType.DMA],
    )(g)


def _sequencer(name, collective_id, scratch_types):
    return pl.kernel(mesh=plsc.ScalarSubcoreMesh(axis_name="sequencer", num_cores=1), name=name,
                     scratch_types=scratch_types, compiler_params=pltpu.CompilerParams(collective_id=collective_id))


def _handshake(peers):
    barrier = pltpu.get_barrier_semaphore()
    for peer in peers:
        pl.semaphore_signal(barrier, inc=1, device_id=peer, device_id_type=MESH)
    pl.semaphore_wait(barrier, len(peers))


def _exchange_sibling_halves_async(g, name, collective_id):
    n, r, cols = g.shape
    half = r // 2
    g_ref = jax.new_ref(g, memory_space=pltpu.MemorySpace.HBM)
    out_ref = jax.empty_ref(jax.ShapeDtypeStruct((n, half, cols), g.dtype), memory_space=pltpu.MemorySpace.HBM)

    @_sequencer(name, collective_id, (pltpu.SemaphoreType.DMA, pltpu.SemaphoreType.DMA))
    def launch(send_sem, recv_sem):
        x, y, c = _coords()
        _handshake([(x, y, 1 - c)])
        cp = pltpu.make_async_remote_copy(
            src_ref=g_ref.at[:, pl.ds((1 - c) * half, half), :], dst_ref=out_ref,
            send_sem=send_sem, recv_sem=recv_sem, device_id=(x, y, 1 - c), device_id_type=MESH)
        cp.start()
        cp.wait()

    launch()
    return out_ref[...]


def _share_halves_async(v, name, collective_id):
    h = v.shape[0] // 2
    v_ref = jax.new_ref(v, memory_space=pltpu.MemorySpace.HBM)

    @_sequencer(name, collective_id, (pltpu.SemaphoreType.DMA, pltpu.SemaphoreType.DMA))
    def launch(send_sem, recv_sem):
        x, y, c = _coords()
        _handshake([(x, y, 1 - c)])
        cp = pltpu.make_async_remote_copy(
            src_ref=v_ref.at[pl.ds(c * h, h), :], dst_ref=v_ref.at[pl.ds(c * h, h), :],
            send_sem=send_sem, recv_sem=recv_sem, device_id=(x, y, 1 - c), device_id_type=MESH)
        cp.start()
        pltpu.make_async_remote_copy(
            src_ref=v_ref.at[pl.ds(c * h, h), :], dst_ref=v_ref.at[pl.ds((1 - c) * h, h), :],
            send_sem=send_sem, recv_sem=recv_sem, device_id=(x, y, 1 - c), device_id_type=MESH).wait_recv()
        cp.wait_send()

    launch()
    return v_ref[...]


def _scatter_to_chips_async(p, name, collective_id):
    p_ref = jax.new_ref(p, memory_space=pltpu.MemorySpace.HBM)
    out_ref = jax.empty_ref(jax.ShapeDtypeStruct(p.shape, p.dtype), memory_space=pltpu.MemorySpace.HBM)

    @_sequencer(name, collective_id, (pltpu.SemaphoreType.DMA((3,)), pltpu.SemaphoreType.DMA((3,))))
    def launch(send_sems, recv_sems):
        x, y, c = _coords()
        me = 2 * x + y
        _handshake([(px, py, c) for px, py in _other_chips(x, y)])
        sends = []
        for j, (px, py) in enumerate(_other_chips(x, y)):
            sends.append(pltpu.make_async_remote_copy(
                src_ref=p_ref.at[2 * px + py], dst_ref=out_ref.at[me],
                send_sem=send_sems.at[j], recv_sem=recv_sems.at[j], device_id=(px, py, c), device_id_type=MESH))
        for cp in sends:
            cp.start()
        for j, (px, py) in enumerate(_other_chips(x, y)):
            pltpu.make_async_remote_copy(
                src_ref=p_ref.at[me], dst_ref=out_ref.at[2 * px + py],
                send_sem=send_sems.at[j], recv_sem=recv_sems.at[j], device_id=(px, py, c),
                device_id_type=MESH).wait_recv()
        for cp in sends:
            cp.wait_send()

    launch()
    return out_ref[...]


def _share_halves(v, name):
    h = v.shape[0] // 2

    def body(v_ref, out_ref, send_sem, recv_sem):
        x, y, c = _coords()
        cp = pltpu.make_async_remote_copy(
            src_ref=v_ref.at[pl.ds(c * h, h), :], dst_ref=out_ref.at[pl.ds(c * h, h), :],
            send_sem=send_sem, recv_sem=recv_sem, device_id=(x, y, 1 - c), device_id_type=MESH)
        cp.start()
        pltpu.make_async_remote_copy(
            src_ref=v_ref.at[pl.ds(c * h, h), :], dst_ref=out_ref.at[pl.ds((1 - c) * h, h), :],
            send_sem=send_sem, recv_sem=recv_sem, device_id=(x, y, 1 - c), device_id_type=MESH).wait_recv()
        cp.wait_send()

    return pl.pallas_call(
        body, name=name, out_shape=jax.ShapeDtypeStruct(v.shape, v.dtype),
        in_specs=[HBM], out_specs=HBM, input_output_aliases={0: 0},
        scratch_shapes=[pltpu.SemaphoreType.DMA, pltpu.SemaphoreType.DMA],
    )(v)


def _allreduce_small(v, name):
    r, cols = v.shape

    def body(v_ref, out_ref, buf_ref, send_sems, recv_sems):
        x, y, c = _coords()
        me = 4 * x + 2 * y + c
        buf_ref[me] = v_ref[...]
        sends = []
        for k in range(1, N_DEV):
            px = 1 - x if k & 4 else x
            py = 1 - y if k & 2 else y
            pc = 1 - c if k & 1 else c
            sends.append(pltpu.make_async_remote_copy(
                src_ref=v_ref, dst_ref=buf_ref.at[me], send_sem=send_sems.at[k - 1], recv_sem=recv_sems.at[k - 1],
                device_id=(px, py, pc), device_id_type=MESH))
        for cp in sends:
            cp.start()
        for cp in sends:
            cp.wait()
        acc = buf_ref[0]
        for d in range(1, N_DEV):
            acc = acc + buf_ref[d]
        out_ref[...] = acc

    return pl.pallas_call(
        body, name=name, out_shape=jax.ShapeDtypeStruct((r, cols), F32),
        in_specs=[pl.BlockSpec(memory_space=pltpu.VMEM)], out_specs=pl.BlockSpec(memory_space=pltpu.VMEM),
        scratch_shapes=[pltpu.VMEM((N_DEV, r, cols), F32), pltpu.SemaphoreType.DMA((N_DEV - 1,)),
                        pltpu.SemaphoreType.DMA((N_DEV - 1,))],
    )(v)


def _add_sibling(g, from_sibling, core, name):
    n, h, cols = from_sibling.shape
    tr = _row_tile(h)
    steps = h // tr

    def body(core_ref, a_ref, b_ref, o_ref):
        o_ref[...] = (a_ref[...] + b_ref[...]).astype(o_ref.dtype)

    return pl.pallas_call(
        body, name=name, out_shape=jax.ShapeDtypeStruct(from_sibling.shape, jnp.bfloat16),
        grid_spec=pltpu.PrefetchScalarGridSpec(
            num_scalar_prefetch=1, grid=(n, steps),
            in_specs=[pl.BlockSpec((1, tr, cols), lambda s, i, core_ref: (s, core_ref[0] * steps + i, 0)),
                      pl.BlockSpec((1, tr, cols), lambda s, i, core_ref: (s, i, 0))],
            out_specs=pl.BlockSpec((1, tr, cols), lambda s, i, core_ref: (s, i, 0))),
        compiler_params=_params("parallel", "parallel"),
    )(core.reshape(1).astype(jnp.int32), g, from_sibling)


def _sum_slots(p, own, chip, core, name):
    n, r, cols = p.shape
    tr = _row_tile(r)
    steps = r // tr

    def body(core_ref, chip_ref, p_ref, own_ref, o_ref):
        parts = [jnp.where(chip_ref[0] == s, own_ref[0], p_ref[s]).astype(F32) for s in range(n)]
        o_ref[...] = ((parts[0] + parts[1]) + parts[2]) + parts[3]

    return pl.pallas_call(
        body, name=name, out_shape=jax.ShapeDtypeStruct((2 * r, cols), F32),
        grid_spec=pltpu.PrefetchScalarGridSpec(
            num_scalar_prefetch=2, grid=(steps,),
            in_specs=[pl.BlockSpec((n, tr, cols), lambda i, core_ref, chip_ref: (0, i, 0)),
                      pl.BlockSpec((1, tr, cols), lambda i, core_ref, chip_ref: (chip_ref[0], i, 0))],
            out_specs=pl.BlockSpec((tr, cols), lambda i, core_ref, chip_ref: (core_ref[0] * steps + i, 0))),
        compiler_params=_params("parallel"),
    )(core.reshape(1).astype(jnp.int32), chip.reshape(1).astype(jnp.int32), p, own)


PACK_COLS = 1024


def _pack_shards(parts):
    return jnp.concatenate([p.reshape(-1, PACK_COLS) for p in parts], axis=0)


def _unpack_shards(buf, shapes):
    out, row = [], 0
    for shp in shapes:
        nrows = math.prod(shp) // PACK_COLS
        out.append(buf[..., row:row + nrows, :].reshape(buf.shape[:-2] + tuple(shp)))
        row += nrows
    return out


def _local_step(x, target, w):
    t = lambda a: a.T
    g = {}
    w_in_g, w_in_x = w["a_w_in"][:, :D_RNN], w["a_w_in"][:, D_RNN:]
    h0, gate_br, x_br = _norm_and_project(x, w["norm_mix_g"][0], w_in_g, w_in_x, "rglru_in")
    y_a, hs = _rglru_fwd(gate_br, x_br, w["a_conv_w"], w["a_conv_b"], w["a_w_r"], w["a_w_i"], w["a_b_r"],
                         w["a_b_i"], w["a_lambda"], "rglru_fwd")
    x1, h1 = _matmul([(y_a, w["a_w_out"])], F32, "mm_a_out", addend=x, norm_gain=w["norm_ffn_g"][0])
    fg0, fu0, act0 = _ffn_up(h1, w["ffn_w_gate"][0], w["ffn_w_up"][0], "ffn0_up")
    x2, h2 = _matmul([(act0, w["ffn_w_down"][0])], F32, "mm_f0_down", addend=x1, norm_gain=w["norm_mix_g"][1],
                     tk=D_FF)
    qkv = _matmul([(h2, w["b_w_qkv"])], CD, "mm_b_qkv", out_lbm=True, tn=3 * D_MODEL)
    o = _attn_fwd(qkv, "attn_fwd")
    x3, h3 = _matmul([(o, w["b_w_out"])], F32, "mm_b_out", a_lbm=True, addend=x2, norm_gain=w["norm_ffn_g"][1])
    fg1, fu1, act1 = _ffn_up(h3, w["ffn_w_gate"][1], w["ffn_w_up"][1], "ffn1_up")
    dx4, dx4c, g["final_g"], loss = _matmul([(act1, w["ffn_w_down"][1])], F32, "mm_f1_down", addend=x3,
                                            loss_head=(w["final_g"], target), tk=D_FF)

    def ffn_bwd(dx_out, dxc, h, x_in, fg, fu, act, layer, tag):
        dg, du = _ffn_dact(dxc, t(w["ffn_w_down"][layer]), fg, fu, "ffn_" + tag + "_dact")
        dwd = _matmul([(act, dxc)], F32, "mm_" + tag + "_dwd", trans_a=True)
        dwg = _matmul([(h, dg)], F32, "mm_" + tag + "_dwg", trans_a=True)
        dwu = _matmul([(h, du)], F32, "mm_" + tag + "_dwu", trans_a=True)
        dx_in, dx_in_c, dgain = _matmul([(dg, t(w["ffn_w_gate"][layer])), (du, t(w["ffn_w_up"][layer]))], F32,
                                        "mm_" + tag + "_dh", norm_bwd=(x_in, w["norm_ffn_g"][layer], dx_out),
                                        tk=D_FF, tm=256)
        return dx_in, dx_in_c, dgain, dwg, dwu, dwd

    dx3, dx3c, dgf1, dwg1, dwu1, dwd1 = ffn_bwd(dx4, dx4c, h3, x3, fg1, fu1, act1, 1, "f1")
    do = _matmul([(dx3c, t(w["b_w_out"]))], CD, "mm_b_do", out_lbm=True, tn=1024)
    g["b_w_out"] = _matmul([(o, dx3c)], F32, "mm_b_dwout", trans_a=True, a_lbm=True)
    dq, dk, dv = _attn_bwd(qkv, o, do, "attn_bwd")
    wq_t = t(w["b_w_qkv"])
    parts = (dq, dk, dv)
    g["b_w_qkv"] = jnp.concatenate(
        [_matmul([(h2, p)], F32, "mm_b_dwqkv%d" % n, trans_a=True, b_lbm=True) for n, p in enumerate(parts)], axis=1)
    dx2, dx2c, dgm1 = _matmul([(p, wq_t[n * D_MODEL:(n + 1) * D_MODEL]) for n, p in enumerate(parts)], F32, "mm_b_dh",
                              a_lbm=True, norm_bwd=(x2, w["norm_mix_g"][1], dx3))
    dx1, dx1c, dgf0, dwg0, dwu0, dwd0 = ffn_bwd(dx2, dx2c, h1, x1, fg0, fu0, act0, 0, "f0")
    dy_a = _matmul([(dx1c, t(w["a_w_out"]))], F32, "mm_a_dy")
    g["a_w_out"] = _matmul([(y_a, dx1c)], F32, "mm_a_dwout", trans_a=True)
    wrt = jnp.swapaxes(w["a_w_r"], 1, 2)
    wit = jnp.swapaxes(w["a_w_i"], 1, 2)
    (dgate, dxbr, g["a_conv_w"], g["a_conv_b"], g["a_b_r"], g["a_b_i"], g["a_lambda"], g["a_w_r"],
     g["a_w_i"]) = _rglru_bwd(dy_a, gate_br, x_br, hs, w["a_conv_w"], w["a_conv_b"], w["a_w_r"], w["a_w_i"], wrt, wit,
                              w["a_b_r"], w["a_b_i"], w["a_lambda"], "rglru_bwd")
    g["a_w_in"] = jnp.concatenate([_matmul([(h0, dgate)], F32, "mm_a_dwin_g", trans_a=True),
                                   _matmul([(h0, dxbr)], F32, "mm_a_dwin_x", trans_a=True)], axis=1)
    dx0, _, dgm0 = _matmul([(dgate, t(w_in_g)), (dxbr, t(w_in_x))], F32, "mm_a_dh",
                           norm_bwd=(x, w["norm_mix_g"][0], dx1))
    g["norm_mix_g"] = jnp.concatenate([dgm0, dgm1], axis=0)
    g["norm_ffn_g"] = jnp.concatenate([dgf0, dgf1], axis=0)
    g["ffn_w_gate"] = [dwg0, dwg1]
    g["ffn_w_up"] = [dwu0, dwu1]
    g["ffn_w_down"] = [dwd0, dwd1]
    return loss, dx0, g


WEIGHTS = ["norm_mix_g", "norm_ffn_g", "a_w_in", "a_conv_w", "a_conv_b", "a_w_r", "a_b_r", "a_w_i", "a_b_i",
           "a_lambda", "a_w_out", "b_w_qkv", "b_w_out", "ffn_w_gate", "ffn_w_up", "ffn_w_down", "final_g"]
BIG = [("a_w_in", 2), ("a_w_r", 2), ("a_w_i", 2), ("a_w_out", 1), ("b_w_qkv", 2), ("b_w_out", 1),
       ("ffn_w_gate", 2), ("ffn_w_up", 2), ("ffn_w_down", 1)]
LAYER1 = ["b_w_qkv", "b_w_out", "ffn_w_gate", "ffn_w_up", "ffn_w_down"]
LAYER0 = ["a_w_in", "a_w_r", "a_w_i", "a_w_out", "ffn_w_gate", "ffn_w_up", "ffn_w_down"]
RS_COLLECTIVE_IDS = {"chips1": 3, "chips0": 4, "sibling1": 5, "share1": 6}
GATHER_COLLECTIVE_IDS = (8, 7)
SMALL = ["norm_mix_g", "norm_ffn_g", "a_conv_w", "a_conv_b", "a_b_r", "a_b_i", "a_lambda", "final_g"]


def _split_chips(full, axis):
    if axis == 1:
        return full.reshape((N_CHIPS, 1, full.shape[1] // N_CHIPS) + full.shape[2:])
    return jnp.stack(jnp.split(full, N_CHIPS, axis=axis))


def _step(x, target, weights, moments_m, moments_v):
    chip = 2 * lax.axis_index("x") + lax.axis_index("y")
    core = lax.axis_index("c")
    axis_of = dict(BIG)
    full = {}
    for group, layer, tag, collective_id in ((LAYER0[:4], 0, "0a", None), (LAYER0[4:], 0, "0f", GATHER_COLLECTIVE_IDS[0]),
                                             (LAYER1, 1, "1", GATHER_COLLECTIVE_IDS[1])):
        shards = [weights[n][layer % weights[n].shape[0]].astype(CD) for n in group]
        packed = _pack_shards(shards)
        if collective_id is not None:
            packed, first_gathered = lax.optimization_barrier((packed, first_gathered))
        gathered = _allgather_chips(packed, "allgather_weights" + tag, collective_id)
        if collective_id is None:
            first_gathered = gathered
        for n, own, stack in zip(group, shards, _unpack_shards(gathered, [sh.shape for sh in shards])):
            joined = jnp.concatenate([jnp.where(chip == s, own, stack[s]) for s in range(N_CHIPS)],
                                     axis=axis_of[n] - 1)
            full.setdefault(n, {})[layer] = joined
    full = {n: (v[0] if n.startswith("a_") else v[1] if n.startswith("b_") else [v[0], v[1]]) for n, v in full.items()}
    cw_rows = jnp.zeros((N_CHIPS, CONV_W, RG_BW), F32)
    cw_rows = lax.dynamic_update_slice(cw_rows, jnp.where(core == 0, weights["a_conv_w"], 0.0), (chip, 0, 0))
    cw_all = _allreduce_small(cw_rows.reshape(-1, LANES), "allgather_conv_w").reshape(N_CHIPS, CONV_W, RG_BW)
    full["a_conv_w"] = jnp.concatenate([cw_all[s] for s in range(N_CHIPS)], axis=1)
    for n in ("norm_mix_g", "norm_ffn_g", "final_g"):
        full[n] = weights[n]
    for n in ("a_conv_b", "a_b_r", "a_b_i", "a_lambda"):
        full[n] = weights[n]
    loss, dx, grads = _local_step(x[0], target[0], full)
    small_parts = [grads[n].reshape(-1) for n in SMALL] + [loss.reshape(-1)]
    sizes = [p.shape[0] for p in small_parts]
    small = _allreduce_small(jnp.concatenate(small_parts).reshape(-1, LANES), "allreduce_small").reshape(-1)
    red, pos = {}, 0
    for n, sz in zip(SMALL + ["loss"], sizes):
        red[n] = small[pos:pos + sz]
        pos += sz
    loss_out = red["loss"][0]
    g_out = {}
    for n in SMALL:
        if n == "a_conv_w":
            g_out[n] = lax.dynamic_slice(red[n].reshape(CONV_W, D_RNN), (0, chip * RG_BW), (CONV_W, RG_BW)).reshape(
                weights[n].shape)
        else:
            g_out[n] = red[n].reshape(weights[n].shape)
    axis_of = dict(BIG)
    pieces = {}
    for group, layer, tag in ((LAYER1, 1, "1"), (LAYER0, 0, "0")):
        stacks, shapes = [], []
        for n in group:
            per_layer = isinstance(grads[n], list)
            gfull = grads[n][layer] if per_layer else grads[n]
            shard_shape = weights[n].shape[1:]
            gfull = gfull.reshape((1,) + gfull.shape)
            stacks.append(_split_chips(gfull, axis_of[n]).reshape(N_CHIPS, -1, PACK_COLS))
            shapes.append((1,) + tuple(shard_shape))
        gbuf = jnp.concatenate(stacks, axis=1)
        if layer == 1:
            from_sibling = _exchange_sibling_halves_async(gbuf, "rs_sibling" + tag, RS_COLLECTIVE_IDS["sibling1"])
        else:
            from_sibling = _exchange_sibling_halves(gbuf, "rs_sibling" + tag)
        chip_partial = _add_sibling(gbuf, from_sibling, core, "rs_add" + tag)
        from_chips = _scatter_to_chips_async(chip_partial, "rs_chips" + tag, RS_COLLECTIVE_IDS["chips" + tag])
        halves = _sum_slots(from_chips, chip_partial, chip, core, "rs_sum" + tag)
        if layer == 1:
            reduced = _share_halves_async(halves, "rs_share" + tag, RS_COLLECTIVE_IDS["share1"])
        else:
            reduced = _share_halves(halves, "rs_share" + tag)
        for n, piece in zip(group, _unpack_shards(reduced, shapes)):
            pieces.setdefault(n, {})[layer] = piece
    for n, _ in BIG:
        layers = pieces[n]
        g_out[n] = jnp.concatenate([layers[k] for k in sorted(layers)], axis=0)
    updates = {}
    for n, _ in BIG:
        update = _adamw_on_tiles if n == "b_w_out" else _adamw
        updates[n] = update(weights[n], g_out[n], moments_m[n], moments_v[n], "adamw_" + n)
    rows = lambda d: jnp.concatenate([d[n].reshape(-1, D_MODEL) for n in SMALL], axis=0)
    small_updates = _adamw(rows(weights), rows(g_out), rows(moments_m), rows(moments_v), "adamw_small")
    pos = 0
    for n in SMALL:
        nrows = weights[n].size // D_MODEL
        updates[n] = tuple(u[pos:pos + nrows].reshape(weights[n].shape) for u in small_updates)
        pos += nrows
    outs_g = [g_out[n] for n in WEIGHTS]
    outs_d, outs_m, outs_v = ([updates[n][k] for n in WEIGHTS] for k in range(3))
    return (loss_out, dx[None], *outs_g, *outs_d, *outs_m, *outs_v)


def kernel(x, norm_mix_g, norm_ffn_g, a_w_in, a_conv_w, a_conv_b, a_w_r, a_b_r, a_w_i, a_b_i, a_lambda, a_w_out, b_w_qkv, b_w_out, ffn_w_gate, ffn_w_up, ffn_w_down, final_g, loss_target, m_norm_mix_g, m_norm_ffn_g, m_a_w_in, m_a_conv_w, m_a_conv_b, m_a_w_r, m_a_b_r, m_a_w_i, m_a_b_i, m_a_lambda, m_a_w_out, m_b_w_qkv, m_b_w_out, m_ffn_w_gate, m_ffn_w_up, m_ffn_w_down, m_final_g, v_norm_mix_g, v_norm_ffn_g, v_a_w_in, v_a_conv_w, v_a_conv_b, v_a_w_r, v_a_b_r, v_a_w_i, v_a_b_i, v_a_lambda, v_a_w_out, v_b_w_qkv, v_b_w_out, v_ffn_w_gate, v_ffn_w_up, v_ffn_w_down, v_final_g):
    ws = [norm_mix_g, norm_ffn_g, a_w_in, a_conv_w, a_conv_b, a_w_r, a_b_r, a_w_i, a_b_i, a_lambda, a_w_out, b_w_qkv,
          b_w_out, ffn_w_gate, ffn_w_up, ffn_w_down, final_g]
    ms = [m_norm_mix_g, m_norm_ffn_g, m_a_w_in, m_a_conv_w, m_a_conv_b, m_a_w_r, m_a_b_r, m_a_w_i, m_a_b_i, m_a_lambda,
          m_a_w_out, m_b_w_qkv, m_b_w_out, m_ffn_w_gate, m_ffn_w_up, m_ffn_w_down, m_final_g]
    vs = [v_norm_mix_g, v_norm_ffn_g, v_a_w_in, v_a_conv_w, v_a_conv_b, v_a_w_r, v_a_b_r, v_a_w_i, v_a_b_i, v_a_lambda,
          v_a_w_out, v_b_w_qkv, v_b_w_out, v_ffn_w_gate, v_ffn_w_up, v_ffn_w_down, v_final_g]
    return _step(x, loss_target, dict(zip(WEIGHTS, ws)), dict(zip(WEIGHTS, ms)), dict(zip(WEIGHTS, vs)))
```

```python
import math

import jax
import jax.numpy as jnp
from jax import lax
from jax.experimental import pallas as pl
from jax.experimental.pallas import tpu as pltpu
from jax.experimental.pallas import tpu_sc as plsc

F32 = jnp.float32
CD = jnp.bfloat16

D_MODEL = 1024
D_RNN = 1024
RG_BLOCKS = 4
RG_BW = 256
CONV_W = 4
RG_C = 8.0
SB_HEADS = 16
SB_HEAD_DIM = 64
D_FF = 2816
RMS_EPS = 1e-6
N_CHIPS = 4
N_DEV = 8

ADAM_LR = 0.001
ADAM_B1 = 0.9
ADAM_B2 = 0.999
ADAM_EPS = 1e-08
ADAM_WD = 0.01
ADAM_STEP = 10

LANES = 128
VMEM_LIMIT = 56 * 1024 * 1024
MESH = pl.DeviceIdType.MESH


def _params(*sem):
    return pltpu.CompilerParams(dimension_semantics=sem, vmem_limit_bytes=VMEM_LIMIT)


def _pick(n, prefs):
    for p in prefs:
        if n % p == 0:
            return p
    return n


def _row_tile(rows):
    return max(d for d in range(16, 1025, 16) if rows % d == 0)


def _matmul(pairs, out_dtype, name, *, trans_a=False, a_lbm=False, b_lbm=False, out_lbm=False, addend=None,
            tm=512, tn=None, tk=None, norm_gain=None, norm_bwd=None, loss_head=None):
    a0, b0 = pairs[0]
    if trans_a:
        kdim = a0.shape[1] if a_lbm else a0.shape[0]
        m = a0.shape[0] * LANES if a_lbm else a0.shape[1]
    else:
        m = a0.shape[1] if a_lbm else a0.shape[0]
        kdim = a0.shape[0] * LANES if a_lbm else a0.shape[1]
    n = b0.shape[0] * LANES if b_lbm else b0.shape[1]
    if trans_a and m <= 1024:
        tm = m
    tm = _pick(m, (tm, 1408, 256, 128))
    tn = tn or _pick(n, (1408, 1024, 768, 512, 256, 128))
    tk = tk or _pick(kdim, (1024, 1408, 512, 256, 128))
    nk = kdim // tk
    npair = len(pairs)

    def cat(ref):
        return jnp.concatenate([ref[p] for p in range(ref.shape[0])], axis=-1)

    def body(*refs):
        ins = refs[: 2 * npair]
        pos = 2 * npair
        add_ref = None
        if addend is not None:
            add_ref = refs[pos]
            pos += 1
        gain_ref = x_ref = dxin_ref = None
        if norm_gain is not None:
            gain_ref = refs[pos]
            pos += 1
        if norm_bwd is not None:
            x_ref, gain_ref, dxin_ref = refs[pos:pos + 3]
            pos += 3
        if loss_head is not None:
            gain_ref, target_ref = refs[pos:pos + 2]
            pos += 2
        o_ref = refs[pos]
        extra_out = refs[pos + 1:-1]
        acc_ref = refs[-1]
        k = pl.program_id(2)

        @pl.when(k == 0)
        def _():
            acc_ref[...] = jnp.zeros_like(acc_ref)

        if norm_bwd is not None or loss_head is not None:
            @pl.when((k == 0) & (pl.program_id(0) == 0))
            def _():
                for ref in extra_out[1:]:
                    ref[...] = jnp.zeros_like(ref)

        acc = acc_ref[...]
        for p in range(npair):
            a = (cat(ins[2 * p]) if a_lbm else ins[2 * p][...]).astype(CD)
            b = (cat(ins[2 * p + 1]) if b_lbm else ins[2 * p + 1][...]).astype(CD)
            dims = (((0,), (0,)), ((), ())) if trans_a else (((1,), (0,)), ((), ()))
            acc = acc + lax.dot_general(a, b, dims, preferred_element_type=F32)
        acc_ref[...] = acc

        @pl.when(k == nk - 1)
        def _():
            res = acc_ref[...]
            if add_ref is not None:
                res = res + add_ref[...]
            if norm_gain is not None:
                rinv = lax.rsqrt(jnp.mean(res * res, axis=-1, keepdims=True) + RMS_EPS)
                extra_out[0][...] = (res * rinv * gain_ref[...]).astype(CD)
            if norm_bwd is not None:
                xv = x_ref[...]
                rinv = lax.rsqrt(jnp.mean(xv * xv, axis=-1, keepdims=True) + RMS_EPS)
                nrm = xv * rinv
                dn = res * gain_ref[...]
                extra_out[1][...] += jnp.sum(res * nrm, axis=0, keepdims=True)
                res = dxin_ref[...] + rinv * (dn - nrm * jnp.mean(dn * nrm, axis=-1, keepdims=True))
                extra_out[0][...] = res.astype(CD)
            if loss_head is not None:
                gv = gain_ref[...]
                rinv = lax.rsqrt(jnp.mean(res * res, axis=-1, keepdims=True) + RMS_EPS)
                nrm = res * rinv
                err = nrm * gv - target_ref[...]
                extra_out[2][...] += 0.5 * jnp.sum(jnp.mean(err * err, axis=-1, keepdims=True), axis=0, keepdims=True)
                dy = err * (1.0 / n)
                dn = dy * gv
                extra_out[1][...] += jnp.sum(dy * nrm, axis=0, keepdims=True)
                res = rinv * (dn - nrm * jnp.mean(dn * nrm, axis=-1, keepdims=True))
                extra_out[0][...] = res.astype(CD)
            res = res.astype(out_dtype)
            if out_lbm:
                for p in range(tn // LANES):
                    o_ref[p] = res[:, p * LANES:(p + 1) * LANES]
            else:
                o_ref[...] = res

    if trans_a:
        a_spec = (pl.BlockSpec((tm // LANES, tk, LANES), lambda i, j, k: (i, k, 0)) if a_lbm
                  else pl.BlockSpec((tk, tm), lambda i, j, k: (k, i)))
    else:
        a_spec = (pl.BlockSpec((tk // LANES, tm, LANES), lambda i, j, k: (k, i, 0)) if a_lbm
                  else pl.BlockSpec((tm, tk), lambda i, j, k: (i, k)))
    b_spec = (pl.BlockSpec((tn // LANES, tk, LANES), lambda i, j, k: (j, k, 0)) if b_lbm
              else pl.BlockSpec((tk, tn), lambda i, j, k: (k, j)))
    in_specs = []
    args = []
    for a, b in pairs:
        in_specs += [a_spec, b_spec]
        args += [a, b]
    if addend is not None:
        in_specs.append(pl.BlockSpec((tm, tn), lambda i, j, k: (i, j)))
        args.append(addend)
    tile = pl.BlockSpec((tm, tn), lambda i, j, k: (i, j))
    vec = pl.BlockSpec((1, tn), lambda i, j, k: (0, j))
    if out_lbm:
        out_shape = jax.ShapeDtypeStruct((n // LANES, m, LANES), out_dtype)
        out_spec = pl.BlockSpec((tn // LANES, tm, LANES), lambda i, j, k: (j, i, 0))
    else:
        out_shape = jax.ShapeDtypeStruct((m, n), out_dtype)
        out_spec = tile
    sem = ("parallel", "parallel", "arbitrary")
    if norm_gain is not None or norm_bwd is not None or loss_head is not None:
        assert tn == n and not out_lbm, "the norm needs whole rows in one tile"
        out_shape, out_spec = [out_shape, jax.ShapeDtypeStruct((m, n), CD)], [out_spec, tile]
    if norm_gain is not None:
        in_specs.append(vec)
        args.append(norm_gain.reshape(1, n))
    if norm_bwd is not None:
        x_in, gain, dx_in = norm_bwd
        in_specs += [tile, vec, tile]
        args += [x_in, gain.reshape(1, n), dx_in]
        out_shape.append(jax.ShapeDtypeStruct((1, n), F32))
        out_spec.append(vec)
        sem = ("arbitrary", "arbitrary", "arbitrary")
    if loss_head is not None:
        gain, target = loss_head
        in_specs += [vec, tile]
        args += [gain.reshape(1, n), target]
        out_shape += [jax.ShapeDtypeStruct((1, n), F32), jax.ShapeDtypeStruct((1, LANES), F32)]
        out_spec += [vec, pl.BlockSpec((1, LANES), lambda i, j, k: (0, 0))]
        sem = ("arbitrary", "arbitrary", "arbitrary")
    return pl.pallas_call(
        body, name=name, out_shape=out_shape, grid=(m // tm, n // tn, nk),
        in_specs=in_specs, out_specs=out_spec,
        scratch_shapes=[pltpu.VMEM((tm, tn), F32)],
        compiler_params=_params(*sem),
    )(*args)


def _norm_and_project(x, g, w_a, w_b, name):
    s, d = x.shape
    n = w_a.shape[1]
    tm = _pick(s, (512, 256))

    def body(x_ref, g_ref, wa_ref, wb_ref, h_ref, a_ref, b_ref):
        xv = x_ref[...]
        rinv = lax.rsqrt(jnp.mean(xv * xv, axis=-1, keepdims=True) + RMS_EPS)
        h = (xv * rinv * g_ref[...]).astype(CD)
        h_ref[...] = h
        a_ref[...] = jnp.dot(h, wa_ref[...], preferred_element_type=F32)
        b_ref[...] = jnp.dot(h, wb_ref[...], preferred_element_type=F32)

    row = pl.BlockSpec((tm, d), lambda i: (i, 0))
    out = pl.BlockSpec((tm, n), lambda i: (i, 0))
    wspec = pl.BlockSpec((d, n), lambda i: (0, 0))
    return pl.pallas_call(
        body, name=name,
        out_shape=(jax.ShapeDtypeStruct((s, d), CD), jax.ShapeDtypeStruct((s, n), F32),
                   jax.ShapeDtypeStruct((s, n), F32)),
        grid=(s // tm,), in_specs=[row, pl.BlockSpec((1, d), lambda i: (0, 0)), wspec, wspec],
        out_specs=(row, out, out), compiler_params=_params("parallel"),
    )(x, g.reshape(1, d), w_a, w_b)


def _sigmoid(z):
    return 1.0 / (1.0 + jnp.exp(-z))


FFN_TM = 512
FFN_TN = 1408


def _ffn_up(h, wg, wu, name):
    s, d = h.shape
    f = wg.shape[1]
    tm = _pick(s, (FFN_TM, 256))

    def body(h_ref, wg_ref, wu_ref, g_ref, u_ref, a_ref):
        hv = h_ref[...]
        gv = jnp.dot(hv, wg_ref[...], preferred_element_type=F32)
        uv = jnp.dot(hv, wu_ref[...], preferred_element_type=F32)
        g_ref[...] = gv
        u_ref[...] = uv
        a_ref[...] = (gv * _sigmoid(gv) * uv).astype(CD)

    a_spec = pl.BlockSpec((tm, d), lambda j, i: (i, 0))
    w_spec = pl.BlockSpec((d, FFN_TN), lambda j, i: (0, j))
    o_spec = pl.BlockSpec((tm, FFN_TN), lambda j, i: (i, j))
    return pl.pallas_call(
        body, name=name,
        out_shape=(jax.ShapeDtypeStruct((s, f), F32), jax.ShapeDtypeStruct((s, f), F32),
                   jax.ShapeDtypeStruct((s, f), CD)),
        grid=(f // FFN_TN, s // tm), in_specs=[a_spec, w_spec, w_spec], out_specs=(o_spec, o_spec, o_spec),
        compiler_params=_params("parallel", "parallel"),
    )(h, wg, wu)


def _ffn_dact(dxc, wd_t, g, u, name):
    s, d = dxc.shape
    f = wd_t.shape[1]
    tm = _pick(s, (FFN_TM, 256))

    def body(dx_ref, w_ref, g_ref, u_ref, dg_ref, du_ref):
        da = jnp.dot(dx_ref[...], w_ref[...], preferred_element_type=F32)
        gv = g_ref[...]
        sg = _sigmoid(gv)
        silu = gv * sg
        dg_ref[...] = (da * u_ref[...] * (sg + silu * (1.0 - sg))).astype(CD)
        du_ref[...] = (da * silu).astype(CD)

    a_spec = pl.BlockSpec((tm, d), lambda j, i: (i, 0))
    w_spec = pl.BlockSpec((d, FFN_TN), lambda j, i: (0, j))
    o_spec = pl.BlockSpec((tm, FFN_TN), lambda j, i: (i, j))
    return pl.pallas_call(
        body, name=name,
        out_shape=(jax.ShapeDtypeStruct((s, f), CD), jax.ShapeDtypeStruct((s, f), CD)),
        grid=(f // FFN_TN, s // tm), in_specs=[a_spec, w_spec, o_spec, o_spec], out_specs=(o_spec, o_spec),
        compiler_params=_params("parallel", "parallel"),
    )(dxc, wd_t, g, u)


TIME_BLOCK = 2048
SUBLANES = 8
GELU_C = math.sqrt(2.0 / math.pi)
GELU_A = 0.044715


def _gelu(x):
    return 0.5 * x * (1.0 + jnp.tanh(GELU_C * (x + GELU_A * x * x * x)))


def _gelu_grad(x):
    t = jnp.tanh(GELU_C * (x + GELU_A * x * x * x))
    return 0.5 * (1.0 + t) + 0.5 * x * (1.0 - t * t) * GELU_C * (1.0 + 3.0 * GELU_A * x * x)


def _neg_expm1(x):
    series = -x * (1.0 + x * (0.5 + x * (1.0 / 6.0 + x * (1.0 / 24.0))))
    return jnp.where(x > -0.05, series, 1.0 - jnp.exp(x))


def _log_sigmoid(x):
    return jnp.minimum(x, 0.0) - jnp.log1p(jnp.exp(-jnp.abs(x)))


def _shift_down(x, tail, s):
    if s == 0:
        return x
    ext = jnp.concatenate([tail, x], axis=0)
    return pltpu.roll(ext, s, axis=0)[SUBLANES:]


def _shift_up(x, head, s):
    if s == 0:
        return x
    n = x.shape[0]
    ext = jnp.concatenate([x, head], axis=0)
    return pltpu.roll(ext, n + SUBLANES - s, axis=0)[:n]


def _rg_gates(xbr, tail, cw_ref, cb, wr, wi, br, bi, ls):
    taps = [_shift_down(xbr, tail, CONV_W - 1 - k) for k in range(CONV_W)]
    xc = cb
    for k in range(CONV_W):
        xc = xc + cw_ref[pl.ds(k, 1), :] * taps[k]
    xcd = xc.astype(CD)
    r = _sigmoid(jnp.dot(xcd, wr, preferred_element_type=F32) + br)
    i = _sigmoid(jnp.dot(xcd, wi, preferred_element_type=F32) + bi)
    log_a = RG_C * r * ls
    a = jnp.exp(log_a)
    mult = jnp.sqrt(jnp.maximum(_neg_expm1(2.0 * log_a), 0.0))
    return taps, xc, r, i, log_a, a, mult


def _scan8_fwd(a, u):
    row = lax.broadcasted_iota(jnp.int32, a.shape, 0)
    for d in (1, 2, 4):
        a_s = pltpu.roll(a, d, axis=0)
        u_s = pltpu.roll(u, d, axis=0)
        m = row >= d
        u = jnp.where(m, a * u_s + u, u)
        a = jnp.where(m, a * a_s, a)
    return a, u


def _scan8_bwd(b, u):
    row = lax.broadcasted_iota(jnp.int32, b.shape, 0)
    for d in (1, 2, 4):
        b_s = pltpu.roll(b, SUBLANES - d, axis=0)
        u_s = pltpu.roll(u, SUBLANES - d, axis=0)
        m = row < SUBLANES - d
        u = jnp.where(m, b * u_s + u, u)
        b = jnp.where(m, b * b_s, b)
    return b, u


def _rglru_fwd(gate_br, x_br, cw, cb, wr, wi, br, bi, lam, name):
    s, c = x_br.shape
    nt = s // TIME_BLOCK
    tb, cbw = TIME_BLOCK, RG_BW
    groups = tb // SUBLANES

    def body(g_ref, x_ref, tail_ref, cw_ref, cb_ref, wr_ref, wi_ref, br_ref, bi_ref, lam_ref,
             y_ref, hs_ref, carry_ref, a_scr, u_scr):
        t = pl.program_id(1)

        @pl.when(t == 0)
        def _():
            carry_ref[...] = jnp.zeros_like(carry_ref)

        tail = jnp.where(t > 0, tail_ref[...], 0.0)
        ls = _log_sigmoid(lam_ref[...])
        _, xc, _, i, _, a, mult = _rg_gates(x_ref[...], tail, cw_ref, cb_ref[...], wr_ref[0], wi_ref[0],
                                            br_ref[...], bi_ref[...], ls)
        a_scr[...] = a
        u_scr[...] = mult * (i * xc)
        carry = carry_ref[...]
        for gi in range(groups):
            rows = pl.ds(gi * SUBLANES, SUBLANES)
            pa, hl = _scan8_fwd(a_scr[rows, :], u_scr[rows, :])
            hs_ref[rows, :] = hl + pa * carry
            carry = hs_ref[pl.ds(gi * SUBLANES + SUBLANES - 1, 1), :]
        carry_ref[...] = carry
        y_ref[...] = (hs_ref[...] * _gelu(g_ref[...])).astype(CD)

    blk = pl.BlockSpec((tb, cbw), lambda n, t: (t, n))
    tail = pl.BlockSpec((SUBLANES, cbw), lambda n, t: (jnp.maximum(t * groups - 1, 0), n))
    vec = pl.BlockSpec((1, cbw), lambda n, t: (0, n))
    wblk = pl.BlockSpec((1, cbw, cbw), lambda n, t: (n, 0, 0))
    return pl.pallas_call(
        body, name=name,
        out_shape=(jax.ShapeDtypeStruct((s, c), CD), jax.ShapeDtypeStruct((s, c), F32)),
        grid=(RG_BLOCKS, nt),
        in_specs=[blk, blk, tail, pl.BlockSpec((CONV_W, cbw), lambda n, t: (0, n)), vec, wblk, wblk, vec, vec, vec],
        out_specs=(blk, blk),
        scratch_shapes=[pltpu.VMEM((1, cbw), F32), pltpu.VMEM((tb, cbw), F32), pltpu.VMEM((tb, cbw), F32)],
        compiler_params=_params("parallel", "arbitrary"),
    )(gate_br, x_br, x_br, cw, cb, wr, wi, br, bi, lam)


def _rglru_bwd(dy, gate_br, x_br, hs, cw, cb, wr, wi, wrt, wit, br, bi, lam, name):
    s, c = x_br.shape
    nt = s // TIME_BLOCK
    tb, cbw = TIME_BLOCK, RG_BW
    groups = tb // SUBLANES

    def body(dy_ref, g_ref, x_ref, tail_ref, hs_ref, hprev_ref, cw_ref, cb_ref, wr_ref, wi_ref, wrt_ref, wit_ref,
             br_ref, bi_ref, lam_ref,
             dg_ref, dx_ref, dcw_ref, dcb_ref, dbr_ref, dbi_ref, dlam_ref, dwr_ref, dwi_ref,
             carry_ref, head_ref, b_scr, u_scr, dh_scr):
        tr = pl.program_id(1)
        first_block = tr == nt - 1

        @pl.when(tr == 0)
        def _():
            carry_ref[...] = jnp.zeros_like(carry_ref)
            head_ref[...] = jnp.zeros_like(head_ref)
            for ref in (dcw_ref, dcb_ref, dbr_ref, dbi_ref, dlam_ref, dwr_ref, dwi_ref):
                ref[...] = jnp.zeros_like(ref)

        tail = jnp.where(first_block, 0.0, tail_ref[...])
        lam_v = lam_ref[...]
        ls = _log_sigmoid(lam_v)
        taps, xc, r, i, log_a, a, mult = _rg_gates(x_ref[...], tail, cw_ref, cb_ref[...], wr_ref[0], wi_ref[0],
                                                   br_ref[...], bi_ref[...], ls)
        gate_v = g_ref[...]
        dyv = dy_ref[...]
        hsv = hs_ref[...]
        dg_ref[...] = (dyv * hsv * _gelu_grad(gate_v)).astype(CD)

        row = lax.broadcasted_iota(jnp.int32, a.shape, 0)
        b_scr[...] = jnp.where(row == tb - 1, 1.0, pltpu.roll(a, tb - 1, axis=0))
        u_scr[...] = dyv * _gelu(gate_v)
        carry = carry_ref[...]
        for gi in reversed(range(groups)):
            rows = pl.ds(gi * SUBLANES, SUBLANES)
            pb, gl = _scan8_bwd(b_scr[rows, :], u_scr[rows, :])
            dh_scr[rows, :] = gl + pb * carry
            carry = dh_scr[pl.ds(gi * SUBLANES, 1), :]
        dh = dh_scr[...]
        carry_ref[...] = carry * jnp.sum(jnp.where(row == 0, a, 0.0), axis=0, keepdims=True)

        hprev_tail = jnp.where(first_block, 0.0, hprev_ref[...])
        h_prev = _shift_down(hsv, hprev_tail, 1)
        da = dh * h_prev
        ixc = i * xc
        dmult = dh * ixc
        di = dh * mult * xc
        dxc = dh * mult * i
        a2 = a * a
        dlog_a = da * a - dmult * a2 / mult
        dpre_r = (dlog_a * (RG_C * ls)) * r * (1.0 - r)
        dpre_i = di * i * (1.0 - i)
        dlam_ref[...] += jnp.sum(dlog_a * r, axis=0, keepdims=True) * (RG_C * _sigmoid(-lam_v))
        dbr_ref[...] += jnp.sum(dpre_r, axis=0, keepdims=True)
        dbi_ref[...] += jnp.sum(dpre_i, axis=0, keepdims=True)
        xcd = xc.astype(CD)
        dprc = dpre_r.astype(CD)
        dpic = dpre_i.astype(CD)
        tn_dims = (((0,), (0,)), ((), ()))
        dwr_ref[0] += lax.dot_general(xcd, dprc, tn_dims, preferred_element_type=F32)
        dwi_ref[0] += lax.dot_general(xcd, dpic, tn_dims, preferred_element_type=F32)
        dxc = dxc + jnp.dot(dprc, wrt_ref[0], preferred_element_type=F32) + jnp.dot(dpic, wit_ref[0],
                                                                                    preferred_element_type=F32)
        dcb_ref[...] += jnp.sum(dxc, axis=0, keepdims=True)
        for k in range(CONV_W):
            dcw_ref[pl.ds(k, 1), :] += jnp.sum(dxc * taps[k], axis=0, keepdims=True)
        head = head_ref[...]
        dxb = jnp.zeros_like(dxc)
        for sft in range(CONV_W):
            dxb = dxb + cw_ref[pl.ds(CONV_W - 1 - sft, 1), :] * _shift_up(dxc, head, sft)
        dx_ref[...] = dxb.astype(CD)
        head_ref[...] = dxc[0:SUBLANES, :]

    blk = pl.BlockSpec((tb, cbw), lambda n, t: (nt - 1 - t, n))
    tail = pl.BlockSpec((SUBLANES, cbw), lambda n, t: (jnp.maximum((nt - 1 - t) * groups - 1, 0), n))
    vec = pl.BlockSpec((1, cbw), lambda n, t: (0, n))
    cwb = pl.BlockSpec((CONV_W, cbw), lambda n, t: (0, n))
    wblk = pl.BlockSpec((1, cbw, cbw), lambda n, t: (n, 0, 0))
    vshape = jax.ShapeDtypeStruct((1, c), F32)
    wshape = jax.ShapeDtypeStruct((RG_BLOCKS, cbw, cbw), F32)
    return pl.pallas_call(
        body, name=name,
        out_shape=(jax.ShapeDtypeStruct((s, c), CD), jax.ShapeDtypeStruct((s, c), CD),
                   jax.ShapeDtypeStruct((CONV_W, c), F32), vshape, vshape, vshape, vshape, wshape, wshape),
        grid=(RG_BLOCKS, nt),
        in_specs=[blk, blk, blk, tail, blk, tail, cwb, vec, wblk, wblk, wblk, wblk, vec, vec, vec],
        out_specs=(blk, blk, cwb, vec, vec, vec, vec, wblk, wblk),
        scratch_shapes=[pltpu.VMEM((1, cbw), F32), pltpu.VMEM((SUBLANES, cbw), F32),
                        pltpu.VMEM((tb, cbw), F32), pltpu.VMEM((tb, cbw), F32), pltpu.VMEM((tb, cbw), F32)],
        compiler_params=_params("parallel", "arbitrary"),
    )(dy, gate_br, x_br, x_br, hs, hs, cw, cb, wr, wi, wrt, wit, br, bi, lam)


ATT_BLOCK = 256
ATT_Q_BLOCK = 1024
ATT_Q_BLOCK_FWD = 2048
ATT_RATIO = ATT_Q_BLOCK // ATT_BLOCK
ATT_SCALE = 1.0 / math.sqrt(SB_HEAD_DIM)
N_PAIRS = SB_HEADS * SB_HEAD_DIM // LANES
NT_DIMS = (((1,), (1,)), ((), ()))
TN_DIMS = (((0,), (0,)), ((), ()))


LOG2E = 1.4426950408889634


def _neg_abs(x):
    bits = lax.bitcast_convert_type(x, jnp.uint32) | jnp.uint32(0x80000000)
    return lax.bitcast_convert_type(bits, F32)


def _qk(qx, kb):
    return lax.dot_general(qx, kb, NT_DIMS, preferred_element_type=F32)


def _sb_logits(qk, valid):
    z2 = qk * (ATT_SCALE * LOG2E)
    lb2 = jnp.minimum(z2, 0.0) - jnp.log2(1.0 + jnp.exp2(_neg_abs(z2)))
    l2 = lb2 - z2
    if valid is not None:
        l2 = jnp.where(valid, l2, 0.0)
    return lb2, l2


def _hi_lo(x):
    hi = x.astype(CD)
    lo = (x - hi.astype(F32)).astype(CD)
    return jnp.concatenate([hi, lo], axis=1)


def _tri(strict, stacked):
    r = lax.broadcasted_iota(jnp.int32, (ATT_BLOCK, ATT_BLOCK), 0)
    c = lax.broadcasted_iota(jnp.int32, (ATT_BLOCK, ATT_BLOCK), 1)
    m = (r > c if strict else r >= c).astype(CD)
    return jnp.concatenate([m, m], axis=0) if stacked else m


def _attn_fwd(qkv, name):
    _, s, _ = qkv.shape
    tq, t = _pick(s, (ATT_Q_BLOCK_FWD, ATT_Q_BLOCK)), ATT_BLOCK
    ratio = tq // t
    nblk = s // tq

    def body(q_ref, k_ref, v_ref, o_ref, qk_scr, w_scr):
        i = pl.program_id(1)
        lane = lax.broadcasted_iota(jnp.int32, (1, LANES), 1)
        head_masks = (lane < SB_HEAD_DIM, lane >= SB_HEAD_DIM)
        q = q_ref[0]
        qs = [jnp.where(m, q, jnp.zeros_like(q)) for m in head_masks]
        tri = _tri(True, False)
        rr = lax.broadcasted_iota(jnp.int32, (tq, t), 0)
        cc = lax.broadcasted_iota(jnp.int32, (tq, t), 1)

        def rows_of(j):
            return pl.ds(pl.multiple_of(j * t, t), t)

        def tail(x, row0):
            return x if row0 == 0 else x[row0:]

        def start_logits(j, row0=0):
            kb = k_ref[0, rows_of(j), :]
            for hd in range(2):
                qk_scr[hd, row0:, :] = _qk(tail(qs[hd], row0), kb)

        def weights(run, diagonal=False, row0=0):
            new_run = []
            valid = (cc < rr)[:tq - row0] if diagonal else None
            for hd in range(2):
                lb2, l2 = _sb_logits(qk_scr[hd, row0:, :], valid)
                w = jnp.exp2(lb2 + (tail(run[hd], row0) + jnp.dot(l2.astype(CD), tri, preferred_element_type=F32)))
                if valid is not None:
                    w = jnp.where(valid, w, 0.0)
                w_scr[row0:, hd * t:(hd + 1) * t] = w.astype(CD)
                rowsum = jnp.sum(l2, axis=1, keepdims=True)
                if row0:
                    rowsum = jnp.concatenate([jnp.zeros((row0, 1), F32), rowsum], axis=0)
                new_run.append(run[hd] + rowsum)
            return tuple(new_run)

        def apply_weights(j, row0=0):
            vb = v_ref[0, rows_of(j), :]
            vcat = jnp.concatenate([jnp.where(m, vb, jnp.zeros_like(vb)) for m in head_masks], axis=0)
            inc = jnp.dot(w_scr[row0:, :], vcat, preferred_element_type=F32)
            return inc if row0 == 0 else jnp.concatenate([jnp.zeros((row0, LANES), F32), inc], axis=0)

        zero = jnp.zeros((tq, 1), F32)
        last = ratio - 1
        start_logits(ratio * i + last, last * t)
        run = weights((zero, zero), True, last * t)
        oacc = jnp.zeros((tq, LANES), F32)
        for d in reversed(range(last)):
            start_logits(ratio * i + d, d * t)
            oacc = oacc + apply_weights(ratio * i + d + 1, (d + 1) * t)
            run = weights(run, True, d * t)
        start_logits(jnp.maximum(ratio * i - 1, 0))

        def step(jj, carry):
            run, oacc = carry
            b = ratio * i - 1 - jj
            oacc = oacc + apply_weights(b + 1)
            run = weights(run)
            start_logits(jnp.maximum(b - 1, 0))
            return run, oacc

        run, oacc = lax.fori_loop(0, ratio * i, step, (run, oacc))
        o_ref[0] = oacc + apply_weights(0)

    return pl.pallas_call(
        body, name=name, out_shape=jax.ShapeDtypeStruct((N_PAIRS, s, LANES), F32), grid=(N_PAIRS, nblk),
        in_specs=[pl.BlockSpec((1, tq, LANES), lambda p, i: (p, i, 0)),
                  pl.BlockSpec((1, s, LANES), lambda p, i: (N_PAIRS + p, 0, 0)),
                  pl.BlockSpec((1, s, LANES), lambda p, i: (2 * N_PAIRS + p, 0, 0))],
        out_specs=pl.BlockSpec((1, tq, LANES), lambda p, i: (p, i, 0)),
        scratch_shapes=[pltpu.VMEM((2, tq, t), F32), pltpu.VMEM((tq, 2 * t), CD)],
        compiler_params=_params("parallel", "arbitrary"),
    )(qkv, qkv, qkv)


def _attn_bwd(qkv, o, do, name):
    _, s, _ = qkv.shape
    tq, t = ATT_Q_BLOCK, ATT_BLOCK
    nblk = s // tq

    def body(q_ref, k_ref, v_ref, o_ref, do_ref, dq_ref, dk_ref, dv_ref, qk_scr, dw_scr, w_scr, dz_scr):
        i = pl.program_id(1)

        @pl.when(i == 0)
        def _():
            dk_ref[...] = jnp.zeros_like(dk_ref)
            dv_ref[...] = jnp.zeros_like(dv_ref)

        lane = lax.broadcasted_iota(jnp.int32, (1, LANES), 1)
        head_masks = (lane < SB_HEAD_DIM, lane >= SB_HEAD_DIM)
        q = q_ref[0]
        dov = do_ref[0]
        ov = o_ref[0]
        qs = [jnp.where(m, q, jnp.zeros_like(q)) for m in head_masks]
        q_scaled_t = jnp.concatenate([(qx.astype(F32) * ATT_SCALE).T for qx in qs], axis=1).astype(CD)
        docs = [jnp.where(m, dov, jnp.zeros_like(dov)) for m in head_masks]
        docat_t = jnp.concatenate([d.astype(F32).T for d in docs], axis=1).astype(CD)
        totals = [jnp.sum(d.astype(F32) * ov, axis=1, keepdims=True) for d in docs]
        tri = _tri(True, False)
        tri_incl = _tri(False, True)
        rr = lax.broadcasted_iota(jnp.int32, (tq, t), 0)
        cc = lax.broadcasted_iota(jnp.int32, (tq, t), 1)

        def rows_of(j):
            return pl.ds(pl.multiple_of(j * t, t), t)

        def tail(x, row0):
            return x if row0 == 0 else x[row0:]

        def pad_rows(x, row0):
            return x if row0 == 0 else jnp.concatenate([jnp.zeros((row0, x.shape[1]), x.dtype), x], axis=0)

        def start_products(j, row0=0):
            kb = k_ref[0, rows_of(j), :]
            vb = v_ref[0, rows_of(j), :]
            for hd in range(2):
                qk_scr[hd, row0:, :] = _qk(tail(qs[hd], row0), kb)
                dw_scr[hd, row0:, :] = lax.dot_general(tail(docs[hd], row0), vb, NT_DIMS, preferred_element_type=F32)

        def logit_grads(run, erun, diagonal=False, row0=0):
            new_run, new_erun = [], []
            valid = (cc < rr)[:tq - row0] if diagonal else None
            for hd in range(2):
                lb2, l2 = _sb_logits(qk_scr[hd, row0:, :], valid)
                w = jnp.exp2(lb2 + (tail(run[hd], row0) + jnp.dot(l2.astype(CD), tri, preferred_element_type=F32)))
                if valid is not None:
                    w = jnp.where(valid, w, 0.0)
                wc = w.astype(CD)
                w_scr[hd * tq + row0:(hd + 1) * tq, :] = wc
                e = dw_scr[hd, row0:, :] * wc.astype(F32)
                prefix = (tail(totals[hd] - erun[hd], row0)
                          - jnp.dot(_hi_lo(e), tri_incl, preferred_element_type=F32))
                dz = e - jnp.exp2(lb2) * (e + prefix)
                if valid is not None:
                    dz = jnp.where(valid, dz, 0.0)
                dz_scr[hd * tq + row0:(hd + 1) * tq, :] = dz.astype(CD)
                new_run.append(run[hd] + pad_rows(jnp.sum(l2, axis=1, keepdims=True), row0))
                new_erun.append(erun[hd] + pad_rows(jnp.sum(e, axis=1, keepdims=True), row0))
            return tuple(new_run), tuple(new_erun)

        def apply_grads(j, row0=0):
            rows = rows_of(j)
            kb = k_ref[0, rows, :]
            kcat = jnp.concatenate([jnp.where(m, kb, jnp.zeros_like(kb)) for m in head_masks], axis=0)
            dz_heads = [dz_scr[hd * tq + row0:(hd + 1) * tq, :] for hd in range(2)]
            w_heads = [w_scr[hd * tq + row0:(hd + 1) * tq, :] for hd in range(2)]
            q_t = jnp.concatenate([q_scaled_t[:, hd * tq + row0:(hd + 1) * tq] for hd in range(2)], axis=1)
            do_t = jnp.concatenate([docat_t[:, hd * tq + row0:(hd + 1) * tq] for hd in range(2)], axis=1)
            dk_ref[0, :, rows] += jnp.dot(q_t, jnp.concatenate(dz_heads, axis=0), preferred_element_type=F32)
            dv_ref[0, :, rows] += jnp.dot(do_t, jnp.concatenate(w_heads, axis=0), preferred_element_type=F32)
            return pad_rows(jnp.dot(jnp.concatenate(dz_heads, axis=1), kcat, preferred_element_type=F32), row0)

        zero = jnp.zeros((tq, 1), F32)
        last = ATT_RATIO - 1
        start_products(ATT_RATIO * i + last, last * t)
        run, erun = logit_grads((zero, zero), (zero, zero), True, last * t)
        dqacc = jnp.zeros((tq, LANES), F32)
        for d in reversed(range(last)):
            start_products(ATT_RATIO * i + d, d * t)
            dqacc = dqacc + apply_grads(ATT_RATIO * i + d + 1, (d + 1) * t)
            run, erun = logit_grads(run, erun, True, d * t)
        start_products(jnp.maximum(ATT_RATIO * i - 1, 0))

        def step(jj, carry):
            run, erun, dqacc = carry
            b = ATT_RATIO * i - 1 - jj
            dqacc = dqacc + apply_grads(b + 1)
            run, erun = logit_grads(run, erun)
            start_products(jnp.maximum(b - 1, 0))
            return run, erun, dqacc

        run, erun, dqacc = lax.fori_loop(0, ATT_RATIO * i, step, (run, erun, dqacc))
        dq_ref[0] = ((dqacc + apply_grads(0)) * ATT_SCALE).astype(CD)

    qblk = pl.BlockSpec((1, tq, LANES), lambda p, i: (p, i, 0))
    full = pl.BlockSpec((1, LANES, s), lambda p, i: (p, 0, 0))
    shape = jax.ShapeDtypeStruct((N_PAIRS, s, LANES), F32)
    shape_t = jax.ShapeDtypeStruct((N_PAIRS, LANES, s), F32)
    dq, dk_t, dv_t = pl.pallas_call(
        body, name=name, out_shape=(jax.ShapeDtypeStruct(shape.shape, CD), shape_t, shape_t), grid=(N_PAIRS, nblk),
        in_specs=[qblk,
                  pl.BlockSpec((1, s, LANES), lambda p, i: (N_PAIRS + p, 0, 0)),
                  pl.BlockSpec((1, s, LANES), lambda p, i: (2 * N_PAIRS + p, 0, 0)),
                  qblk, qblk],
        out_specs=(qblk, full, full),
        scratch_shapes=[pltpu.VMEM((2, tq, t), F32), pltpu.VMEM((2, tq, t), F32),
                        pltpu.VMEM((2 * tq, t), CD), pltpu.VMEM((2 * tq, t), CD)],
        compiler_params=_params("parallel", "arbitrary"),
    )(qkv, qkv, qkv, o, do)
    return dq, jnp.swapaxes(dk_t, 1, 2).astype(CD), jnp.swapaxes(dv_t, 1, 2).astype(CD)


def _adamw(w, g, m, v, name):
    shape = w.shape
    rows, cols = (shape[-2], shape[-1]) if len(shape) >= 2 else (1, shape[-1])
    lead = w.size // (rows * cols)
    tr = _pick(rows, (512, 256, 128, 64, 32, 16, 8))

    def body(w_ref, g_ref, m_ref, v_ref, d_ref, nm_ref, nv_ref):
        gv = g_ref[...]
        nm = ADAM_B1 * m_ref[...] + (1.0 - ADAM_B1) * gv
        nv = ADAM_B2 * v_ref[...] + (1.0 - ADAM_B2) * (gv * gv)
        m_hat = nm / (1.0 - ADAM_B1 ** ADAM_STEP)
        v_hat = nv / (1.0 - ADAM_B2 ** ADAM_STEP)
        d_ref[...] = -ADAM_LR * (m_hat / (jnp.sqrt(v_hat) + ADAM_EPS) + ADAM_WD * w_ref[...])
        nm_ref[...] = nm
        nv_ref[...] = nv

    blk = pl.BlockSpec((1, tr, cols), lambda l, i: (l, i, 0))
    out = jax.ShapeDtypeStruct((lead, rows, cols), F32)
    d, nm, nv = pl.pallas_call(
        body, name=name, out_shape=(out, out, out), grid=(lead, rows // tr),
        in_specs=[blk, blk, blk, blk], out_specs=(blk, blk, blk), compiler_params=_params("parallel", "parallel"),
    )(*[a.reshape(lead, rows, cols) for a in (w, g, m, v)])
    return d.reshape(shape), nm.reshape(shape), nv.reshape(shape)


HBM = pl.BlockSpec(memory_space=pltpu.HBM)


def _coords():
    return lax.axis_index("x"), lax.axis_index("y"), lax.axis_index("c")


def _other_chips(x, y):
    return [(1 - x, y), (x, 1 - y), (1 - x, 1 - y)]


def _allgather_chips(shard, name, collective_id=None):
    r, cols = shard.shape
    half = r // 2
    quarter = half // 2

    def body(src_ref, out_ref, send_sems, recv_sems):
        x, y, c = _coords()
        sibling = (x, y, 1 - c)
        nx, ny, diag = (1 - x, y), (x, 1 - y), (1 - x, 1 - y)

        def piece(chip, core, lo, n):
            return out_ref.at[2 * chip[0] + chip[1], pl.ds(core * half + lo, n), :]

        def copy(k, dst, to, src=None):
            return pltpu.make_async_remote_copy(
                src_ref=dst if src is None else src, dst_ref=dst,
                send_sem=send_sems.at[k], recv_sem=recv_sems.at[k], device_id=to, device_id_type=MESH)

        me = (x, y)
        mine = src_ref.at[pl.ds(c * half, half), :]
        direct = [copy(0, piece(me, c, 0, half), (*nx, c), src=mine), copy(1, piece(me, c, 0, half), (*ny, c), src=mine)]
        for cp in direct:
            cp.start()
        arrivals = [piece(nx, c, 0, half), piece(ny, c, 0, half), piece(diag, c, 0, quarter),
                    piece(diag, c, quarter, quarter)]
        onward = [copy(2, piece(nx, c, 0, quarter), (*ny, c)), copy(3, piece(ny, c, quarter, quarter), (*nx, c))]
        to_sibling = [copy(4 + k, dst, sibling) for k, dst in enumerate(arrivals)]
        for k, dst in enumerate(arrivals):
            copy(k, dst, (x, y, c)).wait_recv()
            if k < 2:
                onward[k].start()
            to_sibling[k].start()
        from_sibling = [piece(nx, 1 - c, 0, half), piece(ny, 1 - c, 0, half), piece(diag, 1 - c, 0, quarter),
                        piece(diag, 1 - c, quarter, quarter)]
        for k, dst in enumerate(from_sibling):
            copy(4 + k, dst, (x, y, c)).wait_recv()
        for cp in direct + onward + to_sibling:
            cp.wait_send()

    out_shape = jax.ShapeDtypeStruct((N_CHIPS, r, cols), shard.dtype)
    sems = (pltpu.SemaphoreType.DMA((8,)), pltpu.SemaphoreType.DMA((8,)))
    if collective_id is None:
        return pl.pallas_call(body, name=name, out_shape=out_shape, in_specs=[HBM], out_specs=HBM,
                              scratch_shapes=list(sems))(shard)
    shard_ref = jax.new_ref(shard, memory_space=pltpu.MemorySpace.HBM)
    gathered_ref = jax.empty_ref(out_shape, memory_space=pltpu.MemorySpace.HBM)

    @_sequencer(name, collective_id, sems)
    def launch(send_sems, recv_sems):
        x, y, c = _coords()
        _handshake([(1 - x, y, c), (x, 1 - y, c), (x, y, 1 - c)])
        body(shard_ref, gathered_ref, send_sems, recv_sems)

    launch()
    return gathered_ref[...]


def _exchange_sibling_halves(g, name):
    n, r, cols = g.shape
    half = r // 2

    def body(g_ref, out_ref, send_sem, recv_sem):
        x, y, c = _coords()
        cp = pltpu.make_async_remote_copy(
            src_ref=g_ref.at[:, pl.ds((1 - c) * half, half), :], dst_ref=out_ref,
            send_sem=send_sem, recv_sem=recv_sem, device_id=(x, y, 1 - c), device_id_type=MESH)
        cp.start()
        cp.wait()

    return pl.pallas_call(
        body, name=name, out_shape=jax.ShapeDtypeStruct((n, half, cols), g.dtype),
        in_specs=[HBM], out_specs=HBM,
        scratch_shapes=[pltpu.SemaphoreType.DMA, pltpu.SemaphoreType.DMA],
    )(g)


def _sequencer(name, collective_id, scratch_types):
    return pl.kernel(mesh=plsc.ScalarSubcoreMesh(axis_name="sequencer", num_cores=1), name=name,
                     scratch_types=scratch_types, compiler_params=pltpu.CompilerParams(collective_id=collective_id))


def _handshake(peers):
    barrier = pltpu.get_barrier_semaphore()
    for peer in peers:
        pl.semaphore_signal(barrier, inc=1, device_id=peer, device_id_type=MESH)
    pl.semaphore_wait(barrier, len(peers))


def _exchange_sibling_halves_async(g, name, collective_id):
    n, r, cols = g.shape
    half = r // 2
    g_ref = jax.new_ref(g, memory_space=pltpu.MemorySpace.HBM)
    out_ref = jax.empty_ref(jax.ShapeDtypeStruct((n, half, cols), g.dtype), memory_space=pltpu.MemorySpace.HBM)

    @_sequencer(name, collective_id, (pltpu.SemaphoreType.DMA, pltpu.SemaphoreType.DMA))
    def launch(send_sem, recv_sem):
        x, y, c = _coords()
        _handshake([(x, y, 1 - c)])
        cp = pltpu.make_async_remote_copy(
            src_ref=g_ref.at[:, pl.ds((1 - c) * half, half), :], dst_ref=out_ref,
            send_sem=send_sem, recv_sem=recv_sem, device_id=(x, y, 1 - c), device_id_type=MESH)
        cp.start()
        cp.wait()

    launch()
    return out_ref[...]


def _share_halves_async(v, name, collective_id):
    h = v.shape[0] // 2
    v_ref = jax.new_ref(v, memory_space=pltpu.MemorySpace.HBM)

    @_sequencer(name, collective_id, (pltpu.SemaphoreType.DMA, pltpu.SemaphoreType.DMA))
    def launch(send_sem, recv_sem):
        x, y, c = _coords()
        _handshake([(x, y, 1 - c)])
        cp = pltpu.make_async_remote_copy(
            src_ref=v_ref.at[pl.ds(c * h, h), :], dst_ref=v_ref.at[pl.ds(c * h, h), :],
            send_sem=send_sem, recv_sem=recv_sem, device_id=(x, y, 1 - c), device_id_type=MESH)
        cp.start()
        pltpu.make_async_remote_copy(
            src_ref=v_ref.at[pl.ds(c * h, h), :], dst_ref=v_ref.at[pl.ds((1 - c) * h, h), :],
            send_sem=send_sem, recv_sem=recv_sem, device_id=(x, y, 1 - c), device_id_type=MESH).wait_recv()
        cp.wait_send()

    launch()
    return v_ref[...]


def _scatter_to_chips_async(p, name, collective_id):
    p_ref = jax.new_ref(p, memory_space=pltpu.MemorySpace.HBM)
    out_ref = jax.empty_ref(jax.ShapeDtypeStruct(p.shape, p.dtype), memory_space=pltpu.MemorySpace.HBM)

    @_sequencer(name, collective_id, (pltpu.SemaphoreType.DMA((3,)), pltpu.SemaphoreType.DMA((3,))))
    def launch(send_sems, recv_sems):
        x, y, c = _coords()
        me = 2 * x + y
        _handshake([(px, py, c) for px, py in _other_chips(x, y)])
        sends = []
        for j, (px, py) in enumerate(_other_chips(x, y)):
            sends.append(pltpu.make_async_remote_copy(
                src_ref=p_ref.at[2 * px + py], dst_ref=out_ref.at[me],
                send_sem=send_sems.at[j], recv_sem=recv_sems.at[j], device_id=(px, py, c), device_id_type=MESH))
        for cp in sends:
            cp.start()
        for j, (px, py) in enumerate(_other_chips(x, y)):
            pltpu.make_async_remote_copy(
                src_ref=p_ref.at[me], dst_ref=out_ref.at[2 * px + py],
                send_sem=send_sems.at[j], recv_sem=recv_sems.at[j], device_id=(px, py, c),
                device_id_type=MESH).wait_recv()
        for cp in sends:
            cp.wait_send()

    launch()
    return out_ref[...]


def _share_halves(v, name):
    h = v.shape[0] // 2

    def body(v_ref, out_ref, send_sem, recv_sem):
        x, y, c = _coords()
        cp = pltpu.make_async_remote_copy(
            src_ref=v_ref.at[pl.ds(c * h, h), :], dst_ref=out_ref.at[pl.ds(c * h, h), :],
            send_sem=send_sem, recv_sem=recv_sem, device_id=(x, y, 1 - c), device_id_type=MESH)
        cp.start()
        pltpu.make_async_remote_copy(
            src_ref=v_ref.at[pl.ds(c * h, h), :], dst_ref=out_ref.at[pl.ds((1 - c) * h, h), :],
            send_sem=send_sem, recv_sem=recv_sem, device_id=(x, y, 1 - c), device_id_type=MESH).wait_recv()
        cp.wait_send()

    return pl.pallas_call(
        body, name=name, out_shape=jax.ShapeDtypeStruct(v.shape, v.dtype),
        in_specs=[HBM], out_specs=HBM, input_output_aliases={0: 0},
        scratch_shapes=[pltpu.SemaphoreType.DMA, pltpu.SemaphoreType.DMA],
    )(v)


def _allreduce_small(v, name):
    r, cols = v.shape

    def body(v_ref, out_ref, buf_ref, send_sems, recv_sems):
        x, y, c = _coords()
        me = 4 * x + 2 * y + c
        buf_ref[me] = v_ref[...]
        sends = []
        for k in range(1, N_DEV):
            px = 1 - x if k & 4 else x
            py = 1 - y if k & 2 else y
            pc = 1 - c if k & 1 else c
            sends.append(pltpu.make_async_remote_copy(
                src_ref=v_ref, dst_ref=buf_ref.at[me], send_sem=send_sems.at[k - 1], recv_sem=recv_sems.at[k - 1],
                device_id=(px, py, pc), device_id_type=MESH))
        for cp in sends:
            cp.start()
        for cp in sends:
            cp.wait()
        acc = buf_ref[0]
        for d in range(1, N_DEV):
            acc = acc + buf_ref[d]
        out_ref[...] = acc

    return pl.pallas_call(
        body, name=name, out_shape=jax.ShapeDtypeStruct((r, cols), F32),
        in_specs=[pl.BlockSpec(memory_space=pltpu.VMEM)], out_specs=pl.BlockSpec(memory_space=pltpu.VMEM),
        scratch_shapes=[pltpu.VMEM((N_DEV, r, cols), F32), pltpu.SemaphoreType.DMA((N_DEV - 1,)),
                        pltpu.SemaphoreType.DMA((N_DEV - 1,))],
    )(v)


def _add_sibling(g, from_sibling, core, name):
    n, h, cols = from_sibling.shape
    tr = _row_tile(h)
    steps = h // tr

    def body(core_ref, a_ref, b_ref, o_ref):
        o_ref[...] = (a_ref[...] + b_ref[...]).astype(o_ref.dtype)

    return pl.pallas_call(
        body, name=name, out_shape=jax.ShapeDtypeStruct(from_sibling.shape, jnp.bfloat16),
        grid_spec=pltpu.PrefetchScalarGridSpec(
            num_scalar_prefetch=1, grid=(n, steps),
            in_specs=[pl.BlockSpec((1, tr, cols), lambda s, i, core_ref: (s, core_ref[0] * steps + i, 0)),
                      pl.BlockSpec((1, tr, cols), lambda s, i, core_ref: (s, i, 0))],
            out_specs=pl.BlockSpec((1, tr, cols), lambda s, i, core_ref: (s, i, 0))),
        compiler_params=_params("parallel", "parallel"),
    )(core.reshape(1).astype(jnp.int32), g, from_sibling)


def _sum_slots(p, own, chip, core, name):
    n, r, cols = p.shape
    tr = _row_tile(r)
    steps = r // tr

    def body(core_ref, chip_ref, p_ref, own_ref, o_ref):
        parts = [jnp.where(chip_ref[0] == s, own_ref[0], p_ref[s]).astype(F32) for s in range(n)]
        o_ref[...] = ((parts[0] + parts[1]) + parts[2]) + parts[3]

    return pl.pallas_call(
        body, name=name, out_shape=jax.ShapeDtypeStruct((2 * r, cols), F32),
        grid_spec=pltpu.PrefetchScalarGridSpec(
            num_scalar_prefetch=2, grid=(steps,),
            in_specs=[pl.BlockSpec((n, tr, cols), lambda i, core_ref, chip_ref: (0, i, 0)),
                      pl.BlockSpec((1, tr, cols), lambda i, core_ref, chip_ref: (chip_ref[0], i, 0))],
            out_specs=pl.BlockSpec((tr, cols), lambda i, core_ref, chip_ref: (core_ref[0] * steps + i, 0))),
        compiler_params=_params("parallel"),
    )(core.reshape(1).astype(jnp.int32), chip.reshape(1).astype(jnp.int32), p, own)


PACK_COLS = 1024


def _pack_shards(parts):
    return jnp.concatenate([p.reshape(-1, PACK_COLS) for p in parts], axis=0)


def _unpack_shards(buf, shapes):
    out, row = [], 0
    for shp in shapes:
        nrows = math.prod(shp) // PACK_COLS
        out.append(buf[..., row:row + nrows, :].reshape(buf.shape[:-2] + tuple(shp)))
        row += nrows
    return out


def _local_step(x, target, w):
    t = lambda a: a.T
    g = {}
    w_in_g, w_in_x = w["a_w_in"][:, :D_RNN], w["a_w_in"][:, D_RNN:]
    h0, gate_br, x_br = _norm_and_project(x, w["norm_mix_g"][0], w_in_g, w_in_x, "rglru_in")
    y_a, hs = _rglru_fwd(gate_br, x_br, w["a_conv_w"], w["a_conv_b"], w["a_w_r"], w["a_w_i"], w["a_b_r"],
                         w["a_b_i"], w["a_lambda"], "rglru_fwd")
    x1, h1 = _matmul([(y_a, w["a_w_out"])], F32, "mm_a_out", addend=x, norm_gain=w["norm_ffn_g"][0])
    fg0, fu0, act0 = _ffn_up(h1, w["ffn_w_gate"][0], w["ffn_w_up"][0], "ffn0_up")
    x2, h2 = _matmul([(act0, w["ffn_w_down"][0])], F32, "mm_f0_down", addend=x1, norm_gain=w["norm_mix_g"][1],
                     tk=D_FF)
    qkv = _matmul([(h2, w["b_w_qkv"])], CD, "mm_b_qkv", out_lbm=True, tn=3 * D_MODEL)
    o = _attn_fwd(qkv, "attn_fwd")
    x3, h3 = _matmul([(o, w["b_w_out"])], F32, "mm_b_out", a_lbm=True, addend=x2, norm_gain=w["norm_ffn_g"][1])
    fg1, fu1, act1 = _ffn_up(h3, w["ffn_w_gate"][1], w["ffn_w_up"][1], "ffn1_up")
    dx4, dx4c, g["final_g"], loss = _matmul([(act1, w["ffn_w_down"][1])], F32, "mm_f1_down", addend=x3,
                                            loss_head=(w["final_g"], target), tk=D_FF)

    def ffn_bwd(dx_out, dxc, h, x_in, fg, fu, act, layer, tag):
        dg, du = _ffn_dact(dxc, t(w["ffn_w_down"][layer]), fg, fu, "ffn_" + tag + "_dact")
        dwd = _matmul([(act, dxc)], F32, "mm_" + tag + "_dwd", trans_a=True)
        dwg = _matmul([(h, dg)], F32, "mm_" + tag + "_dwg", trans_a=True)
        dwu = _matmul([(h, du)], F32, "mm_" + tag + "_dwu", trans_a=True)
        dx_in, dx_in_c, dgain = _matmul([(dg, t(w["ffn_w_gate"][layer])), (du, t(w["ffn_w_up"][layer]))], F32,
                                        "mm_" + tag + "_dh", norm_bwd=(x_in, w["norm_ffn_g"][layer], dx_out),
                                        tk=D_FF, tm=256)
        return dx_in, dx_in_c, dgain, dwg, dwu, dwd

    dx3, dx3c, dgf1, dwg1, dwu1, dwd1 = ffn_bwd(dx4, dx4c, h3, x3, fg1, fu1, act1, 1, "f1")
    do = _matmul([(dx3c, t(w["b_w_out"]))], CD, "mm_b_do", out_lbm=True, tn=1024)
    g["b_w_out"] = _matmul([(o, dx3c)], F32, "mm_b_dwout", trans_a=True, a_lbm=True)
    dq, dk, dv = _attn_bwd(qkv, o, do, "attn_bwd")
    wq_t = t(w["b_w_qkv"])
    parts = (dq, dk, dv)
    g["b_w_qkv"] = jnp.concatenate(
        [_matmul([(h2, p)], F32, "mm_b_dwqkv%d" % n, trans_a=True, b_lbm=True) for n, p in enumerate(parts)], axis=1)
    dx2, dx2c, dgm1 = _matmul([(p, wq_t[n * D_MODEL:(n + 1) * D_MODEL]) for n, p in enumerate(parts)], F32, "mm_b_dh",
                              a_lbm=True, norm_bwd=(x2, w["norm_mix_g"][1], dx3))
    dx1, dx1c, dgf0, dwg0, dwu0, dwd0 = ffn_bwd(dx2, dx2c, h1, x1, fg0, fu0, act0, 0, "f0")
    dy_a = _matmul([(dx1c, t(w["a_w_out"]))], F32, "mm_a_dy")
    g["a_w_out"] = _matmul([(y_a, dx1c)], F32, "mm_a_dwout", trans_a=True)
    wrt = jnp.swapaxes(w["a_w_r"], 1, 2)
    wit = jnp.swapaxes(w["a_w_i"], 1, 2)
    (dgate, dxbr, g["a_conv_w"], g["a_conv_b"], g["a_b_r"], g["a_b_i"], g["a_lambda"], g["a_w_r"],
     g["a_w_i"]) = _rglru_bwd(dy_a, gate_br, x_br, hs, w["a_conv_w"], w["a_conv_b"], w["a_w_r"], w["a_w_i"], wrt, wit,
                              w["a_b_r"], w["a_b_i"], w["a_lambda"], "rglru_bwd")
    g["a_w_in"] = jnp.concatenate([_matmul([(h0, dgate)], F32, "mm_a_dwin_g", trans_a=True),
                                   _matmul([(h0, dxbr)], F32, "mm_a_dwin_x", trans_a=True)], axis=1)
    dx0, _, dgm0 = _matmul([(dgate, t(w_in_g)), (dxbr, t(w_in_x))], F32, "mm_a_dh",
                           norm_bwd=(x, w["norm_mix_g"][0], dx1))
    g["norm_mix_g"] = jnp.concatenate([dgm0, dgm1], axis=0)
    g["norm_ffn_g"] = jnp.concatenate([dgf0, dgf1], axis=0)
    g["ffn_w_gate"] = [dwg0, dwg1]
    g["ffn_w_up"] = [dwu0, dwu1]
    g["ffn_w_down"] = [dwd0, dwd1]
    return loss, dx0, g


WEIGHTS = ["norm_mix_g", "norm_ffn_g", "a_w_in", "a_conv_w", "a_conv_b", "a_w_r", "a_b_r", "a_w_i", "a_b_i",
           "a_lambda", "a_w_out", "b_w_qkv", "b_w_out", "ffn_w_gate", "ffn_w_up", "ffn_w_down", "final_g"]
BIG = [("a_w_in", 2), ("a_w_r", 2), ("a_w_i", 2), ("a_w_out", 1), ("b_w_qkv", 2), ("b_w_out", 1),
       ("ffn_w_gate", 2), ("ffn_w_up", 2), ("ffn_w_down", 1)]
LAYER1 = ["b_w_qkv", "b_w_out", "ffn_w_gate", "ffn_w_up", "ffn_w_down"]
LAYER0 = ["a_w_in", "a_w_r", "a_w_i", "a_w_out", "ffn_w_gate", "ffn_w_up", "ffn_w_down"]
RS_COLLECTIVE_IDS = {"chips1": 3, "chips0": 4, "sibling1": 5, "share1": 6}
GATHER_COLLECTIVE_IDS = (8, 7)
SMALL = ["norm_mix_g", "norm_ffn_g", "a_conv_w", "a_conv_b", "a_b_r", "a_b_i", "a_lambda", "final_g"]


def _split_chips(full, axis):
    if axis == 1:
        return full.reshape((N_CHIPS, 1, full.shape[1] // N_CHIPS) + full.shape[2:])
    return jnp.stack(jnp.split(full, N_CHIPS, axis=axis))


def _step(x, target, weights, moments_m, moments_v):
    chip = 2 * lax.axis_index("x") + lax.axis_index("y")
    core = lax.axis_index("c")
    axis_of = dict(BIG)
    full = {}
    for group, layer, tag, collective_id in ((LAYER0[:4], 0, "0a", None), (LAYER0[4:], 0, "0f", GATHER_COLLECTIVE_IDS[0]),
                                             (LAYER1, 1, "1", GATHER_COLLECTIVE_IDS[1])):
        shards = [weights[n][layer % weights[n].shape[0]].astype(CD) for n in group]
        packed = _pack_shards(shards)
        if collective_id is not None:
            packed, first_gathered = lax.optimization_barrier((packed, first_gathered))
        gathered = _allgather_chips(packed, "allgather_weights" + tag, collective_id)
        if collective_id is None:
            first_gathered = gathered
        for n, own, stack in zip(group, shards, _unpack_shards(gathered, [sh.shape for sh in shards])):
            joined = jnp.concatenate([jnp.where(chip == s, own, stack[s]) for s in range(N_CHIPS)],
                                     axis=axis_of[n] - 1)
            full.setdefault(n, {})[layer] = joined
    full = {n: (v[0] if n.startswith("a_") else v[1] if n.startswith("b_") else [v[0], v[1]]) for n, v in full.items()}
    cw_rows = jnp.zeros((N_CHIPS, CONV_W, RG_BW), F32)
    cw_rows = lax.dynamic_update_slice(cw_rows, jnp.where(core == 0, weights["a_conv_w"], 0.0), (chip, 0, 0))
    cw_all = _allreduce_small(cw_rows.reshape(-1, LANES), "allgather_conv_w").reshape(N_CHIPS, CONV_W, RG_BW)
    full["a_conv_w"] = jnp.concatenate([cw_all[s] for s in range(N_CHIPS)], axis=1)
    for n in ("norm_mix_g", "norm_ffn_g", "final_g"):
        full[n] = weights[n]
    for n in ("a_conv_b", "a_b_r", "a_b_i", "a_lambda"):
        full[n] = weights[n]
    loss, dx, grads = _local_step(x[0], target[0], full)
    small_parts = [grads[n].reshape(-1) for n in SMALL] + [loss.reshape(-1)]
    sizes = [p.shape[0] for p in small_parts]
    small = _allreduce_small(jnp.concatenate(small_parts).reshape(-1, LANES), "allreduce_small").reshape(-1)
    red, pos = {}, 0
    for n, sz in zip(SMALL + ["loss"], sizes):
        red[n] = small[pos:pos + sz]
        pos += sz
    loss_out = red["loss"][0]
    g_out = {}
    for n in SMALL:
        if n == "a_conv_w":
            g_out[n] = lax.dynamic_slice(red[n].reshape(CONV_W, D_RNN), (0, chip * RG_BW), (CONV_W, RG_BW)).reshape(
                weights[n].shape)
        else:
            g_out[n] = red[n].reshape(weights[n].shape)
    axis_of = dict(BIG)
    pieces = {}
    for group, layer, tag in ((LAYER1, 1, "1"), (LAYER0, 0, "0")):
        stacks, shapes = [], []
        for n in group:
            per_layer = isinstance(grads[n], list)
            gfull = grads[n][layer] if per_layer else grads[n]
            shard_shape = weights[n].shape[1:]
            gfull = gfull.reshape((1,) + gfull.shape)
            stacks.append(_split_chips(gfull, axis_of[n]).reshape(N_CHIPS, -1, PACK_COLS))
            shapes.append((1,) + tuple(shard_shape))
        gbuf = jnp.concatenate(stacks, axis=1)
        if layer == 1:
            from_sibling = _exchange_sibling_halves_async(gbuf, "rs_sibling" + tag, RS_COLLECTIVE_IDS["sibling1"])
        else:
            from_sibling = _exchange_sibling_halves(gbuf, "rs_sibling" + tag)
        chip_partial = _add_sibling(gbuf, from_sibling, core, "rs_add" + tag)
        from_chips = _scatter_to_chips_async(chip_partial, "rs_chips" + tag, RS_COLLECTIVE_IDS["chips" + tag])
        halves = _sum_slots(from_chips, chip_partial, chip, core, "rs_sum" + tag)
        if layer == 1:
            reduced = _share_halves_async(halves, "rs_share" + tag, RS_COLLECTIVE_IDS["share1"])
        else:
            reduced = _share_halves(halves, "rs_share" + tag)
        for n, piece in zip(group, _unpack_shards(reduced, shapes)):
            pieces.setdefault(n, {})[layer] = piece
    for n, _ in BIG:
        layers = pieces[n]
        g_out[n] = jnp.concatenate([layers[k] for k in sorted(layers)], axis=0)
    updates = {}
    for n, _ in BIG:
        updates[n] = _adamw(weights[n], g_out[n], moments_m[n], moments_v[n], "adamw_" + n)
    rows = lambda d: jnp.concatenate([d[n].reshape(-1, D_MODEL) for n in SMALL], axis=0)
    small_updates = _adamw(rows(weights), rows(g_out), rows(moments_m), rows(moments_v), "adamw_small")
    pos = 0
    for n in SMALL:
        nrows = weights[n].size // D_MODEL
        updates[n] = tuple(u[pos:pos + nrows].reshape(weights[n].shape) for u in small_updates)
        pos += nrows
    outs_g = [g_out[n] for n in WEIGHTS]
    outs_d, outs_m, outs_v = ([updates[n][k] for n in WEIGHTS] for k in range(3))
    return (loss_out, dx[None], *outs_g, *outs_d, *outs_m, *outs_v)


def kernel(x, norm_mix_g, norm_ffn_g, a_w_in, a_conv_w, a_conv_b, a_w_r, a_b_r, a_w_i, a_b_i, a_lambda, a_w_out, b_w_qkv, b_w_out, ffn_w_gate, ffn_w_up, ffn_w_down, final_g, loss_target, m_norm_mix_g, m_norm_ffn_g, m_a_w_in, m_a_conv_w, m_a_conv_b, m_a_w_r, m_a_b_r, m_a_w_i, m_a_b_i, m_a_lambda, m_a_w_out, m_b_w_qkv, m_b_w_out, m_ffn_w_gate, m_ffn_w_up, m_ffn_w_down, m_final_g, v_norm_mix_g, v_norm_ffn_g, v_a_w_in, v_a_conv_w, v_a_conv_b, v_a_w_r, v_a_b_r, v_a_w_i, v_a_b_i, v_a_lambda, v_a_w_out, v_b_w_qkv, v_b_w_out, v_ffn_w_gate, v_ffn_w_up, v_ffn_w_down, v_final_g):
    ws = [norm_mix_g, norm_ffn_g, a_w_in, a_conv_w, a_conv_b, a_w_r, a_b_r, a_w_i, a_b_i, a_lambda, a_w_out, b_w_qkv,
          b_w_out, ffn_w_gate, ffn_w_up, ffn_w_down, final_g]
    ms = [m_norm_mix_g, m_norm_ffn_g, m_a_w_in, m_a_conv_w, m_a_conv_b, m_a_w_r, m_a_b_r, m_a_w_i, m_a_b_i, m_a_lambda,
          m_a_w_out, m_b_w_qkv, m_b_w_out, m_ffn_w_gate, m_ffn_w_up, m_ffn_w_down, m_final_g]
    vs = [v_norm_mix_g, v_norm_ffn_g, v_a_w_in, v_a_conv_w, v_a_conv_b, v_a_w_r, v_a_b_r, v_a_w_i, v_a_b_i, v_a_lambda,
          v_a_w_out, v_b_w_qkv, v_b_w_out, v_ffn_w_gate, v_ffn_w_up, v_ffn_w_down, v_final_g]
    return _step(x, loss_target, dict(zip(WEIGHTS, ws)), dict(zip(WEIGHTS, ms)), dict(zip(WEIGHTS, vs)))
```
